```python
import math
import jax, jax.numpy as jnp
from jax import lax
import numpy as np

D_MODEL = 1024
BATCH = 8
SEQ = 8192
DEPTH = 2

CHUNK = 64
Q_BLOCK = 128
SB_HEADS = 8
SB_HEAD_DIM = 64
SB_WIDTH = SB_HEADS * SB_HEAD_DIM
SSM_WIDTH = D_MODEL // 2
SSM_GROUP = 16
SSM_GROUPS = SSM_WIDTH // SSM_GROUP
SSM_STATE = 64
DT_MIN = 1e-3
DT_MAX = 1e-1
FFN_HIDDEN = ((8 * D_MODEL // 3 + 255) // 256) * 256
IN_SPLITS = (SB_WIDTH, 2 * SB_WIDTH, 3 * SB_WIDTH, 3 * SB_WIDTH + SSM_WIDTH,
             3 * SB_WIDTH + SSM_WIDTH + D_MODEL)
IN_COLS = 3 * SB_WIDTH + SSM_WIDTH + 2 * D_MODEL
N_MOD = 6
DEEPNORM_ALPHA = (2 * DEPTH) ** 0.25
DEEPNORM_BETA = (8 * DEPTH) ** -0.25
LN_EPS = 1e-5

kernel_name = "hybrid_sb_s5_deepnorm_adaln"


def _normalize(x):
    xf = x.astype(jnp.float32)
    mu = jnp.mean(xf, axis=-1, keepdims=True)
    var = jnp.mean(jnp.square(xf - mu), axis=-1, keepdims=True)
    return ((xf - mu) * lax.rsqrt(var + LN_EPS)).astype(x.dtype)


def _layer_norm(x, g, b):
    return _normalize(x) * g + b


def stick_breaking_attention(q, k, v):
    b, s, h, dh = q.shape
    nb = s // Q_BLOCK
    f32 = jnp.float32
    qb = q.astype(f32).reshape(b, nb, Q_BLOCK, h, dh).transpose(1, 0, 3, 2, 4)
    kt = k.astype(f32).transpose(0, 2, 1, 3)
    vt = v.astype(f32).transpose(0, 2, 1, 3)
    key_pos = jnp.arange(s, dtype=jnp.int32)
    scale = 1.0 / math.sqrt(dh)

    def one_block(args):
        q_blk, blk = args
        q_pos = blk * Q_BLOCK + jnp.arange(Q_BLOCK, dtype=jnp.int32)
        z = jnp.einsum('bhqd,bhkd->bhqk', q_blk, kt) * scale
        causal = key_pos[None, :] < q_pos[:, None]
        log_beta = jax.nn.log_sigmoid(z)
        log_one_minus = jnp.where(causal, log_beta - z, 0.0)
        after = lax.cumsum(log_one_minus, axis=3, reverse=True) - log_one_minus
        w = jnp.where(causal, jnp.exp(log_beta + after), 0.0)
        return jnp.einsum('bhqk,bhkd->bhqd', w, vt)

    out = lax.map(one_block, (qb, jnp.arange(nb, dtype=jnp.int32)))
    return out.transpose(1, 0, 3, 2, 4).reshape(b, s, h * dh).astype(v.dtype)


def s5_branch(u, a_re, a_im, log_dt, b_re, b_im, c_re, c_im, d_skip, w_glu, b_glu):
    f32 = jnp.float32
    c64 = jnp.complex64
    bsz, s, _ = u.shape
    n_chunks = s // CHUNK
    uf = u.astype(f32)
    lam = lax.complex(a_re.astype(f32), a_im.astype(f32))
    dt = jnp.exp(log_dt.astype(f32))[:, None]
    lam_dt = lam * dt
    lam_bar = jnp.exp(lam_dt)
    b_mat = lax.complex(b_re.astype(f32), b_im.astype(f32))
    b_bar = ((lam_bar - 1.0) / lam)[..., None] * b_mat
    c_mat = lax.complex(c_re.astype(f32), c_im.astype(f32))
    steps = jnp.arange(1, CHUNK + 1, dtype=f32)
    powers = jnp.exp(lam_dt[None] * steps[:, None, None].astype(c64))
    a_seq = jnp.broadcast_to(lam_bar[None, None], (CHUNK, bsz, SSM_GROUPS, SSM_STATE))

    u_chunks = uf.reshape(bsz, n_chunks, CHUNK, SSM_GROUPS, SSM_GROUP).transpose(1, 2, 0, 3, 4)

    def combine(left, right):
        a_l, b_l = left
        a_r, b_r = right
        return a_r * a_l, a_r * b_l + b_r

    def step(state, u_c):
        bu = jnp.einsum('gpc,lbgc->lbgp', b_bar, u_c.astype(c64))
        _, h_loc = lax.associative_scan(combine, (a_seq, bu), axis=0)
        h = h_loc + powers[:, None] * state[None]
        y = jnp.einsum('gcp,lbgp->lbgc', c_mat, h).real
        return h[-1], y

    state0 = jnp.zeros((bsz, SSM_GROUPS, SSM_STATE), c64)
    _, ys = lax.scan(step, state0, u_chunks)
    y = ys.transpose(2, 0, 1, 3, 4).reshape(bsz, s, SSM_WIDTH)
    y = y + d_skip.astype(f32) * uf
    y = jax.nn.gelu(y)
    y = y * jax.nn.sigmoid(y @ w_glu.astype(f32) + b_glu.astype(f32))
    return y.astype(u.dtype)


def token_mixer(h, w_in, w_sb_up, a_re, a_im, log_dt, b_re, b_im, c_re, c_im,
                d_skip, w_glu, b_glu, w_ssm_up, w_out):
    bsz, s, _ = h.shape
    proj = h @ w_in
    q, k, v, u, g_sb, g_ssm = jnp.split(proj, IN_SPLITS, axis=-1)
    shp = (bsz, s, SB_HEADS, SB_HEAD_DIM)
    y_sb = stick_breaking_attention(q.reshape(shp), k.reshape(shp), v.reshape(shp)) @ w_sb_up
    y_ssm = s5_branch(u, a_re, a_im, log_dt, b_re, b_im, c_re, c_im,
                      d_skip, w_glu, b_glu) @ w_ssm_up
    merged = jax.nn.sigmoid(g_sb) * y_sb + jax.nn.sigmoid(g_ssm) * y_ssm
    return merged @ w_out


def swiglu_ffn(h, w_ffn_in, w_ffn_out):
    gate, up = jnp.split(h @ w_ffn_in, 2, axis=-1)
    return (jax.nn.silu(gate) * up) @ w_ffn_out


def _fwd_setup_inputs(seed: int = 0) -> dict:
    key = jax.random.key(seed)
    ks = jax.random.split(key, 32)
    f32 = jnp.float32

    def nrm(k, shape, scale):
        return jax.random.normal(k, shape, f32) * scale

    G, P, Cg = SSM_GROUPS, SSM_STATE, SSM_GROUP
    n = jnp.arange(P, dtype=f32)
    return {
        "x": nrm(ks[0], (BATCH, SEQ, D_MODEL), 1.0),
        "c": nrm(ks[1], (BATCH, D_MODEL), 1.0),
        "w_ada": nrm(ks[2], (DEPTH, D_MODEL, N_MOD * D_MODEL), 0.5 * D_MODEL ** -0.5),
        "b_ada": nrm(ks[3], (DEPTH, N_MOD * D_MODEL), 0.02),
        "w_in": nrm(ks[4], (DEPTH, D_MODEL, IN_COLS), D_MODEL ** -0.5),
        "w_sb_up": nrm(ks[5], (DEPTH, SB_WIDTH, D_MODEL), SB_WIDTH ** -0.5),
        "ssm_a_re": -0.5 + nrm(ks[6], (DEPTH, G, P), 0.01),
        "ssm_a_im": math.pi * n + nrm(ks[7], (DEPTH, G, P), 0.01),
        "ssm_log_dt": jax.random.uniform(ks[8], (DEPTH, G), f32,
                                         math.log(DT_MIN), math.log(DT_MAX)),
        "ssm_b_re": nrm(ks[9], (DEPTH, G, P, Cg), (2 * Cg) ** -0.5),
        "ssm_b_im": nrm(ks[10], (DEPTH, G, P, Cg), (2 * Cg) ** -0.5),
        "ssm_c_re": nrm(ks[11], (DEPTH, G, Cg, P), P ** -0.5),
        "ssm_c_im": nrm(ks[12], (DEPTH, G, Cg, P), P ** -0.5),
        "ssm_d": 1.0 + nrm(ks[13], (DEPTH, SSM_WIDTH), 0.1),
        "w_glu": nrm(ks[14], (DEPTH, SSM_WIDTH, SSM_WIDTH), SSM_WIDTH ** -0.5),
        "b_glu": nrm(ks[15], (DEPTH, SSM_WIDTH), 0.02),
        "w_ssm_up": nrm(ks[16], (DEPTH, SSM_WIDTH, D_MODEL), SSM_WIDTH ** -0.5),
        "w_out": nrm(ks[17], (DEPTH, D_MODEL, D_MODEL), D_MODEL ** -0.5 * DEEPNORM_BETA),
        "ln1_g": 1.0 + nrm(ks[18], (DEPTH, D_MODEL), 0.02),
        "ln1_b": nrm(ks[19], (DEPTH, D_MODEL), 0.02),
        "w_ffn_in": nrm(ks[20], (DEPTH, D_MODEL, 2 * FFN_HIDDEN), D_MODEL ** -0.5),
        "w_ffn_out": nrm(ks[21], (DEPTH, FFN_HIDDEN, D_MODEL), FFN_HIDDEN ** -0.5 * DEEPNORM_BETA),
        "ln2_g": 1.0 + nrm(ks[22], (DEPTH, D_MODEL), 0.02),
        "ln2_b": nrm(ks[23], (DEPTH, D_MODEL), 0.02),
    }


def _fwd_reference(x, c, w_ada, b_ada, w_in, w_sb_up, ssm_a_re, ssm_a_im, ssm_log_dt,
              ssm_b_re, ssm_b_im, ssm_c_re, ssm_c_im, ssm_d, w_glu, b_glu,
              w_ssm_up, w_out, ln1_g, ln1_b, w_ffn_in, w_ffn_out, ln2_g, ln2_b):
    c_act = jax.nn.silu(c)
    for l in range(DEPTH):
        mod = c_act @ w_ada[l] + b_ada[l]
        sh_m, sc_m, g_m, sh_f, sc_f, g_f = [m[:, None, :] for m in jnp.split(mod, N_MOD, axis=-1)]
        h = _normalize(x) * (1.0 + sc_m) + sh_m
        y = token_mixer(h, w_in[l], w_sb_up[l], ssm_a_re[l], ssm_a_im[l], ssm_log_dt[l],
                        ssm_b_re[l], ssm_b_im[l], ssm_c_re[l], ssm_c_im[l], ssm_d[l],
                        w_glu[l], b_glu[l], w_ssm_up[l], w_out[l])
        x = _layer_norm(DEEPNORM_ALPHA * x + (1.0 + g_m) * y, ln1_g[l], ln1_b[l])
        h = _normalize(x) * (1.0 + sc_f) + sh_f
        y = swiglu_ffn(h, w_ffn_in[l], w_ffn_out[l])
        x = _layer_norm(DEEPNORM_ALPHA * x + (1.0 + g_f) * y, ln2_g[l], ln2_b[l])
    return x


import jax as _jax
import jax.numpy as _jnp

TWIN_FORMAT = 'train_step'
FWD_PARAMS = ['x', 'c', 'w_ada', 'b_ada', 'w_in', 'w_sb_up', 'ssm_a_re', 'ssm_a_im', 'ssm_log_dt', 'ssm_b_re', 'ssm_b_im', 'ssm_c_re', 'ssm_c_im', 'ssm_d', 'w_glu', 'b_glu', 'w_ssm_up', 'w_out', 'ln1_g', 'ln1_b', 'w_ffn_in', 'w_ffn_out', 'ln2_g', 'ln2_b']
TWIN_WEIGHTS = ['w_ada', 'b_ada', 'w_in', 'w_sb_up', 'ssm_a_re', 'ssm_a_im', 'ssm_log_dt', 'ssm_b_re', 'ssm_b_im', 'ssm_c_re', 'ssm_c_im', 'ssm_d', 'w_glu', 'b_glu', 'w_ssm_up', 'w_out', 'ln1_g', 'ln1_b', 'w_ffn_in', 'w_ffn_out', 'ln2_g', 'ln2_b']
TWIN_DIFF_INPUT = 'x'
TWIN_INPUTS = ['x', 'c', 'w_ada', 'b_ada', 'w_in', 'w_sb_up', 'ssm_a_re', 'ssm_a_im', 'ssm_log_dt', 'ssm_b_re', 'ssm_b_im', 'ssm_c_re', 'ssm_c_im', 'ssm_d', 'w_glu', 'b_glu', 'w_ssm_up', 'w_out', 'ln1_g', 'ln1_b', 'w_ffn_in', 'w_ffn_out', 'ln2_g', 'ln2_b', 'loss_target', 'm_w_ada', 'm_b_ada', 'm_w_in', 'm_w_sb_up', 'm_ssm_a_re', 'm_ssm_a_im', 'm_ssm_log_dt', 'm_ssm_b_re', 'm_ssm_b_im', 'm_ssm_c_re', 'm_ssm_c_im', 'm_ssm_d', 'm_w_glu', 'm_b_glu', 'm_w_ssm_up', 'm_w_out', 'm_ln1_g', 'm_ln1_b', 'm_w_ffn_in', 'm_w_ffn_out', 'm_ln2_g', 'm_ln2_b', 'v_w_ada', 'v_b_ada', 'v_w_in', 'v_w_sb_up', 'v_ssm_a_re', 'v_ssm_a_im', 'v_ssm_log_dt', 'v_ssm_b_re', 'v_ssm_b_im', 'v_ssm_c_re', 'v_ssm_c_im', 'v_ssm_d', 'v_w_glu', 'v_b_glu', 'v_w_ssm_up', 'v_w_out', 'v_ln1_g', 'v_ln1_b', 'v_w_ffn_in', 'v_w_ffn_out', 'v_ln2_g', 'v_ln2_b']
TWIN_OUTPUTS = ['loss', 'grad_x', 'grad_w_ada', 'grad_b_ada', 'grad_w_in', 'grad_w_sb_up', 'grad_ssm_a_re', 'grad_ssm_a_im', 'grad_ssm_log_dt', 'grad_ssm_b_re', 'grad_ssm_b_im', 'grad_ssm_c_re', 'grad_ssm_c_im', 'grad_ssm_d', 'grad_w_glu', 'grad_b_glu', 'grad_w_ssm_up', 'grad_w_out', 'grad_ln1_g', 'grad_ln1_b', 'grad_w_ffn_in', 'grad_w_ffn_out', 'grad_ln2_g', 'grad_ln2_b', 'delta_w_ada', 'delta_b_ada', 'delta_w_in', 'delta_w_sb_up', 'delta_ssm_a_re', 'delta_ssm_a_im', 'delta_ssm_log_dt', 'delta_ssm_b_re', 'delta_ssm_b_im', 'delta_ssm_c_re', 'delta_ssm_c_im', 'delta_ssm_d', 'delta_w_glu', 'delta_b_glu', 'delta_w_ssm_up', 'delta_w_out', 'delta_ln1_g', 'delta_ln1_b', 'delta_w_ffn_in', 'delta_w_ffn_out', 'delta_ln2_g', 'delta_ln2_b', 'new_m_w_ada', 'new_m_b_ada', 'new_m_w_in', 'new_m_w_sb_up', 'new_m_ssm_a_re', 'new_m_ssm_a_im', 'new_m_ssm_log_dt', 'new_m_ssm_b_re', 'new_m_ssm_b_im', 'new_m_ssm_c_re', 'new_m_ssm_c_im', 'new_m_ssm_d', 'new_m_w_glu', 'new_m_b_glu', 'new_m_w_ssm_up', 'new_m_w_out', 'new_m_ln1_g', 'new_m_ln1_b', 'new_m_w_ffn_in', 'new_m_w_ffn_out', 'new_m_ln2_g', 'new_m_ln2_b', 'new_v_w_ada', 'new_v_b_ada', 'new_v_w_in', 'new_v_w_sb_up', 'new_v_ssm_a_re', 'new_v_ssm_a_im', 'new_v_ssm_log_dt', 'new_v_ssm_b_re', 'new_v_ssm_b_im', 'new_v_ssm_c_re', 'new_v_ssm_c_im', 'new_v_ssm_d', 'new_v_w_glu', 'new_v_b_glu', 'new_v_w_ssm_up', 'new_v_w_out', 'new_v_ln1_g', 'new_v_ln1_b', 'new_v_w_ffn_in', 'new_v_w_ffn_out', 'new_v_ln2_g', 'new_v_ln2_b']
TWIN_LEAF_KINDS = {'loss': 'loss', 'grad_x': 'grad_x', 'grad_w_ada': 'grad_w', 'grad_b_ada': 'grad_w', 'grad_w_in': 'grad_w', 'grad_w_sb_up': 'grad_w', 'grad_ssm_a_re': 'grad_w', 'grad_ssm_a_im': 'grad_w', 'grad_ssm_log_dt': 'grad_w', 'grad_ssm_b_re': 'grad_w', 'grad_ssm_b_im': 'grad_w', 'grad_ssm_c_re': 'grad_w', 'grad_ssm_c_im': 'grad_w', 'grad_ssm_d': 'grad_w', 'grad_w_glu': 'grad_w', 'grad_b_glu': 'grad_w', 'grad_w_ssm_up': 'grad_w', 'grad_w_out': 'grad_w', 'grad_ln1_g': 'grad_w', 'grad_ln1_b': 'grad_w', 'grad_w_ffn_in': 'grad_w', 'grad_w_ffn_out': 'grad_w', 'grad_ln2_g': 'grad_w', 'grad_ln2_b': 'grad_w', 'delta_w_ada': 'delta_w', 'delta_b_ada': 'delta_w', 'delta_w_in': 'delta_w', 'delta_w_sb_up': 'delta_w', 'delta_ssm_a_re': 'delta_w', 'delta_ssm_a_im': 'delta_w', 'delta_ssm_log_dt': 'delta_w', 'delta_ssm_b_re': 'delta_w', 'delta_ssm_b_im': 'delta_w', 'delta_ssm_c_re': 'delta_w', 'delta_ssm_c_im': 'delta_w', 'delta_ssm_d': 'delta_w', 'delta_w_glu': 'delta_w', 'delta_b_glu': 'delta_w', 'delta_w_ssm_up': 'delta_w', 'delta_w_out': 'delta_w', 'delta_ln1_g': 'delta_w', 'delta_ln1_b': 'delta_w', 'delta_w_ffn_in': 'delta_w', 'delta_w_ffn_out': 'delta_w', 'delta_ln2_g': 'delta_w', 'delta_ln2_b': 'delta_w', 'new_m_w_ada': 'new_m', 'new_m_b_ada': 'new_m', 'new_m_w_in': 'new_m', 'new_m_w_sb_up': 'new_m', 'new_m_ssm_a_re': 'new_m', 'new_m_ssm_a_im': 'new_m', 'new_m_ssm_log_dt': 'new_m', 'new_m_ssm_b_re': 'new_m', 'new_m_ssm_b_im': 'new_m', 'new_m_ssm_c_re': 'new_m', 'new_m_ssm_c_im': 'new_m', 'new_m_ssm_d': 'new_m', 'new_m_w_glu': 'new_m', 'new_m_b_glu': 'new_m', 'new_m_w_ssm_up': 'new_m', 'new_m_w_out': 'new_m', 'new_m_ln1_g': 'new_m', 'new_m_ln1_b': 'new_m', 'new_m_w_ffn_in': 'new_m', 'new_m_w_ffn_out': 'new_m', 'new_m_ln2_g': 'new_m', 'new_m_ln2_b': 'new_m', 'new_v_w_ada': 'new_v', 'new_v_b_ada': 'new_v', 'new_v_w_in': 'new_v', 'new_v_w_sb_up': 'new_v', 'new_v_ssm_a_re': 'new_v', 'new_v_ssm_a_im': 'new_v', 'new_v_ssm_log_dt': 'new_v', 'new_v_ssm_b_re': 'new_v', 'new_v_ssm_b_im': 'new_v', 'new_v_ssm_c_re': 'new_v', 'new_v_ssm_c_im': 'new_v', 'new_v_ssm_d': 'new_v', 'new_v_w_glu': 'new_v', 'new_v_b_glu': 'new_v', 'new_v_w_ssm_up': 'new_v', 'new_v_w_out': 'new_v', 'new_v_ln1_g': 'new_v', 'new_v_ln1_b': 'new_v', 'new_v_w_ffn_in': 'new_v', 'new_v_w_ffn_out': 'new_v', 'new_v_ln2_g': 'new_v', 'new_v_ln2_b': 'new_v'}


def _forward(args):
    return _fwd_reference(*[args[k] for k in FWD_PARAMS])


def _output_shape():
    def fwd():
        inp = _fwd_setup_inputs(0)
        return _fwd_reference(*[inp[k] for k in FWD_PARAMS])
    out = _jax.eval_shape(fwd)
    return out.shape, out.dtype

N_MICROBATCH = 1
ADAM_LR = 0.001
ADAM_B1 = 0.9
ADAM_B2 = 0.999
ADAM_EPS = 1e-08
ADAM_WD = 0.01
ADAM_STEP = 10
PER_EXAMPLE_BATCH_AXIS = {'x': 0, 'c': 0, 'loss_target': 0}
SHARED_INPUTS = []
_WEIGHT_DTYPES = {'w_ada': _jnp.float32, 'b_ada': _jnp.float32, 'w_in': _jnp.float32, 'w_sb_up': _jnp.float32, 'ssm_a_re': _jnp.float32, 'ssm_a_im': _jnp.float32, 'ssm_log_dt': _jnp.float32, 'ssm_b_re': _jnp.float32, 'ssm_b_im': _jnp.float32, 'ssm_c_re': _jnp.float32, 'ssm_c_im': _jnp.float32, 'ssm_d': _jnp.float32, 'w_glu': _jnp.float32, 'b_glu': _jnp.float32, 'w_ssm_up': _jnp.float32, 'w_out': _jnp.float32, 'ln1_g': _jnp.float32, 'ln1_b': _jnp.float32, 'w_ffn_in': _jnp.float32, 'w_ffn_out': _jnp.float32, 'ln2_g': _jnp.float32, 'ln2_b': _jnp.float32}
MOMENT_SCALE = {'w_ada': 4.610784e-02, 'b_ada': 1.021407e-01, 'w_in': 2.959856e-02, 'w_sb_up': 4.420858e-02, 'ssm_a_re': 6.568846e-03, 'ssm_a_im': 3.974663e-03, 'ssm_log_dt': 1.549955e+00, 'ssm_b_re': 2.259950e-03, 'ssm_b_im': 2.276769e-03, 'ssm_c_re': 3.260860e-03, 'ssm_c_im': 2.919069e-03, 'ssm_d': 5.830259e-02, 'w_glu': 9.926769e-03, 'b_glu': 1.825318e-02, 'w_ssm_up': 3.004081e-02, 'w_out': 1.050699e-01, 'ln1_g': 1.999567e+00, 'ln1_b': 9.668173e-01, 'w_ffn_in': 4.046977e-02, 'w_ffn_out': 1.326415e-01, 'ln2_g': 4.538546e+01, 'ln2_b': 3.025939e+00}


def _to_microbatches(a, axis):
    t = _jnp.moveaxis(a, axis, 0)
    t = t.reshape((N_MICROBATCH, t.shape[0] // N_MICROBATCH) + t.shape[1:])
    return _jnp.moveaxis(t, 1, axis + 1)


def setup_inputs(seed: int = 0) -> dict:
    inp = _fwd_setup_inputs(seed)
    key = _jax.random.fold_in(_jax.random.key(seed), 7919)
    shape, _ = _output_shape()
    out = dict(inp)
    out["loss_target"] = _jax.random.normal(_jax.random.fold_in(key, 0), shape, _jnp.float32)
    for i, name in enumerate(TWIN_WEIGHTS):
        w = inp[name].astype(_jnp.float32)
        if MOMENT_SCALE is None:
            s = _jnp.sqrt(_jnp.mean(_jnp.square(w)) + 1e-30)
        else:
            s = MOMENT_SCALE[name]
        km, kv = _jax.random.split(_jax.random.fold_in(key, i + 1))
        out[name] = w
        out["m_" + name] = s * _jax.random.normal(km, w.shape, _jnp.float32)
        out["v_" + name] = (s * s) * _jax.random.uniform(kv, w.shape, _jnp.float32, 0.5, 1.5)
    if N_MICROBATCH > 1:
        for name, axis in PER_EXAMPLE_BATCH_AXIS.items():
            out[name] = _to_microbatches(out[name], axis)
    return {'x': out['x'], 'c': out['c'], 'w_ada': out['w_ada'], 'b_ada': out['b_ada'], 'w_in': out['w_in'], 'w_sb_up': out['w_sb_up'], 'ssm_a_re': out['ssm_a_re'], 'ssm_a_im': out['ssm_a_im'], 'ssm_log_dt': out['ssm_log_dt'], 'ssm_b_re': out['ssm_b_re'], 'ssm_b_im': out['ssm_b_im'], 'ssm_c_re': out['ssm_c_re'], 'ssm_c_im': out['ssm_c_im'], 'ssm_d': out['ssm_d'], 'w_glu': out['w_glu'], 'b_glu': out['b_glu'], 'w_ssm_up': out['w_ssm_up'], 'w_out': out['w_out'], 'ln1_g': out['ln1_g'], 'ln1_b': out['ln1_b'], 'w_ffn_in': out['w_ffn_in'], 'w_ffn_out': out['w_ffn_out'], 'ln2_g': out['ln2_g'], 'ln2_b': out['ln2_b'], 'loss_target': out['loss_target'], 'm_w_ada': out['m_w_ada'], 'm_b_ada': out['m_b_ada'], 'm_w_in': out['m_w_in'], 'm_w_sb_up': out['m_w_sb_up'], 'm_ssm_a_re': out['m_ssm_a_re'], 'm_ssm_a_im': out['m_ssm_a_im'], 'm_ssm_log_dt': out['m_ssm_log_dt'], 'm_ssm_b_re': out['m_ssm_b_re'], 'm_ssm_b_im': out['m_ssm_b_im'], 'm_ssm_c_re': out['m_ssm_c_re'], 'm_ssm_c_im': out['m_ssm_c_im'], 'm_ssm_d': out['m_ssm_d'], 'm_w_glu': out['m_w_glu'], 'm_b_glu': out['m_b_glu'], 'm_w_ssm_up': out['m_w_ssm_up'], 'm_w_out': out['m_w_out'], 'm_ln1_g': out['m_ln1_g'], 'm_ln1_b': out['m_ln1_b'], 'm_w_ffn_in': out['m_w_ffn_in'], 'm_w_ffn_out': out['m_w_ffn_out'], 'm_ln2_g': out['m_ln2_g'], 'm_ln2_b': out['m_ln2_b'], 'v_w_ada': out['v_w_ada'], 'v_b_ada': out['v_b_ada'], 'v_w_in': out['v_w_in'], 'v_w_sb_up': out['v_w_sb_up'], 'v_ssm_a_re': out['v_ssm_a_re'], 'v_ssm_a_im': out['v_ssm_a_im'], 'v_ssm_log_dt': out['v_ssm_log_dt'], 'v_ssm_b_re': out['v_ssm_b_re'], 'v_ssm_b_im': out['v_ssm_b_im'], 'v_ssm_c_re': out['v_ssm_c_re'], 'v_ssm_c_im': out['v_ssm_c_im'], 'v_ssm_d': out['v_ssm_d'], 'v_w_glu': out['v_w_glu'], 'v_b_glu': out['v_b_glu'], 'v_w_ssm_up': out['v_w_ssm_up'], 'v_w_out': out['v_w_out'], 'v_ln1_g': out['v_ln1_g'], 'v_ln1_b': out['v_ln1_b'], 'v_w_ffn_in': out['v_w_ffn_in'], 'v_w_ffn_out': out['v_w_ffn_out'], 'v_ln2_g': out['v_ln2_g'], 'v_ln2_b': out['v_ln2_b']}


def _loss(weights, diff, rest, loss_target):
    with _jax.named_scope("forward"):
        args = {**rest, TWIN_DIFF_INPUT: diff, **{k: w.astype(_WEIGHT_DTYPES[k]) for k, w in weights.items()}}
        y = _forward(args)
    with _jax.named_scope("loss_head"):
        err = _jnp.square(y.astype(_jnp.float32) - loss_target)
        return 0.5 * _jnp.sum(_jnp.mean(err, axis=-1)) if err.ndim else 0.5 * err


def _adamw(w, g, m, v):
    m = ADAM_B1 * m + (1.0 - ADAM_B1) * g
    v = ADAM_B2 * v + (1.0 - ADAM_B2) * _jnp.square(g)
    m_hat = m / (1.0 - ADAM_B1 ** ADAM_STEP)
    v_hat = v / (1.0 - ADAM_B2 ** ADAM_STEP)
    delta = -ADAM_LR * (m_hat / (_jnp.sqrt(v_hat) + ADAM_EPS) + ADAM_WD * w)
    return delta, m, v


def reference(x, c, w_ada, b_ada, w_in, w_sb_up, ssm_a_re, ssm_a_im, ssm_log_dt, ssm_b_re, ssm_b_im, ssm_c_re, ssm_c_im, ssm_d, w_glu, b_glu, w_ssm_up, w_out, ln1_g, ln1_b, w_ffn_in, w_ffn_out, ln2_g, ln2_b, loss_target, m_w_ada, m_b_ada, m_w_in, m_w_sb_up, m_ssm_a_re, m_ssm_a_im, m_ssm_log_dt, m_ssm_b_re, m_ssm_b_im, m_ssm_c_re, m_ssm_c_im, m_ssm_d, m_w_glu, m_b_glu, m_w_ssm_up, m_w_out, m_ln1_g, m_ln1_b, m_w_ffn_in, m_w_ffn_out, m_ln2_g, m_ln2_b, v_w_ada, v_b_ada, v_w_in, v_w_sb_up, v_ssm_a_re, v_ssm_a_im, v_ssm_log_dt, v_ssm_b_re, v_ssm_b_im, v_ssm_c_re, v_ssm_c_im, v_ssm_d, v_w_glu, v_b_glu, v_w_ssm_up, v_w_out, v_ln1_g, v_ln1_b, v_w_ffn_in, v_w_ffn_out, v_ln2_g, v_ln2_b):
    given = dict(x=x, c=c, w_ada=w_ada, b_ada=b_ada, w_in=w_in, w_sb_up=w_sb_up, ssm_a_re=ssm_a_re, ssm_a_im=ssm_a_im, ssm_log_dt=ssm_log_dt, ssm_b_re=ssm_b_re, ssm_b_im=ssm_b_im, ssm_c_re=ssm_c_re, ssm_c_im=ssm_c_im, ssm_d=ssm_d, w_glu=w_glu, b_glu=b_glu, w_ssm_up=w_ssm_up, w_out=w_out, ln1_g=ln1_g, ln1_b=ln1_b, w_ffn_in=w_ffn_in, w_ffn_out=w_ffn_out, ln2_g=ln2_g, ln2_b=ln2_b, loss_target=loss_target, m_w_ada=m_w_ada, m_b_ada=m_b_ada, m_w_in=m_w_in, m_w_sb_up=m_w_sb_up, m_ssm_a_re=m_ssm_a_re, m_ssm_a_im=m_ssm_a_im, m_ssm_log_dt=m_ssm_log_dt, m_ssm_b_re=m_ssm_b_re, m_ssm_b_im=m_ssm_b_im, m_ssm_c_re=m_ssm_c_re, m_ssm_c_im=m_ssm_c_im, m_ssm_d=m_ssm_d, m_w_glu=m_w_glu, m_b_glu=m_b_glu, m_w_ssm_up=m_w_ssm_up, m_w_out=m_w_out, m_ln1_g=m_ln1_g, m_ln1_b=m_ln1_b, m_w_ffn_in=m_w_ffn_in, m_w_ffn_out=m_w_ffn_out, m_ln2_g=m_ln2_g, m_ln2_b=m_ln2_b, v_w_ada=v_w_ada, v_b_ada=v_b_ada, v_w_in=v_w_in, v_w_sb_up=v_w_sb_up, v_ssm_a_re=v_ssm_a_re, v_ssm_a_im=v_ssm_a_im, v_ssm_log_dt=v_ssm_log_dt, v_ssm_b_re=v_ssm_b_re, v_ssm_b_im=v_ssm_b_im, v_ssm_c_re=v_ssm_c_re, v_ssm_c_im=v_ssm_c_im, v_ssm_d=v_ssm_d, v_w_glu=v_w_glu, v_b_glu=v_b_glu, v_w_ssm_up=v_w_ssm_up, v_w_out=v_w_out, v_ln1_g=v_ln1_g, v_ln1_b=v_ln1_b, v_w_ffn_in=v_w_ffn_in, v_w_ffn_out=v_w_ffn_out, v_ln2_g=v_ln2_g, v_ln2_b=v_ln2_b)
    weights = {n: given[n] for n in TWIN_WEIGHTS}
    shared = {n: given[n] for n in SHARED_INPUTS}
    per_example = {n: given[n] for n in ['x', 'c']}
    grad_fn = _jax.value_and_grad(_loss, argnums=(0, 1))

    def one_microbatch(ex, loss_target):
        ex = dict(ex)
        diff = ex.pop(TWIN_DIFF_INPUT)
        return grad_fn(weights, diff, {**shared, **ex}, loss_target)

    if N_MICROBATCH == 1:
        loss, (grad_w, grad_x) = one_microbatch(per_example, given["loss_target"])
    else:
        def body(carry, xs):
            loss_sum, grad_sum = carry
            l_k, (gw_k, gx_k) = one_microbatch(xs[0], xs[1])
            with _jax.named_scope("update"):
                return (loss_sum + l_k, _jax.tree.map(_jnp.add, grad_sum, gw_k)), gx_k

        init = (_jnp.zeros((), _jnp.float32), _jax.tree.map(_jnp.zeros_like, weights))
        (loss, grad_w), grad_x = _jax.lax.scan(body, init, (per_example, given["loss_target"]))
    with _jax.named_scope("update"):
        delta_w, new_m, new_v = {}, {}, {}
        for n in TWIN_WEIGHTS:
            delta_w[n], new_m[n], new_v[n] = _adamw(weights[n], grad_w[n], given["m_" + n], given["v_" + n])
    return (loss, grad_x, *[grad_w[n] for n in TWIN_WEIGHTS], *[delta_w[n] for n in TWIN_WEIGHTS],
            *[new_m[n] for n in TWIN_WEIGHTS], *[new_v[n] for n in TWIN_WEIGHTS])
```

```python
import jax
import jax.numpy as jnp
from jax import lax
from jax.experimental import pallas as pl
from jax.experimental.pallas import tpu as pltpu

F32 = jnp.float32
BF16 = jnp.bfloat16

N_DEV = 8
LANES = 128
SUBLANES = 8
VMEM_BYTES = 64 * 1024 * 1024
HEAD_DIM = 64
SLAB_GROUPS = 8
LN_EPS = 1e-5
ADAM_LR, ADAM_B1, ADAM_B2, ADAM_EPS, ADAM_WD, ADAM_STEP = 0.001, 0.9, 0.999, 1e-08, 0.01, 10
SB_UNDERFLOW = -120.0

PACK_ROWS = 256
MESH_AXES = ("x", "y", "c")


def _vmem_limit(block_bytes):
    return int(min(max(3 * block_bytes + (8 << 20), 24 << 20), VMEM_BYTES - (8 << 20)))


def _nbytes(shape, dtype):
    n = 1
    for d in shape:
        if d is not None:
            n *= d
    return n * jnp.dtype(dtype).itemsize


def _spec(shape, fn):
    return pl.BlockSpec(shape, fn)


def _exchange(name, scatter, gather):
    arrs = list(scatter) + list(gather)
    n_sc, n = len(scatter), len(arrs)
    out_shapes = [a.shape if i < n_sc else (N_DEV,) + a.shape for i, a in enumerate(arrs)]

    def body(*refs):
        ins, outs = refs[:n], refs[n:2 * n]
        send_sems, recv_sems, own_sems = refs[2 * n:]
        x, y, c = lax.axis_index("x"), lax.axis_index("y"), lax.axis_index("c")
        me = 4 * x + 2 * y + c
        copies = []
        for a in range(n):
            src = ins[a].at[me] if a < n_sc else ins[a]
            cp = pltpu.make_async_copy(src, outs[a].at[me], own_sems.at[a])
            cp.start()
            copies.append(cp)
        for k in range(1, N_DEV):
            px = 1 - x if k & 4 else x
            py = 1 - y if k & 2 else y
            pc = 1 - c if k & 1 else c
            peer = 4 * px + 2 * py + pc
            for a in range(n):
                src = ins[a].at[peer] if a < n_sc else ins[a]
                cp = pltpu.make_async_remote_copy(
                    src_ref=src, dst_ref=outs[a].at[me],
                    send_sem=send_sems.at[a, k - 1], recv_sem=recv_sems.at[a, k - 1],
                    device_id=(px, py, pc), device_id_type=pl.DeviceIdType.MESH)
                cp.start()
                copies.append(cp)
        for cp in copies:
            cp.wait()

    hbm = pl.BlockSpec(memory_space=pltpu.HBM)
    return pl.pallas_call(
        body, name=name,
        in_specs=[hbm] * n, out_specs=[hbm] * n,
        out_shape=[jax.ShapeDtypeStruct(s, a.dtype) for s, a in zip(out_shapes, arrs)],
        scratch_shapes=[pltpu.SemaphoreType.DMA((n, N_DEV - 1)), pltpu.SemaphoreType.DMA((n, N_DEV - 1)),
                        pltpu.SemaphoreType.DMA((n,))],
    )(*arrs)


NN = (((1,), (0,)), ((), ()))
NT = (((1,), (1,)), ((), ()))
TN = (((0,), (0,)), ((), ()))


def _mm(name, a, b, a_spec, b_spec, o_spec, o_shape, o_dtype, grid, dims):
    nk = grid[2]
    acc_shape = tuple(d for d in o_spec.block_shape if d is not None)

    def body(a_ref, b_ref, o_ref, acc_ref):
        k = pl.program_id(2)

        @pl.when(k == 0)
        def _():
            acc_ref[...] = jnp.zeros_like(acc_ref)

        acc_ref[...] += lax.dot_general(a_ref[...].astype(BF16), b_ref[...].astype(BF16), dims,
                                        preferred_element_type=F32)

        @pl.when(k == nk - 1)
        def _():
            o_ref[...] = acc_ref[...].astype(o_ref.dtype)

    blk = (_nbytes(a_spec.block_shape, a.dtype) + _nbytes(b_spec.block_shape, b.dtype)
           + _nbytes(acc_shape, o_dtype) + _nbytes(acc_shape, F32))
    return pl.pallas_call(
        body, name=name, grid=grid, in_specs=[a_spec, b_spec], out_specs=o_spec,
        out_shape=jax.ShapeDtypeStruct(o_shape, o_dtype),
        scratch_shapes=[pltpu.VMEM(acc_shape, F32)],
        compiler_params=pltpu.CompilerParams(dimension_semantics=("parallel", "parallel", "arbitrary"),
                                             vmem_limit_bytes=_vmem_limit(blk)),
    )(a, b)


def _tile(n, pref=512):
    t = pref
    while t >= LANES:
        if n % t == 0:
            return t
        t -= LANES
    return n


def _rowwise(name, fn, ins, outs, grid):
    n_in = len(ins)

    def body(*refs):
        vals = fn(*[r[...].astype(F32) for r in refs[:n_in]])
        if not isinstance(vals, (tuple, list)):
            vals = (vals,)
        for r, v in zip(refs[n_in:], vals):
            r[...] = v.astype(r.dtype)

    blk = sum(_nbytes(bs, a.dtype) for a, bs, _ in ins) + sum(_nbytes(bs, d) + _nbytes(bs, F32) for _, d, bs, _ in outs)
    return pl.pallas_call(
        body, name=name, grid=grid,
        in_specs=[_spec(bs, im) for _, bs, im in ins],
        out_specs=[_spec(bs, im) for _, _, bs, im in outs],
        out_shape=[jax.ShapeDtypeStruct(s, d) for s, d, _, _ in outs],
        compiler_params=pltpu.CompilerParams(dimension_semantics=("parallel",) * len(grid),
                                             vmem_limit_bytes=_vmem_limit(2 * blk)),
    )(*[a for a, _, _ in ins])


def _rowwise_vjp(name, fn, ins, cts, wrt, grid):
    n_in, n_ct = len(ins), len(cts)
    idx = [w[0] for w in wrt]

    def body(*refs):
        prim = [r[...].astype(F32) for r in refs[:n_in]]
        ct = tuple(r[...].astype(F32) for r in refs[n_in:n_in + n_ct])
        o_refs = refs[n_in + n_ct:]

        def g(*sel):
            full = list(prim)
            for i, s in zip(idx, sel):
                full[i] = s
            out = fn(*full)
            return tuple(out) if isinstance(out, (tuple, list)) else (out,)

        _, pull = jax.vjp(g, *[prim[i] for i in idx])
        grads = pull(ct)
        first = pl.program_id(0) == 0
        for d in range(1, len(grid)):
            first = jnp.logical_and(first, pl.program_id(d) == 0)
        for w, o_ref, gr in zip(wrt, o_refs, grads):
            if w[1] == "row":
                o_ref[...] = gr.astype(o_ref.dtype)
            else:
                @pl.when(first)
                def _(o_ref=o_ref):
                    o_ref[...] = jnp.zeros_like(o_ref)

                o_ref[...] += gr.astype(o_ref.dtype)

    blk = (sum(_nbytes(bs, a.dtype) + _nbytes(bs, F32) for a, bs, _ in list(ins) + list(cts))
           + sum(_nbytes(w[4], w[3]) + _nbytes(w[4], F32) for w in wrt))
    return pl.pallas_call(
        body, name=name, grid=grid,
        in_specs=[_spec(bs, im) for _, bs, im in list(ins) + list(cts)],
        out_specs=[_spec(w[4], w[5]) for w in wrt],
        out_shape=[jax.ShapeDtypeStruct(w[2], w[3]) for w in wrt],
        compiler_params=pltpu.CompilerParams(dimension_semantics=("arbitrary",) * len(grid),
                                             vmem_limit_bytes=_vmem_limit(2 * blk)),
    )(*[a for a, _, _ in list(ins) + list(cts)])


def _normalize(x):
    mu = jnp.mean(x, axis=-1, keepdims=True)
    xc = x - mu
    var = jnp.mean(xc * xc, axis=-1, keepdims=True)
    return xc * lax.rsqrt(var + LN_EPS)


def _modulate(x, sc, sh):
    return _normalize(x) * (1.0 + sc) + sh


def _make_resid_fns(alpha):
    def resid_ln(x, y, gate, g, b):
        return _normalize(alpha * x + (1.0 + gate) * y) * g + b

    def resid_ln_mod(x, y, gate, g, b, sc, sh):
        xn = resid_ln(x, y, gate, g, b)
        return xn, _modulate(xn, sc, sh)

    return resid_ln, resid_ln_mod


def _merge_fn(y_sb, y_ssm, g_sb, g_ssm):
    return jax.nn.sigmoid(g_sb) * y_sb + jax.nn.sigmoid(g_ssm) * y_ssm


def _swiglu_fn(gate, up):
    return gate * jax.nn.sigmoid(gate) * up


def _s5_act_fn(yc, u, d_skip):
    return jax.nn.gelu(yc + d_skip * u)


def _s5_glu_fn(yc, u, t, d_skip, b_glu):
    return _s5_act_fn(yc, u, d_skip) * jax.nn.sigmoid(t + b_glu)


def _s5_post_fn(yc, u, t, d_skip, b_glu):
    y1 = _s5_act_fn(yc, u, d_skip)
    return y1, y1 * jax.nn.sigmoid(t + b_glu)


def _sb_tri(kind):
    row = lax.broadcasted_iota(jnp.int32, (LANES, LANES), 0)
    col = lax.broadcasted_iota(jnp.int32, (LANES, LANES), 1)
    if kind == "after":
        return (row > col).astype(BF16)
    if kind == "from":
        return (row >= col).astype(BF16)
    return col < row


def _split_dot(x, m):
    hi = x.astype(BF16)
    lo = (x - hi.astype(F32)).astype(BF16)
    return (lax.dot_general(hi, m, NN, preferred_element_type=F32)
            + lax.dot_general(lo, m, NN, preferred_element_type=F32))


def _sb_scores(qh, k2, scale):
    z = lax.dot_general(qh, k2, NT, preferred_element_type=F32) * scale
    sp = jnp.log(1.0 + jnp.exp(-jnp.abs(z)))
    log_beta = jnp.minimum(z, 0.0) - sp
    log_1m = -jnp.maximum(z, 0.0) - sp
    return log_beta, log_1m


def _sb_attention_fwd(proj, sb_width):
    seq = proj.shape[0]
    n_pair, n_q = sb_width // LANES, seq // LANES
    scale = 1.0 / (HEAD_DIM ** 0.5)

    def body(q_ref, k_ref, v_ref, o_ref, o32_ref):
        qi = pl.program_id(1)
        q2 = q_ref[...]
        lane = lax.broadcasted_iota(jnp.int32, (LANES, LANES), 1)
        m_after, causal = _sb_tri("after"), _sb_tri("mask")
        halves = []
        for half in range(2):
            mine = (lane < HEAD_DIM) if half == 0 else (lane >= HEAD_DIM)
            qh = jnp.where(mine, q2, 0.0).astype(BF16)

            def block(kb, carry, acc, acc_lo, diag, qh=qh):
                ks = pl.multiple_of(kb * LANES, LANES)
                k2 = k_ref[pl.ds(ks, LANES), :].astype(BF16)
                v2 = v_ref[pl.ds(ks, LANES), :].astype(BF16)
                log_beta, log_1m = _sb_scores(qh, k2, scale)
                if diag:
                    log_1m = jnp.where(causal, log_1m, 0.0)
                w = jnp.exp(log_beta + carry + _split_dot(log_1m, m_after))
                if diag:
                    w = jnp.where(causal, w, 0.0)
                w_hi = w.astype(BF16)
                w_lo = (w - w_hi.astype(F32)).astype(BF16)
                acc = acc + lax.dot_general(w_hi, v2, NN, preferred_element_type=F32)
                acc_lo = acc_lo + lax.dot_general(w_lo, v2, NN, preferred_element_type=F32)
                return carry + jnp.sum(log_1m, axis=1, keepdims=True), acc, acc_lo

            zero = jnp.zeros((LANES, LANES), F32)
            carry, acc, acc_lo = block(qi, jnp.zeros((LANES, 1), F32), zero, zero, True)

            def loop(st):
                kb, carry, acc, acc_lo = st
                carry, acc, acc_lo = block(kb, carry, acc, acc_lo, False)
                nxt = jnp.where(jnp.max(carry) < SB_UNDERFLOW, -1, kb - 1)
                return nxt, carry, acc, acc_lo

            _, _, acc, acc_lo = lax.while_loop(lambda st: st[0] >= 0, loop, (qi - 1, carry, acc, acc_lo))
            halves.append((acc, acc + acc_lo))
        o_ref[...] = jnp.where(lane < HEAD_DIM, halves[0][0], halves[1][0]).astype(o_ref.dtype)
        o32_ref[...] = jnp.where(lane < HEAD_DIM, halves[0][1], halves[1][1])

    q_spec = _spec((LANES, LANES), lambda h, i: (i, h))
    kv = [_spec((seq, LANES), lambda h, i, o=o: (0, o + h)) for o in (n_pair, 2 * n_pair)]
    o_spec = _spec((LANES, LANES), lambda h, i: (i, h))
    return pl.pallas_call(
        body, name="sb_attention_fwd", grid=(n_pair, n_q),
        in_specs=[q_spec] + kv, out_specs=[o_spec, o_spec],
        out_shape=[jax.ShapeDtypeStruct((seq, sb_width), BF16), jax.ShapeDtypeStruct((seq, sb_width), F32)],
        compiler_params=pltpu.CompilerParams(dimension_semantics=("parallel", "arbitrary"),
                                             vmem_limit_bytes=_vmem_limit(2 * seq * LANES * 4)),
    )(proj, proj, proj)


def _sb_attention_bwd(proj, o32, do, sb_width):
    seq = proj.shape[0]
    n_pair, n_q = sb_width // LANES, seq // LANES
    scale = 1.0 / (HEAD_DIM ** 0.5)

    def body(q_ref, k_ref, v_ref, o_ref, do_ref, dq_ref, dk_ref, dv_ref):
        qi = pl.program_id(1)

        @pl.when(qi == 0)
        def _():
            dk_ref[...] = jnp.zeros_like(dk_ref)
            dv_ref[...] = jnp.zeros_like(dv_ref)

        q2 = q_ref[...]
        do2 = do_ref[...].astype(F32)
        o2 = o_ref[...]
        lane = lax.broadcasted_iota(jnp.int32, (LANES, LANES), 1)
        m_after, m_from, causal = _sb_tri("after"), _sb_tri("from"), _sb_tri("mask")
        halves = []
        for half in range(2):
            mine = (lane < HEAD_DIM) if half == 0 else (lane >= HEAD_DIM)
            qh = jnp.where(mine, q2, 0.0).astype(BF16)
            doh = jnp.where(mine, do2, 0.0)
            doh_b = doh.astype(BF16)
            total = jnp.sum(doh * o2, axis=1, keepdims=True)

            def block(kb, carry, right, dq, diag, qh=qh, doh_b=doh_b, total=total):
                ks = pl.multiple_of(kb * LANES, LANES)
                k2 = k_ref[pl.ds(ks, LANES), :].astype(BF16)
                v2 = v_ref[pl.ds(ks, LANES), :].astype(BF16)
                log_beta, log_1m_raw = _sb_scores(qh, k2, scale)
                log_1m = jnp.where(causal, log_1m_raw, 0.0) if diag else log_1m_raw
                w = jnp.exp(log_beta + carry + _split_dot(log_1m, m_after))
                if diag:
                    w = jnp.where(causal, w, 0.0)
                d_arg = lax.dot_general(doh_b, v2, NT, preferred_element_type=F32) * w
                dv_ref[pl.ds(ks, LANES), :] += lax.dot_general(w.astype(BF16), doh_b, TN, preferred_element_type=F32)
                d_log_1m = total - right - _split_dot(d_arg, m_from)
                dz = d_arg * jnp.exp(log_1m_raw) - jnp.exp(log_beta) * d_log_1m
                if diag:
                    dz = jnp.where(causal, dz, 0.0)
                dz_b = (dz * scale).astype(BF16)
                dq = dq + lax.dot_general(dz_b, k2, NN, preferred_element_type=F32)
                dk_ref[pl.ds(ks, LANES), :] += lax.dot_general(dz_b, qh, TN, preferred_element_type=F32)
                return (carry + jnp.sum(log_1m, axis=1, keepdims=True),
                        right + jnp.sum(d_arg, axis=1, keepdims=True), dq)

            zcol = jnp.zeros((LANES, 1), F32)
            carry, right, dq = block(qi, zcol, zcol, jnp.zeros((LANES, LANES), F32), True)

            def loop(st):
                kb, carry, right, dq = st
                carry, right, dq = block(kb, carry, right, dq, False)
                nxt = jnp.where(jnp.max(carry) < SB_UNDERFLOW, -1, kb - 1)
                return nxt, carry, right, dq

            _, _, _, dq = lax.while_loop(lambda st: st[0] >= 0, loop, (qi - 1, carry, right, dq))
            halves.append(dq)
        dq_ref[...] = jnp.where(lane < HEAD_DIM, halves[0], halves[1]).astype(dq_ref.dtype)

    q_spec = _spec((LANES, LANES), lambda h, i: (i, h))
    kv = [_spec((seq, LANES), lambda h, i, o=o: (0, o + h)) for o in (n_pair, 2 * n_pair)]
    full = _spec((seq, LANES), lambda h, i: (0, h))
    return pl.pallas_call(
        body, name="sb_attention_bwd", grid=(n_pair, n_q),
        in_specs=[q_spec] + kv + [q_spec, q_spec], out_specs=[q_spec, full, full],
        out_shape=[jax.ShapeDtypeStruct((seq, sb_width), BF16), jax.ShapeDtypeStruct((seq, sb_width), F32),
                   jax.ShapeDtypeStruct((seq, sb_width), F32)],
        compiler_params=pltpu.CompilerParams(dimension_semantics=("parallel", "arbitrary"),
                                             vmem_limit_bytes=_vmem_limit(4 * seq * LANES * 4)),
    )(proj, proj, proj, o32, do)


def _s5_discretize(a_re, a_im, log_dt, b_re, b_im, c_re, c_im):
    n_g, n_p = a_re.shape
    c_g = b_re.shape[-1]
    ns = n_g // SLAB_GROUPS
    dt = jnp.exp(log_dt)[:, None]
    xr, xi = a_re * dt, a_im * dt
    mag = jnp.exp(xr)
    lr, li = mag * jnp.cos(xi), mag * jnp.sin(xi)
    den = a_re * a_re + a_im * a_im
    fr = ((lr - 1.0) * a_re + li * a_im) / den
    fi = (li * a_re - (lr - 1.0) * a_im) / den
    bb_re = fr[..., None] * b_re - fi[..., None] * b_im
    bb_im = fr[..., None] * b_im + fi[..., None] * b_re
    eye = jnp.eye(SLAB_GROUPS, dtype=F32)

    def diag_b(m):
        m = jnp.transpose(m.reshape(ns, SLAB_GROUPS, n_p, c_g), (0, 1, 3, 2))
        m = m[:, :, :, None, :] * eye[None, :, None, :, None]
        return m.reshape(ns, SLAB_GROUPS * c_g, SLAB_GROUPS * n_p)

    def diag_c(m):
        m = jnp.transpose(m.reshape(ns, SLAB_GROUPS, c_g, n_p), (0, 1, 3, 2))
        m = m[:, :, :, None, :] * eye[None, :, None, :, None]
        return m.reshape(ns, SLAB_GROUPS * n_p, SLAB_GROUPS * c_g)

    bs = jnp.concatenate([diag_b(bb_re), diag_b(bb_im)], axis=-1)
    cs = jnp.concatenate([diag_c(c_re), -diag_c(c_im)], axis=1)
    lam = jnp.concatenate([lr.reshape(ns, 1, -1), li.reshape(ns, 1, -1)], axis=-1)
    return bs, cs, lam


def _s5_scan_fwd(proj, u_col, bs, cs, lam, t_blk):
    seq = proj.shape[0]
    ns, _, w2 = bs.shape
    w = w2 // 2
    n_t = seq // t_blk

    def body(u_ref, bs_ref, cs_ref, lam_ref, yc_ref, h_ref, st_ref):
        @pl.when(pl.program_id(1) == 0)
        def _():
            st_ref[...] = jnp.zeros_like(st_ref)

        h_ref[...] = lax.dot_general(u_ref[...].astype(BF16), bs_ref[...], NN, preferred_element_type=F32)
        lr, li = lam_ref[:, :w], lam_ref[:, w:]

        def step(t, c):
            hr, hi = c
            nr = lr * hr - li * hi + h_ref[pl.ds(t, 1), :w]
            ni = li * hr + lr * hi + h_ref[pl.ds(t, 1), w:]
            h_ref[pl.ds(t, 1), :w] = nr
            h_ref[pl.ds(t, 1), w:] = ni
            return nr, ni

        hr, hi = lax.fori_loop(0, t_blk, step, (st_ref[:, :w], st_ref[:, w:]), unroll=SUBLANES)
        st_ref[:, :w] = hr
        st_ref[:, w:] = hi
        yc_ref[...] = lax.dot_general(h_ref[...].astype(BF16), cs_ref[...], NN, preferred_element_type=F32)

    return pl.pallas_call(
        body, name="s5_scan_fwd", grid=(ns, n_t),
        in_specs=[_spec((t_blk, LANES), lambda s, i: (i, u_col + s)),
                  _spec((None, LANES, w2), lambda s, i: (s, 0, 0)),
                  _spec((None, w2, LANES), lambda s, i: (s, 0, 0)),
                  _spec((None, 1, w2), lambda s, i: (s, 0, 0))],
        out_specs=[_spec((t_blk, LANES), lambda s, i: (i, s)),
                   _spec((None, t_blk, w2), lambda s, i: (s, i, 0))],
        out_shape=[jax.ShapeDtypeStruct((seq, ns * LANES), F32), jax.ShapeDtypeStruct((ns, seq, w2), F32)],
        scratch_shapes=[pltpu.VMEM((1, w2), F32)],
        compiler_params=pltpu.CompilerParams(dimension_semantics=("parallel", "arbitrary"),
                                             vmem_limit_bytes=_vmem_limit(2 * t_blk * w2 * 4)),
    )(proj, bs, cs, lam)


def _s5_scan_bwd(proj, u_col, states, d_yc, du_extra, bs, cs, lam, t_blk):
    seq = proj.shape[0]
    ns, _, w2 = bs.shape
    w = w2 // 2
    n_t = seq // t_blk
    rows8 = t_blk // SUBLANES

    def body(u_ref, h_ref, hp_ref, dyc_ref, dux_ref, bs_ref, cs_ref, lam_ref,
             du_ref, dbs_ref, dcs_ref, dlam_ref, g_ref, st_ref):
        i = pl.program_id(1)

        @pl.when(i == 0)
        def _():
            st_ref[...] = jnp.zeros_like(st_ref)
            dbs_ref[...] = jnp.zeros_like(dbs_ref)
            dcs_ref[...] = jnp.zeros_like(dcs_ref)
            dlam_ref[...] = jnp.zeros_like(dlam_ref)

        dyc_b = dyc_ref[...].astype(BF16)
        g_ref[...] = lax.dot_general(dyc_b, cs_ref[...], NT, preferred_element_type=F32)
        lr, li = lam_ref[:, :w], lam_ref[:, w:]

        def update(t, c, pr, pi):
            gr_n, gi_n, acc_r, acc_i = c
            gr = g_ref[pl.ds(t, 1), :w] + lr * gr_n + li * gi_n
            gi = g_ref[pl.ds(t, 1), w:] + lr * gi_n - li * gr_n
            g_ref[pl.ds(t, 1), :w] = gr
            g_ref[pl.ds(t, 1), w:] = gi
            return gr, gi, acc_r + gr * pr + gi * pi, acc_i + gi * pr - gr * pi

        def step(j, c):
            t = t_blk - 1 - j
            return update(t, c, h_ref[pl.ds(t - 1, 1), :w], h_ref[pl.ds(t - 1, 1), w:])

        zero = jnp.zeros((1, w), F32)
        c = lax.fori_loop(0, t_blk - 1, step, (st_ref[:, :w], st_ref[:, w:], zero, zero), unroll=SUBLANES)
        has_prev = (i < n_t - 1).astype(F32)
        c = update(0, c, hp_ref[SUBLANES - 1:, :w] * has_prev, hp_ref[SUBLANES - 1:, w:] * has_prev)
        st_ref[:, :w] = c[0]
        st_ref[:, w:] = c[1]
        dlam_ref[:, :w] += c[2]
        dlam_ref[:, w:] += c[3]
        g_b = g_ref[...].astype(BF16)
        du = lax.dot_general(g_b, bs_ref[...], NT, preferred_element_type=F32) + dux_ref[...]
        du_ref[...] = du.astype(du_ref.dtype)
        dbs_ref[...] += lax.dot_general(u_ref[...].astype(BF16), g_b, TN, preferred_element_type=F32)
        dcs_ref[...] += lax.dot_general(h_ref[...].astype(BF16), dyc_b, TN, preferred_element_type=F32)

    rev = lambda i: n_t - 1 - i
    return pl.pallas_call(
        body, name="s5_scan_bwd", grid=(ns, n_t),
        in_specs=[_spec((t_blk, LANES), lambda s, i: (rev(i), u_col + s)),
                  _spec((None, t_blk, w2), lambda s, i: (s, rev(i), 0)),
                  _spec((None, SUBLANES, w2), lambda s, i: (s, jnp.maximum(rev(i) * rows8 - 1, 0), 0)),
                  _spec((t_blk, LANES), lambda s, i: (rev(i), s)),
                  _spec((t_blk, LANES), lambda s, i: (rev(i), s)),
                  _spec((None, LANES, w2), lambda s, i: (s, 0, 0)),
                  _spec((None, w2, LANES), lambda s, i: (s, 0, 0)),
                  _spec((None, 1, w2), lambda s, i: (s, 0, 0))],
        out_specs=[_spec((t_blk, LANES), lambda s, i: (rev(i), s)),
                   _spec((None, LANES, w2), lambda s, i: (s, 0, 0)),
                   _spec((None, w2, LANES), lambda s, i: (s, 0, 0)),
                   _spec((None, 1, w2), lambda s, i: (s, 0, 0))],
        out_shape=[jax.ShapeDtypeStruct((seq, ns * LANES), BF16), jax.ShapeDtypeStruct(bs.shape, F32),
                   jax.ShapeDtypeStruct(cs.shape, F32), jax.ShapeDtypeStruct(lam.shape, F32)],
        scratch_shapes=[pltpu.VMEM((t_blk, w2), F32), pltpu.VMEM((1, w2), F32)],
        compiler_params=pltpu.CompilerParams(dimension_semantics=("parallel", "arbitrary"),
                                             vmem_limit_bytes=_vmem_limit(4 * t_blk * w2 * 4)),
    )(proj, states, states, d_yc, du_extra, bs, cs, lam)


def _loss_head(y, target, t_m):
    seq, d = y.shape

    def body(y_ref, t_ref, loss_ref, dy_ref):
        @pl.when(pl.program_id(0) == 0)
        def _():
            loss_ref[...] = jnp.zeros_like(loss_ref)

        diff = y_ref[...] - t_ref[...]
        dy_ref[...] = diff / d
        loss_ref[...] += 0.5 * jnp.sum(diff * diff) / d

    row = _spec((t_m, d), lambda i: (i, 0))
    return pl.pallas_call(
        body, name="loss_head", grid=(seq // t_m,), in_specs=[row, row],
        out_specs=[_spec((SUBLANES, LANES), lambda i: (0, 0)), row],
        out_shape=[jax.ShapeDtypeStruct((SUBLANES, LANES), F32), jax.ShapeDtypeStruct((seq, d), F32)],
        compiler_params=pltpu.CompilerParams(dimension_semantics=("arbitrary",),
                                             vmem_limit_bytes=_vmem_limit(6 * t_m * d * 4)),
    )(y, target)


def _adamw_fn(w, m, v, *partials):
    g = partials[0]
    for p in partials[1:]:
        g = g + p
    m2 = ADAM_B1 * m + (1.0 - ADAM_B1) * g
    v2 = ADAM_B2 * v + (1.0 - ADAM_B2) * (g * g)
    m_hat = m2 / (1.0 - ADAM_B1 ** ADAM_STEP)
    v_hat = v2 / (1.0 - ADAM_B2 ** ADAM_STEP)
    delta = -ADAM_LR * (m_hat / (jnp.sqrt(v_hat) + ADAM_EPS) + ADAM_WD * w)
    return g, delta, m2, v2


def _adamw(name, w, m, v, partials):
    rows, cols = w.shape
    t_r = rows
    for cand in (512, 256, 128, 64, 32, 16, 8):
        if rows % cand == 0 and cand * cols * 4 <= (1 << 20):
            t_r = cand
            break
    n_p = partials.shape[0]
    row = lambda i: (i, 0)
    ins = [(a, (t_r, cols), row) for a in (w, m, v)]
    ins += [(partials, (None, t_r, cols), (lambda i, j=j: (j, i, 0))) for j in range(n_p)]
    outs = [((rows, cols), F32, (t_r, cols), row)] * 4
    return _rowwise(name, _adamw_fn, ins, outs, (rows // t_r,))


SMALL_PARAMS = ("b_ada", "ssm_a_re", "ssm_a_im", "ssm_log_dt", "ssm_b_re", "ssm_b_im", "ssm_c_re", "ssm_c_im",
                "ssm_d", "b_glu", "ln1_g", "ln1_b", "ln2_g", "ln2_b")
WEIGHTS = ("w_ada", "b_ada", "w_in", "w_sb_up", "ssm_a_re", "ssm_a_im", "ssm_log_dt", "ssm_b_re", "ssm_b_im",
           "ssm_c_re", "ssm_c_im", "ssm_d", "w_glu", "b_glu", "w_ssm_up", "w_out", "ln1_g", "ln1_b", "w_ffn_in",
           "w_ffn_out", "ln2_g", "ln2_b")
ARG_NAMES = (("x", "c") + WEIGHTS + ("loss_target",) + tuple("m_" + n for n in WEIGHTS)
             + tuple("v_" + n for n in WEIGHTS))


def _pack(arrs):
    flat = jnp.concatenate([a.reshape(-1) for a in arrs])
    pad = (-flat.shape[0]) % (PACK_ROWS * LANES)
    return jnp.pad(flat, (0, pad)).reshape(-1, LANES)


def _unpack(packed, like):
    lead = packed.shape[:-2]
    flat = packed.reshape(lead + (-1,))
    out, off = [], 0
    for a in like:
        out.append(flat[..., off:off + a.size].reshape(lead + a.shape))
        off += a.size
    return out


def kernel(x, c, w_ada, b_ada, w_in, w_sb_up, ssm_a_re, ssm_a_im, ssm_log_dt, ssm_b_re, ssm_b_im, ssm_c_re,
           ssm_c_im, ssm_d, w_glu, b_glu, w_ssm_up, w_out, ln1_g, ln1_b, w_ffn_in, w_ffn_out, ln2_g, ln2_b,
           loss_target, m_w_ada, m_b_ada, m_w_in, m_w_sb_up, m_ssm_a_re, m_ssm_a_im, m_ssm_log_dt, m_ssm_b_re,
           m_ssm_b_im, m_ssm_c_re, m_ssm_c_im, m_ssm_d, m_w_glu, m_b_glu, m_w_ssm_up, m_w_out, m_ln1_g, m_ln1_b,
           m_w_ffn_in, m_w_ffn_out, m_ln2_g, m_ln2_b, v_w_ada, v_b_ada, v_w_in, v_w_sb_up, v_ssm_a_re, v_ssm_a_im,
           v_ssm_log_dt, v_ssm_b_re, v_ssm_b_im, v_ssm_c_re, v_ssm_c_im, v_ssm_d, v_w_glu, v_b_glu, v_w_ssm_up,
           v_w_out, v_ln1_g, v_ln1_b, v_w_ffn_in, v_w_ffn_out, v_ln2_g, v_ln2_b):
    given = locals()
    return _train_step({n: given[n] for n in ARG_NAMES})


def _train_step(p):
    x0 = p["x"][0]
    target = p["loss_target"][0]
    seq, d = x0.shape
    depth = p["w_ada"].shape[0]
    n_ada = p["w_ada"].shape[2]
    n_in = p["w_in"].shape[2]
    sb_w = p["w_sb_up"].shape[1]
    ssm_w = p["w_ssm_up"].shape[1]
    n_up = p["w_sb_up"].shape[2]
    n_ffn = p["w_ffn_in"].shape[2]
    ffn = N_DEV * p["w_ffn_out"].shape[1]
    in_cols = N_DEV * n_in
    alpha = (2 * depth) ** 0.25
    resid_ln, resid_ln_mod = _make_resid_fns(alpha)
    t_m = min(512, seq)
    n_m = seq // t_m
    t_d = _tile(d)
    assert n_ffn * (N_DEV // 2) == ffn and sb_w % LANES == 0 and ssm_w % LANES == 0 and d % LANES == 0
    assert n_in % LANES == 0 and n_up % LANES == 0 and seq % t_m == 0 and in_cols == 3 * sb_w + ssm_w + 2 * d
    assert (3 * sb_w) % ssm_w == 0 and (3 * sb_w + ssm_w) % d == 0

    col_sharded = ("w_in", "w_sb_up", "w_ssm_up", "w_ffn_in")
    row_sharded = ("w_glu", "w_out", "w_ffn_out")
    gathered = _exchange("gather_weights", [], [p[n].astype(BF16) for n in col_sharded + row_sharded] + [p["c"]])
    wg = dict(zip(col_sharded + row_sharded, gathered[:-1]))
    for n in row_sharded:
        wg[n] = jnp.swapaxes(wg[n], 0, 1).reshape(depth, -1, wg[n].shape[-1])
    c_all = gathered[-1].reshape(N_DEV, d)
    w_ffn_out4 = wg["w_ffn_out"].reshape(depth, N_DEV // 2, n_ffn, d)

    c_pad = jnp.pad(c_all, ((0, 2 * SUBLANES - N_DEV), (0, 0)))
    c_act = _rowwise("silu_c", lambda v: v * jax.nn.sigmoid(v), [(c_pad, c_pad.shape, lambda i: (0, 0))],
                     [(c_pad.shape, F32, c_pad.shape, lambda i: (0, 0))], (1,))[0]
    rows_c = c_pad.shape[0]
    mod_cols = [
        _mm(f"mod_{l}", c_act, p["w_ada"],
            _spec((rows_c, d), lambda i, j, k: (0, 0)), _spec((None, d, n_ada), lambda i, j, k, l=l: (l, 0, 0)),
            _spec((rows_c, n_ada), lambda i, j, k: (0, 0)), (rows_c, n_ada), F32, (1, 1, 1), NN)
        for l in range(depth)]
    mod_send = jnp.stack([m[:N_DEV] for m in mod_cols], axis=1)
    mod_recv = _exchange("exchange_mod", [mod_send], [])[0]
    mod_nobias = jnp.swapaxes(mod_recv, 0, 1).reshape(depth, N_DEV * n_ada)
    full2 = lambda a: (a, a.shape, lambda i: (0, 0))
    mod = _rowwise("mod_bias", lambda a, b: a + b, [full2(mod_nobias), full2(p["b_ada"])],
                   [(mod_nobias.shape, F32, mod_nobias.shape, lambda i: (0, 0))], (1,))[0]
    vec = lambda a: a.reshape(1, -1)
    mods = [[vec(mod[l, j * d:(j + 1) * d]) for j in range(6)] for l in range(depth)]
    ln = {n: [vec(p[n][l]) for l in range(depth)] for n in ("ln1_g", "ln1_b", "ln2_g", "ln2_b")}

    row_spec = lambda width: ((t_m, width), lambda i: (i, 0))
    col_spec = lambda width, cb: ((t_m, width), lambda i, cb=cb: (i, cb))
    vec_spec = lambda width: ((1, width), lambda i: (0, 0))
    rows_in = lambda a: (a,) + row_spec(a.shape[1])
    vec_in = lambda a: (a,) + vec_spec(a.shape[1])
    row_out = lambda width, dt: ((seq, width), dt) + row_spec(width)

    s5 = [_s5_discretize(*[p[n][l] for n in ("ssm_a_re", "ssm_a_im", "ssm_log_dt", "ssm_b_re", "ssm_b_im",
                                               "ssm_c_re", "ssm_c_im")]) for l in range(depth)]
    s5_b16 = [(bs.astype(BF16), cs.astype(BF16), lam) for bs, cs, lam in s5]
    t_scan = min(512, seq)
    u_col = 3 * sb_w // LANES
    g_sb_cb, g_ssm_cb = (3 * sb_w + ssm_w) // d, (3 * sb_w + ssm_w) // d + 1
    ssm_d = [vec(p["ssm_d"][l]) for l in range(depth)]
    b_glu = [vec(p["b_glu"][l]) for l in range(depth)]
    n_half = N_DEV // 2

    h = _rowwise("modulate_in", _modulate, [rows_in(x0), vec_in(mods[0][1]), vec_in(mods[0][0])],
                 [row_out(d, BF16)], (n_m,))[0]
    saved = []
    x_cur = x0
    for l in range(depth):
        sv = {"x_in": x_cur, "h": h}
        t_n = _tile(n_in)
        r_n = n_in // t_n
        proj = _mm(f"proj_{l}", h, wg["w_in"],
                   _spec((t_m, d), lambda i, j, k: (i, 0)),
                   _spec((None, None, d, t_n), lambda i, j, k, l=l, r=r_n: (j // r, l, 0, j % r)),
                   _spec((t_m, t_n), lambda i, j, k: (i, j)), (seq, in_cols), F32, (n_m, N_DEV * r_n, 1), NN)
        o_sb, o_sb32 = _sb_attention_fwd(proj, sb_w)
        bs16, cs16, lam = s5_b16[l]
        yc, states = _s5_scan_fwd(proj, u_col, bs16, cs16, lam, t_scan)
        u_in = (proj,) + col_spec(ssm_w, 3 * sb_w // ssm_w)
        y1 = _rowwise(f"s5_act_{l}", _s5_act_fn, [rows_in(yc), u_in, vec_in(ssm_d[l])],
                      [row_out(ssm_w, BF16)], (n_m,))[0]
        t_glu = _mm(f"s5_glu_mm_{l}", y1, wg["w_glu"],
                    _spec((t_m, ssm_w), lambda i, j, k: (i, 0)), _spec((None, ssm_w, ssm_w), lambda i, j, k, l=l: (l, 0, 0)),
                    _spec((t_m, ssm_w), lambda i, j, k: (i, 0)), (seq, ssm_w), F32, (n_m, 1, 1), NN)
        s5_out = _rowwise(f"s5_glu_{l}", _s5_glu_fn,
                          [rows_in(yc), u_in, rows_in(t_glu), vec_in(ssm_d[l]), vec_in(b_glu[l])],
                          [row_out(ssm_w, BF16)], (n_m,))[0]

        def up_proj(name, a, w, l=l):
            return _mm(name, a, w, _spec((t_m, a.shape[1]), lambda i, j, k: (i, 0)),
                       _spec((None, None, a.shape[1], n_up), lambda i, j, k: (j, l, 0, 0)),
                       _spec((t_m, n_up), lambda i, j, k: (i, j)), (seq, d), F32, (n_m, N_DEV, 1), NN)

        y_sb = up_proj(f"sb_up_{l}", o_sb, wg["w_sb_up"])
        y_ssm = up_proj(f"ssm_up_{l}", s5_out, wg["w_ssm_up"])
        gates = [(proj,) + col_spec(d, g_sb_cb), (proj,) + col_spec(d, g_ssm_cb)]
        merged = _rowwise(f"merge_{l}", _merge_fn, [rows_in(y_sb), rows_in(y_ssm)] + gates,
                          [row_out(d, BF16)], (n_m,))[0]
        y_mix = _mm(f"out_proj_{l}", merged, wg["w_out"],
                    _spec((t_m, d), lambda i, j, k: (i, 0)), _spec((None, d, t_d), lambda i, j, k, l=l: (l, 0, j)),
                    _spec((t_m, t_d), lambda i, j, k: (i, j)), (seq, d), F32, (n_m, d // t_d, 1), NN)
        vecs_a = [mods[l][2], ln["ln1_g"][l], ln["ln1_b"][l], mods[l][4], mods[l][3]]
        x_mid, h2 = _rowwise(f"resid_mix_{l}", resid_ln_mod, [rows_in(x_cur), rows_in(y_mix)] + [vec_in(v) for v in vecs_a],
                             [row_out(d, F32), row_out(d, BF16)], (n_m,))
        a_ffn = _mm(f"ffn_in_{l}", h2, wg["w_ffn_in"],
                    _spec((t_m, d), lambda i, j, k: (i, 0)), _spec((None, None, d, n_ffn), lambda i, j, k, l=l: (j, l, 0, 0)),
                    _spec((None, t_m, n_ffn), lambda i, j, k: (j, i, 0)), (N_DEV, seq, n_ffn), F32, (n_m, N_DEV, 1), NN)
        blk_in = lambda a, off: (a, (None, t_m, n_ffn), (lambda j, i, off=off: (j + off, i, 0)))
        f_act = _rowwise(f"swiglu_{l}", _swiglu_fn, [blk_in(a_ffn, 0), blk_in(a_ffn, n_half)],
                         [((n_half, seq, n_ffn), BF16, (None, t_m, n_ffn), lambda j, i: (j, i, 0))], (n_half, n_m))[0]
        y_ffn = _mm(f"ffn_out_{l}", f_act, w_ffn_out4,
                    _spec((None, t_m, n_ffn), lambda i, j, k: (k, i, 0)),
                    _spec((None, None, n_ffn, t_d), lambda i, j, k, l=l: (l, k, 0, j)),
                    _spec((t_m, t_d), lambda i, j, k: (i, j)), (seq, d), F32, (n_m, d // t_d, n_half), NN)
        last = l == depth - 1
        vecs_b = [mods[l][5], ln["ln2_g"][l], ln["ln2_b"][l]] + ([] if last else [mods[l + 1][1], mods[l + 1][0]])
        outs_b = [row_out(d, F32)] + ([] if last else [row_out(d, BF16)])
        res = _rowwise(f"resid_ffn_{l}", resid_ln if last else resid_ln_mod,
                       [rows_in(x_mid), rows_in(y_ffn)] + [vec_in(v) for v in vecs_b], outs_b, (n_m,))
        sv.update(proj=proj, o_sb=o_sb, o_sb32=o_sb32, yc=yc, states=states, y1=y1, t_glu=t_glu, s5_out=s5_out,
                  y_sb=y_sb, y_ssm=y_ssm, merged=merged, y_mix=y_mix, x_mid=x_mid, h2=h2, a_ffn=a_ffn, f_act=f_act,
                  y_ffn=y_ffn, vecs_a=vecs_a, vecs_b=vecs_b)
        saved.append(sv)
        x_cur = res[0]
        h = None if last else res[1]

    loss_part, d_x = _loss_head(x_cur, target, t_m)
    loss = lax.psum(loss_part[0, 0], MESH_AXES)

    d_h_next = None
    grads = {n: [None] * depth for n in WEIGHTS}
    d_mod = [[None] * 6 for _ in range(depth)]
    big_partials = [None] * depth
    big_recv = [None] * depth
    row_wrt = lambda i, width, dt: (i, "row", (seq, width), dt) + row_spec(width)
    sum_wrt = lambda i, width: (i, "sum", (1, width), F32) + vec_spec(width)
    for l in reversed(range(depth)):
        sv = saved[l]
        last = l == depth - 1
        ins_b = [rows_in(sv["x_mid"]), rows_in(sv["y_ffn"])] + [vec_in(v) for v in sv["vecs_b"]]
        cts_b = [rows_in(d_x)] + ([] if last else [rows_in(d_h_next)])
        wrt_b = [row_wrt(0, d, F32), row_wrt(1, d, BF16)] + [sum_wrt(2 + j, d) for j in range(len(sv["vecs_b"]))]
        res = _rowwise_vjp(f"resid_ffn_bwd_{l}", resid_ln if last else resid_ln_mod, ins_b, cts_b, wrt_b, (n_m,))
        d_x_mid, d_y_ffn = res[0], res[1]
        d_mod[l][5], grads["ln2_g"][l], grads["ln2_b"][l] = res[2], res[3], res[4]
        if not last:
            d_mod[l + 1][1], d_mod[l + 1][0] = res[5], res[6]
        d_f = _mm(f"ffn_out_dx_{l}", d_y_ffn, w_ffn_out4,
                  _spec((t_m, d), lambda i, j, k: (i, 0)), _spec((None, None, n_ffn, d), lambda i, j, k, l=l: (l, j, 0, 0)),
                  _spec((None, t_m, n_ffn), lambda i, j, k: (j, i, 0)), (n_half, seq, n_ffn), F32, (n_m, n_half, 1), NT)
        g_ffn_out = _mm(f"ffn_out_dw_{l}", sv["f_act"], d_y_ffn,
                        _spec((None, t_m, n_ffn), lambda i, j, k: (i, k, 0)), _spec((t_m, t_d), lambda i, j, k: (k, j)),
                        _spec((None, n_ffn, t_d), lambda i, j, k: (i, 0, j)), (n_half, n_ffn, d), F32,
                        (n_half, d // t_d, n_m), TN)
        blk_in = lambda a, off: (a, (None, t_m, n_ffn), (lambda j, i, off=off: (j + off, i, 0)))
        d_a_parts = _rowwise_vjp(f"swiglu_bwd_{l}", _swiglu_fn, [blk_in(sv["a_ffn"], 0), blk_in(sv["a_ffn"], n_half)],
                                 [blk_in(d_f, 0)],
                                 [(0, "row", (n_half, seq, n_ffn), BF16, (None, t_m, n_ffn), lambda j, i: (j, i, 0)),
                                  (1, "row", (n_half, seq, n_ffn), BF16, (None, t_m, n_ffn), lambda j, i: (j, i, 0))],
                                 (n_half, n_m))
        d_a = jnp.concatenate(d_a_parts, axis=0)
        d_h2 = _mm(f"ffn_in_dx_{l}", d_a, wg["w_ffn_in"],
                   _spec((None, t_m, n_ffn), lambda i, j, k: (k, i, 0)),
                   _spec((None, None, t_d, n_ffn), lambda i, j, k, l=l: (k, l, j, 0)),
                   _spec((t_m, t_d), lambda i, j, k: (i, j)), (seq, d), F32, (n_m, d // t_d, N_DEV), NT)
        g_ffn_in = _mm(f"ffn_in_dw_{l}", sv["h2"], d_a,
                       _spec((t_m, t_d), lambda i, j, k: (k, j)), _spec((None, t_m, n_ffn), lambda i, j, k: (i, k, 0)),
                       _spec((None, t_d, n_ffn), lambda i, j, k: (i, j, 0)), (N_DEV, d, n_ffn), F32,
                       (N_DEV, d // t_d, n_m), TN)
        ins_a = [rows_in(sv["x_in"]), rows_in(sv["y_mix"])] + [vec_in(v) for v in sv["vecs_a"]]
        wrt_a = [row_wrt(0, d, F32), row_wrt(1, d, BF16)] + [sum_wrt(2 + j, d) for j in range(5)]
        res = _rowwise_vjp(f"resid_mix_bwd_{l}", resid_ln_mod, ins_a, [rows_in(d_x_mid), rows_in(d_h2)], wrt_a, (n_m,))
        d_x_in, d_y_mix = res[0], res[1]
        d_mod[l][2], grads["ln1_g"][l], grads["ln1_b"][l], d_mod[l][4], d_mod[l][3] = res[2:7]
        d_merged = _mm(f"out_proj_dx_{l}", d_y_mix, wg["w_out"],
                       _spec((t_m, d), lambda i, j, k: (i, 0)), _spec((None, t_d, d), lambda i, j, k, l=l: (l, j, 0)),
                       _spec((t_m, t_d), lambda i, j, k: (i, j)), (seq, d), F32, (n_m, d // t_d, 1), NT)
        g_out = _mm(f"out_proj_dw_{l}", sv["merged"], d_y_mix,
                    _spec((t_m, t_d), lambda i, j, k: (k, i)), _spec((t_m, t_d), lambda i, j, k: (k, j)),
                    _spec((t_d, t_d), lambda i, j, k: (i, j)), (d, d), F32, (d // t_d, d // t_d, n_m), TN)
        gates = [(sv["proj"],) + col_spec(d, g_sb_cb), (sv["proj"],) + col_spec(d, g_ssm_cb)]
        d_y_sb, d_y_ssm, d_g_sb, d_g_ssm = _rowwise_vjp(
            f"merge_bwd_{l}", _merge_fn, [rows_in(sv["y_sb"]), rows_in(sv["y_ssm"])] + gates, [rows_in(d_merged)],
            [row_wrt(j, d, BF16) for j in range(4)], (n_m,))

        def up_bwd(name, act, d_y, w, dx_dtype, l=l):
            k_w = act.shape[1]
            dx = _mm(name + "_dx", d_y, w, _spec((t_m, n_up), lambda i, j, k: (i, k)),
                     _spec((None, None, k_w, n_up), lambda i, j, k: (k, l, 0, 0)),
                     _spec((t_m, k_w), lambda i, j, k: (i, 0)), (seq, k_w), dx_dtype, (n_m, 1, N_DEV), NT)
            dw = _mm(name + "_dw", act, d_y, _spec((t_m, k_w), lambda i, j, k: (k, 0)),
                     _spec((t_m, n_up), lambda i, j, k: (k, i)),
                     _spec((None, k_w, n_up), lambda i, j, k: (i, 0, 0)), (N_DEV, k_w, n_up), F32,
                     (N_DEV, 1, n_m), TN)
            return dx, dw

        d_o_sb, g_sb_up = up_bwd(f"sb_up_{l}", sv["o_sb"], d_y_sb, wg["w_sb_up"], BF16)
        d_s5_out, g_ssm_up = up_bwd(f"ssm_up_{l}", sv["s5_out"], d_y_ssm, wg["w_ssm_up"], F32)
        d_q, d_k, d_v = _sb_attention_bwd(sv["proj"], sv["o_sb32"], d_o_sb, sb_w)
        u_in = (sv["proj"],) + col_spec(ssm_w, 3 * sb_w // ssm_w)
        ins_s5 = [rows_in(sv["yc"]), u_in, rows_in(sv["t_glu"]), vec_in(ssm_d[l]), vec_in(b_glu[l])]
        d_t = _rowwise_vjp(f"s5_glu_bwd_{l}", _s5_glu_fn, ins_s5, [rows_in(d_s5_out)],
                           [row_wrt(2, ssm_w, BF16)], (n_m,))[0]
        d_y1 = _mm(f"s5_glu_mm_dx_{l}", d_t, wg["w_glu"],
                   _spec((t_m, ssm_w), lambda i, j, k: (i, 0)), _spec((None, ssm_w, ssm_w), lambda i, j, k, l=l: (l, 0, 0)),
                   _spec((t_m, ssm_w), lambda i, j, k: (i, 0)), (seq, ssm_w), F32, (n_m, 1, 1), NT)
        g_glu = _mm(f"s5_glu_mm_dw_{l}", sv["y1"], d_t,
                    _spec((t_m, ssm_w), lambda i, j, k: (k, 0)), _spec((t_m, ssm_w), lambda i, j, k: (k, 0)),
                    _spec((ssm_w, ssm_w), lambda i, j, k: (0, 0)), (ssm_w, ssm_w), F32, (1, 1, n_m), TN)
        d_yc, d_u_skip, grads["ssm_d"][l], grads["b_glu"][l] = _rowwise_vjp(
            f"s5_post_bwd_{l}", _s5_post_fn, ins_s5, [rows_in(d_y1), rows_in(d_s5_out)],
            [row_wrt(0, ssm_w, F32), row_wrt(1, ssm_w, F32), sum_wrt(3, ssm_w), sum_wrt(4, ssm_w)], (n_m,))
        bs16, cs16, lam = s5_b16[l]
        d_u, d_bs, d_cs, d_lam = _s5_scan_bwd(sv["proj"], u_col, sv["states"], d_yc, d_u_skip, bs16, cs16, lam, t_scan)
        raw = [p[n][l] for n in ("ssm_a_re", "ssm_a_im", "ssm_log_dt", "ssm_b_re", "ssm_b_im", "ssm_c_re", "ssm_c_im")]
        _, pull = jax.vjp(_s5_discretize, *raw)
        (grads["ssm_a_re"][l], grads["ssm_a_im"][l], grads["ssm_log_dt"][l], grads["ssm_b_re"][l],
         grads["ssm_b_im"][l], grads["ssm_c_re"][l], grads["ssm_c_im"][l]) = pull((d_bs, d_cs, d_lam))
        d_proj = jnp.concatenate([d_q, d_k.astype(BF16), d_v.astype(BF16), d_u, d_g_sb, d_g_ssm], axis=1)
        t_n = _tile(n_in)
        d_h = _mm(f"proj_dx_{l}", d_proj, wg["w_in"],
                  _spec((t_m, n_in), lambda i, j, k: (i, k)), _spec((None, None, t_d, n_in), lambda i, j, k, l=l: (k, l, j, 0)),
                  _spec((t_m, t_d), lambda i, j, k: (i, j)), (seq, d), F32, (n_m, d // t_d, N_DEV), NT)
        g_in = _mm(f"proj_dw_{l}", sv["h"], d_proj,
                   _spec((t_m, t_d), lambda i, j, k: (k, j)), _spec((t_m, n_in), lambda i, j, k: (k, i)),
                   _spec((None, t_d, n_in), lambda i, j, k: (i, j, 0)), (N_DEV, d, n_in), F32,
                   (N_DEV, d // t_d, n_m), TN)
        big_partials[l] = [g_in, g_sb_up, g_ssm_up, g_ffn_in,
                           g_glu.reshape(N_DEV, -1, ssm_w), g_out.reshape(N_DEV, -1, d), g_ffn_out.reshape(N_DEV, -1, d)]
        if l > 0:
            big_recv[l] = _exchange(f"exchange_grads_{l}", big_partials[l], [])
        d_x, d_h_next = d_x_in, d_h
    res = _rowwise_vjp("modulate_in_bwd", lambda v, sc, sh: (v, _modulate(v, sc, sh)),
                       [rows_in(x0), vec_in(mods[0][1]), vec_in(mods[0][0])], [rows_in(d_x), rows_in(d_h_next)],
                       [row_wrt(0, d, F32), sum_wrt(1, d), sum_wrt(2, d)], (n_m,))
    grad_x, d_mod[0][1], d_mod[0][0] = res

    d_mod_rows = jnp.concatenate([jnp.concatenate(d_mod[l], axis=1) for l in range(depth)], axis=0)
    grads["b_ada"] = [d_mod_rows[l] for l in range(depth)]
    small_local = [jnp.stack([g.reshape(p[n].shape[1:]) for g in grads[n]]) for n in SMALL_PARAMS]
    d_mod_send = jnp.swapaxes(d_mod_rows.reshape(depth, N_DEV, n_ada), 0, 1)
    recv = _exchange("exchange_grads_0", [d_mod_send] + big_partials[0], [_pack(small_local)])
    d_mod_cols, small_all = recv[0], recv[-1]
    big_recv[0] = recv[1:-1]
    d_mod_pad = jnp.pad(jnp.swapaxes(d_mod_cols, 0, 1), ((0, 0), (0, rows_c - N_DEV), (0, 0)))
    g_ada = [
        _mm(f"mod_dw_{l}", c_act, d_mod_pad,
            _spec((rows_c, d), lambda i, j, k: (0, 0)), _spec((None, rows_c, n_ada), lambda i, j, k, l=l: (l, 0, 0)),
            _spec((d, n_ada), lambda i, j, k: (0, 0)), (d, n_ada), F32, (1, 1, 1), TN)
        for l in range(depth)]

    out = {}

    def update(name, partials):
        shape = p[name].shape
        two_d = lambda a: a.reshape(-1, shape[-1])
        res = _adamw("adamw_" + name, two_d(p[name]), two_d(p["m_" + name]), two_d(p["v_" + name]),
                     partials.reshape(partials.shape[0], -1, shape[-1]))
        out[name] = [r.reshape(shape) for r in res]

    update("w_ada", jnp.stack(g_ada)[None])
    big_names = ("w_in", "w_sb_up", "w_ssm_up", "w_ffn_in", "w_glu", "w_out", "w_ffn_out")
    for j, n in enumerate(big_names):
        update(n, jnp.stack([big_recv[l][j] for l in range(depth)], axis=1))
    small_w = [p[n] for n in SMALL_PARAMS]
    res = _adamw("adamw_small", _pack(small_w), _pack([p["m_" + n] for n in SMALL_PARAMS]),
                 _pack([p["v_" + n] for n in SMALL_PARAMS]), small_all)
    for kind, packed in enumerate(res):
        for n, a in zip(SMALL_PARAMS, _unpack(packed, small_w)):
            out.setdefault(n, [None] * 4)[kind] = a

    return ((loss, grad_x[None]) + tuple(out[n][0] for n in WEIGHTS) + tuple(out[n][1] for n in WEIGHTS)
            + tuple(out[n][2] for n in WEIGHTS) + tuple(out[n][3] for n in WEIGHTS))
```

```python
import jax
import jax.numpy as jnp
from jax import lax
from jax.experimental import pallas as pl
from jax.experimental.pallas import tpu as pltpu

F32 = jnp.float32
BF16 = jnp.bfloat16
GRAD_WIRE = BF16

N_DEV = 8
LANES = 128
SUBLANES = 8
VMEM_BYTES = 64 * 1024 * 1024
HEAD_DIM = 64
SB_BLOCK = 256
SLAB_GROUPS = 8
LN_EPS = 1e-5
ADAM_LR, ADAM_B1, ADAM_B2, ADAM_EPS, ADAM_WD, ADAM_STEP = 0.001, 0.9, 0.999, 1e-08, 0.01, 10
SB_UNDERFLOW = -120.0

PACK_ROWS = 256
MESH_AXES = ("x", "y", "c")


def _vmem_limit(block_bytes):
    return int(min(max(3 * block_bytes + (8 << 20), 24 << 20), VMEM_BYTES - (8 << 20)))


def _nbytes(shape, dtype):
    n = 1
    for d in shape:
        if d is not None:
            n *= d
    return n * jnp.dtype(dtype).itemsize


def _spec(shape, fn):
    return pl.BlockSpec(shape, fn)


def _exchange(name, scatter, gather):
    arrs = list(scatter) + list(gather)
    n_sc, n = len(scatter), len(arrs)
    out_shapes = [a.shape if i < n_sc else (N_DEV,) + a.shape for i, a in enumerate(arrs)]

    def body(*refs):
        ins, outs = refs[:n], refs[n:2 * n]
        send_sems, recv_sems, own_sems = refs[2 * n:]
        x, y, c = lax.axis_index("x"), lax.axis_index("y"), lax.axis_index("c")
        me = 4 * x + 2 * y + c
        copies = []
        for a in range(n):
            src = ins[a].at[me] if a < n_sc else ins[a]
            cp = pltpu.make_async_copy(src, outs[a].at[me], own_sems.at[a])
            cp.start()
            copies.append(cp)
        for k in range(1, N_DEV):
            px = 1 - x if k & 4 else x
            py = 1 - y if k & 2 else y
            pc = 1 - c if k & 1 else c
            peer = 4 * px + 2 * py + pc
            for a in range(n):
                src = ins[a].at[peer] if a < n_sc else ins[a]
                cp = pltpu.make_async_remote_copy(
                    src_ref=src, dst_ref=outs[a].at[me],
                    send_sem=send_sems.at[a, k - 1], recv_sem=recv_sems.at[a, k - 1],
                    device_id=(px, py, pc), device_id_type=pl.DeviceIdType.MESH)
                cp.start()
                copies.append(cp)
        for cp in copies:
            cp.wait()

    hbm = pl.BlockSpec(memory_space=pltpu.HBM)
    return pl.pallas_call(
        body, name=name,
        in_specs=[hbm] * n, out_specs=[hbm] * n,
        out_shape=[jax.ShapeDtypeStruct(s, a.dtype) for s, a in zip(out_shapes, arrs)],
        scratch_shapes=[pltpu.SemaphoreType.DMA((n, N_DEV - 1)), pltpu.SemaphoreType.DMA((n, N_DEV - 1)),
                        pltpu.SemaphoreType.DMA((n,))],
    )(*arrs)


NN = (((1,), (0,)), ((), ()))
NT = (((1,), (1,)), ((), ()))
TN = (((0,), (0,)), ((), ()))


def _mm(name, a, b, a_spec, b_spec, o_spec, o_shape, o_dtype, grid, dims):
    nk = grid[2]
    acc_shape = tuple(d for d in o_spec.block_shape if d is not None)

    def body(a_ref, b_ref, o_ref, acc_ref):
        k = pl.program_id(2)

        @pl.when(k == 0)
        def _():
            acc_ref[...] = jnp.zeros_like(acc_ref)

        acc_ref[...] += lax.dot_general(a_ref[...].astype(BF16), b_ref[...].astype(BF16), dims,
                                        preferred_element_type=F32)

        @pl.when(k == nk - 1)
        def _():
            o_ref[...] = acc_ref[...].astype(o_ref.dtype)

    blk = (_nbytes(a_spec.block_shape, a.dtype) + _nbytes(b_spec.block_shape, b.dtype)
           + _nbytes(acc_shape, o_dtype) + _nbytes(acc_shape, F32))
    return pl.pallas_call(
        body, name=name, grid=grid, in_specs=[a_spec, b_spec], out_specs=o_spec,
        out_shape=jax.ShapeDtypeStruct(o_shape, o_dtype),
        scratch_shapes=[pltpu.VMEM(acc_shape, F32)],
        compiler_params=pltpu.CompilerParams(dimension_semantics=("parallel", "parallel", "arbitrary"),
                                             vmem_limit_bytes=_vmem_limit(blk)),
    )(a, b)


def _tile(n, pref=512):
    t = pref
    while t >= LANES:
        if n % t == 0:
            return t
        t -= LANES
    return n


def _rowwise(name, fn, ins, outs, grid):
    n_in = len(ins)

    def body(*refs):
        vals = fn(*[r[...].astype(F32) for r in refs[:n_in]])
        if not isinstance(vals, (tuple, list)):
            vals = (vals,)
        for r, v in zip(refs[n_in:], vals):
            r[...] = v.astype(r.dtype)

    blk = sum(_nbytes(bs, a.dtype) for a, bs, _ in ins) + sum(_nbytes(bs, d) + _nbytes(bs, F32) for _, d, bs, _ in outs)
    return pl.pallas_call(
        body, name=name, grid=grid,
        in_specs=[_spec(bs, im) for _, bs, im in ins],
        out_specs=[_spec(bs, im) for _, _, bs, im in outs],
        out_shape=[jax.ShapeDtypeStruct(s, d) for s, d, _, _ in outs],
        compiler_params=pltpu.CompilerParams(dimension_semantics=("parallel",) * len(grid),
                                             vmem_limit_bytes=_vmem_limit(2 * blk)),
    )(*[a for a, _, _ in ins])


def _rowwise_vjp(name, fn, ins, cts, wrt, grid):
    n_in, n_ct = len(ins), len(cts)
    idx = [w[0] for w in wrt]

    def body(*refs):
        prim = [r[...].astype(F32) for r in refs[:n_in]]
        ct = tuple(r[...].astype(F32) for r in refs[n_in:n_in + n_ct])
        o_refs = refs[n_in + n_ct:]

        def g(*sel):
            full = list(prim)
            for i, s in zip(idx, sel):
                full[i] = s
            out = fn(*full)
            return tuple(out) if isinstance(out, (tuple, list)) else (out,)

        _, pull = jax.vjp(g, *[prim[i] for i in idx])
        grads = pull(ct)
        first = pl.program_id(0) == 0
        for d in range(1, len(grid)):
            first = jnp.logical_and(first, pl.program_id(d) == 0)
        for w, o_ref, gr in zip(wrt, o_refs, grads):
            if w[1] == "row":
                o_ref[...] = gr.astype(o_ref.dtype)
            else:
                @pl.when(first)
                def _(o_ref=o_ref):
                    o_ref[...] = jnp.zeros_like(o_ref)

                o_ref[...] += gr.astype(o_ref.dtype)

    blk = (sum(_nbytes(bs, a.dtype) + _nbytes(bs, F32) for a, bs, _ in list(ins) + list(cts))
           + sum(_nbytes(w[4], w[3]) + _nbytes(w[4], F32) for w in wrt))
    return pl.pallas_call(
        body, name=name, grid=grid,
        in_specs=[_spec(bs, im) for _, bs, im in list(ins) + list(cts)],
        out_specs=[_spec(w[4], w[5]) for w in wrt],
        out_shape=[jax.ShapeDtypeStruct(w[2], w[3]) for w in wrt],
        compiler_params=pltpu.CompilerParams(dimension_semantics=("arbitrary",) * len(grid),
                                             vmem_limit_bytes=_vmem_limit(2 * blk)),
    )(*[a for a, _, _ in list(ins) + list(cts)])


def _normalize(x):
    mu = jnp.mean(x, axis=-1, keepdims=True)
    xc = x - mu
    var = jnp.mean(xc * xc, axis=-1, keepdims=True)
    return xc * lax.rsqrt(var + LN_EPS)


def _modulate(x, sc, sh):
    return _normalize(x) * (1.0 + sc) + sh


def _make_resid_fns(alpha):
    def resid_ln(x, y, gate, g, b):
        return _normalize(alpha * x + (1.0 + gate) * y) * g + b

    def resid_ln_mod(x, y, gate, g, b, sc, sh):
        xn = resid_ln(x, y, gate, g, b)
        return xn, _modulate(xn, sc, sh)

    return resid_ln, resid_ln_mod


def _merge_fn(y_sb, y_ssm, g_sb, g_ssm):
    return jax.nn.sigmoid(g_sb) * y_sb + jax.nn.sigmoid(g_ssm) * y_ssm


def _swiglu_fn(gate, up):
    return gate * jax.nn.sigmoid(gate) * up


def _s5_act_fn(yc, u, d_skip):
    return jax.nn.gelu(yc + d_skip * u)


def _s5_glu_fn(yc, u, t, d_skip, b_glu):
    return _s5_act_fn(yc, u, d_skip) * jax.nn.sigmoid(t + b_glu)


def _s5_post_fn(yc, u, t, d_skip, b_glu):
    y1 = _s5_act_fn(yc, u, d_skip)
    return y1, y1 * jax.nn.sigmoid(t + b_glu)


def _sb_tri(kind):
    row = lax.broadcasted_iota(jnp.int32, (SB_BLOCK, SB_BLOCK), 0)
    col = lax.broadcasted_iota(jnp.int32, (SB_BLOCK, SB_BLOCK), 1)
    if kind == "after":
        return (row > col).astype(BF16)
    if kind == "from":
        return (row >= col).astype(BF16)
    return col < row


def _split_dot(x, m):
    hi = x.astype(BF16)
    lo = (x - hi.astype(F32)).astype(BF16)
    return (lax.dot_general(hi, m, NN, preferred_element_type=F32)
            + lax.dot_general(lo, m, NN, preferred_element_type=F32))


def _sb_scores(qh, k2, scale):
    z = lax.dot_general(qh, k2, NT, preferred_element_type=F32) * scale
    sp = jnp.log(1.0 + jnp.exp(-jnp.abs(z)))
    log_beta = jnp.minimum(z, 0.0) - sp
    log_1m = -jnp.maximum(z, 0.0) - sp
    return log_beta, log_1m


def _sb_attention_fwd(proj, sb_width):
    seq = proj.shape[0]
    n_pair, n_q = sb_width // LANES, seq // SB_BLOCK
    scale = 1.0 / (HEAD_DIM ** 0.5)

    def body(q_ref, k_ref, v_ref, o_ref, o32_ref):
        qi = pl.program_id(1)
        q2 = q_ref[...]
        lane = lax.broadcasted_iota(jnp.int32, (SB_BLOCK, LANES), 1)
        m_after, causal = _sb_tri("after"), _sb_tri("mask")
        heads = [lane < HEAD_DIM, lane >= HEAD_DIM]
        qh = [jnp.where(m, q2, 0.0).astype(BF16) for m in heads]

        def scores(kb, diag):
            ks = pl.multiple_of(kb * SB_BLOCK, SB_BLOCK)
            k2 = k_ref[pl.ds(ks, SB_BLOCK), :].astype(BF16)
            out = []
            for h in range(2):
                log_beta, log_1m = _sb_scores(qh[h], k2, scale)
                if diag:
                    log_1m = jnp.where(causal, log_1m, 0.0)
                out += [log_beta + _split_dot(log_1m, m_after), jnp.sum(log_1m, axis=1, keepdims=True)]
            return tuple(out)

        def weigh(kb, sc, carry, acc, diag):
            ks = pl.multiple_of(kb * SB_BLOCK, SB_BLOCK)
            v2 = v_ref[pl.ds(ks, SB_BLOCK), :].astype(BF16)
            out = []
            for h in range(2):
                w = jnp.exp(sc[2 * h] + carry[h])
                if diag:
                    w = jnp.where(causal, w, 0.0)
                w_hi = w.astype(BF16)
                w_lo = (w - w_hi.astype(F32)).astype(BF16)
                out += [acc[2 * h] + lax.dot_general(w_hi, v2, NN, preferred_element_type=F32),
                        acc[2 * h + 1] + lax.dot_general(w_lo, v2, NN, preferred_element_type=F32)]
            return tuple(out)

        zero = jnp.zeros((SB_BLOCK, LANES), F32)
        zcol = jnp.zeros((SB_BLOCK, 1), F32)
        sc = scores(qi, True)
        acc = weigh(qi, sc, (zcol, zcol), (zero,) * 4, True)
        carry = (sc[1], sc[3])
        sc = scores(jnp.maximum(qi - 1, 0), False)

        def loop(st):
            kb, sc, carry, acc = st
            after = (carry[0] + sc[1], carry[1] + sc[3])
            done = jnp.maximum(jnp.max(after[0]), jnp.max(after[1])) < SB_UNDERFLOW
            sc_next = scores(jnp.maximum(kb - 1, 0), False)
            acc = weigh(kb, sc, carry, acc, False)
            return jnp.where(done, -1, kb - 1), sc_next, after, acc

        _, _, _, acc = lax.while_loop(lambda st: st[0] >= 0, loop, (qi - 1, sc, carry, acc))
        o_ref[...] = jnp.where(heads[0], acc[0], acc[2]).astype(o_ref.dtype)
        o32_ref[...] = jnp.where(heads[0], acc[0] + acc[1], acc[2] + acc[3])

    q_spec = _spec((SB_BLOCK, LANES), lambda h, i: (i, h))
    kv = [_spec((seq, LANES), lambda h, i, o=o: (0, o + h)) for o in (n_pair, 2 * n_pair)]
    o_spec = _spec((SB_BLOCK, LANES), lambda h, i: (i, h))
    return pl.pallas_call(
        body, name="sb_attention_fwd", grid=(n_pair, n_q),
        in_specs=[q_spec] + kv, out_specs=[o_spec, o_spec],
        out_shape=[jax.ShapeDtypeStruct((seq, sb_width), BF16), jax.ShapeDtypeStruct((seq, sb_width), F32)],
        compiler_params=pltpu.CompilerParams(dimension_semantics=("parallel", "arbitrary"),
                                             vmem_limit_bytes=_vmem_limit(2 * seq * LANES * 4)),
    )(proj, proj, proj)


def _sb_attention_bwd(proj, o32, do, sb_width):
    seq = proj.shape[0]
    n_pair, n_q = sb_width // LANES, seq // SB_BLOCK
    scale = 1.0 / (HEAD_DIM ** 0.5)

    def body(q_ref, k_ref, v_ref, o_ref, do_ref, dq_ref, dk_ref, dv_ref):
        qi = pl.program_id(1)

        @pl.when(qi == 0)
        def _():
            dk_ref[...] = jnp.zeros_like(dk_ref)
            dv_ref[...] = jnp.zeros_like(dv_ref)

        q2 = q_ref[...]
        do2 = do_ref[...].astype(F32)
        o2 = o_ref[...]
        lane = lax.broadcasted_iota(jnp.int32, (SB_BLOCK, LANES), 1)
        m_after, m_from, causal = _sb_tri("after"), _sb_tri("from"), _sb_tri("mask")
        heads = [lane < HEAD_DIM, lane >= HEAD_DIM]
        qh = [jnp.where(m, q2, 0.0).astype(BF16) for m in heads]
        doh = [jnp.where(m, do2, 0.0) for m in heads]
        doh_b = [v.astype(BF16) for v in doh]
        total = [jnp.sum(v * o2, axis=1, keepdims=True) for v in doh]

        def scores(kb, diag):
            ks = pl.multiple_of(kb * SB_BLOCK, SB_BLOCK)
            k2 = k_ref[pl.ds(ks, SB_BLOCK), :].astype(BF16)
            v2 = v_ref[pl.ds(ks, SB_BLOCK), :].astype(BF16)
            out = []
            for h in range(2):
                log_beta, log_1m = _sb_scores(qh[h], k2, scale)
                if diag:
                    log_1m = jnp.where(causal, log_1m, 0.0)
                out += [log_beta + _split_dot(log_1m, m_after), jnp.sum(log_1m, axis=1, keepdims=True),
                        lax.dot_general(doh_b[h], v2, NT, preferred_element_type=F32), log_beta]
            return tuple(out)

        def pull(kb, sc, carry, right, dq, diag):
            ks = pl.multiple_of(kb * SB_BLOCK, SB_BLOCK)
            k2 = k_ref[pl.ds(ks, SB_BLOCK), :].astype(BF16)
            dv_blk, dk_blk, right_out, dq_out = None, None, [], []
            for h in range(2):
                arg, _, d_w, log_beta = sc[4 * h:4 * h + 4]
                w = jnp.exp(arg + carry[h])
                if diag:
                    w = jnp.where(causal, w, 0.0)
                d_arg = d_w * w
                dv_h = lax.dot_general(w.astype(BF16), doh_b[h], TN, preferred_element_type=F32)
                d_log_1m = total[h] - right[h] - _split_dot(d_arg, m_from)
                beta = jnp.exp(log_beta)
                dz = d_arg * (1.0 - beta) - beta * d_log_1m
                if diag:
                    dz = jnp.where(causal, dz, 0.0)
                dz_b = (dz * scale).astype(BF16)
                dk_h = lax.dot_general(dz_b, qh[h], TN, preferred_element_type=F32)
                dv_blk = dv_h if h == 0 else dv_blk + dv_h
                dk_blk = dk_h if h == 0 else dk_blk + dk_h
                dq_out.append(dq[h] + lax.dot_general(dz_b, k2, NN, preferred_element_type=F32))
                right_out.append(right[h] + jnp.sum(d_arg, axis=1, keepdims=True))
            dv_ref[pl.ds(ks, SB_BLOCK), :] += dv_blk
            dk_ref[pl.ds(ks, SB_BLOCK), :] += dk_blk
            return tuple(right_out), tuple(dq_out)

        zero = jnp.zeros((SB_BLOCK, LANES), F32)
        zcol = jnp.zeros((SB_BLOCK, 1), F32)
        sc = scores(qi, True)
        right, dq = pull(qi, sc, (zcol, zcol), (zcol, zcol), (zero, zero), True)
        carry = (sc[1], sc[5])
        sc = scores(jnp.maximum(qi - 1, 0), False)

        def loop(st):
            kb, sc, carry, right, dq = st
            after = (carry[0] + sc[1], carry[1] + sc[5])
            done = jnp.maximum(jnp.max(after[0]), jnp.max(after[1])) < SB_UNDERFLOW
            sc_next = scores(jnp.maximum(kb - 1, 0), False)
            right, dq = pull(kb, sc, carry, right, dq, False)
            return jnp.where(done, -1, kb - 1), sc_next, after, right, dq

        _, _, _, _, dq = lax.while_loop(lambda st: st[0] >= 0, loop, (qi - 1, sc, carry, right, dq))
        dq_ref[...] = jnp.where(heads[0], dq[0], dq[1]).astype(dq_ref.dtype)

    q_spec = _spec((SB_BLOCK, LANES), lambda h, i: (i, h))
    kv = [_spec((seq, LANES), lambda h, i, o=o: (0, o + h)) for o in (n_pair, 2 * n_pair)]
    full = _spec((seq, LANES), lambda h, i: (0, h))
    return pl.pallas_call(
        body, name="sb_attention_bwd", grid=(n_pair, n_q),
        in_specs=[q_spec] + kv + [q_spec, q_spec], out_specs=[q_spec, full, full],
        out_shape=[jax.ShapeDtypeStruct((seq, sb_width), BF16), jax.ShapeDtypeStruct((seq, sb_width), F32),
                   jax.ShapeDtypeStruct((seq, sb_width), F32)],
        compiler_params=pltpu.CompilerParams(dimension_semantics=("parallel", "arbitrary"),
                                             vmem_limit_bytes=_vmem_limit(4 * seq * LANES * 4)),
    )(proj, proj, proj, o32, do)


def _s5_discretize(a_re, a_im, log_dt, b_re, b_im, c_re, c_im):
    n_g, n_p = a_re.shape
    c_g = b_re.shape[-1]
    ns = n_g // SLAB_GROUPS
    dt = jnp.exp(log_dt)[:, None]
    xr, xi = a_re * dt, a_im * dt
    mag = jnp.exp(xr)
    lr, li = mag * jnp.cos(xi), mag * jnp.sin(xi)
    den = a_re * a_re + a_im * a_im
    fr = ((lr - 1.0) * a_re + li * a_im) / den
    fi = (li * a_re - (lr - 1.0) * a_im) / den
    bb_re = fr[..., None] * b_re - fi[..., None] * b_im
    bb_im = fr[..., None] * b_im + fi[..., None] * b_re
    eye = jnp.eye(SLAB_GROUPS, dtype=F32)

    def diag_b(m):
        m = jnp.transpose(m.reshape(ns, SLAB_GROUPS, n_p, c_g), (0, 1, 3, 2))
        m = m[:, :, :, None, :] * eye[None, :, None, :, None]
        return m.reshape(ns, SLAB_GROUPS * c_g, SLAB_GROUPS * n_p)

    def diag_c(m):
        m = jnp.transpose(m.reshape(ns, SLAB_GROUPS, c_g, n_p), (0, 1, 3, 2))
        m = m[:, :, :, None, :] * eye[None, :, None, :, None]
        return m.reshape(ns, SLAB_GROUPS * n_p, SLAB_GROUPS * c_g)

    bs = jnp.concatenate([diag_b(bb_re), diag_b(bb_im)], axis=-1)
    cs = jnp.concatenate([diag_c(c_re), -diag_c(c_im)], axis=1)
    lam = jnp.concatenate([lr.reshape(ns, 1, -1), li.reshape(ns, 1, -1)], axis=-1)
    return bs, cs, lam


def _s5_scan_fwd(proj, u_col, bs, cs, lam, t_blk):
    seq = proj.shape[0]
    ns, _, w2 = bs.shape
    w = w2 // 2
    n_t = seq // t_blk

    def body(u_ref, bs_ref, cs_ref, lam_ref, yc_ref, h_ref, st_ref):
        @pl.when(pl.program_id(1) == 0)
        def _():
            st_ref[...] = jnp.zeros_like(st_ref)

        h_ref[...] = lax.dot_general(u_ref[...].astype(BF16), bs_ref[...], NN, preferred_element_type=F32)
        lr, li = lam_ref[:, :w], lam_ref[:, w:]

        def step(t, c):
            hr, hi = c
            nr = lr * hr - li * hi + h_ref[pl.ds(t, 1), :w]
            ni = li * hr + lr * hi + h_ref[pl.ds(t, 1), w:]
            h_ref[pl.ds(t, 1), :w] = nr
            h_ref[pl.ds(t, 1), w:] = ni
            return nr, ni

        hr, hi = lax.fori_loop(0, t_blk, step, (st_ref[:, :w], st_ref[:, w:]), unroll=SUBLANES)
        st_ref[:, :w] = hr
        st_ref[:, w:] = hi
        yc_ref[...] = lax.dot_general(h_ref[...].astype(BF16), cs_ref[...], NN, preferred_element_type=F32)

    return pl.pallas_call(
        body, name="s5_scan_fwd", grid=(ns, n_t),
        in_specs=[_spec((t_blk, LANES), lambda s, i: (i, u_col + s)),
                  _spec((None, LANES, w2), lambda s, i: (s, 0, 0)),
                  _spec((None, w2, LANES), lambda s, i: (s, 0, 0)),
                  _spec((None, 1, w2), lambda s, i: (s, 0, 0))],
        out_specs=[_spec((t_blk, LANES), lambda s, i: (i, s)),
                   _spec((None, t_blk, w2), lambda s, i: (s, i, 0))],
        out_shape=[jax.ShapeDtypeStruct((seq, ns * LANES), F32), jax.ShapeDtypeStruct((ns, seq, w2), F32)],
        scratch_shapes=[pltpu.VMEM((1, w2), F32)],
        compiler_params=pltpu.CompilerParams(dimension_semantics=("parallel", "arbitrary"),
                                             vmem_limit_bytes=_vmem_limit(2 * t_blk * w2 * 4)),
    )(proj, bs, cs, lam)


def _s5_scan_bwd(proj, u_col, states, d_yc, du_extra, bs, cs, lam, t_blk):
    seq = proj.shape[0]
    ns, _, w2 = bs.shape
    w = w2 // 2
    n_t = seq // t_blk
    rows8 = t_blk // SUBLANES

    def body(u_ref, h_ref, hp_ref, dyc_ref, dux_ref, bs_ref, cs_ref, lam_ref,
             du_ref, dbs_ref, dcs_ref, dlam_ref, g_ref, st_ref):
        i = pl.program_id(1)

        @pl.when(i == 0)
        def _():
            st_ref[...] = jnp.zeros_like(st_ref)
            dbs_ref[...] = jnp.zeros_like(dbs_ref)
            dcs_ref[...] = jnp.zeros_like(dcs_ref)
            dlam_ref[...] = jnp.zeros_like(dlam_ref)

        dyc_b = dyc_ref[...].astype(BF16)
        g_ref[...] = lax.dot_general(dyc_b, cs_ref[...], NT, preferred_element_type=F32)
        lr, li = lam_ref[:, :w], lam_ref[:, w:]

        def update(t, c, pr, pi):
            gr_n, gi_n, acc_r, acc_i = c
            gr = g_ref[pl.ds(t, 1), :w] + lr * gr_n + li * gi_n
            gi = g_ref[pl.ds(t, 1), w:] + lr * gi_n - li * gr_n
            g_ref[pl.ds(t, 1), :w] = gr
            g_ref[pl.ds(t, 1), w:] = gi
            return gr, gi, acc_r + gr * pr + gi * pi, acc_i + gi * pr - gr * pi

        def step(j, c):
            t = t_blk - 1 - j
            return update(t, c, h_ref[pl.ds(t - 1, 1), :w], h_ref[pl.ds(t - 1, 1), w:])

        zero = jnp.zeros((1, w), F32)
        c = lax.fori_loop(0, t_blk - 1, step, (st_ref[:, :w], st_ref[:, w:], zero, zero), unroll=SUBLANES)
        has_prev = (i < n_t - 1).astype(F32)
        c = update(0, c, hp_ref[SUBLANES - 1:, :w] * has_prev, hp_ref[SUBLANES - 1:, w:] * has_prev)
        st_ref[:, :w] = c[0]
        st_ref[:, w:] = c[1]
        dlam_ref[:, :w] += c[2]
        dlam_ref[:, w:] += c[3]
        g_b = g_ref[...].astype(BF16)
        du = lax.dot_general(g_b, bs_ref[...], NT, preferred_element_type=F32) + dux_ref[...]
        du_ref[...] = du.astype(du_ref.dtype)
        dbs_ref[...] += lax.dot_general(u_ref[...].astype(BF16), g_b, TN, preferred_element_type=F32)
        dcs_ref[...] += lax.dot_general(h_ref[...].astype(BF16), dyc_b, TN, preferred_element_type=F32)

    rev = lambda i: n_t - 1 - i
    return pl.pallas_call(
        body, name="s5_scan_bwd", grid=(ns, n_t),
        in_specs=[_spec((t_blk, LANES), lambda s, i: (rev(i), u_col + s)),
                  _spec((None, t_blk, w2), lambda s, i: (s, rev(i), 0)),
                  _spec((None, SUBLANES, w2), lambda s, i: (s, jnp.maximum(rev(i) * rows8 - 1, 0), 0)),
                  _spec((t_blk, LANES), lambda s, i: (rev(i), s)),
                  _spec((t_blk, LANES), lambda s, i: (rev(i), s)),
                  _spec((None, LANES, w2), lambda s, i: (s, 0, 0)),
                  _spec((None, w2, LANES), lambda s, i: (s, 0, 0)),
                  _spec((None, 1, w2), lambda s, i: (s, 0, 0))],
        out_specs=[_spec((t_blk, LANES), lambda s, i: (rev(i), s)),
                   _spec((None, LANES, w2), lambda s, i: (s, 0, 0)),
                   _spec((None, w2, LANES), lambda s, i: (s, 0, 0)),
                   _spec((None, 1, w2), lambda s, i: (s, 0, 0))],
        out_shape=[jax.ShapeDtypeStruct((seq, ns * LANES), BF16), jax.ShapeDtypeStruct(bs.shape, F32),
                   jax.ShapeDtypeStruct(cs.shape, F32), jax.ShapeDtypeStruct(lam.shape, F32)],
        scratch_shapes=[pltpu.VMEM((t_blk, w2), F32), pltpu.VMEM((1, w2), F32)],
        compiler_params=pltpu.CompilerParams(dimension_semantics=("parallel", "arbitrary"),
                                             vmem_limit_bytes=_vmem_limit(4 * t_blk * w2 * 4)),
    )(proj, states, states, d_yc, du_extra, bs, cs, lam)


def _loss_head(y, target, t_m):
    seq, d = y.shape

    def body(y_ref, t_ref, loss_ref, dy_ref):
        @pl.when(pl.program_id(0) == 0)
        def _():
            loss_ref[...] = jnp.zeros_like(loss_ref)

        diff = y_ref[...] - t_ref[...]
        dy_ref[...] = diff / d
        loss_ref[...] += 0.5 * jnp.sum(diff * diff) / d

    row = _spec((t_m, d), lambda i: (i, 0))
    return pl.pallas_call(
        body, name="loss_head", grid=(seq // t_m,), in_specs=[row, row],
        out_specs=[_spec((SUBLANES, LANES), lambda i: (0, 0)), row],
        out_shape=[jax.ShapeDtypeStruct((SUBLANES, LANES), F32), jax.ShapeDtypeStruct((seq, d), F32)],
        compiler_params=pltpu.CompilerParams(dimension_semantics=("arbitrary",),
                                             vmem_limit_bytes=_vmem_limit(6 * t_m * d * 4)),
    )(y, target)


def _adamw_fn(w, m, v, *partials):
    g = partials[0]
    for p in partials[1:]:
        g = g + p
    m2 = ADAM_B1 * m + (1.0 - ADAM_B1) * g
    v2 = ADAM_B2 * v + (1.0 - ADAM_B2) * (g * g)
    m_hat = m2 / (1.0 - ADAM_B1 ** ADAM_STEP)
    v_hat = v2 / (1.0 - ADAM_B2 ** ADAM_STEP)
    delta = -ADAM_LR * (m_hat / (jnp.sqrt(v_hat) + ADAM_EPS) + ADAM_WD * w)
    return g, delta, m2, v2


def _adamw(name, w, m, v, partials):
    rows, cols = w.shape
    t_r = rows
    for cand in (512, 256, 128, 64, 32, 16, 8):
        if rows % cand == 0 and cand * cols * 4 <= (1 << 20):
            t_r = cand
            break
    n_p = partials.shape[0]
    row = lambda i: (i, 0)
    ins = [(a, (t_r, cols), row) for a in (w, m, v)]
    ins += [(partials, (None, t_r, cols), (lambda i, j=j: (j, i, 0))) for j in range(n_p)]
    outs = [((rows, cols), F32, (t_r, cols), row)] * 4
    return _rowwise(name, _adamw_fn, ins, outs, (rows // t_r,))


SMALL_PARAMS = ("b_ada", "ssm_a_re", "ssm_a_im", "ssm_log_dt", "ssm_b_re", "ssm_b_im", "ssm_c_re", "ssm_c_im",
                "ssm_d", "b_glu", "ln1_g", "ln1_b", "ln2_g", "ln2_b")
WEIGHTS = ("w_ada", "b_ada", "w_in", "w_sb_up", "ssm_a_re", "ssm_a_im", "ssm_log_dt", "ssm_b_re", "ssm_b_im",
           "ssm_c_re", "ssm_c_im", "ssm_d", "w_glu", "b_glu", "w_ssm_up", "w_out", "ln1_g", "ln1_b", "w_ffn_in",
           "w_ffn_out", "ln2_g", "ln2_b")
ARG_NAMES = (("x", "c") + WEIGHTS + ("loss_target",) + tuple("m_" + n for n in WEIGHTS)
             + tuple("v_" + n for n in WEIGHTS))


def _pack(arrs):
    flat = jnp.concatenate([a.reshape(-1) for a in arrs])
    pad = (-flat.shape[0]) % (PACK_ROWS * LANES)
    return jnp.pad(flat, (0, pad)).reshape(-1, LANES)


def _unpack(packed, like):
    lead = packed.shape[:-2]
    flat = packed.reshape(lead + (-1,))
    out, off = [], 0
    for a in like:
        out.append(flat[..., off:off + a.size].reshape(lead + a.shape))
        off += a.size
    return out


def kernel(x, c, w_ada, b_ada, w_in, w_sb_up, ssm_a_re, ssm_a_im, ssm_log_dt, ssm_b_re, ssm_b_im, ssm_c_re,
           ssm_c_im, ssm_d, w_glu, b_glu, w_ssm_up, w_out, ln1_g, ln1_b, w_ffn_in, w_ffn_out, ln2_g, ln2_b,
           loss_target, m_w_ada, m_b_ada, m_w_in, m_w_sb_up, m_ssm_a_re, m_ssm_a_im, m_ssm_log_dt, m_ssm_b_re,
           m_ssm_b_im, m_ssm_c_re, m_ssm_c_im, m_ssm_d, m_w_glu, m_b_glu, m_w_ssm_up, m_w_out, m_ln1_g, m_ln1_b,
           m_w_ffn_in, m_w_ffn_out, m_ln2_g, m_ln2_b, v_w_ada, v_b_ada, v_w_in, v_w_sb_up, v_ssm_a_re, v_ssm_a_im,
           v_ssm_log_dt, v_ssm_b_re, v_ssm_b_im, v_ssm_c_re, v_ssm_c_im, v_ssm_d, v_w_glu, v_b_glu, v_w_ssm_up,
           v_w_out, v_ln1_g, v_ln1_b, v_w_ffn_in, v_w_ffn_out, v_ln2_g, v_ln2_b):
    given = locals()
    return _train_step({n: given[n] for n in ARG_NAMES})


def _train_step(p):
    x0 = p["x"][0]
    target = p["loss_target"][0]
    seq, d = x0.shape
    depth = p["w_ada"].shape[0]
    n_ada = p["w_ada"].shape[2]
    n_in = p["w_in"].shape[2]
    sb_w = p["w_sb_up"].shape[1]
    ssm_w = p["w_ssm_up"].shape[1]
    n_up = p["w_sb_up"].shape[2]
    n_ffn = p["w_ffn_in"].shape[2]
    ffn = N_DEV * p["w_ffn_out"].shape[1]
    in_cols = N_DEV * n_in
    alpha = (2 * depth) ** 0.25
    resid_ln, resid_ln_mod = _make_resid_fns(alpha)
    t_m = min(512, seq)
    n_m = seq // t_m
    t_d = _tile(d)
    assert n_ffn * (N_DEV // 2) == ffn and sb_w % LANES == 0 and ssm_w % LANES == 0 and d % LANES == 0
    assert n_in % LANES == 0 and n_up % LANES == 0 and seq % t_m == 0 and in_cols == 3 * sb_w + ssm_w + 2 * d
    assert (3 * sb_w) % ssm_w == 0 and (3 * sb_w + ssm_w) % d == 0

    col_sharded = ("w_in", "w_sb_up", "w_ssm_up", "w_ffn_in")
    row_sharded = ("w_glu", "w_out", "w_ffn_out")
    gathered = _exchange("gather_weights", [], [p[n].astype(BF16) for n in col_sharded + row_sharded] + [p["c"]])
    wg = dict(zip(col_sharded + row_sharded, gathered[:-1]))
    for n in row_sharded:
        wg[n] = jnp.swapaxes(wg[n], 0, 1).reshape(depth, -1, wg[n].shape[-1])
    c_all = gathered[-1].reshape(N_DEV, d)
    w_ffn_out4 = wg["w_ffn_out"].reshape(depth, N_DEV // 2, n_ffn, d)

    c_pad = jnp.pad(c_all, ((0, 2 * SUBLANES - N_DEV), (0, 0)))
    c_act = _rowwise("silu_c", lambda v: v * jax.nn.sigmoid(v), [(c_pad, c_pad.shape, lambda i: (0, 0))],
                     [(c_pad.shape, F32, c_pad.shape, lambda i: (0, 0))], (1,))[0]
    rows_c = c_pad.shape[0]
    mod_cols = [
        _mm(f"mod_{l}", c_act, p["w_ada"],
            _spec((rows_c, d), lambda i, j, k: (0, 0)), _spec((None, d, n_ada), lambda i, j, k, l=l: (l, 0, 0)),
            _spec((rows_c, n_ada), lambda i, j, k: (0, 0)), (rows_c, n_ada), F32, (1, 1, 1), NN)
        for l in range(depth)]
    mod_send = jnp.stack([m[:N_DEV] for m in mod_cols], axis=1)
    mod_recv = _exchange("exchange_mod", [mod_send], [])[0]
    mod_nobias = jnp.swapaxes(mod_recv, 0, 1).reshape(depth, N_DEV * n_ada)
    full2 = lambda a: (a, a.shape, lambda i: (0, 0))
    mod = _rowwise("mod_bias", lambda a, b: a + b, [full2(mod_nobias), full2(p["b_ada"])],
                   [(mod_nobias.shape, F32, mod_nobias.shape, lambda i: (0, 0))], (1,))[0]
    vec = lambda a: a.reshape(1, -1)
    mods = [[vec(mod[l, j * d:(j + 1) * d]) for j in range(6)] for l in range(depth)]
    ln = {n: [vec(p[n][l]) for l in range(depth)] for n in ("ln1_g", "ln1_b", "ln2_g", "ln2_b")}

    row_spec = lambda width: ((t_m, width), lambda i: (i, 0))
    col_spec = lambda width, cb: ((t_m, width), lambda i, cb=cb: (i, cb))
    vec_spec = lambda width: ((1, width), lambda i: (0, 0))
    rows_in = lambda a: (a,) + row_spec(a.shape[1])
    vec_in = lambda a: (a,) + vec_spec(a.shape[1])
    row_out = lambda width, dt: ((seq, width), dt) + row_spec(width)

    s5 = [_s5_discretize(*[p[n][l] for n in ("ssm_a_re", "ssm_a_im", "ssm_log_dt", "ssm_b_re", "ssm_b_im",
                                               "ssm_c_re", "ssm_c_im")]) for l in range(depth)]
    s5_b16 = [(bs.astype(BF16), cs.astype(BF16), lam) for bs, cs, lam in s5]
    t_scan = min(512, seq)
    u_col = 3 * sb_w // LANES
    g_sb_cb, g_ssm_cb = (3 * sb_w + ssm_w) // d, (3 * sb_w + ssm_w) // d + 1
    ssm_d = [vec(p["ssm_d"][l]) for l in range(depth)]
    b_glu = [vec(p["b_glu"][l]) for l in range(depth)]
    n_half = N_DEV // 2

    h = _rowwise("modulate_in", _modulate, [rows_in(x0), vec_in(mods[0][1]), vec_in(mods[0][0])],
                 [row_out(d, BF16)], (n_m,))[0]
    saved = []
    x_cur = x0
    for l in range(depth):
        sv = {"x_in": x_cur, "h": h}
        t_n = _tile(n_in)
        r_n = n_in // t_n
        proj = _mm(f"proj_{l}", h, wg["w_in"],
                   _spec((t_m, d), lambda i, j, k: (i, 0)),
                   _spec((None, None, d, t_n), lambda i, j, k, l=l, r=r_n: (j // r, l, 0, j % r)),
                   _spec((t_m, t_n), lambda i, j, k: (i, j)), (seq, in_cols), F32, (n_m, N_DEV * r_n, 1), NN)
        o_sb, o_sb32 = _sb_attention_fwd(proj, sb_w)
        bs16, cs16, lam = s5_b16[l]
        yc, states = _s5_scan_fwd(proj, u_col, bs16, cs16, lam, t_scan)
        u_in = (proj,) + col_spec(ssm_w, 3 * sb_w // ssm_w)
        y1 = _rowwise(f"s5_act_{l}", _s5_act_fn, [rows_in(yc), u_in, vec_in(ssm_d[l])],
                      [row_out(ssm_w, BF16)], (n_m,))[0]
        t_glu = _mm(f"s5_glu_mm_{l}", y1, wg["w_glu"],
                    _spec((t_m, ssm_w), lambda i, j, k: (i, 0)), _spec((None, ssm_w, ssm_w), lambda i, j, k, l=l: (l, 0, 0)),
                    _spec((t_m, ssm_w), lambda i, j, k: (i, 0)), (seq, ssm_w), F32, (n_m, 1, 1), NN)
        s5_out = _rowwise(f"s5_glu_{l}", _s5_glu_fn,
                          [rows_in(yc), u_in, rows_in(t_glu), vec_in(ssm_d[l]), vec_in(b_glu[l])],
                          [row_out(ssm_w, BF16)], (n_m,))[0]

        def up_proj(name, a, w, l=l):
            return _mm(name, a, w, _spec((t_m, a.shape[1]), lambda i, j, k: (i, 0)),
                       _spec((None, None, a.shape[1], n_up), lambda i, j, k: (j, l, 0, 0)),
                       _spec((t_m, n_up), lambda i, j, k: (i, j)), (seq, d), F32, (n_m, N_DEV, 1), NN)

        y_sb = up_proj(f"sb_up_{l}", o_sb, wg["w_sb_up"])
        y_ssm = up_proj(f"ssm_up_{l}", s5_out, wg["w_ssm_up"])
        gates = [(proj,) + col_spec(d, g_sb_cb), (proj,) + col_spec(d, g_ssm_cb)]
        merged = _rowwise(f"merge_{l}", _merge_fn, [rows_in(y_sb), rows_in(y_ssm)] + gates,
                          [row_out(d, BF16)], (n_m,))[0]
        y_mix = _mm(f"out_proj_{l}", merged, wg["w_out"],
                    _spec((t_m, d), lambda i, j, k: (i, 0)), _spec((None, d, t_d), lambda i, j, k, l=l: (l, 0, j)),
                    _spec((t_m, t_d), lambda i, j, k: (i, j)), (seq, d), F32, (n_m, d // t_d, 1), NN)
        vecs_a = [mods[l][2], ln["ln1_g"][l], ln["ln1_b"][l], mods[l][4], mods[l][3]]
        x_mid, h2 = _rowwise(f"resid_mix_{l}", resid_ln_mod, [rows_in(x_cur), rows_in(y_mix)] + [vec_in(v) for v in vecs_a],
                             [row_out(d, F32), row_out(d, BF16)], (n_m,))
        a_ffn = _mm(f"ffn_in_{l}", h2, wg["w_ffn_in"],
                    _spec((t_m, d), lambda i, j, k: (i, 0)), _spec((None, None, d, n_ffn), lambda i, j, k, l=l: (j, l, 0, 0)),
                    _spec((None, t_m, n_ffn), lambda i, j, k: (j, i, 0)), (N_DEV, seq, n_ffn), F32, (n_m, N_DEV, 1), NN)
        blk_in = lambda a, off: (a, (None, t_m, n_ffn), (lambda j, i, off=off: (j + off, i, 0)))
        f_act = _rowwise(f"swiglu_{l}", _swiglu_fn, [blk_in(a_ffn, 0), blk_in(a_ffn, n_half)],
                         [((n_half, seq, n_ffn), BF16, (None, t_m, n_ffn), lambda j, i: (j, i, 0))], (n_half, n_m))[0]
        y_ffn = _mm(f"ffn_out_{l}", f_act, w_ffn_out4,
                    _spec((None, t_m, n_ffn), lambda i, j, k: (k, i, 0)),
                    _spec((None, None, n_ffn, t_d), lambda i, j, k, l=l: (l, k, 0, j)),
                    _spec((t_m, t_d), lambda i, j, k: (i, j)), (seq, d), F32, (n_m, d // t_d, n_half), NN)
        last = l == depth - 1
        vecs_b = [mods[l][5], ln["ln2_g"][l], ln["ln2_b"][l]] + ([] if last else [mods[l + 1][1], mods[l + 1][0]])
        outs_b = [row_out(d, F32)] + ([] if last else [row_out(d, BF16)])
        res = _rowwise(f"resid_ffn_{l}", resid_ln if last else resid_ln_mod,
                       [rows_in(x_mid), rows_in(y_ffn)] + [vec_in(v) for v in vecs_b], outs_b, (n_m,))
        sv.update(proj=proj, o_sb=o_sb, o_sb32=o_sb32, yc=yc, states=states, y1=y1, t_glu=t_glu, s5_out=s5_out,
                  y_sb=y_sb, y_ssm=y_ssm, merged=merged, y_mix=y_mix, x_mid=x_mid, h2=h2, a_ffn=a_ffn, f_act=f_act,
                  y_ffn=y_ffn, vecs_a=vecs_a, vecs_b=vecs_b)
        saved.append(sv)
        x_cur = res[0]
        h = None if last else res[1]

    loss_part, d_x = _loss_head(x_cur, target, t_m)
    loss = lax.psum(loss_part[0, 0], MESH_AXES)

    d_h_next = None
    grads = {n: [None] * depth for n in WEIGHTS}
    d_mod = [[None] * 6 for _ in range(depth)]
    big_partials = [None] * depth
    big_recv = [None] * depth
    row_wrt = lambda i, width, dt: (i, "row", (seq, width), dt) + row_spec(width)
    sum_wrt = lambda i, width: (i, "sum", (1, width), F32) + vec_spec(width)
    for l in reversed(range(depth)):
        sv = saved[l]
        last = l == depth - 1
        ins_b = [rows_in(sv["x_mid"]), rows_in(sv["y_ffn"])] + [vec_in(v) for v in sv["vecs_b"]]
        cts_b = [rows_in(d_x)] + ([] if last else [rows_in(d_h_next)])
        wrt_b = [row_wrt(0, d, F32), row_wrt(1, d, BF16)] + [sum_wrt(2 + j, d) for j in range(len(sv["vecs_b"]))]
        res = _rowwise_vjp(f"resid_ffn_bwd_{l}", resid_ln if last else resid_ln_mod, ins_b, cts_b, wrt_b, (n_m,))
        d_x_mid, d_y_ffn = res[0], res[1]
        d_mod[l][5], grads["ln2_g"][l], grads["ln2_b"][l] = res[2], res[3], res[4]
        if not last:
            d_mod[l + 1][1], d_mod[l + 1][0] = res[5], res[6]
        d_f = _mm(f"ffn_out_dx_{l}", d_y_ffn, w_ffn_out4,
                  _spec((t_m, d), lambda i, j, k: (i, 0)), _spec((None, None, n_ffn, d), lambda i, j, k, l=l: (l, j, 0, 0)),
                  _spec((None, t_m, n_ffn), lambda i, j, k: (j, i, 0)), (n_half, seq, n_ffn), F32, (n_m, n_half, 1), NT)
        g_ffn_out = _mm(f"ffn_out_dw_{l}", sv["f_act"], d_y_ffn,
                        _spec((None, t_m, n_ffn), lambda i, j, k: (i, k, 0)), _spec((t_m, t_d), lambda i, j, k: (k, j)),
                        _spec((None, n_ffn, t_d), lambda i, j, k: (i, 0, j)), (n_half, n_ffn, d), GRAD_WIRE,
                        (n_half, d // t_d, n_m), TN)
        blk_in = lambda a, off: (a, (None, t_m, n_ffn), (lambda j, i, off=off: (j + off, i, 0)))
        d_a_parts = _rowwise_vjp(f"swiglu_bwd_{l}", _swiglu_fn, [blk_in(sv["a_ffn"], 0), blk_in(sv["a_ffn"], n_half)],
                                 [blk_in(d_f, 0)],
                                 [(0, "row", (n_half, seq, n_ffn), BF16, (None, t_m, n_ffn), lambda j, i: (j, i, 0)),
                                  (1, "row", (n_half, seq, n_ffn), BF16, (None, t_m, n_ffn), lambda j, i: (j, i, 0))],
                                 (n_half, n_m))
        d_a = jnp.concatenate(d_a_parts, axis=0)
        d_h2 = _mm(f"ffn_in_dx_{l}", d_a, wg["w_ffn_in"],
                   _spec((None, t_m, n_ffn), lambda i, j, k: (k, i, 0)),
                   _spec((None, None, t_d, n_ffn), lambda i, j, k, l=l: (k, l, j, 0)),
                   _spec((t_m, t_d), lambda i, j, k: (i, j)), (seq, d), F32, (n_m, d // t_d, N_DEV), NT)
        g_ffn_in = _mm(f"ffn_in_dw_{l}", sv["h2"], d_a,
                       _spec((t_m, t_d), lambda i, j, k: (k, j)), _spec((None, t_m, n_ffn), lambda i, j, k: (i, k, 0)),
                       _spec((None, t_d, n_ffn), lambda i, j, k: (i, j, 0)), (N_DEV, d, n_ffn), GRAD_WIRE,
                       (N_DEV, d // t_d, n_m), TN)
        ins_a = [rows_in(sv["x_in"]), rows_in(sv["y_mix"])] + [vec_in(v) for v in sv["vecs_a"]]
        wrt_a = [row_wrt(0, d, F32), row_wrt(1, d, BF16)] + [sum_wrt(2 + j, d) for j in range(5)]
        res = _rowwise_vjp(f"resid_mix_bwd_{l}", resid_ln_mod, ins_a, [rows_in(d_x_mid), rows_in(d_h2)], wrt_a, (n_m,))
        d_x_in, d_y_mix = res[0], res[1]
        d_mod[l][2], grads["ln1_g"][l], grads["ln1_b"][l], d_mod[l][4], d_mod[l][3] = res[2:7]
        d_merged = _mm(f"out_proj_dx_{l}", d_y_mix, wg["w_out"],
                       _spec((t_m, d), lambda i, j, k: (i, 0)), _spec((None, t_d, d), lambda i, j, k, l=l: (l, j, 0)),
                       _spec((t_m, t_d), lambda i, j, k: (i, j)), (seq, d), F32, (n_m, d // t_d, 1), NT)
        g_out = _mm(f"out_proj_dw_{l}", sv["merged"], d_y_mix,
                    _spec((t_m, t_d), lambda i, j, k: (k, i)), _spec((t_m, t_d), lambda i, j, k: (k, j)),
                    _spec((t_d, t_d), lambda i, j, k: (i, j)), (d, d), GRAD_WIRE, (d // t_d, d // t_d, n_m), TN)
        gates = [(sv["proj"],) + col_spec(d, g_sb_cb), (sv["proj"],) + col_spec(d, g_ssm_cb)]
        d_y_sb, d_y_ssm, d_g_sb, d_g_ssm = _rowwise_vjp(
            f"merge_bwd_{l}", _merge_fn, [rows_in(sv["y_sb"]), rows_in(sv["y_ssm"])] + gates, [rows_in(d_merged)],
            [row_wrt(j, d, BF16) for j in range(4)], (n_m,))

        def up_bwd(name, act, d_y, w, dx_dtype, l=l):
            k_w = act.shape[1]
            dx = _mm(name + "_dx", d_y, w, _spec((t_m, n_up), lambda i, j, k: (i, k)),
                     _spec((None, None, k_w, n_up), lambda i, j, k: (k, l, 0, 0)),
                     _spec((t_m, k_w), lambda i, j, k: (i, 0)), (seq, k_w), dx_dtype, (n_m, 1, N_DEV), NT)
            dw = _mm(name + "_dw", act, d_y, _spec((t_m, k_w), lambda i, j, k: (k, 0)),
                     _spec((t_m, n_up), lambda i, j, k: (k, i)),
                     _spec((None, k_w, n_up), lambda i, j, k: (i, 0, 0)), (N_DEV, k_w, n_up), GRAD_WIRE,
                     (N_DEV, 1, n_m), TN)
            return dx, dw

        d_o_sb, g_sb_up = up_bwd(f"sb_up_{l}", sv["o_sb"], d_y_sb, wg["w_sb_up"], BF16)
        d_s5_out, g_ssm_up = up_bwd(f"ssm_up_{l}", sv["s5_out"], d_y_ssm, wg["w_ssm_up"], F32)
        d_q, d_k, d_v = _sb_attention_bwd(sv["proj"], sv["o_sb32"], d_o_sb, sb_w)
        u_in = (sv["proj"],) + col_spec(ssm_w, 3 * sb_w // ssm_w)
        ins_s5 = [rows_in(sv["yc"]), u_in, rows_in(sv["t_glu"]), vec_in(ssm_d[l]), vec_in(b_glu[l])]
        d_t = _rowwise_vjp(f"s5_glu_bwd_{l}", _s5_glu_fn, ins_s5, [rows_in(d_s5_out)],
                           [row_wrt(2, ssm_w, BF16)], (n_m,))[0]
        d_y1 = _mm(f"s5_glu_mm_dx_{l}", d_t, wg["w_glu"],
                   _spec((t_m, ssm_w), lambda i, j, k: (i, 0)), _spec((None, ssm_w, ssm_w), lambda i, j, k, l=l: (l, 0, 0)),
                   _spec((t_m, ssm_w), lambda i, j, k: (i, 0)), (seq, ssm_w), F32, (n_m, 1, 1), NT)
        g_glu = _mm(f"s5_glu_mm_dw_{l}", sv["y1"], d_t,
                    _spec((t_m, ssm_w), lambda i, j, k: (k, 0)), _spec((t_m, ssm_w), lambda i, j, k: (k, 0)),
                    _spec((ssm_w, ssm_w), lambda i, j, k: (0, 0)), (ssm_w, ssm_w), GRAD_WIRE, (1, 1, n_m), TN)
        d_yc, d_u_skip, grads["ssm_d"][l], grads["b_glu"][l] = _rowwise_vjp(
            f"s5_post_bwd_{l}", _s5_post_fn, ins_s5, [rows_in(d_y1), rows_in(d_s5_out)],
            [row_wrt(0, ssm_w, F32), row_wrt(1, ssm_w, F32), sum_wrt(3, ssm_w), sum_wrt(4, ssm_w)], (n_m,))
        bs16, cs16, lam = s5_b16[l]
        d_u, d_bs, d_cs, d_lam = _s5_scan_bwd(sv["proj"], u_col, sv["states"], d_yc, d_u_skip, bs16, cs16, lam, t_scan)
        raw = [p[n][l] for n in ("ssm_a_re", "ssm_a_im", "ssm_log_dt", "ssm_b_re", "ssm_b_im", "ssm_c_re", "ssm_c_im")]
        _, pull = jax.vjp(_s5_discretize, *raw)
        (grads["ssm_a_re"][l], grads["ssm_a_im"][l], grads["ssm_log_dt"][l], grads["ssm_b_re"][l],
         grads["ssm_b_im"][l], grads["ssm_c_re"][l], grads["ssm_c_im"][l]) = pull((d_bs, d_cs, d_lam))
        d_proj = jnp.concatenate([d_q, d_k.astype(BF16), d_v.astype(BF16), d_u, d_g_sb, d_g_ssm], axis=1)
        t_n = _tile(n_in)
        d_h = _mm(f"proj_dx_{l}", d_proj, wg["w_in"],
                  _spec((t_m, n_in), lambda i, j, k: (i, k)), _spec((None, None, t_d, n_in), lambda i, j, k, l=l: (k, l, j, 0)),
                  _spec((t_m, t_d), lambda i, j, k: (i, j)), (seq, d), F32, (n_m, d // t_d, N_DEV), NT)
        g_in = _mm(f"proj_dw_{l}", sv["h"], d_proj,
                   _spec((t_m, t_d), lambda i, j, k: (k, j)), _spec((t_m, n_in), lambda i, j, k: (k, i)),
                   _spec((None, t_d, n_in), lambda i, j, k: (i, j, 0)), (N_DEV, d, n_in), GRAD_WIRE,
                   (N_DEV, d // t_d, n_m), TN)
        big_partials[l] = [g_in, g_sb_up, g_ssm_up, g_ffn_in,
                           g_glu.reshape(N_DEV, -1, ssm_w), g_out.reshape(N_DEV, -1, d), g_ffn_out.reshape(N_DEV, -1, d)]
        if l > 0:
            big_recv[l] = _exchange(f"exchange_grads_{l}", big_partials[l], [])
        d_x, d_h_next = d_x_in, d_h
    res = _rowwise_vjp("modulate_in_bwd", lambda v, sc, sh: (v, _modulate(v, sc, sh)),
                       [rows_in(x0), vec_in(mods[0][1]), vec_in(mods[0][0])], [rows_in(d_x), rows_in(d_h_next)],
                       [row_wrt(0, d, F32), sum_wrt(1, d), sum_wrt(2, d)], (n_m,))
    grad_x, d_mod[0][1], d_mod[0][0] = res

    d_mod_rows = jnp.concatenate([jnp.concatenate(d_mod[l], axis=1) for l in range(depth)], axis=0)
    grads["b_ada"] = [d_mod_rows[l] for l in range(depth)]
    small_local = [jnp.stack([g.reshape(p[n].shape[1:]) for g in grads[n]]) for n in SMALL_PARAMS]
    d_mod_send = jnp.swapaxes(d_mod_rows.reshape(depth, N_DEV, n_ada), 0, 1)
    recv = _exchange("exchange_grads_0", [d_mod_send] + big_partials[0], [_pack(small_local)])
    d_mod_cols, small_all = recv[0], recv[-1]
    big_recv[0] = recv[1:-1]
    d_mod_pad = jnp.pad(jnp.swapaxes(d_mod_cols, 0, 1), ((0, 0), (0, rows_c - N_DEV), (0, 0)))
    g_ada = [
        _mm(f"mod_dw_{l}", c_act, d_mod_pad,
            _spec((rows_c, d), lambda i, j, k: (0, 0)), _spec((None, rows_c, n_ada), lambda i, j, k, l=l: (l, 0, 0)),
            _spec((d, n_ada), lambda i, j, k: (0, 0)), (d, n_ada), F32, (1, 1, 1), TN)
        for l in range(depth)]

    out = {}

    def update(name, partials):
        shape = p[name].shape
        two_d = lambda a: a.reshape(-1, shape[-1])
        res = _adamw("adamw_" + name, two_d(p[name]), two_d(p["m_" + name]), two_d(p["v_" + name]),
                     partials.reshape(partials.shape[0], -1, shape[-1]))
        out[name] = [r.reshape(shape) for r in res]

    update("w_ada", jnp.stack(g_ada)[None])
    big_names = ("w_in", "w_sb_up", "w_ssm_up", "w_ffn_in", "w_glu", "w_out", "w_ffn_out")
    for j, n in enumerate(big_names):
        update(n, jnp.stack([big_recv[l][j] for l in range(depth)], axis=1))
    small_w = [p[n] for n in SMALL_PARAMS]
    res = _adamw("adamw_small", _pack(small_w), _pack([p["m_" + n] for n in SMALL_PARAMS]),
                 _pack([p["v_" + n] for n in SMALL_PARAMS]), small_all)
    for kind, packed in enumerate(res):
        for n, a in zip(SMALL_PARAMS, _unpack(packed, small_w)):
            out.setdefault(n, [None] * 4)[kind] = a

    return ((loss, grad_x[None]) + tuple(out[n][0] for n in WEIGHTS) + tuple(out[n][1] for n in WEIGHTS)
            + tuple(out[n][2] for n in WEIGHTS) + tuple(out[n][3] for n in WEIGHTS))
```

```python
import jax
import jax.numpy as jnp
from jax import lax
from jax.experimental import pallas as pl
from jax.experimental.pallas import tpu as pltpu

F32 = jnp.float32
BF16 = jnp.bfloat16
GRAD_WIRE = BF16

N_DEV = 8
LANES = 128
SUBLANES = 8
VMEM_BYTES = 64 * 1024 * 1024
HEAD_DIM = 64
SB_BLOCK = 256
SLAB_GROUPS = 8
LN_EPS = 1e-5
ADAM_LR, ADAM_B1, ADAM_B2, ADAM_EPS, ADAM_WD, ADAM_STEP = 0.001, 0.9, 0.999, 1e-08, 0.01, 10
SB_UNDERFLOW = -120.0

PACK_ROWS = 256
MESH_AXES = ("x", "y", "c")


def _vmem_limit(block_bytes):
    return int(min(max(3 * block_bytes + (8 << 20), 24 << 20), VMEM_BYTES - (8 << 20)))


def _nbytes(shape, dtype):
    n = 1
    for d in shape:
        if d is not None:
            n *= d
    return n * jnp.dtype(dtype).itemsize


def _spec(shape, fn):
    return pl.BlockSpec(shape, fn)


def _exchange(name, scatter, gather, layered=(), layer=0, depth=1, into=None):
    arrs = list(scatter) + list(layered) + list(gather)
    n_plain, n_sc, n = len(scatter), len(scatter) + len(layered), len(arrs)
    out_shapes = [a.shape for a in scatter] + [(N_DEV, depth) + a.shape[1:] for a in layered]
    out_shapes += [(N_DEV,) + a.shape for a in gather]
    held = list(into) if into is not None else []

    def body(*refs):
        ins, outs = refs[:n], refs[n + len(held):2 * n + len(held)]
        send_sems, recv_sems, own_sems = refs[2 * n + len(held):]
        x, y, c = lax.axis_index("x"), lax.axis_index("y"), lax.axis_index("c")
        me = 4 * x + 2 * y + c
        landing = [outs[a].at[me, layer] if n_plain <= a < n_sc else outs[a].at[me] for a in range(n)]
        copies = []
        for a in range(n):
            src = ins[a].at[me] if a < n_sc else ins[a]
            cp = pltpu.make_async_copy(src, landing[a], own_sems.at[a])
            cp.start()
            copies.append(cp)
        for k in range(1, N_DEV):
            px = 1 - x if k & 4 else x
            py = 1 - y if k & 2 else y
            pc = 1 - c if k & 1 else c
            peer = 4 * px + 2 * py + pc
            for a in range(n):
                src = ins[a].at[peer] if a < n_sc else ins[a]
                cp = pltpu.make_async_remote_copy(
                    src_ref=src, dst_ref=landing[a],
                    send_sem=send_sems.at[a, k - 1], recv_sem=recv_sems.at[a, k - 1],
                    device_id=(px, py, pc), device_id_type=pl.DeviceIdType.MESH)
                cp.start()
                copies.append(cp)
        for cp in copies:
            cp.wait()

    hbm = pl.BlockSpec(memory_space=pltpu.HBM)
    return pl.pallas_call(
        body, name=name,
        in_specs=[hbm] * (n + len(held)), out_specs=[hbm] * n,
        out_shape=[jax.ShapeDtypeStruct(s, a.dtype) for s, a in zip(out_shapes, arrs)],
        input_output_aliases={n + i: n_plain + i for i in range(len(held))},
        scratch_shapes=[pltpu.SemaphoreType.DMA((n, N_DEV - 1)), pltpu.SemaphoreType.DMA((n, N_DEV - 1)),
                        pltpu.SemaphoreType.DMA((n,))],
    )(*arrs, *held)


NN = (((1,), (0,)), ((), ()))
NT = (((1,), (1,)), ((), ()))
TN = (((0,), (0,)), ((), ()))


def _mm(name, a, b, a_spec, b_spec, o_spec, o_shape, o_dtype, grid, dims):
    nk = grid[2]
    acc_shape = tuple(d for d in o_spec.block_shape if d is not None)

    def product(a_ref, b_ref):
        return lax.dot_general(a_ref[...].astype(BF16), b_ref[...].astype(BF16), dims, preferred_element_type=F32)

    def body_once(a_ref, b_ref, o_ref):
        o_ref[...] = product(a_ref, b_ref).astype(o_ref.dtype)

    def body(a_ref, b_ref, o_ref, acc_ref):
        k = pl.program_id(2)

        @pl.when(k == 0)
        def _():
            acc_ref[...] = product(a_ref, b_ref)

        @pl.when(k > 0)
        def _():
            acc_ref[...] += product(a_ref, b_ref)

        @pl.when(k == nk - 1)
        def _():
            o_ref[...] = acc_ref[...].astype(o_ref.dtype)

    blk = (_nbytes(a_spec.block_shape, a.dtype) + _nbytes(b_spec.block_shape, b.dtype)
           + _nbytes(acc_shape, o_dtype) + _nbytes(acc_shape, F32))
    return pl.pallas_call(
        body_once if nk == 1 else body, name=name, grid=grid, in_specs=[a_spec, b_spec], out_specs=o_spec,
        out_shape=jax.ShapeDtypeStruct(o_shape, o_dtype),
        scratch_shapes=[] if nk == 1 else [pltpu.VMEM(acc_shape, F32)],
        compiler_params=pltpu.CompilerParams(dimension_semantics=("parallel", "parallel", "arbitrary"),
                                             vmem_limit_bytes=_vmem_limit(blk)),
    )(a, b)


def _tile(n, pref=1024):
    t = pref
    while t >= LANES:
        if n % t == 0:
            return t
        t -= LANES
    return n


def _rowwise(name, fn, ins, outs, grid):
    n_in = len(ins)

    def body(*refs):
        vals = fn(*[r[...].astype(F32) for r in refs[:n_in]])
        if not isinstance(vals, (tuple, list)):
            vals = (vals,)
        for r, v in zip(refs[n_in:], vals):
            r[...] = v.astype(r.dtype)

    blk = sum(_nbytes(bs, a.dtype) for a, bs, _ in ins) + sum(_nbytes(bs, d) + _nbytes(bs, F32) for _, d, bs, _ in outs)
    return pl.pallas_call(
        body, name=name, grid=grid,
        in_specs=[_spec(bs, im) for _, bs, im in ins],
        out_specs=[_spec(bs, im) for _, _, bs, im in outs],
        out_shape=[jax.ShapeDtypeStruct(s, d) for s, d, _, _ in outs],
        compiler_params=pltpu.CompilerParams(dimension_semantics=("parallel",) * len(grid),
                                             vmem_limit_bytes=_vmem_limit(2 * blk)),
    )(*[a for a, _, _ in ins])


def _rowwise_vjp(name, fn, ins, cts, wrt, grid):
    n_in, n_ct = len(ins), len(cts)
    idx = [w[0] for w in wrt]

    def body(*refs):
        prim = [r[...].astype(F32) for r in refs[:n_in]]
        ct = tuple(r[...].astype(F32) for r in refs[n_in:n_in + n_ct])
        o_refs = refs[n_in + n_ct:]

        def g(*sel):
            full = list(prim)
            for i, s in zip(idx, sel):
                full[i] = s
            out = fn(*full)
            return tuple(out) if isinstance(out, (tuple, list)) else (out,)

        _, pull = jax.vjp(g, *[prim[i] for i in idx])
        grads = pull(ct)
        first = pl.program_id(0) == 0
        for d in range(1, len(grid)):
            first = jnp.logical_and(first, pl.program_id(d) == 0)
        for w, o_ref, gr in zip(wrt, o_refs, grads):
            if w[1] == "row":
                o_ref[...] = gr.astype(o_ref.dtype)
            else:
                @pl.when(first)
                def _(o_ref=o_ref):
                    o_ref[...] = jnp.zeros_like(o_ref)

                o_ref[...] += gr.astype(o_ref.dtype)

    blk = (sum(_nbytes(bs, a.dtype) + _nbytes(bs, F32) for a, bs, _ in list(ins) + list(cts))
           + sum(_nbytes(w[4], w[3]) + _nbytes(w[4], F32) for w in wrt))
    return pl.pallas_call(
        body, name=name, grid=grid,
        in_specs=[_spec(bs, im) for _, bs, im in list(ins) + list(cts)],
        out_specs=[_spec(w[4], w[5]) for w in wrt],
        out_shape=[jax.ShapeDtypeStruct(w[2], w[3]) for w in wrt],
        compiler_params=pltpu.CompilerParams(dimension_semantics=("arbitrary",) * len(grid),
                                             vmem_limit_bytes=_vmem_limit(2 * blk)),
    )(*[a for a, _, _ in list(ins) + list(cts)])


def _normalize(x):
    mu = jnp.mean(x, axis=-1, keepdims=True)
    xc = x - mu
    var = jnp.mean(xc * xc, axis=-1, keepdims=True)
    return xc * lax.rsqrt(var + LN_EPS)


def _modulate(x, sc, sh):
    return _normalize(x) * (1.0 + sc) + sh


def _make_resid_fns(alpha):
    def resid_ln(x, y, gate, g, b):
        return _normalize(alpha * x + (1.0 + gate) * y) * g + b

    def resid_ln_mod(x, y, gate, g, b, sc, sh):
        xn = resid_ln(x, y, gate, g, b)
        return xn, _modulate(xn, sc, sh)

    return resid_ln, resid_ln_mod


def _merge_fn(y_sb, y_ssm, g_sb, g_ssm):
    return jax.nn.sigmoid(g_sb) * y_sb + jax.nn.sigmoid(g_ssm) * y_ssm


def _swiglu_fn(gate, up):
    return gate * jax.nn.sigmoid(gate) * up


def _s5_act_fn(yc, u, d_skip):
    return jax.nn.gelu(yc + d_skip * u)


def _s5_glu_fn(yc, u, t, d_skip, b_glu):
    return _s5_act_fn(yc, u, d_skip) * jax.nn.sigmoid(t + b_glu)


def _s5_post_fn(yc, u, t, d_skip, b_glu):
    y1 = _s5_act_fn(yc, u, d_skip)
    return y1, y1 * jax.nn.sigmoid(t + b_glu)


def _sb_tri(kind):
    row = lax.broadcasted_iota(jnp.int32, (SB_BLOCK, SB_BLOCK), 0)
    col = lax.broadcasted_iota(jnp.int32, (SB_BLOCK, SB_BLOCK), 1)
    if kind == "after":
        return (row > col).astype(BF16)
    if kind == "from":
        return (row >= col).astype(BF16)
    return col < row


def _split_dot(x, m):
    hi = x.astype(BF16)
    lo = (x - hi.astype(F32)).astype(BF16)
    return (lax.dot_general(hi, m, NN, preferred_element_type=F32)
            + lax.dot_general(lo, m, NN, preferred_element_type=F32))


def _sb_scores(qh, k2, scale):
    z = lax.dot_general(qh, k2, NT, preferred_element_type=F32) * scale
    sp = jnp.log(1.0 + jnp.exp(-jnp.abs(z)))
    log_beta = jnp.minimum(z, 0.0) - sp
    log_1m = -jnp.maximum(z, 0.0) - sp
    return log_beta, log_1m


def _sb_attention_fwd(proj, sb_width):
    seq = proj.shape[0]
    n_pair, n_q = sb_width // LANES, seq // SB_BLOCK
    scale = 1.0 / (HEAD_DIM ** 0.5)

    def body(q_ref, k_ref, v_ref, o_ref, o32_ref):
        qi = pl.program_id(1)
        q2 = q_ref[...]
        lane = lax.broadcasted_iota(jnp.int32, (SB_BLOCK, LANES), 1)
        m_after, causal = _sb_tri("after"), _sb_tri("mask")
        heads = [lane < HEAD_DIM, lane >= HEAD_DIM]
        qh = [jnp.where(m, q2, 0.0).astype(BF16) for m in heads]

        def scores(kb, diag):
            ks = pl.multiple_of(kb * SB_BLOCK, SB_BLOCK)
            k2 = k_ref[pl.ds(ks, SB_BLOCK), :].astype(BF16)
            out = []
            for h in range(2):
                log_beta, log_1m = _sb_scores(qh[h], k2, scale)
                if diag:
                    log_1m = jnp.where(causal, log_1m, 0.0)
                out += [log_beta + _split_dot(log_1m, m_after), jnp.sum(log_1m, axis=1, keepdims=True)]
            return tuple(out)

        def weigh(kb, sc, carry, acc, diag):
            ks = pl.multiple_of(kb * SB_BLOCK, SB_BLOCK)
            v2 = v_ref[pl.ds(ks, SB_BLOCK), :].astype(BF16)
            out = []
            for h in range(2):
                w = jnp.exp(sc[2 * h] + carry[h])
                if diag:
                    w = jnp.where(causal, w, 0.0)
                w_hi = w.astype(BF16)
                w_lo = (w - w_hi.astype(F32)).astype(BF16)
                out += [acc[2 * h] + lax.dot_general(w_hi, v2, NN, preferred_element_type=F32),
                        acc[2 * h + 1] + lax.dot_general(w_lo, v2, NN, preferred_element_type=F32)]
            return tuple(out)

        zero = jnp.zeros((SB_BLOCK, LANES), F32)
        zcol = jnp.zeros((SB_BLOCK, 1), F32)
        sc = scores(qi, True)
        acc = weigh(qi, sc, (zcol, zcol), (zero,) * 4, True)
        carry = (sc[1], sc[3])
        sc = scores(jnp.maximum(qi - 1, 0), False)

        def loop(st):
            kb, sc, carry, acc = st
            after = (carry[0] + sc[1], carry[1] + sc[3])
            done = jnp.maximum(jnp.max(after[0]), jnp.max(after[1])) < SB_UNDERFLOW
            sc_next = scores(jnp.maximum(kb - 1, 0), False)
            acc = weigh(kb, sc, carry, acc, False)
            return jnp.where(done, -1, kb - 1), sc_next, after, acc

        _, _, _, acc = lax.while_loop(lambda st: st[0] >= 0, loop, (qi - 1, sc, carry, acc))
        o_ref[...] = jnp.where(heads[0], acc[0], acc[2]).astype(o_ref.dtype)
        o32_ref[...] = jnp.where(heads[0], acc[0] + acc[1], acc[2] + acc[3])

    q_spec = _spec((SB_BLOCK, LANES), lambda h, i: (i, h))
    kv = [_spec((seq, LANES), lambda h, i, o=o: (0, o + h)) for o in (n_pair, 2 * n_pair)]
    o_spec = _spec((SB_BLOCK, LANES), lambda h, i: (i, h))
    return pl.pallas_call(
        body, name="sb_attention_fwd", grid=(n_pair, n_q),
        in_specs=[q_spec] + kv, out_specs=[o_spec, o_spec],
        out_shape=[jax.ShapeDtypeStruct((seq, sb_width), BF16), jax.ShapeDtypeStruct((seq, sb_width), F32)],
        compiler_params=pltpu.CompilerParams(dimension_semantics=("parallel", "arbitrary"),
                                             vmem_limit_bytes=_vmem_limit(2 * seq * LANES * 4)),
    )(proj, proj, proj)


def _sb_attention_bwd(proj, o32, do, sb_width):
    seq = proj.shape[0]
    n_pair, n_q = sb_width // LANES, seq // SB_BLOCK
    scale = 1.0 / (HEAD_DIM ** 0.5)

    def body(q_ref, k_ref, v_ref, o_ref, do_ref, dq_ref, dk_ref, dv_ref):
        qi = pl.program_id(1)

        @pl.when(qi == 0)
        def _():
            dk_ref[...] = jnp.zeros_like(dk_ref)
            dv_ref[...] = jnp.zeros_like(dv_ref)

        q2 = q_ref[...]
        do2 = do_ref[...].astype(F32)
        o2 = o_ref[...]
        lane = lax.broadcasted_iota(jnp.int32, (SB_BLOCK, LANES), 1)
        m_after, m_from, causal = _sb_tri("after"), _sb_tri("from"), _sb_tri("mask")
        heads = [lane < HEAD_DIM, lane >= HEAD_DIM]
        qh = [jnp.where(m, q2, 0.0).astype(BF16) for m in heads]
        doh = [jnp.where(m, do2, 0.0) for m in heads]
        doh_b = [v.astype(BF16) for v in doh]
        total = [jnp.sum(v * o2, axis=1, keepdims=True) for v in doh]

        def scores(kb, diag):
            ks = pl.multiple_of(kb * SB_BLOCK, SB_BLOCK)
            k2 = k_ref[pl.ds(ks, SB_BLOCK), :].astype(BF16)
            v2 = v_ref[pl.ds(ks, SB_BLOCK), :].astype(BF16)
            out = []
            for h in range(2):
                log_beta, log_1m = _sb_scores(qh[h], k2, scale)
                if diag:
                    log_1m = jnp.where(causal, log_1m, 0.0)
                out += [log_beta + _split_dot(log_1m, m_after), jnp.sum(log_1m, axis=1, keepdims=True),
                        lax.dot_general(doh_b[h], v2, NT, preferred_element_type=F32), log_beta]
            return tuple(out)

        def pull(kb, sc, carry, right, dq, diag):
            ks = pl.multiple_of(kb * SB_BLOCK, SB_BLOCK)
            k2 = k_ref[pl.ds(ks, SB_BLOCK), :].astype(BF16)
            dv_blk, dk_blk, right_out, dq_out = None, None, [], []
            for h in range(2):
                arg, _, d_w, log_beta = sc[4 * h:4 * h + 4]
                w = jnp.exp(arg + carry[h])
                if diag:
                    w = jnp.where(causal, w, 0.0)
                d_arg = d_w * w
                dv_h = lax.dot_general(w.astype(BF16), doh_b[h], TN, preferred_element_type=F32)
                d_log_1m = total[h] - right[h] - _split_dot(d_arg, m_from)
                beta = jnp.exp(log_beta)
                dz = d_arg * (1.0 - beta) - beta * d_log_1m
                if diag:
                    dz = jnp.where(causal, dz, 0.0)
                dz_b = (dz * scale).astype(BF16)
                dk_h = lax.dot_general(dz_b, qh[h], TN, preferred_element_type=F32)
                dv_blk = dv_h if h == 0 else dv_blk + dv_h
                dk_blk = dk_h if h == 0 else dk_blk + dk_h
                dq_out.append(dq[h] + lax.dot_general(dz_b, k2, NN, preferred_element_type=F32))
                right_out.append(right[h] + jnp.sum(d_arg, axis=1, keepdims=True))
            dv_ref[pl.ds(ks, SB_BLOCK), :] += dv_blk
            dk_ref[pl.ds(ks, SB_BLOCK), :] += dk_blk
            return tuple(right_out), tuple(dq_out)

        zero = jnp.zeros((SB_BLOCK, LANES), F32)
        zcol = jnp.zeros((SB_BLOCK, 1), F32)
        sc = scores(qi, True)
        right, dq = pull(qi, sc, (zcol, zcol), (zcol, zcol), (zero, zero), True)
        carry = (sc[1], sc[5])
        sc = scores(jnp.maximum(qi - 1, 0), False)

        def loop(st):
            kb, sc, carry, right, dq = st
            after = (carry[0] + sc[1], carry[1] + sc[5])
            done = jnp.maximum(jnp.max(after[0]), jnp.max(after[1])) < SB_UNDERFLOW
            sc_next = scores(jnp.maximum(kb - 1, 0), False)
            right, dq = pull(kb, sc, carry, right, dq, False)
            return jnp.where(done, -1, kb - 1), sc_next, after, right, dq

        _, _, _, _, dq = lax.while_loop(lambda st: st[0] >= 0, loop, (qi - 1, sc, carry, right, dq))
        dq_ref[...] = jnp.where(heads[0], dq[0], dq[1]).astype(dq_ref.dtype)

    q_spec = _spec((SB_BLOCK, LANES), lambda h, i: (i, h))
    kv = [_spec((seq, LANES), lambda h, i, o=o: (0, o + h)) for o in (n_pair, 2 * n_pair)]
    full = _spec((seq, LANES), lambda h, i: (0, h))
    return pl.pallas_call(
        body, name="sb_attention_bwd", grid=(n_pair, n_q),
        in_specs=[q_spec] + kv + [q_spec, q_spec], out_specs=[q_spec, full, full],
        out_shape=[jax.ShapeDtypeStruct((seq, sb_width), BF16), jax.ShapeDtypeStruct((seq, sb_width), F32),
                   jax.ShapeDtypeStruct((seq, sb_width), F32)],
        compiler_params=pltpu.CompilerParams(dimension_semantics=("parallel", "arbitrary"),
                                             vmem_limit_bytes=_vmem_limit(4 * seq * LANES * 4)),
    )(proj, proj, proj, o32, do)


def _s5_discretize(a_re, a_im, log_dt, b_re, b_im, c_re, c_im):
    n_g, n_p = a_re.shape
    c_g = b_re.shape[-1]
    ns = n_g // SLAB_GROUPS
    dt = jnp.exp(log_dt)[:, None]
    xr, xi = a_re * dt, a_im * dt
    mag = jnp.exp(xr)
    lr, li = mag * jnp.cos(xi), mag * jnp.sin(xi)
    den = a_re * a_re + a_im * a_im
    fr = ((lr - 1.0) * a_re + li * a_im) / den
    fi = (li * a_re - (lr - 1.0) * a_im) / den
    bb_re = fr[..., None] * b_re - fi[..., None] * b_im
    bb_im = fr[..., None] * b_im + fi[..., None] * b_re
    eye = jnp.eye(SLAB_GROUPS, dtype=F32)

    def diag_b(m):
        m = jnp.transpose(m.reshape(ns, SLAB_GROUPS, n_p, c_g), (0, 1, 3, 2))
        m = m[:, :, :, None, :] * eye[None, :, None, :, None]
        return m.reshape(ns, SLAB_GROUPS * c_g, SLAB_GROUPS * n_p)

    def diag_c(m):
        m = jnp.transpose(m.reshape(ns, SLAB_GROUPS, c_g, n_p), (0, 1, 3, 2))
        m = m[:, :, :, None, :] * eye[None, :, None, :, None]
        return m.reshape(ns, SLAB_GROUPS * n_p, SLAB_GROUPS * c_g)

    bs = jnp.concatenate([diag_b(bb_re), diag_b(bb_im)], axis=-1)
    cs = jnp.concatenate([diag_c(c_re), -diag_c(c_im)], axis=1)
    lam = jnp.concatenate([lr.reshape(ns, 1, -1), li.reshape(ns, 1, -1)], axis=-1)
    return bs, cs, lam


def _s5_powers(a_re, a_im, log_dt, n):
    n_g, n_p = a_re.shape
    ns = n_g // SLAB_GROUPS
    dt = jnp.exp(log_dt)[:, None]
    mag = jnp.exp(a_re * dt)
    base_r, base_i = mag * jnp.cos(a_im * dt), mag * jnp.sin(a_im * dt)
    steps = jnp.arange(1, n + 1, dtype=jnp.int32)[:, None, None]
    pr, pi = jnp.ones((n, n_g, n_p), F32), jnp.zeros((n, n_g, n_p), F32)
    for b in range(n.bit_length()):
        take = ((steps >> b) & 1) == 1
        pr, pi = (jnp.where(take, pr * base_r - pi * base_i, pr), jnp.where(take, pr * base_i + pi * base_r, pi))
        base_r, base_i = base_r * base_r - base_i * base_i, 2.0 * base_r * base_i

    def slabs(re, im):
        one = lambda m: jnp.transpose(m.reshape(n, ns, SLAB_GROUPS * n_p), (1, 0, 2))
        return jnp.concatenate([one(re), one(im)], axis=-1)

    return slabs(pr, pi), slabs(pr[::-1], -pi[::-1])


def _lanes(j):
    return slice(j * LANES, (j + 1) * LANES)


def _tile8(k):
    return pl.ds(pl.multiple_of(k * SUBLANES, SUBLANES), SUBLANES)


def _s5_interleave(dst_ref, src_ref, t_seg):
    def body(k, _):
        dst_ref[_tile8(k), :] = src_ref[pl.ds(k, SUBLANES, stride=t_seg), :]
        return 0

    lax.fori_loop(0, t_seg, body, 0, unroll=4)


def _s5_join_segments(st_ref, end_ref, car_ref, tab_ref, row, order, n_pair):
    for j in range(n_pair):
        re, im = _lanes(j), _lanes(n_pair + j)
        cr, ci = st_ref[:, re], st_ref[:, im]
        tr, ti = tab_ref[row:row + 1, re], tab_ref[row:row + 1, im]
        for s in order:
            car_ref[s:s + 1, re] = cr
            car_ref[s:s + 1, im] = ci
            er, ei = end_ref[s:s + 1, re], end_ref[s:s + 1, im]
            cr, ci = er + tr * cr - ti * ci, ei + tr * ci + ti * cr
        st_ref[:, re] = cr
        st_ref[:, im] = ci


def _s5_add_carries(buf_ref, car_ref, tab_ref, t_seg, n_pair):
    def fix(k, _):
        rows = _tile8(k)
        tab = tab_ref[pl.ds(k, 1), :]
        for j in range(n_pair):
            re, im = _lanes(j), _lanes(n_pair + j)
            cr, ci = car_ref[:, re], car_ref[:, im]
            tr, ti = tab[:, re], tab[:, im]
            buf_ref[rows, re] += tr * cr - ti * ci
            buf_ref[rows, im] += tr * ci + ti * cr
        return 0

    lax.fori_loop(0, t_seg, fix, 0, unroll=2)


def _s5_scan_fwd(proj, u_col, bs, cs, lam, pw, t_blk):
    seq = proj.shape[0]
    ns, _, w2 = bs.shape
    n_pair = w2 // (2 * LANES)
    t_seg, n_t = t_blk // SUBLANES, seq // t_blk

    def body(u_ref, bs_ref, cs_ref, lam_ref, pw_ref, yc_ref, h_ref, st_ref, end_ref, car_ref, ui_ref, bu_ref, yi_ref):
        @pl.when(pl.program_id(1) == 0)
        def _():
            st_ref[...] = jnp.zeros_like(st_ref)

        _s5_interleave(ui_ref, u_ref, t_seg)
        bu_ref[...] = lax.dot_general(ui_ref[...].astype(BF16), bs_ref[...], NN, preferred_element_type=F32)
        lam_r = [jnp.broadcast_to(lam_ref[:, _lanes(j)], (SUBLANES, LANES)) for j in range(n_pair)]
        lam_i = [jnp.broadcast_to(lam_ref[:, _lanes(n_pair + j)], (SUBLANES, LANES)) for j in range(n_pair)]

        def step(k, c):
            rows = _tile8(k)
            out = []
            for j in range(n_pair):
                hr, hi = c[2 * j], c[2 * j + 1]
                nr = lam_r[j] * hr - lam_i[j] * hi + bu_ref[rows, _lanes(j)]
                ni = lam_i[j] * hr + lam_r[j] * hi + bu_ref[rows, _lanes(n_pair + j)]
                h_ref[rows, _lanes(j)] = nr
                h_ref[rows, _lanes(n_pair + j)] = ni
                out += [nr, ni]
            return tuple(out)

        ends = lax.fori_loop(0, t_seg, step, (jnp.zeros((SUBLANES, LANES), F32),) * (2 * n_pair), unroll=4)
        for j in range(n_pair):
            end_ref[:, _lanes(j)] = ends[2 * j]
            end_ref[:, _lanes(n_pair + j)] = ends[2 * j + 1]
        _s5_join_segments(st_ref, end_ref, car_ref, pw_ref, t_seg - 1, list(range(SUBLANES)), n_pair)
        _s5_add_carries(h_ref, car_ref, pw_ref, t_seg, n_pair)
        yi_ref[...] = lax.dot_general(h_ref[...].astype(BF16), cs_ref[...], NN, preferred_element_type=F32)

        def scatter(k, _):
            yc_ref[pl.ds(k, SUBLANES, stride=t_seg), :] = yi_ref[_tile8(k), :]
            return 0

        lax.fori_loop(0, t_seg, scatter, 0, unroll=4)

    return pl.pallas_call(
        body, name="s5_scan_fwd", grid=(ns, n_t),
        in_specs=[_spec((t_blk, LANES), lambda s, i: (i, u_col + s)),
                  _spec((None, LANES, w2), lambda s, i: (s, 0, 0)),
                  _spec((None, w2, LANES), lambda s, i: (s, 0, 0)),
                  _spec((None, 1, w2), lambda s, i: (s, 0, 0)),
                  _spec((None, t_seg, w2), lambda s, i: (s, 0, 0))],
        out_specs=[_spec((t_blk, LANES), lambda s, i: (i, s)),
                   _spec((None, t_blk, w2), lambda s, i: (s, i, 0))],
        out_shape=[jax.ShapeDtypeStruct((seq, ns * LANES), F32), jax.ShapeDtypeStruct((ns, seq, w2), F32)],
        scratch_shapes=[pltpu.VMEM((1, w2), F32), pltpu.VMEM((SUBLANES, w2), F32), pltpu.VMEM((SUBLANES, w2), F32),
                        pltpu.VMEM((t_blk, LANES), F32), pltpu.VMEM((t_blk, w2), F32), pltpu.VMEM((t_blk, LANES), F32)],
        compiler_params=pltpu.CompilerParams(dimension_semantics=("parallel", "arbitrary"),
                                             vmem_limit_bytes=_vmem_limit(3 * t_blk * w2 * 4)),
    )(proj, bs, cs, lam, pw)


def _s5_scan_bwd(proj, u_col, states, d_yc, du_extra, bs, cs, lam, qw, t_blk):
    seq = proj.shape[0]
    ns, _, w2 = bs.shape
    n_pair = w2 // (2 * LANES)
    t_seg, n_t = t_blk // SUBLANES, seq // t_blk

    def body(u_ref, h_ref, hp_ref, dyc_ref, dux_ref, bs_ref, cs_ref, lam_ref, qw_ref,
             du_ref, dbs_ref, dcs_ref, dlam_ref, g_ref, gd_ref, st_ref, end_ref, car_ref, ui_ref, dyi_ref, dui_ref):
        i = pl.program_id(1)

        @pl.when(i == 0)
        def _():
            st_ref[...] = jnp.zeros_like(st_ref)
            dbs_ref[...] = jnp.zeros_like(dbs_ref)
            dcs_ref[...] = jnp.zeros_like(dcs_ref)
            dlam_ref[...] = jnp.zeros_like(dlam_ref)

        _s5_interleave(ui_ref, u_ref, t_seg)
        _s5_interleave(dyi_ref, dyc_ref, t_seg)
        dyc_b = dyi_ref[...].astype(BF16)
        gd_ref[...] = lax.dot_general(dyc_b, cs_ref[...], NT, preferred_element_type=F32)
        lam_r = [jnp.broadcast_to(lam_ref[:, _lanes(j)], (SUBLANES, LANES)) for j in range(n_pair)]
        lam_i = [jnp.broadcast_to(lam_ref[:, _lanes(n_pair + j)], (SUBLANES, LANES)) for j in range(n_pair)]

        def step(kk, c):
            rows = _tile8(t_seg - 1 - kk)
            out = []
            for j in range(n_pair):
                gr_n, gi_n = c[2 * j], c[2 * j + 1]
                gr = gd_ref[rows, _lanes(j)] + lam_r[j] * gr_n + lam_i[j] * gi_n
                gi = gd_ref[rows, _lanes(n_pair + j)] + lam_r[j] * gi_n - lam_i[j] * gr_n
                g_ref[rows, _lanes(j)] = gr
                g_ref[rows, _lanes(n_pair + j)] = gi
                out += [gr, gi]
            return tuple(out)

        zero = jnp.zeros((SUBLANES, LANES), F32)
        firsts = lax.fori_loop(0, t_seg, step, (zero,) * (2 * n_pair), unroll=4)
        for j in range(n_pair):
            end_ref[:, _lanes(j)] = firsts[2 * j]
            end_ref[:, _lanes(n_pair + j)] = firsts[2 * j + 1]
        _s5_join_segments(st_ref, end_ref, car_ref, qw_ref, 0, list(range(SUBLANES))[::-1], n_pair)
        _s5_add_carries(g_ref, car_ref, qw_ref, t_seg, n_pair)

        def pair_up(k, c):
            rows, prev = _tile8(k), _tile8(k - 1)
            out = []
            for j in range(n_pair):
                re, im = _lanes(j), _lanes(n_pair + j)
                gr, gi, hr, hi = g_ref[rows, re], g_ref[rows, im], h_ref[prev, re], h_ref[prev, im]
                out += [c[2 * j] + gr * hr + gi * hi, c[2 * j + 1] + gi * hr - gr * hi]
            return tuple(out)

        acc = lax.fori_loop(1, t_seg, pair_up, (zero,) * (2 * n_pair), unroll=4)
        has_prev = (i < n_t - 1).astype(F32)
        first_seg = lax.broadcasted_iota(jnp.int32, (SUBLANES, LANES), 0) == 0
        last = _tile8(t_seg - 1)
        for j in range(n_pair):
            re, im = _lanes(j), _lanes(n_pair + j)
            gr, gi = g_ref[0:SUBLANES, re], g_ref[0:SUBLANES, im]
            hr = jnp.where(first_seg, hp_ref[SUBLANES - 1:, re] * has_prev, pltpu.roll(h_ref[last, re], 1, 0))
            hi = jnp.where(first_seg, hp_ref[SUBLANES - 1:, im] * has_prev, pltpu.roll(h_ref[last, im], 1, 0))
            dlam_ref[:, re] += jnp.sum(acc[2 * j] + gr * hr + gi * hi, axis=0, keepdims=True)
            dlam_ref[:, im] += jnp.sum(acc[2 * j + 1] + gi * hr - gr * hi, axis=0, keepdims=True)

        g_b = g_ref[...].astype(BF16)
        dui_ref[...] = lax.dot_general(g_b, bs_ref[...], NT, preferred_element_type=F32)
        dbs_ref[...] += lax.dot_general(ui_ref[...].astype(BF16), g_b, TN, preferred_element_type=F32)
        dcs_ref[...] += lax.dot_general(h_ref[...].astype(BF16), dyc_b, TN, preferred_element_type=F32)

        def scatter(k, _):
            rows = pl.ds(k, SUBLANES, stride=t_seg)
            du_ref[rows, :] = (dui_ref[_tile8(k), :] + dux_ref[rows, :]).astype(du_ref.dtype)
            return 0

        lax.fori_loop(0, t_seg, scatter, 0, unroll=4)

    rev = lambda i: n_t - 1 - i
    return pl.pallas_call(
        body, name="s5_scan_bwd", grid=(ns, n_t),
        in_specs=[_spec((t_blk, LANES), lambda s, i: (rev(i), u_col + s)),
                  _spec((None, t_blk, w2), lambda s, i: (s, rev(i), 0)),
                  _spec((None, SUBLANES, w2), lambda s, i: (s, jnp.maximum(rev(i) * t_seg - 1, 0), 0)),
                  _spec((t_blk, LANES), lambda s, i: (rev(i), s)),
                  _spec((t_blk, LANES), lambda s, i: (rev(i), s)),
                  _spec((None, LANES, w2), lambda s, i: (s, 0, 0)),
                  _spec((None, w2, LANES), lambda s, i: (s, 0, 0)),
                  _spec((None, 1, w2), lambda s, i: (s, 0, 0)),
                  _spec((None, t_seg, w2), lambda s, i: (s, 0, 0))],
        out_specs=[_spec((t_blk, LANES), lambda s, i: (rev(i), s)),
                   _spec((None, LANES, w2), lambda s, i: (s, 0, 0)),
                   _spec((None, w2, LANES), lambda s, i: (s, 0, 0)),
                   _spec((None, 1, w2), lambda s, i: (s, 0, 0))],
        out_shape=[jax.ShapeDtypeStruct((seq, ns * LANES), F32), jax.ShapeDtypeStruct(bs.shape, F32),
                   jax.ShapeDtypeStruct(cs.shape, F32), jax.ShapeDtypeStruct(lam.shape, F32)],
        scratch_shapes=[pltpu.VMEM((t_blk, w2), F32), pltpu.VMEM((t_blk, w2), F32), pltpu.VMEM((1, w2), F32),
                        pltpu.VMEM((SUBLANES, w2), F32), pltpu.VMEM((SUBLANES, w2), F32),
                        pltpu.VMEM((t_blk, LANES), F32), pltpu.VMEM((t_blk, LANES), F32), pltpu.VMEM((t_blk, LANES), F32)],
        compiler_params=pltpu.CompilerParams(dimension_semantics=("parallel", "arbitrary"),
                                             vmem_limit_bytes=_vmem_limit(5 * t_blk * w2 * 4)),
    )(proj, states, states, d_yc, du_extra, bs, cs, lam, qw)


def _loss_head(y, target, t_m):
    seq, d = y.shape

    def body(y_ref, t_ref, loss_ref, dy_ref):
        @pl.when(pl.program_id(0) == 0)
        def _():
            loss_ref[...] = jnp.zeros_like(loss_ref)

        diff = y_ref[...] - t_ref[...]
        dy_ref[...] = diff / d
        loss_ref[...] += 0.5 * jnp.sum(diff * diff) / d

    row = _spec((t_m, d), lambda i: (i, 0))
    return pl.pallas_call(
        body, name="loss_head", grid=(seq // t_m,), in_specs=[row, row],
        out_specs=[_spec((SUBLANES, LANES), lambda i: (0, 0)), row],
        out_shape=[jax.ShapeDtypeStruct((SUBLANES, LANES), F32), jax.ShapeDtypeStruct((seq, d), F32)],
        compiler_params=pltpu.CompilerParams(dimension_semantics=("arbitrary",),
                                             vmem_limit_bytes=_vmem_limit(6 * t_m * d * 4)),
    )(y, target)


def _adamw_fn(w, m, v, *partials):
    g = partials[0]
    for p in partials[1:]:
        g = g + p
    m2 = ADAM_B1 * m + (1.0 - ADAM_B1) * g
    v2 = ADAM_B2 * v + (1.0 - ADAM_B2) * (g * g)
    m_hat = m2 / (1.0 - ADAM_B1 ** ADAM_STEP)
    v_hat = v2 / (1.0 - ADAM_B2 ** ADAM_STEP)
    delta = -ADAM_LR * (m_hat / (jnp.sqrt(v_hat) + ADAM_EPS) + ADAM_WD * w)
    return g, delta, m2, v2


def _adamw(name, w, m, v, partials):
    rows, cols = w.shape
    t_r = rows
    for cand in (512, 256, 128, 64, 32, 16, 8):
        if rows % cand == 0 and cand * cols * 4 <= (1 << 20):
            t_r = cand
            break
    n_p = partials.shape[0]
    row = lambda i: (i, 0)
    ins = [(a, (t_r, cols), row) for a in (w, m, v)]
    ins += [(partials, (None, t_r, cols), (lambda i, j=j: (j, i, 0))) for j in range(n_p)]
    outs = [((rows, cols), F32, (t_r, cols), row)] * 4
    return _rowwise(name, _adamw_fn, ins, outs, (rows // t_r,))


SMALL_PARAMS = ("b_ada", "ssm_a_re", "ssm_a_im", "ssm_log_dt", "ssm_b_re", "ssm_b_im", "ssm_c_re", "ssm_c_im",
                "ssm_d", "b_glu", "ln1_g", "ln1_b", "ln2_g", "ln2_b")
WEIGHTS = ("w_ada", "b_ada", "w_in", "w_sb_up", "ssm_a_re", "ssm_a_im", "ssm_log_dt", "ssm_b_re", "ssm_b_im",
           "ssm_c_re", "ssm_c_im", "ssm_d", "w_glu", "b_glu", "w_ssm_up", "w_out", "ln1_g", "ln1_b", "w_ffn_in",
           "w_ffn_out", "ln2_g", "ln2_b")
ARG_NAMES = (("x", "c") + WEIGHTS + ("loss_target",) + tuple("m_" + n for n in WEIGHTS)
             + tuple("v_" + n for n in WEIGHTS))


def _pack(arrs):
    flat = jnp.concatenate([a.reshape(-1) for a in arrs])
    pad = (-flat.shape[0]) % (PACK_ROWS * LANES)
    return jnp.pad(flat, (0, pad)).reshape(-1, LANES)


def _unpack(packed, like):
    lead = packed.shape[:-2]
    flat = packed.reshape(lead + (-1,))
    out, off = [], 0
    for a in like:
        out.append(flat[..., off:off + a.size].reshape(lead + a.shape))
        off += a.size
    return out


def kernel(x, c, w_ada, b_ada, w_in, w_sb_up, ssm_a_re, ssm_a_im, ssm_log_dt, ssm_b_re, ssm_b_im, ssm_c_re,
           ssm_c_im, ssm_d, w_glu, b_glu, w_ssm_up, w_out, ln1_g, ln1_b, w_ffn_in, w_ffn_out, ln2_g, ln2_b,
           loss_target, m_w_ada, m_b_ada, m_w_in, m_w_sb_up, m_ssm_a_re, m_ssm_a_im, m_ssm_log_dt, m_ssm_b_re,
           m_ssm_b_im, m_ssm_c_re, m_ssm_c_im, m_ssm_d, m_w_glu, m_b_glu, m_w_ssm_up, m_w_out, m_ln1_g, m_ln1_b,
           m_w_ffn_in, m_w_ffn_out, m_ln2_g, m_ln2_b, v_w_ada, v_b_ada, v_w_in, v_w_sb_up, v_ssm_a_re, v_ssm_a_im,
           v_ssm_log_dt, v_ssm_b_re, v_ssm_b_im, v_ssm_c_re, v_ssm_c_im, v_ssm_d, v_w_glu, v_b_glu, v_w_ssm_up,
           v_w_out, v_ln1_g, v_ln1_b, v_w_ffn_in, v_w_ffn_out, v_ln2_g, v_ln2_b):
    given = locals()
    return _train_step({n: given[n] for n in ARG_NAMES})


def _train_step(p):
    x0 = p["x"][0]
    target = p["loss_target"][0]
    seq, d = x0.shape
    depth = p["w_ada"].shape[0]
    n_ada = p["w_ada"].shape[2]
    n_in = p["w_in"].shape[2]
    sb_w = p["w_sb_up"].shape[1]
    ssm_w = p["w_ssm_up"].shape[1]
    n_up = p["w_sb_up"].shape[2]
    n_ffn = p["w_ffn_in"].shape[2]
    ffn = N_DEV * p["w_ffn_out"].shape[1]
    in_cols = N_DEV * n_in
    alpha = (2 * depth) ** 0.25
    resid_ln, resid_ln_mod = _make_resid_fns(alpha)
    t_r = min(512, seq)
    n_r = seq // t_r
    t_m = min(1024, seq)
    n_m = seq // t_m
    t_d = _tile(d)
    assert n_ffn * (N_DEV // 2) == ffn and sb_w % LANES == 0 and ssm_w % LANES == 0 and d % LANES == 0
    assert n_in % LANES == 0 and n_up % LANES == 0 and seq % t_m == 0 and in_cols == 3 * sb_w + ssm_w + 2 * d
    assert (3 * sb_w) % ssm_w == 0 and (3 * sb_w + ssm_w) % d == 0

    col_sharded = ("w_in", "w_sb_up", "w_ssm_up", "w_ffn_in")
    row_sharded = ("w_glu", "w_out", "w_ffn_out")
    gathered = _exchange("gather_weights", [], [p[n].astype(BF16) for n in col_sharded + row_sharded] + [p["c"]])
    wg = dict(zip(col_sharded + row_sharded, gathered[:-1]))
    for n in row_sharded:
        wg[n] = jnp.swapaxes(wg[n], 0, 1).reshape(depth, -1, wg[n].shape[-1])
    c_all = gathered[-1].reshape(N_DEV, d)
    w_ffn_out4 = wg["w_ffn_out"].reshape(depth, N_DEV // 2, n_ffn, d)

    c_pad = jnp.pad(c_all, ((0, 2 * SUBLANES - N_DEV), (0, 0)))
    c_act = _rowwise("silu_c", lambda v: v * jax.nn.sigmoid(v), [(c_pad, c_pad.shape, lambda i: (0, 0))],
                     [(c_pad.shape, F32, c_pad.shape, lambda i: (0, 0))], (1,))[0]
    rows_c = c_pad.shape[0]
    mod_cols = [
        _mm(f"mod_{l}", c_act, p["w_ada"],
            _spec((rows_c, d), lambda i, j, k: (0, 0)), _spec((None, d, n_ada), lambda i, j, k, l=l: (l, 0, 0)),
            _spec((rows_c, n_ada), lambda i, j, k: (0, 0)), (rows_c, n_ada), F32, (1, 1, 1), NN)
        for l in range(depth)]
    mod_send = jnp.stack([m[:N_DEV] for m in mod_cols], axis=1)
    mod_recv = _exchange("exchange_mod", [mod_send], [])[0]
    mod_nobias = jnp.swapaxes(mod_recv, 0, 1).reshape(depth, N_DEV * n_ada)
    full2 = lambda a: (a, a.shape, lambda i: (0, 0))
    mod = _rowwise("mod_bias", lambda a, b: a + b, [full2(mod_nobias), full2(p["b_ada"])],
                   [(mod_nobias.shape, F32, mod_nobias.shape, lambda i: (0, 0))], (1,))[0]
    vec = lambda a: a.reshape(1, -1)
    mods = [[vec(mod[l, j * d:(j + 1) * d]) for j in range(6)] for l in range(depth)]
    ln = {n: [vec(p[n][l]) for l in range(depth)] for n in ("ln1_g", "ln1_b", "ln2_g", "ln2_b")}

    row_spec = lambda width: ((t_r, width), lambda i: (i, 0))
    col_spec = lambda width, cb: ((t_r, width), lambda i, cb=cb: (i, cb))
    vec_spec = lambda width: ((1, width), lambda i: (0, 0))
    rows_in = lambda a: (a,) + row_spec(a.shape[1])
    vec_in = lambda a: (a,) + vec_spec(a.shape[1])
    row_out = lambda width, dt: ((seq, width), dt) + row_spec(width)

    s5 = [_s5_discretize(*[p[n][l] for n in ("ssm_a_re", "ssm_a_im", "ssm_log_dt", "ssm_b_re", "ssm_b_im",
                                               "ssm_c_re", "ssm_c_im")]) for l in range(depth)]
    s5_b16 = [(bs.astype(BF16), cs.astype(BF16), lam) for bs, cs, lam in s5]
    t_scan = min(512, seq)
    s5_pw = [_s5_powers(p["ssm_a_re"][l], p["ssm_a_im"][l], p["ssm_log_dt"][l], t_scan // SUBLANES)
             for l in range(depth)]
    u_col = 3 * sb_w // LANES
    g_sb_cb, g_ssm_cb = (3 * sb_w + ssm_w) // d, (3 * sb_w + ssm_w) // d + 1
    ssm_d = [vec(p["ssm_d"][l]) for l in range(depth)]
    b_glu = [vec(p["b_glu"][l]) for l in range(depth)]
    n_half = N_DEV // 2

    h = _rowwise("modulate_in", _modulate, [rows_in(x0), vec_in(mods[0][1]), vec_in(mods[0][0])],
                 [row_out(d, BF16)], (n_r,))[0]
    saved = []
    x_cur = x0
    for l in range(depth):
        sv = {"x_in": x_cur, "h": h}
        t_n = _tile(n_in)
        r_n = n_in // t_n
        proj = _mm(f"proj_{l}", h, wg["w_in"],
                   _spec((t_m, d), lambda i, j, k: (i, 0)),
                   _spec((None, None, d, t_n), lambda i, j, k, l=l, r=r_n: (j // r, l, 0, j % r)),
                   _spec((t_m, t_n), lambda i, j, k: (i, j)), (seq, in_cols), F32, (n_m, N_DEV * r_n, 1), NN)
        o_sb, o_sb32 = _sb_attention_fwd(proj, sb_w)
        bs16, cs16, lam = s5_b16[l]
        yc, states = _s5_scan_fwd(proj, u_col, bs16, cs16, lam, s5_pw[l][0], t_scan)
        u_in = (proj,) + col_spec(ssm_w, 3 * sb_w // ssm_w)
        y1 = _rowwise(f"s5_act_{l}", _s5_act_fn, [rows_in(yc), u_in, vec_in(ssm_d[l])],
                      [row_out(ssm_w, BF16)], (n_r,))[0]
        t_glu = _mm(f"s5_glu_mm_{l}", y1, wg["w_glu"],
                    _spec((t_m, ssm_w), lambda i, j, k: (i, 0)), _spec((None, ssm_w, ssm_w), lambda i, j, k, l=l: (l, 0, 0)),
                    _spec((t_m, ssm_w), lambda i, j, k: (i, 0)), (seq, ssm_w), F32, (n_m, 1, 1), NN)
        s5_out = _rowwise(f"s5_glu_{l}", _s5_glu_fn,
                          [rows_in(yc), u_in, rows_in(t_glu), vec_in(ssm_d[l]), vec_in(b_glu[l])],
                          [row_out(ssm_w, BF16)], (n_r,))[0]

        def up_proj(name, a, w, l=l):
            return _mm(name, a, w, _spec((t_m, a.shape[1]), lambda i, j, k: (i, 0)),
                       _spec((None, None, a.shape[1], n_up), lambda i, j, k: (j, l, 0, 0)),
                       _spec((t_m, n_up), lambda i, j, k: (i, j)), (seq, d), F32, (n_m, N_DEV, 1), NN)

        y_sb = up_proj(f"sb_up_{l}", o_sb, wg["w_sb_up"])
        y_ssm = up_proj(f"ssm_up_{l}", s5_out, wg["w_ssm_up"])
        gates = [(proj,) + col_spec(d, g_sb_cb), (proj,) + col_spec(d, g_ssm_cb)]
        merged = _rowwise(f"merge_{l}", _merge_fn, [rows_in(y_sb), rows_in(y_ssm)] + gates,
                          [row_out(d, BF16)], (n_r,))[0]
        y_mix = _mm(f"out_proj_{l}", merged, wg["w_out"],
                    _spec((t_m, d), lambda i, j, k: (i, 0)), _spec((None, d, t_d), lambda i, j, k, l=l: (l, 0, j)),
                    _spec((t_m, t_d), lambda i, j, k: (i, j)), (seq, d), F32, (n_m, d // t_d, 1), NN)
        vecs_a = [mods[l][2], ln["ln1_g"][l], ln["ln1_b"][l], mods[l][4], mods[l][3]]
        x_mid, h2 = _rowwise(f"resid_mix_{l}", resid_ln_mod, [rows_in(x_cur), rows_in(y_mix)] + [vec_in(v) for v in vecs_a],
                             [row_out(d, F32), row_out(d, BF16)], (n_r,))
        a_ffn = _mm(f"ffn_in_{l}", h2, wg["w_ffn_in"],
                    _spec((t_m, d), lambda i, j, k: (i, 0)), _spec((None, None, d, n_ffn), lambda i, j, k, l=l: (j, l, 0, 0)),
                    _spec((None, t_m, n_ffn), lambda i, j, k: (j, i, 0)), (N_DEV, seq, n_ffn), F32, (n_m, N_DEV, 1), NN)
        blk_in = lambda a, off: (a, (None, t_r, n_ffn), (lambda j, i, off=off: (j + off, i, 0)))
        f_act = _rowwise(f"swiglu_{l}", _swiglu_fn, [blk_in(a_ffn, 0), blk_in(a_ffn, n_half)],
                         [((n_half, seq, n_ffn), BF16, (None, t_r, n_ffn), lambda j, i: (j, i, 0))], (n_half, n_r))[0]
        y_ffn = _mm(f"ffn_out_{l}", f_act, w_ffn_out4,
                    _spec((None, t_m, n_ffn), lambda i, j, k: (k, i, 0)),
                    _spec((None, None, n_ffn, t_d), lambda i, j, k, l=l: (l, k, 0, j)),
                    _spec((t_m, t_d), lambda i, j, k: (i, j)), (seq, d), F32, (n_m, d // t_d, n_half), NN)
        last = l == depth - 1
        vecs_b = [mods[l][5], ln["ln2_g"][l], ln["ln2_b"][l]] + ([] if last else [mods[l + 1][1], mods[l + 1][0]])
        outs_b = [row_out(d, F32)] + ([] if last else [row_out(d, BF16)])
        res = _rowwise(f"resid_ffn_{l}", resid_ln if last else resid_ln_mod,
                       [rows_in(x_mid), rows_in(y_ffn)] + [vec_in(v) for v in vecs_b], outs_b, (n_r,))
        sv.update(proj=proj, o_sb=o_sb, o_sb32=o_sb32, yc=yc, states=states, y1=y1, t_glu=t_glu, s5_out=s5_out,
                  y_sb=y_sb, y_ssm=y_ssm, merged=merged, y_mix=y_mix, x_mid=x_mid, h2=h2, a_ffn=a_ffn, f_act=f_act,
                  y_ffn=y_ffn, vecs_a=vecs_a, vecs_b=vecs_b)
        saved.append(sv)
        x_cur = res[0]
        h = None if last else res[1]

    loss_part, d_x = _loss_head(x_cur, target, t_r)
    loss = lax.psum(loss_part[0, 0], MESH_AXES)

    d_h_next = None
    grads = {n: [None] * depth for n in WEIGHTS}
    d_mod = [[None] * 6 for _ in range(depth)]
    big_partials = [None] * depth
    big_recv = None
    row_wrt = lambda i, width, dt: (i, "row", (seq, width), dt) + row_spec(width)
    sum_wrt = lambda i, width: (i, "sum", (1, width), F32) + vec_spec(width)
    for l in reversed(range(depth)):
        sv = saved[l]
        last = l == depth - 1
        ins_b = [rows_in(sv["x_mid"]), rows_in(sv["y_ffn"])] + [vec_in(v) for v in sv["vecs_b"]]
        cts_b = [rows_in(d_x)] + ([] if last else [rows_in(d_h_next)])
        wrt_b = [row_wrt(0, d, F32), row_wrt(1, d, BF16)] + [sum_wrt(2 + j, d) for j in range(len(sv["vecs_b"]))]
        res = _rowwise_vjp(f"resid_ffn_bwd_{l}", resid_ln if last else resid_ln_mod, ins_b, cts_b, wrt_b, (n_r,))
        d_x_mid, d_y_ffn = res[0], res[1]
        d_mod[l][5], grads["ln2_g"][l], grads["ln2_b"][l] = res[2], res[3], res[4]
        if not last:
            d_mod[l + 1][1], d_mod[l + 1][0] = res[5], res[6]
        d_f = _mm(f"ffn_out_dx_{l}", d_y_ffn, w_ffn_out4,
                  _spec((t_m, d), lambda i, j, k: (i, 0)), _spec((None, None, n_ffn, d), lambda i, j, k, l=l: (l, j, 0, 0)),
                  _spec((None, t_m, n_ffn), lambda i, j, k: (j, i, 0)), (n_half, seq, n_ffn), F32, (n_m, n_half, 1), NT)
        g_ffn_out = _mm(f"ffn_out_dw_{l}", sv["f_act"], d_y_ffn,
                        _spec((None, t_m, n_ffn), lambda i, j, k: (i, k, 0)), _spec((t_m, t_d), lambda i, j, k: (k, j)),
                        _spec((None, n_ffn, t_d), lambda i, j, k: (i, 0, j)), (n_half, n_ffn, d), GRAD_WIRE,
                        (n_half, d // t_d, n_m), TN)
        blk_in = lambda a, off: (a, (None, t_r, n_ffn), (lambda j, i, off=off: (j + off, i, 0)))
        d_a_parts = _rowwise_vjp(f"swiglu_bwd_{l}", _swiglu_fn, [blk_in(sv["a_ffn"], 0), blk_in(sv["a_ffn"], n_half)],
                                 [blk_in(d_f, 0)],
                                 [(0, "row", (n_half, seq, n_ffn), BF16, (None, t_r, n_ffn), lambda j, i: (j, i, 0)),
                                  (1, "row", (n_half, seq, n_ffn), BF16, (None, t_r, n_ffn), lambda j, i: (j, i, 0))],
                                 (n_half, n_r))
        d_a = jnp.concatenate(d_a_parts, axis=0)
        d_h2 = _mm(f"ffn_in_dx_{l}", d_a, wg["w_ffn_in"],
                   _spec((None, t_m, n_ffn), lambda i, j, k: (k, i, 0)),
                   _spec((None, None, t_d, n_ffn), lambda i, j, k, l=l: (k, l, j, 0)),
                   _spec((t_m, t_d), lambda i, j, k: (i, j)), (seq, d), F32, (n_m, d // t_d, N_DEV), NT)
        g_ffn_in = _mm(f"ffn_in_dw_{l}", sv["h2"], d_a,
                       _spec((t_m, t_d), lambda i, j, k: (k, j)), _spec((None, t_m, n_ffn), lambda i, j, k: (i, k, 0)),
                       _spec((None, t_d, n_ffn), lambda i, j, k: (i, j, 0)), (N_DEV, d, n_ffn), GRAD_WIRE,
                       (N_DEV, d // t_d, n_m), TN)
        ins_a = [rows_in(sv["x_in"]), rows_in(sv["y_mix"])] + [vec_in(v) for v in sv["vecs_a"]]
        wrt_a = [row_wrt(0, d, F32), row_wrt(1, d, BF16)] + [sum_wrt(2 + j, d) for j in range(5)]
        res = _rowwise_vjp(f"resid_mix_bwd_{l}", resid_ln_mod, ins_a, [rows_in(d_x_mid), rows_in(d_h2)], wrt_a, (n_r,))
        d_x_in, d_y_mix = res[0], res[1]
        d_mod[l][2], grads["ln1_g"][l], grads["ln1_b"][l], d_mod[l][4], d_mod[l][3] = res[2:7]
        d_merged = _mm(f"out_proj_dx_{l}", d_y_mix, wg["w_out"],
                       _spec((t_m, d), lambda i, j, k: (i, 0)), _spec((None, t_d, d), lambda i, j, k, l=l: (l, j, 0)),
                       _spec((t_m, t_d), lambda i, j, k: (i, j)), (seq, d), F32, (n_m, d // t_d, 1), NT)
        g_out = _mm(f"out_proj_dw_{l}", sv["merged"], d_y_mix,
                    _spec((t_m, t_d), lambda i, j, k: (k, i)), _spec((t_m, t_d), lambda i, j, k: (k, j)),
                    _spec((t_d, t_d), lambda i, j, k: (i, j)), (d, d), GRAD_WIRE, (d // t_d, d // t_d, n_m), TN)
        gates = [(sv["proj"],) + col_spec(d, g_sb_cb), (sv["proj"],) + col_spec(d, g_ssm_cb)]
        d_y_sb, d_y_ssm, d_g_sb, d_g_ssm = _rowwise_vjp(
            f"merge_bwd_{l}", _merge_fn, [rows_in(sv["y_sb"]), rows_in(sv["y_ssm"])] + gates, [rows_in(d_merged)],
            [row_wrt(j, d, BF16) for j in range(4)], (n_r,))

        def up_bwd(name, act, d_y, w, dx_dtype, l=l):
            k_w = act.shape[1]
            dx = _mm(name + "_dx", d_y, w, _spec((t_m, n_up), lambda i, j, k: (i, k)),
                     _spec((None, None, k_w, n_up), lambda i, j, k: (k, l, 0, 0)),
                     _spec((t_m, k_w), lambda i, j, k: (i, 0)), (seq, k_w), dx_dtype, (n_m, 1, N_DEV), NT)
            dw = _mm(name + "_dw", act, d_y, _spec((t_m, k_w), lambda i, j, k: (k, 0)),
                     _spec((t_m, n_up), lambda i, j, k: (k, i)),
                     _spec((None, k_w, n_up), lambda i, j, k: (i, 0, 0)), (N_DEV, k_w, n_up), GRAD_WIRE,
                     (N_DEV, 1, n_m), TN)
            return dx, dw

        d_o_sb, g_sb_up = up_bwd(f"sb_up_{l}", sv["o_sb"], d_y_sb, wg["w_sb_up"], BF16)
        d_s5_out, g_ssm_up = up_bwd(f"ssm_up_{l}", sv["s5_out"], d_y_ssm, wg["w_ssm_up"], F32)
        d_q, d_k, d_v = _sb_attention_bwd(sv["proj"], sv["o_sb32"], d_o_sb, sb_w)
        u_in = (sv["proj"],) + col_spec(ssm_w, 3 * sb_w // ssm_w)
        ins_s5 = [rows_in(sv["yc"]), u_in, rows_in(sv["t_glu"]), vec_in(ssm_d[l]), vec_in(b_glu[l])]
        d_t = _rowwise_vjp(f"s5_glu_bwd_{l}", _s5_glu_fn, ins_s5, [rows_in(d_s5_out)],
                           [row_wrt(2, ssm_w, BF16)], (n_r,))[0]
        d_y1 = _mm(f"s5_glu_mm_dx_{l}", d_t, wg["w_glu"],
                   _spec((t_m, ssm_w), lambda i, j, k: (i, 0)), _spec((None, ssm_w, ssm_w), lambda i, j, k, l=l: (l, 0, 0)),
                   _spec((t_m, ssm_w), lambda i, j, k: (i, 0)), (seq, ssm_w), F32, (n_m, 1, 1), NT)
        g_glu = _mm(f"s5_glu_mm_dw_{l}", sv["y1"], d_t,
                    _spec((t_m, ssm_w), lambda i, j, k: (k, 0)), _spec((t_m, ssm_w), lambda i, j, k: (k, 0)),
                    _spec((ssm_w, ssm_w), lambda i, j, k: (0, 0)), (ssm_w, ssm_w), GRAD_WIRE, (1, 1, n_m), TN)
        d_yc, d_u_skip, grads["ssm_d"][l], grads["b_glu"][l] = _rowwise_vjp(
            f"s5_post_bwd_{l}", _s5_post_fn, ins_s5, [rows_in(d_y1), rows_in(d_s5_out)],
            [row_wrt(0, ssm_w, F32), row_wrt(1, ssm_w, F32), sum_wrt(3, ssm_w), sum_wrt(4, ssm_w)], (n_r,))
        bs16, cs16, lam = s5_b16[l]
        d_u, d_bs, d_cs, d_lam = _s5_scan_bwd(sv["proj"], u_col, sv["states"], d_yc, d_u_skip, bs16, cs16, lam,
                                              s5_pw[l][1], t_scan)
        raw = [p[n][l] for n in ("ssm_a_re", "ssm_a_im", "ssm_log_dt", "ssm_b_re", "ssm_b_im", "ssm_c_re", "ssm_c_im")]
        _, pull = jax.vjp(_s5_discretize, *raw)
        (grads["ssm_a_re"][l], grads["ssm_a_im"][l], grads["ssm_log_dt"][l], grads["ssm_b_re"][l],
         grads["ssm_b_im"][l], grads["ssm_c_re"][l], grads["ssm_c_im"][l]) = pull((d_bs, d_cs, d_lam))
        d_proj = jnp.concatenate([d_q, d_k.astype(BF16), d_v.astype(BF16), d_u.astype(BF16), d_g_sb, d_g_ssm], axis=1)
        t_n = _tile(n_in)
        d_h = _mm(f"proj_dx_{l}", d_proj, wg["w_in"],
                  _spec((t_m, n_in), lambda i, j, k: (i, k)), _spec((None, None, t_d, n_in), lambda i, j, k, l=l: (k, l, j, 0)),
                  _spec((t_m, t_d), lambda i, j, k: (i, j)), (seq, d), F32, (n_m, d // t_d, N_DEV), NT)
        g_in = _mm(f"proj_dw_{l}", sv["h"], d_proj,
                   _spec((t_m, t_d), lambda i, j, k: (k, j)), _spec((t_m, n_in), lambda i, j, k: (k, i)),
                   _spec((None, t_d, n_in), lambda i, j, k: (i, j, 0)), (N_DEV, d, n_in), GRAD_WIRE,
                   (N_DEV, d // t_d, n_m), TN)
        big_partials[l] = [g_in, g_sb_up, g_ssm_up, g_ffn_in,
                           g_glu.reshape(N_DEV, -1, ssm_w), g_out.reshape(N_DEV, -1, d), g_ffn_out.reshape(N_DEV, -1, d)]
        if l > 0:
            big_recv = _exchange(f"exchange_grads_{l}", [], [], layered=big_partials[l], layer=l, depth=depth,
                                 into=big_recv)
        d_x, d_h_next = d_x_in, d_h
    res = _rowwise_vjp("modulate_in_bwd", lambda v, sc, sh: (v, _modulate(v, sc, sh)),
                       [rows_in(x0), vec_in(mods[0][1]), vec_in(mods[0][0])], [rows_in(d_x), rows_in(d_h_next)],
                       [row_wrt(0, d, F32), sum_wrt(1, d), sum_wrt(2, d)], (n_r,))
    grad_x, d_mod[0][1], d_mod[0][0] = res

    d_mod_rows = jnp.concatenate([jnp.concatenate(d_mod[l], axis=1) for l in range(depth)], axis=0)
    grads["b_ada"] = [d_mod_rows[l] for l in range(depth)]
    small_local = [jnp.stack([g.reshape(p[n].shape[1:]) for g in grads[n]]) for n in SMALL_PARAMS]
    d_mod_send = jnp.swapaxes(d_mod_rows.reshape(depth, N_DEV, n_ada), 0, 1)
    recv = _exchange("exchange_grads_0", [d_mod_send], [_pack(small_local)], layered=big_partials[0], layer=0,
                     depth=depth, into=big_recv)
    d_mod_cols, small_all = recv[0], recv[-1]
    big_recv = recv[1:-1]
    d_mod_pad = jnp.pad(jnp.swapaxes(d_mod_cols, 0, 1), ((0, 0), (0, rows_c - N_DEV), (0, 0)))
    g_ada = [
        _mm(f"mod_dw_{l}", c_act, d_mod_pad,
            _spec((rows_c, d), lambda i, j, k: (0, 0)), _spec((None, rows_c, n_ada), lambda i, j, k, l=l: (l, 0, 0)),
            _spec((d, n_ada), lambda i, j, k: (0, 0)), (d, n_ada), F32, (1, 1, 1), TN)
        for l in range(depth)]

    out = {}

    def update(name, partials):
        shape = p[name].shape
        two_d = lambda a: a.reshape(-1, shape[-1])
        res = _adamw("adamw_" + name, two_d(p[name]), two_d(p["m_" + name]), two_d(p["v_" + name]),
                     partials.reshape(partials.shape[0], -1, shape[-1]))
        out[name] = [r.reshape(shape) for r in res]

    update("w_ada", jnp.stack(g_ada)[None])
    big_names = ("w_in", "w_sb_up", "w_ssm_up", "w_ffn_in", "w_glu", "w_out", "w_ffn_out")
    for j, n in enumerate(big_names):
        update(n, big_recv[j])
    small_w = [p[n] for n in SMALL_PARAMS]
    res = _adamw("adamw_small", _pack(small_w), _pack([p["m_" + n] for n in SMALL_PARAMS]),
                 _pack([p["v_" + n] for n in SMALL_PARAMS]), small_all)
    for kind, packed in enumerate(res):
        for n, a in zip(SMALL_PARAMS, _unpack(packed, small_w)):
            out.setdefault(n, [None] * 4)[kind] = a

    return ((loss, grad_x[None]) + tuple(out[n][0] for n in WEIGHTS) + tuple(out[n][1] for n in WEIGHTS)
            + tuple(out[n][2] for n in WEIGHTS) + tuple(out[n][3] for n in WEIGHTS))
```

```python
import jax
import jax.numpy as jnp
from jax import lax
from jax.experimental import pallas as pl
from jax.experimental.pallas import tpu as pltpu

F32 = jnp.float32
BF16 = jnp.bfloat16
GRAD_WIRE = BF16

N_DEV = 8
LANES = 128
SUBLANES = 8
VMEM_BYTES = 64 * 1024 * 1024
HEAD_DIM = 64
SB_BLOCK = 256
SLAB_GROUPS = 8
LN_EPS = 1e-5
ADAM_LR, ADAM_B1, ADAM_B2, ADAM_EPS, ADAM_WD, ADAM_STEP = 0.001, 0.9, 0.999, 1e-08, 0.01, 10
SB_UNDERFLOW = -120.0

PACK_ROWS = 256
MESH_AXES = ("x", "y", "c")


def _vmem_limit(block_bytes):
    return int(min(max(3 * block_bytes + (8 << 20), 24 << 20), VMEM_BYTES - (8 << 20)))


def _nbytes(shape, dtype):
    n = 1
    for d in shape:
        if d is not None:
            n *= d
    return n * jnp.dtype(dtype).itemsize


def _spec(shape, fn):
    return pl.BlockSpec(shape, fn)


class _Exchange:
    def __init__(self, scatter=(), gather=(), layered=()):
        self.arrs = list(scatter) + [a for a, _, _, _ in layered] + list(gather)
        self.n = len(self.arrs)
        self.n_sc = len(scatter) + len(layered)
        self.layer = [None] * len(scatter) + [l for _, l, _, _ in layered] + [None] * len(gather)
        self.shapes = ([a.shape for a in scatter] + [(N_DEV, dp) + a.shape[1:] for a, _, dp, _ in layered]
                       + [(N_DEV,) + a.shape for a in gather])
        self.held = [(len(scatter) + i, b) for i, (_, _, _, b) in enumerate(layered) if b is not None]
        self.operands = self.arrs + [b for _, b in self.held]
        hbm = pl.BlockSpec(memory_space=pltpu.HBM)
        self.in_specs = [hbm] * len(self.operands)
        self.out_specs = [hbm] * self.n
        self.out_shape = [jax.ShapeDtypeStruct(s, a.dtype) for s, a in zip(self.shapes, self.arrs)]
        self.scratch = [pltpu.SemaphoreType.DMA((self.n, N_DEV - 1)), pltpu.SemaphoreType.DMA((self.n, N_DEV - 1)),
                        pltpu.SemaphoreType.DMA((self.n,))]

    def aliases(self, first_in, first_out):
        return {first_in + self.n + i: first_out + a for i, (a, _) in enumerate(self.held)}

    def copies(self, ins, outs, sems):
        send_sems, recv_sems, own_sems = sems
        x, y, c = lax.axis_index("x"), lax.axis_index("y"), lax.axis_index("c")
        me = 4 * x + 2 * y + c
        landing = [outs[a].at[me] if self.layer[a] is None else outs[a].at[me, self.layer[a]] for a in range(self.n)]
        out = [pltpu.make_async_copy(ins[a].at[me] if a < self.n_sc else ins[a], landing[a], own_sems.at[a])
               for a in range(self.n)]
        for k in range(1, N_DEV):
            px = 1 - x if k & 4 else x
            py = 1 - y if k & 2 else y
            pc = 1 - c if k & 1 else c
            peer = 4 * px + 2 * py + pc
            for a in range(self.n):
                out.append(pltpu.make_async_remote_copy(
                    src_ref=ins[a].at[peer] if a < self.n_sc else ins[a], dst_ref=landing[a],
                    send_sem=send_sems.at[a, k - 1], recv_sem=recv_sems.at[a, k - 1],
                    device_id=(px, py, pc), device_id_type=pl.DeviceIdType.MESH))
        return out


def _exchange(name, scatter, gather, layered=()):
    ex = _Exchange(scatter, gather, layered)

    def body(*refs):
        copies = ex.copies(refs[:ex.n], refs[len(ex.operands):len(ex.operands) + ex.n], refs[-3:])
        for cp in copies:
            cp.start()
        for cp in copies:
            cp.wait()

    return pl.pallas_call(body, name=name, in_specs=ex.in_specs, out_specs=ex.out_specs, out_shape=ex.out_shape,
                          input_output_aliases=ex.aliases(0, 0), scratch_shapes=ex.scratch)(*ex.operands)


def _call_beside(ex, body, name, grid, in_specs, out_specs, out_shape, scratch_shapes, vmem_bytes, operands,
                 semantics):
    if ex is None:
        res = pl.pallas_call(
            body, name=name, grid=grid, in_specs=in_specs, out_specs=out_specs, out_shape=out_shape,
            scratch_shapes=scratch_shapes,
            compiler_params=pltpu.CompilerParams(dimension_semantics=semantics, vmem_limit_bytes=vmem_bytes),
        )(*operands)
        return res, None
    n_in, n_out, n_scr = len(in_specs), len(out_specs), len(scratch_shapes)
    n_xin = len(ex.operands)

    def fused(*refs):
        mine = refs[:n_in] + refs[n_in + n_xin:n_in + n_xin + n_out]
        mine += refs[n_in + n_xin + n_out + ex.n:n_in + n_xin + n_out + ex.n + n_scr]
        first = pl.program_id(0) == 0
        last = pl.program_id(0) == grid[0] - 1
        for dim in range(1, len(grid)):
            first = jnp.logical_and(first, pl.program_id(dim) == 0)
            last = jnp.logical_and(last, pl.program_id(dim) == grid[dim] - 1)
        x_ins = refs[n_in:n_in + ex.n]
        x_outs = refs[n_in + n_xin + n_out:n_in + n_xin + n_out + ex.n]

        @pl.when(first)
        def _():
            for cp in ex.copies(x_ins, x_outs, refs[-3:]):
                cp.start()

        body(*mine)

        @pl.when(last)
        def _():
            for cp in ex.copies(x_ins, x_outs, refs[-3:]):
                cp.wait()

    res = pl.pallas_call(
        fused, name=name, grid=grid, in_specs=list(in_specs) + ex.in_specs, out_specs=list(out_specs) + ex.out_specs,
        out_shape=list(out_shape) + ex.out_shape, input_output_aliases=ex.aliases(n_in, n_out),
        scratch_shapes=list(scratch_shapes) + ex.scratch,
        compiler_params=pltpu.CompilerParams(dimension_semantics=("arbitrary",) * len(grid),
                                             vmem_limit_bytes=vmem_bytes),
    )(*operands, *ex.operands)
    return res[:n_out], res[n_out:]


NN = (((1,), (0,)), ((), ()))
NT = (((1,), (1,)), ((), ()))
TN = (((0,), (0,)), ((), ()))


def _mm(name, a, b, a_spec, b_spec, o_spec, o_shape, o_dtype, grid, dims):
    nk = grid[2]
    acc_shape = tuple(d for d in o_spec.block_shape if d is not None)

    def product(a_ref, b_ref):
        return lax.dot_general(a_ref[...].astype(BF16), b_ref[...].astype(BF16), dims, preferred_element_type=F32)

    def body_once(a_ref, b_ref, o_ref):
        o_ref[...] = product(a_ref, b_ref).astype(o_ref.dtype)

    def body(a_ref, b_ref, o_ref, acc_ref):
        k = pl.program_id(2)

        @pl.when(k == 0)
        def _():
            acc_ref[...] = product(a_ref, b_ref)

        @pl.when(k > 0)
        def _():
            acc_ref[...] += product(a_ref, b_ref)

        @pl.when(k == nk - 1)
        def _():
            o_ref[...] = acc_ref[...].astype(o_ref.dtype)

    blk = (_nbytes(a_spec.block_shape, a.dtype) + _nbytes(b_spec.block_shape, b.dtype)
           + _nbytes(acc_shape, o_dtype) + _nbytes(acc_shape, F32))
    return pl.pallas_call(
        body_once if nk == 1 else body, name=name, grid=grid, in_specs=[a_spec, b_spec], out_specs=o_spec,
        out_shape=jax.ShapeDtypeStruct(o_shape, o_dtype),
        scratch_shapes=[] if nk == 1 else [pltpu.VMEM(acc_shape, F32)],
        compiler_params=pltpu.CompilerParams(dimension_semantics=("parallel", "parallel", "arbitrary"),
                                             vmem_limit_bytes=_vmem_limit(blk)),
    )(a, b)


def _tile(n, pref=1024):
    t = pref
    while t >= LANES:
        if n % t == 0:
            return t
        t -= LANES
    return n


def _rowwise(name, fn, ins, outs, grid):
    n_in = len(ins)

    def body(*refs):
        vals = fn(*[r[...].astype(F32) for r in refs[:n_in]])
        if not isinstance(vals, (tuple, list)):
            vals = (vals,)
        for r, v in zip(refs[n_in:], vals):
            r[...] = v.astype(r.dtype)

    blk = sum(_nbytes(bs, a.dtype) for a, bs, _ in ins) + sum(_nbytes(bs, d) + _nbytes(bs, F32) for _, d, bs, _ in outs)
    return pl.pallas_call(
        body, name=name, grid=grid,
        in_specs=[_spec(bs, im) for _, bs, im in ins],
        out_specs=[_spec(bs, im) for _, _, bs, im in outs],
        out_shape=[jax.ShapeDtypeStruct(s, d) for s, d, _, _ in outs],
        compiler_params=pltpu.CompilerParams(dimension_semantics=("parallel",) * len(grid),
                                             vmem_limit_bytes=_vmem_limit(2 * blk)),
    )(*[a for a, _, _ in ins])


def _rowwise_vjp(name, fn, ins, cts, wrt, grid):
    n_in, n_ct = len(ins), len(cts)
    idx = [w[0] for w in wrt]

    def body(*refs):
        prim = [r[...].astype(F32) for r in refs[:n_in]]
        ct = tuple(r[...].astype(F32) for r in refs[n_in:n_in + n_ct])
        o_refs = refs[n_in + n_ct:]

        def g(*sel):
            full = list(prim)
            for i, s in zip(idx, sel):
                full[i] = s
            out = fn(*full)
            return tuple(out) if isinstance(out, (tuple, list)) else (out,)

        _, pull = jax.vjp(g, *[prim[i] for i in idx])
        grads = pull(ct)
        first = pl.program_id(0) == 0
        for d in range(1, len(grid)):
            first = jnp.logical_and(first, pl.program_id(d) == 0)
        for w, o_ref, gr in zip(wrt, o_refs, grads):
            if w[1] == "row":
                o_ref[...] = gr.astype(o_ref.dtype)
            else:
                @pl.when(first)
                def _(o_ref=o_ref):
                    o_ref[...] = jnp.zeros_like(o_ref)

                o_ref[...] += gr.astype(o_ref.dtype)

    blk = (sum(_nbytes(bs, a.dtype) + _nbytes(bs, F32) for a, bs, _ in list(ins) + list(cts))
           + sum(_nbytes(w[4], w[3]) + _nbytes(w[4], F32) for w in wrt))
    return pl.pallas_call(
        body, name=name, grid=grid,
        in_specs=[_spec(bs, im) for _, bs, im in list(ins) + list(cts)],
        out_specs=[_spec(w[4], w[5]) for w in wrt],
        out_shape=[jax.ShapeDtypeStruct(w[2], w[3]) for w in wrt],
        compiler_params=pltpu.CompilerParams(dimension_semantics=("arbitrary",) * len(grid),
                                             vmem_limit_bytes=_vmem_limit(2 * blk)),
    )(*[a for a, _, _ in list(ins) + list(cts)])


def _normalize(x):
    mu = jnp.mean(x, axis=-1, keepdims=True)
    xc = x - mu
    var = jnp.mean(xc * xc, axis=-1, keepdims=True)
    return xc * lax.rsqrt(var + LN_EPS)


def _modulate(x, sc, sh):
    return _normalize(x) * (1.0 + sc) + sh


def _make_resid_fns(alpha):
    def resid_ln(x, y, gate, g, b):
        return _normalize(alpha * x + (1.0 + gate) * y) * g + b

    def resid_ln_mod(x, y, gate, g, b, sc, sh):
        xn = resid_ln(x, y, gate, g, b)
        return xn, _modulate(xn, sc, sh)

    return resid_ln, resid_ln_mod


def _merge_fn(y_sb, y_ssm, g_sb, g_ssm):
    return jax.nn.sigmoid(g_sb) * y_sb + jax.nn.sigmoid(g_ssm) * y_ssm


def _swiglu_fn(gate, up):
    return gate * jax.nn.sigmoid(gate) * up


def _s5_act_fn(yc, u, d_skip):
    return jax.nn.gelu(yc + d_skip * u)


def _s5_glu_fn(yc, u, t, d_skip, b_glu):
    return _s5_act_fn(yc, u, d_skip) * jax.nn.sigmoid(t + b_glu)


def _s5_post_fn(yc, u, t, d_skip, b_glu):
    y1 = _s5_act_fn(yc, u, d_skip)
    return y1, y1 * jax.nn.sigmoid(t + b_glu)


def _sb_tri(kind):
    row = lax.broadcasted_iota(jnp.int32, (SB_BLOCK, SB_BLOCK), 0)
    col = lax.broadcasted_iota(jnp.int32, (SB_BLOCK, SB_BLOCK), 1)
    if kind == "after":
        return (row > col).astype(BF16)
    if kind == "from":
        return (row >= col).astype(BF16)
    return col < row


def _split_dot(x, m):
    hi = x.astype(BF16)
    lo = (x - hi.astype(F32)).astype(BF16)
    return (lax.dot_general(hi, m, NN, preferred_element_type=F32)
            + lax.dot_general(lo, m, NN, preferred_element_type=F32))


def _sb_scores(qh, k2, scale):
    z = lax.dot_general(qh, k2, NT, preferred_element_type=F32) * scale
    sp = jnp.log(1.0 + jnp.exp(-jnp.abs(z)))
    log_beta = jnp.minimum(z, 0.0) - sp
    log_1m = -jnp.maximum(z, 0.0) - sp
    return log_beta, log_1m


def _sb_attention_fwd(proj, sb_width, beside=None):
    seq = proj.shape[0]
    n_pair, n_q = sb_width // LANES, seq // SB_BLOCK
    scale = 1.0 / (HEAD_DIM ** 0.5)

    def body(q_ref, k_ref, v_ref, o_ref, o32_ref):
        qi = pl.program_id(1)
        q2 = q_ref[...]
        lane = lax.broadcasted_iota(jnp.int32, (SB_BLOCK, LANES), 1)
        m_after, causal = _sb_tri("after"), _sb_tri("mask")
        heads = [lane < HEAD_DIM, lane >= HEAD_DIM]
        qh = [jnp.where(m, q2, 0.0).astype(BF16) for m in heads]

        def scores(kb, diag):
            ks = pl.multiple_of(kb * SB_BLOCK, SB_BLOCK)
            k2 = k_ref[pl.ds(ks, SB_BLOCK), :].astype(BF16)
            out = []
            for h in range(2):
                log_beta, log_1m = _sb_scores(qh[h], k2, scale)
                if diag:
                    log_1m = jnp.where(causal, log_1m, 0.0)
                out += [log_beta + _split_dot(log_1m, m_after), jnp.sum(log_1m, axis=1, keepdims=True)]
            return tuple(out)

        def weigh(kb, sc, carry, acc, diag):
            ks = pl.multiple_of(kb * SB_BLOCK, SB_BLOCK)
            v2 = v_ref[pl.ds(ks, SB_BLOCK), :].astype(BF16)
            out = []
            for h in range(2):
                w = jnp.exp(sc[2 * h] + carry[h])
                if diag:
                    w = jnp.where(causal, w, 0.0)
                w_hi = w.astype(BF16)
                w_lo = (w - w_hi.astype(F32)).astype(BF16)
                out += [acc[2 * h] + lax.dot_general(w_hi, v2, NN, preferred_element_type=F32),
                        acc[2 * h + 1] + lax.dot_general(w_lo, v2, NN, preferred_element_type=F32)]
            return tuple(out)

        zero = jnp.zeros((SB_BLOCK, LANES), F32)
        zcol = jnp.zeros((SB_BLOCK, 1), F32)
        sc = scores(qi, True)
        acc = weigh(qi, sc, (zcol, zcol), (zero,) * 4, True)
        carry = (sc[1], sc[3])
        sc = scores(jnp.maximum(qi - 1, 0), False)

        def loop(st):
            kb, sc, carry, acc = st
            after = (carry[0] + sc[1], carry[1] + sc[3])
            done = jnp.maximum(jnp.max(after[0]), jnp.max(after[1])) < SB_UNDERFLOW
            sc_next = scores(jnp.maximum(kb - 1, 0), False)
            acc = weigh(kb, sc, carry, acc, False)
            return jnp.where(done, -1, kb - 1), sc_next, after, acc

        _, _, _, acc = lax.while_loop(lambda st: st[0] >= 0, loop, (qi - 1, sc, carry, acc))
        o_ref[...] = jnp.where(heads[0], acc[0], acc[2]).astype(o_ref.dtype)
        o32_ref[...] = jnp.where(heads[0], acc[0] + acc[1], acc[2] + acc[3])

    q_spec = _spec((SB_BLOCK, LANES), lambda h, i: (i, h))
    kv = [_spec((seq, LANES), lambda h, i, o=o: (0, o + h)) for o in (n_pair, 2 * n_pair)]
    o_spec = _spec((SB_BLOCK, LANES), lambda h, i: (i, h))
    return _call_beside(
        beside, body, "sb_attention_fwd", (n_pair, n_q), [q_spec] + kv, [o_spec, o_spec],
        [jax.ShapeDtypeStruct((seq, sb_width), BF16), jax.ShapeDtypeStruct((seq, sb_width), F32)], [],
        _vmem_limit(2 * seq * LANES * 4), (proj, proj, proj), ("parallel", "arbitrary"))


def _sb_attention_bwd(proj, o32, do, sb_width, beside=None):
    seq = proj.shape[0]
    n_pair, n_q = sb_width // LANES, seq // SB_BLOCK
    scale = 1.0 / (HEAD_DIM ** 0.5)

    def body(q_ref, k_ref, v_ref, o_ref, do_ref, dq_ref, dk_ref, dv_ref):
        qi = pl.program_id(1)

        @pl.when(qi == 0)
        def _():
            dk_ref[...] = jnp.zeros_like(dk_ref)
            dv_ref[...] = jnp.zeros_like(dv_ref)

        q2 = q_ref[...]
        do2 = do_ref[...].astype(F32)
        o2 = o_ref[...]
        lane = lax.broadcasted_iota(jnp.int32, (SB_BLOCK, LANES), 1)
        m_after, m_from, causal = _sb_tri("after"), _sb_tri("from"), _sb_tri("mask")
        heads = [lane < HEAD_DIM, lane >= HEAD_DIM]
        qh = [jnp.where(m, q2, 0.0).astype(BF16) for m in heads]
        doh = [jnp.where(m, do2, 0.0) for m in heads]
        doh_b = [v.astype(BF16) for v in doh]
        total = [jnp.sum(v * o2, axis=1, keepdims=True) for v in doh]

        def scores(kb, diag):
            ks = pl.multiple_of(kb * SB_BLOCK, SB_BLOCK)
            k2 = k_ref[pl.ds(ks, SB_BLOCK), :].astype(BF16)
            v2 = v_ref[pl.ds(ks, SB_BLOCK), :].astype(BF16)
            out = []
            for h in range(2):
                log_beta, log_1m = _sb_scores(qh[h], k2, scale)
                if diag:
                    log_1m = jnp.where(causal, log_1m, 0.0)
                out += [log_beta + _split_dot(log_1m, m_after), jnp.sum(log_1m, axis=1, keepdims=True),
                        lax.dot_general(doh_b[h], v2, NT, preferred_element_type=F32), log_beta]
            return tuple(out)

        def pull(kb, sc, carry, right, dq, diag):
            ks = pl.multiple_of(kb * SB_BLOCK, SB_BLOCK)
            k2 = k_ref[pl.ds(ks, SB_BLOCK), :].astype(BF16)
            dv_blk, dk_blk, right_out, dq_out = None, None, [], []
            for h in range(2):
                arg, _, d_w, log_beta = sc[4 * h:4 * h + 4]
                w = jnp.exp(arg + carry[h])
                if diag:
                    w = jnp.where(causal, w, 0.0)
                d_arg = d_w * w
                dv_h = lax.dot_general(w.astype(BF16), doh_b[h], TN, preferred_element_type=F32)
                d_log_1m = total[h] - right[h] - _split_dot(d_arg, m_from)
                beta = jnp.exp(log_beta)
                dz = d_arg * (1.0 - beta) - beta * d_log_1m
                if diag:
                    dz = jnp.where(causal, dz, 0.0)
                dz_b = (dz * scale).astype(BF16)
                dk_h = lax.dot_general(dz_b, qh[h], TN, preferred_element_type=F32)
                dv_blk = dv_h if h == 0 else dv_blk + dv_h
                dk_blk = dk_h if h == 0 else dk_blk + dk_h
                dq_out.append(dq[h] + lax.dot_general(dz_b, k2, NN, preferred_element_type=F32))
                right_out.append(right[h] + jnp.sum(d_arg, axis=1, keepdims=True))
            dv_ref[pl.ds(ks, SB_BLOCK), :] += dv_blk
            dk_ref[pl.ds(ks, SB_BLOCK), :] += dk_blk
            return tuple(right_out), tuple(dq_out)

        zero = jnp.zeros((SB_BLOCK, LANES), F32)
        zcol = jnp.zeros((SB_BLOCK, 1), F32)
        sc = scores(qi, True)
        right, dq = pull(qi, sc, (zcol, zcol), (zcol, zcol), (zero, zero), True)
        carry = (sc[1], sc[5])
        sc = scores(jnp.maximum(qi - 1, 0), False)

        def loop(st):
            kb, sc, carry, right, dq = st
            after = (carry[0] + sc[1], carry[1] + sc[5])
            done = jnp.maximum(jnp.max(after[0]), jnp.max(after[1])) < SB_UNDERFLOW
            sc_next = scores(jnp.maximum(kb - 1, 0), False)
            right, dq = pull(kb, sc, carry, right, dq, False)
            return jnp.where(done, -1, kb - 1), sc_next, after, right, dq

        _, _, _, _, dq = lax.while_loop(lambda st: st[0] >= 0, loop, (qi - 1, sc, carry, right, dq))
        dq_ref[...] = jnp.where(heads[0], dq[0], dq[1]).astype(dq_ref.dtype)

    q_spec = _spec((SB_BLOCK, LANES), lambda h, i: (i, h))
    kv = [_spec((seq, LANES), lambda h, i, o=o: (0, o + h)) for o in (n_pair, 2 * n_pair)]
    full = _spec((seq, LANES), lambda h, i: (0, h))
    return _call_beside(
        beside, body, "sb_attention_bwd", (n_pair, n_q), [q_spec] + kv + [q_spec, q_spec], [q_spec, full, full],
        [jax.ShapeDtypeStruct((seq, sb_width), BF16), jax.ShapeDtypeStruct((seq, sb_width), F32),
         jax.ShapeDtypeStruct((seq, sb_width), F32)], [],
        _vmem_limit(4 * seq * LANES * 4), (proj, proj, proj, o32, do), ("parallel", "arbitrary"))


def _s5_discretize(a_re, a_im, log_dt, b_re, b_im, c_re, c_im):
    n_g, n_p = a_re.shape
    c_g = b_re.shape[-1]
    ns = n_g // SLAB_GROUPS
    dt = jnp.exp(log_dt)[:, None]
    xr, xi = a_re * dt, a_im * dt
    mag = jnp.exp(xr)
    lr, li = mag * jnp.cos(xi), mag * jnp.sin(xi)
    den = a_re * a_re + a_im * a_im
    fr = ((lr - 1.0) * a_re + li * a_im) / den
    fi = (li * a_re - (lr - 1.0) * a_im) / den
    bb_re = fr[..., None] * b_re - fi[..., None] * b_im
    bb_im = fr[..., None] * b_im + fi[..., None] * b_re
    eye = jnp.eye(SLAB_GROUPS, dtype=F32)

    def diag_b(m):
        m = jnp.transpose(m.reshape(ns, SLAB_GROUPS, n_p, c_g), (0, 1, 3, 2))
        m = m[:, :, :, None, :] * eye[None, :, None, :, None]
        return m.reshape(ns, SLAB_GROUPS * c_g, SLAB_GROUPS * n_p)

    def diag_c(m):
        m = jnp.transpose(m.reshape(ns, SLAB_GROUPS, c_g, n_p), (0, 1, 3, 2))
        m = m[:, :, :, None, :] * eye[None, :, None, :, None]
        return m.reshape(ns, SLAB_GROUPS * n_p, SLAB_GROUPS * c_g)

    bs = jnp.concatenate([diag_b(bb_re), diag_b(bb_im)], axis=-1)
    cs = jnp.concatenate([diag_c(c_re), -diag_c(c_im)], axis=1)
    lam = jnp.concatenate([lr.reshape(ns, 1, -1), li.reshape(ns, 1, -1)], axis=-1)
    return bs, cs, lam


def _s5_powers(a_re, a_im, log_dt, n):
    n_g, n_p = a_re.shape
    ns = n_g // SLAB_GROUPS
    dt = jnp.exp(log_dt)[:, None]
    mag = jnp.exp(a_re * dt)
    base_r, base_i = mag * jnp.cos(a_im * dt), mag * jnp.sin(a_im * dt)
    steps = jnp.arange(1, n + 1, dtype=jnp.int32)[:, None, None]
    pr, pi = jnp.ones((n, n_g, n_p), F32), jnp.zeros((n, n_g, n_p), F32)
    for b in range(n.bit_length()):
        take = ((steps >> b) & 1) == 1
        pr, pi = (jnp.where(take, pr * base_r - pi * base_i, pr), jnp.where(take, pr * base_i + pi * base_r, pi))
        base_r, base_i = base_r * base_r - base_i * base_i, 2.0 * base_r * base_i

    def slabs(re, im):
        one = lambda m: jnp.transpose(m.reshape(n, ns, SLAB_GROUPS * n_p), (1, 0, 2))
        return jnp.concatenate([one(re), one(im)], axis=-1)

    return slabs(pr, pi), slabs(pr[::-1], -pi[::-1])


def _lanes(j):
    return slice(j * LANES, (j + 1) * LANES)


def _tile8(k):
    return pl.ds(pl.multiple_of(k * SUBLANES, SUBLANES), SUBLANES)


def _s5_interleave(dst_ref, src_ref, t_seg):
    def body(k, _):
        dst_ref[_tile8(k), :] = src_ref[pl.ds(k, SUBLANES, stride=t_seg), :]
        return 0

    lax.fori_loop(0, t_seg, body, 0, unroll=4)


def _s5_join_segments(st_ref, end_ref, car_ref, tab_ref, row, order, n_pair):
    for j in range(n_pair):
        re, im = _lanes(j), _lanes(n_pair + j)
        cr, ci = st_ref[:, re], st_ref[:, im]
        tr, ti = tab_ref[row:row + 1, re], tab_ref[row:row + 1, im]
        for s in order:
            car_ref[s:s + 1, re] = cr
            car_ref[s:s + 1, im] = ci
            er, ei = end_ref[s:s + 1, re], end_ref[s:s + 1, im]
            cr, ci = er + tr * cr - ti * ci, ei + tr * ci + ti * cr
        st_ref[:, re] = cr
        st_ref[:, im] = ci


def _s5_add_carries(buf_ref, car_ref, tab_ref, t_seg, n_pair):
    def fix(k, _):
        rows = _tile8(k)
        tab = tab_ref[pl.ds(k, 1), :]
        for j in range(n_pair):
            re, im = _lanes(j), _lanes(n_pair + j)
            cr, ci = car_ref[:, re], car_ref[:, im]
            tr, ti = tab[:, re], tab[:, im]
            buf_ref[rows, re] += tr * cr - ti * ci
            buf_ref[rows, im] += tr * ci + ti * cr
        return 0

    lax.fori_loop(0, t_seg, fix, 0, unroll=2)


def _s5_scan_fwd(proj, u_col, bs, cs, lam, pw, t_blk, beside=None):
    seq = proj.shape[0]
    ns, _, w2 = bs.shape
    n_pair = w2 // (2 * LANES)
    t_seg, n_t = t_blk // SUBLANES, seq // t_blk

    def body(u_ref, bs_ref, cs_ref, lam_ref, pw_ref, yc_ref, h_ref, st_ref, end_ref, car_ref, ui_ref, bu_ref, yi_ref):
        @pl.when(pl.program_id(1) == 0)
        def _():
            st_ref[...] = jnp.zeros_like(st_ref)

        _s5_interleave(ui_ref, u_ref, t_seg)
        bu_ref[...] = lax.dot_general(ui_ref[...].astype(BF16), bs_ref[...], NN, preferred_element_type=F32)
        lam_r = [jnp.broadcast_to(lam_ref[:, _lanes(j)], (SUBLANES, LANES)) for j in range(n_pair)]
        lam_i = [jnp.broadcast_to(lam_ref[:, _lanes(n_pair + j)], (SUBLANES, LANES)) for j in range(n_pair)]

        def step(k, c):
            rows = _tile8(k)
            out = []
            for j in range(n_pair):
                hr, hi = c[2 * j], c[2 * j + 1]
                nr = lam_r[j] * hr - lam_i[j] * hi + bu_ref[rows, _lanes(j)]
                ni = lam_i[j] * hr + lam_r[j] * hi + bu_ref[rows, _lanes(n_pair + j)]
                h_ref[rows, _lanes(j)] = nr
                h_ref[rows, _lanes(n_pair + j)] = ni
                out += [nr, ni]
            return tuple(out)

        ends = lax.fori_loop(0, t_seg, step, (jnp.zeros((SUBLANES, LANES), F32),) * (2 * n_pair), unroll=4)
        for j in range(n_pair):
            end_ref[:, _lanes(j)] = ends[2 * j]
            end_ref[:, _lanes(n_pair + j)] = ends[2 * j + 1]
        _s5_join_segments(st_ref, end_ref, car_ref, pw_ref, t_seg - 1, list(range(SUBLANES)), n_pair)
        _s5_add_carries(h_ref, car_ref, pw_ref, t_seg, n_pair)
        yi_ref[...] = lax.dot_general(h_ref[...].astype(BF16), cs_ref[...], NN, preferred_element_type=F32)

        def scatter(k, _):
            yc_ref[pl.ds(k, SUBLANES, stride=t_seg), :] = yi_ref[_tile8(k), :]
            return 0

        lax.fori_loop(0, t_seg, scatter, 0, unroll=4)

    return _call_beside(
        beside, body, "s5_scan_fwd", (ns, n_t),
        [_spec((t_blk, LANES), lambda s, i: (i, u_col + s)),
         _spec((None, LANES, w2), lambda s, i: (s, 0, 0)),
         _spec((None, w2, LANES), lambda s, i: (s, 0, 0)),
         _spec((None, 1, w2), lambda s, i: (s, 0, 0)),
         _spec((None, t_seg, w2), lambda s, i: (s, 0, 0))],
        [_spec((t_blk, LANES), lambda s, i: (i, s)),
         _spec((None, t_blk, w2), lambda s, i: (s, i, 0))],
        [jax.ShapeDtypeStruct((seq, ns * LANES), F32), jax.ShapeDtypeStruct((ns, seq, w2), F32)],
        [pltpu.VMEM((1, w2), F32), pltpu.VMEM((SUBLANES, w2), F32), pltpu.VMEM((SUBLANES, w2), F32),
         pltpu.VMEM((t_blk, LANES), F32), pltpu.VMEM((t_blk, w2), F32), pltpu.VMEM((t_blk, LANES), F32)],
        _vmem_limit(3 * t_blk * w2 * 4), (proj, bs, cs, lam, pw), ("parallel", "arbitrary"))


def _s5_scan_bwd(proj, u_col, states, d_yc, du_extra, bs, cs, lam, qw, t_blk):
    seq = proj.shape[0]
    ns, _, w2 = bs.shape
    n_pair = w2 // (2 * LANES)
    t_seg, n_t = t_blk // SUBLANES, seq // t_blk

    def body(u_ref, h_ref, hp_ref, dyc_ref, dux_ref, bs_ref, cs_ref, lam_ref, qw_ref,
             du_ref, dbs_ref, dcs_ref, dlam_ref, g_ref, gd_ref, st_ref, end_ref, car_ref, ui_ref, dyi_ref, dui_ref):
        i = pl.program_id(1)

        @pl.when(i == 0)
        def _():
            st_ref[...] = jnp.zeros_like(st_ref)
            dbs_ref[...] = jnp.zeros_like(dbs_ref)
            dcs_ref[...] = jnp.zeros_like(dcs_ref)
            dlam_ref[...] = jnp.zeros_like(dlam_ref)

        _s5_interleave(ui_ref, u_ref, t_seg)
        _s5_interleave(dyi_ref, dyc_ref, t_seg)
        dyc_b = dyi_ref[...].astype(BF16)
        gd_ref[...] = lax.dot_general(dyc_b, cs_ref[...], NT, preferred_element_type=F32)
        lam_r = [jnp.broadcast_to(lam_ref[:, _lanes(j)], (SUBLANES, LANES)) for j in range(n_pair)]
        lam_i = [jnp.broadcast_to(lam_ref[:, _lanes(n_pair + j)], (SUBLANES, LANES)) for j in range(n_pair)]

        def step(kk, c):
            rows = _tile8(t_seg - 1 - kk)
            out = []
            for j in range(n_pair):
                gr_n, gi_n = c[2 * j], c[2 * j + 1]
                gr = gd_ref[rows, _lanes(j)] + lam_r[j] * gr_n + lam_i[j] * gi_n
                gi = gd_ref[rows, _lanes(n_pair + j)] + lam_r[j] * gi_n - lam_i[j] * gr_n
                g_ref[rows, _lanes(j)] = gr
                g_ref[rows, _lanes(n_pair + j)] = gi
                out += [gr, gi]
            return tuple(out)

        zero = jnp.zeros((SUBLANES, LANES), F32)
        firsts = lax.fori_loop(0, t_seg, step, (zero,) * (2 * n_pair), unroll=4)
        for j in range(n_pair):
            end_ref[:, _lanes(j)] = firsts[2 * j]
            end_ref[:, _lanes(n_pair + j)] = firsts[2 * j + 1]
        _s5_join_segments(st_ref, end_ref, car_ref, qw_ref, 0, list(range(SUBLANES))[::-1], n_pair)
        _s5_add_carries(g_ref, car_ref, qw_ref, t_seg, n_pair)

        def pair_up(k, c):
            rows, prev = _tile8(k), _tile8(k - 1)
            out = []
            for j in range(n_pair):
                re, im = _lanes(j), _lanes(n_pair + j)
                gr, gi, hr, hi = g_ref[rows, re], g_ref[rows, im], h_ref[prev, re], h_ref[prev, im]
                out += [c[2 * j] + gr * hr + gi * hi, c[2 * j + 1] + gi * hr - gr * hi]
            return tuple(out)

        acc = lax.fori_loop(1, t_seg, pair_up, (zero,) * (2 * n_pair), unroll=4)
        has_prev = (i < n_t - 1).astype(F32)
        first_seg = lax.broadcasted_iota(jnp.int32, (SUBLANES, LANES), 0) == 0
        last = _tile8(t_seg - 1)
        for j in range(n_pair):
            re, im = _lanes(j), _lanes(n_pair + j)
            gr, gi = g_ref[0:SUBLANES, re], g_ref[0:SUBLANES, im]
            hr = jnp.where(first_seg, hp_ref[SUBLANES - 1:, re] * has_prev, pltpu.roll(h_ref[last, re], 1, 0))
            hi = jnp.where(first_seg, hp_ref[SUBLANES - 1:, im] * has_prev, pltpu.roll(h_ref[last, im], 1, 0))
            dlam_ref[:, re] += jnp.sum(acc[2 * j] + gr * hr + gi * hi, axis=0, keepdims=True)
            dlam_ref[:, im] += jnp.sum(acc[2 * j + 1] + gi * hr - gr * hi, axis=0, keepdims=True)

        g_b = g_ref[...].astype(BF16)
        dui_ref[...] = lax.dot_general(g_b, bs_ref[...], NT, preferred_element_type=F32)
        dbs_ref[...] += lax.dot_general(ui_ref[...].astype(BF16), g_b, TN, preferred_element_type=F32)
        dcs_ref[...] += lax.dot_general(h_ref[...].astype(BF16), dyc_b, TN, preferred_element_type=F32)

        def scatter(k, _):
            rows = pl.ds(k, SUBLANES, stride=t_seg)
            du_ref[rows, :] = (dui_ref[_tile8(k), :] + dux_ref[rows, :]).astype(du_ref.dtype)
            return 0

        lax.fori_loop(0, t_seg, scatter, 0, unroll=4)

    rev = lambda i: n_t - 1 - i
    return pl.pallas_call(
        body, name="s5_scan_bwd", grid=(ns, n_t),
        in_specs=[_spec((t_blk, LANES), lambda s, i: (rev(i), u_col + s)),
                  _spec((None, t_blk, w2), lambda s, i: (s, rev(i), 0)),
                  _spec((None, SUBLANES, w2), lambda s, i: (s, jnp.maximum(rev(i) * t_seg - 1, 0), 0)),
                  _spec((t_blk, LANES), lambda s, i: (rev(i), s)),
                  _spec((t_blk, LANES), lambda s, i: (rev(i), s)),
                  _spec((None, LANES, w2), lambda s, i: (s, 0, 0)),
                  _spec((None, w2, LANES), lambda s, i: (s, 0, 0)),
                  _spec((None, 1, w2), lambda s, i: (s, 0, 0)),
                  _spec((None, t_seg, w2), lambda s, i: (s, 0, 0))],
        out_specs=[_spec((t_blk, LANES), lambda s, i: (rev(i), s)),
                   _spec((None, LANES, w2), lambda s, i: (s, 0, 0)),
                   _spec((None, w2, LANES), lambda s, i: (s, 0, 0)),
                   _spec((None, 1, w2), lambda s, i: (s, 0, 0))],
        out_shape=[jax.ShapeDtypeStruct((seq, ns * LANES), F32), jax.ShapeDtypeStruct(bs.shape, F32),
                   jax.ShapeDtypeStruct(cs.shape, F32), jax.ShapeDtypeStruct(lam.shape, F32)],
        scratch_shapes=[pltpu.VMEM((t_blk, w2), F32), pltpu.VMEM((t_blk, w2), F32), pltpu.VMEM((1, w2), F32),
                        pltpu.VMEM((SUBLANES, w2), F32), pltpu.VMEM((SUBLANES, w2), F32),
                        pltpu.VMEM((t_blk, LANES), F32), pltpu.VMEM((t_blk, LANES), F32), pltpu.VMEM((t_blk, LANES), F32)],
        compiler_params=pltpu.CompilerParams(dimension_semantics=("parallel", "arbitrary"),
                                             vmem_limit_bytes=_vmem_limit(5 * t_blk * w2 * 4)),
    )(proj, states, states, d_yc, du_extra, bs, cs, lam, qw)


def _loss_head(y, target, t_m):
    seq, d = y.shape

    def body(y_ref, t_ref, loss_ref, dy_ref):
        @pl.when(pl.program_id(0) == 0)
        def _():
            loss_ref[...] = jnp.zeros_like(loss_ref)

        diff = y_ref[...] - t_ref[...]
        dy_ref[...] = diff / d
        loss_ref[...] += 0.5 * jnp.sum(diff * diff) / d

    row = _spec((t_m, d), lambda i: (i, 0))
    return pl.pallas_call(
        body, name="loss_head", grid=(seq // t_m,), in_specs=[row, row],
        out_specs=[_spec((SUBLANES, LANES), lambda i: (0, 0)), row],
        out_shape=[jax.ShapeDtypeStruct((SUBLANES, LANES), F32), jax.ShapeDtypeStruct((seq, d), F32)],
        compiler_params=pltpu.CompilerParams(dimension_semantics=("arbitrary",),
                                             vmem_limit_bytes=_vmem_limit(6 * t_m * d * 4)),
    )(y, target)


def _adamw_fn(w, m, v, *partials):
    g = partials[0]
    for p in partials[1:]:
        g = g + p
    m2 = ADAM_B1 * m + (1.0 - ADAM_B1) * g
    v2 = ADAM_B2 * v + (1.0 - ADAM_B2) * (g * g)
    m_hat = m2 / (1.0 - ADAM_B1 ** ADAM_STEP)
    v_hat = v2 / (1.0 - ADAM_B2 ** ADAM_STEP)
    delta = -ADAM_LR * (m_hat / (jnp.sqrt(v_hat) + ADAM_EPS) + ADAM_WD * w)
    return g, delta, m2, v2


def _adamw(name, w, m, v, partials):
    rows, cols = w.shape
    t_r = rows
    for cand in (512, 256, 128, 64, 32, 16, 8):
        if rows % cand == 0 and cand * cols * 4 <= (1 << 20):
            t_r = cand
            break
    n_p = partials.shape[0]
    row = lambda i: (i, 0)
    ins = [(a, (t_r, cols), row) for a in (w, m, v)]
    ins += [(partials, (None, t_r, cols), (lambda i, j=j: (j, i, 0))) for j in range(n_p)]
    outs = [((rows, cols), F32, (t_r, cols), row)] * 4
    return _rowwise(name, _adamw_fn, ins, outs, (rows // t_r,))


SMALL_PARAMS = ("b_ada", "ssm_a_re", "ssm_a_im", "ssm_log_dt", "ssm_b_re", "ssm_b_im", "ssm_c_re", "ssm_c_im",
                "ssm_d", "b_glu", "ln1_g", "ln1_b", "ln2_g", "ln2_b")
WEIGHTS = ("w_ada", "b_ada", "w_in", "w_sb_up", "ssm_a_re", "ssm_a_im", "ssm_log_dt", "ssm_b_re", "ssm_b_im",
           "ssm_c_re", "ssm_c_im", "ssm_d", "w_glu", "b_glu", "w_ssm_up", "w_out", "ln1_g", "ln1_b", "w_ffn_in",
           "w_ffn_out", "ln2_g", "ln2_b")
ARG_NAMES = (("x", "c") + WEIGHTS + ("loss_target",) + tuple("m_" + n for n in WEIGHTS)
             + tuple("v_" + n for n in WEIGHTS))


def _pack(arrs):
    flat = jnp.concatenate([a.reshape(-1) for a in arrs])
    pad = (-flat.shape[0]) % (PACK_ROWS * LANES)
    return jnp.pad(flat, (0, pad)).reshape(-1, LANES)


def _unpack(packed, like):
    lead = packed.shape[:-2]
    flat = packed.reshape(lead + (-1,))
    out, off = [], 0
    for a in like:
        out.append(flat[..., off:off + a.size].reshape(lead + a.shape))
        off += a.size
    return out


def kernel(x, c, w_ada, b_ada, w_in, w_sb_up, ssm_a_re, ssm_a_im, ssm_log_dt, ssm_b_re, ssm_b_im, ssm_c_re,
           ssm_c_im, ssm_d, w_glu, b_glu, w_ssm_up, w_out, ln1_g, ln1_b, w_ffn_in, w_ffn_out, ln2_g, ln2_b,
           loss_target, m_w_ada, m_b_ada, m_w_in, m_w_sb_up, m_ssm_a_re, m_ssm_a_im, m_ssm_log_dt, m_ssm_b_re,
           m_ssm_b_im, m_ssm_c_re, m_ssm_c_im, m_ssm_d, m_w_glu, m_b_glu, m_w_ssm_up, m_w_out, m_ln1_g, m_ln1_b,
           m_w_ffn_in, m_w_ffn_out, m_ln2_g, m_ln2_b, v_w_ada, v_b_ada, v_w_in, v_w_sb_up, v_ssm_a_re, v_ssm_a_im,
           v_ssm_log_dt, v_ssm_b_re, v_ssm_b_im, v_ssm_c_re, v_ssm_c_im, v_ssm_d, v_w_glu, v_b_glu, v_w_ssm_up,
           v_w_out, v_ln1_g, v_ln1_b, v_w_ffn_in, v_w_ffn_out, v_ln2_g, v_ln2_b):
    given = locals()
    return _train_step({n: given[n] for n in ARG_NAMES})


def _train_step(p):
    x0 = p["x"][0]
    target = p["loss_target"][0]
    seq, d = x0.shape
    depth = p["w_ada"].shape[0]
    n_ada = p["w_ada"].shape[2]
    n_in = p["w_in"].shape[2]
    sb_w = p["w_sb_up"].shape[1]
    ssm_w = p["w_ssm_up"].shape[1]
    n_up = p["w_sb_up"].shape[2]
    n_ffn = p["w_ffn_in"].shape[2]
    ffn = N_DEV * p["w_ffn_out"].shape[1]
    in_cols = N_DEV * n_in
    alpha = (2 * depth) ** 0.25
    resid_ln, resid_ln_mod = _make_resid_fns(alpha)
    t_r = min(512, seq)
    n_r = seq // t_r
    t_m = min(1024, seq)
    n_m = seq // t_m
    t_d = _tile(d)
    assert n_ffn * (N_DEV // 2) == ffn and sb_w % LANES == 0 and ssm_w % LANES == 0 and d % LANES == 0
    assert n_in % LANES == 0 and n_up % LANES == 0 and seq % t_m == 0 and in_cols == 3 * sb_w + ssm_w + 2 * d
    assert (3 * sb_w) % ssm_w == 0 and (3 * sb_w + ssm_w) % d == 0

    bf = lambda a: a.astype(BF16)
    got = _exchange("gather_first", [], [bf(p["w_in"][0]), p["c"]])
    wg_in = [got[0]] + [None] * (depth - 1)
    c_all = got[1].reshape(N_DEV, d)
    small_names = ("w_sb_up", "w_ssm_up", "w_glu", "w_out")
    wg_ffn_in, wg_ffn_out, wg = [None] * depth, [None] * depth, {}

    c_pad = jnp.pad(c_all, ((0, 2 * SUBLANES - N_DEV), (0, 0)))
    c_act = _rowwise("silu_c", lambda v: v * jax.nn.sigmoid(v), [(c_pad, c_pad.shape, lambda i: (0, 0))],
                     [(c_pad.shape, F32, c_pad.shape, lambda i: (0, 0))], (1,))[0]
    rows_c = c_pad.shape[0]
    mod_cols = [
        _mm(f"mod_{l}", c_act, p["w_ada"],
            _spec((rows_c, d), lambda i, j, k: (0, 0)), _spec((None, d, n_ada), lambda i, j, k, l=l: (l, 0, 0)),
            _spec((rows_c, n_ada), lambda i, j, k: (0, 0)), (rows_c, n_ada), F32, (1, 1, 1), NN)
        for l in range(depth)]
    mod_send = jnp.stack([m[:N_DEV] for m in mod_cols], axis=1)
    mod_recv = _exchange("exchange_mod", [mod_send], [])[0]
    mod_nobias = jnp.swapaxes(mod_recv, 0, 1).reshape(depth, N_DEV * n_ada)
    full2 = lambda a: (a, a.shape, lambda i: (0, 0))
    mod = _rowwise("mod_bias", lambda a, b: a + b, [full2(mod_nobias), full2(p["b_ada"])],
                   [(mod_nobias.shape, F32, mod_nobias.shape, lambda i: (0, 0))], (1,))[0]
    vec = lambda a: a.reshape(1, -1)
    mods = [[vec(mod[l, j * d:(j + 1) * d]) for j in range(6)] for l in range(depth)]
    ln = {n: [vec(p[n][l]) for l in range(depth)] for n in ("ln1_g", "ln1_b", "ln2_g", "ln2_b")}

    row_spec = lambda width: ((t_r, width), lambda i: (i, 0))
    col_spec = lambda width, cb: ((t_r, width), lambda i, cb=cb: (i, cb))
    vec_spec = lambda width: ((1, width), lambda i: (0, 0))
    rows_in = lambda a: (a,) + row_spec(a.shape[1])
    vec_in = lambda a: (a,) + vec_spec(a.shape[1])
    row_out = lambda width, dt: ((seq, width), dt) + row_spec(width)

    s5 = [_s5_discretize(*[p[n][l] for n in ("ssm_a_re", "ssm_a_im", "ssm_log_dt", "ssm_b_re", "ssm_b_im",
                                               "ssm_c_re", "ssm_c_im")]) for l in range(depth)]
    s5_b16 = [(bs.astype(BF16), cs.astype(BF16), lam) for bs, cs, lam in s5]
    t_scan = min(512, seq)
    s5_pw = [_s5_powers(p["ssm_a_re"][l], p["ssm_a_im"][l], p["ssm_log_dt"][l], t_scan // SUBLANES)
             for l in range(depth)]
    u_col = 3 * sb_w // LANES
    g_sb_cb, g_ssm_cb = (3 * sb_w + ssm_w) // d, (3 * sb_w + ssm_w) // d + 1
    ssm_d = [vec(p["ssm_d"][l]) for l in range(depth)]
    b_glu = [vec(p["b_glu"][l]) for l in range(depth)]
    n_half = N_DEV // 2

    h = _rowwise("modulate_in", _modulate, [rows_in(x0), vec_in(mods[0][1]), vec_in(mods[0][0])],
                 [row_out(d, BF16)], (n_r,))[0]
    saved = []
    x_cur = x0
    for l in range(depth):
        sv = {"x_in": x_cur, "h": h}
        last = l == depth - 1
        t_n = _tile(n_in)
        r_n = n_in // t_n
        proj = _mm(f"proj_{l}", h, wg_in[l],
                   _spec((t_m, d), lambda i, j, k: (i, 0)),
                   _spec((None, d, t_n), lambda i, j, k, r=r_n: (j // r, 0, j % r)),
                   _spec((t_m, t_n), lambda i, j, k: (i, j)), (seq, in_cols), F32, (n_m, N_DEV * r_n, 1), NN)
        arriving = [bf(p["w_ffn_in"][l]), bf(p["w_ffn_out"][l])] + ([bf(p[n]) for n in small_names] if l == 0 else [])
        (o_sb, o_sb32), got = _sb_attention_fwd(proj, sb_w, beside=_Exchange(gather=arriving))
        wg_ffn_in[l] = got[0]
        wg_ffn_out[l] = got[1].reshape(n_half, n_ffn, d)
        if l == 0:
            wg = dict(zip(small_names, got[2:]))
            for n in ("w_glu", "w_out"):
                wg[n] = jnp.swapaxes(wg[n], 0, 1).reshape(depth, -1, wg[n].shape[-1])
        bs16, cs16, lam = s5_b16[l]
        (yc, states), got = _s5_scan_fwd(proj, u_col, bs16, cs16, lam, s5_pw[l][0], t_scan,
                                         beside=None if last else _Exchange(gather=[bf(p["w_in"][l + 1])]))
        if not last:
            wg_in[l + 1] = got[0]
        u_in = (proj,) + col_spec(ssm_w, 3 * sb_w // ssm_w)
        y1 = _rowwise(f"s5_act_{l}", _s5_act_fn, [rows_in(yc), u_in, vec_in(ssm_d[l])],
                      [row_out(ssm_w, BF16)], (n_r,))[0]
        t_glu = _mm(f"s5_glu_mm_{l}", y1, wg["w_glu"],
                    _spec((t_m, ssm_w), lambda i, j, k: (i, 0)), _spec((None, ssm_w, ssm_w), lambda i, j, k, l=l: (l, 0, 0)),
                    _spec((t_m, ssm_w), lambda i, j, k: (i, 0)), (seq, ssm_w), F32, (n_m, 1, 1), NN)
        s5_out = _rowwise(f"s5_glu_{l}", _s5_glu_fn,
                          [rows_in(yc), u_in, rows_in(t_glu), vec_in(ssm_d[l]), vec_in(b_glu[l])],
                          [row_out(ssm_w, BF16)], (n_r,))[0]

        def up_proj(name, a, w, l=l):
            return _mm(name, a, w, _spec((t_m, a.shape[1]), lambda i, j, k: (i, 0)),
                       _spec((None, None, a.shape[1], n_up), lambda i, j, k: (j, l, 0, 0)),
                       _spec((t_m, n_up), lambda i, j, k: (i, j)), (seq, d), F32, (n_m, N_DEV, 1), NN)

        y_sb = up_proj(f"sb_up_{l}", o_sb, wg["w_sb_up"])
        y_ssm = up_proj(f"ssm_up_{l}", s5_out, wg["w_ssm_up"])
        gates = [(proj,) + col_spec(d, g_sb_cb), (proj,) + col_spec(d, g_ssm_cb)]
        merged = _rowwise(f"merge_{l}", _merge_fn, [rows_in(y_sb), rows_in(y_ssm)] + gates,
                          [row_out(d, BF16)], (n_r,))[0]
        y_mix = _mm(f"out_proj_{l}", merged, wg["w_out"],
                    _spec((t_m, d), lambda i, j, k: (i, 0)), _spec((None, d, t_d), lambda i, j, k, l=l: (l, 0, j)),
                    _spec((t_m, t_d), lambda i, j, k: (i, j)), (seq, d), F32, (n_m, d // t_d, 1), NN)
        vecs_a = [mods[l][2], ln["ln1_g"][l], ln["ln1_b"][l], mods[l][4], mods[l][3]]
        x_mid, h2 = _rowwise(f"resid_mix_{l}", resid_ln_mod, [rows_in(x_cur), rows_in(y_mix)] + [vec_in(v) for v in vecs_a],
                             [row_out(d, F32), row_out(d, BF16)], (n_r,))
        a_ffn = _mm(f"ffn_in_{l}", h2, wg_ffn_in[l],
                    _spec((t_m, d), lambda i, j, k: (i, 0)), _spec((None, d, n_ffn), lambda i, j, k: (j, 0, 0)),
                    _spec((None, t_m, n_ffn), lambda i, j, k: (j, i, 0)), (N_DEV, seq, n_ffn), F32, (n_m, N_DEV, 1), NN)
        blk_in = lambda a, off: (a, (None, t_r, n_ffn), (lambda j, i, off=off: (j + off, i, 0)))
        f_act = _rowwise(f"swiglu_{l}", _swiglu_fn, [blk_in(a_ffn, 0), blk_in(a_ffn, n_half)],
                         [((n_half, seq, n_ffn), BF16, (None, t_r, n_ffn), lambda j, i: (j, i, 0))], (n_half, n_r))[0]
        y_ffn = _mm(f"ffn_out_{l}", f_act, wg_ffn_out[l],
                    _spec((None, t_m, n_ffn), lambda i, j, k: (k, i, 0)),
                    _spec((None, n_ffn, t_d), lambda i, j, k: (k, 0, j)),
                    _spec((t_m, t_d), lambda i, j, k: (i, j)), (seq, d), F32, (n_m, d // t_d, n_half), NN)
        vecs_b = [mods[l][5], ln["ln2_g"][l], ln["ln2_b"][l]] + ([] if last else [mods[l + 1][1], mods[l + 1][0]])
        outs_b = [row_out(d, F32)] + ([] if last else [row_out(d, BF16)])
        res = _rowwise(f"resid_ffn_{l}", resid_ln if last else resid_ln_mod,
                       [rows_in(x_mid), rows_in(y_ffn)] + [vec_in(v) for v in vecs_b], outs_b, (n_r,))
        sv.update(proj=proj, o_sb=o_sb, o_sb32=o_sb32, yc=yc, states=states, y1=y1, t_glu=t_glu, s5_out=s5_out,
                  y_sb=y_sb, y_ssm=y_ssm, merged=merged, y_mix=y_mix, x_mid=x_mid, h2=h2, a_ffn=a_ffn, f_act=f_act,
                  y_ffn=y_ffn, vecs_a=vecs_a, vecs_b=vecs_b)
        saved.append(sv)
        x_cur = res[0]
        h = None if last else res[1]

    loss_part, d_x = _loss_head(x_cur, target, t_r)
    loss = lax.psum(loss_part[0, 0], MESH_AXES)

    d_h_next = None
    grads = {n: [None] * depth for n in WEIGHTS}
    d_mod = [[None] * 6 for _ in range(depth)]
    land = {}
    waiting = []
    row_wrt = lambda i, width, dt: (i, "row", (seq, width), dt) + row_spec(width)
    sum_wrt = lambda i, width: (i, "sum", (1, width), F32) + vec_spec(width)
    for l in reversed(range(depth)):
        sv = saved[l]
        last = l == depth - 1
        ins_b = [rows_in(sv["x_mid"]), rows_in(sv["y_ffn"])] + [vec_in(v) for v in sv["vecs_b"]]
        cts_b = [rows_in(d_x)] + ([] if last else [rows_in(d_h_next)])
        wrt_b = [row_wrt(0, d, F32), row_wrt(1, d, BF16)] + [sum_wrt(2 + j, d) for j in range(len(sv["vecs_b"]))]
        res = _rowwise_vjp(f"resid_ffn_bwd_{l}", resid_ln if last else resid_ln_mod, ins_b, cts_b, wrt_b, (n_r,))
        d_x_mid, d_y_ffn = res[0], res[1]
        d_mod[l][5], grads["ln2_g"][l], grads["ln2_b"][l] = res[2], res[3], res[4]
        if not last:
            d_mod[l + 1][1], d_mod[l + 1][0] = res[5], res[6]
        d_f = _mm(f"ffn_out_dx_{l}", d_y_ffn, wg_ffn_out[l],
                  _spec((t_m, d), lambda i, j, k: (i, 0)), _spec((None, n_ffn, d), lambda i, j, k: (j, 0, 0)),
                  _spec((None, t_m, n_ffn), lambda i, j, k: (j, i, 0)), (n_half, seq, n_ffn), F32, (n_m, n_half, 1), NT)
        g_ffn_out = _mm(f"ffn_out_dw_{l}", sv["f_act"], d_y_ffn,
                        _spec((None, t_m, n_ffn), lambda i, j, k: (i, k, 0)), _spec((t_m, t_d), lambda i, j, k: (k, j)),
                        _spec((None, n_ffn, t_d), lambda i, j, k: (i, 0, j)), (n_half, n_ffn, d), GRAD_WIRE,
                        (n_half, d // t_d, n_m), TN)
        blk_in = lambda a, off: (a, (None, t_r, n_ffn), (lambda j, i, off=off: (j + off, i, 0)))
        d_a_parts = _rowwise_vjp(f"swiglu_bwd_{l}", _swiglu_fn, [blk_in(sv["a_ffn"], 0), blk_in(sv["a_ffn"], n_half)],
                                 [blk_in(d_f, 0)],
                                 [(0, "row", (n_half, seq, n_ffn), BF16, (None, t_r, n_ffn), lambda j, i: (j, i, 0)),
                                  (1, "row", (n_half, seq, n_ffn), BF16, (None, t_r, n_ffn), lambda j, i: (j, i, 0))],
                                 (n_half, n_r))
        d_a = jnp.concatenate(d_a_parts, axis=0)
        d_h2 = _mm(f"ffn_in_dx_{l}", d_a, wg_ffn_in[l],
                   _spec((None, t_m, n_ffn), lambda i, j, k: (k, i, 0)),
                   _spec((None, t_d, n_ffn), lambda i, j, k: (k, j, 0)),
                   _spec((t_m, t_d), lambda i, j, k: (i, j)), (seq, d), F32, (n_m, d // t_d, N_DEV), NT)
        g_ffn_in = _mm(f"ffn_in_dw_{l}", sv["h2"], d_a,
                       _spec((t_m, t_d), lambda i, j, k: (k, j)), _spec((None, t_m, n_ffn), lambda i, j, k: (i, k, 0)),
                       _spec((None, t_d, n_ffn), lambda i, j, k: (i, j, 0)), (N_DEV, d, n_ffn), GRAD_WIRE,
                       (N_DEV, d // t_d, n_m), TN)
        ins_a = [rows_in(sv["x_in"]), rows_in(sv["y_mix"])] + [vec_in(v) for v in sv["vecs_a"]]
        wrt_a = [row_wrt(0, d, F32), row_wrt(1, d, BF16)] + [sum_wrt(2 + j, d) for j in range(5)]
        res = _rowwise_vjp(f"resid_mix_bwd_{l}", resid_ln_mod, ins_a, [rows_in(d_x_mid), rows_in(d_h2)], wrt_a, (n_r,))
        d_x_in, d_y_mix = res[0], res[1]
        d_mod[l][2], grads["ln1_g"][l], grads["ln1_b"][l], d_mod[l][4], d_mod[l][3] = res[2:7]
        d_merged = _mm(f"out_proj_dx_{l}", d_y_mix, wg["w_out"],
                       _spec((t_m, d), lambda i, j, k: (i, 0)), _spec((None, t_d, d), lambda i, j, k, l=l: (l, j, 0)),
                       _spec((t_m, t_d), lambda i, j, k: (i, j)), (seq, d), F32, (n_m, d // t_d, 1), NT)
        g_out = _mm(f"out_proj_dw_{l}", sv["merged"], d_y_mix,
                    _spec((t_m, t_d), lambda i, j, k: (k, i)), _spec((t_m, t_d), lambda i, j, k: (k, j)),
                    _spec((t_d, t_d), lambda i, j, k: (i, j)), (d, d), GRAD_WIRE, (d // t_d, d // t_d, n_m), TN)
        gates = [(sv["proj"],) + col_spec(d, g_sb_cb), (sv["proj"],) + col_spec(d, g_ssm_cb)]
        d_y_sb, d_y_ssm, d_g_sb, d_g_ssm = _rowwise_vjp(
            f"merge_bwd_{l}", _merge_fn, [rows_in(sv["y_sb"]), rows_in(sv["y_ssm"])] + gates, [rows_in(d_merged)],
            [row_wrt(j, d, BF16) for j in range(4)], (n_r,))

        def up_bwd(name, act, d_y, w, dx_dtype, l=l):
            k_w = act.shape[1]
            dx = _mm(name + "_dx", d_y, w, _spec((t_m, n_up), lambda i, j, k: (i, k)),
                     _spec((None, None, k_w, n_up), lambda i, j, k: (k, l, 0, 0)),
                     _spec((t_m, k_w), lambda i, j, k: (i, 0)), (seq, k_w), dx_dtype, (n_m, 1, N_DEV), NT)
            dw = _mm(name + "_dw", act, d_y, _spec((t_m, k_w), lambda i, j, k: (k, 0)),
                     _spec((t_m, n_up), lambda i, j, k: (k, i)),
                     _spec((None, k_w, n_up), lambda i, j, k: (i, 0, 0)), (N_DEV, k_w, n_up), GRAD_WIRE,
                     (N_DEV, 1, n_m), TN)
            return dx, dw

        d_o_sb, g_sb_up = up_bwd(f"sb_up_{l}", sv["o_sb"], d_y_sb, wg["w_sb_up"], BF16)
        d_s5_out, g_ssm_up = up_bwd(f"ssm_up_{l}", sv["s5_out"], d_y_ssm, wg["w_ssm_up"], F32)
        waiting += [("w_ffn_in", g_ffn_in), ("w_ffn_out", g_ffn_out.reshape(N_DEV, -1, d))]
        levels = [l + 1] * (len(waiting) - 2) + [l, l]
        (d_q, d_k, d_v), got = _sb_attention_bwd(
            sv["proj"], sv["o_sb32"], d_o_sb, sb_w,
            beside=_Exchange(layered=[(g, lv, depth, land.get(n)) for (n, g), lv in zip(waiting, levels)]))
        land.update({n: buf for (n, _), buf in zip(waiting, got)})
        u_in = (sv["proj"],) + col_spec(ssm_w, 3 * sb_w // ssm_w)
        ins_s5 = [rows_in(sv["yc"]), u_in, rows_in(sv["t_glu"]), vec_in(ssm_d[l]), vec_in(b_glu[l])]
        d_t = _rowwise_vjp(f"s5_glu_bwd_{l}", _s5_glu_fn, ins_s5, [rows_in(d_s5_out)],
                           [row_wrt(2, ssm_w, BF16)], (n_r,))[0]
        d_y1 = _mm(f"s5_glu_mm_dx_{l}", d_t, wg["w_glu"],
                   _spec((t_m, ssm_w), lambda i, j, k: (i, 0)), _spec((None, ssm_w, ssm_w), lambda i, j, k, l=l: (l, 0, 0)),
                   _spec((t_m, ssm_w), lambda i, j, k: (i, 0)), (seq, ssm_w), F32, (n_m, 1, 1), NT)
        g_glu = _mm(f"s5_glu_mm_dw_{l}", sv["y1"], d_t,
                    _spec((t_m, ssm_w), lambda i, j, k: (k, 0)), _spec((t_m, ssm_w), lambda i, j, k: (k, 0)),
                    _spec((ssm_w, ssm_w), lambda i, j, k: (0, 0)), (ssm_w, ssm_w), GRAD_WIRE, (1, 1, n_m), TN)
        d_yc, d_u_skip, grads["ssm_d"][l], grads["b_glu"][l] = _rowwise_vjp(
            f"s5_post_bwd_{l}", _s5_post_fn, ins_s5, [rows_in(d_y1), rows_in(d_s5_out)],
            [row_wrt(0, ssm_w, F32), row_wrt(1, ssm_w, F32), sum_wrt(3, ssm_w), sum_wrt(4, ssm_w)], (n_r,))
        bs16, cs16, lam = s5_b16[l]
        d_u, d_bs, d_cs, d_lam = _s5_scan_bwd(sv["proj"], u_col, sv["states"], d_yc, d_u_skip, bs16, cs16, lam,
                                              s5_pw[l][1], t_scan)
        raw = [p[n][l] for n in ("ssm_a_re", "ssm_a_im", "ssm_log_dt", "ssm_b_re", "ssm_b_im", "ssm_c_re", "ssm_c_im")]
        _, pull = jax.vjp(_s5_discretize, *raw)
        (grads["ssm_a_re"][l], grads["ssm_a_im"][l], grads["ssm_log_dt"][l], grads["ssm_b_re"][l],
         grads["ssm_b_im"][l], grads["ssm_c_re"][l], grads["ssm_c_im"][l]) = pull((d_bs, d_cs, d_lam))
        d_proj = jnp.concatenate([d_q, d_k.astype(BF16), d_v.astype(BF16), d_u.astype(BF16), d_g_sb, d_g_ssm], axis=1)
        t_n = _tile(n_in)
        d_h = _mm(f"proj_dx_{l}", d_proj, wg_in[l],
                  _spec((t_m, n_in), lambda i, j, k: (i, k)), _spec((None, t_d, n_in), lambda i, j, k: (k, j, 0)),
                  _spec((t_m, t_d), lambda i, j, k: (i, j)), (seq, d), F32, (n_m, d // t_d, N_DEV), NT)
        g_in = _mm(f"proj_dw_{l}", sv["h"], d_proj,
                   _spec((t_m, t_d), lambda i, j, k: (k, j)), _spec((t_m, n_in), lambda i, j, k: (k, i)),
                   _spec((None, t_d, n_in), lambda i, j, k: (i, j, 0)), (N_DEV, d, n_in), GRAD_WIRE,
                   (N_DEV, d // t_d, n_m), TN)
        waiting = [("w_in", g_in), ("w_sb_up", g_sb_up), ("w_ssm_up", g_ssm_up),
                   ("w_glu", g_glu.reshape(N_DEV, -1, ssm_w)), ("w_out", g_out.reshape(N_DEV, -1, d))]
        d_x, d_h_next = d_x_in, d_h
    res = _rowwise_vjp("modulate_in_bwd", lambda v, sc, sh: (v, _modulate(v, sc, sh)),
                       [rows_in(x0), vec_in(mods[0][1]), vec_in(mods[0][0])], [rows_in(d_x), rows_in(d_h_next)],
                       [row_wrt(0, d, F32), sum_wrt(1, d), sum_wrt(2, d)], (n_r,))
    grad_x, d_mod[0][1], d_mod[0][0] = res

    d_mod_rows = jnp.concatenate([jnp.concatenate(d_mod[l], axis=1) for l in range(depth)], axis=0)
    grads["b_ada"] = [d_mod_rows[l] for l in range(depth)]
    small_local = [jnp.stack([g.reshape(p[n].shape[1:]) for g in grads[n]]) for n in SMALL_PARAMS]
    d_mod_send = jnp.swapaxes(d_mod_rows.reshape(depth, N_DEV, n_ada), 0, 1)
    recv = _exchange("exchange_last", [d_mod_send], [_pack(small_local)],
                     layered=[(g, 0, depth, land.get(n)) for n, g in waiting])
    d_mod_cols, small_all = recv[0], recv[-1]
    land.update({n: buf for (n, _), buf in zip(waiting, recv[1:-1])})
    d_mod_pad = jnp.pad(jnp.swapaxes(d_mod_cols, 0, 1), ((0, 0), (0, rows_c - N_DEV), (0, 0)))
    g_ada = [
        _mm(f"mod_dw_{l}", c_act, d_mod_pad,
            _spec((rows_c, d), lambda i, j, k: (0, 0)), _spec((None, rows_c, n_ada), lambda i, j, k, l=l: (l, 0, 0)),
            _spec((d, n_ada), lambda i, j, k: (0, 0)), (d, n_ada), F32, (1, 1, 1), TN)
        for l in range(depth)]

    out = {}

    def update(name, partials):
        shape = p[name].shape
        two_d = lambda a: a.reshape(-1, shape[-1])
        res = _adamw("adamw_" + name, two_d(p[name]), two_d(p["m_" + name]), two_d(p["v_" + name]),
                     partials.reshape(partials.shape[0], -1, shape[-1]))
        out[name] = [r.reshape(shape) for r in res]

    update("w_ada", jnp.stack(g_ada)[None])
    for n in ("w_in", "w_sb_up", "w_ssm_up", "w_ffn_in", "w_glu", "w_out", "w_ffn_out"):
        update(n, land[n])
    small_w = [p[n] for n in SMALL_PARAMS]
    res = _adamw("adamw_small", _pack(small_w), _pack([p["m_" + n] for n in SMALL_PARAMS]),
                 _pack([p["v_" + n] for n in SMALL_PARAMS]), small_all)
    for kind, packed in enumerate(res):
        for n, a in zip(SMALL_PARAMS, _unpack(packed, small_w)):
            out.setdefault(n, [None] * 4)[kind] = a

    return ((loss, grad_x[None]) + tuple(out[n][0] for n in WEIGHTS) + tuple(out[n][1] for n in WEIGHTS)
            + tuple(out[n][2] for n in WEIGHTS) + tuple(out[n][3] for n in WEIGHTS))
```

```python
import jax
import jax.numpy as jnp
from jax import lax
from jax.experimental import pallas as pl
from jax.experimental.pallas import tpu as pltpu

F32 = jnp.float32
BF16 = jnp.bfloat16
GRAD_WIRE = BF16

N_DEV = 8
LANES = 128
SUBLANES = 8
VMEM_BYTES = 64 * 1024 * 1024
HEAD_DIM = 64
SB_BLOCK = 256
SLAB_GROUPS = 8
LN_EPS = 1e-5
ADAM_LR, ADAM_B1, ADAM_B2, ADAM_EPS, ADAM_WD, ADAM_STEP = 0.001, 0.9, 0.999, 1e-08, 0.01, 10
SB_UNDERFLOW = -120.0

PACK_ROWS = 256
MESH_AXES = ("x", "y", "c")


def _vmem_limit(block_bytes):
    return int(min(max(3 * block_bytes + (8 << 20), 24 << 20), VMEM_BYTES - (8 << 20)))


def _nbytes(shape, dtype):
    n = 1
    for d in shape:
        if d is not None:
            n *= d
    return n * jnp.dtype(dtype).itemsize


def _spec(shape, fn):
    return pl.BlockSpec(shape, fn)


class _Exchange:
    def __init__(self, scatter=(), gather=(), layered=()):
        self.arrs = list(scatter) + [a for a, _, _, _ in layered] + list(gather)
        self.n = len(self.arrs)
        self.n_sc = len(scatter) + len(layered)
        self.layer = [None] * len(scatter) + [l for _, l, _, _ in layered] + [None] * len(gather)
        self.shapes = ([a.shape for a in scatter] + [(N_DEV, dp) + a.shape[1:] for a, _, dp, _ in layered]
                       + [(N_DEV,) + a.shape for a in gather])
        self.held = [(len(scatter) + i, b) for i, (_, _, _, b) in enumerate(layered) if b is not None]
        self.operands = self.arrs + [b for _, b in self.held]
        hbm = pl.BlockSpec(memory_space=pltpu.HBM)
        self.in_specs = [hbm] * len(self.operands)
        self.out_specs = [hbm] * self.n
        self.out_shape = [jax.ShapeDtypeStruct(s, a.dtype) for s, a in zip(self.shapes, self.arrs)]
        self.scratch = [pltpu.SemaphoreType.DMA((self.n, N_DEV - 1)), pltpu.SemaphoreType.DMA((self.n, N_DEV - 1)),
                        pltpu.SemaphoreType.DMA((self.n,))]

    def aliases(self, first_in, first_out):
        return {first_in + self.n + i: first_out + a for i, (a, _) in enumerate(self.held)}

    def copies(self, ins, outs, sems):
        send_sems, recv_sems, own_sems = sems
        x, y, c = lax.axis_index("x"), lax.axis_index("y"), lax.axis_index("c")
        me = 4 * x + 2 * y + c
        landing = [outs[a].at[me] if self.layer[a] is None else outs[a].at[me, self.layer[a]] for a in range(self.n)]
        out = [pltpu.make_async_copy(ins[a].at[me] if a < self.n_sc else ins[a], landing[a], own_sems.at[a])
               for a in range(self.n)]
        for k in range(1, N_DEV):
            px = 1 - x if k & 4 else x
            py = 1 - y if k & 2 else y
            pc = 1 - c if k & 1 else c
            peer = 4 * px + 2 * py + pc
            for a in range(self.n):
                out.append(pltpu.make_async_remote_copy(
                    src_ref=ins[a].at[peer] if a < self.n_sc else ins[a], dst_ref=landing[a],
                    send_sem=send_sems.at[a, k - 1], recv_sem=recv_sems.at[a, k - 1],
                    device_id=(px, py, pc), device_id_type=pl.DeviceIdType.MESH))
        return out


def _exchange(name, scatter, gather, layered=()):
    ex = _Exchange(scatter, gather, layered)

    def body(*refs):
        copies = ex.copies(refs[:ex.n], refs[len(ex.operands):len(ex.operands) + ex.n], refs[-3:])
        for cp in copies:
            cp.start()
        for cp in copies:
            cp.wait()

    return pl.pallas_call(body, name=name, in_specs=ex.in_specs, out_specs=ex.out_specs, out_shape=ex.out_shape,
                          input_output_aliases=ex.aliases(0, 0), scratch_shapes=ex.scratch)(*ex.operands)


def _call_beside(ex, body, name, grid, in_specs, out_specs, out_shape, scratch_shapes, vmem_bytes, operands,
                 semantics):
    if ex is None:
        res = pl.pallas_call(
            body, name=name, grid=grid, in_specs=in_specs, out_specs=out_specs, out_shape=out_shape,
            scratch_shapes=scratch_shapes,
            compiler_params=pltpu.CompilerParams(dimension_semantics=semantics, vmem_limit_bytes=vmem_bytes),
        )(*operands)
        return res, None
    n_in, n_out, n_scr = len(in_specs), len(out_specs), len(scratch_shapes)
    n_xin = len(ex.operands)

    def fused(*refs):
        mine = refs[:n_in] + refs[n_in + n_xin:n_in + n_xin + n_out]
        mine += refs[n_in + n_xin + n_out + ex.n:n_in + n_xin + n_out + ex.n + n_scr]
        first = pl.program_id(0) == 0
        last = pl.program_id(0) == grid[0] - 1
        for dim in range(1, len(grid)):
            first = jnp.logical_and(first, pl.program_id(dim) == 0)
            last = jnp.logical_and(last, pl.program_id(dim) == grid[dim] - 1)
        x_ins = refs[n_in:n_in + ex.n]
        x_outs = refs[n_in + n_xin + n_out:n_in + n_xin + n_out + ex.n]

        @pl.when(first)
        def _():
            for cp in ex.copies(x_ins, x_outs, refs[-3:]):
                cp.start()

        body(*mine)

        @pl.when(last)
        def _():
            for cp in ex.copies(x_ins, x_outs, refs[-3:]):
                cp.wait()

    res = pl.pallas_call(
        fused, name=name, grid=grid, in_specs=list(in_specs) + ex.in_specs, out_specs=list(out_specs) + ex.out_specs,
        out_shape=list(out_shape) + ex.out_shape, input_output_aliases=ex.aliases(n_in, n_out),
        scratch_shapes=list(scratch_shapes) + ex.scratch,
        compiler_params=pltpu.CompilerParams(dimension_semantics=("arbitrary",) * len(grid),
                                             vmem_limit_bytes=vmem_bytes),
    )(*operands, *ex.operands)
    return res[:n_out], res[n_out:]


NN = (((1,), (0,)), ((), ()))
NT = (((1,), (1,)), ((), ()))
TN = (((0,), (0,)), ((), ()))


def _mm(name, a, b, a_spec, b_spec, o_spec, o_shape, o_dtype, grid, dims):
    nk = grid[2]
    acc_shape = tuple(d for d in o_spec.block_shape if d is not None)

    def product(a_ref, b_ref):
        return lax.dot_general(a_ref[...].astype(BF16), b_ref[...].astype(BF16), dims, preferred_element_type=F32)

    def body_once(a_ref, b_ref, o_ref):
        o_ref[...] = product(a_ref, b_ref).astype(o_ref.dtype)

    def body(a_ref, b_ref, o_ref, acc_ref):
        k = pl.program_id(2)

        @pl.when(k == 0)
        def _():
            acc_ref[...] = product(a_ref, b_ref)

        @pl.when(k > 0)
        def _():
            acc_ref[...] += product(a_ref, b_ref)

        @pl.when(k == nk - 1)
        def _():
            o_ref[...] = acc_ref[...].astype(o_ref.dtype)

    blk = (_nbytes(a_spec.block_shape, a.dtype) + _nbytes(b_spec.block_shape, b.dtype)
           + _nbytes(acc_shape, o_dtype) + _nbytes(acc_shape, F32))
    return pl.pallas_call(
        body_once if nk == 1 else body, name=name, grid=grid, in_specs=[a_spec, b_spec], out_specs=o_spec,
        out_shape=jax.ShapeDtypeStruct(o_shape, o_dtype),
        scratch_shapes=[] if nk == 1 else [pltpu.VMEM(acc_shape, F32)],
        compiler_params=pltpu.CompilerParams(dimension_semantics=("parallel", "parallel", "arbitrary"),
                                             vmem_limit_bytes=_vmem_limit(blk)),
    )(a, b)


def _tile(n, pref=1024):
    t = pref
    while t >= LANES:
        if n % t == 0:
            return t
        t -= LANES
    return n


def _rowwise(name, fn, ins, outs, grid):
    n_in = len(ins)

    def body(*refs):
        vals = fn(*[r[...].astype(F32) for r in refs[:n_in]])
        if not isinstance(vals, (tuple, list)):
            vals = (vals,)
        for r, v in zip(refs[n_in:], vals):
            r[...] = v.astype(r.dtype)

    blk = sum(_nbytes(bs, a.dtype) for a, bs, _ in ins) + sum(_nbytes(bs, d) + _nbytes(bs, F32) for _, d, bs, _ in outs)
    return pl.pallas_call(
        body, name=name, grid=grid,
        in_specs=[_spec(bs, im) for _, bs, im in ins],
        out_specs=[_spec(bs, im) for _, _, bs, im in outs],
        out_shape=[jax.ShapeDtypeStruct(s, d) for s, d, _, _ in outs],
        compiler_params=pltpu.CompilerParams(dimension_semantics=("parallel",) * len(grid),
                                             vmem_limit_bytes=_vmem_limit(2 * blk)),
    )(*[a for a, _, _ in ins])


def _rowwise_vjp(name, fn, ins, cts, wrt, grid):
    n_in, n_ct = len(ins), len(cts)
    idx = [w[0] for w in wrt]

    def body(*refs):
        prim = [r[...].astype(F32) for r in refs[:n_in]]
        ct = tuple(r[...].astype(F32) for r in refs[n_in:n_in + n_ct])
        o_refs = refs[n_in + n_ct:]

        def g(*sel):
            full = list(prim)
            for i, s in zip(idx, sel):
                full[i] = s
            out = fn(*full)
            return tuple(out) if isinstance(out, (tuple, list)) else (out,)

        _, pull = jax.vjp(g, *[prim[i] for i in idx])
        grads = pull(ct)
        first = pl.program_id(0) == 0
        for d in range(1, len(grid)):
            first = jnp.logical_and(first, pl.program_id(d) == 0)
        for w, o_ref, gr in zip(wrt, o_refs, grads):
            if w[1] == "row":
                o_ref[...] = gr.astype(o_ref.dtype)
            else:
                @pl.when(first)
                def _(o_ref=o_ref):
                    o_ref[...] = jnp.zeros_like(o_ref)

                o_ref[...] += gr.astype(o_ref.dtype)

    blk = (sum(_nbytes(bs, a.dtype) + _nbytes(bs, F32) for a, bs, _ in list(ins) + list(cts))
           + sum(_nbytes(w[4], w[3]) + _nbytes(w[4], F32) for w in wrt))
    return pl.pallas_call(
        body, name=name, grid=grid,
        in_specs=[_spec(bs, im) for _, bs, im in list(ins) + list(cts)],
        out_specs=[_spec(w[4], w[5]) for w in wrt],
        out_shape=[jax.ShapeDtypeStruct(w[2], w[3]) for w in wrt],
        compiler_params=pltpu.CompilerParams(dimension_semantics=("arbitrary",) * len(grid),
                                             vmem_limit_bytes=_vmem_limit(2 * blk)),
    )(*[a for a, _, _ in list(ins) + list(cts)])


def _normalize(x):
    mu = jnp.mean(x, axis=-1, keepdims=True)
    xc = x - mu
    var = jnp.mean(xc * xc, axis=-1, keepdims=True)
    return xc * lax.rsqrt(var + LN_EPS)


def _modulate(x, sc, sh):
    return _normalize(x) * (1.0 + sc) + sh


def _make_resid_fns(alpha):
    def resid_ln(x, y, gate, g, b):
        return _normalize(alpha * x + (1.0 + gate) * y) * g + b

    def resid_ln_mod(x, y, gate, g, b, sc, sh):
        xn = resid_ln(x, y, gate, g, b)
        return xn, _modulate(xn, sc, sh)

    return resid_ln, resid_ln_mod


def _merge_fn(y_sb, y_ssm, gates):
    half = gates.shape[-1] // 2
    return jax.nn.sigmoid(gates[:, :half]) * y_sb + jax.nn.sigmoid(gates[:, half:]) * y_ssm


def _swiglu_fn(gate_up):
    gate, up = gate_up[0], gate_up[1]
    return gate * jax.nn.sigmoid(gate) * up


def _s5_act_fn(yc, u, d_skip):
    return jax.nn.gelu(yc + d_skip * u)


def _s5_glu_fn(yc, u, t, d_skip, b_glu):
    return _s5_act_fn(yc, u, d_skip) * jax.nn.sigmoid(t + b_glu)


def _s5_post_fn(yc, u, t, d_skip, b_glu):
    y1 = _s5_act_fn(yc, u, d_skip)
    return y1, y1 * jax.nn.sigmoid(t + b_glu)


def _sb_tri(kind):
    row = lax.broadcasted_iota(jnp.int32, (SB_BLOCK, SB_BLOCK), 0)
    col = lax.broadcasted_iota(jnp.int32, (SB_BLOCK, SB_BLOCK), 1)
    if kind == "after":
        return (row > col).astype(BF16)
    if kind == "from":
        return (row >= col).astype(BF16)
    return col < row


def _split_dot(x, m):
    hi = x.astype(BF16)
    lo = (x - hi.astype(F32)).astype(BF16)
    return (lax.dot_general(hi, m, NN, preferred_element_type=F32)
            + lax.dot_general(lo, m, NN, preferred_element_type=F32))


def _sb_scores(qh, k2, scale):
    z = lax.dot_general(qh, k2, NT, preferred_element_type=F32) * scale
    sp = jnp.log(1.0 + jnp.exp(-jnp.abs(z)))
    log_beta = jnp.minimum(z, 0.0) - sp
    log_1m = -jnp.maximum(z, 0.0) - sp
    return log_beta, log_1m


def _sb_attention_fwd(proj, sb_width, beside=None):
    seq = proj.shape[0]
    n_pair, n_q = sb_width // LANES, seq // SB_BLOCK
    scale = 1.0 / (HEAD_DIM ** 0.5)

    def body(q_ref, k_ref, v_ref, o_ref, o32_ref):
        qi = pl.program_id(1)
        q2 = q_ref[...]
        lane = lax.broadcasted_iota(jnp.int32, (SB_BLOCK, LANES), 1)
        m_after, causal = _sb_tri("after"), _sb_tri("mask")
        heads = [lane < HEAD_DIM, lane >= HEAD_DIM]
        qh = [jnp.where(m, q2, 0.0).astype(BF16) for m in heads]

        def scores(kb, diag):
            ks = pl.multiple_of(kb * SB_BLOCK, SB_BLOCK)
            k2 = k_ref[pl.ds(ks, SB_BLOCK), :].astype(BF16)
            out = []
            for h in range(2):
                log_beta, log_1m = _sb_scores(qh[h], k2, scale)
                if diag:
                    log_1m = jnp.where(causal, log_1m, 0.0)
                out += [log_beta + _split_dot(log_1m, m_after), jnp.sum(log_1m, axis=1, keepdims=True)]
            return tuple(out)

        def weigh(kb, sc, carry, acc, diag):
            ks = pl.multiple_of(kb * SB_BLOCK, SB_BLOCK)
            v2 = v_ref[pl.ds(ks, SB_BLOCK), :].astype(BF16)
            out = []
            for h in range(2):
                w = jnp.exp(sc[2 * h] + carry[h])
                if diag:
                    w = jnp.where(causal, w, 0.0)
                w_hi = w.astype(BF16)
                w_lo = (w - w_hi.astype(F32)).astype(BF16)
                out += [acc[2 * h] + lax.dot_general(w_hi, v2, NN, preferred_element_type=F32),
                        acc[2 * h + 1] + lax.dot_general(w_lo, v2, NN, preferred_element_type=F32)]
            return tuple(out)

        zero = jnp.zeros((SB_BLOCK, LANES), F32)
        zcol = jnp.zeros((SB_BLOCK, 1), F32)
        sc = scores(qi, True)
        acc = weigh(qi, sc, (zcol, zcol), (zero,) * 4, True)
        carry = (sc[1], sc[3])
        sc = scores(jnp.maximum(qi - 1, 0), False)

        def loop(st):
            kb, sc, carry, acc = st
            after = (carry[0] + sc[1], carry[1] + sc[3])
            done = jnp.maximum(jnp.max(after[0]), jnp.max(after[1])) < SB_UNDERFLOW
            sc_next = scores(jnp.maximum(kb - 1, 0), False)
            acc = weigh(kb, sc, carry, acc, False)
            return jnp.where(done, -1, kb - 1), sc_next, after, acc

        _, _, _, acc = lax.while_loop(lambda st: st[0] >= 0, loop, (qi - 1, sc, carry, acc))
        o_ref[...] = jnp.where(heads[0], acc[0], acc[2]).astype(o_ref.dtype)
        o32_ref[...] = jnp.where(heads[0], acc[0] + acc[1], acc[2] + acc[3])

    q_spec = _spec((SB_BLOCK, LANES), lambda h, i: (i, h))
    kv = [_spec((seq, LANES), lambda h, i, o=o: (0, o + h)) for o in (n_pair, 2 * n_pair)]
    o_spec = _spec((SB_BLOCK, LANES), lambda h, i: (i, h))
    return _call_beside(
        beside, body, "sb_attention_fwd", (n_pair, n_q), [q_spec] + kv, [o_spec, o_spec],
        [jax.ShapeDtypeStruct((seq, sb_width), BF16), jax.ShapeDtypeStruct((seq, sb_width), F32)], [],
        _vmem_limit(2 * seq * LANES * 4), (proj, proj, proj), ("parallel", "arbitrary"))


def _sb_attention_bwd(proj, o32, do, sb_width, beside=None):
    seq = proj.shape[0]
    n_pair, n_q = sb_width // LANES, seq // SB_BLOCK
    scale = 1.0 / (HEAD_DIM ** 0.5)

    def body(q_ref, k_ref, v_ref, o_ref, do_ref, dq_ref, dk_ref, dv_ref):
        qi = pl.program_id(1)

        @pl.when(qi == 0)
        def _():
            dk_ref[...] = jnp.zeros_like(dk_ref)
            dv_ref[...] = jnp.zeros_like(dv_ref)

        q2 = q_ref[...]
        do2 = do_ref[...].astype(F32)
        o2 = o_ref[...]
        lane = lax.broadcasted_iota(jnp.int32, (SB_BLOCK, LANES), 1)
        m_after, m_from, causal = _sb_tri("after"), _sb_tri("from"), _sb_tri("mask")
        heads = [lane < HEAD_DIM, lane >= HEAD_DIM]
        qh = [jnp.where(m, q2, 0.0).astype(BF16) for m in heads]
        doh = [jnp.where(m, do2, 0.0) for m in heads]
        doh_b = [v.astype(BF16) for v in doh]
        total = [jnp.sum(v * o2, axis=1, keepdims=True) for v in doh]

        def scores(kb, diag):
            ks = pl.multiple_of(kb * SB_BLOCK, SB_BLOCK)
            k2 = k_ref[pl.ds(ks, SB_BLOCK), :].astype(BF16)
            v2 = v_ref[pl.ds(ks, SB_BLOCK), :].astype(BF16)
            out = []
            for h in range(2):
                log_beta, log_1m = _sb_scores(qh[h], k2, scale)
                if diag:
                    log_1m = jnp.where(causal, log_1m, 0.0)
                out += [log_beta + _split_dot(log_1m, m_after), jnp.sum(log_1m, axis=1, keepdims=True),
                        lax.dot_general(doh_b[h], v2, NT, preferred_element_type=F32), log_beta]
            return tuple(out)

        def pull(kb, sc, carry, right, dq, diag):
            ks = pl.multiple_of(kb * SB_BLOCK, SB_BLOCK)
            k2 = k_ref[pl.ds(ks, SB_BLOCK), :].astype(BF16)
            dv_blk, dk_blk, right_out, dq_out = None, None, [], []
            for h in range(2):
                arg, _, d_w, log_beta = sc[4 * h:4 * h + 4]
                w = jnp.exp(arg + carry[h])
                if diag:
                    w = jnp.where(causal, w, 0.0)
                d_arg = d_w * w
                dv_h = lax.dot_general(w.astype(BF16), doh_b[h], TN, preferred_element_type=F32)
                d_log_1m = total[h] - right[h] - _split_dot(d_arg, m_from)
                beta = jnp.exp(log_beta)
                dz = d_arg * (1.0 - beta) - beta * d_log_1m
                if diag:
                    dz = jnp.where(causal, dz, 0.0)
                dz_b = (dz * scale).astype(BF16)
                dk_h = lax.dot_general(dz_b, qh[h], TN, preferred_element_type=F32)
                dv_blk = dv_h if h == 0 else dv_blk + dv_h
                dk_blk = dk_h if h == 0 else dk_blk + dk_h
                dq_out.append(dq[h] + lax.dot_general(dz_b, k2, NN, preferred_element_type=F32))
                right_out.append(right[h] + jnp.sum(d_arg, axis=1, keepdims=True))
            dv_ref[pl.ds(ks, SB_BLOCK), :] += dv_blk
            dk_ref[pl.ds(ks, SB_BLOCK), :] += dk_blk
            return tuple(right_out), tuple(dq_out)

        zero = jnp.zeros((SB_BLOCK, LANES), F32)
        zcol = jnp.zeros((SB_BLOCK, 1), F32)
        sc = scores(qi, True)
        right, dq = pull(qi, sc, (zcol, zcol), (zcol, zcol), (zero, zero), True)
        carry = (sc[1], sc[5])
        sc = scores(jnp.maximum(qi - 1, 0), False)

        def loop(st):
            kb, sc, carry, right, dq = st
            after = (carry[0] + sc[1], carry[1] + sc[5])
            done = jnp.maximum(jnp.max(after[0]), jnp.max(after[1])) < SB_UNDERFLOW
            sc_next = scores(jnp.maximum(kb - 1, 0), False)
            right, dq = pull(kb, sc, carry, right, dq, False)
            return jnp.where(done, -1, kb - 1), sc_next, after, right, dq

        _, _, _, _, dq = lax.while_loop(lambda st: st[0] >= 0, loop, (qi - 1, sc, carry, right, dq))
        dq_ref[...] = jnp.where(heads[0], dq[0], dq[1]).astype(dq_ref.dtype)

    q_spec = _spec((SB_BLOCK, LANES), lambda h, i: (i, h))
    kv = [_spec((seq, LANES), lambda h, i, o=o: (0, o + h)) for o in (n_pair, 2 * n_pair)]
    full = _spec((seq, LANES), lambda h, i: (0, h))
    return _call_beside(
        beside, body, "sb_attention_bwd", (n_pair, n_q), [q_spec] + kv + [q_spec, q_spec], [q_spec, full, full],
        [jax.ShapeDtypeStruct((seq, sb_width), BF16), jax.ShapeDtypeStruct((seq, sb_width), F32),
         jax.ShapeDtypeStruct((seq, sb_width), F32)], [],
        _vmem_limit(4 * seq * LANES * 4), (proj, proj, proj, o32, do), ("parallel", "arbitrary"))


def _s5_discretize(a_re, a_im, log_dt, b_re, b_im, c_re, c_im):
    n_g, n_p = a_re.shape
    c_g = b_re.shape[-1]
    ns = n_g // SLAB_GROUPS
    dt = jnp.exp(log_dt)[:, None]
    xr, xi = a_re * dt, a_im * dt
    mag = jnp.exp(xr)
    lr, li = mag * jnp.cos(xi), mag * jnp.sin(xi)
    den = a_re * a_re + a_im * a_im
    fr = ((lr - 1.0) * a_re + li * a_im) / den
    fi = (li * a_re - (lr - 1.0) * a_im) / den
    bb_re = fr[..., None] * b_re - fi[..., None] * b_im
    bb_im = fr[..., None] * b_im + fi[..., None] * b_re
    eye = jnp.eye(SLAB_GROUPS, dtype=F32)

    def diag_b(m):
        m = jnp.transpose(m.reshape(ns, SLAB_GROUPS, n_p, c_g), (0, 1, 3, 2))
        m = m[:, :, :, None, :] * eye[None, :, None, :, None]
        return m.reshape(ns, SLAB_GROUPS * c_g, SLAB_GROUPS * n_p)

    def diag_c(m):
        m = jnp.transpose(m.reshape(ns, SLAB_GROUPS, c_g, n_p), (0, 1, 3, 2))
        m = m[:, :, :, None, :] * eye[None, :, None, :, None]
        return m.reshape(ns, SLAB_GROUPS * n_p, SLAB_GROUPS * c_g)

    bs = jnp.concatenate([diag_b(bb_re), diag_b(bb_im)], axis=-1)
    cs = jnp.concatenate([diag_c(c_re), -diag_c(c_im)], axis=1)
    lam = jnp.concatenate([lr.reshape(ns, 1, -1), li.reshape(ns, 1, -1)], axis=-1)
    return bs, cs, lam


def _s5_powers(a_re, a_im, log_dt, n):
    n_g, n_p = a_re.shape
    ns = n_g // SLAB_GROUPS
    dt = jnp.exp(log_dt)[:, None]
    mag = jnp.exp(a_re * dt)
    base_r, base_i = mag * jnp.cos(a_im * dt), mag * jnp.sin(a_im * dt)
    steps = jnp.arange(1, n + 1, dtype=jnp.int32)[:, None, None]
    pr, pi = jnp.ones((n, n_g, n_p), F32), jnp.zeros((n, n_g, n_p), F32)
    for b in range(n.bit_length()):
        take = ((steps >> b) & 1) == 1
        pr, pi = (jnp.where(take, pr * base_r - pi * base_i, pr), jnp.where(take, pr * base_i + pi * base_r, pi))
        base_r, base_i = base_r * base_r - base_i * base_i, 2.0 * base_r * base_i

    def slabs(re, im):
        one = lambda m: jnp.transpose(m.reshape(n, ns, SLAB_GROUPS * n_p), (1, 0, 2))
        return jnp.concatenate([one(re), one(im)], axis=-1)

    return slabs(pr, pi), slabs(pr[::-1], -pi[::-1])


def _lanes(j):
    return slice(j * LANES, (j + 1) * LANES)


def _tile8(k):
    return pl.ds(pl.multiple_of(k * SUBLANES, SUBLANES), SUBLANES)


def _s5_interleave(dst_ref, src_ref, t_seg):
    def body(k, _):
        dst_ref[_tile8(k), :] = src_ref[pl.ds(k, SUBLANES, stride=t_seg), :]
        return 0

    lax.fori_loop(0, t_seg, body, 0, unroll=4)


def _s5_join_segments(st_ref, end_ref, car_ref, tab_ref, row, order, n_pair):
    for j in range(n_pair):
        re, im = _lanes(j), _lanes(n_pair + j)
        cr, ci = st_ref[:, re], st_ref[:, im]
        tr, ti = tab_ref[row:row + 1, re], tab_ref[row:row + 1, im]
        for s in order:
            car_ref[s:s + 1, re] = cr
            car_ref[s:s + 1, im] = ci
            er, ei = end_ref[s:s + 1, re], end_ref[s:s + 1, im]
            cr, ci = er + tr * cr - ti * ci, ei + tr * ci + ti * cr
        st_ref[:, re] = cr
        st_ref[:, im] = ci


def _s5_add_carries(buf_ref, car_ref, tab_ref, t_seg, n_pair):
    def fix(k, _):
        rows = _tile8(k)
        tab = tab_ref[pl.ds(k, 1), :]
        for j in range(n_pair):
            re, im = _lanes(j), _lanes(n_pair + j)
            cr, ci = car_ref[:, re], car_ref[:, im]
            tr, ti = tab[:, re], tab[:, im]
            buf_ref[rows, re] += tr * cr - ti * ci
            buf_ref[rows, im] += tr * ci + ti * cr
        return 0

    lax.fori_loop(0, t_seg, fix, 0, unroll=2)


def _s5_scan_fwd(proj, u_col, bs, cs, lam, pw, t_blk, beside=None):
    seq = proj.shape[0]
    ns, _, w2 = bs.shape
    n_pair = w2 // (2 * LANES)
    t_seg, n_t = t_blk // SUBLANES, seq // t_blk

    def body(u_ref, bs_ref, cs_ref, lam_ref, pw_ref, yc_ref, h_ref, st_ref, end_ref, car_ref, ui_ref, bu_ref, yi_ref):
        @pl.when(pl.program_id(1) == 0)
        def _():
            st_ref[...] = jnp.zeros_like(st_ref)

        _s5_interleave(ui_ref, u_ref, t_seg)
        bu_ref[...] = lax.dot_general(ui_ref[...].astype(BF16), bs_ref[...], NN, preferred_element_type=F32)
        lam_r = [jnp.broadcast_to(lam_ref[:, _lanes(j)], (SUBLANES, LANES)) for j in range(n_pair)]
        lam_i = [jnp.broadcast_to(lam_ref[:, _lanes(n_pair + j)], (SUBLANES, LANES)) for j in range(n_pair)]

        def step(k, c):
            rows = _tile8(k)
            out = []
            for j in range(n_pair):
                hr, hi = c[2 * j], c[2 * j + 1]
                nr = lam_r[j] * hr - lam_i[j] * hi + bu_ref[rows, _lanes(j)]
                ni = lam_i[j] * hr + lam_r[j] * hi + bu_ref[rows, _lanes(n_pair + j)]
                h_ref[rows, _lanes(j)] = nr
                h_ref[rows, _lanes(n_pair + j)] = ni
                out += [nr, ni]
            return tuple(out)

        ends = lax.fori_loop(0, t_seg, step, (jnp.zeros((SUBLANES, LANES), F32),) * (2 * n_pair), unroll=4)
        for j in range(n_pair):
            end_ref[:, _lanes(j)] = ends[2 * j]
            end_ref[:, _lanes(n_pair + j)] = ends[2 * j + 1]
        _s5_join_segments(st_ref, end_ref, car_ref, pw_ref, t_seg - 1, list(range(SUBLANES)), n_pair)
        _s5_add_carries(h_ref, car_ref, pw_ref, t_seg, n_pair)
        yi_ref[...] = lax.dot_general(h_ref[...].astype(BF16), cs_ref[...], NN, preferred_element_type=F32)

        def scatter(k, _):
            yc_ref[pl.ds(k, SUBLANES, stride=t_seg), :] = yi_ref[_tile8(k), :]
            return 0

        lax.fori_loop(0, t_seg, scatter, 0, unroll=4)

    return _call_beside(
        beside, body, "s5_scan_fwd", (ns, n_t),
        [_spec((t_blk, LANES), lambda s, i: (i, u_col + s)),
         _spec((None, LANES, w2), lambda s, i: (s, 0, 0)),
         _spec((None, w2, LANES), lambda s, i: (s, 0, 0)),
         _spec((None, 1, w2), lambda s, i: (s, 0, 0)),
         _spec((None, t_seg, w2), lambda s, i: (s, 0, 0))],
        [_spec((t_blk, LANES), lambda s, i: (i, s)),
         _spec((None, t_blk, w2), lambda s, i: (s, i, 0))],
        [jax.ShapeDtypeStruct((seq, ns * LANES), F32), jax.ShapeDtypeStruct((ns, seq, w2), F32)],
        [pltpu.VMEM((1, w2), F32), pltpu.VMEM((SUBLANES, w2), F32), pltpu.VMEM((SUBLANES, w2), F32),
         pltpu.VMEM((t_blk, LANES), F32), pltpu.VMEM((t_blk, w2), F32), pltpu.VMEM((t_blk, LANES), F32)],
        _vmem_limit(3 * t_blk * w2 * 4), (proj, bs, cs, lam, pw), ("parallel", "arbitrary"))


def _s5_scan_bwd(proj, u_col, states, d_yc, du_extra, bs, cs, lam, qw, t_blk):
    seq = proj.shape[0]
    ns, _, w2 = bs.shape
    n_pair = w2 // (2 * LANES)
    t_seg, n_t = t_blk // SUBLANES, seq // t_blk

    def body(u_ref, h_ref, hp_ref, dyc_ref, dux_ref, bs_ref, cs_ref, lam_ref, qw_ref,
             du_ref, dbs_ref, dcs_ref, dlam_ref, g_ref, gd_ref, st_ref, end_ref, car_ref, ui_ref, dyi_ref, dui_ref):
        i = pl.program_id(1)

        @pl.when(i == 0)
        def _():
            st_ref[...] = jnp.zeros_like(st_ref)
            dbs_ref[...] = jnp.zeros_like(dbs_ref)
            dcs_ref[...] = jnp.zeros_like(dcs_ref)
            dlam_ref[...] = jnp.zeros_like(dlam_ref)

        _s5_interleave(ui_ref, u_ref, t_seg)
        _s5_interleave(dyi_ref, dyc_ref, t_seg)
        dyc_b = dyi_ref[...].astype(BF16)
        gd_ref[...] = lax.dot_general(dyc_b, cs_ref[...], NT, preferred_element_type=F32)
        lam_r = [jnp.broadcast_to(lam_ref[:, _lanes(j)], (SUBLANES, LANES)) for j in range(n_pair)]
        lam_i = [jnp.broadcast_to(lam_ref[:, _lanes(n_pair + j)], (SUBLANES, LANES)) for j in range(n_pair)]

        def step(kk, c):
            rows = _tile8(t_seg - 1 - kk)
            out = []
            for j in range(n_pair):
                gr_n, gi_n = c[2 * j], c[2 * j + 1]
                gr = gd_ref[rows, _lanes(j)] + lam_r[j] * gr_n + lam_i[j] * gi_n
                gi = gd_ref[rows, _lanes(n_pair + j)] + lam_r[j] * gi_n - lam_i[j] * gr_n
                g_ref[rows, _lanes(j)] = gr
                g_ref[rows, _lanes(n_pair + j)] = gi
                out += [gr, gi]
            return tuple(out)

        zero = jnp.zeros((SUBLANES, LANES), F32)
        firsts = lax.fori_loop(0, t_seg, step, (zero,) * (2 * n_pair), unroll=4)
        for j in range(n_pair):
            end_ref[:, _lanes(j)] = firsts[2 * j]
            end_ref[:, _lanes(n_pair + j)] = firsts[2 * j + 1]
        _s5_join_segments(st_ref, end_ref, car_ref, qw_ref, 0, list(range(SUBLANES))[::-1], n_pair)
        _s5_add_carries(g_ref, car_ref, qw_ref, t_seg, n_pair)

        def pair_up(k, c):
            rows, prev = _tile8(k), _tile8(k - 1)
            out = []
            for j in range(n_pair):
                re, im = _lanes(j), _lanes(n_pair + j)
                gr, gi, hr, hi = g_ref[rows, re], g_ref[rows, im], h_ref[prev, re], h_ref[prev, im]
                out += [c[2 * j] + gr * hr + gi * hi, c[2 * j + 1] + gi * hr - gr * hi]
            return tuple(out)

        acc = lax.fori_loop(1, t_seg, pair_up, (zero,) * (2 * n_pair), unroll=4)
        has_prev = (i < n_t - 1).astype(F32)
        first_seg = lax.broadcasted_iota(jnp.int32, (SUBLANES, LANES), 0) == 0
        last = _tile8(t_seg - 1)
        for j in range(n_pair):
            re, im = _lanes(j), _lanes(n_pair + j)
            gr, gi = g_ref[0:SUBLANES, re], g_ref[0:SUBLANES, im]
            hr = jnp.where(first_seg, hp_ref[SUBLANES - 1:, re] * has_prev, pltpu.roll(h_ref[last, re], 1, 0))
            hi = jnp.where(first_seg, hp_ref[SUBLANES - 1:, im] * has_prev, pltpu.roll(h_ref[last, im], 1, 0))
            dlam_ref[:, re] += jnp.sum(acc[2 * j] + gr * hr + gi * hi, axis=0, keepdims=True)
            dlam_ref[:, im] += jnp.sum(acc[2 * j + 1] + gi * hr - gr * hi, axis=0, keepdims=True)

        g_b = g_ref[...].astype(BF16)
        dui_ref[...] = lax.dot_general(g_b, bs_ref[...], NT, preferred_element_type=F32)
        dbs_ref[...] += lax.dot_general(ui_ref[...].astype(BF16), g_b, TN, preferred_element_type=F32)
        dcs_ref[...] += lax.dot_general(h_ref[...].astype(BF16), dyc_b, TN, preferred_element_type=F32)

        def scatter(k, _):
            rows = pl.ds(k, SUBLANES, stride=t_seg)
            du_ref[rows, :] = (dui_ref[_tile8(k), :] + dux_ref[rows, :]).astype(du_ref.dtype)
            return 0

        lax.fori_loop(0, t_seg, scatter, 0, unroll=4)

    rev = lambda i: n_t - 1 - i
    return pl.pallas_call(
        body, name="s5_scan_bwd", grid=(ns, n_t),
        in_specs=[_spec((t_blk, LANES), lambda s, i: (rev(i), u_col + s)),
                  _spec((None, t_blk, w2), lambda s, i: (s, rev(i), 0)),
                  _spec((None, SUBLANES, w2), lambda s, i: (s, jnp.maximum(rev(i) * t_seg - 1, 0), 0)),
                  _spec((t_blk, LANES), lambda s, i: (rev(i), s)),
                  _spec((t_blk, LANES), lambda s, i: (rev(i), s)),
                  _spec((None, LANES, w2), lambda s, i: (s, 0, 0)),
                  _spec((None, w2, LANES), lambda s, i: (s, 0, 0)),
                  _spec((None, 1, w2), lambda s, i: (s, 0, 0)),
                  _spec((None, t_seg, w2), lambda s, i: (s, 0, 0))],
        out_specs=[_spec((t_blk, LANES), lambda s, i: (rev(i), s)),
                   _spec((None, LANES, w2), lambda s, i: (s, 0, 0)),
                   _spec((None, w2, LANES), lambda s, i: (s, 0, 0)),
                   _spec((None, 1, w2), lambda s, i: (s, 0, 0))],
        out_shape=[jax.ShapeDtypeStruct((seq, ns * LANES), F32), jax.ShapeDtypeStruct(bs.shape, F32),
                   jax.ShapeDtypeStruct(cs.shape, F32), jax.ShapeDtypeStruct(lam.shape, F32)],
        scratch_shapes=[pltpu.VMEM((t_blk, w2), F32), pltpu.VMEM((t_blk, w2), F32), pltpu.VMEM((1, w2), F32),
                        pltpu.VMEM((SUBLANES, w2), F32), pltpu.VMEM((SUBLANES, w2), F32),
                        pltpu.VMEM((t_blk, LANES), F32), pltpu.VMEM((t_blk, LANES), F32), pltpu.VMEM((t_blk, LANES), F32)],
        compiler_params=pltpu.CompilerParams(dimension_semantics=("parallel", "arbitrary"),
                                             vmem_limit_bytes=_vmem_limit(5 * t_blk * w2 * 4)),
    )(proj, states, states, d_yc, du_extra, bs, cs, lam, qw)


def _loss_head(y, target, t_m):
    seq, d = y.shape

    def body(y_ref, t_ref, loss_ref, dy_ref):
        @pl.when(pl.program_id(0) == 0)
        def _():
            loss_ref[...] = jnp.zeros_like(loss_ref)

        diff = y_ref[...] - t_ref[...]
        dy_ref[...] = diff / d
        loss_ref[...] += 0.5 * jnp.sum(diff * diff) / d

    row = _spec((t_m, d), lambda i: (i, 0))
    return pl.pallas_call(
        body, name="loss_head", grid=(seq // t_m,), in_specs=[row, row],
        out_specs=[_spec((SUBLANES, LANES), lambda i: (0, 0)), row],
        out_shape=[jax.ShapeDtypeStruct((SUBLANES, LANES), F32), jax.ShapeDtypeStruct((seq, d), F32)],
        compiler_params=pltpu.CompilerParams(dimension_semantics=("arbitrary",),
                                             vmem_limit_bytes=_vmem_limit(6 * t_m * d * 4)),
    )(y, target)


def _adamw_fn(w, m, v, *partials):
    g = partials[0]
    for p in partials[1:]:
        g = g + p
    m2 = ADAM_B1 * m + (1.0 - ADAM_B1) * g
    v2 = ADAM_B2 * v + (1.0 - ADAM_B2) * (g * g)
    m_hat = m2 / (1.0 - ADAM_B1 ** ADAM_STEP)
    v_hat = v2 / (1.0 - ADAM_B2 ** ADAM_STEP)
    delta = -ADAM_LR * (m_hat / (jnp.sqrt(v_hat) + ADAM_EPS) + ADAM_WD * w)
    return g, delta, m2, v2


def _adamw(name, w, m, v, partials):
    rows, cols = w.shape
    t_r = rows
    for cand in (512, 256, 128, 64, 32, 16, 8):
        if rows % cand == 0 and cand * cols * 4 <= (1 << 20):
            t_r = cand
            break
    n_p = partials.shape[0]
    row = lambda i: (i, 0)
    ins = [(a, (t_r, cols), row) for a in (w, m, v)]
    ins += [(partials, (None, t_r, cols), (lambda i, j=j: (j, i, 0))) for j in range(n_p)]
    outs = [((rows, cols), F32, (t_r, cols), row)] * 4
    return _rowwise(name, _adamw_fn, ins, outs, (rows // t_r,))


SMALL_PARAMS = ("b_ada", "ssm_a_re", "ssm_a_im", "ssm_log_dt", "ssm_b_re", "ssm_b_im", "ssm_c_re", "ssm_c_im",
                "ssm_d", "b_glu", "ln1_g", "ln1_b", "ln2_g", "ln2_b")
WEIGHTS = ("w_ada", "b_ada", "w_in", "w_sb_up", "ssm_a_re", "ssm_a_im", "ssm_log_dt", "ssm_b_re", "ssm_b_im",
           "ssm_c_re", "ssm_c_im", "ssm_d", "w_glu", "b_glu", "w_ssm_up", "w_out", "ln1_g", "ln1_b", "w_ffn_in",
           "w_ffn_out", "ln2_g", "ln2_b")
ARG_NAMES = (("x", "c") + WEIGHTS + ("loss_target",) + tuple("m_" + n for n in WEIGHTS)
             + tuple("v_" + n for n in WEIGHTS))


def _pack(arrs):
    flat = jnp.concatenate([a.reshape(-1) for a in arrs])
    pad = (-flat.shape[0]) % (PACK_ROWS * LANES)
    return jnp.pad(flat, (0, pad)).reshape(-1, LANES)


def _unpack(packed, like):
    lead = packed.shape[:-2]
    flat = packed.reshape(lead + (-1,))
    out, off = [], 0
    for a in like:
        out.append(flat[..., off:off + a.size].reshape(lead + a.shape))
        off += a.size
    return out


def kernel(x, c, w_ada, b_ada, w_in, w_sb_up, ssm_a_re, ssm_a_im, ssm_log_dt, ssm_b_re, ssm_b_im, ssm_c_re,
           ssm_c_im, ssm_d, w_glu, b_glu, w_ssm_up, w_out, ln1_g, ln1_b, w_ffn_in, w_ffn_out, ln2_g, ln2_b,
           loss_target, m_w_ada, m_b_ada, m_w_in, m_w_sb_up, m_ssm_a_re, m_ssm_a_im, m_ssm_log_dt, m_ssm_b_re,
           m_ssm_b_im, m_ssm_c_re, m_ssm_c_im, m_ssm_d, m_w_glu, m_b_glu, m_w_ssm_up, m_w_out, m_ln1_g, m_ln1_b,
           m_w_ffn_in, m_w_ffn_out, m_ln2_g, m_ln2_b, v_w_ada, v_b_ada, v_w_in, v_w_sb_up, v_ssm_a_re, v_ssm_a_im,
           v_ssm_log_dt, v_ssm_b_re, v_ssm_b_im, v_ssm_c_re, v_ssm_c_im, v_ssm_d, v_w_glu, v_b_glu, v_w_ssm_up,
           v_w_out, v_ln1_g, v_ln1_b, v_w_ffn_in, v_w_ffn_out, v_ln2_g, v_ln2_b):
    given = locals()
    return _train_step({n: given[n] for n in ARG_NAMES})


def _train_step(p):
    x0 = p["x"][0]
    target = p["loss_target"][0]
    seq, d = x0.shape
    depth = p["w_ada"].shape[0]
    n_ada = p["w_ada"].shape[2]
    n_in = p["w_in"].shape[2]
    sb_w = p["w_sb_up"].shape[1]
    ssm_w = p["w_ssm_up"].shape[1]
    n_up = p["w_sb_up"].shape[2]
    n_ffn = p["w_ffn_in"].shape[2]
    ffn = N_DEV * p["w_ffn_out"].shape[1]
    in_cols = N_DEV * n_in
    alpha = (2 * depth) ** 0.25
    resid_ln, resid_ln_mod = _make_resid_fns(alpha)
    t_r = min(512, seq)
    n_r = seq // t_r
    t_m = min(1024, seq)
    n_m = seq // t_m
    t_d = _tile(d)
    assert n_ffn * (N_DEV // 2) == ffn and sb_w % LANES == 0 and ssm_w % LANES == 0 and d % LANES == 0
    assert n_in % LANES == 0 and n_up % LANES == 0 and seq % t_m == 0 and in_cols == 3 * sb_w + ssm_w + 2 * d
    assert (3 * sb_w) % ssm_w == 0 and (3 * sb_w + ssm_w) % (2 * d) == 0

    bf = lambda a: a.astype(BF16)
    got = _exchange("gather_first", [], [bf(p["w_in"][0]), p["c"]])
    wg_in = [got[0]] + [None] * (depth - 1)
    c_all = got[1].reshape(N_DEV, d)
    small_names = ("w_sb_up", "w_ssm_up", "w_glu", "w_out")
    wg_ffn_in, wg_ffn_out, wg = [None] * depth, [None] * depth, {}

    c_pad = jnp.pad(c_all, ((0, 2 * SUBLANES - N_DEV), (0, 0)))
    c_act = _rowwise("silu_c", lambda v: v * jax.nn.sigmoid(v), [(c_pad, c_pad.shape, lambda i: (0, 0))],
                     [(c_pad.shape, F32, c_pad.shape, lambda i: (0, 0))], (1,))[0]
    rows_c = c_pad.shape[0]
    mod_cols = [
        _mm(f"mod_{l}", c_act, p["w_ada"],
            _spec((rows_c, d), lambda i, j, k: (0, 0)), _spec((None, d, n_ada), lambda i, j, k, l=l: (l, 0, 0)),
            _spec((rows_c, n_ada), lambda i, j, k: (0, 0)), (rows_c, n_ada), F32, (1, 1, 1), NN)
        for l in range(depth)]
    mod_send = jnp.stack([m[:N_DEV] for m in mod_cols], axis=1)
    mod_recv = _exchange("exchange_mod", [mod_send], [])[0]
    mod_nobias = jnp.swapaxes(mod_recv, 0, 1).reshape(depth, N_DEV * n_ada)
    full2 = lambda a: (a, a.shape, lambda i: (0, 0))
    mod = _rowwise("mod_bias", lambda a, b: a + b, [full2(mod_nobias), full2(p["b_ada"])],
                   [(mod_nobias.shape, F32, mod_nobias.shape, lambda i: (0, 0))], (1,))[0]
    vec = lambda a: a.reshape(1, -1)
    mods = [[vec(mod[l, j * d:(j + 1) * d]) for j in range(6)] for l in range(depth)]
    ln = {n: [vec(p[n][l]) for l in range(depth)] for n in ("ln1_g", "ln1_b", "ln2_g", "ln2_b")}

    row_spec = lambda width: ((t_r, width), lambda i: (i, 0))
    col_spec = lambda width, cb: ((t_r, width), lambda i, cb=cb: (i, cb))
    vec_spec = lambda width: ((1, width), lambda i: (0, 0))
    rows_in = lambda a: (a,) + row_spec(a.shape[1])
    vec_in = lambda a: (a,) + vec_spec(a.shape[1])
    row_out = lambda width, dt: ((seq, width), dt) + row_spec(width)

    s5 = [_s5_discretize(*[p[n][l] for n in ("ssm_a_re", "ssm_a_im", "ssm_log_dt", "ssm_b_re", "ssm_b_im",
                                               "ssm_c_re", "ssm_c_im")]) for l in range(depth)]
    s5_b16 = [(bs.astype(BF16), cs.astype(BF16), lam) for bs, cs, lam in s5]
    t_scan = min(512, seq)
    s5_pw = [_s5_powers(p["ssm_a_re"][l], p["ssm_a_im"][l], p["ssm_log_dt"][l], t_scan // SUBLANES)
             for l in range(depth)]
    u_col = 3 * sb_w // LANES
    gates_cb = (3 * sb_w + ssm_w) // (2 * d)
    ssm_d = [vec(p["ssm_d"][l]) for l in range(depth)]
    b_glu = [vec(p["b_glu"][l]) for l in range(depth)]
    n_half = N_DEV // 2

    h = _rowwise("modulate_in", _modulate, [rows_in(x0), vec_in(mods[0][1]), vec_in(mods[0][0])],
                 [row_out(d, BF16)], (n_r,))[0]
    saved = []
    x_cur = x0
    for l in range(depth):
        sv = {"x_in": x_cur, "h": h}
        last = l == depth - 1
        t_n = _tile(n_in)
        r_n = n_in // t_n
        proj = _mm(f"proj_{l}", h, wg_in[l],
                   _spec((t_m, d), lambda i, j, k: (i, 0)),
                   _spec((None, d, t_n), lambda i, j, k, r=r_n: (j // r, 0, j % r)),
                   _spec((t_m, t_n), lambda i, j, k: (i, j)), (seq, in_cols), F32, (n_m, N_DEV * r_n, 1), NN)
        arriving = [bf(p["w_ffn_in"][l]), bf(p["w_ffn_out"][l])] + ([bf(p[n]) for n in small_names] if l == 0 else [])
        (o_sb, o_sb32), got = _sb_attention_fwd(proj, sb_w, beside=_Exchange(gather=arriving))
        wg_ffn_in[l] = got[0]
        wg_ffn_out[l] = got[1].reshape(n_half, n_ffn, d)
        if l == 0:
            wg = dict(zip(small_names, got[2:]))
            for n in ("w_glu", "w_out"):
                wg[n] = jnp.swapaxes(wg[n], 0, 1).reshape(depth, -1, wg[n].shape[-1])
            for n in ("w_sb_up", "w_ssm_up"):
                wg[n] = jnp.transpose(wg[n], (1, 2, 0, 3)).reshape(depth, wg[n].shape[2], d)
        bs16, cs16, lam = s5_b16[l]
        (yc, states), got = _s5_scan_fwd(proj, u_col, bs16, cs16, lam, s5_pw[l][0], t_scan,
                                         beside=None if last else _Exchange(gather=[bf(p["w_in"][l + 1])]))
        if not last:
            wg_in[l + 1] = got[0]
        u_in = (proj,) + col_spec(ssm_w, 3 * sb_w // ssm_w)
        y1 = _rowwise(f"s5_act_{l}", _s5_act_fn, [rows_in(yc), u_in, vec_in(ssm_d[l])],
                      [row_out(ssm_w, BF16)], (n_r,))[0]
        t_glu = _mm(f"s5_glu_mm_{l}", y1, wg["w_glu"],
                    _spec((t_m, ssm_w), lambda i, j, k: (i, 0)), _spec((None, ssm_w, ssm_w), lambda i, j, k, l=l: (l, 0, 0)),
                    _spec((t_m, ssm_w), lambda i, j, k: (i, 0)), (seq, ssm_w), F32, (n_m, 1, 1), NN)
        s5_out = _rowwise(f"s5_glu_{l}", _s5_glu_fn,
                          [rows_in(yc), u_in, rows_in(t_glu), vec_in(ssm_d[l]), vec_in(b_glu[l])],
                          [row_out(ssm_w, BF16)], (n_r,))[0]

        def up_proj(name, a, w, l=l):
            return _mm(name, a, w, _spec((t_m, a.shape[1]), lambda i, j, k: (i, 0)),
                       _spec((None, a.shape[1], t_d), lambda i, j, k: (l, 0, j)),
                       _spec((t_m, t_d), lambda i, j, k: (i, j)), (seq, d), F32, (n_m, d // t_d, 1), NN)

        y_sb = up_proj(f"sb_up_{l}", o_sb, wg["w_sb_up"])
        y_ssm = up_proj(f"ssm_up_{l}", s5_out, wg["w_ssm_up"])
        gates = (proj,) + col_spec(2 * d, gates_cb)
        merged = _rowwise(f"merge_{l}", _merge_fn, [rows_in(y_sb), rows_in(y_ssm), gates],
                          [row_out(d, BF16)], (n_r,))[0]
        y_mix = _mm(f"out_proj_{l}", merged, wg["w_out"],
                    _spec((t_m, d), lambda i, j, k: (i, 0)), _spec((None, d, t_d), lambda i, j, k, l=l: (l, 0, j)),
                    _spec((t_m, t_d), lambda i, j, k: (i, j)), (seq, d), F32, (n_m, d // t_d, 1), NN)
        vecs_a = [mods[l][2], ln["ln1_g"][l], ln["ln1_b"][l], mods[l][4], mods[l][3]]
        x_mid, h2 = _rowwise(f"resid_mix_{l}", resid_ln_mod, [rows_in(x_cur), rows_in(y_mix)] + [vec_in(v) for v in vecs_a],
                             [row_out(d, F32), row_out(d, BF16)], (n_r,))
        a_ffn = _mm(f"ffn_in_{l}", h2, wg_ffn_in[l],
                    _spec((t_m, d), lambda i, j, k: (i, 0)), _spec((None, d, n_ffn), lambda i, j, k: (j, 0, 0)),
                    _spec((None, t_m, n_ffn), lambda i, j, k: (j, i, 0)), (N_DEV, seq, n_ffn), F32, (n_m, N_DEV, 1), NN)
        pair_in = lambda a: (a.reshape(2, n_half, seq, n_ffn), (2, None, t_r, n_ffn), lambda j, i: (0, j, i, 0))
        f_act = _rowwise(f"swiglu_{l}", _swiglu_fn, [pair_in(a_ffn)],
                         [((n_half, seq, n_ffn), BF16, (None, t_r, n_ffn), lambda j, i: (j, i, 0))], (n_half, n_r))[0]
        y_ffn = _mm(f"ffn_out_{l}", f_act, wg_ffn_out[l],
                    _spec((None, t_m, n_ffn), lambda i, j, k: (k, i, 0)),
                    _spec((None, n_ffn, t_d), lambda i, j, k: (k, 0, j)),
                    _spec((t_m, t_d), lambda i, j, k: (i, j)), (seq, d), F32, (n_m, d // t_d, n_half), NN)
        vecs_b = [mods[l][5], ln["ln2_g"][l], ln["ln2_b"][l]] + ([] if last else [mods[l + 1][1], mods[l + 1][0]])
        outs_b = [row_out(d, F32)] + ([] if last else [row_out(d, BF16)])
        res = _rowwise(f"resid_ffn_{l}", resid_ln if last else resid_ln_mod,
                       [rows_in(x_mid), rows_in(y_ffn)] + [vec_in(v) for v in vecs_b], outs_b, (n_r,))
        sv.update(proj=proj, o_sb=o_sb, o_sb32=o_sb32, yc=yc, states=states, y1=y1, t_glu=t_glu, s5_out=s5_out,
                  y_sb=y_sb, y_ssm=y_ssm, merged=merged, y_mix=y_mix, x_mid=x_mid, h2=h2, a_ffn=a_ffn, f_act=f_act,
                  y_ffn=y_ffn, vecs_a=vecs_a, vecs_b=vecs_b)
        saved.append(sv)
        x_cur = res[0]
        h = None if last else res[1]

    loss_part, d_x = _loss_head(x_cur, target, t_r)
    loss = lax.psum(loss_part[0, 0], MESH_AXES)

    d_h_next = None
    grads = {n: [None] * depth for n in WEIGHTS}
    d_mod = [[None] * 6 for _ in range(depth)]
    land = {}
    waiting = []
    row_wrt = lambda i, width, dt: (i, "row", (seq, width), dt) + row_spec(width)
    sum_wrt = lambda i, width: (i, "sum", (1, width), F32) + vec_spec(width)
    for l in reversed(range(depth)):
        sv = saved[l]
        last = l == depth - 1
        ins_b = [rows_in(sv["x_mid"]), rows_in(sv["y_ffn"])] + [vec_in(v) for v in sv["vecs_b"]]
        cts_b = [rows_in(d_x)] + ([] if last else [rows_in(d_h_next)])
        wrt_b = [row_wrt(0, d, F32), row_wrt(1, d, BF16)] + [sum_wrt(2 + j, d) for j in range(len(sv["vecs_b"]))]
        res = _rowwise_vjp(f"resid_ffn_bwd_{l}", resid_ln if last else resid_ln_mod, ins_b, cts_b, wrt_b, (n_r,))
        d_x_mid, d_y_ffn = res[0], res[1]
        d_mod[l][5], grads["ln2_g"][l], grads["ln2_b"][l] = res[2], res[3], res[4]
        if not last:
            d_mod[l + 1][1], d_mod[l + 1][0] = res[5], res[6]
        d_f = _mm(f"ffn_out_dx_{l}", d_y_ffn, wg_ffn_out[l],
                  _spec((t_m, d), lambda i, j, k: (i, 0)), _spec((None, n_ffn, d), lambda i, j, k: (j, 0, 0)),
                  _spec((None, t_m, n_ffn), lambda i, j, k: (j, i, 0)), (n_half, seq, n_ffn), F32, (n_m, n_half, 1), NT)
        g_ffn_out = _mm(f"ffn_out_dw_{l}", sv["f_act"], d_y_ffn,
                        _spec((None, t_m, n_ffn), lambda i, j, k: (i, k, 0)), _spec((t_m, t_d), lambda i, j, k: (k, j)),
                        _spec((None, n_ffn, t_d), lambda i, j, k: (i, 0, j)), (n_half, n_ffn, d), GRAD_WIRE,
                        (n_half, d // t_d, n_m), TN)
        pair_in = lambda a: (a.reshape(2, n_half, seq, n_ffn), (2, None, t_r, n_ffn), lambda j, i: (0, j, i, 0))
        d_a = _rowwise_vjp(f"swiglu_bwd_{l}", _swiglu_fn, [pair_in(sv["a_ffn"])],
                           [(d_f, (None, t_r, n_ffn), lambda j, i: (j, i, 0))],
                           [(0, "row", (2, n_half, seq, n_ffn), BF16, (2, None, t_r, n_ffn), lambda j, i: (0, j, i, 0))],
                           (n_half, n_r))[0].reshape(N_DEV, seq, n_ffn)
        d_h2 = _mm(f"ffn_in_dx_{l}", d_a, wg_ffn_in[l],
                   _spec((None, t_m, n_ffn), lambda i, j, k: (k, i, 0)),
                   _spec((None, t_d, n_ffn), lambda i, j, k: (k, j, 0)),
                   _spec((t_m, t_d), lambda i, j, k: (i, j)), (seq, d), F32, (n_m, d // t_d, N_DEV), NT)
        g_ffn_in = _mm(f"ffn_in_dw_{l}", sv["h2"], d_a,
                       _spec((t_m, t_d), lambda i, j, k: (k, j)), _spec((None, t_m, n_ffn), lambda i, j, k: (i, k, 0)),
                       _spec((None, t_d, n_ffn), lambda i, j, k: (i, j, 0)), (N_DEV, d, n_ffn), GRAD_WIRE,
                       (N_DEV, d // t_d, n_m), TN)
        ins_a = [rows_in(sv["x_in"]), rows_in(sv["y_mix"])] + [vec_in(v) for v in sv["vecs_a"]]
        wrt_a = [row_wrt(0, d, F32), row_wrt(1, d, BF16)] + [sum_wrt(2 + j, d) for j in range(5)]
        res = _rowwise_vjp(f"resid_mix_bwd_{l}", resid_ln_mod, ins_a, [rows_in(d_x_mid), rows_in(d_h2)], wrt_a, (n_r,))
        d_x_in, d_y_mix = res[0], res[1]
        d_mod[l][2], grads["ln1_g"][l], grads["ln1_b"][l], d_mod[l][4], d_mod[l][3] = res[2:7]
        d_merged = _mm(f"out_proj_dx_{l}", d_y_mix, wg["w_out"],
                       _spec((t_m, d), lambda i, j, k: (i, 0)), _spec((None, t_d, d), lambda i, j, k, l=l: (l, j, 0)),
                       _spec((t_m, t_d), lambda i, j, k: (i, j)), (seq, d), F32, (n_m, d // t_d, 1), NT)
        g_out = _mm(f"out_proj_dw_{l}", sv["merged"], d_y_mix,
                    _spec((t_m, t_d), lambda i, j, k: (k, i)), _spec((t_m, t_d), lambda i, j, k: (k, j)),
                    _spec((t_d, t_d), lambda i, j, k: (i, j)), (d, d), GRAD_WIRE, (d // t_d, d // t_d, n_m), TN)
        gates = (sv["proj"],) + col_spec(2 * d, gates_cb)
        d_y_sb, d_y_ssm, d_gates = _rowwise_vjp(
            f"merge_bwd_{l}", _merge_fn, [rows_in(sv["y_sb"]), rows_in(sv["y_ssm"]), gates], [rows_in(d_merged)],
            [row_wrt(0, d, BF16), row_wrt(1, d, BF16), row_wrt(2, 2 * d, BF16)], (n_r,))

        def up_bwd(name, act, d_y, w, dx_dtype, l=l):
            k_w = act.shape[1]
            dx = _mm(name + "_dx", d_y, w, _spec((t_m, d), lambda i, j, k: (i, 0)),
                     _spec((None, k_w, d), lambda i, j, k: (l, 0, 0)),
                     _spec((t_m, k_w), lambda i, j, k: (i, 0)), (seq, k_w), dx_dtype, (n_m, 1, 1), NT)
            dw = _mm(name + "_dw", act, d_y, _spec((t_m, k_w), lambda i, j, k: (k, 0)),
                     _spec((t_m, t_d), lambda i, j, k: (k, j)),
                     _spec((k_w, t_d), lambda i, j, k: (0, j)), (k_w, d), GRAD_WIRE, (1, d // t_d, n_m), TN)
            return dx, jnp.swapaxes(dw.reshape(k_w, N_DEV, n_up), 0, 1)

        d_o_sb, g_sb_up = up_bwd(f"sb_up_{l}", sv["o_sb"], d_y_sb, wg["w_sb_up"], BF16)
        d_s5_out, g_ssm_up = up_bwd(f"ssm_up_{l}", sv["s5_out"], d_y_ssm, wg["w_ssm_up"], F32)
        waiting += [("w_ffn_in", g_ffn_in), ("w_ffn_out", g_ffn_out.reshape(N_DEV, -1, d)),
                    ("w_out", g_out.reshape(N_DEV, -1, d)), ("w_sb_up", g_sb_up), ("w_ssm_up", g_ssm_up)]
        levels = [l + 1] * (len(waiting) - 5) + [l] * 5
        (d_q, d_k, d_v), got = _sb_attention_bwd(
            sv["proj"], sv["o_sb32"], d_o_sb, sb_w,
            beside=_Exchange(layered=[(g, lv, depth, land.get(n)) for (n, g), lv in zip(waiting, levels)]))
        land.update({n: buf for (n, _), buf in zip(waiting, got)})
        u_in = (sv["proj"],) + col_spec(ssm_w, 3 * sb_w // ssm_w)
        ins_s5 = [rows_in(sv["yc"]), u_in, rows_in(sv["t_glu"]), vec_in(ssm_d[l]), vec_in(b_glu[l])]
        d_t = _rowwise_vjp(f"s5_glu_bwd_{l}", _s5_glu_fn, ins_s5, [rows_in(d_s5_out)],
                           [row_wrt(2, ssm_w, BF16)], (n_r,))[0]
        d_y1 = _mm(f"s5_glu_mm_dx_{l}", d_t, wg["w_glu"],
                   _spec((t_m, ssm_w), lambda i, j, k: (i, 0)), _spec((None, ssm_w, ssm_w), lambda i, j, k, l=l: (l, 0, 0)),
                   _spec((t_m, ssm_w), lambda i, j, k: (i, 0)), (seq, ssm_w), F32, (n_m, 1, 1), NT)
        g_glu = _mm(f"s5_glu_mm_dw_{l}", sv["y1"], d_t,
                    _spec((t_m, ssm_w), lambda i, j, k: (k, 0)), _spec((t_m, ssm_w), lambda i, j, k: (k, 0)),
                    _spec((ssm_w, ssm_w), lambda i, j, k: (0, 0)), (ssm_w, ssm_w), GRAD_WIRE, (1, 1, n_m), TN)
        d_yc, d_u_skip, grads["ssm_d"][l], grads["b_glu"][l] = _rowwise_vjp(
            f"s5_post_bwd_{l}", _s5_post_fn, ins_s5, [rows_in(d_y1), rows_in(d_s5_out)],
            [row_wrt(0, ssm_w, F32), row_wrt(1, ssm_w, F32), sum_wrt(3, ssm_w), sum_wrt(4, ssm_w)], (n_r,))
        bs16, cs16, lam = s5_b16[l]
        d_u, d_bs, d_cs, d_lam = _s5_scan_bwd(sv["proj"], u_col, sv["states"], d_yc, d_u_skip, bs16, cs16, lam,
                                              s5_pw[l][1], t_scan)
        raw = [p[n][l] for n in ("ssm_a_re", "ssm_a_im", "ssm_log_dt", "ssm_b_re", "ssm_b_im", "ssm_c_re", "ssm_c_im")]
        _, pull = jax.vjp(_s5_discretize, *raw)
        (grads["ssm_a_re"][l], grads["ssm_a_im"][l], grads["ssm_log_dt"][l], grads["ssm_b_re"][l],
         grads["ssm_b_im"][l], grads["ssm_c_re"][l], grads["ssm_c_im"][l]) = pull((d_bs, d_cs, d_lam))
        d_proj = jnp.concatenate([d_q, d_k.astype(BF16), d_v.astype(BF16), d_u.astype(BF16), d_gates], axis=1)
        t_n = _tile(n_in)
        d_h = _mm(f"proj_dx_{l}", d_proj, wg_in[l],
                  _spec((t_m, n_in), lambda i, j, k: (i, k)), _spec((None, t_d, n_in), lambda i, j, k: (k, j, 0)),
                  _spec((t_m, t_d), lambda i, j, k: (i, j)), (seq, d), F32, (n_m, d // t_d, N_DEV), NT)
        g_in = _mm(f"proj_dw_{l}", sv["h"], d_proj,
                   _spec((t_m, t_d), lambda i, j, k: (k, j)), _spec((t_m, n_in), lambda i, j, k: (k, i)),
                   _spec((None, t_d, n_in), lambda i, j, k: (i, j, 0)), (N_DEV, d, n_in), GRAD_WIRE,
                   (N_DEV, d // t_d, n_m), TN)
        waiting = [("w_in", g_in), ("w_glu", g_glu.reshape(N_DEV, -1, ssm_w))]
        d_x, d_h_next = d_x_in, d_h
    res = _rowwise_vjp("modulate_in_bwd", lambda v, sc, sh: (v, _modulate(v, sc, sh)),
                       [rows_in(x0), vec_in(mods[0][1]), vec_in(mods[0][0])], [rows_in(d_x), rows_in(d_h_next)],
                       [row_wrt(0, d, F32), sum_wrt(1, d), sum_wrt(2, d)], (n_r,))
    grad_x, d_mod[0][1], d_mod[0][0] = res

    d_mod_rows = jnp.concatenate([jnp.concatenate(d_mod[l], axis=1) for l in range(depth)], axis=0)
    grads["b_ada"] = [d_mod_rows[l] for l in range(depth)]
    small_local = [jnp.stack([g.reshape(p[n].shape[1:]) for g in grads[n]]) for n in SMALL_PARAMS]
    d_mod_send = jnp.swapaxes(d_mod_rows.reshape(depth, N_DEV, n_ada), 0, 1)
    recv = _exchange("exchange_last", [d_mod_send], [_pack(small_local)],
                     layered=[(g, 0, depth, land.get(n)) for n, g in waiting])
    d_mod_cols, small_all = recv[0], recv[-1]
    land.update({n: buf for (n, _), buf in zip(waiting, recv[1:-1])})
    d_mod_pad = jnp.pad(jnp.swapaxes(d_mod_cols, 0, 1), ((0, 0), (0, rows_c - N_DEV), (0, 0)))
    g_ada = [
        _mm(f"mod_dw_{l}", c_act, d_mod_pad,
            _spec((rows_c, d), lambda i, j, k: (0, 0)), _spec((None, rows_c, n_ada), lambda i, j, k, l=l: (l, 0, 0)),
            _spec((d, n_ada), lambda i, j, k: (0, 0)), (d, n_ada), F32, (1, 1, 1), TN)
        for l in range(depth)]

    out = {}

    def update(name, partials):
        shape = p[name].shape
        two_d = lambda a: a.reshape(-1, shape[-1])
        res = _adamw("adamw_" + name, two_d(p[name]), two_d(p["m_" + name]), two_d(p["v_" + name]),
                     partials.reshape(partials.shape[0], -1, shape[-1]))
        out[name] = [r.reshape(shape) for r in res]

    update("w_ada", jnp.stack(g_ada)[None])
    for n in ("w_in", "w_sb_up", "w_ssm_up", "w_ffn_in", "w_glu", "w_out", "w_ffn_out"):
        update(n, land[n])
    small_w = [p[n] for n in SMALL_PARAMS]
    res = _adamw("adamw_small", _pack(small_w), _pack([p["m_" + n] for n in SMALL_PARAMS]),
                 _pack([p["v_" + n] for n in SMALL_PARAMS]), small_all)
    for kind, packed in enumerate(res):
        for n, a in zip(SMALL_PARAMS, _unpack(packed, small_w)):
            out.setdefault(n, [None] * 4)[kind] = a

    return ((loss, grad_x[None]) + tuple(out[n][0] for n in WEIGHTS) + tuple(out[n][1] for n in WEIGHTS)
            + tuple(out[n][2] for n in WEIGHTS) + tuple(out[n][3] for n in WEIGHTS))
```

```python
import jax
import jax.numpy as jnp
from jax import lax
from jax.experimental import pallas as pl
from jax.experimental.pallas import tpu as pltpu

F32 = jnp.float32
BF16 = jnp.bfloat16
GRAD_WIRE = BF16
FFN_ACT = BF16

N_DEV = 8
LANES = 128
SUBLANES = 8
VMEM_BYTES = 64 * 1024 * 1024
HEAD_DIM = 64
SB_BLOCK = 256
SLAB_GROUPS = 8
LN_EPS = 1e-5
ADAM_LR, ADAM_B1, ADAM_B2, ADAM_EPS, ADAM_WD, ADAM_STEP = 0.001, 0.9, 0.999, 1e-08, 0.01, 10
SB_UNDERFLOW = -120.0

PACK_ROWS = 256
MESH_AXES = ("x", "y", "c")


def _vmem_limit(block_bytes):
    return int(min(max(3 * block_bytes + (8 << 20), 24 << 20), VMEM_BYTES - (8 << 20)))


def _nbytes(shape, dtype):
    n = 1
    for d in shape:
        if d is not None:
            n *= d
    return n * jnp.dtype(dtype).itemsize


def _spec(shape, fn):
    return pl.BlockSpec(shape, fn)


class _Exchange:
    def __init__(self, scatter=(), gather=(), layered=()):
        self.arrs = list(scatter) + [a for a, _, _, _ in layered] + list(gather)
        self.n = len(self.arrs)
        self.n_sc = len(scatter) + len(layered)
        self.layer = [None] * len(scatter) + [l for _, l, _, _ in layered] + [None] * len(gather)
        self.shapes = ([a.shape for a in scatter] + [(N_DEV, dp) + a.shape[1:] for a, _, dp, _ in layered]
                       + [(N_DEV,) + a.shape for a in gather])
        self.held = [(len(scatter) + i, b) for i, (_, _, _, b) in enumerate(layered) if b is not None]
        self.operands = self.arrs + [b for _, b in self.held]
        hbm = pl.BlockSpec(memory_space=pltpu.HBM)
        self.in_specs = [hbm] * len(self.operands)
        self.out_specs = [hbm] * self.n
        self.out_shape = [jax.ShapeDtypeStruct(s, a.dtype) for s, a in zip(self.shapes, self.arrs)]
        self.scratch = [pltpu.SemaphoreType.DMA((self.n, N_DEV - 1)), pltpu.SemaphoreType.DMA((self.n, N_DEV - 1)),
                        pltpu.SemaphoreType.DMA((self.n,))]

    def aliases(self, first_in, first_out):
        return {first_in + self.n + i: first_out + a for i, (a, _) in enumerate(self.held)}

    def copies(self, ins, outs, sems):
        send_sems, recv_sems, own_sems = sems
        x, y, c = lax.axis_index("x"), lax.axis_index("y"), lax.axis_index("c")
        me = 4 * x + 2 * y + c
        landing = [outs[a].at[me] if self.layer[a] is None else outs[a].at[me, self.layer[a]] for a in range(self.n)]
        out = [pltpu.make_async_copy(ins[a].at[me] if a < self.n_sc else ins[a], landing[a], own_sems.at[a])
               for a in range(self.n)]
        for k in range(1, N_DEV):
            px = 1 - x if k & 4 else x
            py = 1 - y if k & 2 else y
            pc = 1 - c if k & 1 else c
            peer = 4 * px + 2 * py + pc
            for a in range(self.n):
                out.append(pltpu.make_async_remote_copy(
                    src_ref=ins[a].at[peer] if a < self.n_sc else ins[a], dst_ref=landing[a],
                    send_sem=send_sems.at[a, k - 1], recv_sem=recv_sems.at[a, k - 1],
                    device_id=(px, py, pc), device_id_type=pl.DeviceIdType.MESH))
        return out


def _exchange(name, scatter, gather, layered=()):
    ex = _Exchange(scatter, gather, layered)

    def body(*refs):
        copies = ex.copies(refs[:ex.n], refs[len(ex.operands):len(ex.operands) + ex.n], refs[-3:])
        for cp in copies:
            cp.start()
        for cp in copies:
            cp.wait()

    return pl.pallas_call(body, name=name, in_specs=ex.in_specs, out_specs=ex.out_specs, out_shape=ex.out_shape,
                          input_output_aliases=ex.aliases(0, 0), scratch_shapes=ex.scratch)(*ex.operands)


def _call_beside(ex, body, name, grid, in_specs, out_specs, out_shape, scratch_shapes, vmem_bytes, operands,
                 semantics):
    if ex is None:
        res = pl.pallas_call(
            body, name=name, grid=grid, in_specs=in_specs, out_specs=out_specs, out_shape=out_shape,
            scratch_shapes=scratch_shapes,
            compiler_params=pltpu.CompilerParams(dimension_semantics=semantics, vmem_limit_bytes=vmem_bytes),
        )(*operands)
        return res, None
    n_in, n_out, n_scr = len(in_specs), len(out_specs), len(scratch_shapes)
    n_xin = len(ex.operands)

    def fused(*refs):
        mine = refs[:n_in] + refs[n_in + n_xin:n_in + n_xin + n_out]
        mine += refs[n_in + n_xin + n_out + ex.n:n_in + n_xin + n_out + ex.n + n_scr]
        first = pl.program_id(0) == 0
        last = pl.program_id(0) == grid[0] - 1
        for dim in range(1, len(grid)):
            first = jnp.logical_and(first, pl.program_id(dim) == 0)
            last = jnp.logical_and(last, pl.program_id(dim) == grid[dim] - 1)
        x_ins = refs[n_in:n_in + ex.n]
        x_outs = refs[n_in + n_xin + n_out:n_in + n_xin + n_out + ex.n]

        @pl.when(first)
        def _():
            for cp in ex.copies(x_ins, x_outs, refs[-3:]):
                cp.start()

        body(*mine)

        @pl.when(last)
        def _():
            for cp in ex.copies(x_ins, x_outs, refs[-3:]):
                cp.wait()

    res = pl.pallas_call(
        fused, name=name, grid=grid, in_specs=list(in_specs) + ex.in_specs, out_specs=list(out_specs) + ex.out_specs,
        out_shape=list(out_shape) + ex.out_shape, input_output_aliases=ex.aliases(n_in, n_out),
        scratch_shapes=list(scratch_shapes) + ex.scratch,
        compiler_params=pltpu.CompilerParams(dimension_semantics=("arbitrary",) * len(grid),
                                             vmem_limit_bytes=vmem_bytes),
    )(*operands, *ex.operands)
    return res[:n_out], res[n_out:]


NN = (((1,), (0,)), ((), ()))
NT = (((1,), (1,)), ((), ()))
TN = (((0,), (0,)), ((), ()))


def _mm(name, a, b, a_spec, b_spec, o_spec, o_shape, o_dtype, grid, dims):
    nk = grid[2]
    acc_shape = tuple(d for d in o_spec.block_shape if d is not None)

    def product(a_ref, b_ref):
        return lax.dot_general(a_ref[...].astype(BF16), b_ref[...].astype(BF16), dims, preferred_element_type=F32)

    def body_once(a_ref, b_ref, o_ref):
        o_ref[...] = product(a_ref, b_ref).astype(o_ref.dtype)

    def body(a_ref, b_ref, o_ref, acc_ref):
        k = pl.program_id(2)

        @pl.when(k == 0)
        def _():
            acc_ref[...] = product(a_ref, b_ref)

        @pl.when(k > 0)
        def _():
            acc_ref[...] += product(a_ref, b_ref)

        @pl.when(k == nk - 1)
        def _():
            o_ref[...] = acc_ref[...].astype(o_ref.dtype)

    blk = (_nbytes(a_spec.block_shape, a.dtype) + _nbytes(b_spec.block_shape, b.dtype)
           + _nbytes(acc_shape, o_dtype) + _nbytes(acc_shape, F32))
    return pl.pallas_call(
        body_once if nk == 1 else body, name=name, grid=grid, in_specs=[a_spec, b_spec], out_specs=o_spec,
        out_shape=jax.ShapeDtypeStruct(o_shape, o_dtype),
        scratch_shapes=[] if nk == 1 else [pltpu.VMEM(acc_shape, F32)],
        compiler_params=pltpu.CompilerParams(dimension_semantics=("parallel", "parallel", "arbitrary"),
                                             vmem_limit_bytes=_vmem_limit(blk)),
    )(a, b)


def _tile(n, pref=1024):
    t = pref
    while t >= LANES:
        if n % t == 0:
            return t
        t -= LANES
    return n


def _rowwise(name, fn, ins, outs, grid):
    n_in = len(ins)

    def body(*refs):
        vals = fn(*[r[...].astype(F32) for r in refs[:n_in]])
        if not isinstance(vals, (tuple, list)):
            vals = (vals,)
        for r, v in zip(refs[n_in:], vals):
            r[...] = v.astype(r.dtype)

    blk = sum(_nbytes(bs, a.dtype) for a, bs, _ in ins) + sum(_nbytes(bs, d) + _nbytes(bs, F32) for _, d, bs, _ in outs)
    return pl.pallas_call(
        body, name=name, grid=grid,
        in_specs=[_spec(bs, im) for _, bs, im in ins],
        out_specs=[_spec(bs, im) for _, _, bs, im in outs],
        out_shape=[jax.ShapeDtypeStruct(s, d) for s, d, _, _ in outs],
        compiler_params=pltpu.CompilerParams(dimension_semantics=("parallel",) * len(grid),
                                             vmem_limit_bytes=_vmem_limit(2 * blk)),
    )(*[a for a, _, _ in ins])


def _rowwise_vjp(name, fn, ins, cts, wrt, grid):
    n_in, n_ct = len(ins), len(cts)
    idx = [w[0] for w in wrt]

    def body(*refs):
        prim = [r[...].astype(F32) for r in refs[:n_in]]
        ct = tuple(r[...].astype(F32) for r in refs[n_in:n_in + n_ct])
        o_refs = refs[n_in + n_ct:]

        def g(*sel):
            full = list(prim)
            for i, s in zip(idx, sel):
                full[i] = s
            out = fn(*full)
            return tuple(out) if isinstance(out, (tuple, list)) else (out,)

        _, pull = jax.vjp(g, *[prim[i] for i in idx])
        grads = pull(ct)
        first = pl.program_id(0) == 0
        for d in range(1, len(grid)):
            first = jnp.logical_and(first, pl.program_id(d) == 0)
        for w, o_ref, gr in zip(wrt, o_refs, grads):
            if w[1] == "row":
                o_ref[...] = gr.astype(o_ref.dtype)
            else:
                @pl.when(first)
                def _(o_ref=o_ref):
                    o_ref[...] = jnp.zeros_like(o_ref)

                o_ref[...] += gr.astype(o_ref.dtype)

    blk = (sum(_nbytes(bs, a.dtype) + _nbytes(bs, F32) for a, bs, _ in list(ins) + list(cts))
           + sum(_nbytes(w[4], w[3]) + _nbytes(w[4], F32) for w in wrt))
    return pl.pallas_call(
        body, name=name, grid=grid,
        in_specs=[_spec(bs, im) for _, bs, im in list(ins) + list(cts)],
        out_specs=[_spec(w[4], w[5]) for w in wrt],
        out_shape=[jax.ShapeDtypeStruct(w[2], w[3]) for w in wrt],
        compiler_params=pltpu.CompilerParams(dimension_semantics=("arbitrary",) * len(grid),
                                             vmem_limit_bytes=_vmem_limit(2 * blk)),
    )(*[a for a, _, _ in list(ins) + list(cts)])


def _normalize(x):
    mu = jnp.mean(x, axis=-1, keepdims=True)
    xc = x - mu
    var = jnp.mean(xc * xc, axis=-1, keepdims=True)
    return xc * lax.rsqrt(var + LN_EPS)


def _modulate(x, sc, sh):
    return _normalize(x) * (1.0 + sc) + sh


def _make_resid_fns(alpha):
    def resid_ln(x, y, gate, g, b):
        return _normalize(alpha * x + (1.0 + gate) * y) * g + b

    def resid_ln_mod(x, y, gate, g, b, sc, sh):
        xn = resid_ln(x, y, gate, g, b)
        return xn, _modulate(xn, sc, sh)

    return resid_ln, resid_ln_mod


def _merge_fn(y_sb, y_ssm, gates):
    half = gates.shape[-1] // 2
    return jax.nn.sigmoid(gates[:, :half]) * y_sb + jax.nn.sigmoid(gates[:, half:]) * y_ssm


def _swiglu_fn(gate_up):
    gate, up = gate_up[0], gate_up[1]
    return gate * jax.nn.sigmoid(gate) * up


def _s5_act_fn(yc, u, d_skip):
    return jax.nn.gelu(yc + d_skip * u)


def _s5_glu_fn(yc, u, t, d_skip, b_glu):
    return _s5_act_fn(yc, u, d_skip) * jax.nn.sigmoid(t + b_glu)


def _s5_post_fn(yc, u, t, d_skip, b_glu):
    y1 = _s5_act_fn(yc, u, d_skip)
    return y1, y1 * jax.nn.sigmoid(t + b_glu)


def _sb_tri(kind):
    row = lax.broadcasted_iota(jnp.int32, (SB_BLOCK, SB_BLOCK), 0)
    col = lax.broadcasted_iota(jnp.int32, (SB_BLOCK, SB_BLOCK), 1)
    if kind == "after":
        return (row > col).astype(BF16)
    if kind == "from":
        return (row >= col).astype(BF16)
    return col < row


def _split_dot(x, m):
    hi = x.astype(BF16)
    lo = (x - hi.astype(F32)).astype(BF16)
    return (lax.dot_general(hi, m, NN, preferred_element_type=F32)
            + lax.dot_general(lo, m, NN, preferred_element_type=F32))


def _sb_scores(qh, k2):
    z = lax.dot_general(qh, k2, NT, preferred_element_type=F32)
    log_beta = jnp.minimum(z, 0.0) - jnp.log(1.0 + jnp.exp(-jnp.abs(z)))
    return log_beta, log_beta - z


def _sb_attention_fwd(proj, sb_width, beside=None):
    seq = proj.shape[0]
    n_pair, n_q = sb_width // LANES, seq // SB_BLOCK
    scale = 1.0 / (HEAD_DIM ** 0.5)

    def body(q_ref, k_ref, v_ref, o_ref, o32_ref):
        qi = pl.program_id(1)
        q2 = q_ref[...]
        lane = lax.broadcasted_iota(jnp.int32, (SB_BLOCK, LANES), 1)
        m_after, causal = _sb_tri("after"), _sb_tri("mask")
        heads = [lane < HEAD_DIM, lane >= HEAD_DIM]
        qh = [(jnp.where(m, q2, 0.0) * scale).astype(BF16) for m in heads]

        def scores(kb, diag):
            ks = pl.multiple_of(kb * SB_BLOCK, SB_BLOCK)
            k2 = k_ref[pl.ds(ks, SB_BLOCK), :].astype(BF16)
            out = []
            for h in range(2):
                log_beta, log_1m = _sb_scores(qh[h], k2)
                if diag:
                    log_1m = jnp.where(causal, log_1m, 0.0)
                out += [log_beta + _split_dot(log_1m, m_after), jnp.sum(log_1m, axis=1, keepdims=True)]
            return tuple(out)

        def weigh(kb, sc, carry, acc, diag):
            ks = pl.multiple_of(kb * SB_BLOCK, SB_BLOCK)
            v2 = v_ref[pl.ds(ks, SB_BLOCK), :].astype(BF16)
            out = []
            for h in range(2):
                w = jnp.exp(sc[2 * h] + carry[h])
                if diag:
                    w = jnp.where(causal, w, 0.0)
                out.append(acc[h] + lax.dot_general(w.astype(BF16), v2, NN, preferred_element_type=F32))
            return tuple(out)

        zero = jnp.zeros((SB_BLOCK, LANES), F32)
        zcol = jnp.zeros((SB_BLOCK, 1), F32)
        sc = scores(qi, True)
        acc = weigh(qi, sc, (zcol, zcol), (zero, zero), True)
        carry = (sc[1], sc[3])
        sc = scores(jnp.maximum(qi - 1, 0), False)

        def loop(st):
            kb, sc, carry, acc = st
            after = (carry[0] + sc[1], carry[1] + sc[3])
            done = jnp.maximum(jnp.max(after[0]), jnp.max(after[1])) < SB_UNDERFLOW
            sc_next = scores(jnp.maximum(kb - 1, 0), False)
            acc = weigh(kb, sc, carry, acc, False)
            return jnp.where(done, -1, kb - 1), sc_next, after, acc

        _, _, _, acc = lax.while_loop(lambda st: st[0] >= 0, loop, (qi - 1, sc, carry, acc))
        out = jnp.where(heads[0], acc[0], acc[1])
        o_ref[...] = out.astype(o_ref.dtype)
        o32_ref[...] = out

    q_spec = _spec((SB_BLOCK, LANES), lambda h, i: (i, h))
    kv = [_spec((seq, LANES), lambda h, i, o=o: (0, o + h)) for o in (n_pair, 2 * n_pair)]
    o_spec = _spec((SB_BLOCK, LANES), lambda h, i: (i, h))
    return _call_beside(
        beside, body, "sb_attention_fwd", (n_pair, n_q), [q_spec] + kv, [o_spec, o_spec],
        [jax.ShapeDtypeStruct((seq, sb_width), BF16), jax.ShapeDtypeStruct((seq, sb_width), F32)], [],
        _vmem_limit(2 * seq * LANES * 4), (proj, proj, proj), ("parallel", "arbitrary"))


def _sb_attention_bwd(proj, o32, do, sb_width, beside=None):
    seq = proj.shape[0]
    n_pair, n_q = sb_width // LANES, seq // SB_BLOCK
    scale = 1.0 / (HEAD_DIM ** 0.5)

    def body(q_ref, k_ref, v_ref, o_ref, do_ref, dq_ref, dk_ref, dv_ref):
        qi = pl.program_id(1)

        @pl.when(qi == 0)
        def _():
            dk_ref[...] = jnp.zeros_like(dk_ref)
            dv_ref[...] = jnp.zeros_like(dv_ref)

        q2 = q_ref[...]
        do2 = do_ref[...].astype(F32)
        o2 = o_ref[...]
        lane = lax.broadcasted_iota(jnp.int32, (SB_BLOCK, LANES), 1)
        m_after, m_from, causal = _sb_tri("after"), _sb_tri("from"), _sb_tri("mask")
        heads = [lane < HEAD_DIM, lane >= HEAD_DIM]
        qh = [(jnp.where(m, q2, 0.0) * scale).astype(BF16) for m in heads]
        doh = [jnp.where(m, do2, 0.0) for m in heads]
        doh_b = [v.astype(BF16) for v in doh]
        total = [jnp.sum(v * o2, axis=1, keepdims=True) for v in doh]

        def scores(kb, diag):
            ks = pl.multiple_of(kb * SB_BLOCK, SB_BLOCK)
            k2 = k_ref[pl.ds(ks, SB_BLOCK), :].astype(BF16)
            v2 = v_ref[pl.ds(ks, SB_BLOCK), :].astype(BF16)
            out = []
            for h in range(2):
                log_beta, log_1m = _sb_scores(qh[h], k2)
                if diag:
                    log_1m = jnp.where(causal, log_1m, 0.0)
                out += [log_beta + _split_dot(log_1m, m_after), jnp.sum(log_1m, axis=1, keepdims=True),
                        lax.dot_general(doh_b[h], v2, NT, preferred_element_type=F32), log_beta]
            return tuple(out)

        def pull(kb, sc, carry, right, dq, diag):
            ks = pl.multiple_of(kb * SB_BLOCK, SB_BLOCK)
            k2 = k_ref[pl.ds(ks, SB_BLOCK), :].astype(BF16)
            dv_blk, dk_blk, right_out, dq_out = None, None, [], []
            for h in range(2):
                arg, _, d_w, log_beta = sc[4 * h:4 * h + 4]
                w = jnp.exp(arg + carry[h])
                if diag:
                    w = jnp.where(causal, w, 0.0)
                w_b = w.astype(BF16)
                d_arg = d_w * w_b.astype(F32)
                dv_h = lax.dot_general(w_b, doh_b[h], TN, preferred_element_type=F32)
                d_log_1m = total[h] - right[h] - _split_dot(d_arg, m_from)
                beta = jnp.exp(log_beta)
                dz = d_arg * (1.0 - beta) - beta * d_log_1m
                if diag:
                    dz = jnp.where(causal, dz, 0.0)
                dz_b = dz.astype(BF16)
                dk_h = lax.dot_general(dz_b, qh[h], TN, preferred_element_type=F32)
                dv_blk = dv_h if h == 0 else dv_blk + dv_h
                dk_blk = dk_h if h == 0 else dk_blk + dk_h
                dq_out.append(dq[h] + lax.dot_general(dz_b, k2, NN, preferred_element_type=F32))
                right_out.append(right[h] + jnp.sum(d_arg, axis=1, keepdims=True))
            dv_ref[pl.ds(ks, SB_BLOCK), :] += dv_blk
            dk_ref[pl.ds(ks, SB_BLOCK), :] += dk_blk
            return tuple(right_out), tuple(dq_out)

        zero = jnp.zeros((SB_BLOCK, LANES), F32)
        zcol = jnp.zeros((SB_BLOCK, 1), F32)
        sc = scores(qi, True)
        right, dq = pull(qi, sc, (zcol, zcol), (zcol, zcol), (zero, zero), True)
        carry = (sc[1], sc[5])
        sc = scores(jnp.maximum(qi - 1, 0), False)

        def loop(st):
            kb, sc, carry, right, dq = st
            after = (carry[0] + sc[1], carry[1] + sc[5])
            done = jnp.maximum(jnp.max(after[0]), jnp.max(after[1])) < SB_UNDERFLOW
            sc_next = scores(jnp.maximum(kb - 1, 0), False)
            right, dq = pull(kb, sc, carry, right, dq, False)
            return jnp.where(done, -1, kb - 1), sc_next, after, right, dq

        _, _, _, _, dq = lax.while_loop(lambda st: st[0] >= 0, loop, (qi - 1, sc, carry, right, dq))
        dq_ref[...] = (jnp.where(heads[0], dq[0], dq[1]) * scale).astype(dq_ref.dtype)

    q_spec = _spec((SB_BLOCK, LANES), lambda h, i: (i, h))
    kv = [_spec((seq, LANES), lambda h, i, o=o: (0, o + h)) for o in (n_pair, 2 * n_pair)]
    full = _spec((seq, LANES), lambda h, i: (0, h))
    return _call_beside(
        beside, body, "sb_attention_bwd", (n_pair, n_q), [q_spec] + kv + [q_spec, q_spec], [q_spec, full, full],
        [jax.ShapeDtypeStruct((seq, sb_width), BF16), jax.ShapeDtypeStruct((seq, sb_width), F32),
         jax.ShapeDtypeStruct((seq, sb_width), F32)], [],
        _vmem_limit(4 * seq * LANES * 4), (proj, proj, proj, o32, do), ("parallel", "arbitrary"))


def _s5_discretize(a_re, a_im, log_dt, b_re, b_im, c_re, c_im):
    n_g, n_p = a_re.shape
    c_g = b_re.shape[-1]
    ns = n_g // SLAB_GROUPS
    dt = jnp.exp(log_dt)[:, None]
    xr, xi = a_re * dt, a_im * dt
    mag = jnp.exp(xr)
    lr, li = mag * jnp.cos(xi), mag * jnp.sin(xi)
    den = a_re * a_re + a_im * a_im
    fr = ((lr - 1.0) * a_re + li * a_im) / den
    fi = (li * a_re - (lr - 1.0) * a_im) / den
    bb_re = fr[..., None] * b_re - fi[..., None] * b_im
    bb_im = fr[..., None] * b_im + fi[..., None] * b_re
    eye = jnp.eye(SLAB_GROUPS, dtype=F32)

    def diag_b(m):
        m = jnp.transpose(m.reshape(ns, SLAB_GROUPS, n_p, c_g), (0, 1, 3, 2))
        m = m[:, :, :, None, :] * eye[None, :, None, :, None]
        return m.reshape(ns, SLAB_GROUPS * c_g, SLAB_GROUPS * n_p)

    def diag_c(m):
        m = jnp.transpose(m.reshape(ns, SLAB_GROUPS, c_g, n_p), (0, 1, 3, 2))
        m = m[:, :, :, None, :] * eye[None, :, None, :, None]
        return m.reshape(ns, SLAB_GROUPS * n_p, SLAB_GROUPS * c_g)

    bs = jnp.concatenate([diag_b(bb_re), diag_b(bb_im)], axis=-1)
    cs = jnp.concatenate([diag_c(c_re), -diag_c(c_im)], axis=1)
    lam = jnp.concatenate([lr.reshape(ns, 1, -1), li.reshape(ns, 1, -1)], axis=-1)
    return bs, cs, lam


def _s5_powers(a_re, a_im, log_dt, n):
    n_g, n_p = a_re.shape
    ns = n_g // SLAB_GROUPS
    dt = jnp.exp(log_dt)[:, None]
    mag = jnp.exp(a_re * dt)
    base_r, base_i = mag * jnp.cos(a_im * dt), mag * jnp.sin(a_im * dt)
    steps = jnp.arange(1, n + 1, dtype=jnp.int32)[:, None, None]
    pr, pi = jnp.ones((n, n_g, n_p), F32), jnp.zeros((n, n_g, n_p), F32)
    for b in range(n.bit_length()):
        take = ((steps >> b) & 1) == 1
        pr, pi = (jnp.where(take, pr * base_r - pi * base_i, pr), jnp.where(take, pr * base_i + pi * base_r, pi))
        base_r, base_i = base_r * base_r - base_i * base_i, 2.0 * base_r * base_i

    def slabs(re, im):
        one = lambda m: jnp.transpose(m.reshape(n, ns, SLAB_GROUPS * n_p), (1, 0, 2))
        return jnp.concatenate([one(re), one(im)], axis=-1)

    return slabs(pr, pi), slabs(pr[::-1], -pi[::-1])


def _lanes(j):
    return slice(j * LANES, (j + 1) * LANES)


def _tile8(k):
    return pl.ds(pl.multiple_of(k * SUBLANES, SUBLANES), SUBLANES)


def _s5_interleave(dst_ref, src_ref, t_seg):
    def body(k, _):
        dst_ref[_tile8(k), :] = src_ref[pl.ds(k, SUBLANES, stride=t_seg), :]
        return 0

    lax.fori_loop(0, t_seg, body, 0, unroll=4)


def _s5_join_segments(st_ref, end_ref, car_ref, tab_ref, row, order, n_pair):
    for j in range(n_pair):
        re, im = _lanes(j), _lanes(n_pair + j)
        cr, ci = st_ref[:, re], st_ref[:, im]
        tr, ti = tab_ref[row:row + 1, re], tab_ref[row:row + 1, im]
        for s in order:
            car_ref[s:s + 1, re] = cr
            car_ref[s:s + 1, im] = ci
            er, ei = end_ref[s:s + 1, re], end_ref[s:s + 1, im]
            cr, ci = er + tr * cr - ti * ci, ei + tr * ci + ti * cr
        st_ref[:, re] = cr
        st_ref[:, im] = ci


def _s5_add_carries(buf_ref, car_ref, tab_ref, t_seg, n_pair):
    def fix(k, _):
        rows = _tile8(k)
        tab = tab_ref[pl.ds(k, 1), :]
        for j in range(n_pair):
            re, im = _lanes(j), _lanes(n_pair + j)
            cr, ci = car_ref[:, re], car_ref[:, im]
            tr, ti = tab[:, re], tab[:, im]
            buf_ref[rows, re] += tr * cr - ti * ci
            buf_ref[rows, im] += tr * ci + ti * cr
        return 0

    lax.fori_loop(0, t_seg, fix, 0, unroll=2)


def _s5_scan_fwd(proj, u_col, bs, cs, lam, pw, t_blk, beside=None):
    seq = proj.shape[0]
    ns, _, w2 = bs.shape
    n_pair = w2 // (2 * LANES)
    t_seg, n_t = t_blk // SUBLANES, seq // t_blk

    def body(u_ref, bs_ref, cs_ref, lam_ref, pw_ref, yc_ref, h_ref, st_ref, end_ref, car_ref, ui_ref, bu_ref, yi_ref):
        @pl.when(pl.program_id(1) == 0)
        def _():
            st_ref[...] = jnp.zeros_like(st_ref)

        _s5_interleave(ui_ref, u_ref, t_seg)
        bu_ref[...] = lax.dot_general(ui_ref[...].astype(BF16), bs_ref[...], NN, preferred_element_type=F32)
        lam_r = [jnp.broadcast_to(lam_ref[:, _lanes(j)], (SUBLANES, LANES)) for j in range(n_pair)]
        lam_i = [jnp.broadcast_to(lam_ref[:, _lanes(n_pair + j)], (SUBLANES, LANES)) for j in range(n_pair)]

        def step(k, c):
            rows = _tile8(k)
            out = []
            for j in range(n_pair):
                hr, hi = c[2 * j], c[2 * j + 1]
                nr = lam_r[j] * hr - lam_i[j] * hi + bu_ref[rows, _lanes(j)]
                ni = lam_i[j] * hr + lam_r[j] * hi + bu_ref[rows, _lanes(n_pair + j)]
                h_ref[rows, _lanes(j)] = nr
                h_ref[rows, _lanes(n_pair + j)] = ni
                out += [nr, ni]
            return tuple(out)

        ends = lax.fori_loop(0, t_seg, step, (jnp.zeros((SUBLANES, LANES), F32),) * (2 * n_pair), unroll=4)
        for j in range(n_pair):
            end_ref[:, _lanes(j)] = ends[2 * j]
            end_ref[:, _lanes(n_pair + j)] = ends[2 * j + 1]
        _s5_join_segments(st_ref, end_ref, car_ref, pw_ref, t_seg - 1, list(range(SUBLANES)), n_pair)
        _s5_add_carries(h_ref, car_ref, pw_ref, t_seg, n_pair)
        yi_ref[...] = lax.dot_general(h_ref[...].astype(BF16), cs_ref[...], NN, preferred_element_type=F32)

        def scatter(k, _):
            yc_ref[pl.ds(k, SUBLANES, stride=t_seg), :] = yi_ref[_tile8(k), :]
            return 0

        lax.fori_loop(0, t_seg, scatter, 0, unroll=4)

    return _call_beside(
        beside, body, "s5_scan_fwd", (ns, n_t),
        [_spec((t_blk, LANES), lambda s, i: (i, u_col + s)),
         _spec((None, LANES, w2), lambda s, i: (s, 0, 0)),
         _spec((None, w2, LANES), lambda s, i: (s, 0, 0)),
         _spec((None, 1, w2), lambda s, i: (s, 0, 0)),
         _spec((None, t_seg, w2), lambda s, i: (s, 0, 0))],
        [_spec((t_blk, LANES), lambda s, i: (i, s)),
         _spec((None, t_blk, w2), lambda s, i: (s, i, 0))],
        [jax.ShapeDtypeStruct((seq, ns * LANES), F32), jax.ShapeDtypeStruct((ns, seq, w2), F32)],
        [pltpu.VMEM((1, w2), F32), pltpu.VMEM((SUBLANES, w2), F32), pltpu.VMEM((SUBLANES, w2), F32),
         pltpu.VMEM((t_blk, LANES), F32), pltpu.VMEM((t_blk, w2), F32), pltpu.VMEM((t_blk, LANES), F32)],
        _vmem_limit(3 * t_blk * w2 * 4), (proj, bs, cs, lam, pw), ("parallel", "arbitrary"))


def _s5_scan_bwd(proj, u_col, states, d_yc, du_extra, bs, cs, lam, qw, t_blk):
    seq = proj.shape[0]
    ns, _, w2 = bs.shape
    n_pair = w2 // (2 * LANES)
    t_seg, n_t = t_blk // SUBLANES, seq // t_blk

    def body(u_ref, h_ref, hp_ref, dyc_ref, dux_ref, bs_ref, cs_ref, lam_ref, qw_ref,
             du_ref, dbs_ref, dcs_ref, dlam_ref, g_ref, gd_ref, st_ref, end_ref, car_ref, ui_ref, dyi_ref, dui_ref):
        i = pl.program_id(1)

        @pl.when(i == 0)
        def _():
            st_ref[...] = jnp.zeros_like(st_ref)
            dbs_ref[...] = jnp.zeros_like(dbs_ref)
            dcs_ref[...] = jnp.zeros_like(dcs_ref)
            dlam_ref[...] = jnp.zeros_like(dlam_ref)

        _s5_interleave(ui_ref, u_ref, t_seg)
        _s5_interleave(dyi_ref, dyc_ref, t_seg)
        dyc_b = dyi_ref[...].astype(BF16)
        gd_ref[...] = lax.dot_general(dyc_b, cs_ref[...], NT, preferred_element_type=F32)
        lam_r = [jnp.broadcast_to(lam_ref[:, _lanes(j)], (SUBLANES, LANES)) for j in range(n_pair)]
        lam_i = [jnp.broadcast_to(lam_ref[:, _lanes(n_pair + j)], (SUBLANES, LANES)) for j in range(n_pair)]

        def step(kk, c):
            rows = _tile8(t_seg - 1 - kk)
            out = []
            for j in range(n_pair):
                gr_n, gi_n = c[2 * j], c[2 * j + 1]
                gr = gd_ref[rows, _lanes(j)] + lam_r[j] * gr_n + lam_i[j] * gi_n
                gi = gd_ref[rows, _lanes(n_pair + j)] + lam_r[j] * gi_n - lam_i[j] * gr_n
                g_ref[rows, _lanes(j)] = gr
                g_ref[rows, _lanes(n_pair + j)] = gi
                out += [gr, gi]
            return tuple(out)

        zero = jnp.zeros((SUBLANES, LANES), F32)
        firsts = lax.fori_loop(0, t_seg, step, (zero,) * (2 * n_pair), unroll=4)
        for j in range(n_pair):
            end_ref[:, _lanes(j)] = firsts[2 * j]
            end_ref[:, _lanes(n_pair + j)] = firsts[2 * j + 1]
        _s5_join_segments(st_ref, end_ref, car_ref, qw_ref, 0, list(range(SUBLANES))[::-1], n_pair)
        _s5_add_carries(g_ref, car_ref, qw_ref, t_seg, n_pair)

        def pair_up(k, c):
            rows, prev = _tile8(k), _tile8(k - 1)
            out = []
            for j in range(n_pair):
                re, im = _lanes(j), _lanes(n_pair + j)
                gr, gi, hr, hi = g_ref[rows, re], g_ref[rows, im], h_ref[prev, re], h_ref[prev, im]
                out += [c[2 * j] + gr * hr + gi * hi, c[2 * j + 1] + gi * hr - gr * hi]
            return tuple(out)

        acc = lax.fori_loop(1, t_seg, pair_up, (zero,) * (2 * n_pair), unroll=4)
        has_prev = (i < n_t - 1).astype(F32)
        first_seg = lax.broadcasted_iota(jnp.int32, (SUBLANES, LANES), 0) == 0
        last = _tile8(t_seg - 1)
        for j in range(n_pair):
            re, im = _lanes(j), _lanes(n_pair + j)
            gr, gi = g_ref[0:SUBLANES, re], g_ref[0:SUBLANES, im]
            hr = jnp.where(first_seg, hp_ref[SUBLANES - 1:, re] * has_prev, pltpu.roll(h_ref[last, re], 1, 0))
            hi = jnp.where(first_seg, hp_ref[SUBLANES - 1:, im] * has_prev, pltpu.roll(h_ref[last, im], 1, 0))
            dlam_ref[:, re] += jnp.sum(acc[2 * j] + gr * hr + gi * hi, axis=0, keepdims=True)
            dlam_ref[:, im] += jnp.sum(acc[2 * j + 1] + gi * hr - gr * hi, axis=0, keepdims=True)

        g_b = g_ref[...].astype(BF16)
        dui_ref[...] = lax.dot_general(g_b, bs_ref[...], NT, preferred_element_type=F32)
        dbs_ref[...] += lax.dot_general(ui_ref[...].astype(BF16), g_b, TN, preferred_element_type=F32)
        dcs_ref[...] += lax.dot_general(h_ref[...].astype(BF16), dyc_b, TN, preferred_element_type=F32)

        def scatter(k, _):
            rows = pl.ds(k, SUBLANES, stride=t_seg)
            du_ref[rows, :] = (dui_ref[_tile8(k), :] + dux_ref[rows, :]).astype(du_ref.dtype)
            return 0

        lax.fori_loop(0, t_seg, scatter, 0, unroll=4)

    rev = lambda i: n_t - 1 - i
    return pl.pallas_call(
        body, name="s5_scan_bwd", grid=(ns, n_t),
        in_specs=[_spec((t_blk, LANES), lambda s, i: (rev(i), u_col + s)),
                  _spec((None, t_blk, w2), lambda s, i: (s, rev(i), 0)),
                  _spec((None, SUBLANES, w2), lambda s, i: (s, jnp.maximum(rev(i) * t_seg - 1, 0), 0)),
                  _spec((t_blk, LANES), lambda s, i: (rev(i), s)),
                  _spec((t_blk, LANES), lambda s, i: (rev(i), s)),
                  _spec((None, LANES, w2), lambda s, i: (s, 0, 0)),
                  _spec((None, w2, LANES), lambda s, i: (s, 0, 0)),
                  _spec((None, 1, w2), lambda s, i: (s, 0, 0)),
                  _spec((None, t_seg, w2), lambda s, i: (s, 0, 0))],
        out_specs=[_spec((t_blk, LANES), lambda s, i: (rev(i), s)),
                   _spec((None, LANES, w2), lambda s, i: (s, 0, 0)),
                   _spec((None, w2, LANES), lambda s, i: (s, 0, 0)),
                   _spec((None, 1, w2), lambda s, i: (s, 0, 0))],
        out_shape=[jax.ShapeDtypeStruct((seq, ns * LANES), F32), jax.ShapeDtypeStruct(bs.shape, F32),
                   jax.ShapeDtypeStruct(cs.shape, F32), jax.ShapeDtypeStruct(lam.shape, F32)],
        scratch_shapes=[pltpu.VMEM((t_blk, w2), F32), pltpu.VMEM((t_blk, w2), F32), pltpu.VMEM((1, w2), F32),
                        pltpu.VMEM((SUBLANES, w2), F32), pltpu.VMEM((SUBLANES, w2), F32),
                        pltpu.VMEM((t_blk, LANES), F32), pltpu.VMEM((t_blk, LANES), F32), pltpu.VMEM((t_blk, LANES), F32)],
        compiler_params=pltpu.CompilerParams(dimension_semantics=("parallel", "arbitrary"),
                                             vmem_limit_bytes=_vmem_limit(5 * t_blk * w2 * 4)),
    )(proj, states, states, d_yc, du_extra, bs, cs, lam, qw)


def _loss_head(y, target, t_m):
    seq, d = y.shape

    def body(y_ref, t_ref, loss_ref, dy_ref):
        @pl.when(pl.program_id(0) == 0)
        def _():
            loss_ref[...] = jnp.zeros_like(loss_ref)

        diff = y_ref[...] - t_ref[...]
        dy_ref[...] = diff / d
        loss_ref[...] += 0.5 * jnp.sum(diff * diff) / d

    row = _spec((t_m, d), lambda i: (i, 0))
    return pl.pallas_call(
        body, name="loss_head", grid=(seq // t_m,), in_specs=[row, row],
        out_specs=[_spec((SUBLANES, LANES), lambda i: (0, 0)), row],
        out_shape=[jax.ShapeDtypeStruct((SUBLANES, LANES), F32), jax.ShapeDtypeStruct((seq, d), F32)],
        compiler_params=pltpu.CompilerParams(dimension_semantics=("arbitrary",),
                                             vmem_limit_bytes=_vmem_limit(6 * t_m * d * 4)),
    )(y, target)


def _adamw_fn(w, m, v, *partials):
    g = partials[0]
    for p in partials[1:]:
        g = g + p
    m2 = ADAM_B1 * m + (1.0 - ADAM_B1) * g
    v2 = ADAM_B2 * v + (1.0 - ADAM_B2) * (g * g)
    m_hat = m2 / (1.0 - ADAM_B1 ** ADAM_STEP)
    v_hat = v2 / (1.0 - ADAM_B2 ** ADAM_STEP)
    delta = -ADAM_LR * (m_hat / (jnp.sqrt(v_hat) + ADAM_EPS) + ADAM_WD * w)
    return g, delta, m2, v2


def _adamw(name, w, m, v, partials):
    rows, cols = w.shape
    t_r = rows
    for cand in (512, 256, 128, 64, 32, 16, 8):
        if rows % cand == 0 and cand * cols * 4 <= (1 << 20):
            t_r = cand
            break
    n_p = partials.shape[0]
    row = lambda i: (i, 0)
    ins = [(a, (t_r, cols), row) for a in (w, m, v)]
    ins += [(partials, (None, t_r, cols), (lambda i, j=j: (j, i, 0))) for j in range(n_p)]
    outs = [((rows, cols), F32, (t_r, cols), row)] * 4
    return _rowwise(name, _adamw_fn, ins, outs, (rows // t_r,))


SMALL_PARAMS = ("b_ada", "ssm_a_re", "ssm_a_im", "ssm_log_dt", "ssm_b_re", "ssm_b_im", "ssm_c_re", "ssm_c_im",
                "ssm_d", "b_glu", "ln1_g", "ln1_b", "ln2_g", "ln2_b")
WEIGHTS = ("w_ada", "b_ada", "w_in", "w_sb_up", "ssm_a_re", "ssm_a_im", "ssm_log_dt", "ssm_b_re", "ssm_b_im",
           "ssm_c_re", "ssm_c_im", "ssm_d", "w_glu", "b_glu", "w_ssm_up", "w_out", "ln1_g", "ln1_b", "w_ffn_in",
           "w_ffn_out", "ln2_g", "ln2_b")
ARG_NAMES = (("x", "c") + WEIGHTS + ("loss_target",) + tuple("m_" + n for n in WEIGHTS)
             + tuple("v_" + n for n in WEIGHTS))


def _pack(arrs):
    flat = jnp.concatenate([a.reshape(-1) for a in arrs])
    pad = (-flat.shape[0]) % (PACK_ROWS * LANES)
    return jnp.pad(flat, (0, pad)).reshape(-1, LANES)


def _unpack(packed, like):
    lead = packed.shape[:-2]
    flat = packed.reshape(lead + (-1,))
    out, off = [], 0
    for a in like:
        out.append(flat[..., off:off + a.size].reshape(lead + a.shape))
        off += a.size
    return out


def kernel(x, c, w_ada, b_ada, w_in, w_sb_up, ssm_a_re, ssm_a_im, ssm_log_dt, ssm_b_re, ssm_b_im, ssm_c_re,
           ssm_c_im, ssm_d, w_glu, b_glu, w_ssm_up, w_out, ln1_g, ln1_b, w_ffn_in, w_ffn_out, ln2_g, ln2_b,
           loss_target, m_w_ada, m_b_ada, m_w_in, m_w_sb_up, m_ssm_a_re, m_ssm_a_im, m_ssm_log_dt, m_ssm_b_re,
           m_ssm_b_im, m_ssm_c_re, m_ssm_c_im, m_ssm_d, m_w_glu, m_b_glu, m_w_ssm_up, m_w_out, m_ln1_g, m_ln1_b,
           m_w_ffn_in, m_w_ffn_out, m_ln2_g, m_ln2_b, v_w_ada, v_b_ada, v_w_in, v_w_sb_up, v_ssm_a_re, v_ssm_a_im,
           v_ssm_log_dt, v_ssm_b_re, v_ssm_b_im, v_ssm_c_re, v_ssm_c_im, v_ssm_d, v_w_glu, v_b_glu, v_w_ssm_up,
           v_w_out, v_ln1_g, v_ln1_b, v_w_ffn_in, v_w_ffn_out, v_ln2_g, v_ln2_b):
    given = locals()
    return _train_step({n: given[n] for n in ARG_NAMES})


def _train_step(p):
    x0 = p["x"][0]
    target = p["loss_target"][0]
    seq, d = x0.shape
    depth = p["w_ada"].shape[0]
    n_ada = p["w_ada"].shape[2]
    n_in = p["w_in"].shape[2]
    sb_w = p["w_sb_up"].shape[1]
    ssm_w = p["w_ssm_up"].shape[1]
    n_up = p["w_sb_up"].shape[2]
    n_ffn = p["w_ffn_in"].shape[2]
    ffn = N_DEV * p["w_ffn_out"].shape[1]
    in_cols = N_DEV * n_in
    alpha = (2 * depth) ** 0.25
    resid_ln, resid_ln_mod = _make_resid_fns(alpha)
    t_r = min(512, seq)
    n_r = seq // t_r
    t_m = min(1024, seq)
    n_m = seq // t_m
    t_d = _tile(d)
    assert n_ffn * (N_DEV // 2) == ffn and sb_w % LANES == 0 and ssm_w % LANES == 0 and d % LANES == 0
    assert n_in % LANES == 0 and n_up % LANES == 0 and seq % t_m == 0 and in_cols == 3 * sb_w + ssm_w + 2 * d
    assert (3 * sb_w) % ssm_w == 0 and (3 * sb_w + ssm_w) % (2 * d) == 0

    bf = lambda a: a.astype(BF16)
    got = _exchange("gather_first", [], [bf(p["w_in"][0]), p["c"]])
    wg_in = [got[0]] + [None] * (depth - 1)
    c_all = got[1].reshape(N_DEV, d)
    small_names = ("w_sb_up", "w_ssm_up", "w_glu", "w_out")
    wg_ffn_in, wg_ffn_out, wg = [None] * depth, [None] * depth, {}

    c_pad = jnp.pad(c_all, ((0, 2 * SUBLANES - N_DEV), (0, 0)))
    c_act = _rowwise("silu_c", lambda v: v * jax.nn.sigmoid(v), [(c_pad, c_pad.shape, lambda i: (0, 0))],
                     [(c_pad.shape, F32, c_pad.shape, lambda i: (0, 0))], (1,))[0]
    rows_c = c_pad.shape[0]
    mod_cols = [
        _mm(f"mod_{l}", c_act, p["w_ada"],
            _spec((rows_c, d), lambda i, j, k: (0, 0)), _spec((None, d, n_ada), lambda i, j, k, l=l: (l, 0, 0)),
            _spec((rows_c, n_ada), lambda i, j, k: (0, 0)), (rows_c, n_ada), F32, (1, 1, 1), NN)
        for l in range(depth)]
    mod_send = jnp.stack([m[:N_DEV] for m in mod_cols], axis=1)
    mod_recv = _exchange("exchange_mod", [mod_send], [])[0]
    mod_nobias = jnp.swapaxes(mod_recv, 0, 1).reshape(depth, N_DEV * n_ada)
    full2 = lambda a: (a, a.shape, lambda i: (0, 0))
    mod = _rowwise("mod_bias", lambda a, b: a + b, [full2(mod_nobias), full2(p["b_ada"])],
                   [(mod_nobias.shape, F32, mod_nobias.shape, lambda i: (0, 0))], (1,))[0]
    vec = lambda a: a.reshape(1, -1)
    mods = [[vec(mod[l, j * d:(j + 1) * d]) for j in range(6)] for l in range(depth)]
    ln = {n: [vec(p[n][l]) for l in range(depth)] for n in ("ln1_g", "ln1_b", "ln2_g", "ln2_b")}

    row_spec = lambda width: ((t_r, width), lambda i: (i, 0))
    col_spec = lambda width, cb: ((t_r, width), lambda i, cb=cb: (i, cb))
    vec_spec = lambda width: ((1, width), lambda i: (0, 0))
    rows_in = lambda a: (a,) + row_spec(a.shape[1])
    vec_in = lambda a: (a,) + vec_spec(a.shape[1])
    row_out = lambda width, dt: ((seq, width), dt) + row_spec(width)

    s5 = [_s5_discretize(*[p[n][l] for n in ("ssm_a_re", "ssm_a_im", "ssm_log_dt", "ssm_b_re", "ssm_b_im",
                                               "ssm_c_re", "ssm_c_im")]) for l in range(depth)]
    s5_b16 = [(bs.astype(BF16), cs.astype(BF16), lam) for bs, cs, lam in s5]
    t_scan = min(512, seq)
    s5_pw = [_s5_powers(p["ssm_a_re"][l], p["ssm_a_im"][l], p["ssm_log_dt"][l], t_scan // SUBLANES)
             for l in range(depth)]
    u_col = 3 * sb_w // LANES
    gates_cb = (3 * sb_w + ssm_w) // (2 * d)
    ssm_d = [vec(p["ssm_d"][l]) for l in range(depth)]
    b_glu = [vec(p["b_glu"][l]) for l in range(depth)]
    n_half = N_DEV // 2

    h = _rowwise("modulate_in", _modulate, [rows_in(x0), vec_in(mods[0][1]), vec_in(mods[0][0])],
                 [row_out(d, BF16)], (n_r,))[0]
    saved = []
    x_cur = x0
    for l in range(depth):
        sv = {"x_in": x_cur, "h": h}
        last = l == depth - 1
        t_n = _tile(n_in)
        r_n = n_in // t_n
        proj = _mm(f"proj_{l}", h, wg_in[l],
                   _spec((t_m, d), lambda i, j, k: (i, 0)),
                   _spec((None, d, t_n), lambda i, j, k, r=r_n: (j // r, 0, j % r)),
                   _spec((t_m, t_n), lambda i, j, k: (i, j)), (seq, in_cols), F32, (n_m, N_DEV * r_n, 1), NN)
        arriving = [bf(p["w_ffn_in"][l]), bf(p["w_ffn_out"][l])] + ([bf(p[n]) for n in small_names] if l == 0 else [])
        (o_sb, o_sb32), got = _sb_attention_fwd(proj, sb_w, beside=_Exchange(gather=arriving))
        wg_ffn_in[l] = got[0]
        wg_ffn_out[l] = got[1].reshape(n_half, n_ffn, d)
        if l == 0:
            wg = dict(zip(small_names, got[2:]))
            for n in ("w_glu", "w_out"):
                wg[n] = jnp.swapaxes(wg[n], 0, 1).reshape(depth, -1, wg[n].shape[-1])
            for n in ("w_sb_up", "w_ssm_up"):
                wg[n] = jnp.transpose(wg[n], (1, 2, 0, 3)).reshape(depth, wg[n].shape[2], d)
        bs16, cs16, lam = s5_b16[l]
        (yc, states), got = _s5_scan_fwd(proj, u_col, bs16, cs16, lam, s5_pw[l][0], t_scan,
                                         beside=None if last else _Exchange(gather=[bf(p["w_in"][l + 1])]))
        if not last:
            wg_in[l + 1] = got[0]
        u_in = (proj,) + col_spec(ssm_w, 3 * sb_w // ssm_w)
        y1 = _rowwise(f"s5_act_{l}", _s5_act_fn, [rows_in(yc), u_in, vec_in(ssm_d[l])],
                      [row_out(ssm_w, BF16)], (n_r,))[0]
        t_glu = _mm(f"s5_glu_mm_{l}", y1, wg["w_glu"],
                    _spec((t_m, ssm_w), lambda i, j, k: (i, 0)), _spec((None, ssm_w, ssm_w), lambda i, j, k, l=l: (l, 0, 0)),
                    _spec((t_m, ssm_w), lambda i, j, k: (i, 0)), (seq, ssm_w), F32, (n_m, 1, 1), NN)
        s5_out = _rowwise(f"s5_glu_{l}", _s5_glu_fn,
                          [rows_in(yc), u_in, rows_in(t_glu), vec_in(ssm_d[l]), vec_in(b_glu[l])],
                          [row_out(ssm_w, BF16)], (n_r,))[0]

        def up_proj(name, a, w, l=l):
            return _mm(name, a, w, _spec((t_m, a.shape[1]), lambda i, j, k: (i, 0)),
                       _spec((None, a.shape[1], t_d), lambda i, j, k: (l, 0, j)),
                       _spec((t_m, t_d), lambda i, j, k: (i, j)), (seq, d), F32, (n_m, d // t_d, 1), NN)

        y_sb = up_proj(f"sb_up_{l}", o_sb, wg["w_sb_up"])
        y_ssm = up_proj(f"ssm_up_{l}", s5_out, wg["w_ssm_up"])
        gates = (proj,) + col_spec(2 * d, gates_cb)
        merged = _rowwise(f"merge_{l}", _merge_fn, [rows_in(y_sb), rows_in(y_ssm), gates],
                          [row_out(d, BF16)], (n_r,))[0]
        y_mix = _mm(f"out_proj_{l}", merged, wg["w_out"],
                    _spec((t_m, d), lambda i, j, k: (i, 0)), _spec((None, d, t_d), lambda i, j, k, l=l: (l, 0, j)),
                    _spec((t_m, t_d), lambda i, j, k: (i, j)), (seq, d), F32, (n_m, d // t_d, 1), NN)
        vecs_a = [mods[l][2], ln["ln1_g"][l], ln["ln1_b"][l], mods[l][4], mods[l][3]]
        x_mid, h2 = _rowwise(f"resid_mix_{l}", resid_ln_mod, [rows_in(x_cur), rows_in(y_mix)] + [vec_in(v) for v in vecs_a],
                             [row_out(d, F32), row_out(d, BF16)], (n_r,))
        a_ffn = _mm(f"ffn_in_{l}", h2, wg_ffn_in[l],
                    _spec((t_m, d), lambda i, j, k: (i, 0)), _spec((None, d, n_ffn), lambda i, j, k: (j, 0, 0)),
                    _spec((None, t_m, n_ffn), lambda i, j, k: (j, i, 0)), (N_DEV, seq, n_ffn), FFN_ACT, (n_m, N_DEV, 1), NN)
        pair_in = lambda a: (a.reshape(2, n_half, seq, n_ffn), (2, None, t_r, n_ffn), lambda j, i: (0, j, i, 0))
        f_act = _rowwise(f"swiglu_{l}", _swiglu_fn, [pair_in(a_ffn)],
                         [((n_half, seq, n_ffn), BF16, (None, t_r, n_ffn), lambda j, i: (j, i, 0))], (n_half, n_r))[0]
        y_ffn = _mm(f"ffn_out_{l}", f_act, wg_ffn_out[l],
                    _spec((None, t_m, n_ffn), lambda i, j, k: (k, i, 0)),
                    _spec((None, n_ffn, t_d), lambda i, j, k: (k, 0, j)),
                    _spec((t_m, t_d), lambda i, j, k: (i, j)), (seq, d), F32, (n_m, d // t_d, n_half), NN)
        vecs_b = [mods[l][5], ln["ln2_g"][l], ln["ln2_b"][l]] + ([] if last else [mods[l + 1][1], mods[l + 1][0]])
        outs_b = [row_out(d, F32)] + ([] if last else [row_out(d, BF16)])
        res = _rowwise(f"resid_ffn_{l}", resid_ln if last else resid_ln_mod,
                       [rows_in(x_mid), rows_in(y_ffn)] + [vec_in(v) for v in vecs_b], outs_b, (n_r,))
        sv.update(proj=proj, o_sb=o_sb, o_sb32=o_sb32, yc=yc, states=states, y1=y1, t_glu=t_glu, s5_out=s5_out,
                  y_sb=y_sb, y_ssm=y_ssm, merged=merged, y_mix=y_mix, x_mid=x_mid, h2=h2, a_ffn=a_ffn, f_act=f_act,
                  y_ffn=y_ffn, vecs_a=vecs_a, vecs_b=vecs_b)
        saved.append(sv)
        x_cur = res[0]
        h = None if last else res[1]

    loss_part, d_x = _loss_head(x_cur, target, t_r)
    loss = lax.psum(loss_part[0, 0], MESH_AXES)

    d_h_next = None
    grads = {n: [None] * depth for n in WEIGHTS}
    d_mod = [[None] * 6 for _ in range(depth)]
    land = {}
    waiting = []
    row_wrt = lambda i, width, dt: (i, "row", (seq, width), dt) + row_spec(width)
    sum_wrt = lambda i, width: (i, "sum", (1, width), F32) + vec_spec(width)
    for l in reversed(range(depth)):
        sv = saved[l]
        last = l == depth - 1
        ins_b = [rows_in(sv["x_mid"]), rows_in(sv["y_ffn"])] + [vec_in(v) for v in sv["vecs_b"]]
        cts_b = [rows_in(d_x)] + ([] if last else [rows_in(d_h_next)])
        wrt_b = [row_wrt(0, d, F32), row_wrt(1, d, BF16)] + [sum_wrt(2 + j, d) for j in range(len(sv["vecs_b"]))]
        res = _rowwise_vjp(f"resid_ffn_bwd_{l}", resid_ln if last else resid_ln_mod, ins_b, cts_b, wrt_b, (n_r,))
        d_x_mid, d_y_ffn = res[0], res[1]
        d_mod[l][5], grads["ln2_g"][l], grads["ln2_b"][l] = res[2], res[3], res[4]
        if not last:
            d_mod[l + 1][1], d_mod[l + 1][0] = res[5], res[6]
        d_f = _mm(f"ffn_out_dx_{l}", d_y_ffn, wg_ffn_out[l],
                  _spec((t_m, d), lambda i, j, k: (i, 0)), _spec((None, n_ffn, d), lambda i, j, k: (j, 0, 0)),
                  _spec((None, t_m, n_ffn), lambda i, j, k: (j, i, 0)), (n_half, seq, n_ffn), FFN_ACT, (n_m, n_half, 1), NT)
        g_ffn_out = _mm(f"ffn_out_dw_{l}", sv["f_act"], d_y_ffn,
                        _spec((None, t_m, n_ffn), lambda i, j, k: (i, k, 0)), _spec((t_m, t_d), lambda i, j, k: (k, j)),
                        _spec((None, n_ffn, t_d), lambda i, j, k: (i, 0, j)), (n_half, n_ffn, d), GRAD_WIRE,
                        (n_half, d // t_d, n_m), TN)
        pair_in = lambda a: (a.reshape(2, n_half, seq, n_ffn), (2, None, t_r, n_ffn), lambda j, i: (0, j, i, 0))
        d_a = _rowwise_vjp(f"swiglu_bwd_{l}", _swiglu_fn, [pair_in(sv["a_ffn"])],
                           [(d_f, (None, t_r, n_ffn), lambda j, i: (j, i, 0))],
                           [(0, "row", (2, n_half, seq, n_ffn), BF16, (2, None, t_r, n_ffn), lambda j, i: (0, j, i, 0))],
                           (n_half, n_r))[0].reshape(N_DEV, seq, n_ffn)
        d_h2 = _mm(f"ffn_in_dx_{l}", d_a, wg_ffn_in[l],
                   _spec((None, t_m, n_ffn), lambda i, j, k: (k, i, 0)),
                   _spec((None, t_d, n_ffn), lambda i, j, k: (k, j, 0)),
                   _spec((t_m, t_d), lambda i, j, k: (i, j)), (seq, d), F32, (n_m, d // t_d, N_DEV), NT)
        g_ffn_in = _mm(f"ffn_in_dw_{l}", sv["h2"], d_a,
                       _spec((t_m, t_d), lambda i, j, k: (k, j)), _spec((None, t_m, n_ffn), lambda i, j, k: (i, k, 0)),
                       _spec((None, t_d, n_ffn), lambda i, j, k: (i, j, 0)), (N_DEV, d, n_ffn), GRAD_WIRE,
                       (N_DEV, d // t_d, n_m), TN)
        ins_a = [rows_in(sv["x_in"]), rows_in(sv["y_mix"])] + [vec_in(v) for v in sv["vecs_a"]]
        wrt_a = [row_wrt(0, d, F32), row_wrt(1, d, BF16)] + [sum_wrt(2 + j, d) for j in range(5)]
        res = _rowwise_vjp(f"resid_mix_bwd_{l}", resid_ln_mod, ins_a, [rows_in(d_x_mid), rows_in(d_h2)], wrt_a, (n_r,))
        d_x_in, d_y_mix = res[0], res[1]
        d_mod[l][2], grads["ln1_g"][l], grads["ln1_b"][l], d_mod[l][4], d_mod[l][3] = res[2:7]
        d_merged = _mm(f"out_proj_dx_{l}", d_y_mix, wg["w_out"],
                       _spec((t_m, d), lambda i, j, k: (i, 0)), _spec((None, t_d, d), lambda i, j, k, l=l: (l, j, 0)),
                       _spec((t_m, t_d), lambda i, j, k: (i, j)), (seq, d), F32, (n_m, d // t_d, 1), NT)
        g_out = _mm(f"out_proj_dw_{l}", sv["merged"], d_y_mix,
                    _spec((t_m, t_d), lambda i, j, k: (k, i)), _spec((t_m, t_d), lambda i, j, k: (k, j)),
                    _spec((t_d, t_d), lambda i, j, k: (i, j)), (d, d), GRAD_WIRE, (d // t_d, d // t_d, n_m), TN)
        gates = (sv["proj"],) + col_spec(2 * d, gates_cb)
        d_y_sb, d_y_ssm, d_gates = _rowwise_vjp(
            f"merge_bwd_{l}", _merge_fn, [rows_in(sv["y_sb"]), rows_in(sv["y_ssm"]), gates], [rows_in(d_merged)],
            [row_wrt(0, d, BF16), row_wrt(1, d, BF16), row_wrt(2, 2 * d, BF16)], (n_r,))

        def up_bwd(name, act, d_y, w, dx_dtype, l=l):
            k_w = act.shape[1]
            dx = _mm(name + "_dx", d_y, w, _spec((t_m, d), lambda i, j, k: (i, 0)),
                     _spec((None, k_w, d), lambda i, j, k: (l, 0, 0)),
                     _spec((t_m, k_w), lambda i, j, k: (i, 0)), (seq, k_w), dx_dtype, (n_m, 1, 1), NT)
            dw = _mm(name + "_dw", act, d_y, _spec((t_m, k_w), lambda i, j, k: (k, 0)),
                     _spec((t_m, t_d), lambda i, j, k: (k, j)),
                     _spec((k_w, t_d), lambda i, j, k: (0, j)), (k_w, d), GRAD_WIRE, (1, d // t_d, n_m), TN)
            return dx, jnp.swapaxes(dw.reshape(k_w, N_DEV, n_up), 0, 1)

        d_o_sb, g_sb_up = up_bwd(f"sb_up_{l}", sv["o_sb"], d_y_sb, wg["w_sb_up"], BF16)
        d_s5_out, g_ssm_up = up_bwd(f"ssm_up_{l}", sv["s5_out"], d_y_ssm, wg["w_ssm_up"], F32)
        waiting += [("w_ffn_in", g_ffn_in), ("w_ffn_out", g_ffn_out.reshape(N_DEV, -1, d)),
                    ("w_out", g_out.reshape(N_DEV, -1, d)), ("w_sb_up", g_sb_up), ("w_ssm_up", g_ssm_up)]
        levels = [l + 1] * (len(waiting) - 5) + [l] * 5
        (d_q, d_k, d_v), got = _sb_attention_bwd(
            sv["proj"], sv["o_sb32"], d_o_sb, sb_w,
            beside=_Exchange(layered=[(g, lv, depth, land.get(n)) for (n, g), lv in zip(waiting, levels)]))
        land.update({n: buf for (n, _), buf in zip(waiting, got)})
        u_in = (sv["proj"],) + col_spec(ssm_w, 3 * sb_w // ssm_w)
        ins_s5 = [rows_in(sv["yc"]), u_in, rows_in(sv["t_glu"]), vec_in(ssm_d[l]), vec_in(b_glu[l])]
        d_t = _rowwise_vjp(f"s5_glu_bwd_{l}", _s5_glu_fn, ins_s5, [rows_in(d_s5_out)],
                           [row_wrt(2, ssm_w, BF16)], (n_r,))[0]
        d_y1 = _mm(f"s5_glu_mm_dx_{l}", d_t, wg["w_glu"],
                   _spec((t_m, ssm_w), lambda i, j, k: (i, 0)), _spec((None, ssm_w, ssm_w), lambda i, j, k, l=l: (l, 0, 0)),
                   _spec((t_m, ssm_w), lambda i, j, k: (i, 0)), (seq, ssm_w), F32, (n_m, 1, 1), NT)
        g_glu = _mm(f"s5_glu_mm_dw_{l}", sv["y1"], d_t,
                    _spec((t_m, ssm_w), lambda i, j, k: (k, 0)), _spec((t_m, ssm_w), lambda i, j, k: (k, 0)),
                    _spec((ssm_w, ssm_w), lambda i, j, k: (0, 0)), (ssm_w, ssm_w), GRAD_WIRE, (1, 1, n_m), TN)
        d_yc, d_u_skip, grads["ssm_d"][l], grads["b_glu"][l] = _rowwise_vjp(
            f"s5_post_bwd_{l}", _s5_post_fn, ins_s5, [rows_in(d_y1), rows_in(d_s5_out)],
            [row_wrt(0, ssm_w, F32), row_wrt(1, ssm_w, F32), sum_wrt(3, ssm_w), sum_wrt(4, ssm_w)], (n_r,))
        bs16, cs16, lam = s5_b16[l]
        d_u, d_bs, d_cs, d_lam = _s5_scan_bwd(sv["proj"], u_col, sv["states"], d_yc, d_u_skip, bs16, cs16, lam,
                                              s5_pw[l][1], t_scan)
        raw = [p[n][l] for n in ("ssm_a_re", "ssm_a_im", "ssm_log_dt", "ssm_b_re", "ssm_b_im", "ssm_c_re", "ssm_c_im")]
        _, pull = jax.vjp(_s5_discretize, *raw)
        (grads["ssm_a_re"][l], grads["ssm_a_im"][l], grads["ssm_log_dt"][l], grads["ssm_b_re"][l],
         grads["ssm_b_im"][l], grads["ssm_c_re"][l], grads["ssm_c_im"][l]) = pull((d_bs, d_cs, d_lam))
        d_proj = jnp.concatenate([d_q, d_k.astype(BF16), d_v.astype(BF16), d_u.astype(BF16), d_gates], axis=1)
        t_n = _tile(n_in)
        d_h = _mm(f"proj_dx_{l}", d_proj, wg_in[l],
                  _spec((t_m, n_in), lambda i, j, k: (i, k)), _spec((None, t_d, n_in), lambda i, j, k: (k, j, 0)),
                  _spec((t_m, t_d), lambda i, j, k: (i, j)), (seq, d), F32, (n_m, d // t_d, N_DEV), NT)
        g_in = _mm(f"proj_dw_{l}", sv["h"], d_proj,
                   _spec((t_m, t_d), lambda i, j, k: (k, j)), _spec((t_m, n_in), lambda i, j, k: (k, i)),
                   _spec((None, t_d, n_in), lambda i, j, k: (i, j, 0)), (N_DEV, d, n_in), GRAD_WIRE,
                   (N_DEV, d // t_d, n_m), TN)
        waiting = [("w_in", g_in), ("w_glu", g_glu.reshape(N_DEV, -1, ssm_w))]
        d_x, d_h_next = d_x_in, d_h
    res = _rowwise_vjp("modulate_in_bwd", lambda v, sc, sh: (v, _modulate(v, sc, sh)),
                       [rows_in(x0), vec_in(mods[0][1]), vec_in(mods[0][0])], [rows_in(d_x), rows_in(d_h_next)],
                       [row_wrt(0, d, F32), sum_wrt(1, d), sum_wrt(2, d)], (n_r,))
    grad_x, d_mod[0][1], d_mod[0][0] = res

    d_mod_rows = jnp.concatenate([jnp.concatenate(d_mod[l], axis=1) for l in range(depth)], axis=0)
    grads["b_ada"] = [d_mod_rows[l] for l in range(depth)]
    small_local = [jnp.stack([g.reshape(p[n].shape[1:]) for g in grads[n]]) for n in SMALL_PARAMS]
    d_mod_send = jnp.swapaxes(d_mod_rows.reshape(depth, N_DEV, n_ada), 0, 1)
    recv = _exchange("exchange_last", [d_mod_send], [_pack(small_local)],
                     layered=[(g, 0, depth, land.get(n)) for n, g in waiting])
    d_mod_cols, small_all = recv[0], recv[-1]
    land.update({n: buf for (n, _), buf in zip(waiting, recv[1:-1])})
    d_mod_pad = jnp.pad(jnp.swapaxes(d_mod_cols, 0, 1), ((0, 0), (0, rows_c - N_DEV), (0, 0)))
    g_ada = [
        _mm(f"mod_dw_{l}", c_act, d_mod_pad,
            _spec((rows_c, d), lambda i, j, k: (0, 0)), _spec((None, rows_c, n_ada), lambda i, j, k, l=l: (l, 0, 0)),
            _spec((d, n_ada), lambda i, j, k: (0, 0)), (d, n_ada), F32, (1, 1, 1), TN)
        for l in range(depth)]

    out = {}

    def update(name, partials):
        shape = p[name].shape
        two_d = lambda a: a.reshape(-1, shape[-1])
        res = _adamw("adamw_" + name, two_d(p[name]), two_d(p["m_" + name]), two_d(p["v_" + name]),
                     partials.reshape(partials.shape[0], -1, shape[-1]))
        out[name] = [r.reshape(shape) for r in res]

    update("w_ada", jnp.stack(g_ada)[None])
    for n in ("w_in", "w_sb_up", "w_ssm_up", "w_ffn_in", "w_glu", "w_out", "w_ffn_out"):
        update(n, land[n])
    small_w = [p[n] for n in SMALL_PARAMS]
    res = _adamw("adamw_small", _pack(small_w), _pack([p["m_" + n] for n in SMALL_PARAMS]),
                 _pack([p["v_" + n] for n in SMALL_PARAMS]), small_all)
    for kind, packed in enumerate(res):
        for n, a in zip(SMALL_PARAMS, _unpack(packed, small_w)):
            out.setdefault(n, [None] * 4)[kind] = a

    return ((loss, grad_x[None]) + tuple(out[n][0] for n in WEIGHTS) + tuple(out[n][1] for n in WEIGHTS)
            + tuple(out[n][2] for n in WEIGHTS) + tuple(out[n][3] for n in WEIGHTS))
```

```python
import jax
import jax.numpy as jnp
from jax import lax
from jax.experimental import pallas as pl
from jax.experimental.pallas import tpu as pltpu

F32 = jnp.float32
BF16 = jnp.bfloat16
GRAD_WIRE = BF16
FFN_ACT = BF16

N_DEV = 8
LANES = 128
SUBLANES = 8
VMEM_BYTES = 64 * 1024 * 1024
HEAD_DIM = 64
SB_BLOCK = 256
SLAB_GROUPS = 8
LN_EPS = 1e-5
ADAM_LR, ADAM_B1, ADAM_B2, ADAM_EPS, ADAM_WD, ADAM_STEP = 0.001, 0.9, 0.999, 1e-08, 0.01, 10
SB_UNDERFLOW = -120.0

PACK_ROWS = 256
MESH_AXES = ("x", "y", "c")


def _vmem_limit(block_bytes):
    return int(min(max(3 * block_bytes + (8 << 20), 24 << 20), VMEM_BYTES - (8 << 20)))


def _nbytes(shape, dtype):
    n = 1
    for d in shape:
        if d is not None:
            n *= d
    return n * jnp.dtype(dtype).itemsize


def _spec(shape, fn):
    return pl.BlockSpec(shape, fn)


class _Exchange:
    def __init__(self, scatter=(), gather=(), layered=()):
        self.arrs = list(scatter) + [a for a, _, _, _ in layered] + list(gather)
        self.n = len(self.arrs)
        self.n_sc = len(scatter) + len(layered)
        self.layer = [None] * len(scatter) + [l for _, l, _, _ in layered] + [None] * len(gather)
        self.shapes = ([a.shape for a in scatter] + [(N_DEV, dp) + a.shape[1:] for a, _, dp, _ in layered]
                       + [(N_DEV,) + a.shape for a in gather])
        self.held = [(len(scatter) + i, b) for i, (_, _, _, b) in enumerate(layered) if b is not None]
        self.operands = self.arrs + [b for _, b in self.held]
        hbm = pl.BlockSpec(memory_space=pltpu.HBM)
        self.in_specs = [hbm] * len(self.operands)
        self.out_specs = [hbm] * self.n
        self.out_shape = [jax.ShapeDtypeStruct(s, a.dtype) for s, a in zip(self.shapes, self.arrs)]
        self.scratch = [pltpu.SemaphoreType.DMA((self.n, N_DEV - 1)), pltpu.SemaphoreType.DMA((self.n, N_DEV - 1)),
                        pltpu.SemaphoreType.DMA((self.n,))]

    def aliases(self, first_in, first_out):
        return {first_in + self.n + i: first_out + a for i, (a, _) in enumerate(self.held)}

    def copies(self, ins, outs, sems):
        send_sems, recv_sems, own_sems = sems
        x, y, c = lax.axis_index("x"), lax.axis_index("y"), lax.axis_index("c")
        me = 4 * x + 2 * y + c
        landing = [outs[a].at[me] if self.layer[a] is None else outs[a].at[me, self.layer[a]] for a in range(self.n)]
        out = [pltpu.make_async_copy(ins[a].at[me] if a < self.n_sc else ins[a], landing[a], own_sems.at[a])
               for a in range(self.n)]
        for k in range(1, N_DEV):
            px = 1 - x if k & 4 else x
            py = 1 - y if k & 2 else y
            pc = 1 - c if k & 1 else c
            peer = 4 * px + 2 * py + pc
            for a in range(self.n):
                out.append(pltpu.make_async_remote_copy(
                    src_ref=ins[a].at[peer] if a < self.n_sc else ins[a], dst_ref=landing[a],
                    send_sem=send_sems.at[a, k - 1], recv_sem=recv_sems.at[a, k - 1],
                    device_id=(px, py, pc), device_id_type=pl.DeviceIdType.MESH))
        return out


def _exchange(name, scatter, gather, layered=()):
    ex = _Exchange(scatter, gather, layered)

    def body(*refs):
        copies = ex.copies(refs[:ex.n], refs[len(ex.operands):len(ex.operands) + ex.n], refs[-3:])
        for cp in copies:
            cp.start()
        for cp in copies:
            cp.wait()

    return pl.pallas_call(body, name=name, in_specs=ex.in_specs, out_specs=ex.out_specs, out_shape=ex.out_shape,
                          input_output_aliases=ex.aliases(0, 0), scratch_shapes=ex.scratch)(*ex.operands)


def _call_beside(ex, body, name, grid, in_specs, out_specs, out_shape, scratch_shapes, vmem_bytes, operands,
                 semantics):
    operands = [_in_hbm(a) for a in operands]
    if ex is None:
        res = pl.pallas_call(
            body, name=name, grid=grid, in_specs=in_specs, out_specs=out_specs, out_shape=out_shape,
            scratch_shapes=scratch_shapes,
            compiler_params=pltpu.CompilerParams(dimension_semantics=semantics, vmem_limit_bytes=vmem_bytes),
        )(*operands)
        return res, None
    n_in, n_out, n_scr = len(in_specs), len(out_specs), len(scratch_shapes)
    n_xin = len(ex.operands)

    def fused(*refs):
        mine = refs[:n_in] + refs[n_in + n_xin:n_in + n_xin + n_out]
        mine += refs[n_in + n_xin + n_out + ex.n:n_in + n_xin + n_out + ex.n + n_scr]
        first = pl.program_id(0) == 0
        last = pl.program_id(0) == grid[0] - 1
        for dim in range(1, len(grid)):
            first = jnp.logical_and(first, pl.program_id(dim) == 0)
            last = jnp.logical_and(last, pl.program_id(dim) == grid[dim] - 1)
        x_ins = refs[n_in:n_in + ex.n]
        x_outs = refs[n_in + n_xin + n_out:n_in + n_xin + n_out + ex.n]

        @pl.when(first)
        def _():
            for cp in ex.copies(x_ins, x_outs, refs[-3:]):
                cp.start()

        body(*mine)

        @pl.when(last)
        def _():
            for cp in ex.copies(x_ins, x_outs, refs[-3:]):
                cp.wait()

    res = pl.pallas_call(
        fused, name=name, grid=grid, in_specs=list(in_specs) + ex.in_specs, out_specs=list(out_specs) + ex.out_specs,
        out_shape=list(out_shape) + ex.out_shape, input_output_aliases=ex.aliases(n_in, n_out),
        scratch_shapes=list(scratch_shapes) + ex.scratch,
        compiler_params=pltpu.CompilerParams(dimension_semantics=("arbitrary",) * len(grid),
                                             vmem_limit_bytes=vmem_bytes),
    )(*operands, *ex.operands)
    return res[:n_out], res[n_out:]


NN = (((1,), (0,)), ((), ()))
NT = (((1,), (1,)), ((), ()))
TN = (((0,), (0,)), ((), ()))


def _in_hbm(a):
    return pltpu.with_memory_space_constraint(a, pltpu.HBM)


def _mm(name, a, b, a_spec, b_spec, o_spec, o_shape, o_dtype, grid, dims, beside=None):
    nk = grid[2]
    acc_shape = tuple(d for d in o_spec.block_shape if d is not None)

    def product(a_ref, b_ref):
        return lax.dot_general(a_ref[...].astype(BF16), b_ref[...].astype(BF16), dims, preferred_element_type=F32)

    def body_once(a_ref, b_ref, o_ref):
        o_ref[...] = product(a_ref, b_ref).astype(o_ref.dtype)

    def body(a_ref, b_ref, o_ref, acc_ref):
        k = pl.program_id(2)

        @pl.when(k == 0)
        def _():
            acc_ref[...] = product(a_ref, b_ref)

        @pl.when(k > 0)
        def _():
            acc_ref[...] += product(a_ref, b_ref)

        @pl.when(k == nk - 1)
        def _():
            o_ref[...] = acc_ref[...].astype(o_ref.dtype)

    blk = (_nbytes(a_spec.block_shape, a.dtype) + _nbytes(b_spec.block_shape, b.dtype)
           + _nbytes(acc_shape, o_dtype) + _nbytes(acc_shape, F32))
    res, got = _call_beside(
        beside, body_once if nk == 1 else body, name, grid, [a_spec, b_spec], [o_spec],
        [jax.ShapeDtypeStruct(o_shape, o_dtype)], [] if nk == 1 else [pltpu.VMEM(acc_shape, F32)],
        _vmem_limit(blk), (a, b), ("parallel", "parallel", "arbitrary"))
    return res[0] if beside is None else (res[0], got)


def _tile(n, pref=1024):
    t = pref
    while t >= LANES:
        if n % t == 0:
            return t
        t -= LANES
    return n


def _rowwise(name, fn, ins, outs, grid):
    n_in = len(ins)

    def body(*refs):
        vals = fn(*[r[...].astype(F32) for r in refs[:n_in]])
        if not isinstance(vals, (tuple, list)):
            vals = (vals,)
        for r, v in zip(refs[n_in:], vals):
            r[...] = v.astype(r.dtype)

    blk = sum(_nbytes(bs, a.dtype) for a, bs, _ in ins) + sum(_nbytes(bs, d) + _nbytes(bs, F32) for _, d, bs, _ in outs)
    return pl.pallas_call(
        body, name=name, grid=grid,
        in_specs=[_spec(bs, im) for _, bs, im in ins],
        out_specs=[_spec(bs, im) for _, _, bs, im in outs],
        out_shape=[jax.ShapeDtypeStruct(s, d) for s, d, _, _ in outs],
        compiler_params=pltpu.CompilerParams(dimension_semantics=("parallel",) * len(grid),
                                             vmem_limit_bytes=_vmem_limit(2 * blk)),
    )(*[_in_hbm(a) for a, _, _ in ins])


def _rowwise_vjp(name, fn, ins, cts, wrt, grid):
    n_in, n_ct = len(ins), len(cts)
    idx = [w[0] for w in wrt]

    def body(*refs):
        prim = [r[...].astype(F32) for r in refs[:n_in]]
        ct = tuple(r[...].astype(F32) for r in refs[n_in:n_in + n_ct])
        o_refs = refs[n_in + n_ct:]

        def g(*sel):
            full = list(prim)
            for i, s in zip(idx, sel):
                full[i] = s
            out = fn(*full)
            return tuple(out) if isinstance(out, (tuple, list)) else (out,)

        _, pull = jax.vjp(g, *[prim[i] for i in idx])
        grads = pull(ct)
        first = pl.program_id(0) == 0
        for d in range(1, len(grid)):
            first = jnp.logical_and(first, pl.program_id(d) == 0)
        for w, o_ref, gr in zip(wrt, o_refs, grads):
            if w[1] == "row":
                o_ref[...] = gr.astype(o_ref.dtype)
            else:
                @pl.when(first)
                def _(o_ref=o_ref):
                    o_ref[...] = jnp.zeros_like(o_ref)

                o_ref[...] += gr.astype(o_ref.dtype)

    blk = (sum(_nbytes(bs, a.dtype) + _nbytes(bs, F32) for a, bs, _ in list(ins) + list(cts))
           + sum(_nbytes(w[4], w[3]) + _nbytes(w[4], F32) for w in wrt))
    return pl.pallas_call(
        body, name=name, grid=grid,
        in_specs=[_spec(bs, im) for _, bs, im in list(ins) + list(cts)],
        out_specs=[_spec(w[4], w[5]) for w in wrt],
        out_shape=[jax.ShapeDtypeStruct(w[2], w[3]) for w in wrt],
        compiler_params=pltpu.CompilerParams(dimension_semantics=("arbitrary",) * len(grid),
                                             vmem_limit_bytes=_vmem_limit(2 * blk)),
    )(*[_in_hbm(a) for a, _, _ in list(ins) + list(cts)])


def _normalize(x):
    mu = jnp.mean(x, axis=-1, keepdims=True)
    xc = x - mu
    var = jnp.mean(xc * xc, axis=-1, keepdims=True)
    return xc * lax.rsqrt(var + LN_EPS)


def _modulate(x, sc, sh):
    return _normalize(x) * (1.0 + sc) + sh


def _make_resid_fns(alpha):
    def resid_ln(x, y, gate, g, b):
        return _normalize(alpha * x + (1.0 + gate) * y) * g + b

    def resid_ln_mod(x, y, gate, g, b, sc, sh):
        xn = resid_ln(x, y, gate, g, b)
        return xn, _modulate(xn, sc, sh)

    return resid_ln, resid_ln_mod


def _merge_fn(y_sb, y_ssm, gates):
    half = gates.shape[-1] // 2
    return jax.nn.sigmoid(gates[:, :half]) * y_sb + jax.nn.sigmoid(gates[:, half:]) * y_ssm


def _swiglu_fn(gate_up):
    gate, up = gate_up[0], gate_up[1]
    return gate * jax.nn.sigmoid(gate) * up


def _s5_act_fn(yc, u, d_skip):
    return jax.nn.gelu(yc + d_skip * u)


def _s5_glu_fn(yc, u, t, d_skip, b_glu):
    return _s5_act_fn(yc, u, d_skip) * jax.nn.sigmoid(t + b_glu)


def _s5_post_fn(yc, u, t, d_skip, b_glu):
    y1 = _s5_act_fn(yc, u, d_skip)
    return y1, y1 * jax.nn.sigmoid(t + b_glu)


def _sb_tri(kind):
    row = lax.broadcasted_iota(jnp.int32, (SB_BLOCK, SB_BLOCK), 0)
    col = lax.broadcasted_iota(jnp.int32, (SB_BLOCK, SB_BLOCK), 1)
    if kind == "after":
        return (row > col).astype(BF16)
    if kind == "from":
        return (row >= col).astype(BF16)
    return col < row


def _split_dot(x, m):
    hi = x.astype(BF16)
    lo = (x - hi.astype(F32)).astype(BF16)
    return (lax.dot_general(hi, m, NN, preferred_element_type=F32)
            + lax.dot_general(lo, m, NN, preferred_element_type=F32))


def _sb_scores(qh, k2):
    z = lax.dot_general(qh, k2, NT, preferred_element_type=F32)
    log_beta = jnp.minimum(z, 0.0) - jnp.log(1.0 + jnp.exp(-jnp.abs(z)))
    return log_beta, log_beta - z


def _sb_attention_fwd(proj, sb_width, beside=None):
    seq = proj.shape[0]
    n_pair, n_q = sb_width // LANES, seq // SB_BLOCK
    scale = 1.0 / (HEAD_DIM ** 0.5)

    def body(q_ref, k_ref, v_ref, o_ref, o32_ref):
        qi = pl.program_id(1)
        q2 = q_ref[...]
        lane = lax.broadcasted_iota(jnp.int32, (SB_BLOCK, LANES), 1)
        m_after, causal = _sb_tri("after"), _sb_tri("mask")
        heads = [lane < HEAD_DIM, lane >= HEAD_DIM]
        qh = [(jnp.where(m, q2, 0.0) * scale).astype(BF16) for m in heads]

        def scores(kb, diag):
            ks = pl.multiple_of(kb * SB_BLOCK, SB_BLOCK)
            k2 = k_ref[pl.ds(ks, SB_BLOCK), :].astype(BF16)
            out = []
            for h in range(2):
                log_beta, log_1m = _sb_scores(qh[h], k2)
                if diag:
                    log_1m = jnp.where(causal, log_1m, 0.0)
                out += [log_beta + _split_dot(log_1m, m_after), jnp.sum(log_1m, axis=1, keepdims=True)]
            return tuple(out)

        def weigh(kb, sc, carry, acc, diag):
            ks = pl.multiple_of(kb * SB_BLOCK, SB_BLOCK)
            v2 = v_ref[pl.ds(ks, SB_BLOCK), :].astype(BF16)
            out = []
            for h in range(2):
                w = jnp.exp(sc[2 * h] + carry[h])
                if diag:
                    w = jnp.where(causal, w, 0.0)
                out.append(acc[h] + lax.dot_general(w.astype(BF16), v2, NN, preferred_element_type=F32))
            return tuple(out)

        zero = jnp.zeros((SB_BLOCK, LANES), F32)
        zcol = jnp.zeros((SB_BLOCK, 1), F32)
        sc = scores(qi, True)
        acc = weigh(qi, sc, (zcol, zcol), (zero, zero), True)
        carry = (sc[1], sc[3])
        sc = scores(jnp.maximum(qi - 1, 0), False)

        def loop(st):
            kb, sc, carry, acc = st
            after = (carry[0] + sc[1], carry[1] + sc[3])
            done = jnp.maximum(jnp.max(after[0]), jnp.max(after[1])) < SB_UNDERFLOW
            sc_next = scores(jnp.maximum(kb - 1, 0), False)
            acc = weigh(kb, sc, carry, acc, False)
            return jnp.where(done, -1, kb - 1), sc_next, after, acc

        _, _, _, acc = lax.while_loop(lambda st: st[0] >= 0, loop, (qi - 1, sc, carry, acc))
        out = jnp.where(heads[0], acc[0], acc[1])
        o_ref[...] = out.astype(o_ref.dtype)
        o32_ref[...] = out

    q_spec = _spec((SB_BLOCK, LANES), lambda h, i: (i, h))
    kv = [_spec((seq, LANES), lambda h, i, o=o: (0, o + h)) for o in (n_pair, 2 * n_pair)]
    o_spec = _spec((SB_BLOCK, LANES), lambda h, i: (i, h))
    return _call_beside(
        beside, body, "sb_attention_fwd", (n_pair, n_q), [q_spec] + kv, [o_spec, o_spec],
        [jax.ShapeDtypeStruct((seq, sb_width), BF16), jax.ShapeDtypeStruct((seq, sb_width), F32)], [],
        _vmem_limit(2 * seq * LANES * 4), (proj, proj, proj), ("parallel", "arbitrary"))


def _sb_attention_bwd(proj, o32, do, sb_width, beside=None):
    seq = proj.shape[0]
    n_pair, n_q = sb_width // LANES, seq // SB_BLOCK
    scale = 1.0 / (HEAD_DIM ** 0.5)

    def body(q_ref, k_ref, v_ref, o_ref, do_ref, dq_ref, dk_ref, dv_ref):
        qi = pl.program_id(1)

        @pl.when(qi == 0)
        def _():
            dk_ref[...] = jnp.zeros_like(dk_ref)
            dv_ref[...] = jnp.zeros_like(dv_ref)

        q2 = q_ref[...]
        do2 = do_ref[...].astype(F32)
        o2 = o_ref[...]
        lane = lax.broadcasted_iota(jnp.int32, (SB_BLOCK, LANES), 1)
        m_after, m_from, causal = _sb_tri("after"), _sb_tri("from"), _sb_tri("mask")
        heads = [lane < HEAD_DIM, lane >= HEAD_DIM]
        qh = [(jnp.where(m, q2, 0.0) * scale).astype(BF16) for m in heads]
        doh = [jnp.where(m, do2, 0.0) for m in heads]
        doh_b = [v.astype(BF16) for v in doh]
        total = [jnp.sum(v * o2, axis=1, keepdims=True) for v in doh]

        def scores(kb, diag):
            ks = pl.multiple_of(kb * SB_BLOCK, SB_BLOCK)
            k2 = k_ref[pl.ds(ks, SB_BLOCK), :].astype(BF16)
            v2 = v_ref[pl.ds(ks, SB_BLOCK), :].astype(BF16)
            out = []
            for h in range(2):
                log_beta, log_1m = _sb_scores(qh[h], k2)
                if diag:
                    log_1m = jnp.where(causal, log_1m, 0.0)
                out += [log_beta + _split_dot(log_1m, m_after), jnp.sum(log_1m, axis=1, keepdims=True),
                        lax.dot_general(doh_b[h], v2, NT, preferred_element_type=F32), log_beta]
            return tuple(out)

        def pull(kb, sc, carry, right, dq, diag):
            ks = pl.multiple_of(kb * SB_BLOCK, SB_BLOCK)
            k2 = k_ref[pl.ds(ks, SB_BLOCK), :].astype(BF16)
            dv_blk, dk_blk, right_out, dq_out = None, None, [], []
            for h in range(2):
                arg, _, d_w, log_beta = sc[4 * h:4 * h + 4]
                w = jnp.exp(arg + carry[h])
                if diag:
                    w = jnp.where(causal, w, 0.0)
                w_b = w.astype(BF16)
                d_arg = d_w * w_b.astype(F32)
                dv_h = lax.dot_general(w_b, doh_b[h], TN, preferred_element_type=F32)
                d_log_1m = total[h] - right[h] - _split_dot(d_arg, m_from)
                beta = jnp.exp(log_beta)
                dz = d_arg * (1.0 - beta) - beta * d_log_1m
                if diag:
                    dz = jnp.where(causal, dz, 0.0)
                dz_b = dz.astype(BF16)
                dk_h = lax.dot_general(dz_b, qh[h], TN, preferred_element_type=F32)
                dv_blk = dv_h if h == 0 else dv_blk + dv_h
                dk_blk = dk_h if h == 0 else dk_blk + dk_h
                dq_out.append(dq[h] + lax.dot_general(dz_b, k2, NN, preferred_element_type=F32))
                right_out.append(right[h] + jnp.sum(d_arg, axis=1, keepdims=True))
            dv_ref[pl.ds(ks, SB_BLOCK), :] += dv_blk
            dk_ref[pl.ds(ks, SB_BLOCK), :] += dk_blk
            return tuple(right_out), tuple(dq_out)

        zero = jnp.zeros((SB_BLOCK, LANES), F32)
        zcol = jnp.zeros((SB_BLOCK, 1), F32)
        sc = scores(qi, True)
        right, dq = pull(qi, sc, (zcol, zcol), (zcol, zcol), (zero, zero), True)
        carry = (sc[1], sc[5])
        sc = scores(jnp.maximum(qi - 1, 0), False)

        def loop(st):
            kb, sc, carry, right, dq = st
            after = (carry[0] + sc[1], carry[1] + sc[5])
            done = jnp.maximum(jnp.max(after[0]), jnp.max(after[1])) < SB_UNDERFLOW
            sc_next = scores(jnp.maximum(kb - 1, 0), False)
            right, dq = pull(kb, sc, carry, right, dq, False)
            return jnp.where(done, -1, kb - 1), sc_next, after, right, dq

        _, _, _, _, dq = lax.while_loop(lambda st: st[0] >= 0, loop, (qi - 1, sc, carry, right, dq))
        dq_ref[...] = (jnp.where(heads[0], dq[0], dq[1]) * scale).astype(dq_ref.dtype)

    q_spec = _spec((SB_BLOCK, LANES), lambda h, i: (i, h))
    kv = [_spec((seq, LANES), lambda h, i, o=o: (0, o + h)) for o in (n_pair, 2 * n_pair)]
    full = _spec((seq, LANES), lambda h, i: (0, h))
    return _call_beside(
        beside, body, "sb_attention_bwd", (n_pair, n_q), [q_spec] + kv + [q_spec, q_spec], [q_spec, full, full],
        [jax.ShapeDtypeStruct((seq, sb_width), BF16), jax.ShapeDtypeStruct((seq, sb_width), F32),
         jax.ShapeDtypeStruct((seq, sb_width), F32)], [],
        _vmem_limit(4 * seq * LANES * 4), (proj, proj, proj, o32, do), ("parallel", "arbitrary"))


def _s5_discretize(a_re, a_im, log_dt, b_re, b_im, c_re, c_im):
    n_g, n_p = a_re.shape
    c_g = b_re.shape[-1]
    ns = n_g // SLAB_GROUPS
    dt = jnp.exp(log_dt)[:, None]
    xr, xi = a_re * dt, a_im * dt
    mag = jnp.exp(xr)
    lr, li = mag * jnp.cos(xi), mag * jnp.sin(xi)
    den = a_re * a_re + a_im * a_im
    fr = ((lr - 1.0) * a_re + li * a_im) / den
    fi = (li * a_re - (lr - 1.0) * a_im) / den
    bb_re = fr[..., None] * b_re - fi[..., None] * b_im
    bb_im = fr[..., None] * b_im + fi[..., None] * b_re
    eye = jnp.eye(SLAB_GROUPS, dtype=F32)

    def diag_b(m):
        m = jnp.transpose(m.reshape(ns, SLAB_GROUPS, n_p, c_g), (0, 1, 3, 2))
        m = m[:, :, :, None, :] * eye[None, :, None, :, None]
        return m.reshape(ns, SLAB_GROUPS * c_g, SLAB_GROUPS * n_p)

    def diag_c(m):
        m = jnp.transpose(m.reshape(ns, SLAB_GROUPS, c_g, n_p), (0, 1, 3, 2))
        m = m[:, :, :, None, :] * eye[None, :, None, :, None]
        return m.reshape(ns, SLAB_GROUPS * n_p, SLAB_GROUPS * c_g)

    bs = jnp.concatenate([diag_b(bb_re), diag_b(bb_im)], axis=-1)
    cs = jnp.concatenate([diag_c(c_re), -diag_c(c_im)], axis=1)
    lam = jnp.concatenate([lr.reshape(ns, 1, -1), li.reshape(ns, 1, -1)], axis=-1)
    return bs, cs, lam


def _s5_powers(a_re, a_im, log_dt, n):
    n_g, n_p = a_re.shape
    ns = n_g // SLAB_GROUPS
    dt = jnp.exp(log_dt)[:, None]
    mag = jnp.exp(a_re * dt)
    base_r, base_i = mag * jnp.cos(a_im * dt), mag * jnp.sin(a_im * dt)
    steps = jnp.arange(1, n + 1, dtype=jnp.int32)[:, None, None]
    pr, pi = jnp.ones((n, n_g, n_p), F32), jnp.zeros((n, n_g, n_p), F32)
    for b in range(n.bit_length()):
        take = ((steps >> b) & 1) == 1
        pr, pi = (jnp.where(take, pr * base_r - pi * base_i, pr), jnp.where(take, pr * base_i + pi * base_r, pi))
        base_r, base_i = base_r * base_r - base_i * base_i, 2.0 * base_r * base_i

    def slabs(re, im):
        one = lambda m: jnp.transpose(m.reshape(n, ns, SLAB_GROUPS * n_p), (1, 0, 2))
        return jnp.concatenate([one(re), one(im)], axis=-1)

    return slabs(pr, pi), slabs(pr[::-1], -pi[::-1])


def _lanes(j):
    return slice(j * LANES, (j + 1) * LANES)


def _tile8(k):
    return pl.ds(pl.multiple_of(k * SUBLANES, SUBLANES), SUBLANES)


def _s5_interleave(dst_ref, src_ref, t_seg):
    def body(k, _):
        dst_ref[_tile8(k), :] = src_ref[pl.ds(k, SUBLANES, stride=t_seg), :]
        return 0

    lax.fori_loop(0, t_seg, body, 0, unroll=4)


def _s5_join_segments(st_ref, end_ref, car_ref, tab_ref, row, order, n_pair):
    for j in range(n_pair):
        re, im = _lanes(j), _lanes(n_pair + j)
        cr, ci = st_ref[:, re], st_ref[:, im]
        tr, ti = tab_ref[row:row + 1, re], tab_ref[row:row + 1, im]
        for s in order:
            car_ref[s:s + 1, re] = cr
            car_ref[s:s + 1, im] = ci
            er, ei = end_ref[s:s + 1, re], end_ref[s:s + 1, im]
            cr, ci = er + tr * cr - ti * ci, ei + tr * ci + ti * cr
        st_ref[:, re] = cr
        st_ref[:, im] = ci


def _s5_add_carries(buf_ref, car_ref, tab_ref, t_seg, n_pair):
    def fix(k, _):
        rows = _tile8(k)
        tab = tab_ref[pl.ds(k, 1), :]
        for j in range(n_pair):
            re, im = _lanes(j), _lanes(n_pair + j)
            cr, ci = car_ref[:, re], car_ref[:, im]
            tr, ti = tab[:, re], tab[:, im]
            buf_ref[rows, re] += tr * cr - ti * ci
            buf_ref[rows, im] += tr * ci + ti * cr
        return 0

    lax.fori_loop(0, t_seg, fix, 0, unroll=2)


def _s5_scan_fwd(proj, u_col, bs, cs, lam, pw, t_blk, beside=None):
    seq = proj.shape[0]
    ns, _, w2 = bs.shape
    n_pair = w2 // (2 * LANES)
    t_seg, n_t = t_blk // SUBLANES, seq // t_blk

    def body(u_ref, bs_ref, cs_ref, lam_ref, pw_ref, yc_ref, h_ref, st_ref, end_ref, car_ref, ui_ref, bu_ref, yi_ref):
        @pl.when(pl.program_id(1) == 0)
        def _():
            st_ref[...] = jnp.zeros_like(st_ref)

        _s5_interleave(ui_ref, u_ref, t_seg)
        bu_ref[...] = lax.dot_general(ui_ref[...].astype(BF16), bs_ref[...], NN, preferred_element_type=F32)
        lam_r = [jnp.broadcast_to(lam_ref[:, _lanes(j)], (SUBLANES, LANES)) for j in range(n_pair)]
        lam_i = [jnp.broadcast_to(lam_ref[:, _lanes(n_pair + j)], (SUBLANES, LANES)) for j in range(n_pair)]

        def step(k, c):
            rows = _tile8(k)
            out = []
            for j in range(n_pair):
                hr, hi = c[2 * j], c[2 * j + 1]
                nr = lam_r[j] * hr - lam_i[j] * hi + bu_ref[rows, _lanes(j)]
                ni = lam_i[j] * hr + lam_r[j] * hi + bu_ref[rows, _lanes(n_pair + j)]
                h_ref[rows, _lanes(j)] = nr
                h_ref[rows, _lanes(n_pair + j)] = ni
                out += [nr, ni]
            return tuple(out)

        ends = lax.fori_loop(0, t_seg, step, (jnp.zeros((SUBLANES, LANES), F32),) * (2 * n_pair), unroll=4)
        for j in range(n_pair):
            end_ref[:, _lanes(j)] = ends[2 * j]
            end_ref[:, _lanes(n_pair + j)] = ends[2 * j + 1]
        _s5_join_segments(st_ref, end_ref, car_ref, pw_ref, t_seg - 1, list(range(SUBLANES)), n_pair)
        _s5_add_carries(h_ref, car_ref, pw_ref, t_seg, n_pair)
        yi_ref[...] = lax.dot_general(h_ref[...].astype(BF16), cs_ref[...], NN, preferred_element_type=F32)

        def scatter(k, _):
            yc_ref[pl.ds(k, SUBLANES, stride=t_seg), :] = yi_ref[_tile8(k), :]
            return 0

        lax.fori_loop(0, t_seg, scatter, 0, unroll=4)

    return _call_beside(
        beside, body, "s5_scan_fwd", (ns, n_t),
        [_spec((t_blk, LANES), lambda s, i: (i, u_col + s)),
         _spec((None, LANES, w2), lambda s, i: (s, 0, 0)),
         _spec((None, w2, LANES), lambda s, i: (s, 0, 0)),
         _spec((None, 1, w2), lambda s, i: (s, 0, 0)),
         _spec((None, t_seg, w2), lambda s, i: (s, 0, 0))],
        [_spec((t_blk, LANES), lambda s, i: (i, s)),
         _spec((None, t_blk, w2), lambda s, i: (s, i, 0))],
        [jax.ShapeDtypeStruct((seq, ns * LANES), F32), jax.ShapeDtypeStruct((ns, seq, w2), F32)],
        [pltpu.VMEM((1, w2), F32), pltpu.VMEM((SUBLANES, w2), F32), pltpu.VMEM((SUBLANES, w2), F32),
         pltpu.VMEM((t_blk, LANES), F32), pltpu.VMEM((t_blk, w2), F32), pltpu.VMEM((t_blk, LANES), F32)],
        _vmem_limit(3 * t_blk * w2 * 4), (proj, bs, cs, lam, pw), ("parallel", "arbitrary"))


def _s5_scan_bwd(proj, u_col, states, d_yc, du_extra, bs, cs, lam, qw, t_blk):
    seq = proj.shape[0]
    ns, _, w2 = bs.shape
    n_pair = w2 // (2 * LANES)
    t_seg, n_t = t_blk // SUBLANES, seq // t_blk

    def body(u_ref, h_ref, hp_ref, dyc_ref, dux_ref, bs_ref, cs_ref, lam_ref, qw_ref,
             du_ref, dbs_ref, dcs_ref, dlam_ref, g_ref, gd_ref, st_ref, end_ref, car_ref, ui_ref, dyi_ref, dui_ref):
        i = pl.program_id(1)

        @pl.when(i == 0)
        def _():
            st_ref[...] = jnp.zeros_like(st_ref)
            dbs_ref[...] = jnp.zeros_like(dbs_ref)
            dcs_ref[...] = jnp.zeros_like(dcs_ref)
            dlam_ref[...] = jnp.zeros_like(dlam_ref)

        _s5_interleave(ui_ref, u_ref, t_seg)
        _s5_interleave(dyi_ref, dyc_ref, t_seg)
        dyc_b = dyi_ref[...].astype(BF16)
        gd_ref[...] = lax.dot_general(dyc_b, cs_ref[...], NT, preferred_element_type=F32)
        lam_r = [jnp.broadcast_to(lam_ref[:, _lanes(j)], (SUBLANES, LANES)) for j in range(n_pair)]
        lam_i = [jnp.broadcast_to(lam_ref[:, _lanes(n_pair + j)], (SUBLANES, LANES)) for j in range(n_pair)]

        def step(kk, c):
            rows = _tile8(t_seg - 1 - kk)
            out = []
            for j in range(n_pair):
                gr_n, gi_n = c[2 * j], c[2 * j + 1]
                gr = gd_ref[rows, _lanes(j)] + lam_r[j] * gr_n + lam_i[j] * gi_n
                gi = gd_ref[rows, _lanes(n_pair + j)] + lam_r[j] * gi_n - lam_i[j] * gr_n
                g_ref[rows, _lanes(j)] = gr
                g_ref[rows, _lanes(n_pair + j)] = gi
                out += [gr, gi]
            return tuple(out)

        zero = jnp.zeros((SUBLANES, LANES), F32)
        firsts = lax.fori_loop(0, t_seg, step, (zero,) * (2 * n_pair), unroll=4)
        for j in range(n_pair):
            end_ref[:, _lanes(j)] = firsts[2 * j]
            end_ref[:, _lanes(n_pair + j)] = firsts[2 * j + 1]
        _s5_join_segments(st_ref, end_ref, car_ref, qw_ref, 0, list(range(SUBLANES))[::-1], n_pair)
        _s5_add_carries(g_ref, car_ref, qw_ref, t_seg, n_pair)

        def pair_up(k, c):
            rows, prev = _tile8(k), _tile8(k - 1)
            out = []
            for j in range(n_pair):
                re, im = _lanes(j), _lanes(n_pair + j)
                gr, gi, hr, hi = g_ref[rows, re], g_ref[rows, im], h_ref[prev, re], h_ref[prev, im]
                out += [c[2 * j] + gr * hr + gi * hi, c[2 * j + 1] + gi * hr - gr * hi]
            return tuple(out)

        acc = lax.fori_loop(1, t_seg, pair_up, (zero,) * (2 * n_pair), unroll=4)
        has_prev = (i < n_t - 1).astype(F32)
        first_seg = lax.broadcasted_iota(jnp.int32, (SUBLANES, LANES), 0) == 0
        last = _tile8(t_seg - 1)
        for j in range(n_pair):
            re, im = _lanes(j), _lanes(n_pair + j)
            gr, gi = g_ref[0:SUBLANES, re], g_ref[0:SUBLANES, im]
            hr = jnp.where(first_seg, hp_ref[SUBLANES - 1:, re] * has_prev, pltpu.roll(h_ref[last, re], 1, 0))
            hi = jnp.where(first_seg, hp_ref[SUBLANES - 1:, im] * has_prev, pltpu.roll(h_ref[last, im], 1, 0))
            dlam_ref[:, re] += jnp.sum(acc[2 * j] + gr * hr + gi * hi, axis=0, keepdims=True)
            dlam_ref[:, im] += jnp.sum(acc[2 * j + 1] + gi * hr - gr * hi, axis=0, keepdims=True)

        g_b = g_ref[...].astype(BF16)
        dui_ref[...] = lax.dot_general(g_b, bs_ref[...], NT, preferred_element_type=F32)
        dbs_ref[...] += lax.dot_general(ui_ref[...].astype(BF16), g_b, TN, preferred_element_type=F32)
        dcs_ref[...] += lax.dot_general(h_ref[...].astype(BF16), dyc_b, TN, preferred_element_type=F32)

        def scatter(k, _):
            rows = pl.ds(k, SUBLANES, stride=t_seg)
            du_ref[rows, :] = (dui_ref[_tile8(k), :] + dux_ref[rows, :]).astype(du_ref.dtype)
            return 0

        lax.fori_loop(0, t_seg, scatter, 0, unroll=4)

    rev = lambda i: n_t - 1 - i
    return pl.pallas_call(
        body, name="s5_scan_bwd", grid=(ns, n_t),
        in_specs=[_spec((t_blk, LANES), lambda s, i: (rev(i), u_col + s)),
                  _spec((None, t_blk, w2), lambda s, i: (s, rev(i), 0)),
                  _spec((None, SUBLANES, w2), lambda s, i: (s, jnp.maximum(rev(i) * t_seg - 1, 0), 0)),
                  _spec((t_blk, LANES), lambda s, i: (rev(i), s)),
                  _spec((t_blk, LANES), lambda s, i: (rev(i), s)),
                  _spec((None, LANES, w2), lambda s, i: (s, 0, 0)),
                  _spec((None, w2, LANES), lambda s, i: (s, 0, 0)),
                  _spec((None, 1, w2), lambda s, i: (s, 0, 0)),
                  _spec((None, t_seg, w2), lambda s, i: (s, 0, 0))],
        out_specs=[_spec((t_blk, LANES), lambda s, i: (rev(i), s)),
                   _spec((None, LANES, w2), lambda s, i: (s, 0, 0)),
                   _spec((None, w2, LANES), lambda s, i: (s, 0, 0)),
                   _spec((None, 1, w2), lambda s, i: (s, 0, 0))],
        out_shape=[jax.ShapeDtypeStruct((seq, ns * LANES), F32), jax.ShapeDtypeStruct(bs.shape, F32),
                   jax.ShapeDtypeStruct(cs.shape, F32), jax.ShapeDtypeStruct(lam.shape, F32)],
        scratch_shapes=[pltpu.VMEM((t_blk, w2), F32), pltpu.VMEM((t_blk, w2), F32), pltpu.VMEM((1, w2), F32),
                        pltpu.VMEM((SUBLANES, w2), F32), pltpu.VMEM((SUBLANES, w2), F32),
                        pltpu.VMEM((t_blk, LANES), F32), pltpu.VMEM((t_blk, LANES), F32), pltpu.VMEM((t_blk, LANES), F32)],
        compiler_params=pltpu.CompilerParams(dimension_semantics=("parallel", "arbitrary"),
                                             vmem_limit_bytes=_vmem_limit(5 * t_blk * w2 * 4)),
    )(*[_in_hbm(a) for a in (proj, states, states, d_yc, du_extra, bs, cs, lam, qw)])


def _loss_head(y, target, t_m):
    seq, d = y.shape

    def body(y_ref, t_ref, loss_ref, dy_ref):
        @pl.when(pl.program_id(0) == 0)
        def _():
            loss_ref[...] = jnp.zeros_like(loss_ref)

        diff = y_ref[...] - t_ref[...]
        dy_ref[...] = diff / d
        loss_ref[...] += 0.5 * jnp.sum(diff * diff) / d

    row = _spec((t_m, d), lambda i: (i, 0))
    return pl.pallas_call(
        body, name="loss_head", grid=(seq // t_m,), in_specs=[row, row],
        out_specs=[_spec((SUBLANES, LANES), lambda i: (0, 0)), row],
        out_shape=[jax.ShapeDtypeStruct((SUBLANES, LANES), F32), jax.ShapeDtypeStruct((seq, d), F32)],
        compiler_params=pltpu.CompilerParams(dimension_semantics=("arbitrary",),
                                             vmem_limit_bytes=_vmem_limit(6 * t_m * d * 4)),
    )(_in_hbm(y), _in_hbm(target))


def _adamw_fn(w, m, v, *partials):
    g = partials[0]
    for p in partials[1:]:
        g = g + p
    m2 = ADAM_B1 * m + (1.0 - ADAM_B1) * g
    v2 = ADAM_B2 * v + (1.0 - ADAM_B2) * (g * g)
    m_hat = m2 / (1.0 - ADAM_B1 ** ADAM_STEP)
    v_hat = v2 / (1.0 - ADAM_B2 ** ADAM_STEP)
    delta = -ADAM_LR * (m_hat / (jnp.sqrt(v_hat) + ADAM_EPS) + ADAM_WD * w)
    return g, delta, m2, v2


def _adamw(name, w, m, v, partials):
    rows, cols = w.shape
    t_r = rows
    for cand in (512, 256, 128, 64, 32, 16, 8):
        if rows % cand == 0 and cand * cols * 4 <= (1 << 20):
            t_r = cand
            break
    n_p = partials.shape[0]
    row = lambda i: (i, 0)
    ins = [(a, (t_r, cols), row) for a in (w, m, v)]
    ins += [(partials, (None, t_r, cols), (lambda i, j=j: (j, i, 0))) for j in range(n_p)]
    outs = [((rows, cols), F32, (t_r, cols), row)] * 4
    return _rowwise(name, _adamw_fn, ins, outs, (rows // t_r,))


SMALL_PARAMS = ("b_ada", "ssm_a_re", "ssm_a_im", "ssm_log_dt", "ssm_b_re", "ssm_b_im", "ssm_c_re", "ssm_c_im",
                "ssm_d", "b_glu", "ln1_g", "ln1_b", "ln2_g", "ln2_b")
WEIGHTS = ("w_ada", "b_ada", "w_in", "w_sb_up", "ssm_a_re", "ssm_a_im", "ssm_log_dt", "ssm_b_re", "ssm_b_im",
           "ssm_c_re", "ssm_c_im", "ssm_d", "w_glu", "b_glu", "w_ssm_up", "w_out", "ln1_g", "ln1_b", "w_ffn_in",
           "w_ffn_out", "ln2_g", "ln2_b")
ARG_NAMES = (("x", "c") + WEIGHTS + ("loss_target",) + tuple("m_" + n for n in WEIGHTS)
             + tuple("v_" + n for n in WEIGHTS))


def _pack(arrs):
    flat = jnp.concatenate([a.reshape(-1) for a in arrs])
    pad = (-flat.shape[0]) % (PACK_ROWS * LANES)
    return jnp.pad(flat, (0, pad)).reshape(-1, LANES)


def _unpack(packed, like):
    lead = packed.shape[:-2]
    flat = packed.reshape(lead + (-1,))
    out, off = [], 0
    for a in like:
        out.append(flat[..., off:off + a.size].reshape(lead + a.shape))
        off += a.size
    return out


def kernel(x, c, w_ada, b_ada, w_in, w_sb_up, ssm_a_re, ssm_a_im, ssm_log_dt, ssm_b_re, ssm_b_im, ssm_c_re,
           ssm_c_im, ssm_d, w_glu, b_glu, w_ssm_up, w_out, ln1_g, ln1_b, w_ffn_in, w_ffn_out, ln2_g, ln2_b,
           loss_target, m_w_ada, m_b_ada, m_w_in, m_w_sb_up, m_ssm_a_re, m_ssm_a_im, m_ssm_log_dt, m_ssm_b_re,
           m_ssm_b_im, m_ssm_c_re, m_ssm_c_im, m_ssm_d, m_w_glu, m_b_glu, m_w_ssm_up, m_w_out, m_ln1_g, m_ln1_b,
           m_w_ffn_in, m_w_ffn_out, m_ln2_g, m_ln2_b, v_w_ada, v_b_ada, v_w_in, v_w_sb_up, v_ssm_a_re, v_ssm_a_im,
           v_ssm_log_dt, v_ssm_b_re, v_ssm_b_im, v_ssm_c_re, v_ssm_c_im, v_ssm_d, v_w_glu, v_b_glu, v_w_ssm_up,
           v_w_out, v_ln1_g, v_ln1_b, v_w_ffn_in, v_w_ffn_out, v_ln2_g, v_ln2_b):
    given = locals()
    return _train_step({n: given[n] for n in ARG_NAMES})


def _train_step(p):
    x0 = p["x"][0]
    target = p["loss_target"][0]
    seq, d = x0.shape
    depth = p["w_ada"].shape[0]
    n_ada = p["w_ada"].shape[2]
    n_in = p["w_in"].shape[2]
    sb_w = p["w_sb_up"].shape[1]
    ssm_w = p["w_ssm_up"].shape[1]
    n_up = p["w_sb_up"].shape[2]
    n_ffn = p["w_ffn_in"].shape[2]
    ffn = N_DEV * p["w_ffn_out"].shape[1]
    in_cols = N_DEV * n_in
    alpha = (2 * depth) ** 0.25
    resid_ln, resid_ln_mod = _make_resid_fns(alpha)
    t_r = min(512, seq)
    n_r = seq // t_r
    t_m = min(1024, seq)
    n_m = seq // t_m
    t_d = _tile(d)
    assert n_ffn * (N_DEV // 2) == ffn and sb_w % LANES == 0 and ssm_w % LANES == 0 and d % LANES == 0
    assert n_in % LANES == 0 and n_up % LANES == 0 and seq % t_m == 0 and in_cols == 3 * sb_w + ssm_w + 2 * d
    assert (3 * sb_w) % ssm_w == 0 and (3 * sb_w + ssm_w) % (2 * d) == 0

    bf = lambda a: a.astype(BF16)
    got = _exchange("gather_first", [], [bf(p["w_in"][0]), p["c"]])
    wg_in = [got[0]] + [None] * (depth - 1)
    c_all = got[1].reshape(N_DEV, d)
    small_names = ("w_sb_up", "w_ssm_up", "w_glu", "w_out")
    wg_ffn_in, wg_ffn_out, wg = [None] * depth, [None] * depth, {}

    c_pad = jnp.pad(c_all, ((0, 2 * SUBLANES - N_DEV), (0, 0)))
    c_act = _rowwise("silu_c", lambda v: v * jax.nn.sigmoid(v), [(c_pad, c_pad.shape, lambda i: (0, 0))],
                     [(c_pad.shape, F32, c_pad.shape, lambda i: (0, 0))], (1,))[0]
    rows_c = c_pad.shape[0]
    mod_cols = [
        _mm(f"mod_{l}", c_act, p["w_ada"],
            _spec((rows_c, d), lambda i, j, k: (0, 0)), _spec((None, d, n_ada), lambda i, j, k, l=l: (l, 0, 0)),
            _spec((rows_c, n_ada), lambda i, j, k: (0, 0)), (rows_c, n_ada), F32, (1, 1, 1), NN)
        for l in range(depth)]
    mod_send = jnp.stack([m[:N_DEV] for m in mod_cols], axis=1)
    mod_recv = _exchange("exchange_mod", [mod_send], [])[0]
    mod_nobias = jnp.swapaxes(mod_recv, 0, 1).reshape(depth, N_DEV * n_ada)
    full2 = lambda a: (a, a.shape, lambda i: (0, 0))
    mod = _rowwise("mod_bias", lambda a, b: a + b, [full2(mod_nobias), full2(p["b_ada"])],
                   [(mod_nobias.shape, F32, mod_nobias.shape, lambda i: (0, 0))], (1,))[0]
    vec = lambda a: a.reshape(1, -1)
    mods = [[vec(mod[l, j * d:(j + 1) * d]) for j in range(6)] for l in range(depth)]
    ln = {n: [vec(p[n][l]) for l in range(depth)] for n in ("ln1_g", "ln1_b", "ln2_g", "ln2_b")}

    row_spec = lambda width: ((t_r, width), lambda i: (i, 0))
    col_spec = lambda width, cb: ((t_r, width), lambda i, cb=cb: (i, cb))
    vec_spec = lambda width: ((1, width), lambda i: (0, 0))
    rows_in = lambda a: (a,) + row_spec(a.shape[1])
    vec_in = lambda a: (a,) + vec_spec(a.shape[1])
    row_out = lambda width, dt: ((seq, width), dt) + row_spec(width)

    s5 = [_s5_discretize(*[p[n][l] for n in ("ssm_a_re", "ssm_a_im", "ssm_log_dt", "ssm_b_re", "ssm_b_im",
                                               "ssm_c_re", "ssm_c_im")]) for l in range(depth)]
    s5_b16 = [(bs.astype(BF16), cs.astype(BF16), lam) for bs, cs, lam in s5]
    t_scan = min(512, seq)
    s5_pw = [_s5_powers(p["ssm_a_re"][l], p["ssm_a_im"][l], p["ssm_log_dt"][l], t_scan // SUBLANES)
             for l in range(depth)]
    u_col = 3 * sb_w // LANES
    gates_cb = (3 * sb_w + ssm_w) // (2 * d)
    ssm_d = [vec(p["ssm_d"][l]) for l in range(depth)]
    b_glu = [vec(p["b_glu"][l]) for l in range(depth)]
    n_half = N_DEV // 2

    h = _rowwise("modulate_in", _modulate, [rows_in(x0), vec_in(mods[0][1]), vec_in(mods[0][0])],
                 [row_out(d, BF16)], (n_r,))[0]
    saved = []
    x_cur = x0
    for l in range(depth):
        sv = {"x_in": x_cur, "h": h}
        last = l == depth - 1
        t_n = _tile(n_in)
        r_n = n_in // t_n
        proj = _mm(f"proj_{l}", h, wg_in[l],
                   _spec((t_m, d), lambda i, j, k: (i, 0)),
                   _spec((None, d, t_n), lambda i, j, k, r=r_n: (j // r, 0, j % r)),
                   _spec((t_m, t_n), lambda i, j, k: (i, j)), (seq, in_cols), F32, (n_m, N_DEV * r_n, 1), NN)
        arriving = [bf(p["w_ffn_in"][l]), bf(p["w_ffn_out"][l])] + ([bf(p[n]) for n in small_names] if l == 0 else [])
        (o_sb, o_sb32), got = _sb_attention_fwd(proj, sb_w, beside=_Exchange(gather=arriving))
        wg_ffn_in[l] = got[0]
        wg_ffn_out[l] = got[1].reshape(n_half, n_ffn, d)
        if l == 0:
            wg = dict(zip(small_names, got[2:]))
            for n in ("w_glu", "w_out"):
                wg[n] = jnp.swapaxes(wg[n], 0, 1).reshape(depth, -1, wg[n].shape[-1])
            for n in ("w_sb_up", "w_ssm_up"):
                wg[n] = jnp.transpose(wg[n], (1, 2, 0, 3)).reshape(depth, wg[n].shape[2], d)
        bs16, cs16, lam = s5_b16[l]
        (yc, states), got = _s5_scan_fwd(proj, u_col, bs16, cs16, lam, s5_pw[l][0], t_scan,
                                         beside=None if last else _Exchange(gather=[bf(p["w_in"][l + 1])]))
        if not last:
            wg_in[l + 1] = got[0]
        u_in = (proj,) + col_spec(ssm_w, 3 * sb_w // ssm_w)
        y1 = _rowwise(f"s5_act_{l}", _s5_act_fn, [rows_in(yc), u_in, vec_in(ssm_d[l])],
                      [row_out(ssm_w, BF16)], (n_r,))[0]
        t_glu = _mm(f"s5_glu_mm_{l}", y1, wg["w_glu"],
                    _spec((t_m, ssm_w), lambda i, j, k: (i, 0)), _spec((None, ssm_w, ssm_w), lambda i, j, k, l=l: (l, 0, 0)),
                    _spec((t_m, ssm_w), lambda i, j, k: (i, 0)), (seq, ssm_w), F32, (n_m, 1, 1), NN)
        s5_out = _rowwise(f"s5_glu_{l}", _s5_glu_fn,
                          [rows_in(yc), u_in, rows_in(t_glu), vec_in(ssm_d[l]), vec_in(b_glu[l])],
                          [row_out(ssm_w, BF16)], (n_r,))[0]

        def up_proj(name, a, w, l=l):
            return _mm(name, a, w, _spec((t_m, a.shape[1]), lambda i, j, k: (i, 0)),
                       _spec((None, a.shape[1], t_d), lambda i, j, k: (l, 0, j)),
                       _spec((t_m, t_d), lambda i, j, k: (i, j)), (seq, d), F32, (n_m, d // t_d, 1), NN)

        y_sb = up_proj(f"sb_up_{l}", o_sb, wg["w_sb_up"])
        y_ssm = up_proj(f"ssm_up_{l}", s5_out, wg["w_ssm_up"])
        gates = (proj,) + col_spec(2 * d, gates_cb)
        merged = _rowwise(f"merge_{l}", _merge_fn, [rows_in(y_sb), rows_in(y_ssm), gates],
                          [row_out(d, BF16)], (n_r,))[0]
        y_mix = _mm(f"out_proj_{l}", merged, wg["w_out"],
                    _spec((t_m, d), lambda i, j, k: (i, 0)), _spec((None, d, t_d), lambda i, j, k, l=l: (l, 0, j)),
                    _spec((t_m, t_d), lambda i, j, k: (i, j)), (seq, d), F32, (n_m, d // t_d, 1), NN)
        vecs_a = [mods[l][2], ln["ln1_g"][l], ln["ln1_b"][l], mods[l][4], mods[l][3]]
        x_mid, h2 = _rowwise(f"resid_mix_{l}", resid_ln_mod, [rows_in(x_cur), rows_in(y_mix)] + [vec_in(v) for v in vecs_a],
                             [row_out(d, F32), row_out(d, BF16)], (n_r,))
        a_ffn = _mm(f"ffn_in_{l}", h2, wg_ffn_in[l],
                    _spec((t_m, d), lambda i, j, k: (i, 0)), _spec((None, d, n_ffn), lambda i, j, k: (j, 0, 0)),
                    _spec((None, t_m, n_ffn), lambda i, j, k: (j, i, 0)), (N_DEV, seq, n_ffn), FFN_ACT, (n_m, N_DEV, 1), NN)
        pair_in = lambda a: (a.reshape(2, n_half, seq, n_ffn), (2, None, t_r, n_ffn), lambda j, i: (0, j, i, 0))
        f_act = _rowwise(f"swiglu_{l}", _swiglu_fn, [pair_in(a_ffn)],
                         [((n_half, seq, n_ffn), BF16, (None, t_r, n_ffn), lambda j, i: (j, i, 0))], (n_half, n_r))[0]
        y_ffn = _mm(f"ffn_out_{l}", f_act, wg_ffn_out[l],
                    _spec((None, t_m, n_ffn), lambda i, j, k: (k, i, 0)),
                    _spec((None, n_ffn, t_d), lambda i, j, k: (k, 0, j)),
                    _spec((t_m, t_d), lambda i, j, k: (i, j)), (seq, d), F32, (n_m, d // t_d, n_half), NN)
        vecs_b = [mods[l][5], ln["ln2_g"][l], ln["ln2_b"][l]] + ([] if last else [mods[l + 1][1], mods[l + 1][0]])
        outs_b = [row_out(d, F32)] + ([] if last else [row_out(d, BF16)])
        res = _rowwise(f"resid_ffn_{l}", resid_ln if last else resid_ln_mod,
                       [rows_in(x_mid), rows_in(y_ffn)] + [vec_in(v) for v in vecs_b], outs_b, (n_r,))
        sv.update(proj=proj, o_sb=o_sb, o_sb32=o_sb32, yc=yc, states=states, y1=y1, t_glu=t_glu, s5_out=s5_out,
                  y_sb=y_sb, y_ssm=y_ssm, merged=merged, y_mix=y_mix, x_mid=x_mid, h2=h2, a_ffn=a_ffn, f_act=f_act,
                  y_ffn=y_ffn, vecs_a=vecs_a, vecs_b=vecs_b)
        saved.append(sv)
        x_cur = res[0]
        h = None if last else res[1]

    loss_part, d_x = _loss_head(x_cur, target, t_r)
    loss = lax.psum(loss_part[0, 0], MESH_AXES)

    d_h_next = None
    grads = {n: [None] * depth for n in WEIGHTS}
    d_mod = [[None] * 6 for _ in range(depth)]
    land = {}
    waiting = []
    row_wrt = lambda i, width, dt: (i, "row", (seq, width), dt) + row_spec(width)
    sum_wrt = lambda i, width: (i, "sum", (1, width), F32) + vec_spec(width)
    for l in reversed(range(depth)):
        sv = saved[l]
        last = l == depth - 1
        ins_b = [rows_in(sv["x_mid"]), rows_in(sv["y_ffn"])] + [vec_in(v) for v in sv["vecs_b"]]
        cts_b = [rows_in(d_x)] + ([] if last else [rows_in(d_h_next)])
        wrt_b = [row_wrt(0, d, F32), row_wrt(1, d, BF16)] + [sum_wrt(2 + j, d) for j in range(len(sv["vecs_b"]))]
        res = _rowwise_vjp(f"resid_ffn_bwd_{l}", resid_ln if last else resid_ln_mod, ins_b, cts_b, wrt_b, (n_r,))
        d_x_mid, d_y_ffn = res[0], res[1]
        d_mod[l][5], grads["ln2_g"][l], grads["ln2_b"][l] = res[2], res[3], res[4]
        if not last:
            d_mod[l + 1][1], d_mod[l + 1][0] = res[5], res[6]
        d_f = _mm(f"ffn_out_dx_{l}", d_y_ffn, wg_ffn_out[l],
                  _spec((t_m, d), lambda i, j, k: (i, 0)), _spec((None, n_ffn, d), lambda i, j, k: (j, 0, 0)),
                  _spec((None, t_m, n_ffn), lambda i, j, k: (j, i, 0)), (n_half, seq, n_ffn), FFN_ACT, (n_m, n_half, 1), NT)
        g_ffn_out = _mm(f"ffn_out_dw_{l}", sv["f_act"], d_y_ffn,
                        _spec((None, t_m, n_ffn), lambda i, j, k: (i, k, 0)), _spec((t_m, t_d), lambda i, j, k: (k, j)),
                        _spec((None, n_ffn, t_d), lambda i, j, k: (i, 0, j)), (n_half, n_ffn, d), GRAD_WIRE,
                        (n_half, d // t_d, n_m), TN)
        pair_in = lambda a: (a.reshape(2, n_half, seq, n_ffn), (2, None, t_r, n_ffn), lambda j, i: (0, j, i, 0))
        d_a = _rowwise_vjp(f"swiglu_bwd_{l}", _swiglu_fn, [pair_in(sv["a_ffn"])],
                           [(d_f, (None, t_r, n_ffn), lambda j, i: (j, i, 0))],
                           [(0, "row", (2, n_half, seq, n_ffn), BF16, (2, None, t_r, n_ffn), lambda j, i: (0, j, i, 0))],
                           (n_half, n_r))[0].reshape(N_DEV, seq, n_ffn)
        d_h2 = _mm(f"ffn_in_dx_{l}", d_a, wg_ffn_in[l],
                   _spec((None, t_m, n_ffn), lambda i, j, k: (k, i, 0)),
                   _spec((None, t_d, n_ffn), lambda i, j, k: (k, j, 0)),
                   _spec((t_m, t_d), lambda i, j, k: (i, j)), (seq, d), F32, (n_m, d // t_d, N_DEV), NT)
        g_ffn_in = _mm(f"ffn_in_dw_{l}", sv["h2"], d_a,
                       _spec((t_m, t_d), lambda i, j, k: (k, j)), _spec((None, t_m, n_ffn), lambda i, j, k: (i, k, 0)),
                       _spec((None, t_d, n_ffn), lambda i, j, k: (i, j, 0)), (N_DEV, d, n_ffn), GRAD_WIRE,
                       (N_DEV, d // t_d, n_m), TN)
        ins_a = [rows_in(sv["x_in"]), rows_in(sv["y_mix"])] + [vec_in(v) for v in sv["vecs_a"]]
        wrt_a = [row_wrt(0, d, F32), row_wrt(1, d, BF16)] + [sum_wrt(2 + j, d) for j in range(5)]
        res = _rowwise_vjp(f"resid_mix_bwd_{l}", resid_ln_mod, ins_a, [rows_in(d_x_mid), rows_in(d_h2)], wrt_a, (n_r,))
        d_x_in, d_y_mix = res[0], res[1]
        d_mod[l][2], grads["ln1_g"][l], grads["ln1_b"][l], d_mod[l][4], d_mod[l][3] = res[2:7]
        d_merged = _mm(f"out_proj_dx_{l}", d_y_mix, wg["w_out"],
                       _spec((t_m, d), lambda i, j, k: (i, 0)), _spec((None, t_d, d), lambda i, j, k, l=l: (l, j, 0)),
                       _spec((t_m, t_d), lambda i, j, k: (i, j)), (seq, d), F32, (n_m, d // t_d, 1), NT)
        g_out = _mm(f"out_proj_dw_{l}", sv["merged"], d_y_mix,
                    _spec((t_m, t_d), lambda i, j, k: (k, i)), _spec((t_m, t_d), lambda i, j, k: (k, j)),
                    _spec((t_d, t_d), lambda i, j, k: (i, j)), (d, d), GRAD_WIRE, (d // t_d, d // t_d, n_m), TN)
        gates = (sv["proj"],) + col_spec(2 * d, gates_cb)
        d_y_sb, d_y_ssm, d_gates = _rowwise_vjp(
            f"merge_bwd_{l}", _merge_fn, [rows_in(sv["y_sb"]), rows_in(sv["y_ssm"]), gates], [rows_in(d_merged)],
            [row_wrt(0, d, BF16), row_wrt(1, d, BF16), row_wrt(2, 2 * d, BF16)], (n_r,))

        def up_bwd(name, act, d_y, w, dx_dtype, l=l):
            k_w = act.shape[1]
            dx = _mm(name + "_dx", d_y, w, _spec((t_m, d), lambda i, j, k: (i, 0)),
                     _spec((None, k_w, d), lambda i, j, k: (l, 0, 0)),
                     _spec((t_m, k_w), lambda i, j, k: (i, 0)), (seq, k_w), dx_dtype, (n_m, 1, 1), NT)
            dw = _mm(name + "_dw", act, d_y, _spec((t_m, k_w), lambda i, j, k: (k, 0)),
                     _spec((t_m, t_d), lambda i, j, k: (k, j)),
                     _spec((k_w, t_d), lambda i, j, k: (0, j)), (k_w, d), GRAD_WIRE, (1, d // t_d, n_m), TN)
            return dx, jnp.swapaxes(dw.reshape(k_w, N_DEV, n_up), 0, 1)

        d_o_sb, g_sb_up = up_bwd(f"sb_up_{l}", sv["o_sb"], d_y_sb, wg["w_sb_up"], BF16)
        d_s5_out, g_ssm_up = up_bwd(f"ssm_up_{l}", sv["s5_out"], d_y_ssm, wg["w_ssm_up"], F32)
        waiting += [("w_ffn_in", g_ffn_in), ("w_ffn_out", g_ffn_out.reshape(N_DEV, -1, d)),
                    ("w_out", g_out.reshape(N_DEV, -1, d)), ("w_sb_up", g_sb_up), ("w_ssm_up", g_ssm_up)]
        levels = [l + 1] * (len(waiting) - 5) + [l] * 5
        (d_q, d_k, d_v), got = _sb_attention_bwd(
            sv["proj"], sv["o_sb32"], d_o_sb, sb_w,
            beside=_Exchange(layered=[(g, lv, depth, land.get(n)) for (n, g), lv in zip(waiting, levels)]))
        land.update({n: buf for (n, _), buf in zip(waiting, got)})
        u_in = (sv["proj"],) + col_spec(ssm_w, 3 * sb_w // ssm_w)
        ins_s5 = [rows_in(sv["yc"]), u_in, rows_in(sv["t_glu"]), vec_in(ssm_d[l]), vec_in(b_glu[l])]
        d_t = _rowwise_vjp(f"s5_glu_bwd_{l}", _s5_glu_fn, ins_s5, [rows_in(d_s5_out)],
                           [row_wrt(2, ssm_w, BF16)], (n_r,))[0]
        d_y1 = _mm(f"s5_glu_mm_dx_{l}", d_t, wg["w_glu"],
                   _spec((t_m, ssm_w), lambda i, j, k: (i, 0)), _spec((None, ssm_w, ssm_w), lambda i, j, k, l=l: (l, 0, 0)),
                   _spec((t_m, ssm_w), lambda i, j, k: (i, 0)), (seq, ssm_w), F32, (n_m, 1, 1), NT)
        g_glu = _mm(f"s5_glu_mm_dw_{l}", sv["y1"], d_t,
                    _spec((t_m, ssm_w), lambda i, j, k: (k, 0)), _spec((t_m, ssm_w), lambda i, j, k: (k, 0)),
                    _spec((ssm_w, ssm_w), lambda i, j, k: (0, 0)), (ssm_w, ssm_w), GRAD_WIRE, (1, 1, n_m), TN)
        d_yc, d_u_skip, grads["ssm_d"][l], grads["b_glu"][l] = _rowwise_vjp(
            f"s5_post_bwd_{l}", _s5_post_fn, ins_s5, [rows_in(d_y1), rows_in(d_s5_out)],
            [row_wrt(0, ssm_w, F32), row_wrt(1, ssm_w, F32), sum_wrt(3, ssm_w), sum_wrt(4, ssm_w)], (n_r,))
        bs16, cs16, lam = s5_b16[l]
        d_u, d_bs, d_cs, d_lam = _s5_scan_bwd(sv["proj"], u_col, sv["states"], d_yc, d_u_skip, bs16, cs16, lam,
                                              s5_pw[l][1], t_scan)
        raw = [p[n][l] for n in ("ssm_a_re", "ssm_a_im", "ssm_log_dt", "ssm_b_re", "ssm_b_im", "ssm_c_re", "ssm_c_im")]
        _, pull = jax.vjp(_s5_discretize, *raw)
        (grads["ssm_a_re"][l], grads["ssm_a_im"][l], grads["ssm_log_dt"][l], grads["ssm_b_re"][l],
         grads["ssm_b_im"][l], grads["ssm_c_re"][l], grads["ssm_c_im"][l]) = pull((d_bs, d_cs, d_lam))
        d_proj = jnp.concatenate([d_q, d_k.astype(BF16), d_v.astype(BF16), d_u.astype(BF16), d_gates], axis=1)
        t_n = _tile(n_in)
        g_in = _mm(f"proj_dw_{l}", sv["h"], d_proj,
                   _spec((t_m, t_d), lambda i, j, k: (k, j)), _spec((t_m, n_in), lambda i, j, k: (k, i)),
                   _spec((None, t_d, n_in), lambda i, j, k: (i, j, 0)), (N_DEV, d, n_in), GRAD_WIRE,
                   (N_DEV, d // t_d, n_m), TN)
        waiting = [("w_in", g_in), ("w_glu", g_glu.reshape(N_DEV, -1, ssm_w))]
        closing = _Exchange(layered=[(g, 0, depth, land.get(n)) for n, g in waiting]) if l == 0 else None
        d_h = _mm(f"proj_dx_{l}", d_proj, wg_in[l],
                  _spec((t_m, n_in), lambda i, j, k: (i, k)), _spec((None, t_d, n_in), lambda i, j, k: (k, j, 0)),
                  _spec((t_m, t_d), lambda i, j, k: (i, j)), (seq, d), F32, (n_m, d // t_d, N_DEV), NT, beside=closing)
        if l == 0:
            d_h, got = d_h
            land.update({n: buf for (n, _), buf in zip(waiting, got)})
        d_x, d_h_next = d_x_in, d_h
    res = _rowwise_vjp("modulate_in_bwd", lambda v, sc, sh: (v, _modulate(v, sc, sh)),
                       [rows_in(x0), vec_in(mods[0][1]), vec_in(mods[0][0])], [rows_in(d_x), rows_in(d_h_next)],
                       [row_wrt(0, d, F32), sum_wrt(1, d), sum_wrt(2, d)], (n_r,))
    grad_x, d_mod[0][1], d_mod[0][0] = res

    d_mod_rows = jnp.concatenate([jnp.concatenate(d_mod[l], axis=1) for l in range(depth)], axis=0)
    grads["b_ada"] = [d_mod_rows[l] for l in range(depth)]
    small_local = [jnp.stack([g.reshape(p[n].shape[1:]) for g in grads[n]]) for n in SMALL_PARAMS]
    d_mod_send = jnp.swapaxes(d_mod_rows.reshape(depth, N_DEV, n_ada), 0, 1)
    d_mod_cols, small_all = _exchange("exchange_last", [d_mod_send], [_pack(small_local)])
    d_mod_pad = jnp.pad(jnp.swapaxes(d_mod_cols, 0, 1), ((0, 0), (0, rows_c - N_DEV), (0, 0)))
    g_ada = [
        _mm(f"mod_dw_{l}", c_act, d_mod_pad,
            _spec((rows_c, d), lambda i, j, k: (0, 0)), _spec((None, rows_c, n_ada), lambda i, j, k, l=l: (l, 0, 0)),
            _spec((d, n_ada), lambda i, j, k: (0, 0)), (d, n_ada), F32, (1, 1, 1), TN)
        for l in range(depth)]

    out = {}

    def update(name, partials):
        shape = p[name].shape
        two_d = lambda a: a.reshape(-1, shape[-1])
        res = _adamw("adamw_" + name, two_d(p[name]), two_d(p["m_" + name]), two_d(p["v_" + name]),
                     partials.reshape(partials.shape[0], -1, shape[-1]))
        out[name] = [r.reshape(shape) for r in res]

    update("w_ada", jnp.stack(g_ada)[None])
    for n in ("w_in", "w_sb_up", "w_ssm_up", "w_ffn_in", "w_glu", "w_out", "w_ffn_out"):
        update(n, land[n])
    small_w = [p[n] for n in SMALL_PARAMS]
    res = _adamw("adamw_small", _pack(small_w), _pack([p["m_" + n] for n in SMALL_PARAMS]),
                 _pack([p["v_" + n] for n in SMALL_PARAMS]), small_all)
    for kind, packed in enumerate(res):
        for n, a in zip(SMALL_PARAMS, _unpack(packed, small_w)):
            out.setdefault(n, [None] * 4)[kind] = a

    return ((loss, grad_x[None]) + tuple(out[n][0] for n in WEIGHTS) + tuple(out[n][1] for n in WEIGHTS)
            + tuple(out[n][2] for n in WEIGHTS) + tuple(out[n][3] for n in WEIGHTS))
```

```python
import jax
import jax.numpy as jnp
from jax import lax
from jax.experimental import pallas as pl
from jax.experimental.pallas import tpu as pltpu

F32 = jnp.float32
BF16 = jnp.bfloat16
GRAD_WIRE = BF16
FFN_ACT = BF16

N_DEV = 8
LANES = 128
SUBLANES = 8
VMEM_BYTES = 64 * 1024 * 1024
HEAD_DIM = 64
SB_BLOCK = 256
SLAB_GROUPS = 8
LN_EPS = 1e-5
ADAM_LR, ADAM_B1, ADAM_B2, ADAM_EPS, ADAM_WD, ADAM_STEP = 0.001, 0.9, 0.999, 1e-08, 0.01, 10
SB_UNDERFLOW = -120.0

PACK_ROWS = 256
MESH_AXES = ("x", "y", "c")


def _vmem_limit(block_bytes):
    return int(min(max(3 * block_bytes + (8 << 20), 24 << 20), VMEM_BYTES - (8 << 20)))


def _nbytes(shape, dtype):
    n = 1
    for d in shape:
        if d is not None:
            n *= d
    return n * jnp.dtype(dtype).itemsize


def _spec(shape, fn):
    return pl.BlockSpec(shape, fn)


class _Exchange:
    def __init__(self, scatter=(), gather=(), layered=()):
        self.arrs = list(scatter) + [a for a, _, _, _ in layered] + list(gather)
        self.n = len(self.arrs)
        self.n_sc = len(scatter) + len(layered)
        self.layer = [None] * len(scatter) + [l for _, l, _, _ in layered] + [None] * len(gather)
        self.shapes = ([a.shape for a in scatter] + [(N_DEV, dp) + a.shape[1:] for a, _, dp, _ in layered]
                       + [(N_DEV,) + a.shape for a in gather])
        self.held = [(len(scatter) + i, b) for i, (_, _, _, b) in enumerate(layered) if b is not None]
        self.operands = self.arrs + [b for _, b in self.held]
        hbm = pl.BlockSpec(memory_space=pltpu.HBM)
        self.in_specs = [hbm] * len(self.operands)
        self.out_specs = [hbm] * self.n
        self.out_shape = [jax.ShapeDtypeStruct(s, a.dtype) for s, a in zip(self.shapes, self.arrs)]
        self.scratch = [pltpu.SemaphoreType.DMA((self.n, N_DEV - 1)), pltpu.SemaphoreType.DMA((self.n, N_DEV - 1)),
                        pltpu.SemaphoreType.DMA((self.n,))]

    def aliases(self, first_in, first_out):
        return {first_in + self.n + i: first_out + a for i, (a, _) in enumerate(self.held)}

    def copies(self, ins, outs, sems):
        send_sems, recv_sems, own_sems = sems
        x, y, c = lax.axis_index("x"), lax.axis_index("y"), lax.axis_index("c")
        me = 4 * x + 2 * y + c
        landing = [outs[a].at[me] if self.layer[a] is None else outs[a].at[me, self.layer[a]] for a in range(self.n)]
        out = [pltpu.make_async_copy(ins[a].at[me] if a < self.n_sc else ins[a], landing[a], own_sems.at[a])
               for a in range(self.n)]
        for k in range(1, N_DEV):
            px = 1 - x if k & 4 else x
            py = 1 - y if k & 2 else y
            pc = 1 - c if k & 1 else c
            peer = 4 * px + 2 * py + pc
            for a in range(self.n):
                out.append(pltpu.make_async_remote_copy(
                    src_ref=ins[a].at[peer] if a < self.n_sc else ins[a], dst_ref=landing[a],
                    send_sem=send_sems.at[a, k - 1], recv_sem=recv_sems.at[a, k - 1],
                    device_id=(px, py, pc), device_id_type=pl.DeviceIdType.MESH))
        return out


def _exchange(name, scatter, gather, layered=()):
    ex = _Exchange(scatter, gather, layered)

    def body(*refs):
        copies = ex.copies(refs[:ex.n], refs[len(ex.operands):len(ex.operands) + ex.n], refs[-3:])
        for cp in copies:
            cp.start()
        for cp in copies:
            cp.wait()

    return pl.pallas_call(body, name=name, in_specs=ex.in_specs, out_specs=ex.out_specs, out_shape=ex.out_shape,
                          input_output_aliases=ex.aliases(0, 0), scratch_shapes=ex.scratch)(*ex.operands)


def _call_beside(ex, body, name, grid, in_specs, out_specs, out_shape, scratch_shapes, vmem_bytes, operands,
                 semantics, in_hbm=True):
    if in_hbm:
        operands = [_in_hbm(a) for a in operands]
    if ex is None:
        res = pl.pallas_call(
            body, name=name, grid=grid, in_specs=in_specs, out_specs=out_specs, out_shape=out_shape,
            scratch_shapes=scratch_shapes,
            compiler_params=pltpu.CompilerParams(dimension_semantics=semantics, vmem_limit_bytes=vmem_bytes),
        )(*operands)
        return res, None
    n_in, n_out, n_scr = len(in_specs), len(out_specs), len(scratch_shapes)
    n_xin = len(ex.operands)

    def fused(*refs):
        mine = refs[:n_in] + refs[n_in + n_xin:n_in + n_xin + n_out]
        mine += refs[n_in + n_xin + n_out + ex.n:n_in + n_xin + n_out + ex.n + n_scr]
        first = pl.program_id(0) == 0
        last = pl.program_id(0) == grid[0] - 1
        for dim in range(1, len(grid)):
            first = jnp.logical_and(first, pl.program_id(dim) == 0)
            last = jnp.logical_and(last, pl.program_id(dim) == grid[dim] - 1)
        x_ins = refs[n_in:n_in + ex.n]
        x_outs = refs[n_in + n_xin + n_out:n_in + n_xin + n_out + ex.n]

        @pl.when(first)
        def _():
            for cp in ex.copies(x_ins, x_outs, refs[-3:]):
                cp.start()

        body(*mine)

        @pl.when(last)
        def _():
            for cp in ex.copies(x_ins, x_outs, refs[-3:]):
                cp.wait()

    res = pl.pallas_call(
        fused, name=name, grid=grid, in_specs=list(in_specs) + ex.in_specs, out_specs=list(out_specs) + ex.out_specs,
        out_shape=list(out_shape) + ex.out_shape, input_output_aliases=ex.aliases(n_in, n_out),
        scratch_shapes=list(scratch_shapes) + ex.scratch,
        compiler_params=pltpu.CompilerParams(dimension_semantics=("arbitrary",) * len(grid),
                                             vmem_limit_bytes=vmem_bytes),
    )(*operands, *ex.operands)
    return res[:n_out], res[n_out:]


NN = (((1,), (0,)), ((), ()))
NT = (((1,), (1,)), ((), ()))
TN = (((0,), (0,)), ((), ()))


def _in_hbm(a):
    return pltpu.with_memory_space_constraint(a, pltpu.HBM)


def _mm(name, a, b, a_spec, b_spec, o_spec, o_shape, o_dtype, grid, dims, beside=None):
    nk = grid[2]
    acc_shape = tuple(d for d in o_spec.block_shape if d is not None)

    def product(a_ref, b_ref):
        return lax.dot_general(a_ref[...].astype(BF16), b_ref[...].astype(BF16), dims, preferred_element_type=F32)

    def body_once(a_ref, b_ref, o_ref):
        o_ref[...] = product(a_ref, b_ref).astype(o_ref.dtype)

    def body(a_ref, b_ref, o_ref, acc_ref):
        k = pl.program_id(2)

        @pl.when(k == 0)
        def _():
            acc_ref[...] = product(a_ref, b_ref)

        @pl.when(k > 0)
        def _():
            acc_ref[...] += product(a_ref, b_ref)

        @pl.when(k == nk - 1)
        def _():
            o_ref[...] = acc_ref[...].astype(o_ref.dtype)

    blk = (_nbytes(a_spec.block_shape, a.dtype) + _nbytes(b_spec.block_shape, b.dtype)
           + _nbytes(acc_shape, o_dtype) + _nbytes(acc_shape, F32))
    res, got = _call_beside(
        beside, body_once if nk == 1 else body, name, grid, [a_spec, b_spec], [o_spec],
        [jax.ShapeDtypeStruct(o_shape, o_dtype)], [] if nk == 1 else [pltpu.VMEM(acc_shape, F32)],
        _vmem_limit(blk), (a, b), ("parallel", "parallel", "arbitrary"), in_hbm=False)
    return res[0] if beside is None else (res[0], got)


def _swiglu_fn(gate_up):
    gate, up = gate_up[0], gate_up[1]
    return gate * jax.nn.sigmoid(gate) * up


def _ffn_in_swiglu(name, h, w, t_m):
    seq, d = h.shape
    n_half, n = w.shape[0] // 2, w.shape[2]

    def body(h_ref, wg_ref, wu_ref, a_ref, f_ref):
        hb = h_ref[...]
        a_ref[0] = lax.dot_general(hb, wg_ref[...], NN, preferred_element_type=F32).astype(a_ref.dtype)
        a_ref[1] = lax.dot_general(hb, wu_ref[...], NN, preferred_element_type=F32).astype(a_ref.dtype)
        f_ref[...] = _swiglu_fn(a_ref[...].astype(F32)).astype(f_ref.dtype)

    blk = 2 * t_m * d + 4 * d * n + 6 * t_m * n + 12 * t_m * n
    return pl.pallas_call(
        body, name=name, grid=(seq // t_m, n_half),
        in_specs=[_spec((t_m, d), lambda i, j: (i, 0)), _spec((None, d, n), lambda i, j: (j, 0, 0)),
                  _spec((None, d, n), lambda i, j: (j + n_half, 0, 0))],
        out_specs=[_spec((2, None, t_m, n), lambda i, j: (0, j, i, 0)), _spec((None, t_m, n), lambda i, j: (j, i, 0))],
        out_shape=[jax.ShapeDtypeStruct((2, n_half, seq, n), FFN_ACT), jax.ShapeDtypeStruct((n_half, seq, n), BF16)],
        compiler_params=pltpu.CompilerParams(dimension_semantics=("parallel", "parallel"),
                                             vmem_limit_bytes=_vmem_limit(blk)),
    )(h, w, w)


def _ffn_out_dx_swiglu(name, d_y, w, a, t_m):
    seq, d = d_y.shape
    n_half, n = w.shape[0], w.shape[1]

    def body(dy_ref, w_ref, a_ref, da_ref):
        d_f = lax.dot_general(dy_ref[...], w_ref[...], NT, preferred_element_type=F32)
        _, pull = jax.vjp(_swiglu_fn, a_ref[...].astype(F32))
        da_ref[...] = pull(d_f)[0].astype(da_ref.dtype)

    blk = 2 * t_m * d + 2 * d * n + 8 * t_m * n + 24 * t_m * n
    return pl.pallas_call(
        body, name=name, grid=(seq // t_m, n_half),
        in_specs=[_spec((t_m, d), lambda i, j: (i, 0)), _spec((None, n, d), lambda i, j: (j, 0, 0)),
                  _spec((2, None, t_m, n), lambda i, j: (0, j, i, 0))],
        out_specs=_spec((2, None, t_m, n), lambda i, j: (0, j, i, 0)),
        out_shape=jax.ShapeDtypeStruct((2, n_half, seq, n), BF16),
        compiler_params=pltpu.CompilerParams(dimension_semantics=("parallel", "parallel"),
                                             vmem_limit_bytes=_vmem_limit(blk)),
    )(d_y, w, a)


def _tile(n, pref=1024):
    t = pref
    while t >= LANES:
        if n % t == 0:
            return t
        t -= LANES
    return n


def _rowwise(name, fn, ins, outs, grid):
    n_in = len(ins)

    def body(*refs):
        vals = fn(*[r[...].astype(F32) for r in refs[:n_in]])
        if not isinstance(vals, (tuple, list)):
            vals = (vals,)
        for r, v in zip(refs[n_in:], vals):
            r[...] = v.astype(r.dtype)

    blk = sum(_nbytes(bs, a.dtype) for a, bs, _ in ins) + sum(_nbytes(bs, d) + _nbytes(bs, F32) for _, d, bs, _ in outs)
    return pl.pallas_call(
        body, name=name, grid=grid,
        in_specs=[_spec(bs, im) for _, bs, im in ins],
        out_specs=[_spec(bs, im) for _, _, bs, im in outs],
        out_shape=[jax.ShapeDtypeStruct(s, d) for s, d, _, _ in outs],
        compiler_params=pltpu.CompilerParams(dimension_semantics=("parallel",) * len(grid),
                                             vmem_limit_bytes=_vmem_limit(2 * blk)),
    )(*[_in_hbm(a) for a, _, _ in ins])


def _rowwise_vjp(name, fn, ins, cts, wrt, grid):
    n_in, n_ct = len(ins), len(cts)
    idx = [w[0] for w in wrt]

    def body(*refs):
        prim = [r[...].astype(F32) for r in refs[:n_in]]
        ct = tuple(r[...].astype(F32) for r in refs[n_in:n_in + n_ct])
        o_refs = refs[n_in + n_ct:]

        def g(*sel):
            full = list(prim)
            for i, s in zip(idx, sel):
                full[i] = s
            out = fn(*full)
            return tuple(out) if isinstance(out, (tuple, list)) else (out,)

        _, pull = jax.vjp(g, *[prim[i] for i in idx])
        grads = pull(ct)
        first = pl.program_id(0) == 0
        for d in range(1, len(grid)):
            first = jnp.logical_and(first, pl.program_id(d) == 0)
        for w, o_ref, gr in zip(wrt, o_refs, grads):
            if w[1] == "row":
                o_ref[...] = gr.astype(o_ref.dtype)
            else:
                @pl.when(first)
                def _(o_ref=o_ref):
                    o_ref[...] = jnp.zeros_like(o_ref)

                o_ref[...] += gr.astype(o_ref.dtype)

    blk = (sum(_nbytes(bs, a.dtype) + _nbytes(bs, F32) for a, bs, _ in list(ins) + list(cts))
           + sum(_nbytes(w[4], w[3]) + _nbytes(w[4], F32) for w in wrt))
    return pl.pallas_call(
        body, name=name, grid=grid,
        in_specs=[_spec(bs, im) for _, bs, im in list(ins) + list(cts)],
        out_specs=[_spec(w[4], w[5]) for w in wrt],
        out_shape=[jax.ShapeDtypeStruct(w[2], w[3]) for w in wrt],
        compiler_params=pltpu.CompilerParams(dimension_semantics=("arbitrary",) * len(grid),
                                             vmem_limit_bytes=_vmem_limit(2 * blk)),
    )(*[_in_hbm(a) for a, _, _ in list(ins) + list(cts)])


def _normalize(x):
    mu = jnp.mean(x, axis=-1, keepdims=True)
    xc = x - mu
    var = jnp.mean(xc * xc, axis=-1, keepdims=True)
    return xc * lax.rsqrt(var + LN_EPS)


def _modulate(x, sc, sh):
    return _normalize(x) * (1.0 + sc) + sh


def _make_resid_fns(alpha):
    def resid_ln(x, y, gate, g, b):
        return _normalize(alpha * x + (1.0 + gate) * y) * g + b

    def resid_ln_mod(x, y, gate, g, b, sc, sh):
        xn = resid_ln(x, y, gate, g, b)
        return xn, _modulate(xn, sc, sh)

    return resid_ln, resid_ln_mod


def _merge_fn(y_sb, y_ssm, gates):
    half = gates.shape[-1] // 2
    return jax.nn.sigmoid(gates[:, :half]) * y_sb + jax.nn.sigmoid(gates[:, half:]) * y_ssm


def _s5_act_fn(yc, u, d_skip):
    return jax.nn.gelu(yc + d_skip * u)


def _s5_glu_fn(yc, u, t, d_skip, b_glu):
    return _s5_act_fn(yc, u, d_skip) * jax.nn.sigmoid(t + b_glu)


def _s5_post_fn(yc, u, t, d_skip, b_glu):
    y1 = _s5_act_fn(yc, u, d_skip)
    return y1, y1 * jax.nn.sigmoid(t + b_glu)


def _sb_tri(kind):
    row = lax.broadcasted_iota(jnp.int32, (SB_BLOCK, SB_BLOCK), 0)
    col = lax.broadcasted_iota(jnp.int32, (SB_BLOCK, SB_BLOCK), 1)
    if kind == "after":
        return (row > col).astype(BF16)
    if kind == "from":
        return (row >= col).astype(BF16)
    return col < row


def _split_dot(x, m):
    hi = x.astype(BF16)
    lo = (x - hi.astype(F32)).astype(BF16)
    return (lax.dot_general(hi, m, NN, preferred_element_type=F32)
            + lax.dot_general(lo, m, NN, preferred_element_type=F32))


def _sb_scores(qh, k2):
    z = lax.dot_general(qh, k2, NT, preferred_element_type=F32)
    log_beta = jnp.minimum(z, 0.0) - jnp.log(1.0 + jnp.exp(-jnp.abs(z)))
    return log_beta, log_beta - z


def _sb_attention_fwd(proj, sb_width, beside=None):
    seq = proj.shape[0]
    n_pair, n_q = sb_width // LANES, seq // SB_BLOCK
    scale = 1.0 / (HEAD_DIM ** 0.5)

    def body(q_ref, k_ref, v_ref, o_ref, o32_ref):
        qi = pl.program_id(1)
        q2 = q_ref[...]
        lane = lax.broadcasted_iota(jnp.int32, (SB_BLOCK, LANES), 1)
        m_after, causal = _sb_tri("after"), _sb_tri("mask")
        heads = [lane < HEAD_DIM, lane >= HEAD_DIM]
        qh = [(jnp.where(m, q2, 0.0) * scale).astype(BF16) for m in heads]

        def scores(kb, diag):
            ks = pl.multiple_of(kb * SB_BLOCK, SB_BLOCK)
            k2 = k_ref[pl.ds(ks, SB_BLOCK), :].astype(BF16)
            out = []
            for h in range(2):
                log_beta, log_1m = _sb_scores(qh[h], k2)
                if diag:
                    log_1m = jnp.where(causal, log_1m, 0.0)
                out += [log_beta + _split_dot(log_1m, m_after), jnp.sum(log_1m, axis=1, keepdims=True)]
            return tuple(out)

        def weigh(kb, sc, carry, acc, diag):
            ks = pl.multiple_of(kb * SB_BLOCK, SB_BLOCK)
            v2 = v_ref[pl.ds(ks, SB_BLOCK), :].astype(BF16)
            out = []
            for h in range(2):
                w = jnp.exp(sc[2 * h] + carry[h])
                if diag:
                    w = jnp.where(causal, w, 0.0)
                out.append(acc[h] + lax.dot_general(w.astype(BF16), v2, NN, preferred_element_type=F32))
            return tuple(out)

        zero = jnp.zeros((SB_BLOCK, LANES), F32)
        zcol = jnp.zeros((SB_BLOCK, 1), F32)
        sc = scores(qi, True)
        acc = weigh(qi, sc, (zcol, zcol), (zero, zero), True)
        carry = (sc[1], sc[3])
        sc = scores(jnp.maximum(qi - 1, 0), False)

        def loop(st):
            kb, sc, carry, acc = st
            after = (carry[0] + sc[1], carry[1] + sc[3])
            done = jnp.maximum(jnp.max(after[0]), jnp.max(after[1])) < SB_UNDERFLOW
            sc_next = scores(jnp.maximum(kb - 1, 0), False)
            acc = weigh(kb, sc, carry, acc, False)
            return jnp.where(done, -1, kb - 1), sc_next, after, acc

        _, _, _, acc = lax.while_loop(lambda st: st[0] >= 0, loop, (qi - 1, sc, carry, acc))
        out = jnp.where(heads[0], acc[0], acc[1])
        o_ref[...] = out.astype(o_ref.dtype)
        o32_ref[...] = out

    q_spec = _spec((SB_BLOCK, LANES), lambda h, i: (i, h))
    kv = [_spec((seq, LANES), lambda h, i, o=o: (0, o + h)) for o in (n_pair, 2 * n_pair)]
    o_spec = _spec((SB_BLOCK, LANES), lambda h, i: (i, h))
    return _call_beside(
        beside, body, "sb_attention_fwd", (n_pair, n_q), [q_spec] + kv, [o_spec, o_spec],
        [jax.ShapeDtypeStruct((seq, sb_width), BF16), jax.ShapeDtypeStruct((seq, sb_width), F32)], [],
        _vmem_limit(2 * seq * LANES * 4), (proj, proj, proj), ("parallel", "arbitrary"))


def _sb_attention_bwd(proj, o32, do, sb_width, beside=None):
    seq = proj.shape[0]
    n_pair, n_q = sb_width // LANES, seq // SB_BLOCK
    scale = 1.0 / (HEAD_DIM ** 0.5)

    def body(q_ref, k_ref, v_ref, o_ref, do_ref, dq_ref, dk_ref, dv_ref):
        qi = pl.program_id(1)

        @pl.when(qi == 0)
        def _():
            dk_ref[...] = jnp.zeros_like(dk_ref)
            dv_ref[...] = jnp.zeros_like(dv_ref)

        q2 = q_ref[...]
        do2 = do_ref[...].astype(F32)
        o2 = o_ref[...]
        lane = lax.broadcasted_iota(jnp.int32, (SB_BLOCK, LANES), 1)
        m_after, m_from, causal = _sb_tri("after"), _sb_tri("from"), _sb_tri("mask")
        heads = [lane < HEAD_DIM, lane >= HEAD_DIM]
        qh = [(jnp.where(m, q2, 0.0) * scale).astype(BF16) for m in heads]
        doh = [jnp.where(m, do2, 0.0) for m in heads]
        doh_b = [v.astype(BF16) for v in doh]
        total = [jnp.sum(v * o2, axis=1, keepdims=True) for v in doh]

        def scores(kb, diag):
            ks = pl.multiple_of(kb * SB_BLOCK, SB_BLOCK)
            k2 = k_ref[pl.ds(ks, SB_BLOCK), :].astype(BF16)
            v2 = v_ref[pl.ds(ks, SB_BLOCK), :].astype(BF16)
            out = []
            for h in range(2):
                log_beta, log_1m = _sb_scores(qh[h], k2)
                if diag:
                    log_1m = jnp.where(causal, log_1m, 0.0)
                out += [log_beta + _split_dot(log_1m, m_after), jnp.sum(log_1m, axis=1, keepdims=True),
                        lax.dot_general(doh_b[h], v2, NT, preferred_element_type=F32), log_beta]
            return tuple(out)

        def pull(kb, sc, carry, right, dq, diag):
            ks = pl.multiple_of(kb * SB_BLOCK, SB_BLOCK)
            k2 = k_ref[pl.ds(ks, SB_BLOCK), :].astype(BF16)
            dv_blk, dk_blk, right_out, dq_out = None, None, [], []
            for h in range(2):
                arg, _, d_w, log_beta = sc[4 * h:4 * h + 4]
                w = jnp.exp(arg + carry[h])
                if diag:
                    w = jnp.where(causal, w, 0.0)
                w_b = w.astype(BF16)
                d_arg = d_w * w_b.astype(F32)
                dv_h = lax.dot_general(w_b, doh_b[h], TN, preferred_element_type=F32)
                d_log_1m = total[h] - right[h] - _split_dot(d_arg, m_from)
                beta = jnp.exp(log_beta)
                dz = d_arg * (1.0 - beta) - beta * d_log_1m
                if diag:
                    dz = jnp.where(causal, dz, 0.0)
                dz_b = dz.astype(BF16)
                dk_h = lax.dot_general(dz_b, qh[h], TN, preferred_element_type=F32)
                dv_blk = dv_h if h == 0 else dv_blk + dv_h
                dk_blk = dk_h if h == 0 else dk_blk + dk_h
                dq_out.append(dq[h] + lax.dot_general(dz_b, k2, NN, preferred_element_type=F32))
                right_out.append(right[h] + jnp.sum(d_arg, axis=1, keepdims=True))
            dv_ref[pl.ds(ks, SB_BLOCK), :] += dv_blk
            dk_ref[pl.ds(ks, SB_BLOCK), :] += dk_blk
            return tuple(right_out), tuple(dq_out)

        zero = jnp.zeros((SB_BLOCK, LANES), F32)
        zcol = jnp.zeros((SB_BLOCK, 1), F32)
        sc = scores(qi, True)
        right, dq = pull(qi, sc, (zcol, zcol), (zcol, zcol), (zero, zero), True)
        carry = (sc[1], sc[5])
        sc = scores(jnp.maximum(qi - 1, 0), False)

        def loop(st):
            kb, sc, carry, right, dq = st
            after = (carry[0] + sc[1], carry[1] + sc[5])
            done = jnp.maximum(jnp.max(after[0]), jnp.max(after[1])) < SB_UNDERFLOW
            sc_next = scores(jnp.maximum(kb - 1, 0), False)
            right, dq = pull(kb, sc, carry, right, dq, False)
            return jnp.where(done, -1, kb - 1), sc_next, after, right, dq

        _, _, _, _, dq = lax.while_loop(lambda st: st[0] >= 0, loop, (qi - 1, sc, carry, right, dq))
        dq_ref[...] = (jnp.where(heads[0], dq[0], dq[1]) * scale).astype(dq_ref.dtype)

    q_spec = _spec((SB_BLOCK, LANES), lambda h, i: (i, h))
    kv = [_spec((seq, LANES), lambda h, i, o=o: (0, o + h)) for o in (n_pair, 2 * n_pair)]
    full = _spec((seq, LANES), lambda h, i: (0, h))
    return _call_beside(
        beside, body, "sb_attention_bwd", (n_pair, n_q), [q_spec] + kv + [q_spec, q_spec], [q_spec, full, full],
        [jax.ShapeDtypeStruct((seq, sb_width), BF16), jax.ShapeDtypeStruct((seq, sb_width), F32),
         jax.ShapeDtypeStruct((seq, sb_width), F32)], [],
        _vmem_limit(4 * seq * LANES * 4), (proj, proj, proj, o32, do), ("parallel", "arbitrary"))


def _s5_discretize(a_re, a_im, log_dt, b_re, b_im, c_re, c_im):
    n_g, n_p = a_re.shape
    c_g = b_re.shape[-1]
    ns = n_g // SLAB_GROUPS
    dt = jnp.exp(log_dt)[:, None]
    xr, xi = a_re * dt, a_im * dt
    mag = jnp.exp(xr)
    lr, li = mag * jnp.cos(xi), mag * jnp.sin(xi)
    den = a_re * a_re + a_im * a_im
    fr = ((lr - 1.0) * a_re + li * a_im) / den
    fi = (li * a_re - (lr - 1.0) * a_im) / den
    bb_re = fr[..., None] * b_re - fi[..., None] * b_im
    bb_im = fr[..., None] * b_im + fi[..., None] * b_re
    eye = jnp.eye(SLAB_GROUPS, dtype=F32)

    def diag_b(m):
        m = jnp.transpose(m.reshape(ns, SLAB_GROUPS, n_p, c_g), (0, 1, 3, 2))
        m = m[:, :, :, None, :] * eye[None, :, None, :, None]
        return m.reshape(ns, SLAB_GROUPS * c_g, SLAB_GROUPS * n_p)

    def diag_c(m):
        m = jnp.transpose(m.reshape(ns, SLAB_GROUPS, c_g, n_p), (0, 1, 3, 2))
        m = m[:, :, :, None, :] * eye[None, :, None, :, None]
        return m.reshape(ns, SLAB_GROUPS * n_p, SLAB_GROUPS * c_g)

    bs = jnp.concatenate([diag_b(bb_re), diag_b(bb_im)], axis=-1)
    cs = jnp.concatenate([diag_c(c_re), -diag_c(c_im)], axis=1)
    lam = jnp.concatenate([lr.reshape(ns, 1, -1), li.reshape(ns, 1, -1)], axis=-1)
    return bs, cs, lam


def _s5_powers(a_re, a_im, log_dt, n):
    n_g, n_p = a_re.shape
    ns = n_g // SLAB_GROUPS
    dt = jnp.exp(log_dt)[:, None]
    mag = jnp.exp(a_re * dt)
    base_r, base_i = mag * jnp.cos(a_im * dt), mag * jnp.sin(a_im * dt)
    steps = jnp.arange(1, n + 1, dtype=jnp.int32)[:, None, None]
    pr, pi = jnp.ones((n, n_g, n_p), F32), jnp.zeros((n, n_g, n_p), F32)
    for b in range(n.bit_length()):
        take = ((steps >> b) & 1) == 1
        pr, pi = (jnp.where(take, pr * base_r - pi * base_i, pr), jnp.where(take, pr * base_i + pi * base_r, pi))
        base_r, base_i = base_r * base_r - base_i * base_i, 2.0 * base_r * base_i

    def slabs(re, im):
        one = lambda m: jnp.transpose(m.reshape(n, ns, SLAB_GROUPS * n_p), (1, 0, 2))
        return jnp.concatenate([one(re), one(im)], axis=-1)

    return slabs(pr, pi), slabs(pr[::-1], -pi[::-1])


def _lanes(j):
    return slice(j * LANES, (j + 1) * LANES)


def _tile8(k):
    return pl.ds(pl.multiple_of(k * SUBLANES, SUBLANES), SUBLANES)


def _s5_interleave(dst_ref, src_ref, t_seg):
    def body(k, _):
        dst_ref[_tile8(k), :] = src_ref[pl.ds(k, SUBLANES, stride=t_seg), :]
        return 0

    lax.fori_loop(0, t_seg, body, 0, unroll=4)


def _s5_join_segments(st_ref, end_ref, car_ref, tab_ref, row, order, n_pair):
    for j in range(n_pair):
        re, im = _lanes(j), _lanes(n_pair + j)
        cr, ci = st_ref[:, re], st_ref[:, im]
        tr, ti = tab_ref[row:row + 1, re], tab_ref[row:row + 1, im]
        for s in order:
            car_ref[s:s + 1, re] = cr
            car_ref[s:s + 1, im] = ci
            er, ei = end_ref[s:s + 1, re], end_ref[s:s + 1, im]
            cr, ci = er + tr * cr - ti * ci, ei + tr * ci + ti * cr
        st_ref[:, re] = cr
        st_ref[:, im] = ci


def _s5_add_carries(buf_ref, car_ref, tab_ref, t_seg, n_pair):
    def fix(k, _):
        rows = _tile8(k)
        tab = tab_ref[pl.ds(k, 1), :]
        for j in range(n_pair):
            re, im = _lanes(j), _lanes(n_pair + j)
            cr, ci = car_ref[:, re], car_ref[:, im]
            tr, ti = tab[:, re], tab[:, im]
            buf_ref[rows, re] += tr * cr - ti * ci
            buf_ref[rows, im] += tr * ci + ti * cr
        return 0

    lax.fori_loop(0, t_seg, fix, 0, unroll=2)


def _s5_scan_fwd(proj, u_col, bs, cs, lam, pw, t_blk, beside=None):
    seq = proj.shape[0]
    ns, _, w2 = bs.shape
    n_pair = w2 // (2 * LANES)
    t_seg, n_t = t_blk // SUBLANES, seq // t_blk

    def body(u_ref, bs_ref, cs_ref, lam_ref, pw_ref, yc_ref, h_ref, st_ref, end_ref, car_ref, ui_ref, bu_ref, yi_ref):
        @pl.when(pl.program_id(1) == 0)
        def _():
            st_ref[...] = jnp.zeros_like(st_ref)

        _s5_interleave(ui_ref, u_ref, t_seg)
        bu_ref[...] = lax.dot_general(ui_ref[...].astype(BF16), bs_ref[...], NN, preferred_element_type=F32)
        lam_r = [jnp.broadcast_to(lam_ref[:, _lanes(j)], (SUBLANES, LANES)) for j in range(n_pair)]
        lam_i = [jnp.broadcast_to(lam_ref[:, _lanes(n_pair + j)], (SUBLANES, LANES)) for j in range(n_pair)]

        def step(k, c):
            rows = _tile8(k)
            out = []
            for j in range(n_pair):
                hr, hi = c[2 * j], c[2 * j + 1]
                nr = lam_r[j] * hr - lam_i[j] * hi + bu_ref[rows, _lanes(j)]
                ni = lam_i[j] * hr + lam_r[j] * hi + bu_ref[rows, _lanes(n_pair + j)]
                h_ref[rows, _lanes(j)] = nr
                h_ref[rows, _lanes(n_pair + j)] = ni
                out += [nr, ni]
            return tuple(out)

        ends = lax.fori_loop(0, t_seg, step, (jnp.zeros((SUBLANES, LANES), F32),) * (2 * n_pair), unroll=4)
        for j in range(n_pair):
            end_ref[:, _lanes(j)] = ends[2 * j]
            end_ref[:, _lanes(n_pair + j)] = ends[2 * j + 1]
        _s5_join_segments(st_ref, end_ref, car_ref, pw_ref, t_seg - 1, list(range(SUBLANES)), n_pair)
        _s5_add_carries(h_ref, car_ref, pw_ref, t_seg, n_pair)
        yi_ref[...] = lax.dot_general(h_ref[...].astype(BF16), cs_ref[...], NN, preferred_element_type=F32)

        def scatter(k, _):
            yc_ref[pl.ds(k, SUBLANES, stride=t_seg), :] = yi_ref[_tile8(k), :]
            return 0

        lax.fori_loop(0, t_seg, scatter, 0, unroll=4)

    return _call_beside(
        beside, body, "s5_scan_fwd", (ns, n_t),
        [_spec((t_blk, LANES), lambda s, i: (i, u_col + s)),
         _spec((None, LANES, w2), lambda s, i: (s, 0, 0)),
         _spec((None, w2, LANES), lambda s, i: (s, 0, 0)),
         _spec((None, 1, w2), lambda s, i: (s, 0, 0)),
         _spec((None, t_seg, w2), lambda s, i: (s, 0, 0))],
        [_spec((t_blk, LANES), lambda s, i: (i, s)),
         _spec((None, t_blk, w2), lambda s, i: (s, i, 0))],
        [jax.ShapeDtypeStruct((seq, ns * LANES), F32), jax.ShapeDtypeStruct((ns, seq, w2), F32)],
        [pltpu.VMEM((1, w2), F32), pltpu.VMEM((SUBLANES, w2), F32), pltpu.VMEM((SUBLANES, w2), F32),
         pltpu.VMEM((t_blk, LANES), F32), pltpu.VMEM((t_blk, w2), F32), pltpu.VMEM((t_blk, LANES), F32)],
        _vmem_limit(3 * t_blk * w2 * 4), (proj, bs, cs, lam, pw), ("parallel", "arbitrary"))


def _s5_scan_bwd(proj, u_col, states, d_yc, du_extra, bs, cs, lam, qw, t_blk):
    seq = proj.shape[0]
    ns, _, w2 = bs.shape
    n_pair = w2 // (2 * LANES)
    t_seg, n_t = t_blk // SUBLANES, seq // t_blk

    def body(u_ref, h_ref, hp_ref, dyc_ref, dux_ref, bs_ref, cs_ref, lam_ref, qw_ref,
             du_ref, dbs_ref, dcs_ref, dlam_ref, g_ref, gd_ref, st_ref, end_ref, car_ref, ui_ref, dyi_ref, dui_ref):
        i = pl.program_id(1)

        @pl.when(i == 0)
        def _():
            st_ref[...] = jnp.zeros_like(st_ref)
            dbs_ref[...] = jnp.zeros_like(dbs_ref)
            dcs_ref[...] = jnp.zeros_like(dcs_ref)
            dlam_ref[...] = jnp.zeros_like(dlam_ref)

        _s5_interleave(ui_ref, u_ref, t_seg)
        _s5_interleave(dyi_ref, dyc_ref, t_seg)
        dyc_b = dyi_ref[...].astype(BF16)
        gd_ref[...] = lax.dot_general(dyc_b, cs_ref[...], NT, preferred_element_type=F32)
        lam_r = [jnp.broadcast_to(lam_ref[:, _lanes(j)], (SUBLANES, LANES)) for j in range(n_pair)]
        lam_i = [jnp.broadcast_to(lam_ref[:, _lanes(n_pair + j)], (SUBLANES, LANES)) for j in range(n_pair)]

        def step(kk, c):
            rows = _tile8(t_seg - 1 - kk)
            out = []
            for j in range(n_pair):
                gr_n, gi_n = c[2 * j], c[2 * j + 1]
                gr = gd_ref[rows, _lanes(j)] + lam_r[j] * gr_n + lam_i[j] * gi_n
                gi = gd_ref[rows, _lanes(n_pair + j)] + lam_r[j] * gi_n - lam_i[j] * gr_n
                g_ref[rows, _lanes(j)] = gr
                g_ref[rows, _lanes(n_pair + j)] = gi
                out += [gr, gi]
            return tuple(out)

        zero = jnp.zeros((SUBLANES, LANES), F32)
        firsts = lax.fori_loop(0, t_seg, step, (zero,) * (2 * n_pair), unroll=4)
        for j in range(n_pair):
            end_ref[:, _lanes(j)] = firsts[2 * j]
            end_ref[:, _lanes(n_pair + j)] = firsts[2 * j + 1]
        _s5_join_segments(st_ref, end_ref, car_ref, qw_ref, 0, list(range(SUBLANES))[::-1], n_pair)
        _s5_add_carries(g_ref, car_ref, qw_ref, t_seg, n_pair)

        def pair_up(k, c):
            rows, prev = _tile8(k), _tile8(k - 1)
            out = []
            for j in range(n_pair):
                re, im = _lanes(j), _lanes(n_pair + j)
                gr, gi, hr, hi = g_ref[rows, re], g_ref[rows, im], h_ref[prev, re], h_ref[prev, im]
                out += [c[2 * j] + gr * hr + gi * hi, c[2 * j + 1] + gi * hr - gr * hi]
            return tuple(out)

        acc = lax.fori_loop(1, t_seg, pair_up, (zero,) * (2 * n_pair), unroll=4)
        has_prev = (i < n_t - 1).astype(F32)
        first_seg = lax.broadcasted_iota(jnp.int32, (SUBLANES, LANES), 0) == 0
        last = _tile8(t_seg - 1)
        for j in range(n_pair):
            re, im = _lanes(j), _lanes(n_pair + j)
            gr, gi = g_ref[0:SUBLANES, re], g_ref[0:SUBLANES, im]
            hr = jnp.where(first_seg, hp_ref[SUBLANES - 1:, re] * has_prev, pltpu.roll(h_ref[last, re], 1, 0))
            hi = jnp.where(first_seg, hp_ref[SUBLANES - 1:, im] * has_prev, pltpu.roll(h_ref[last, im], 1, 0))
            dlam_ref[:, re] += jnp.sum(acc[2 * j] + gr * hr + gi * hi, axis=0, keepdims=True)
            dlam_ref[:, im] += jnp.sum(acc[2 * j + 1] + gi * hr - gr * hi, axis=0, keepdims=True)

        g_b = g_ref[...].astype(BF16)
        dui_ref[...] = lax.dot_general(g_b, bs_ref[...], NT, preferred_element_type=F32)
        dbs_ref[...] += lax.dot_general(ui_ref[...].astype(BF16), g_b, TN, preferred_element_type=F32)
        dcs_ref[...] += lax.dot_general(h_ref[...].astype(BF16), dyc_b, TN, preferred_element_type=F32)

        def scatter(k, _):
            rows = pl.ds(k, SUBLANES, stride=t_seg)
            du_ref[rows, :] = (dui_ref[_tile8(k), :] + dux_ref[rows, :]).astype(du_ref.dtype)
            return 0

        lax.fori_loop(0, t_seg, scatter, 0, unroll=4)

    rev = lambda i: n_t - 1 - i
    return pl.pallas_call(
        body, name="s5_scan_bwd", grid=(ns, n_t),
        in_specs=[_spec((t_blk, LANES), lambda s, i: (rev(i), u_col + s)),
                  _spec((None, t_blk, w2), lambda s, i: (s, rev(i), 0)),
                  _spec((None, SUBLANES, w2), lambda s, i: (s, jnp.maximum(rev(i) * t_seg - 1, 0), 0)),
                  _spec((t_blk, LANES), lambda s, i: (rev(i), s)),
                  _spec((t_blk, LANES), lambda s, i: (rev(i), s)),
                  _spec((None, LANES, w2), lambda s, i: (s, 0, 0)),
                  _spec((None, w2, LANES), lambda s, i: (s, 0, 0)),
                  _spec((None, 1, w2), lambda s, i: (s, 0, 0)),
                  _spec((None, t_seg, w2), lambda s, i: (s, 0, 0))],
        out_specs=[_spec((t_blk, LANES), lambda s, i: (rev(i), s)),
                   _spec((None, LANES, w2), lambda s, i: (s, 0, 0)),
                   _spec((None, w2, LANES), lambda s, i: (s, 0, 0)),
                   _spec((None, 1, w2), lambda s, i: (s, 0, 0))],
        out_shape=[jax.ShapeDtypeStruct((seq, ns * LANES), F32), jax.ShapeDtypeStruct(bs.shape, F32),
                   jax.ShapeDtypeStruct(cs.shape, F32), jax.ShapeDtypeStruct(lam.shape, F32)],
        scratch_shapes=[pltpu.VMEM((t_blk, w2), F32), pltpu.VMEM((t_blk, w2), F32), pltpu.VMEM((1, w2), F32),
                        pltpu.VMEM((SUBLANES, w2), F32), pltpu.VMEM((SUBLANES, w2), F32),
                        pltpu.VMEM((t_blk, LANES), F32), pltpu.VMEM((t_blk, LANES), F32), pltpu.VMEM((t_blk, LANES), F32)],
        compiler_params=pltpu.CompilerParams(dimension_semantics=("parallel", "arbitrary"),
                                             vmem_limit_bytes=_vmem_limit(5 * t_blk * w2 * 4)),
    )(*[_in_hbm(a) for a in (proj, states, states, d_yc, du_extra, bs, cs, lam, qw)])


def _loss_head(y, target, t_m):
    seq, d = y.shape

    def body(y_ref, t_ref, loss_ref, dy_ref):
        @pl.when(pl.program_id(0) == 0)
        def _():
            loss_ref[...] = jnp.zeros_like(loss_ref)

        diff = y_ref[...] - t_ref[...]
        dy_ref[...] = diff / d
        loss_ref[...] += 0.5 * jnp.sum(diff * diff) / d

    row = _spec((t_m, d), lambda i: (i, 0))
    return pl.pallas_call(
        body, name="loss_head", grid=(seq // t_m,), in_specs=[row, row],
        out_specs=[_spec((SUBLANES, LANES), lambda i: (0, 0)), row],
        out_shape=[jax.ShapeDtypeStruct((SUBLANES, LANES), F32), jax.ShapeDtypeStruct((seq, d), F32)],
        compiler_params=pltpu.CompilerParams(dimension_semantics=("arbitrary",),
                                             vmem_limit_bytes=_vmem_limit(6 * t_m * d * 4)),
    )(_in_hbm(y), _in_hbm(target))


def _adamw_fn(w, m, v, *partials):
    g = partials[0]
    for p in partials[1:]:
        g = g + p
    m2 = ADAM_B1 * m + (1.0 - ADAM_B1) * g
    v2 = ADAM_B2 * v + (1.0 - ADAM_B2) * (g * g)
    m_hat = m2 / (1.0 - ADAM_B1 ** ADAM_STEP)
    v_hat = v2 / (1.0 - ADAM_B2 ** ADAM_STEP)
    delta = -ADAM_LR * (m_hat / (jnp.sqrt(v_hat) + ADAM_EPS) + ADAM_WD * w)
    return g, delta, m2, v2


def _adamw(name, w, m, v, partials):
    rows, cols = w.shape
    t_r = rows
    for cand in (512, 256, 128, 64, 32, 16, 8):
        if rows % cand == 0 and cand * cols * 4 <= (1 << 20):
            t_r = cand
            break
    n_p = partials.shape[0]
    row = lambda i: (i, 0)
    ins = [(a, (t_r, cols), row) for a in (w, m, v)]
    ins += [(partials, (None, t_r, cols), (lambda i, j=j: (j, i, 0))) for j in range(n_p)]
    outs = [((rows, cols), F32, (t_r, cols), row)] * 4
    return _rowwise(name, _adamw_fn, ins, outs, (rows // t_r,))


SMALL_PARAMS = ("b_ada", "ssm_a_re", "ssm_a_im", "ssm_log_dt", "ssm_b_re", "ssm_b_im", "ssm_c_re", "ssm_c_im",
                "ssm_d", "b_glu", "ln1_g", "ln1_b", "ln2_g", "ln2_b")
WEIGHTS = ("w_ada", "b_ada", "w_in", "w_sb_up", "ssm_a_re", "ssm_a_im", "ssm_log_dt", "ssm_b_re", "ssm_b_im",
           "ssm_c_re", "ssm_c_im", "ssm_d", "w_glu", "b_glu", "w_ssm_up", "w_out", "ln1_g", "ln1_b", "w_ffn_in",
           "w_ffn_out", "ln2_g", "ln2_b")
ARG_NAMES = (("x", "c") + WEIGHTS + ("loss_target",) + tuple("m_" + n for n in WEIGHTS)
             + tuple("v_" + n for n in WEIGHTS))


def _pack(arrs):
    flat = jnp.concatenate([a.reshape(-1) for a in arrs])
    pad = (-flat.shape[0]) % (PACK_ROWS * LANES)
    return jnp.pad(flat, (0, pad)).reshape(-1, LANES)


def _unpack(packed, like):
    lead = packed.shape[:-2]
    flat = packed.reshape(lead + (-1,))
    out, off = [], 0
    for a in like:
        out.append(flat[..., off:off + a.size].reshape(lead + a.shape))
        off += a.size
    return out


def kernel(x, c, w_ada, b_ada, w_in, w_sb_up, ssm_a_re, ssm_a_im, ssm_log_dt, ssm_b_re, ssm_b_im, ssm_c_re,
           ssm_c_im, ssm_d, w_glu, b_glu, w_ssm_up, w_out, ln1_g, ln1_b, w_ffn_in, w_ffn_out, ln2_g, ln2_b,
           loss_target, m_w_ada, m_b_ada, m_w_in, m_w_sb_up, m_ssm_a_re, m_ssm_a_im, m_ssm_log_dt, m_ssm_b_re,
           m_ssm_b_im, m_ssm_c_re, m_ssm_c_im, m_ssm_d, m_w_glu, m_b_glu, m_w_ssm_up, m_w_out, m_ln1_g, m_ln1_b,
           m_w_ffn_in, m_w_ffn_out, m_ln2_g, m_ln2_b, v_w_ada, v_b_ada, v_w_in, v_w_sb_up, v_ssm_a_re, v_ssm_a_im,
           v_ssm_log_dt, v_ssm_b_re, v_ssm_b_im, v_ssm_c_re, v_ssm_c_im, v_ssm_d, v_w_glu, v_b_glu, v_w_ssm_up,
           v_w_out, v_ln1_g, v_ln1_b, v_w_ffn_in, v_w_ffn_out, v_ln2_g, v_ln2_b):
    given = locals()
    return _train_step({n: given[n] for n in ARG_NAMES})


def _train_step(p):
    x0 = p["x"][0]
    target = p["loss_target"][0]
    seq, d = x0.shape
    depth = p["w_ada"].shape[0]
    n_ada = p["w_ada"].shape[2]
    n_in = p["w_in"].shape[2]
    sb_w = p["w_sb_up"].shape[1]
    ssm_w = p["w_ssm_up"].shape[1]
    n_up = p["w_sb_up"].shape[2]
    n_ffn = p["w_ffn_in"].shape[2]
    ffn = N_DEV * p["w_ffn_out"].shape[1]
    in_cols = N_DEV * n_in
    alpha = (2 * depth) ** 0.25
    resid_ln, resid_ln_mod = _make_resid_fns(alpha)
    t_r = min(512, seq)
    n_r = seq // t_r
    t_m = min(1024, seq)
    n_m = seq // t_m
    t_d = _tile(d)
    assert n_ffn * (N_DEV // 2) == ffn and sb_w % LANES == 0 and ssm_w % LANES == 0 and d % LANES == 0
    assert n_in % LANES == 0 and n_up % LANES == 0 and seq % t_m == 0 and in_cols == 3 * sb_w + ssm_w + 2 * d
    assert (3 * sb_w) % ssm_w == 0 and (3 * sb_w + ssm_w) % (2 * d) == 0

    bf = lambda a: a.astype(BF16)
    got = _exchange("gather_first", [], [bf(p["w_in"][0]), p["c"]])
    wg_in = [got[0]] + [None] * (depth - 1)
    c_all = got[1].reshape(N_DEV, d)
    small_names = ("w_sb_up", "w_ssm_up", "w_glu", "w_out")
    wg_ffn_in, wg_ffn_out, wg = [None] * depth, [None] * depth, {}

    c_pad = jnp.pad(c_all, ((0, 2 * SUBLANES - N_DEV), (0, 0)))
    c_act = _rowwise("silu_c", lambda v: v * jax.nn.sigmoid(v), [(c_pad, c_pad.shape, lambda i: (0, 0))],
                     [(c_pad.shape, F32, c_pad.shape, lambda i: (0, 0))], (1,))[0]
    rows_c = c_pad.shape[0]
    mod_cols = [
        _mm(f"mod_{l}", c_act, p["w_ada"],
            _spec((rows_c, d), lambda i, j, k: (0, 0)), _spec((None, d, n_ada), lambda i, j, k, l=l: (l, 0, 0)),
            _spec((rows_c, n_ada), lambda i, j, k: (0, 0)), (rows_c, n_ada), F32, (1, 1, 1), NN)
        for l in range(depth)]
    mod_send = jnp.stack([m[:N_DEV] for m in mod_cols], axis=1)
    mod_recv = _exchange("exchange_mod", [mod_send], [])[0]
    mod_nobias = jnp.swapaxes(mod_recv, 0, 1).reshape(depth, N_DEV * n_ada)
    full2 = lambda a: (a, a.shape, lambda i: (0, 0))
    mod = _rowwise("mod_bias", lambda a, b: a + b, [full2(mod_nobias), full2(p["b_ada"])],
                   [(mod_nobias.shape, F32, mod_nobias.shape, lambda i: (0, 0))], (1,))[0]
    vec = lambda a: a.reshape(1, -1)
    mods = [[vec(mod[l, j * d:(j + 1) * d]) for j in range(6)] for l in range(depth)]
    ln = {n: [vec(p[n][l]) for l in range(depth)] for n in ("ln1_g", "ln1_b", "ln2_g", "ln2_b")}

    row_spec = lambda width: ((t_r, width), lambda i: (i, 0))
    col_spec = lambda width, cb: ((t_r, width), lambda i, cb=cb: (i, cb))
    vec_spec = lambda width: ((1, width), lambda i: (0, 0))
    rows_in = lambda a: (a,) + row_spec(a.shape[1])
    vec_in = lambda a: (a,) + vec_spec(a.shape[1])
    row_out = lambda width, dt: ((seq, width), dt) + row_spec(width)

    s5 = [_s5_discretize(*[p[n][l] for n in ("ssm_a_re", "ssm_a_im", "ssm_log_dt", "ssm_b_re", "ssm_b_im",
                                               "ssm_c_re", "ssm_c_im")]) for l in range(depth)]
    s5_b16 = [(bs.astype(BF16), cs.astype(BF16), lam) for bs, cs, lam in s5]
    t_scan = min(512, seq)
    s5_pw = [_s5_powers(p["ssm_a_re"][l], p["ssm_a_im"][l], p["ssm_log_dt"][l], t_scan // SUBLANES)
             for l in range(depth)]
    u_col = 3 * sb_w // LANES
    gates_cb = (3 * sb_w + ssm_w) // (2 * d)
    ssm_d = [vec(p["ssm_d"][l]) for l in range(depth)]
    b_glu = [vec(p["b_glu"][l]) for l in range(depth)]
    n_half = N_DEV // 2

    h = _rowwise("modulate_in", _modulate, [rows_in(x0), vec_in(mods[0][1]), vec_in(mods[0][0])],
                 [row_out(d, BF16)], (n_r,))[0]
    saved = []
    x_cur = x0
    for l in range(depth):
        sv = {"x_in": x_cur, "h": h}
        last = l == depth - 1
        t_n = _tile(n_in)
        r_n = n_in // t_n
        proj = _mm(f"proj_{l}", h, wg_in[l],
                   _spec((t_m, d), lambda i, j, k: (i, 0)),
                   _spec((None, d, t_n), lambda i, j, k, r=r_n: (j // r, 0, j % r)),
                   _spec((t_m, t_n), lambda i, j, k: (i, j)), (seq, in_cols), F32, (n_m, N_DEV * r_n, 1), NN)
        arriving = [bf(p["w_ffn_in"][l]), bf(p["w_ffn_out"][l])] + ([bf(p[n]) for n in small_names] if l == 0 else [])
        (o_sb, o_sb32), got = _sb_attention_fwd(proj, sb_w, beside=_Exchange(gather=arriving))
        wg_ffn_in[l] = got[0]
        wg_ffn_out[l] = got[1].reshape(n_half, n_ffn, d)
        if l == 0:
            wg = dict(zip(small_names, got[2:]))
            for n in ("w_glu", "w_out"):
                wg[n] = jnp.swapaxes(wg[n], 0, 1).reshape(depth, -1, wg[n].shape[-1])
            for n in ("w_sb_up", "w_ssm_up"):
                wg[n] = jnp.transpose(wg[n], (1, 2, 0, 3)).reshape(depth, wg[n].shape[2], d)
        bs16, cs16, lam = s5_b16[l]
        (yc, states), got = _s5_scan_fwd(proj, u_col, bs16, cs16, lam, s5_pw[l][0], t_scan,
                                         beside=None if last else _Exchange(gather=[bf(p["w_in"][l + 1])]))
        if not last:
            wg_in[l + 1] = got[0]
        u_in = (proj,) + col_spec(ssm_w, 3 * sb_w // ssm_w)
        y1 = _rowwise(f"s5_act_{l}", _s5_act_fn, [rows_in(yc), u_in, vec_in(ssm_d[l])],
                      [row_out(ssm_w, BF16)], (n_r,))[0]
        t_glu = _mm(f"s5_glu_mm_{l}", y1, wg["w_glu"],
                    _spec((t_m, ssm_w), lambda i, j, k: (i, 0)), _spec((None, ssm_w, ssm_w), lambda i, j, k, l=l: (l, 0, 0)),
                    _spec((t_m, ssm_w), lambda i, j, k: (i, 0)), (seq, ssm_w), F32, (n_m, 1, 1), NN)
        s5_out = _rowwise(f"s5_glu_{l}", _s5_glu_fn,
                          [rows_in(yc), u_in, rows_in(t_glu), vec_in(ssm_d[l]), vec_in(b_glu[l])],
                          [row_out(ssm_w, BF16)], (n_r,))[0]

        def up_proj(name, a, w, l=l):
            return _mm(name, a, w, _spec((t_m, a.shape[1]), lambda i, j, k: (i, 0)),
                       _spec((None, a.shape[1], t_d), lambda i, j, k: (l, 0, j)),
                       _spec((t_m, t_d), lambda i, j, k: (i, j)), (seq, d), F32, (n_m, d // t_d, 1), NN)

        y_sb = up_proj(f"sb_up_{l}", o_sb, wg["w_sb_up"])
        y_ssm = up_proj(f"ssm_up_{l}", s5_out, wg["w_ssm_up"])
        gates = (proj,) + col_spec(2 * d, gates_cb)
        merged = _rowwise(f"merge_{l}", _merge_fn, [rows_in(y_sb), rows_in(y_ssm), gates],
                          [row_out(d, BF16)], (n_r,))[0]
        y_mix = _mm(f"out_proj_{l}", merged, wg["w_out"],
                    _spec((t_m, d), lambda i, j, k: (i, 0)), _spec((None, d, t_d), lambda i, j, k, l=l: (l, 0, j)),
                    _spec((t_m, t_d), lambda i, j, k: (i, j)), (seq, d), F32, (n_m, d // t_d, 1), NN)
        vecs_a = [mods[l][2], ln["ln1_g"][l], ln["ln1_b"][l], mods[l][4], mods[l][3]]
        x_mid, h2 = _rowwise(f"resid_mix_{l}", resid_ln_mod, [rows_in(x_cur), rows_in(y_mix)] + [vec_in(v) for v in vecs_a],
                             [row_out(d, F32), row_out(d, BF16)], (n_r,))
        a_ffn, f_act = _ffn_in_swiglu(f"ffn_in_{l}", h2, wg_ffn_in[l], t_r)
        y_ffn = _mm(f"ffn_out_{l}", f_act, wg_ffn_out[l],
                    _spec((None, t_m, n_ffn), lambda i, j, k: (k, i, 0)),
                    _spec((None, n_ffn, t_d), lambda i, j, k: (k, 0, j)),
                    _spec((t_m, t_d), lambda i, j, k: (i, j)), (seq, d), F32, (n_m, d // t_d, n_half), NN)
        vecs_b = [mods[l][5], ln["ln2_g"][l], ln["ln2_b"][l]] + ([] if last else [mods[l + 1][1], mods[l + 1][0]])
        outs_b = [row_out(d, F32)] + ([] if last else [row_out(d, BF16)])
        res = _rowwise(f"resid_ffn_{l}", resid_ln if last else resid_ln_mod,
                       [rows_in(x_mid), rows_in(y_ffn)] + [vec_in(v) for v in vecs_b], outs_b, (n_r,))
        sv.update(proj=proj, o_sb=o_sb, o_sb32=o_sb32, yc=yc, states=states, y1=y1, t_glu=t_glu, s5_out=s5_out,
                  y_sb=y_sb, y_ssm=y_ssm, merged=merged, y_mix=y_mix, x_mid=x_mid, h2=h2, a_ffn=a_ffn, f_act=f_act,
                  y_ffn=y_ffn, vecs_a=vecs_a, vecs_b=vecs_b)
        saved.append(sv)
        x_cur = res[0]
        h = None if last else res[1]

    loss_part, d_x = _loss_head(x_cur, target, t_r)
    loss = lax.psum(loss_part[0, 0], MESH_AXES)

    d_h_next = None
    grads = {n: [None] * depth for n in WEIGHTS}
    d_mod = [[None] * 6 for _ in range(depth)]
    land = {}
    waiting = []
    row_wrt = lambda i, width, dt: (i, "row", (seq, width), dt) + row_spec(width)
    sum_wrt = lambda i, width: (i, "sum", (1, width), F32) + vec_spec(width)
    for l in reversed(range(depth)):
        sv = saved[l]
        last = l == depth - 1
        ins_b = [rows_in(sv["x_mid"]), rows_in(sv["y_ffn"])] + [vec_in(v) for v in sv["vecs_b"]]
        cts_b = [rows_in(d_x)] + ([] if last else [rows_in(d_h_next)])
        wrt_b = [row_wrt(0, d, F32), row_wrt(1, d, BF16)] + [sum_wrt(2 + j, d) for j in range(len(sv["vecs_b"]))]
        res = _rowwise_vjp(f"resid_ffn_bwd_{l}", resid_ln if last else resid_ln_mod, ins_b, cts_b, wrt_b, (n_r,))
        d_x_mid, d_y_ffn = res[0], res[1]
        d_mod[l][5], grads["ln2_g"][l], grads["ln2_b"][l] = res[2], res[3], res[4]
        if not last:
            d_mod[l + 1][1], d_mod[l + 1][0] = res[5], res[6]
        d_a = _ffn_out_dx_swiglu(f"ffn_out_dx_{l}", d_y_ffn, wg_ffn_out[l], sv["a_ffn"], t_r).reshape(N_DEV, seq, n_ffn)
        g_ffn_out = _mm(f"ffn_out_dw_{l}", sv["f_act"], d_y_ffn,
                        _spec((None, t_m, n_ffn), lambda i, j, k: (i, k, 0)), _spec((t_m, t_d), lambda i, j, k: (k, j)),
                        _spec((None, n_ffn, t_d), lambda i, j, k: (i, 0, j)), (n_half, n_ffn, d), GRAD_WIRE,
                        (n_half, d // t_d, n_m), TN)
        d_h2 = _mm(f"ffn_in_dx_{l}", d_a, wg_ffn_in[l],
                   _spec((None, t_m, n_ffn), lambda i, j, k: (k, i, 0)),
                   _spec((None, t_d, n_ffn), lambda i, j, k: (k, j, 0)),
                   _spec((t_m, t_d), lambda i, j, k: (i, j)), (seq, d), F32, (n_m, d // t_d, N_DEV), NT)
        g_ffn_in = _mm(f"ffn_in_dw_{l}", sv["h2"], d_a,
                       _spec((t_m, t_d), lambda i, j, k: (k, j)), _spec((None, t_m, n_ffn), lambda i, j, k: (i, k, 0)),
                       _spec((None, t_d, n_ffn), lambda i, j, k: (i, j, 0)), (N_DEV, d, n_ffn), GRAD_WIRE,
                       (N_DEV, d // t_d, n_m), TN)
        ins_a = [rows_in(sv["x_in"]), rows_in(sv["y_mix"])] + [vec_in(v) for v in sv["vecs_a"]]
        wrt_a = [row_wrt(0, d, F32), row_wrt(1, d, BF16)] + [sum_wrt(2 + j, d) for j in range(5)]
        res = _rowwise_vjp(f"resid_mix_bwd_{l}", resid_ln_mod, ins_a, [rows_in(d_x_mid), rows_in(d_h2)], wrt_a, (n_r,))
        d_x_in, d_y_mix = res[0], res[1]
        d_mod[l][2], grads["ln1_g"][l], grads["ln1_b"][l], d_mod[l][4], d_mod[l][3] = res[2:7]
        d_merged = _mm(f"out_proj_dx_{l}", d_y_mix, wg["w_out"],
                       _spec((t_m, d), lambda i, j, k: (i, 0)), _spec((None, t_d, d), lambda i, j, k, l=l: (l, j, 0)),
                       _spec((t_m, t_d), lambda i, j, k: (i, j)), (seq, d), F32, (n_m, d // t_d, 1), NT)
        g_out = _mm(f"out_proj_dw_{l}", sv["merged"], d_y_mix,
                    _spec((t_m, t_d), lambda i, j, k: (k, i)), _spec((t_m, t_d), lambda i, j, k: (k, j)),
                    _spec((t_d, t_d), lambda i, j, k: (i, j)), (d, d), GRAD_WIRE, (d // t_d, d // t_d, n_m), TN)
        gates = (sv["proj"],) + col_spec(2 * d, gates_cb)
        d_y_sb, d_y_ssm, d_gates = _rowwise_vjp(
            f"merge_bwd_{l}", _merge_fn, [rows_in(sv["y_sb"]), rows_in(sv["y_ssm"]), gates], [rows_in(d_merged)],
            [row_wrt(0, d, BF16), row_wrt(1, d, BF16), row_wrt(2, 2 * d, BF16)], (n_r,))

        def up_bwd(name, act, d_y, w, dx_dtype, l=l):
            k_w = act.shape[1]
            dx = _mm(name + "_dx", d_y, w, _spec((t_m, d), lambda i, j, k: (i, 0)),
                     _spec((None, k_w, d), lambda i, j, k: (l, 0, 0)),
                     _spec((t_m, k_w), lambda i, j, k: (i, 0)), (seq, k_w), dx_dtype, (n_m, 1, 1), NT)
            dw = _mm(name + "_dw", act, d_y, _spec((t_m, k_w), lambda i, j, k: (k, 0)),
                     _spec((t_m, t_d), lambda i, j, k: (k, j)),
                     _spec((k_w, t_d), lambda i, j, k: (0, j)), (k_w, d), GRAD_WIRE, (1, d // t_d, n_m), TN)
            return dx, jnp.swapaxes(dw.reshape(k_w, N_DEV, n_up), 0, 1)

        d_o_sb, g_sb_up = up_bwd(f"sb_up_{l}", sv["o_sb"], d_y_sb, wg["w_sb_up"], BF16)
        d_s5_out, g_ssm_up = up_bwd(f"ssm_up_{l}", sv["s5_out"], d_y_ssm, wg["w_ssm_up"], F32)
        waiting += [("w_ffn_in", g_ffn_in), ("w_ffn_out", g_ffn_out.reshape(N_DEV, -1, d)),
                    ("w_out", g_out.reshape(N_DEV, -1, d)), ("w_sb_up", g_sb_up), ("w_ssm_up", g_ssm_up)]
        levels = [l + 1] * (len(waiting) - 5) + [l] * 5
        (d_q, d_k, d_v), got = _sb_attention_bwd(
            sv["proj"], sv["o_sb32"], d_o_sb, sb_w,
            beside=_Exchange(layered=[(g, lv, depth, land.get(n)) for (n, g), lv in zip(waiting, levels)]))
        land.update({n: buf for (n, _), buf in zip(waiting, got)})
        u_in = (sv["proj"],) + col_spec(ssm_w, 3 * sb_w // ssm_w)
        ins_s5 = [rows_in(sv["yc"]), u_in, rows_in(sv["t_glu"]), vec_in(ssm_d[l]), vec_in(b_glu[l])]
        d_t = _rowwise_vjp(f"s5_glu_bwd_{l}", _s5_glu_fn, ins_s5, [rows_in(d_s5_out)],
                           [row_wrt(2, ssm_w, BF16)], (n_r,))[0]
        d_y1 = _mm(f"s5_glu_mm_dx_{l}", d_t, wg["w_glu"],
                   _spec((t_m, ssm_w), lambda i, j, k: (i, 0)), _spec((None, ssm_w, ssm_w), lambda i, j, k, l=l: (l, 0, 0)),
                   _spec((t_m, ssm_w), lambda i, j, k: (i, 0)), (seq, ssm_w), F32, (n_m, 1, 1), NT)
        g_glu = _mm(f"s5_glu_mm_dw_{l}", sv["y1"], d_t,
                    _spec((t_m, ssm_w), lambda i, j, k: (k, 0)), _spec((t_m, ssm_w), lambda i, j, k: (k, 0)),
                    _spec((ssm_w, ssm_w), lambda i, j, k: (0, 0)), (ssm_w, ssm_w), GRAD_WIRE, (1, 1, n_m), TN)
        d_yc, d_u_skip, grads["ssm_d"][l], grads["b_glu"][l] = _rowwise_vjp(
            f"s5_post_bwd_{l}", _s5_post_fn, ins_s5, [rows_in(d_y1), rows_in(d_s5_out)],
            [row_wrt(0, ssm_w, F32), row_wrt(1, ssm_w, F32), sum_wrt(3, ssm_w), sum_wrt(4, ssm_w)], (n_r,))
        bs16, cs16, lam = s5_b16[l]
        d_u, d_bs, d_cs, d_lam = _s5_scan_bwd(sv["proj"], u_col, sv["states"], d_yc, d_u_skip, bs16, cs16, lam,
                                              s5_pw[l][1], t_scan)
        raw = [p[n][l] for n in ("ssm_a_re", "ssm_a_im", "ssm_log_dt", "ssm_b_re", "ssm_b_im", "ssm_c_re", "ssm_c_im")]
        _, pull = jax.vjp(_s5_discretize, *raw)
        (grads["ssm_a_re"][l], grads["ssm_a_im"][l], grads["ssm_log_dt"][l], grads["ssm_b_re"][l],
         grads["ssm_b_im"][l], grads["ssm_c_re"][l], grads["ssm_c_im"][l]) = pull((d_bs, d_cs, d_lam))
        d_proj = jnp.concatenate([d_q, d_k.astype(BF16), d_v.astype(BF16), d_u.astype(BF16), d_gates], axis=1)
        t_n = _tile(n_in)
        g_in = _mm(f"proj_dw_{l}", sv["h"], d_proj,
                   _spec((t_m, t_d), lambda i, j, k: (k, j)), _spec((t_m, n_in), lambda i, j, k: (k, i)),
                   _spec((None, t_d, n_in), lambda i, j, k: (i, j, 0)), (N_DEV, d, n_in), GRAD_WIRE,
                   (N_DEV, d // t_d, n_m), TN)
        waiting = [("w_in", g_in), ("w_glu", g_glu.reshape(N_DEV, -1, ssm_w))]
        closing = _Exchange(layered=[(g, 0, depth, land.get(n)) for n, g in waiting]) if l == 0 else None
        d_h = _mm(f"proj_dx_{l}", d_proj, wg_in[l],
                  _spec((t_m, n_in), lambda i, j, k: (i, k)), _spec((None, t_d, n_in), lambda i, j, k: (k, j, 0)),
                  _spec((t_m, t_d), lambda i, j, k: (i, j)), (seq, d), F32, (n_m, d // t_d, N_DEV), NT, beside=closing)
        if l == 0:
            d_h, got = d_h
            land.update({n: buf for (n, _), buf in zip(waiting, got)})
        d_x, d_h_next = d_x_in, d_h
    res = _rowwise_vjp("modulate_in_bwd", lambda v, sc, sh: (v, _modulate(v, sc, sh)),
                       [rows_in(x0), vec_in(mods[0][1]), vec_in(mods[0][0])], [rows_in(d_x), rows_in(d_h_next)],
                       [row_wrt(0, d, F32), sum_wrt(1, d), sum_wrt(2, d)], (n_r,))
    grad_x, d_mod[0][1], d_mod[0][0] = res

    d_mod_rows = jnp.concatenate([jnp.concatenate(d_mod[l], axis=1) for l in range(depth)], axis=0)
    grads["b_ada"] = [d_mod_rows[l] for l in range(depth)]
    small_local = [jnp.stack([g.reshape(p[n].shape[1:]) for g in grads[n]]) for n in SMALL_PARAMS]
    d_mod_send = jnp.swapaxes(d_mod_rows.reshape(depth, N_DEV, n_ada), 0, 1)
    d_mod_cols, small_all = _exchange("exchange_last", [d_mod_send], [_pack(small_local)])
    d_mod_pad = jnp.pad(jnp.swapaxes(d_mod_cols, 0, 1), ((0, 0), (0, rows_c - N_DEV), (0, 0)))
    g_ada = [
        _mm(f"mod_dw_{l}", c_act, d_mod_pad,
            _spec((rows_c, d), lambda i, j, k: (0, 0)), _spec((None, rows_c, n_ada), lambda i, j, k, l=l: (l, 0, 0)),
            _spec((d, n_ada), lambda i, j, k: (0, 0)), (d, n_ada), F32, (1, 1, 1), TN)
        for l in range(depth)]

    out = {}

    def update(name, partials):
        shape = p[name].shape
        two_d = lambda a: a.reshape(-1, shape[-1])
        res = _adamw("adamw_" + name, two_d(p[name]), two_d(p["m_" + name]), two_d(p["v_" + name]),
                     partials.reshape(partials.shape[0], -1, shape[-1]))
        out[name] = [r.reshape(shape) for r in res]

    update("w_ada", jnp.stack(g_ada)[None])
    for n in ("w_in", "w_sb_up", "w_ssm_up", "w_ffn_in", "w_glu", "w_out", "w_ffn_out"):
        update(n, land[n])
    small_w = [p[n] for n in SMALL_PARAMS]
    res = _adamw("adamw_small", _pack(small_w), _pack([p["m_" + n] for n in SMALL_PARAMS]),
                 _pack([p["v_" + n] for n in SMALL_PARAMS]), small_all)
    for kind, packed in enumerate(res):
        for n, a in zip(SMALL_PARAMS, _unpack(packed, small_w)):
            out.setdefault(n, [None] * 4)[kind] = a

    return ((loss, grad_x[None]) + tuple(out[n][0] for n in WEIGHTS) + tuple(out[n][1] for n in WEIGHTS)
            + tuple(out[n][2] for n in WEIGHTS) + tuple(out[n][3] for n in WEIGHTS))
```

```python
import jax
import jax.numpy as jnp
from jax import lax
from jax.experimental import pallas as pl
from jax.experimental.pallas import tpu as pltpu

F32 = jnp.float32
BF16 = jnp.bfloat16
GRAD_WIRE = BF16
FFN_ACT = BF16

N_DEV = 8
LANES = 128
SUBLANES = 8
VMEM_BYTES = 64 * 1024 * 1024
HEAD_DIM = 64
SB_BLOCK = 256
SLAB_GROUPS = 8
LN_EPS = 1e-5
ADAM_LR, ADAM_B1, ADAM_B2, ADAM_EPS, ADAM_WD, ADAM_STEP = 0.001, 0.9, 0.999, 1e-08, 0.01, 10
SB_UNDERFLOW = -120.0

PACK_ROWS = 256
MESH_AXES = ("x", "y", "c")


def _vmem_limit(block_bytes):
    return int(min(max(3 * block_bytes + (8 << 20), 24 << 20), VMEM_BYTES - (8 << 20)))


def _nbytes(shape, dtype):
    n = 1
    for d in shape:
        if d is not None:
            n *= d
    return n * jnp.dtype(dtype).itemsize


def _spec(shape, fn):
    return pl.BlockSpec(shape, fn)


class _Exchange:
    def __init__(self, scatter=(), gather=(), layered=()):
        self.arrs = list(scatter) + [a for a, _, _, _ in layered] + list(gather)
        self.n = len(self.arrs)
        self.n_sc = len(scatter) + len(layered)
        self.layer = [None] * len(scatter) + [l for _, l, _, _ in layered] + [None] * len(gather)
        self.shapes = ([a.shape for a in scatter] + [(N_DEV, dp) + a.shape[1:] for a, _, dp, _ in layered]
                       + [(N_DEV,) + a.shape for a in gather])
        self.held = [(len(scatter) + i, b) for i, (_, _, _, b) in enumerate(layered) if b is not None]
        self.operands = self.arrs + [b for _, b in self.held]
        hbm = pl.BlockSpec(memory_space=pltpu.HBM)
        self.in_specs = [hbm] * len(self.operands)
        self.out_specs = [hbm] * self.n
        self.out_shape = [jax.ShapeDtypeStruct(s, a.dtype) for s, a in zip(self.shapes, self.arrs)]
        self.scratch = [pltpu.SemaphoreType.DMA((self.n, N_DEV - 1)), pltpu.SemaphoreType.DMA((self.n, N_DEV - 1)),
                        pltpu.SemaphoreType.DMA((self.n,))]

    def aliases(self, first_in, first_out):
        return {first_in + self.n + i: first_out + a for i, (a, _) in enumerate(self.held)}

    def copies(self, ins, outs, sems):
        send_sems, recv_sems, own_sems = sems
        x, y, c = lax.axis_index("x"), lax.axis_index("y"), lax.axis_index("c")
        me = 4 * x + 2 * y + c
        landing = [outs[a].at[me] if self.layer[a] is None else outs[a].at[me, self.layer[a]] for a in range(self.n)]
        out = [pltpu.make_async_copy(ins[a].at[me] if a < self.n_sc else ins[a], landing[a], own_sems.at[a])
               for a in range(self.n)]
        for k in range(1, N_DEV):
            px = 1 - x if k & 4 else x
            py = 1 - y if k & 2 else y
            pc = 1 - c if k & 1 else c
            peer = 4 * px + 2 * py + pc
            for a in range(self.n):
                out.append(pltpu.make_async_remote_copy(
                    src_ref=ins[a].at[peer] if a < self.n_sc else ins[a], dst_ref=landing[a],
                    send_sem=send_sems.at[a, k - 1], recv_sem=recv_sems.at[a, k - 1],
                    device_id=(px, py, pc), device_id_type=pl.DeviceIdType.MESH))
        return out


def _exchange(name, scatter, gather, layered=()):
    ex = _Exchange(scatter, gather, layered)

    def body(*refs):
        copies = ex.copies(refs[:ex.n], refs[len(ex.operands):len(ex.operands) + ex.n], refs[-3:])
        for cp in copies:
            cp.start()
        for cp in copies:
            cp.wait()

    return pl.pallas_call(body, name=name, in_specs=ex.in_specs, out_specs=ex.out_specs, out_shape=ex.out_shape,
                          input_output_aliases=ex.aliases(0, 0), scratch_shapes=ex.scratch)(*ex.operands)


def _reduce_packed(name, packed, scatter):
    rows = packed.shape[0]
    blk = rows // N_DEV
    ex = _Exchange(scatter=[packed.reshape(N_DEV, blk, LANES)] + list(scatter))
    n_in = len(ex.operands)

    def body(*refs):
        ins, outs = refs[:ex.n], refs[n_in:n_in + ex.n]
        total_ref = refs[n_in + ex.n]
        sems, (send2, recv2, own2, load_sem) = refs[n_in + ex.n + 1:n_in + ex.n + 4], refs[n_in + ex.n + 4:-2]
        land_v, sum_v = refs[-2:]
        copies = ex.copies(ins, outs, sems)
        for cp in copies:
            cp.start()
        for cp in copies:
            cp.wait()
        load = pltpu.make_async_copy(outs[0], land_v, load_sem)
        load.start()
        load.wait()
        acc = land_v[0]
        for i in range(1, N_DEV):
            acc = acc + land_v[i]
        sum_v[...] = acc
        x, y, c = lax.axis_index("x"), lax.axis_index("y"), lax.axis_index("c")
        me = 4 * x + 2 * y + c
        back = [pltpu.make_async_copy(sum_v, total_ref.at[me], own2)]
        for k in range(1, N_DEV):
            peer = (1 - x if k & 4 else x, 1 - y if k & 2 else y, 1 - c if k & 1 else c)
            back.append(pltpu.make_async_remote_copy(
                src_ref=sum_v, dst_ref=total_ref.at[me], send_sem=send2.at[k - 1], recv_sem=recv2.at[k - 1],
                device_id=peer, device_id_type=pl.DeviceIdType.MESH))
        for cp in back:
            cp.start()
        for cp in back:
            cp.wait()

    hbm = pl.BlockSpec(memory_space=pltpu.HBM)
    res = pl.pallas_call(
        body, name=name, in_specs=ex.in_specs, out_specs=ex.out_specs + [hbm],
        out_shape=ex.out_shape + [jax.ShapeDtypeStruct((N_DEV, blk, LANES), F32)],
        scratch_shapes=ex.scratch + [pltpu.SemaphoreType.DMA((N_DEV - 1,)), pltpu.SemaphoreType.DMA((N_DEV - 1,)),
                                     pltpu.SemaphoreType.DMA, pltpu.SemaphoreType.DMA,
                                     pltpu.VMEM((N_DEV, blk, LANES), F32), pltpu.VMEM((blk, LANES), F32)],
    )(*ex.operands)
    return res[-1].reshape(rows, LANES), res[1:-1]


def _call_beside(ex, body, name, grid, in_specs, out_specs, out_shape, scratch_shapes, vmem_bytes, operands,
                 semantics, in_hbm=True):
    if in_hbm:
        operands = [_in_hbm(a) for a in operands]
    if ex is None:
        res = pl.pallas_call(
            body, name=name, grid=grid, in_specs=in_specs, out_specs=out_specs, out_shape=out_shape,
            scratch_shapes=scratch_shapes,
            compiler_params=pltpu.CompilerParams(dimension_semantics=semantics, vmem_limit_bytes=vmem_bytes),
        )(*operands)
        return res, None
    n_in, n_out, n_scr = len(in_specs), len(out_specs), len(scratch_shapes)
    n_xin = len(ex.operands)

    def fused(*refs):
        mine = refs[:n_in] + refs[n_in + n_xin:n_in + n_xin + n_out]
        mine += refs[n_in + n_xin + n_out + ex.n:n_in + n_xin + n_out + ex.n + n_scr]
        first = pl.program_id(0) == 0
        last = pl.program_id(0) == grid[0] - 1
        for dim in range(1, len(grid)):
            first = jnp.logical_and(first, pl.program_id(dim) == 0)
            last = jnp.logical_and(last, pl.program_id(dim) == grid[dim] - 1)
        x_ins = refs[n_in:n_in + ex.n]
        x_outs = refs[n_in + n_xin + n_out:n_in + n_xin + n_out + ex.n]

        @pl.when(first)
        def _():
            for cp in ex.copies(x_ins, x_outs, refs[-3:]):
                cp.start()

        body(*mine)

        @pl.when(last)
        def _():
            for cp in ex.copies(x_ins, x_outs, refs[-3:]):
                cp.wait()

    res = pl.pallas_call(
        fused, name=name, grid=grid, in_specs=list(in_specs) + ex.in_specs, out_specs=list(out_specs) + ex.out_specs,
        out_shape=list(out_shape) + ex.out_shape, input_output_aliases=ex.aliases(n_in, n_out),
        scratch_shapes=list(scratch_shapes) + ex.scratch,
        compiler_params=pltpu.CompilerParams(dimension_semantics=("arbitrary",) * len(grid),
                                             vmem_limit_bytes=vmem_bytes),
    )(*operands, *ex.operands)
    return res[:n_out], res[n_out:]


NN = (((1,), (0,)), ((), ()))
NT = (((1,), (1,)), ((), ()))
TN = (((0,), (0,)), ((), ()))


def _in_hbm(a):
    return pltpu.with_memory_space_constraint(a, pltpu.HBM)


def _mm(name, a, b, a_spec, b_spec, o_spec, o_shape, o_dtype, grid, dims, beside=None):
    nk = grid[2]
    acc_shape = tuple(d for d in o_spec.block_shape if d is not None)

    def product(a_ref, b_ref):
        return lax.dot_general(a_ref[...].astype(BF16), b_ref[...].astype(BF16), dims, preferred_element_type=F32)

    def body_once(a_ref, b_ref, o_ref):
        o_ref[...] = product(a_ref, b_ref).astype(o_ref.dtype)

    def body(a_ref, b_ref, o_ref, acc_ref):
        k = pl.program_id(2)

        @pl.when(k == 0)
        def _():
            acc_ref[...] = product(a_ref, b_ref)

        @pl.when(k > 0)
        def _():
            acc_ref[...] += product(a_ref, b_ref)

        @pl.when(k == nk - 1)
        def _():
            o_ref[...] = acc_ref[...].astype(o_ref.dtype)

    blk = (_nbytes(a_spec.block_shape, a.dtype) + _nbytes(b_spec.block_shape, b.dtype)
           + _nbytes(acc_shape, o_dtype) + _nbytes(acc_shape, F32))
    res, got = _call_beside(
        beside, body_once if nk == 1 else body, name, grid, [a_spec, b_spec], [o_spec],
        [jax.ShapeDtypeStruct(o_shape, o_dtype)], [] if nk == 1 else [pltpu.VMEM(acc_shape, F32)],
        _vmem_limit(blk), (a, b), ("parallel", "parallel", "arbitrary"), in_hbm=False)
    return res[0] if beside is None else (res[0], got)


def _swiglu_fn(gate_up):
    gate, up = gate_up[0], gate_up[1]
    return gate * jax.nn.sigmoid(gate) * up


def _ffn_in_swiglu(name, h, w, t_m):
    seq, d = h.shape
    n_half, n = w.shape[0] // 2, w.shape[2]

    def body(h_ref, wg_ref, wu_ref, a_ref, f_ref):
        hb = h_ref[...]
        a_ref[0] = lax.dot_general(hb, wg_ref[...], NN, preferred_element_type=F32).astype(a_ref.dtype)
        a_ref[1] = lax.dot_general(hb, wu_ref[...], NN, preferred_element_type=F32).astype(a_ref.dtype)
        f_ref[...] = _swiglu_fn(a_ref[...].astype(F32)).astype(f_ref.dtype)

    blk = 2 * t_m * d + 4 * d * n + 6 * t_m * n + 12 * t_m * n
    return pl.pallas_call(
        body, name=name, grid=(seq // t_m, n_half),
        in_specs=[_spec((t_m, d), lambda i, j: (i, 0)), _spec((None, d, n), lambda i, j: (j, 0, 0)),
                  _spec((None, d, n), lambda i, j: (j + n_half, 0, 0))],
        out_specs=[_spec((2, None, t_m, n), lambda i, j: (0, j, i, 0)), _spec((None, t_m, n), lambda i, j: (j, i, 0))],
        out_shape=[jax.ShapeDtypeStruct((2, n_half, seq, n), FFN_ACT), jax.ShapeDtypeStruct((n_half, seq, n), BF16)],
        compiler_params=pltpu.CompilerParams(dimension_semantics=("parallel", "parallel"),
                                             vmem_limit_bytes=_vmem_limit(blk)),
    )(h, w, w)


def _ffn_out_dx_swiglu(name, d_y, w, a, t_m):
    seq, d = d_y.shape
    n_half, n = w.shape[0], w.shape[1]

    def body(dy_ref, w_ref, a_ref, da_ref):
        d_f = lax.dot_general(dy_ref[...], w_ref[...], NT, preferred_element_type=F32)
        _, pull = jax.vjp(_swiglu_fn, a_ref[...].astype(F32))
        da_ref[...] = pull(d_f)[0].astype(da_ref.dtype)

    blk = 2 * t_m * d + 2 * d * n + 8 * t_m * n + 24 * t_m * n
    return pl.pallas_call(
        body, name=name, grid=(seq // t_m, n_half),
        in_specs=[_spec((t_m, d), lambda i, j: (i, 0)), _spec((None, n, d), lambda i, j: (j, 0, 0)),
                  _spec((2, None, t_m, n), lambda i, j: (0, j, i, 0))],
        out_specs=_spec((2, None, t_m, n), lambda i, j: (0, j, i, 0)),
        out_shape=jax.ShapeDtypeStruct((2, n_half, seq, n), BF16),
        compiler_params=pltpu.CompilerParams(dimension_semantics=("parallel", "parallel"),
                                             vmem_limit_bytes=_vmem_limit(blk)),
    )(d_y, w, a)


def _tile(n, pref=1024):
    t = pref
    while t >= LANES:
        if n % t == 0:
            return t
        t -= LANES
    return n


def _rowwise(name, fn, ins, outs, grid):
    n_in = len(ins)

    def body(*refs):
        vals = fn(*[r[...].astype(F32) for r in refs[:n_in]])
        if not isinstance(vals, (tuple, list)):
            vals = (vals,)
        for r, v in zip(refs[n_in:], vals):
            r[...] = v.astype(r.dtype)

    blk = sum(_nbytes(bs, a.dtype) for a, bs, _ in ins) + sum(_nbytes(bs, d) + _nbytes(bs, F32) for _, d, bs, _ in outs)
    return pl.pallas_call(
        body, name=name, grid=grid,
        in_specs=[_spec(bs, im) for _, bs, im in ins],
        out_specs=[_spec(bs, im) for _, _, bs, im in outs],
        out_shape=[jax.ShapeDtypeStruct(s, d) for s, d, _, _ in outs],
        compiler_params=pltpu.CompilerParams(dimension_semantics=("parallel",) * len(grid),
                                             vmem_limit_bytes=_vmem_limit(2 * blk)),
    )(*[_in_hbm(a) for a, _, _ in ins])


def _rowwise_vjp(name, fn, ins, cts, wrt, grid):
    n_in, n_ct = len(ins), len(cts)
    idx = [w[0] for w in wrt]

    def body(*refs):
        prim = [r[...].astype(F32) for r in refs[:n_in]]
        ct = tuple(r[...].astype(F32) for r in refs[n_in:n_in + n_ct])
        o_refs = refs[n_in + n_ct:]

        def g(*sel):
            full = list(prim)
            for i, s in zip(idx, sel):
                full[i] = s
            out = fn(*full)
            return tuple(out) if isinstance(out, (tuple, list)) else (out,)

        _, pull = jax.vjp(g, *[prim[i] for i in idx])
        grads = pull(ct)
        first = pl.program_id(0) == 0
        for d in range(1, len(grid)):
            first = jnp.logical_and(first, pl.program_id(d) == 0)
        for w, o_ref, gr in zip(wrt, o_refs, grads):
            if w[1] == "row":
                o_ref[...] = gr.astype(o_ref.dtype)
            else:
                @pl.when(first)
                def _(o_ref=o_ref):
                    o_ref[...] = jnp.zeros_like(o_ref)

                o_ref[...] += gr.astype(o_ref.dtype)

    blk = (sum(_nbytes(bs, a.dtype) + _nbytes(bs, F32) for a, bs, _ in list(ins) + list(cts))
           + sum(_nbytes(w[4], w[3]) + _nbytes(w[4], F32) for w in wrt))
    return pl.pallas_call(
        body, name=name, grid=grid,
        in_specs=[_spec(bs, im) for _, bs, im in list(ins) + list(cts)],
        out_specs=[_spec(w[4], w[5]) for w in wrt],
        out_shape=[jax.ShapeDtypeStruct(w[2], w[3]) for w in wrt],
        compiler_params=pltpu.CompilerParams(dimension_semantics=("arbitrary",) * len(grid),
                                             vmem_limit_bytes=_vmem_limit(2 * blk)),
    )(*[_in_hbm(a) for a, _, _ in list(ins) + list(cts)])


def _normalize(x):
    mu = jnp.mean(x, axis=-1, keepdims=True)
    xc = x - mu
    var = jnp.mean(xc * xc, axis=-1, keepdims=True)
    return xc * lax.rsqrt(var + LN_EPS)


def _modulate(x, sc, sh):
    return _normalize(x) * (1.0 + sc) + sh


def _make_resid_fns(alpha):
    def resid_ln(x, y, gate, g, b):
        return _normalize(alpha * x + (1.0 + gate) * y) * g + b

    def resid_ln_mod(x, y, gate, g, b, sc, sh):
        xn = resid_ln(x, y, gate, g, b)
        return xn, _modulate(xn, sc, sh)

    return resid_ln, resid_ln_mod


def _merge_fn(y_sb, y_ssm, gates):
    half = gates.shape[-1] // 2
    return jax.nn.sigmoid(gates[:, :half]) * y_sb + jax.nn.sigmoid(gates[:, half:]) * y_ssm


def _s5_act_fn(yc, u, d_skip):
    return jax.nn.gelu(yc + d_skip * u)


def _s5_glu_fn(yc, u, t, d_skip, b_glu):
    return _s5_act_fn(yc, u, d_skip) * jax.nn.sigmoid(t + b_glu)


def _s5_post_fn(yc, u, t, d_skip, b_glu):
    y1 = _s5_act_fn(yc, u, d_skip)
    return y1, y1 * jax.nn.sigmoid(t + b_glu)


def _sb_tri(kind):
    row = lax.broadcasted_iota(jnp.int32, (SB_BLOCK, SB_BLOCK), 0)
    col = lax.broadcasted_iota(jnp.int32, (SB_BLOCK, SB_BLOCK), 1)
    if kind == "after":
        return (row > col).astype(BF16)
    if kind == "from":
        return (row >= col).astype(BF16)
    return col < row


def _split_dot(x, m):
    hi = x.astype(BF16)
    lo = (x - hi.astype(F32)).astype(BF16)
    return (lax.dot_general(hi, m, NN, preferred_element_type=F32)
            + lax.dot_general(lo, m, NN, preferred_element_type=F32))


def _sb_scores(qh, k2):
    z = lax.dot_general(qh, k2, NT, preferred_element_type=F32)
    log_beta = jnp.minimum(z, 0.0) - jnp.log(1.0 + jnp.exp(-jnp.abs(z)))
    return log_beta, log_beta - z


def _sb_attention_fwd(proj, sb_width, beside=None):
    seq = proj.shape[0]
    n_pair, n_q = sb_width // LANES, seq // SB_BLOCK
    scale = 1.0 / (HEAD_DIM ** 0.5)

    def body(q_ref, k_ref, v_ref, o_ref, o32_ref):
        qi = pl.program_id(1)
        q2 = q_ref[...]
        lane = lax.broadcasted_iota(jnp.int32, (SB_BLOCK, LANES), 1)
        m_after, causal = _sb_tri("after"), _sb_tri("mask")
        heads = [lane < HEAD_DIM, lane >= HEAD_DIM]
        qh = [(jnp.where(m, q2, 0.0) * scale).astype(BF16) for m in heads]

        def scores(kb, diag):
            ks = pl.multiple_of(kb * SB_BLOCK, SB_BLOCK)
            k2 = k_ref[pl.ds(ks, SB_BLOCK), :].astype(BF16)
            out = []
            for h in range(2):
                log_beta, log_1m = _sb_scores(qh[h], k2)
                if diag:
                    log_1m = jnp.where(causal, log_1m, 0.0)
                out += [log_beta + _split_dot(log_1m, m_after), jnp.sum(log_1m, axis=1, keepdims=True)]
            return tuple(out)

        def weigh(kb, sc, carry, acc, diag):
            ks = pl.multiple_of(kb * SB_BLOCK, SB_BLOCK)
            v2 = v_ref[pl.ds(ks, SB_BLOCK), :].astype(BF16)
            out = []
            for h in range(2):
                w = jnp.exp(sc[2 * h] + carry[h])
                if diag:
                    w = jnp.where(causal, w, 0.0)
                out.append(acc[h] + lax.dot_general(w.astype(BF16), v2, NN, preferred_element_type=F32))
            return tuple(out)

        zero = jnp.zeros((SB_BLOCK, LANES), F32)
        zcol = jnp.zeros((SB_BLOCK, 1), F32)
        sc = scores(qi, True)
        acc = weigh(qi, sc, (zcol, zcol), (zero, zero), True)
        carry = (sc[1], sc[3])

        def loop(st):
            kb, carry, acc = st
            sc = scores(kb, False)
            after = (carry[0] + sc[1], carry[1] + sc[3])
            done = jnp.maximum(jnp.max(after[0]), jnp.max(after[1])) < SB_UNDERFLOW
            acc = weigh(kb, sc, carry, acc, False)
            return jnp.where(done, -1, kb - 1), after, acc

        _, _, acc = lax.while_loop(lambda st: st[0] >= 0, loop, (qi - 1, carry, acc))
        out = jnp.where(heads[0], acc[0], acc[1])
        o_ref[...] = out.astype(o_ref.dtype)
        o32_ref[...] = out

    q_spec = _spec((SB_BLOCK, LANES), lambda h, i: (i, h))
    kv = [_spec((seq, LANES), lambda h, i, o=o: (0, o + h)) for o in (n_pair, 2 * n_pair)]
    o_spec = _spec((SB_BLOCK, LANES), lambda h, i: (i, h))
    return _call_beside(
        beside, body, "sb_attention_fwd", (n_pair, n_q), [q_spec] + kv, [o_spec, o_spec],
        [jax.ShapeDtypeStruct((seq, sb_width), BF16), jax.ShapeDtypeStruct((seq, sb_width), F32)], [],
        _vmem_limit(2 * seq * LANES * 4), (proj, proj, proj), ("parallel", "arbitrary"))


def _sb_attention_bwd(proj, o32, do, sb_width, beside=None):
    seq = proj.shape[0]
    n_pair, n_q = sb_width // LANES, seq // SB_BLOCK
    scale = 1.0 / (HEAD_DIM ** 0.5)

    def body(q_ref, k_ref, v_ref, o_ref, do_ref, dq_ref, dk_ref, dv_ref):
        qi = pl.program_id(1)

        @pl.when(qi == 0)
        def _():
            dk_ref[...] = jnp.zeros_like(dk_ref)
            dv_ref[...] = jnp.zeros_like(dv_ref)

        q2 = q_ref[...]
        do2 = do_ref[...].astype(F32)
        o2 = o_ref[...]
        lane = lax.broadcasted_iota(jnp.int32, (SB_BLOCK, LANES), 1)
        m_after, m_from, causal = _sb_tri("after"), _sb_tri("from"), _sb_tri("mask")
        heads = [lane < HEAD_DIM, lane >= HEAD_DIM]
        qh = [(jnp.where(m, q2, 0.0) * scale).astype(BF16) for m in heads]
        doh = [jnp.where(m, do2, 0.0) for m in heads]
        doh_b = [v.astype(BF16) for v in doh]
        total = [jnp.sum(v * o2, axis=1, keepdims=True) for v in doh]

        def scores(kb, diag):
            ks = pl.multiple_of(kb * SB_BLOCK, SB_BLOCK)
            k2 = k_ref[pl.ds(ks, SB_BLOCK), :].astype(BF16)
            v2 = v_ref[pl.ds(ks, SB_BLOCK), :].astype(BF16)
            out = []
            for h in range(2):
                log_beta, log_1m = _sb_scores(qh[h], k2)
                if diag:
                    log_1m = jnp.where(causal, log_1m, 0.0)
                out += [log_beta + _split_dot(log_1m, m_after), jnp.sum(log_1m, axis=1, keepdims=True),
                        lax.dot_general(doh_b[h], v2, NT, preferred_element_type=F32), log_beta]
            return tuple(out)

        def pull(kb, sc, carry, right, dq, diag):
            ks = pl.multiple_of(kb * SB_BLOCK, SB_BLOCK)
            k2 = k_ref[pl.ds(ks, SB_BLOCK), :].astype(BF16)
            dv_blk, dk_blk, right_out, dq_out = None, None, [], []
            for h in range(2):
                arg, _, d_w, log_beta = sc[4 * h:4 * h + 4]
                w = jnp.exp(arg + carry[h])
                if diag:
                    w = jnp.where(causal, w, 0.0)
                w_b = w.astype(BF16)
                d_arg = d_w * w_b.astype(F32)
                dv_h = lax.dot_general(w_b, doh_b[h], TN, preferred_element_type=F32)
                d_log_1m = total[h] - right[h] - _split_dot(d_arg, m_from)
                beta = jnp.exp(log_beta)
                dz = d_arg * (1.0 - beta) - beta * d_log_1m
                if diag:
                    dz = jnp.where(causal, dz, 0.0)
                dz_b = dz.astype(BF16)
                dk_h = lax.dot_general(dz_b, qh[h], TN, preferred_element_type=F32)
                dv_blk = dv_h if h == 0 else dv_blk + dv_h
                dk_blk = dk_h if h == 0 else dk_blk + dk_h
                dq_out.append(dq[h] + lax.dot_general(dz_b, k2, NN, preferred_element_type=F32))
                right_out.append(right[h] + jnp.sum(d_arg, axis=1, keepdims=True))
            dv_ref[pl.ds(ks, SB_BLOCK), :] += dv_blk
            dk_ref[pl.ds(ks, SB_BLOCK), :] += dk_blk
            return tuple(right_out), tuple(dq_out)

        zero = jnp.zeros((SB_BLOCK, LANES), F32)
        zcol = jnp.zeros((SB_BLOCK, 1), F32)
        sc = scores(qi, True)
        right, dq = pull(qi, sc, (zcol, zcol), (zcol, zcol), (zero, zero), True)
        carry = (sc[1], sc[5])

        def loop(st):
            kb, carry, right, dq = st
            sc = scores(kb, False)
            after = (carry[0] + sc[1], carry[1] + sc[5])
            done = jnp.maximum(jnp.max(after[0]), jnp.max(after[1])) < SB_UNDERFLOW
            right, dq = pull(kb, sc, carry, right, dq, False)
            return jnp.where(done, -1, kb - 1), after, right, dq

        _, _, _, dq = lax.while_loop(lambda st: st[0] >= 0, loop, (qi - 1, carry, right, dq))
        dq_ref[...] = (jnp.where(heads[0], dq[0], dq[1]) * scale).astype(dq_ref.dtype)

    q_spec = _spec((SB_BLOCK, LANES), lambda h, i: (i, h))
    kv = [_spec((seq, LANES), lambda h, i, o=o: (0, o + h)) for o in (n_pair, 2 * n_pair)]
    full = _spec((seq, LANES), lambda h, i: (0, h))
    return _call_beside(
        beside, body, "sb_attention_bwd", (n_pair, n_q), [q_spec] + kv + [q_spec, q_spec], [q_spec, full, full],
        [jax.ShapeDtypeStruct((seq, sb_width), BF16), jax.ShapeDtypeStruct((seq, sb_width), F32),
         jax.ShapeDtypeStruct((seq, sb_width), F32)], [],
        _vmem_limit(4 * seq * LANES * 4), (proj, proj, proj, o32, do), ("parallel", "arbitrary"))


def _s5_discretize(a_re, a_im, log_dt, b_re, b_im, c_re, c_im):
    n_g, n_p = a_re.shape
    c_g = b_re.shape[-1]
    ns = n_g // SLAB_GROUPS
    dt = jnp.exp(log_dt)[:, None]
    xr, xi = a_re * dt, a_im * dt
    mag = jnp.exp(xr)
    lr, li = mag * jnp.cos(xi), mag * jnp.sin(xi)
    den = a_re * a_re + a_im * a_im
    fr = ((lr - 1.0) * a_re + li * a_im) / den
    fi = (li * a_re - (lr - 1.0) * a_im) / den
    bb_re = fr[..., None] * b_re - fi[..., None] * b_im
    bb_im = fr[..., None] * b_im + fi[..., None] * b_re
    eye = jnp.eye(SLAB_GROUPS, dtype=F32)

    def diag_b(m):
        m = jnp.transpose(m.reshape(ns, SLAB_GROUPS, n_p, c_g), (0, 1, 3, 2))
        m = m[:, :, :, None, :] * eye[None, :, None, :, None]
        return m.reshape(ns, SLAB_GROUPS * c_g, SLAB_GROUPS * n_p)

    def diag_c(m):
        m = jnp.transpose(m.reshape(ns, SLAB_GROUPS, c_g, n_p), (0, 1, 3, 2))
        m = m[:, :, :, None, :] * eye[None, :, None, :, None]
        return m.reshape(ns, SLAB_GROUPS * n_p, SLAB_GROUPS * c_g)

    bs = jnp.concatenate([diag_b(bb_re), diag_b(bb_im)], axis=-1)
    cs = jnp.concatenate([diag_c(c_re), -diag_c(c_im)], axis=1)
    lam = jnp.concatenate([lr.reshape(ns, 1, -1), li.reshape(ns, 1, -1)], axis=-1)
    return bs, cs, lam


def _s5_powers(a_re, a_im, log_dt, n):
    n_g, n_p = a_re.shape
    ns = n_g // SLAB_GROUPS
    dt = jnp.exp(log_dt)[:, None]
    mag = jnp.exp(a_re * dt)
    base_r, base_i = mag * jnp.cos(a_im * dt), mag * jnp.sin(a_im * dt)
    steps = jnp.arange(1, n + 1, dtype=jnp.int32)[:, None, None]
    pr, pi = jnp.ones((n, n_g, n_p), F32), jnp.zeros((n, n_g, n_p), F32)
    for b in range(n.bit_length()):
        take = ((steps >> b) & 1) == 1
        pr, pi = (jnp.where(take, pr * base_r - pi * base_i, pr), jnp.where(take, pr * base_i + pi * base_r, pi))
        base_r, base_i = base_r * base_r - base_i * base_i, 2.0 * base_r * base_i

    def slabs(re, im):
        one = lambda m: jnp.transpose(m.reshape(n, ns, SLAB_GROUPS * n_p), (1, 0, 2))
        return jnp.concatenate([one(re), one(im)], axis=-1)

    return slabs(pr, pi), slabs(pr[::-1], -pi[::-1])


def _lanes(j):
    return slice(j * LANES, (j + 1) * LANES)


def _tile8(k):
    return pl.ds(pl.multiple_of(k * SUBLANES, SUBLANES), SUBLANES)


def _s5_interleave(dst_ref, src_ref, t_seg):
    def body(k, _):
        dst_ref[_tile8(k), :] = src_ref[pl.ds(k, SUBLANES, stride=t_seg), :]
        return 0

    lax.fori_loop(0, t_seg, body, 0, unroll=4)


def _s5_join_segments(st_ref, end_ref, car_ref, tab_ref, row, order, n_pair):
    for j in range(n_pair):
        re, im = _lanes(j), _lanes(n_pair + j)
        cr, ci = st_ref[:, re], st_ref[:, im]
        tr, ti = tab_ref[row:row + 1, re], tab_ref[row:row + 1, im]
        for s in order:
            car_ref[s:s + 1, re] = cr
            car_ref[s:s + 1, im] = ci
            er, ei = end_ref[s:s + 1, re], end_ref[s:s + 1, im]
            cr, ci = er + tr * cr - ti * ci, ei + tr * ci + ti * cr
        st_ref[:, re] = cr
        st_ref[:, im] = ci


def _s5_add_carries(buf_ref, car_ref, tab_ref, t_seg, n_pair):
    def fix(k, _):
        rows = _tile8(k)
        tab = tab_ref[pl.ds(k, 1), :]
        for j in range(n_pair):
            re, im = _lanes(j), _lanes(n_pair + j)
            cr, ci = car_ref[:, re], car_ref[:, im]
            tr, ti = tab[:, re], tab[:, im]
            buf_ref[rows, re] += tr * cr - ti * ci
            buf_ref[rows, im] += tr * ci + ti * cr
        return 0

    lax.fori_loop(0, t_seg, fix, 0, unroll=2)


def _s5_scan_fwd(proj, u_col, bs, cs, lam, pw, t_blk, beside=None):
    seq = proj.shape[0]
    ns, _, w2 = bs.shape
    n_pair = w2 // (2 * LANES)
    t_seg, n_t = t_blk // SUBLANES, seq // t_blk

    def body(u_ref, bs_ref, cs_ref, lam_ref, pw_ref, yc_ref, h_ref, st_ref, end_ref, car_ref, ui_ref, bu_ref, yi_ref):
        @pl.when(pl.program_id(1) == 0)
        def _():
            st_ref[...] = jnp.zeros_like(st_ref)

        _s5_interleave(ui_ref, u_ref, t_seg)
        bu_ref[...] = lax.dot_general(ui_ref[...].astype(BF16), bs_ref[...], NN, preferred_element_type=F32)
        lam_r = [jnp.broadcast_to(lam_ref[:, _lanes(j)], (SUBLANES, LANES)) for j in range(n_pair)]
        lam_i = [jnp.broadcast_to(lam_ref[:, _lanes(n_pair + j)], (SUBLANES, LANES)) for j in range(n_pair)]

        def step(k, c):
            rows = _tile8(k)
            out = []
            for j in range(n_pair):
                hr, hi = c[2 * j], c[2 * j + 1]
                nr = lam_r[j] * hr - lam_i[j] * hi + bu_ref[rows, _lanes(j)]
                ni = lam_i[j] * hr + lam_r[j] * hi + bu_ref[rows, _lanes(n_pair + j)]
                h_ref[rows, _lanes(j)] = nr
                h_ref[rows, _lanes(n_pair + j)] = ni
                out += [nr, ni]
            return tuple(out)

        ends = lax.fori_loop(0, t_seg, step, (jnp.zeros((SUBLANES, LANES), F32),) * (2 * n_pair), unroll=4)
        for j in range(n_pair):
            end_ref[:, _lanes(j)] = ends[2 * j]
            end_ref[:, _lanes(n_pair + j)] = ends[2 * j + 1]
        _s5_join_segments(st_ref, end_ref, car_ref, pw_ref, t_seg - 1, list(range(SUBLANES)), n_pair)
        _s5_add_carries(h_ref, car_ref, pw_ref, t_seg, n_pair)
        yi_ref[...] = lax.dot_general(h_ref[...].astype(BF16), cs_ref[...], NN, preferred_element_type=F32)

        def scatter(k, _):
            yc_ref[pl.ds(k, SUBLANES, stride=t_seg), :] = yi_ref[_tile8(k), :]
            return 0

        lax.fori_loop(0, t_seg, scatter, 0, unroll=4)

    return _call_beside(
        beside, body, "s5_scan_fwd", (ns, n_t),
        [_spec((t_blk, LANES), lambda s, i: (i, u_col + s)),
         _spec((None, LANES, w2), lambda s, i: (s, 0, 0)),
         _spec((None, w2, LANES), lambda s, i: (s, 0, 0)),
         _spec((None, 1, w2), lambda s, i: (s, 0, 0)),
         _spec((None, t_seg, w2), lambda s, i: (s, 0, 0))],
        [_spec((t_blk, LANES), lambda s, i: (i, s)),
         _spec((None, t_blk, w2), lambda s, i: (s, i, 0))],
        [jax.ShapeDtypeStruct((seq, ns * LANES), F32), jax.ShapeDtypeStruct((ns, seq, w2), F32)],
        [pltpu.VMEM((1, w2), F32), pltpu.VMEM((SUBLANES, w2), F32), pltpu.VMEM((SUBLANES, w2), F32),
         pltpu.VMEM((t_blk, LANES), F32), pltpu.VMEM((t_blk, w2), F32), pltpu.VMEM((t_blk, LANES), F32)],
        _vmem_limit(3 * t_blk * w2 * 4), (proj, bs, cs, lam, pw), ("parallel", "arbitrary"))


def _s5_scan_bwd(proj, u_col, states, d_yc, du_extra, bs, cs, lam, qw, t_blk):
    seq = proj.shape[0]
    ns, _, w2 = bs.shape
    n_pair = w2 // (2 * LANES)
    t_seg, n_t = t_blk // SUBLANES, seq // t_blk

    def body(u_ref, h_ref, hp_ref, dyc_ref, dux_ref, bs_ref, cs_ref, lam_ref, qw_ref,
             du_ref, dbs_ref, dcs_ref, dlam_ref, g_ref, gd_ref, st_ref, end_ref, car_ref, ui_ref, dyi_ref, dui_ref):
        i = pl.program_id(1)

        @pl.when(i == 0)
        def _():
            st_ref[...] = jnp.zeros_like(st_ref)
            dbs_ref[...] = jnp.zeros_like(dbs_ref)
            dcs_ref[...] = jnp.zeros_like(dcs_ref)
            dlam_ref[...] = jnp.zeros_like(dlam_ref)

        _s5_interleave(ui_ref, u_ref, t_seg)
        _s5_interleave(dyi_ref, dyc_ref, t_seg)
        dyc_b = dyi_ref[...].astype(BF16)
        gd_ref[...] = lax.dot_general(dyc_b, cs_ref[...], NT, preferred_element_type=F32)
        lam_r = [jnp.broadcast_to(lam_ref[:, _lanes(j)], (SUBLANES, LANES)) for j in range(n_pair)]
        lam_i = [jnp.broadcast_to(lam_ref[:, _lanes(n_pair + j)], (SUBLANES, LANES)) for j in range(n_pair)]

        def step(kk, c):
            rows = _tile8(t_seg - 1 - kk)
            out = []
            for j in range(n_pair):
                gr_n, gi_n = c[2 * j], c[2 * j + 1]
                gr = gd_ref[rows, _lanes(j)] + lam_r[j] * gr_n + lam_i[j] * gi_n
                gi = gd_ref[rows, _lanes(n_pair + j)] + lam_r[j] * gi_n - lam_i[j] * gr_n
                g_ref[rows, _lanes(j)] = gr
                g_ref[rows, _lanes(n_pair + j)] = gi
                out += [gr, gi]
            return tuple(out)

        zero = jnp.zeros((SUBLANES, LANES), F32)
        firsts = lax.fori_loop(0, t_seg, step, (zero,) * (2 * n_pair), unroll=4)
        for j in range(n_pair):
            end_ref[:, _lanes(j)] = firsts[2 * j]
            end_ref[:, _lanes(n_pair + j)] = firsts[2 * j + 1]
        _s5_join_segments(st_ref, end_ref, car_ref, qw_ref, 0, list(range(SUBLANES))[::-1], n_pair)
        _s5_add_carries(g_ref, car_ref, qw_ref, t_seg, n_pair)

        def pair_up(k, c):
            rows, prev = _tile8(k), _tile8(k - 1)
            out = []
            for j in range(n_pair):
                re, im = _lanes(j), _lanes(n_pair + j)
                gr, gi, hr, hi = g_ref[rows, re], g_ref[rows, im], h_ref[prev, re], h_ref[prev, im]
                out += [c[2 * j] + gr * hr + gi * hi, c[2 * j + 1] + gi * hr - gr * hi]
            return tuple(out)

        acc = lax.fori_loop(1, t_seg, pair_up, (zero,) * (2 * n_pair), unroll=4)
        has_prev = (i < n_t - 1).astype(F32)
        first_seg = lax.broadcasted_iota(jnp.int32, (SUBLANES, LANES), 0) == 0
        last = _tile8(t_seg - 1)
        for j in range(n_pair):
            re, im = _lanes(j), _lanes(n_pair + j)
            gr, gi = g_ref[0:SUBLANES, re], g_ref[0:SUBLANES, im]
            hr = jnp.where(first_seg, hp_ref[SUBLANES - 1:, re] * has_prev, pltpu.roll(h_ref[last, re], 1, 0))
            hi = jnp.where(first_seg, hp_ref[SUBLANES - 1:, im] * has_prev, pltpu.roll(h_ref[last, im], 1, 0))
            dlam_ref[:, re] += jnp.sum(acc[2 * j] + gr * hr + gi * hi, axis=0, keepdims=True)
            dlam_ref[:, im] += jnp.sum(acc[2 * j + 1] + gi * hr - gr * hi, axis=0, keepdims=True)

        g_b = g_ref[...].astype(BF16)
        dui_ref[...] = lax.dot_general(g_b, bs_ref[...], NT, preferred_element_type=F32)
        dbs_ref[...] += lax.dot_general(ui_ref[...].astype(BF16), g_b, TN, preferred_element_type=F32)
        dcs_ref[...] += lax.dot_general(h_ref[...].astype(BF16), dyc_b, TN, preferred_element_type=F32)

        def scatter(k, _):
            rows = pl.ds(k, SUBLANES, stride=t_seg)
            du_ref[rows, :] = (dui_ref[_tile8(k), :] + dux_ref[rows, :]).astype(du_ref.dtype)
            return 0

        lax.fori_loop(0, t_seg, scatter, 0, unroll=4)

    rev = lambda i: n_t - 1 - i
    return pl.pallas_call(
        body, name="s5_scan_bwd", grid=(ns, n_t),
        in_specs=[_spec((t_blk, LANES), lambda s, i: (rev(i), u_col + s)),
                  _spec((None, t_blk, w2), lambda s, i: (s, rev(i), 0)),
                  _spec((None, SUBLANES, w2), lambda s, i: (s, jnp.maximum(rev(i) * t_seg - 1, 0), 0)),
                  _spec((t_blk, LANES), lambda s, i: (rev(i), s)),
                  _spec((t_blk, LANES), lambda s, i: (rev(i), s)),
                  _spec((None, LANES, w2), lambda s, i: (s, 0, 0)),
                  _spec((None, w2, LANES), lambda s, i: (s, 0, 0)),
                  _spec((None, 1, w2), lambda s, i: (s, 0, 0)),
                  _spec((None, t_seg, w2), lambda s, i: (s, 0, 0))],
        out_specs=[_spec((t_blk, LANES), lambda s, i: (rev(i), s)),
                   _spec((None, LANES, w2), lambda s, i: (s, 0, 0)),
                   _spec((None, w2, LANES), lambda s, i: (s, 0, 0)),
                   _spec((None, 1, w2), lambda s, i: (s, 0, 0))],
        out_shape=[jax.ShapeDtypeStruct((seq, ns * LANES), F32), jax.ShapeDtypeStruct(bs.shape, F32),
                   jax.ShapeDtypeStruct(cs.shape, F32), jax.ShapeDtypeStruct(lam.shape, F32)],
        scratch_shapes=[pltpu.VMEM((t_blk, w2), F32), pltpu.VMEM((t_blk, w2), F32), pltpu.VMEM((1, w2), F32),
                        pltpu.VMEM((SUBLANES, w2), F32), pltpu.VMEM((SUBLANES, w2), F32),
                        pltpu.VMEM((t_blk, LANES), F32), pltpu.VMEM((t_blk, LANES), F32), pltpu.VMEM((t_blk, LANES), F32)],
        compiler_params=pltpu.CompilerParams(dimension_semantics=("parallel", "arbitrary"),
                                             vmem_limit_bytes=_vmem_limit(5 * t_blk * w2 * 4)),
    )(*[_in_hbm(a) for a in (proj, states, states, d_yc, du_extra, bs, cs, lam, qw)])


def _loss_head(y, target, t_m):
    seq, d = y.shape

    def body(y_ref, t_ref, loss_ref, dy_ref):
        @pl.when(pl.program_id(0) == 0)
        def _():
            loss_ref[...] = jnp.zeros_like(loss_ref)

        diff = y_ref[...] - t_ref[...]
        dy_ref[...] = diff / d
        loss_ref[...] += 0.5 * jnp.sum(diff * diff) / d

    row = _spec((t_m, d), lambda i: (i, 0))
    return pl.pallas_call(
        body, name="loss_head", grid=(seq // t_m,), in_specs=[row, row],
        out_specs=[_spec((SUBLANES, LANES), lambda i: (0, 0)), row],
        out_shape=[jax.ShapeDtypeStruct((SUBLANES, LANES), F32), jax.ShapeDtypeStruct((seq, d), F32)],
        compiler_params=pltpu.CompilerParams(dimension_semantics=("arbitrary",),
                                             vmem_limit_bytes=_vmem_limit(6 * t_m * d * 4)),
    )(_in_hbm(y), _in_hbm(target))


def _adamw_fn(w, m, v, *partials):
    g = partials[0]
    for p in partials[1:]:
        g = g + p
    m2 = ADAM_B1 * m + (1.0 - ADAM_B1) * g
    v2 = ADAM_B2 * v + (1.0 - ADAM_B2) * (g * g)
    m_hat = m2 / (1.0 - ADAM_B1 ** ADAM_STEP)
    v_hat = v2 / (1.0 - ADAM_B2 ** ADAM_STEP)
    delta = -ADAM_LR * (m_hat / (jnp.sqrt(v_hat) + ADAM_EPS) + ADAM_WD * w)
    return g, delta, m2, v2


def _adamw(name, w, m, v, partials):
    rows, cols = w.shape
    t_r = rows
    for cand in (512, 256, 128, 64, 32, 16, 8):
        if rows % cand == 0 and cand * cols * 4 <= (1 << 20):
            t_r = cand
            break
    n_p = partials.shape[0]
    row = lambda i: (i, 0)
    ins = [(a, (t_r, cols), row) for a in (w, m, v)]
    ins += [(partials, (None, t_r, cols), (lambda i, j=j: (j, i, 0))) for j in range(n_p)]
    outs = [((rows, cols), F32, (t_r, cols), row)] * 4
    return _rowwise(name, _adamw_fn, ins, outs, (rows // t_r,))


SMALL_PARAMS = ("b_ada", "ssm_a_re", "ssm_a_im", "ssm_log_dt", "ssm_b_re", "ssm_b_im", "ssm_c_re", "ssm_c_im",
                "ssm_d", "b_glu", "ln1_g", "ln1_b", "ln2_g", "ln2_b")
WEIGHTS = ("w_ada", "b_ada", "w_in", "w_sb_up", "ssm_a_re", "ssm_a_im", "ssm_log_dt", "ssm_b_re", "ssm_b_im",
           "ssm_c_re", "ssm_c_im", "ssm_d", "w_glu", "b_glu", "w_ssm_up", "w_out", "ln1_g", "ln1_b", "w_ffn_in",
           "w_ffn_out", "ln2_g", "ln2_b")
ARG_NAMES = (("x", "c") + WEIGHTS + ("loss_target",) + tuple("m_" + n for n in WEIGHTS)
             + tuple("v_" + n for n in WEIGHTS))


def _pack(arrs):
    flat = jnp.concatenate([a.reshape(-1) for a in arrs])
    pad = (-flat.shape[0]) % (PACK_ROWS * LANES)
    return jnp.pad(flat, (0, pad)).reshape(-1, LANES)


def _unpack(packed, like):
    lead = packed.shape[:-2]
    flat = packed.reshape(lead + (-1,))
    out, off = [], 0
    for a in like:
        out.append(flat[..., off:off + a.size].reshape(lead + a.shape))
        off += a.size
    return out


def kernel(x, c, w_ada, b_ada, w_in, w_sb_up, ssm_a_re, ssm_a_im, ssm_log_dt, ssm_b_re, ssm_b_im, ssm_c_re,
           ssm_c_im, ssm_d, w_glu, b_glu, w_ssm_up, w_out, ln1_g, ln1_b, w_ffn_in, w_ffn_out, ln2_g, ln2_b,
           loss_target, m_w_ada, m_b_ada, m_w_in, m_w_sb_up, m_ssm_a_re, m_ssm_a_im, m_ssm_log_dt, m_ssm_b_re,
           m_ssm_b_im, m_ssm_c_re, m_ssm_c_im, m_ssm_d, m_w_glu, m_b_glu, m_w_ssm_up, m_w_out, m_ln1_g, m_ln1_b,
           m_w_ffn_in, m_w_ffn_out, m_ln2_g, m_ln2_b, v_w_ada, v_b_ada, v_w_in, v_w_sb_up, v_ssm_a_re, v_ssm_a_im,
           v_ssm_log_dt, v_ssm_b_re, v_ssm_b_im, v_ssm_c_re, v_ssm_c_im, v_ssm_d, v_w_glu, v_b_glu, v_w_ssm_up,
           v_w_out, v_ln1_g, v_ln1_b, v_w_ffn_in, v_w_ffn_out, v_ln2_g, v_ln2_b):
    given = locals()
    return _train_step({n: given[n] for n in ARG_NAMES})


def _train_step(p):
    x0 = p["x"][0]
    target = p["loss_target"][0]
    seq, d = x0.shape
    depth = p["w_ada"].shape[0]
    n_ada = p["w_ada"].shape[2]
    n_in = p["w_in"].shape[2]
    sb_w = p["w_sb_up"].shape[1]
    ssm_w = p["w_ssm_up"].shape[1]
    n_up = p["w_sb_up"].shape[2]
    n_ffn = p["w_ffn_in"].shape[2]
    ffn = N_DEV * p["w_ffn_out"].shape[1]
    in_cols = N_DEV * n_in
    alpha = (2 * depth) ** 0.25
    resid_ln, resid_ln_mod = _make_resid_fns(alpha)
    t_r = min(512, seq)
    n_r = seq // t_r
    t_m = min(1024, seq)
    n_m = seq // t_m
    t_d = _tile(d)
    assert n_ffn * (N_DEV // 2) == ffn and sb_w % LANES == 0 and ssm_w % LANES == 0 and d % LANES == 0
    assert n_in % LANES == 0 and n_up % LANES == 0 and seq % t_m == 0 and in_cols == 3 * sb_w + ssm_w + 2 * d
    assert (3 * sb_w) % ssm_w == 0 and (3 * sb_w + ssm_w) % (2 * d) == 0

    bf = lambda a: a.astype(BF16)
    got = _exchange("gather_first", [], [bf(p["w_in"][0]), p["c"]])
    wg_in = [got[0]] + [None] * (depth - 1)
    c_all = got[1].reshape(N_DEV, d)
    small_names = ("w_sb_up", "w_ssm_up", "w_glu", "w_out")
    wg_ffn_in, wg_ffn_out, wg = [None] * depth, [None] * depth, {}

    c_pad = jnp.pad(c_all, ((0, 2 * SUBLANES - N_DEV), (0, 0)))
    c_act = _rowwise("silu_c", lambda v: v * jax.nn.sigmoid(v), [(c_pad, c_pad.shape, lambda i: (0, 0))],
                     [(c_pad.shape, F32, c_pad.shape, lambda i: (0, 0))], (1,))[0]
    rows_c = c_pad.shape[0]
    mod_cols = [
        _mm(f"mod_{l}", c_act, p["w_ada"],
            _spec((rows_c, d), lambda i, j, k: (0, 0)), _spec((None, d, n_ada), lambda i, j, k, l=l: (l, 0, 0)),
            _spec((rows_c, n_ada), lambda i, j, k: (0, 0)), (rows_c, n_ada), F32, (1, 1, 1), NN)
        for l in range(depth)]
    mod_send = jnp.stack([m[:N_DEV] for m in mod_cols], axis=1)
    mod_recv = _exchange("exchange_mod", [mod_send], [])[0]
    mod_nobias = jnp.swapaxes(mod_recv, 0, 1).reshape(depth, N_DEV * n_ada)
    full2 = lambda a: (a, a.shape, lambda i: (0, 0))
    mod = _rowwise("mod_bias", lambda a, b: a + b, [full2(mod_nobias), full2(p["b_ada"])],
                   [(mod_nobias.shape, F32, mod_nobias.shape, lambda i: (0, 0))], (1,))[0]
    vec = lambda a: a.reshape(1, -1)
    mods = [[vec(mod[l, j * d:(j + 1) * d]) for j in range(6)] for l in range(depth)]
    ln = {n: [vec(p[n][l]) for l in range(depth)] for n in ("ln1_g", "ln1_b", "ln2_g", "ln2_b")}

    row_spec = lambda width: ((t_r, width), lambda i: (i, 0))
    col_spec = lambda width, cb: ((t_r, width), lambda i, cb=cb: (i, cb))
    vec_spec = lambda width: ((1, width), lambda i: (0, 0))
    rows_in = lambda a: (a,) + row_spec(a.shape[1])
    vec_in = lambda a: (a,) + vec_spec(a.shape[1])
    row_out = lambda width, dt: ((seq, width), dt) + row_spec(width)

    s5 = [_s5_discretize(*[p[n][l] for n in ("ssm_a_re", "ssm_a_im", "ssm_log_dt", "ssm_b_re", "ssm_b_im",
                                               "ssm_c_re", "ssm_c_im")]) for l in range(depth)]
    s5_b16 = [(bs.astype(BF16), cs.astype(BF16), lam) for bs, cs, lam in s5]
    t_scan = min(512, seq)
    s5_pw = [_s5_powers(p["ssm_a_re"][l], p["ssm_a_im"][l], p["ssm_log_dt"][l], t_scan // SUBLANES)
             for l in range(depth)]
    u_col = 3 * sb_w // LANES
    gates_cb = (3 * sb_w + ssm_w) // (2 * d)
    ssm_d = [vec(p["ssm_d"][l]) for l in range(depth)]
    b_glu = [vec(p["b_glu"][l]) for l in range(depth)]
    n_half = N_DEV // 2

    h = _rowwise("modulate_in", _modulate, [rows_in(x0), vec_in(mods[0][1]), vec_in(mods[0][0])],
                 [row_out(d, BF16)], (n_r,))[0]
    saved = []
    x_cur = x0
    for l in range(depth):
        sv = {"x_in": x_cur, "h": h}
        last = l == depth - 1
        t_n = _tile(n_in)
        r_n = n_in // t_n
        proj = _mm(f"proj_{l}", h, wg_in[l],
                   _spec((t_m, d), lambda i, j, k: (i, 0)),
                   _spec((None, d, t_n), lambda i, j, k, r=r_n: (j // r, 0, j % r)),
                   _spec((t_m, t_n), lambda i, j, k: (i, j)), (seq, in_cols), F32, (n_m, N_DEV * r_n, 1), NN)
        arriving = [bf(p["w_ffn_in"][l]), bf(p["w_ffn_out"][l])] + ([bf(p[n]) for n in small_names] if l == 0 else [])
        (o_sb, o_sb32), got = _sb_attention_fwd(proj, sb_w, beside=_Exchange(gather=arriving))
        wg_ffn_in[l] = got[0]
        wg_ffn_out[l] = got[1].reshape(n_half, n_ffn, d)
        if l == 0:
            wg = dict(zip(small_names, got[2:]))
            for n in ("w_glu", "w_out"):
                wg[n] = jnp.swapaxes(wg[n], 0, 1).reshape(depth, -1, wg[n].shape[-1])
            for n in ("w_sb_up", "w_ssm_up"):
                wg[n] = jnp.transpose(wg[n], (1, 2, 0, 3)).reshape(depth, wg[n].shape[2], d)
        bs16, cs16, lam = s5_b16[l]
        (yc, states), got = _s5_scan_fwd(proj, u_col, bs16, cs16, lam, s5_pw[l][0], t_scan,
                                         beside=None if last else _Exchange(gather=[bf(p["w_in"][l + 1])]))
        if not last:
            wg_in[l + 1] = got[0]
        u_in = (proj,) + col_spec(ssm_w, 3 * sb_w // ssm_w)
        y1 = _rowwise(f"s5_act_{l}", _s5_act_fn, [rows_in(yc), u_in, vec_in(ssm_d[l])],
                      [row_out(ssm_w, BF16)], (n_r,))[0]
        t_glu = _mm(f"s5_glu_mm_{l}", y1, wg["w_glu"],
                    _spec((t_m, ssm_w), lambda i, j, k: (i, 0)), _spec((None, ssm_w, ssm_w), lambda i, j, k, l=l: (l, 0, 0)),
                    _spec((t_m, ssm_w), lambda i, j, k: (i, 0)), (seq, ssm_w), F32, (n_m, 1, 1), NN)
        s5_out = _rowwise(f"s5_glu_{l}", _s5_glu_fn,
                          [rows_in(yc), u_in, rows_in(t_glu), vec_in(ssm_d[l]), vec_in(b_glu[l])],
                          [row_out(ssm_w, BF16)], (n_r,))[0]

        def up_proj(name, a, w, l=l):
            return _mm(name, a, w, _spec((t_m, a.shape[1]), lambda i, j, k: (i, 0)),
                       _spec((None, a.shape[1], t_d), lambda i, j, k: (l, 0, j)),
                       _spec((t_m, t_d), lambda i, j, k: (i, j)), (seq, d), F32, (n_m, d // t_d, 1), NN)

        y_sb = up_proj(f"sb_up_{l}", o_sb, wg["w_sb_up"])
        y_ssm = up_proj(f"ssm_up_{l}", s5_out, wg["w_ssm_up"])
        gates = (proj,) + col_spec(2 * d, gates_cb)
        merged = _rowwise(f"merge_{l}", _merge_fn, [rows_in(y_sb), rows_in(y_ssm), gates],
                          [row_out(d, BF16)], (n_r,))[0]
        y_mix = _mm(f"out_proj_{l}", merged, wg["w_out"],
                    _spec((t_m, d), lambda i, j, k: (i, 0)), _spec((None, d, t_d), lambda i, j, k, l=l: (l, 0, j)),
                    _spec((t_m, t_d), lambda i, j, k: (i, j)), (seq, d), F32, (n_m, d // t_d, 1), NN)
        vecs_a = [mods[l][2], ln["ln1_g"][l], ln["ln1_b"][l], mods[l][4], mods[l][3]]
        x_mid, h2 = _rowwise(f"resid_mix_{l}", resid_ln_mod, [rows_in(x_cur), rows_in(y_mix)] + [vec_in(v) for v in vecs_a],
                             [row_out(d, F32), row_out(d, BF16)], (n_r,))
        a_ffn, f_act = _ffn_in_swiglu(f"ffn_in_{l}", h2, wg_ffn_in[l], t_r)
        y_ffn = _mm(f"ffn_out_{l}", f_act, wg_ffn_out[l],
                    _spec((None, t_m, n_ffn), lambda i, j, k: (k, i, 0)),
                    _spec((None, n_ffn, t_d), lambda i, j, k: (k, 0, j)),
                    _spec((t_m, t_d), lambda i, j, k: (i, j)), (seq, d), F32, (n_m, d // t_d, n_half), NN)
        vecs_b = [mods[l][5], ln["ln2_g"][l], ln["ln2_b"][l]] + ([] if last else [mods[l + 1][1], mods[l + 1][0]])
        outs_b = [row_out(d, F32)] + ([] if last else [row_out(d, BF16)])
        res = _rowwise(f"resid_ffn_{l}", resid_ln if last else resid_ln_mod,
                       [rows_in(x_mid), rows_in(y_ffn)] + [vec_in(v) for v in vecs_b], outs_b, (n_r,))
        sv.update(proj=proj, o_sb=o_sb, o_sb32=o_sb32, yc=yc, states=states, y1=y1, t_glu=t_glu, s5_out=s5_out,
                  y_sb=y_sb, y_ssm=y_ssm, merged=merged, y_mix=y_mix, x_mid=x_mid, h2=h2, a_ffn=a_ffn, f_act=f_act,
                  y_ffn=y_ffn, vecs_a=vecs_a, vecs_b=vecs_b)
        saved.append(sv)
        x_cur = res[0]
        h = None if last else res[1]

    loss_part, d_x = _loss_head(x_cur, target, t_r)
    loss = lax.psum(loss_part[0, 0], MESH_AXES)

    d_h_next = None
    grads = {n: [None] * depth for n in WEIGHTS}
    d_mod = [[None] * 6 for _ in range(depth)]
    land = {}
    waiting = []
    row_wrt = lambda i, width, dt: (i, "row", (seq, width), dt) + row_spec(width)
    sum_wrt = lambda i, width: (i, "sum", (1, width), F32) + vec_spec(width)
    for l in reversed(range(depth)):
        sv = saved[l]
        last = l == depth - 1
        ins_b = [rows_in(sv["x_mid"]), rows_in(sv["y_ffn"])] + [vec_in(v) for v in sv["vecs_b"]]
        cts_b = [rows_in(d_x)] + ([] if last else [rows_in(d_h_next)])
        wrt_b = [row_wrt(0, d, F32), row_wrt(1, d, BF16)] + [sum_wrt(2 + j, d) for j in range(len(sv["vecs_b"]))]
        res = _rowwise_vjp(f"resid_ffn_bwd_{l}", resid_ln if last else resid_ln_mod, ins_b, cts_b, wrt_b, (n_r,))
        d_x_mid, d_y_ffn = res[0], res[1]
        d_mod[l][5], grads["ln2_g"][l], grads["ln2_b"][l] = res[2], res[3], res[4]
        if not last:
            d_mod[l + 1][1], d_mod[l + 1][0] = res[5], res[6]
        d_a = _ffn_out_dx_swiglu(f"ffn_out_dx_{l}", d_y_ffn, wg_ffn_out[l], sv["a_ffn"], t_r).reshape(N_DEV, seq, n_ffn)
        g_ffn_out = _mm(f"ffn_out_dw_{l}", sv["f_act"], d_y_ffn,
                        _spec((None, t_m, n_ffn), lambda i, j, k: (i, k, 0)), _spec((t_m, t_d), lambda i, j, k: (k, j)),
                        _spec((None, n_ffn, t_d), lambda i, j, k: (i, 0, j)), (n_half, n_ffn, d), GRAD_WIRE,
                        (n_half, d // t_d, n_m), TN)
        d_h2 = _mm(f"ffn_in_dx_{l}", d_a, wg_ffn_in[l],
                   _spec((None, t_m, n_ffn), lambda i, j, k: (k, i, 0)),
                   _spec((None, t_d, n_ffn), lambda i, j, k: (k, j, 0)),
                   _spec((t_m, t_d), lambda i, j, k: (i, j)), (seq, d), F32, (n_m, d // t_d, N_DEV), NT)
        g_ffn_in = _mm(f"ffn_in_dw_{l}", sv["h2"], d_a,
                       _spec((t_m, t_d), lambda i, j, k: (k, j)), _spec((None, t_m, n_ffn), lambda i, j, k: (i, k, 0)),
                       _spec((None, t_d, n_ffn), lambda i, j, k: (i, j, 0)), (N_DEV, d, n_ffn), GRAD_WIRE,
                       (N_DEV, d // t_d, n_m), TN)
        ins_a = [rows_in(sv["x_in"]), rows_in(sv["y_mix"])] + [vec_in(v) for v in sv["vecs_a"]]
        wrt_a = [row_wrt(0, d, F32), row_wrt(1, d, BF16)] + [sum_wrt(2 + j, d) for j in range(5)]
        res = _rowwise_vjp(f"resid_mix_bwd_{l}", resid_ln_mod, ins_a, [rows_in(d_x_mid), rows_in(d_h2)], wrt_a, (n_r,))
        d_x_in, d_y_mix = res[0], res[1]
        d_mod[l][2], grads["ln1_g"][l], grads["ln1_b"][l], d_mod[l][4], d_mod[l][3] = res[2:7]
        d_merged = _mm(f"out_proj_dx_{l}", d_y_mix, wg["w_out"],
                       _spec((t_m, d), lambda i, j, k: (i, 0)), _spec((None, t_d, d), lambda i, j, k, l=l: (l, j, 0)),
                       _spec((t_m, t_d), lambda i, j, k: (i, j)), (seq, d), F32, (n_m, d // t_d, 1), NT)
        g_out = _mm(f"out_proj_dw_{l}", sv["merged"], d_y_mix,
                    _spec((t_m, t_d), lambda i, j, k: (k, i)), _spec((t_m, t_d), lambda i, j, k: (k, j)),
                    _spec((t_d, t_d), lambda i, j, k: (i, j)), (d, d), GRAD_WIRE, (d // t_d, d // t_d, n_m), TN)
        gates = (sv["proj"],) + col_spec(2 * d, gates_cb)
        d_y_sb, d_y_ssm, d_gates = _rowwise_vjp(
            f"merge_bwd_{l}", _merge_fn, [rows_in(sv["y_sb"]), rows_in(sv["y_ssm"]), gates], [rows_in(d_merged)],
            [row_wrt(0, d, BF16), row_wrt(1, d, BF16), row_wrt(2, 2 * d, BF16)], (n_r,))

        def up_bwd(name, act, d_y, w, dx_dtype, l=l):
            k_w = act.shape[1]
            dx = _mm(name + "_dx", d_y, w, _spec((t_m, d), lambda i, j, k: (i, 0)),
                     _spec((None, k_w, d), lambda i, j, k: (l, 0, 0)),
                     _spec((t_m, k_w), lambda i, j, k: (i, 0)), (seq, k_w), dx_dtype, (n_m, 1, 1), NT)
            dw = _mm(name + "_dw", act, d_y, _spec((t_m, k_w), lambda i, j, k: (k, 0)),
                     _spec((t_m, t_d), lambda i, j, k: (k, j)),
                     _spec((k_w, t_d), lambda i, j, k: (0, j)), (k_w, d), GRAD_WIRE, (1, d // t_d, n_m), TN)
            return dx, jnp.swapaxes(dw.reshape(k_w, N_DEV, n_up), 0, 1)

        d_o_sb, g_sb_up = up_bwd(f"sb_up_{l}", sv["o_sb"], d_y_sb, wg["w_sb_up"], BF16)
        d_s5_out, g_ssm_up = up_bwd(f"ssm_up_{l}", sv["s5_out"], d_y_ssm, wg["w_ssm_up"], F32)
        waiting += [("w_ffn_in", g_ffn_in), ("w_ffn_out", g_ffn_out.reshape(N_DEV, -1, d)),
                    ("w_out", g_out.reshape(N_DEV, -1, d)), ("w_sb_up", g_sb_up), ("w_ssm_up", g_ssm_up)]
        levels = [l + 1] * (len(waiting) - 5) + [l] * 5
        (d_q, d_k, d_v), got = _sb_attention_bwd(
            sv["proj"], sv["o_sb32"], d_o_sb, sb_w,
            beside=_Exchange(layered=[(g, lv, depth, land.get(n)) for (n, g), lv in zip(waiting, levels)]))
        land.update({n: buf for (n, _), buf in zip(waiting, got)})
        u_in = (sv["proj"],) + col_spec(ssm_w, 3 * sb_w // ssm_w)
        ins_s5 = [rows_in(sv["yc"]), u_in, rows_in(sv["t_glu"]), vec_in(ssm_d[l]), vec_in(b_glu[l])]
        d_t = _rowwise_vjp(f"s5_glu_bwd_{l}", _s5_glu_fn, ins_s5, [rows_in(d_s5_out)],
                           [row_wrt(2, ssm_w, BF16)], (n_r,))[0]
        d_y1 = _mm(f"s5_glu_mm_dx_{l}", d_t, wg["w_glu"],
                   _spec((t_m, ssm_w), lambda i, j, k: (i, 0)), _spec((None, ssm_w, ssm_w), lambda i, j, k, l=l: (l, 0, 0)),
                   _spec((t_m, ssm_w), lambda i, j, k: (i, 0)), (seq, ssm_w), F32, (n_m, 1, 1), NT)
        g_glu = _mm(f"s5_glu_mm_dw_{l}", sv["y1"], d_t,
                    _spec((t_m, ssm_w), lambda i, j, k: (k, 0)), _spec((t_m, ssm_w), lambda i, j, k: (k, 0)),
                    _spec((ssm_w, ssm_w), lambda i, j, k: (0, 0)), (ssm_w, ssm_w), GRAD_WIRE, (1, 1, n_m), TN)
        d_yc, d_u_skip, grads["ssm_d"][l], grads["b_glu"][l] = _rowwise_vjp(
            f"s5_post_bwd_{l}", _s5_post_fn, ins_s5, [rows_in(d_y1), rows_in(d_s5_out)],
            [row_wrt(0, ssm_w, F32), row_wrt(1, ssm_w, F32), sum_wrt(3, ssm_w), sum_wrt(4, ssm_w)], (n_r,))
        bs16, cs16, lam = s5_b16[l]
        d_u, d_bs, d_cs, d_lam = _s5_scan_bwd(sv["proj"], u_col, sv["states"], d_yc, d_u_skip, bs16, cs16, lam,
                                              s5_pw[l][1], t_scan)
        raw = [p[n][l] for n in ("ssm_a_re", "ssm_a_im", "ssm_log_dt", "ssm_b_re", "ssm_b_im", "ssm_c_re", "ssm_c_im")]
        _, pull = jax.vjp(_s5_discretize, *raw)
        (grads["ssm_a_re"][l], grads["ssm_a_im"][l], grads["ssm_log_dt"][l], grads["ssm_b_re"][l],
         grads["ssm_b_im"][l], grads["ssm_c_re"][l], grads["ssm_c_im"][l]) = pull((d_bs, d_cs, d_lam))
        d_proj = jnp.concatenate([d_q, d_k.astype(BF16), d_v.astype(BF16), d_u.astype(BF16), d_gates], axis=1)
        t_n = _tile(n_in)
        g_in = _mm(f"proj_dw_{l}", sv["h"], d_proj,
                   _spec((t_m, t_d), lambda i, j, k: (k, j)), _spec((t_m, n_in), lambda i, j, k: (k, i)),
                   _spec((None, t_d, n_in), lambda i, j, k: (i, j, 0)), (N_DEV, d, n_in), GRAD_WIRE,
                   (N_DEV, d // t_d, n_m), TN)
        waiting = [("w_in", g_in), ("w_glu", g_glu.reshape(N_DEV, -1, ssm_w))]
        closing = _Exchange(layered=[(g, 0, depth, land.get(n)) for n, g in waiting]) if l == 0 else None
        d_h = _mm(f"proj_dx_{l}", d_proj, wg_in[l],
                  _spec((t_m, n_in), lambda i, j, k: (i, k)), _spec((None, t_d, n_in), lambda i, j, k: (k, j, 0)),
                  _spec((t_m, t_d), lambda i, j, k: (i, j)), (seq, d), F32, (n_m, d // t_d, N_DEV), NT, beside=closing)
        if l == 0:
            d_h, got = d_h
            land.update({n: buf for (n, _), buf in zip(waiting, got)})
        d_x, d_h_next = d_x_in, d_h
    res = _rowwise_vjp("modulate_in_bwd", lambda v, sc, sh: (v, _modulate(v, sc, sh)),
                       [rows_in(x0), vec_in(mods[0][1]), vec_in(mods[0][0])], [rows_in(d_x), rows_in(d_h_next)],
                       [row_wrt(0, d, F32), sum_wrt(1, d), sum_wrt(2, d)], (n_r,))
    grad_x, d_mod[0][1], d_mod[0][0] = res

    d_mod_rows = jnp.concatenate([jnp.concatenate(d_mod[l], axis=1) for l in range(depth)], axis=0)
    grads["b_ada"] = [d_mod_rows[l] for l in range(depth)]
    small_local = [jnp.stack([g.reshape(p[n].shape[1:]) for g in grads[n]]) for n in SMALL_PARAMS]
    d_mod_send = jnp.swapaxes(d_mod_rows.reshape(depth, N_DEV, n_ada), 0, 1)
    small_sum, (d_mod_cols,) = _reduce_packed("exchange_last", _pack(small_local), [d_mod_send])
    d_mod_pad = jnp.pad(jnp.swapaxes(d_mod_cols, 0, 1), ((0, 0), (0, rows_c - N_DEV), (0, 0)))
    g_ada = [
        _mm(f"mod_dw_{l}", c_act, d_mod_pad,
            _spec((rows_c, d), lambda i, j, k: (0, 0)), _spec((None, rows_c, n_ada), lambda i, j, k, l=l: (l, 0, 0)),
            _spec((d, n_ada), lambda i, j, k: (0, 0)), (d, n_ada), F32, (1, 1, 1), TN)
        for l in range(depth)]

    out = {}

    def update(name, partials):
        shape = p[name].shape
        two_d = lambda a: a.reshape(-1, shape[-1])
        res = _adamw("adamw_" + name, two_d(p[name]), two_d(p["m_" + name]), two_d(p["v_" + name]),
                     partials.reshape(partials.shape[0], -1, shape[-1]))
        out[name] = [r.reshape(shape) for r in res]

    update("w_ada", jnp.stack(g_ada)[None])
    for n in ("w_in", "w_sb_up", "w_ssm_up", "w_ffn_in", "w_glu", "w_out", "w_ffn_out"):
        update(n, land[n])
    small_w = [p[n] for n in SMALL_PARAMS]
    res = _adamw("adamw_small", _pack(small_w), _pack([p["m_" + n] for n in SMALL_PARAMS]),
                 _pack([p["v_" + n] for n in SMALL_PARAMS]), small_sum[None])
    for kind, packed in enumerate(res):
        for n, a in zip(SMALL_PARAMS, _unpack(packed, small_w)):
            out.setdefault(n, [None] * 4)[kind] = a

    return ((loss, grad_x[None]) + tuple(out[n][0] for n in WEIGHTS) + tuple(out[n][1] for n in WEIGHTS)
            + tuple(out[n][2] for n in WEIGHTS) + tuple(out[n][3] for n in WEIGHTS))
```

```python
import jax
import jax.numpy as jnp
from jax import lax
from jax.experimental import pallas as pl
from jax.experimental.pallas import tpu as pltpu

F32 = jnp.float32
BF16 = jnp.bfloat16
GRAD_WIRE = BF16
FFN_ACT = BF16

N_DEV = 8
LANES = 128
SUBLANES = 8
VMEM_BYTES = 64 * 1024 * 1024
HEAD_DIM = 64
SB_BLOCK = 256
SLAB_GROUPS = 8
LN_EPS = 1e-5
ADAM_LR, ADAM_B1, ADAM_B2, ADAM_EPS, ADAM_WD, ADAM_STEP = 0.001, 0.9, 0.999, 1e-08, 0.01, 10
SB_UNDERFLOW = -120.0

PACK_ROWS = 256
MESH_AXES = ("x", "y", "c")


def _vmem_limit(block_bytes):
    return int(min(max(3 * block_bytes + (8 << 20), 24 << 20), VMEM_BYTES - (8 << 20)))


def _nbytes(shape, dtype):
    n = 1
    for d in shape:
        if d is not None:
            n *= d
    return n * jnp.dtype(dtype).itemsize


def _spec(shape, fn):
    return pl.BlockSpec(shape, fn)


class _Exchange:
    def __init__(self, scatter=(), gather=(), layered=()):
        self.arrs = list(scatter) + [a for a, _, _, _ in layered] + list(gather)
        self.n = len(self.arrs)
        self.n_sc = len(scatter) + len(layered)
        self.layer = [None] * len(scatter) + [l for _, l, _, _ in layered] + [None] * len(gather)
        self.shapes = ([a.shape for a in scatter] + [(N_DEV, dp) + a.shape[1:] for a, _, dp, _ in layered]
                       + [(N_DEV,) + a.shape for a in gather])
        self.held = [(len(scatter) + i, b) for i, (_, _, _, b) in enumerate(layered) if b is not None]
        self.operands = self.arrs + [b for _, b in self.held]
        hbm = pl.BlockSpec(memory_space=pltpu.HBM)
        self.in_specs = [hbm] * len(self.operands)
        self.out_specs = [hbm] * self.n
        self.out_shape = [jax.ShapeDtypeStruct(s, a.dtype) for s, a in zip(self.shapes, self.arrs)]
        self.scratch = [pltpu.SemaphoreType.DMA((self.n, N_DEV - 1)), pltpu.SemaphoreType.DMA((self.n, N_DEV - 1)),
                        pltpu.SemaphoreType.DMA((self.n,))]

    def aliases(self, first_in, first_out):
        return {first_in + self.n + i: first_out + a for i, (a, _) in enumerate(self.held)}

    def copies(self, ins, outs, sems):
        send_sems, recv_sems, own_sems = sems
        x, y, c = lax.axis_index("x"), lax.axis_index("y"), lax.axis_index("c")
        me = 4 * x + 2 * y + c
        landing = [outs[a].at[me] if self.layer[a] is None else outs[a].at[me, self.layer[a]] for a in range(self.n)]
        out = [pltpu.make_async_copy(ins[a].at[me] if a < self.n_sc else ins[a], landing[a], own_sems.at[a])
               for a in range(self.n)]
        for k in range(1, N_DEV):
            px = 1 - x if k & 4 else x
            py = 1 - y if k & 2 else y
            pc = 1 - c if k & 1 else c
            peer = 4 * px + 2 * py + pc
            for a in range(self.n):
                out.append(pltpu.make_async_remote_copy(
                    src_ref=ins[a].at[peer] if a < self.n_sc else ins[a], dst_ref=landing[a],
                    send_sem=send_sems.at[a, k - 1], recv_sem=recv_sems.at[a, k - 1],
                    device_id=(px, py, pc), device_id_type=pl.DeviceIdType.MESH))
        return out


def _exchange(name, scatter, gather, layered=()):
    ex = _Exchange(scatter, gather, layered)

    def body(*refs):
        copies = ex.copies(refs[:ex.n], refs[len(ex.operands):len(ex.operands) + ex.n], refs[-3:])
        for cp in copies:
            cp.start()
        for cp in copies:
            cp.wait()

    return pl.pallas_call(body, name=name, in_specs=ex.in_specs, out_specs=ex.out_specs, out_shape=ex.out_shape,
                          input_output_aliases=ex.aliases(0, 0), scratch_shapes=ex.scratch)(*ex.operands)


def _reduce_packed(name, packed, scatter):
    rows = packed.shape[0]
    blk = rows // N_DEV
    ex = _Exchange(scatter=[packed.reshape(N_DEV, blk, LANES)] + list(scatter))
    n_in = len(ex.operands)

    def body(*refs):
        ins, outs = refs[:ex.n], refs[n_in:n_in + ex.n]
        total_ref = refs[n_in + ex.n]
        sems, (send2, recv2, own2, load_sem) = refs[n_in + ex.n + 1:n_in + ex.n + 4], refs[n_in + ex.n + 4:-2]
        land_v, sum_v = refs[-2:]
        copies = ex.copies(ins, outs, sems)
        for cp in copies:
            cp.start()
        for cp in copies:
            cp.wait()
        load = pltpu.make_async_copy(outs[0], land_v, load_sem)
        load.start()
        load.wait()
        acc = land_v[0]
        for i in range(1, N_DEV):
            acc = acc + land_v[i]
        sum_v[...] = acc
        x, y, c = lax.axis_index("x"), lax.axis_index("y"), lax.axis_index("c")
        me = 4 * x + 2 * y + c
        back = [pltpu.make_async_copy(sum_v, total_ref.at[me], own2)]
        for k in range(1, N_DEV):
            peer = (1 - x if k & 4 else x, 1 - y if k & 2 else y, 1 - c if k & 1 else c)
            back.append(pltpu.make_async_remote_copy(
                src_ref=sum_v, dst_ref=total_ref.at[me], send_sem=send2.at[k - 1], recv_sem=recv2.at[k - 1],
                device_id=peer, device_id_type=pl.DeviceIdType.MESH))
        for cp in back:
            cp.start()
        for cp in back:
            cp.wait()

    hbm = pl.BlockSpec(memory_space=pltpu.HBM)
    res = pl.pallas_call(
        body, name=name, in_specs=ex.in_specs, out_specs=ex.out_specs + [hbm],
        out_shape=ex.out_shape + [jax.ShapeDtypeStruct((N_DEV, blk, LANES), F32)],
        scratch_shapes=ex.scratch + [pltpu.SemaphoreType.DMA((N_DEV - 1,)), pltpu.SemaphoreType.DMA((N_DEV - 1,)),
                                     pltpu.SemaphoreType.DMA, pltpu.SemaphoreType.DMA,
                                     pltpu.VMEM((N_DEV, blk, LANES), F32), pltpu.VMEM((blk, LANES), F32)],
    )(*ex.operands)
    return res[-1].reshape(rows, LANES), res[1:-1]


def _call_beside(ex, body, name, grid, in_specs, out_specs, out_shape, scratch_shapes, vmem_bytes, operands,
                 semantics, in_hbm=True):
    if in_hbm:
        operands = [_in_hbm(a) for a in operands]
    if ex is None:
        res = pl.pallas_call(
            body, name=name, grid=grid, in_specs=in_specs, out_specs=out_specs, out_shape=out_shape,
            scratch_shapes=scratch_shapes,
            compiler_params=pltpu.CompilerParams(dimension_semantics=semantics, vmem_limit_bytes=vmem_bytes),
        )(*operands)
        return res, None
    n_in, n_out, n_scr = len(in_specs), len(out_specs), len(scratch_shapes)
    n_xin = len(ex.operands)

    def fused(*refs):
        mine = refs[:n_in] + refs[n_in + n_xin:n_in + n_xin + n_out]
        mine += refs[n_in + n_xin + n_out + ex.n:n_in + n_xin + n_out + ex.n + n_scr]
        first = pl.program_id(0) == 0
        last = pl.program_id(0) == grid[0] - 1
        for dim in range(1, len(grid)):
            first = jnp.logical_and(first, pl.program_id(dim) == 0)
            last = jnp.logical_and(last, pl.program_id(dim) == grid[dim] - 1)
        x_ins = refs[n_in:n_in + ex.n]
        x_outs = refs[n_in + n_xin + n_out:n_in + n_xin + n_out + ex.n]

        @pl.when(first)
        def _():
            for cp in ex.copies(x_ins, x_outs, refs[-3:]):
                cp.start()

        body(*mine)

        @pl.when(last)
        def _():
            for cp in ex.copies(x_ins, x_outs, refs[-3:]):
                cp.wait()

    res = pl.pallas_call(
        fused, name=name, grid=grid, in_specs=list(in_specs) + ex.in_specs, out_specs=list(out_specs) + ex.out_specs,
        out_shape=list(out_shape) + ex.out_shape, input_output_aliases=ex.aliases(n_in, n_out),
        scratch_shapes=list(scratch_shapes) + ex.scratch,
        compiler_params=pltpu.CompilerParams(dimension_semantics=("arbitrary",) * len(grid),
                                             vmem_limit_bytes=vmem_bytes),
    )(*operands, *ex.operands)
    return res[:n_out], res[n_out:]


NN = (((1,), (0,)), ((), ()))
NT = (((1,), (1,)), ((), ()))
TN = (((0,), (0,)), ((), ()))


def _in_hbm(a):
    return pltpu.with_memory_space_constraint(a, pltpu.HBM)


def _mm(name, a, b, a_spec, b_spec, o_spec, o_shape, o_dtype, grid, dims, beside=None, reread=(False, True)):
    nk = grid[2]
    a, b = (x if again else _in_hbm(x) for x, again in zip((a, b), reread))
    acc_shape = tuple(d for d in o_spec.block_shape if d is not None)

    def product(a_ref, b_ref):
        return lax.dot_general(a_ref[...].astype(BF16), b_ref[...].astype(BF16), dims, preferred_element_type=F32)

    def body_once(a_ref, b_ref, o_ref):
        o_ref[...] = product(a_ref, b_ref).astype(o_ref.dtype)

    def body(a_ref, b_ref, o_ref, acc_ref):
        k = pl.program_id(2)

        @pl.when(k == 0)
        def _():
            acc_ref[...] = product(a_ref, b_ref)

        @pl.when(k > 0)
        def _():
            acc_ref[...] += product(a_ref, b_ref)

        @pl.when(k == nk - 1)
        def _():
            o_ref[...] = acc_ref[...].astype(o_ref.dtype)

    blk = (_nbytes(a_spec.block_shape, a.dtype) + _nbytes(b_spec.block_shape, b.dtype)
           + _nbytes(acc_shape, o_dtype) + _nbytes(acc_shape, F32))
    res, got = _call_beside(
        beside, body_once if nk == 1 else body, name, grid, [a_spec, b_spec], [o_spec],
        [jax.ShapeDtypeStruct(o_shape, o_dtype)], [] if nk == 1 else [pltpu.VMEM(acc_shape, F32)],
        _vmem_limit(blk), (a, b), ("parallel", "parallel", "arbitrary"), in_hbm=False)
    return res[0] if beside is None else (res[0], got)


def _swiglu_fn(gate_up):
    gate, up = gate_up[0], gate_up[1]
    return gate * jax.nn.sigmoid(gate) * up


def _ffn_in_swiglu(name, h, w, t_m):
    seq, d = h.shape
    n_half, n = w.shape[0] // 2, w.shape[2]

    def body(h_ref, wg_ref, wu_ref, a_ref, f_ref):
        hb = h_ref[...]
        a_ref[0] = lax.dot_general(hb, wg_ref[...], NN, preferred_element_type=F32).astype(a_ref.dtype)
        a_ref[1] = lax.dot_general(hb, wu_ref[...], NN, preferred_element_type=F32).astype(a_ref.dtype)
        f_ref[...] = _swiglu_fn(a_ref[...].astype(F32)).astype(f_ref.dtype)

    blk = 2 * t_m * d + 4 * d * n + 6 * t_m * n + 12 * t_m * n
    return pl.pallas_call(
        body, name=name, grid=(seq // t_m, n_half),
        in_specs=[_spec((t_m, d), lambda i, j: (i, 0)), _spec((None, d, n), lambda i, j: (j, 0, 0)),
                  _spec((None, d, n), lambda i, j: (j + n_half, 0, 0))],
        out_specs=[_spec((2, None, t_m, n), lambda i, j: (0, j, i, 0)), _spec((None, t_m, n), lambda i, j: (j, i, 0))],
        out_shape=[jax.ShapeDtypeStruct((2, n_half, seq, n), FFN_ACT), jax.ShapeDtypeStruct((n_half, seq, n), BF16)],
        compiler_params=pltpu.CompilerParams(dimension_semantics=("parallel", "parallel"),
                                             vmem_limit_bytes=_vmem_limit(blk)),
    )(h, w, w)


def _ffn_out_dx_swiglu(name, d_y, w, a, t_m):
    seq, d = d_y.shape
    n_half, n = w.shape[0], w.shape[1]

    def body(dy_ref, w_ref, a_ref, da_ref):
        d_f = lax.dot_general(dy_ref[...], w_ref[...], NT, preferred_element_type=F32)
        _, pull = jax.vjp(_swiglu_fn, a_ref[...].astype(F32))
        da_ref[...] = pull(d_f)[0].astype(da_ref.dtype)

    blk = 2 * t_m * d + 2 * d * n + 8 * t_m * n + 24 * t_m * n
    return pl.pallas_call(
        body, name=name, grid=(seq // t_m, n_half),
        in_specs=[_spec((t_m, d), lambda i, j: (i, 0)), _spec((None, n, d), lambda i, j: (j, 0, 0)),
                  _spec((2, None, t_m, n), lambda i, j: (0, j, i, 0))],
        out_specs=_spec((2, None, t_m, n), lambda i, j: (0, j, i, 0)),
        out_shape=jax.ShapeDtypeStruct((2, n_half, seq, n), BF16),
        compiler_params=pltpu.CompilerParams(dimension_semantics=("parallel", "parallel"),
                                             vmem_limit_bytes=_vmem_limit(blk)),
    )(d_y, w, a)


def _merge_fn(y_sb, y_ssm, gates):
    half = gates.shape[-1] // 2
    return jax.nn.sigmoid(gates[:, :half]) * y_sb + jax.nn.sigmoid(gates[:, half:]) * y_ssm


def _up_merge(name, o_sb, s5_out, proj, gates_cb, w_sb, w_ssm, layer, t_rows):
    seq = o_sb.shape[0]
    d = w_sb.shape[2]

    def body(o_ref, s_ref, g_ref, w1_ref, w2_ref, m_ref, y1_ref, y2_ref):
        y_sb = lax.dot_general(o_ref[...], w1_ref[...], NN, preferred_element_type=F32)
        y_ssm = lax.dot_general(s_ref[...], w2_ref[...], NN, preferred_element_type=F32)
        m_ref[...] = _merge_fn(y_sb, y_ssm, g_ref[...]).astype(m_ref.dtype)
        y1_ref[...] = y_sb.astype(y1_ref.dtype)
        y2_ref[...] = y_ssm.astype(y2_ref.dtype)

    row = lambda width: _spec((t_rows, width), lambda i: (i, 0))
    whole = lambda w: _spec((None,) + w.shape[1:], lambda i: (layer, 0, 0))
    blk = t_rows * (2 * o_sb.shape[1] + 2 * s5_out.shape[1] + 8 * d + 6 * d + 24 * d) + 4 * d * (o_sb.shape[1] + s5_out.shape[1])
    return pl.pallas_call(
        body, name=name, grid=(seq // t_rows,),
        in_specs=[row(o_sb.shape[1]), row(s5_out.shape[1]), _spec((t_rows, 2 * d), lambda i: (i, gates_cb)),
                  whole(w_sb), whole(w_ssm)],
        out_specs=[row(d)] * 3, out_shape=[jax.ShapeDtypeStruct((seq, d), BF16)] * 3,
        compiler_params=pltpu.CompilerParams(dimension_semantics=("parallel",), vmem_limit_bytes=_vmem_limit(blk)),
    )(_in_hbm(o_sb), _in_hbm(s5_out), _in_hbm(proj), w_sb, w_ssm)


def _tile(n, pref=1024):
    t = pref
    while t >= LANES:
        if n % t == 0:
            return t
        t -= LANES
    return n


def _rowwise(name, fn, ins, outs, grid):
    n_in = len(ins)

    def body(*refs):
        vals = fn(*[r[...].astype(F32) for r in refs[:n_in]])
        if not isinstance(vals, (tuple, list)):
            vals = (vals,)
        for r, v in zip(refs[n_in:], vals):
            r[...] = v.astype(r.dtype)

    blk = sum(_nbytes(bs, a.dtype) for a, bs, _ in ins) + sum(_nbytes(bs, d) + _nbytes(bs, F32) for _, d, bs, _ in outs)
    return pl.pallas_call(
        body, name=name, grid=grid,
        in_specs=[_spec(bs, im) for _, bs, im in ins],
        out_specs=[_spec(bs, im) for _, _, bs, im in outs],
        out_shape=[jax.ShapeDtypeStruct(s, d) for s, d, _, _ in outs],
        compiler_params=pltpu.CompilerParams(dimension_semantics=("parallel",) * len(grid),
                                             vmem_limit_bytes=_vmem_limit(2 * blk)),
    )(*[_in_hbm(a) for a, _, _ in ins])


def _rowwise_vjp(name, fn, ins, cts, wrt, grid):
    n_in, n_ct = len(ins), len(cts)
    idx = [w[0] for w in wrt]

    def body(*refs):
        prim = [r[...].astype(F32) for r in refs[:n_in]]
        ct = tuple(r[...].astype(F32) for r in refs[n_in:n_in + n_ct])
        o_refs = refs[n_in + n_ct:]

        def g(*sel):
            full = list(prim)
            for i, s in zip(idx, sel):
                full[i] = s
            out = fn(*full)
            return tuple(out) if isinstance(out, (tuple, list)) else (out,)

        _, pull = jax.vjp(g, *[prim[i] for i in idx])
        grads = pull(ct)
        first = pl.program_id(0) == 0
        for d in range(1, len(grid)):
            first = jnp.logical_and(first, pl.program_id(d) == 0)
        for w, o_ref, gr in zip(wrt, o_refs, grads):
            if w[1] == "row":
                o_ref[...] = gr.astype(o_ref.dtype)
            else:
                @pl.when(first)
                def _(o_ref=o_ref):
                    o_ref[...] = jnp.zeros_like(o_ref)

                o_ref[...] += gr.astype(o_ref.dtype)

    blk = (sum(_nbytes(bs, a.dtype) + _nbytes(bs, F32) for a, bs, _ in list(ins) + list(cts))
           + sum(_nbytes(w[4], w[3]) + _nbytes(w[4], F32) for w in wrt))
    return pl.pallas_call(
        body, name=name, grid=grid,
        in_specs=[_spec(bs, im) for _, bs, im in list(ins) + list(cts)],
        out_specs=[_spec(w[4], w[5]) for w in wrt],
        out_shape=[jax.ShapeDtypeStruct(w[2], w[3]) for w in wrt],
        compiler_params=pltpu.CompilerParams(dimension_semantics=("arbitrary",) * len(grid),
                                             vmem_limit_bytes=_vmem_limit(2 * blk)),
    )(*[_in_hbm(a) for a, _, _ in list(ins) + list(cts)])


def _normalize(x):
    mu = jnp.mean(x, axis=-1, keepdims=True)
    xc = x - mu
    var = jnp.mean(xc * xc, axis=-1, keepdims=True)
    return xc * lax.rsqrt(var + LN_EPS)


def _modulate(x, sc, sh):
    return _normalize(x) * (1.0 + sc) + sh


def _make_resid_fns(alpha):
    def resid_ln(x, y, gate, g, b):
        return _normalize(alpha * x + (1.0 + gate) * y) * g + b

    def resid_ln_mod(x, y, gate, g, b, sc, sh):
        xn = resid_ln(x, y, gate, g, b)
        return xn, _modulate(xn, sc, sh)

    return resid_ln, resid_ln_mod


def _s5_act_fn(yc, u, d_skip):
    return jax.nn.gelu(yc + d_skip * u)


def _s5_glu_fn(yc, u, t, d_skip, b_glu):
    return _s5_act_fn(yc, u, d_skip) * jax.nn.sigmoid(t + b_glu)


def _s5_post_fn(yc, u, t, d_skip, b_glu):
    y1 = _s5_act_fn(yc, u, d_skip)
    return y1, y1 * jax.nn.sigmoid(t + b_glu)


def _sb_tri(kind):
    row = lax.broadcasted_iota(jnp.int32, (SB_BLOCK, SB_BLOCK), 0)
    col = lax.broadcasted_iota(jnp.int32, (SB_BLOCK, SB_BLOCK), 1)
    if kind == "after":
        return (row > col).astype(BF16)
    if kind == "from":
        return (row >= col).astype(BF16)
    return col < row


def _split_dot(x, m):
    hi = x.astype(BF16)
    lo = (x - hi.astype(F32)).astype(BF16)
    return (lax.dot_general(hi, m, NN, preferred_element_type=F32)
            + lax.dot_general(lo, m, NN, preferred_element_type=F32))


def _sb_scores(qh, k2):
    z = lax.dot_general(qh, k2, NT, preferred_element_type=F32)
    log_beta = jnp.minimum(z, 0.0) - jnp.log(1.0 + jnp.exp(-jnp.abs(z)))
    return log_beta, log_beta - z


def _sb_attention_fwd(proj, sb_width, beside=None):
    seq = proj.shape[0]
    n_pair, n_q = sb_width // LANES, seq // SB_BLOCK
    scale = 1.0 / (HEAD_DIM ** 0.5)

    def body(q_ref, k_ref, v_ref, o_ref, o32_ref):
        qi = pl.program_id(1)
        q2 = q_ref[...]
        lane = lax.broadcasted_iota(jnp.int32, (SB_BLOCK, LANES), 1)
        m_after, causal = _sb_tri("after"), _sb_tri("mask")
        heads = [lane < HEAD_DIM, lane >= HEAD_DIM]
        qh = [(jnp.where(m, q2, 0.0) * scale).astype(BF16) for m in heads]

        def scores(kb, diag):
            ks = pl.multiple_of(kb * SB_BLOCK, SB_BLOCK)
            k2 = k_ref[pl.ds(ks, SB_BLOCK), :].astype(BF16)
            out = []
            for h in range(2):
                log_beta, log_1m = _sb_scores(qh[h], k2)
                if diag:
                    log_1m = jnp.where(causal, log_1m, 0.0)
                out += [log_beta + _split_dot(log_1m, m_after), jnp.sum(log_1m, axis=1, keepdims=True)]
            return tuple(out)

        def weigh(kb, sc, carry, acc, diag):
            ks = pl.multiple_of(kb * SB_BLOCK, SB_BLOCK)
            v2 = v_ref[pl.ds(ks, SB_BLOCK), :].astype(BF16)
            out = []
            for h in range(2):
                w = jnp.exp(sc[2 * h] + carry[h])
                if diag:
                    w = jnp.where(causal, w, 0.0)
                out.append(acc[h] + lax.dot_general(w.astype(BF16), v2, NN, preferred_element_type=F32))
            return tuple(out)

        zero = jnp.zeros((SB_BLOCK, LANES), F32)
        zcol = jnp.zeros((SB_BLOCK, 1), F32)
        sc = scores(qi, True)
        acc = weigh(qi, sc, (zcol, zcol), (zero, zero), True)
        carry = (sc[1], sc[3])

        def loop(st):
            kb, carry, acc = st
            sc = scores(kb, False)
            after = (carry[0] + sc[1], carry[1] + sc[3])
            done = jnp.maximum(jnp.max(after[0]), jnp.max(after[1])) < SB_UNDERFLOW
            acc = weigh(kb, sc, carry, acc, False)
            return jnp.where(done, -1, kb - 1), after, acc

        _, _, acc = lax.while_loop(lambda st: st[0] >= 0, loop, (qi - 1, carry, acc))
        out = jnp.where(heads[0], acc[0], acc[1])
        o_ref[...] = out.astype(o_ref.dtype)
        o32_ref[...] = out

    q_spec = _spec((SB_BLOCK, LANES), lambda h, i: (i, h))
    kv = [_spec((seq, LANES), lambda h, i, o=o: (0, o + h)) for o in (n_pair, 2 * n_pair)]
    o_spec = _spec((SB_BLOCK, LANES), lambda h, i: (i, h))
    return _call_beside(
        beside, body, "sb_attention_fwd", (n_pair, n_q), [q_spec] + kv, [o_spec, o_spec],
        [jax.ShapeDtypeStruct((seq, sb_width), BF16), jax.ShapeDtypeStruct((seq, sb_width), F32)], [],
        _vmem_limit(2 * seq * LANES * 4), (proj, proj, proj), ("parallel", "arbitrary"))


def _sb_attention_bwd(proj, o32, do, sb_width, beside=None):
    seq = proj.shape[0]
    n_pair, n_q = sb_width // LANES, seq // SB_BLOCK
    scale = 1.0 / (HEAD_DIM ** 0.5)

    def body(q_ref, k_ref, v_ref, o_ref, do_ref, dq_ref, dk_ref, dv_ref):
        qi = pl.program_id(1)

        @pl.when(qi == 0)
        def _():
            dk_ref[...] = jnp.zeros_like(dk_ref)
            dv_ref[...] = jnp.zeros_like(dv_ref)

        q2 = q_ref[...]
        do2 = do_ref[...].astype(F32)
        o2 = o_ref[...]
        lane = lax.broadcasted_iota(jnp.int32, (SB_BLOCK, LANES), 1)
        m_after, m_from, causal = _sb_tri("after"), _sb_tri("from"), _sb_tri("mask")
        heads = [lane < HEAD_DIM, lane >= HEAD_DIM]
        qh = [(jnp.where(m, q2, 0.0) * scale).astype(BF16) for m in heads]
        doh = [jnp.where(m, do2, 0.0) for m in heads]
        doh_b = [v.astype(BF16) for v in doh]
        total = [jnp.sum(v * o2, axis=1, keepdims=True) for v in doh]

        def scores(kb, diag):
            ks = pl.multiple_of(kb * SB_BLOCK, SB_BLOCK)
            k2 = k_ref[pl.ds(ks, SB_BLOCK), :].astype(BF16)
            v2 = v_ref[pl.ds(ks, SB_BLOCK), :].astype(BF16)
            out = []
            for h in range(2):
                log_beta, log_1m = _sb_scores(qh[h], k2)
                if diag:
                    log_1m = jnp.where(causal, log_1m, 0.0)
                out += [log_beta + _split_dot(log_1m, m_after), jnp.sum(log_1m, axis=1, keepdims=True),
                        lax.dot_general(doh_b[h], v2, NT, preferred_element_type=F32), log_beta]
            return tuple(out)

        def pull(kb, sc, carry, right, dq, diag):
            ks = pl.multiple_of(kb * SB_BLOCK, SB_BLOCK)
            k2 = k_ref[pl.ds(ks, SB_BLOCK), :].astype(BF16)
            dv_blk, dk_blk, right_out, dq_out = None, None, [], []
            for h in range(2):
                arg, _, d_w, log_beta = sc[4 * h:4 * h + 4]
                w = jnp.exp(arg + carry[h])
                if diag:
                    w = jnp.where(causal, w, 0.0)
                w_b = w.astype(BF16)
                d_arg = d_w * w_b.astype(F32)
                dv_h = lax.dot_general(w_b, doh_b[h], TN, preferred_element_type=F32)
                d_log_1m = total[h] - right[h] - _split_dot(d_arg, m_from)
                beta = jnp.exp(log_beta)
                dz = d_arg * (1.0 - beta) - beta * d_log_1m
                if diag:
                    dz = jnp.where(causal, dz, 0.0)
                dz_b = dz.astype(BF16)
                dk_h = lax.dot_general(dz_b, qh[h], TN, preferred_element_type=F32)
                dv_blk = dv_h if h == 0 else dv_blk + dv_h
                dk_blk = dk_h if h == 0 else dk_blk + dk_h
                dq_out.append(dq[h] + lax.dot_general(dz_b, k2, NN, preferred_element_type=F32))
                right_out.append(right[h] + jnp.sum(d_arg, axis=1, keepdims=True))
            dv_ref[pl.ds(ks, SB_BLOCK), :] += dv_blk
            dk_ref[pl.ds(ks, SB_BLOCK), :] += dk_blk
            return tuple(right_out), tuple(dq_out)

        zero = jnp.zeros((SB_BLOCK, LANES), F32)
        zcol = jnp.zeros((SB_BLOCK, 1), F32)
        sc = scores(qi, True)
        right, dq = pull(qi, sc, (zcol, zcol), (zcol, zcol), (zero, zero), True)
        carry = (sc[1], sc[5])

        def loop(st):
            kb, carry, right, dq = st
            sc = scores(kb, False)
            after = (carry[0] + sc[1], carry[1] + sc[5])
            done = jnp.maximum(jnp.max(after[0]), jnp.max(after[1])) < SB_UNDERFLOW
            right, dq = pull(kb, sc, carry, right, dq, False)
            return jnp.where(done, -1, kb - 1), after, right, dq

        _, _, _, dq = lax.while_loop(lambda st: st[0] >= 0, loop, (qi - 1, carry, right, dq))
        dq_ref[...] = (jnp.where(heads[0], dq[0], dq[1]) * scale).astype(dq_ref.dtype)

    q_spec = _spec((SB_BLOCK, LANES), lambda h, i: (i, h))
    kv = [_spec((seq, LANES), lambda h, i, o=o: (0, o + h)) for o in (n_pair, 2 * n_pair)]
    full = _spec((seq, LANES), lambda h, i: (0, h))
    return _call_beside(
        beside, body, "sb_attention_bwd", (n_pair, n_q), [q_spec] + kv + [q_spec, q_spec], [q_spec, full, full],
        [jax.ShapeDtypeStruct((seq, sb_width), BF16), jax.ShapeDtypeStruct((seq, sb_width), F32),
         jax.ShapeDtypeStruct((seq, sb_width), F32)], [],
        _vmem_limit(4 * seq * LANES * 4), (proj, proj, proj, o32, do), ("parallel", "arbitrary"))


def _s5_discretize(a_re, a_im, log_dt, b_re, b_im, c_re, c_im):
    n_g, n_p = a_re.shape
    c_g = b_re.shape[-1]
    ns = n_g // SLAB_GROUPS
    dt = jnp.exp(log_dt)[:, None]
    xr, xi = a_re * dt, a_im * dt
    mag = jnp.exp(xr)
    lr, li = mag * jnp.cos(xi), mag * jnp.sin(xi)
    den = a_re * a_re + a_im * a_im
    fr = ((lr - 1.0) * a_re + li * a_im) / den
    fi = (li * a_re - (lr - 1.0) * a_im) / den
    bb_re = fr[..., None] * b_re - fi[..., None] * b_im
    bb_im = fr[..., None] * b_im + fi[..., None] * b_re
    eye = jnp.eye(SLAB_GROUPS, dtype=F32)

    def diag_b(m):
        m = jnp.transpose(m.reshape(ns, SLAB_GROUPS, n_p, c_g), (0, 1, 3, 2))
        m = m[:, :, :, None, :] * eye[None, :, None, :, None]
        return m.reshape(ns, SLAB_GROUPS * c_g, SLAB_GROUPS * n_p)

    def diag_c(m):
        m = jnp.transpose(m.reshape(ns, SLAB_GROUPS, c_g, n_p), (0, 1, 3, 2))
        m = m[:, :, :, None, :] * eye[None, :, None, :, None]
        return m.reshape(ns, SLAB_GROUPS * n_p, SLAB_GROUPS * c_g)

    bs = jnp.concatenate([diag_b(bb_re), diag_b(bb_im)], axis=-1)
    cs = jnp.concatenate([diag_c(c_re), -diag_c(c_im)], axis=1)
    lam = jnp.concatenate([lr.reshape(ns, 1, -1), li.reshape(ns, 1, -1)], axis=-1)
    return bs, cs, lam


def _s5_powers(a_re, a_im, log_dt, n):
    n_g, n_p = a_re.shape
    ns = n_g // SLAB_GROUPS
    dt = jnp.exp(log_dt)[:, None]
    mag = jnp.exp(a_re * dt)
    base_r, base_i = mag * jnp.cos(a_im * dt), mag * jnp.sin(a_im * dt)
    steps = jnp.arange(1, n + 1, dtype=jnp.int32)[:, None, None]
    pr, pi = jnp.ones((n, n_g, n_p), F32), jnp.zeros((n, n_g, n_p), F32)
    for b in range(n.bit_length()):
        take = ((steps >> b) & 1) == 1
        pr, pi = (jnp.where(take, pr * base_r - pi * base_i, pr), jnp.where(take, pr * base_i + pi * base_r, pi))
        base_r, base_i = base_r * base_r - base_i * base_i, 2.0 * base_r * base_i

    def slabs(re, im):
        one = lambda m: jnp.transpose(m.reshape(n, ns, SLAB_GROUPS * n_p), (1, 0, 2))
        return jnp.concatenate([one(re), one(im)], axis=-1)

    return slabs(pr, pi), slabs(pr[::-1], -pi[::-1])


def _lanes(j):
    return slice(j * LANES, (j + 1) * LANES)


def _tile8(k):
    return pl.ds(pl.multiple_of(k * SUBLANES, SUBLANES), SUBLANES)


def _s5_interleave(dst_ref, src_ref, t_seg):
    def body(k, _):
        dst_ref[_tile8(k), :] = src_ref[pl.ds(k, SUBLANES, stride=t_seg), :]
        return 0

    lax.fori_loop(0, t_seg, body, 0, unroll=4)


def _s5_join_segments(st_ref, end_ref, car_ref, tab_ref, row, order, n_pair):
    for j in range(n_pair):
        re, im = _lanes(j), _lanes(n_pair + j)
        cr, ci = st_ref[:, re], st_ref[:, im]
        tr, ti = tab_ref[row:row + 1, re], tab_ref[row:row + 1, im]
        for s in order:
            car_ref[s:s + 1, re] = cr
            car_ref[s:s + 1, im] = ci
            er, ei = end_ref[s:s + 1, re], end_ref[s:s + 1, im]
            cr, ci = er + tr * cr - ti * ci, ei + tr * ci + ti * cr
        st_ref[:, re] = cr
        st_ref[:, im] = ci


def _s5_add_carries(buf_ref, car_ref, tab_ref, t_seg, n_pair):
    def fix(k, _):
        rows = _tile8(k)
        tab = tab_ref[pl.ds(k, 1), :]
        for j in range(n_pair):
            re, im = _lanes(j), _lanes(n_pair + j)
            cr, ci = car_ref[:, re], car_ref[:, im]
            tr, ti = tab[:, re], tab[:, im]
            buf_ref[rows, re] += tr * cr - ti * ci
            buf_ref[rows, im] += tr * ci + ti * cr
        return 0

    lax.fori_loop(0, t_seg, fix, 0, unroll=2)


def _s5_scan_fwd(proj, u_col, bs, cs, lam, pw, t_blk, beside=None):
    seq = proj.shape[0]
    ns, _, w2 = bs.shape
    n_pair = w2 // (2 * LANES)
    t_seg, n_t = t_blk // SUBLANES, seq // t_blk

    def body(u_ref, bs_ref, cs_ref, lam_ref, pw_ref, yc_ref, h_ref, st_ref, end_ref, car_ref, ui_ref, bu_ref, yi_ref):
        @pl.when(pl.program_id(1) == 0)
        def _():
            st_ref[...] = jnp.zeros_like(st_ref)

        _s5_interleave(ui_ref, u_ref, t_seg)
        bu_ref[...] = lax.dot_general(ui_ref[...].astype(BF16), bs_ref[...], NN, preferred_element_type=F32)
        lam_r = [jnp.broadcast_to(lam_ref[:, _lanes(j)], (SUBLANES, LANES)) for j in range(n_pair)]
        lam_i = [jnp.broadcast_to(lam_ref[:, _lanes(n_pair + j)], (SUBLANES, LANES)) for j in range(n_pair)]

        def step(k, c):
            rows = _tile8(k)
            out = []
            for j in range(n_pair):
                hr, hi = c[2 * j], c[2 * j + 1]
                nr = lam_r[j] * hr - lam_i[j] * hi + bu_ref[rows, _lanes(j)]
                ni = lam_i[j] * hr + lam_r[j] * hi + bu_ref[rows, _lanes(n_pair + j)]
                h_ref[rows, _lanes(j)] = nr
                h_ref[rows, _lanes(n_pair + j)] = ni
                out += [nr, ni]
            return tuple(out)

        ends = lax.fori_loop(0, t_seg, step, (jnp.zeros((SUBLANES, LANES), F32),) * (2 * n_pair), unroll=4)
        for j in range(n_pair):
            end_ref[:, _lanes(j)] = ends[2 * j]
            end_ref[:, _lanes(n_pair + j)] = ends[2 * j + 1]
        _s5_join_segments(st_ref, end_ref, car_ref, pw_ref, t_seg - 1, list(range(SUBLANES)), n_pair)
        _s5_add_carries(h_ref, car_ref, pw_ref, t_seg, n_pair)
        yi_ref[...] = lax.dot_general(h_ref[...].astype(BF16), cs_ref[...], NN, preferred_element_type=F32)

        def scatter(k, _):
            yc_ref[pl.ds(k, SUBLANES, stride=t_seg), :] = yi_ref[_tile8(k), :]
            return 0

        lax.fori_loop(0, t_seg, scatter, 0, unroll=4)

    return _call_beside(
        beside, body, "s5_scan_fwd", (ns, n_t),
        [_spec((t_blk, LANES), lambda s, i: (i, u_col + s)),
         _spec((None, LANES, w2), lambda s, i: (s, 0, 0)),
         _spec((None, w2, LANES), lambda s, i: (s, 0, 0)),
         _spec((None, 1, w2), lambda s, i: (s, 0, 0)),
         _spec((None, t_seg, w2), lambda s, i: (s, 0, 0))],
        [_spec((t_blk, LANES), lambda s, i: (i, s)),
         _spec((None, t_blk, w2), lambda s, i: (s, i, 0))],
        [jax.ShapeDtypeStruct((seq, ns * LANES), F32), jax.ShapeDtypeStruct((ns, seq, w2), F32)],
        [pltpu.VMEM((1, w2), F32), pltpu.VMEM((SUBLANES, w2), F32), pltpu.VMEM((SUBLANES, w2), F32),
         pltpu.VMEM((t_blk, LANES), F32), pltpu.VMEM((t_blk, w2), F32), pltpu.VMEM((t_blk, LANES), F32)],
        _vmem_limit(3 * t_blk * w2 * 4), (proj, bs, cs, lam, pw), ("parallel", "arbitrary"))


def _s5_scan_bwd(proj, u_col, states, d_yc, du_extra, bs, cs, lam, qw, t_blk):
    seq = proj.shape[0]
    ns, _, w2 = bs.shape
    n_pair = w2 // (2 * LANES)
    t_seg, n_t = t_blk // SUBLANES, seq // t_blk

    def body(u_ref, h_ref, hp_ref, dyc_ref, dux_ref, bs_ref, cs_ref, lam_ref, qw_ref,
             du_ref, dbs_ref, dcs_ref, dlam_ref, g_ref, gd_ref, st_ref, end_ref, car_ref, ui_ref, dyi_ref, dui_ref):
        i = pl.program_id(1)

        @pl.when(i == 0)
        def _():
            st_ref[...] = jnp.zeros_like(st_ref)
            dbs_ref[...] = jnp.zeros_like(dbs_ref)
            dcs_ref[...] = jnp.zeros_like(dcs_ref)
            dlam_ref[...] = jnp.zeros_like(dlam_ref)

        _s5_interleave(ui_ref, u_ref, t_seg)
        _s5_interleave(dyi_ref, dyc_ref, t_seg)
        dyc_b = dyi_ref[...].astype(BF16)
        gd_ref[...] = lax.dot_general(dyc_b, cs_ref[...], NT, preferred_element_type=F32)
        lam_r = [jnp.broadcast_to(lam_ref[:, _lanes(j)], (SUBLANES, LANES)) for j in range(n_pair)]
        lam_i = [jnp.broadcast_to(lam_ref[:, _lanes(n_pair + j)], (SUBLANES, LANES)) for j in range(n_pair)]

        def step(kk, c):
            rows = _tile8(t_seg - 1 - kk)
            out = []
            for j in range(n_pair):
                gr_n, gi_n = c[2 * j], c[2 * j + 1]
                gr = gd_ref[rows, _lanes(j)] + lam_r[j] * gr_n + lam_i[j] * gi_n
                gi = gd_ref[rows, _lanes(n_pair + j)] + lam_r[j] * gi_n - lam_i[j] * gr_n
                g_ref[rows, _lanes(j)] = gr
                g_ref[rows, _lanes(n_pair + j)] = gi
                out += [gr, gi]
            return tuple(out)

        zero = jnp.zeros((SUBLANES, LANES), F32)
        firsts = lax.fori_loop(0, t_seg, step, (zero,) * (2 * n_pair), unroll=4)
        for j in range(n_pair):
            end_ref[:, _lanes(j)] = firsts[2 * j]
            end_ref[:, _lanes(n_pair + j)] = firsts[2 * j + 1]
        _s5_join_segments(st_ref, end_ref, car_ref, qw_ref, 0, list(range(SUBLANES))[::-1], n_pair)
        _s5_add_carries(g_ref, car_ref, qw_ref, t_seg, n_pair)

        def pair_up(k, c):
            rows, prev = _tile8(k), _tile8(k - 1)
            out = []
            for j in range(n_pair):
                re, im = _lanes(j), _lanes(n_pair + j)
                gr, gi, hr, hi = g_ref[rows, re], g_ref[rows, im], h_ref[prev, re], h_ref[prev, im]
                out += [c[2 * j] + gr * hr + gi * hi, c[2 * j + 1] + gi * hr - gr * hi]
            return tuple(out)

        acc = lax.fori_loop(1, t_seg, pair_up, (zero,) * (2 * n_pair), unroll=4)
        has_prev = (i < n_t - 1).astype(F32)
        first_seg = lax.broadcasted_iota(jnp.int32, (SUBLANES, LANES), 0) == 0
        last = _tile8(t_seg - 1)
        for j in range(n_pair):
            re, im = _lanes(j), _lanes(n_pair + j)
            gr, gi = g_ref[0:SUBLANES, re], g_ref[0:SUBLANES, im]
            hr = jnp.where(first_seg, hp_ref[SUBLANES - 1:, re] * has_prev, pltpu.roll(h_ref[last, re], 1, 0))
            hi = jnp.where(first_seg, hp_ref[SUBLANES - 1:, im] * has_prev, pltpu.roll(h_ref[last, im], 1, 0))
            dlam_ref[:, re] += jnp.sum(acc[2 * j] + gr * hr + gi * hi, axis=0, keepdims=True)
            dlam_ref[:, im] += jnp.sum(acc[2 * j + 1] + gi * hr - gr * hi, axis=0, keepdims=True)

        g_b = g_ref[...].astype(BF16)
        dui_ref[...] = lax.dot_general(g_b, bs_ref[...], NT, preferred_element_type=F32)
        dbs_ref[...] += lax.dot_general(ui_ref[...].astype(BF16), g_b, TN, preferred_element_type=F32)
        dcs_ref[...] += lax.dot_general(h_ref[...].astype(BF16), dyc_b, TN, preferred_element_type=F32)

        def scatter(k, _):
            rows = pl.ds(k, SUBLANES, stride=t_seg)
            du_ref[rows, :] = (dui_ref[_tile8(k), :] + dux_ref[rows, :]).astype(du_ref.dtype)
            return 0

        lax.fori_loop(0, t_seg, scatter, 0, unroll=4)

    rev = lambda i: n_t - 1 - i
    return pl.pallas_call(
        body, name="s5_scan_bwd", grid=(ns, n_t),
        in_specs=[_spec((t_blk, LANES), lambda s, i: (rev(i), u_col + s)),
                  _spec((None, t_blk, w2), lambda s, i: (s, rev(i), 0)),
                  _spec((None, SUBLANES, w2), lambda s, i: (s, jnp.maximum(rev(i) * t_seg - 1, 0), 0)),
                  _spec((t_blk, LANES), lambda s, i: (rev(i), s)),
                  _spec((t_blk, LANES), lambda s, i: (rev(i), s)),
                  _spec((None, LANES, w2), lambda s, i: (s, 0, 0)),
                  _spec((None, w2, LANES), lambda s, i: (s, 0, 0)),
                  _spec((None, 1, w2), lambda s, i: (s, 0, 0)),
                  _spec((None, t_seg, w2), lambda s, i: (s, 0, 0))],
        out_specs=[_spec((t_blk, LANES), lambda s, i: (rev(i), s)),
                   _spec((None, LANES, w2), lambda s, i: (s, 0, 0)),
                   _spec((None, w2, LANES), lambda s, i: (s, 0, 0)),
                   _spec((None, 1, w2), lambda s, i: (s, 0, 0))],
        out_shape=[jax.ShapeDtypeStruct((seq, ns * LANES), F32), jax.ShapeDtypeStruct(bs.shape, F32),
                   jax.ShapeDtypeStruct(cs.shape, F32), jax.ShapeDtypeStruct(lam.shape, F32)],
        scratch_shapes=[pltpu.VMEM((t_blk, w2), F32), pltpu.VMEM((t_blk, w2), F32), pltpu.VMEM((1, w2), F32),
                        pltpu.VMEM((SUBLANES, w2), F32), pltpu.VMEM((SUBLANES, w2), F32),
                        pltpu.VMEM((t_blk, LANES), F32), pltpu.VMEM((t_blk, LANES), F32), pltpu.VMEM((t_blk, LANES), F32)],
        compiler_params=pltpu.CompilerParams(dimension_semantics=("parallel", "arbitrary"),
                                             vmem_limit_bytes=_vmem_limit(5 * t_blk * w2 * 4)),
    )(*[_in_hbm(a) for a in (proj, states, states, d_yc, du_extra, bs, cs, lam, qw)])


def _loss_head(y, target, t_m):
    seq, d = y.shape

    def body(y_ref, t_ref, loss_ref, dy_ref):
        @pl.when(pl.program_id(0) == 0)
        def _():
            loss_ref[...] = jnp.zeros_like(loss_ref)

        diff = y_ref[...] - t_ref[...]
        dy_ref[...] = diff / d
        loss_ref[...] += 0.5 * jnp.sum(diff * diff) / d

    row = _spec((t_m, d), lambda i: (i, 0))
    return pl.pallas_call(
        body, name="loss_head", grid=(seq // t_m,), in_specs=[row, row],
        out_specs=[_spec((SUBLANES, LANES), lambda i: (0, 0)), row],
        out_shape=[jax.ShapeDtypeStruct((SUBLANES, LANES), F32), jax.ShapeDtypeStruct((seq, d), F32)],
        compiler_params=pltpu.CompilerParams(dimension_semantics=("arbitrary",),
                                             vmem_limit_bytes=_vmem_limit(6 * t_m * d * 4)),
    )(_in_hbm(y), _in_hbm(target))


def _adamw_fn(w, m, v, *partials):
    g = partials[0]
    for p in partials[1:]:
        g = g + p
    m2 = ADAM_B1 * m + (1.0 - ADAM_B1) * g
    v2 = ADAM_B2 * v + (1.0 - ADAM_B2) * (g * g)
    m_hat = m2 / (1.0 - ADAM_B1 ** ADAM_STEP)
    v_hat = v2 / (1.0 - ADAM_B2 ** ADAM_STEP)
    delta = -ADAM_LR * (m_hat / (jnp.sqrt(v_hat) + ADAM_EPS) + ADAM_WD * w)
    return g, delta, m2, v2


def _adamw(name, w, m, v, partials):
    rows, cols = w.shape
    t_r = rows
    for cand in (512, 256, 128, 64, 32, 16, 8):
        if rows % cand == 0 and cand * cols * 4 <= (1 << 20):
            t_r = cand
            break
    n_p = partials.shape[0]
    row = lambda i: (i, 0)
    ins = [(a, (t_r, cols), row) for a in (w, m, v)]
    ins += [(partials, (None, t_r, cols), (lambda i, j=j: (j, i, 0))) for j in range(n_p)]
    outs = [((rows, cols), F32, (t_r, cols), row)] * 4
    return _rowwise(name, _adamw_fn, ins, outs, (rows // t_r,))


SMALL_PARAMS = ("b_ada", "ssm_a_re", "ssm_a_im", "ssm_log_dt", "ssm_b_re", "ssm_b_im", "ssm_c_re", "ssm_c_im",
                "ssm_d", "b_glu", "ln1_g", "ln1_b", "ln2_g", "ln2_b")
WEIGHTS = ("w_ada", "b_ada", "w_in", "w_sb_up", "ssm_a_re", "ssm_a_im", "ssm_log_dt", "ssm_b_re", "ssm_b_im",
           "ssm_c_re", "ssm_c_im", "ssm_d", "w_glu", "b_glu", "w_ssm_up", "w_out", "ln1_g", "ln1_b", "w_ffn_in",
           "w_ffn_out", "ln2_g", "ln2_b")
ARG_NAMES = (("x", "c") + WEIGHTS + ("loss_target",) + tuple("m_" + n for n in WEIGHTS)
             + tuple("v_" + n for n in WEIGHTS))


def _pack(arrs):
    flat = jnp.concatenate([a.reshape(-1) for a in arrs])
    pad = (-flat.shape[0]) % (PACK_ROWS * LANES)
    return jnp.pad(flat, (0, pad)).reshape(-1, LANES)


def _unpack(packed, like):
    lead = packed.shape[:-2]
    flat = packed.reshape(lead + (-1,))
    out, off = [], 0
    for a in like:
        out.append(flat[..., off:off + a.size].reshape(lead + a.shape))
        off += a.size
    return out


def kernel(x, c, w_ada, b_ada, w_in, w_sb_up, ssm_a_re, ssm_a_im, ssm_log_dt, ssm_b_re, ssm_b_im, ssm_c_re,
           ssm_c_im, ssm_d, w_glu, b_glu, w_ssm_up, w_out, ln1_g, ln1_b, w_ffn_in, w_ffn_out, ln2_g, ln2_b,
           loss_target, m_w_ada, m_b_ada, m_w_in, m_w_sb_up, m_ssm_a_re, m_ssm_a_im, m_ssm_log_dt, m_ssm_b_re,
           m_ssm_b_im, m_ssm_c_re, m_ssm_c_im, m_ssm_d, m_w_glu, m_b_glu, m_w_ssm_up, m_w_out, m_ln1_g, m_ln1_b,
           m_w_ffn_in, m_w_ffn_out, m_ln2_g, m_ln2_b, v_w_ada, v_b_ada, v_w_in, v_w_sb_up, v_ssm_a_re, v_ssm_a_im,
           v_ssm_log_dt, v_ssm_b_re, v_ssm_b_im, v_ssm_c_re, v_ssm_c_im, v_ssm_d, v_w_glu, v_b_glu, v_w_ssm_up,
           v_w_out, v_ln1_g, v_ln1_b, v_w_ffn_in, v_w_ffn_out, v_ln2_g, v_ln2_b):
    given = locals()
    return _train_step({n: given[n] for n in ARG_NAMES})


def _train_step(p):
    x0 = p["x"][0]
    target = p["loss_target"][0]
    seq, d = x0.shape
    depth = p["w_ada"].shape[0]
    n_ada = p["w_ada"].shape[2]
    n_in = p["w_in"].shape[2]
    sb_w = p["w_sb_up"].shape[1]
    ssm_w = p["w_ssm_up"].shape[1]
    n_up = p["w_sb_up"].shape[2]
    n_ffn = p["w_ffn_in"].shape[2]
    ffn = N_DEV * p["w_ffn_out"].shape[1]
    in_cols = N_DEV * n_in
    alpha = (2 * depth) ** 0.25
    resid_ln, resid_ln_mod = _make_resid_fns(alpha)
    t_r = min(512, seq)
    n_r = seq // t_r
    t_m = min(1024, seq)
    n_m = seq // t_m
    t_d = _tile(d)
    assert n_ffn * (N_DEV // 2) == ffn and sb_w % LANES == 0 and ssm_w % LANES == 0 and d % LANES == 0
    assert n_in % LANES == 0 and n_up % LANES == 0 and seq % t_m == 0 and in_cols == 3 * sb_w + ssm_w + 2 * d
    assert (3 * sb_w) % ssm_w == 0 and (3 * sb_w + ssm_w) % (2 * d) == 0

    bf = lambda a: a.astype(BF16)
    got = _exchange("gather_first", [], [bf(p["w_in"][0]), p["c"]])
    wg_in = [got[0]] + [None] * (depth - 1)
    c_all = got[1].reshape(N_DEV, d)
    small_names = ("w_sb_up", "w_ssm_up", "w_glu", "w_out")
    wg_ffn_in, wg_ffn_out, wg = [None] * depth, [None] * depth, {}

    c_pad = jnp.pad(c_all, ((0, 2 * SUBLANES - N_DEV), (0, 0)))
    c_act = _rowwise("silu_c", lambda v: v * jax.nn.sigmoid(v), [(c_pad, c_pad.shape, lambda i: (0, 0))],
                     [(c_pad.shape, F32, c_pad.shape, lambda i: (0, 0))], (1,))[0]
    rows_c = c_pad.shape[0]
    mod_cols = [
        _mm(f"mod_{l}", c_act, p["w_ada"],
            _spec((rows_c, d), lambda i, j, k: (0, 0)), _spec((None, d, n_ada), lambda i, j, k, l=l: (l, 0, 0)),
            _spec((rows_c, n_ada), lambda i, j, k: (0, 0)), (rows_c, n_ada), F32, (1, 1, 1), NN)
        for l in range(depth)]
    mod_send = jnp.stack([m[:N_DEV] for m in mod_cols], axis=1)
    mod_recv = _exchange("exchange_mod", [mod_send], [])[0]
    mod_nobias = jnp.swapaxes(mod_recv, 0, 1).reshape(depth, N_DEV * n_ada)
    full2 = lambda a: (a, a.shape, lambda i: (0, 0))
    mod = _rowwise("mod_bias", lambda a, b: a + b, [full2(mod_nobias), full2(p["b_ada"])],
                   [(mod_nobias.shape, F32, mod_nobias.shape, lambda i: (0, 0))], (1,))[0]
    vec = lambda a: a.reshape(1, -1)
    mods = [[vec(mod[l, j * d:(j + 1) * d]) for j in range(6)] for l in range(depth)]
    ln = {n: [vec(p[n][l]) for l in range(depth)] for n in ("ln1_g", "ln1_b", "ln2_g", "ln2_b")}

    row_spec = lambda width: ((t_r, width), lambda i: (i, 0))
    col_spec = lambda width, cb: ((t_r, width), lambda i, cb=cb: (i, cb))
    vec_spec = lambda width: ((1, width), lambda i: (0, 0))
    rows_in = lambda a: (a,) + row_spec(a.shape[1])
    vec_in = lambda a: (a,) + vec_spec(a.shape[1])
    row_out = lambda width, dt: ((seq, width), dt) + row_spec(width)

    s5 = [_s5_discretize(*[p[n][l] for n in ("ssm_a_re", "ssm_a_im", "ssm_log_dt", "ssm_b_re", "ssm_b_im",
                                               "ssm_c_re", "ssm_c_im")]) for l in range(depth)]
    s5_b16 = [(bs.astype(BF16), cs.astype(BF16), lam) for bs, cs, lam in s5]
    t_scan = min(512, seq)
    s5_pw = [_s5_powers(p["ssm_a_re"][l], p["ssm_a_im"][l], p["ssm_log_dt"][l], t_scan // SUBLANES)
             for l in range(depth)]
    u_col = 3 * sb_w // LANES
    gates_cb = (3 * sb_w + ssm_w) // (2 * d)
    ssm_d = [vec(p["ssm_d"][l]) for l in range(depth)]
    b_glu = [vec(p["b_glu"][l]) for l in range(depth)]
    n_half = N_DEV // 2

    h = _rowwise("modulate_in", _modulate, [rows_in(x0), vec_in(mods[0][1]), vec_in(mods[0][0])],
                 [row_out(d, BF16)], (n_r,))[0]
    saved = []
    x_cur = x0
    for l in range(depth):
        sv = {"x_in": x_cur, "h": h}
        last = l == depth - 1
        t_n = _tile(n_in)
        r_n = n_in // t_n
        proj = _mm(f"proj_{l}", h, wg_in[l],
                   _spec((t_m, d), lambda i, j, k: (i, 0)),
                   _spec((None, d, t_n), lambda i, j, k, r=r_n: (j // r, 0, j % r)),
                   _spec((t_m, t_n), lambda i, j, k: (i, j)), (seq, in_cols), F32, (n_m, N_DEV * r_n, 1), NN)
        arriving = [bf(p["w_ffn_in"][l]), bf(p["w_ffn_out"][l])] + ([bf(p[n]) for n in small_names] if l == 0 else [])
        (o_sb, o_sb32), got = _sb_attention_fwd(proj, sb_w, beside=_Exchange(gather=arriving))
        wg_ffn_in[l] = got[0]
        wg_ffn_out[l] = got[1].reshape(n_half, n_ffn, d)
        if l == 0:
            wg = dict(zip(small_names, got[2:]))
            for n in ("w_glu", "w_out"):
                wg[n] = jnp.swapaxes(wg[n], 0, 1).reshape(depth, -1, wg[n].shape[-1])
            for n in ("w_sb_up", "w_ssm_up"):
                wg[n] = jnp.transpose(wg[n], (1, 2, 0, 3)).reshape(depth, wg[n].shape[2], d)
        bs16, cs16, lam = s5_b16[l]
        (yc, states), got = _s5_scan_fwd(proj, u_col, bs16, cs16, lam, s5_pw[l][0], t_scan,
                                         beside=None if last else _Exchange(gather=[bf(p["w_in"][l + 1])]))
        if not last:
            wg_in[l + 1] = got[0]
        u_in = (proj,) + col_spec(ssm_w, 3 * sb_w // ssm_w)
        y1 = _rowwise(f"s5_act_{l}", _s5_act_fn, [rows_in(yc), u_in, vec_in(ssm_d[l])],
                      [row_out(ssm_w, BF16)], (n_r,))[0]
        t_glu = _mm(f"s5_glu_mm_{l}", y1, wg["w_glu"],
                    _spec((t_m, ssm_w), lambda i, j, k: (i, 0)), _spec((None, ssm_w, ssm_w), lambda i, j, k, l=l: (l, 0, 0)),
                    _spec((t_m, ssm_w), lambda i, j, k: (i, 0)), (seq, ssm_w), F32, (n_m, 1, 1), NN)
        s5_out = _rowwise(f"s5_glu_{l}", _s5_glu_fn,
                          [rows_in(yc), u_in, rows_in(t_glu), vec_in(ssm_d[l]), vec_in(b_glu[l])],
                          [row_out(ssm_w, BF16)], (n_r,))[0]

        merged, y_sb, y_ssm = _up_merge(f"up_merge_{l}", o_sb, s5_out, proj, gates_cb, wg["w_sb_up"], wg["w_ssm_up"],
                                        l, t_r)
        y_mix = _mm(f"out_proj_{l}", merged, wg["w_out"],
                    _spec((t_m, d), lambda i, j, k: (i, 0)), _spec((None, d, t_d), lambda i, j, k, l=l: (l, 0, j)),
                    _spec((t_m, t_d), lambda i, j, k: (i, j)), (seq, d), F32, (n_m, d // t_d, 1), NN)
        vecs_a = [mods[l][2], ln["ln1_g"][l], ln["ln1_b"][l], mods[l][4], mods[l][3]]
        x_mid, h2 = _rowwise(f"resid_mix_{l}", resid_ln_mod, [rows_in(x_cur), rows_in(y_mix)] + [vec_in(v) for v in vecs_a],
                             [row_out(d, F32), row_out(d, BF16)], (n_r,))
        a_ffn, f_act = _ffn_in_swiglu(f"ffn_in_{l}", h2, wg_ffn_in[l], t_r)
        y_ffn = _mm(f"ffn_out_{l}", f_act, wg_ffn_out[l],
                    _spec((None, t_m, n_ffn), lambda i, j, k: (k, i, 0)),
                    _spec((None, n_ffn, t_d), lambda i, j, k: (k, 0, j)),
                    _spec((t_m, t_d), lambda i, j, k: (i, j)), (seq, d), F32, (n_m, d // t_d, n_half), NN)
        vecs_b = [mods[l][5], ln["ln2_g"][l], ln["ln2_b"][l]] + ([] if last else [mods[l + 1][1], mods[l + 1][0]])
        outs_b = [row_out(d, F32)] + ([] if last else [row_out(d, BF16)])
        res = _rowwise(f"resid_ffn_{l}", resid_ln if last else resid_ln_mod,
                       [rows_in(x_mid), rows_in(y_ffn)] + [vec_in(v) for v in vecs_b], outs_b, (n_r,))
        sv.update(proj=proj, o_sb=o_sb, o_sb32=o_sb32, yc=yc, states=states, y1=y1, t_glu=t_glu, s5_out=s5_out,
                  y_sb=y_sb, y_ssm=y_ssm, merged=merged, y_mix=y_mix, x_mid=x_mid, h2=h2, a_ffn=a_ffn, f_act=f_act,
                  y_ffn=y_ffn, vecs_a=vecs_a, vecs_b=vecs_b)
        saved.append(sv)
        x_cur = res[0]
        h = None if last else res[1]

    loss_part, d_x = _loss_head(x_cur, target, t_r)
    loss = lax.psum(loss_part[0, 0], MESH_AXES)

    d_h_next = None
    grads = {n: [None] * depth for n in WEIGHTS}
    d_mod = [[None] * 6 for _ in range(depth)]
    land = {}
    waiting = []
    row_wrt = lambda i, width, dt: (i, "row", (seq, width), dt) + row_spec(width)
    sum_wrt = lambda i, width: (i, "sum", (1, width), F32) + vec_spec(width)
    for l in reversed(range(depth)):
        sv = saved[l]
        last = l == depth - 1
        ins_b = [rows_in(sv["x_mid"]), rows_in(sv["y_ffn"])] + [vec_in(v) for v in sv["vecs_b"]]
        cts_b = [rows_in(d_x)] + ([] if last else [rows_in(d_h_next)])
        wrt_b = [row_wrt(0, d, F32), row_wrt(1, d, BF16)] + [sum_wrt(2 + j, d) for j in range(len(sv["vecs_b"]))]
        res = _rowwise_vjp(f"resid_ffn_bwd_{l}", resid_ln if last else resid_ln_mod, ins_b, cts_b, wrt_b, (n_r,))
        d_x_mid, d_y_ffn = res[0], res[1]
        d_mod[l][5], grads["ln2_g"][l], grads["ln2_b"][l] = res[2], res[3], res[4]
        if not last:
            d_mod[l + 1][1], d_mod[l + 1][0] = res[5], res[6]
        d_a = _ffn_out_dx_swiglu(f"ffn_out_dx_{l}", d_y_ffn, wg_ffn_out[l], sv["a_ffn"], t_r).reshape(N_DEV, seq, n_ffn)
        g_ffn_out = _mm(f"ffn_out_dw_{l}", sv["f_act"], d_y_ffn,
                        _spec((None, t_m, n_ffn), lambda i, j, k: (i, k, 0)), _spec((t_m, t_d), lambda i, j, k: (k, j)),
                        _spec((None, n_ffn, t_d), lambda i, j, k: (i, 0, j)), (n_half, n_ffn, d), GRAD_WIRE,
                        (n_half, d // t_d, n_m), TN, reread=(False, True))
        d_h2 = _mm(f"ffn_in_dx_{l}", d_a, wg_ffn_in[l],
                   _spec((None, t_m, n_ffn), lambda i, j, k: (k, i, 0)),
                   _spec((None, t_d, n_ffn), lambda i, j, k: (k, j, 0)),
                   _spec((t_m, t_d), lambda i, j, k: (i, j)), (seq, d), F32, (n_m, d // t_d, N_DEV), NT)
        g_ffn_in = _mm(f"ffn_in_dw_{l}", sv["h2"], d_a,
                       _spec((t_m, t_d), lambda i, j, k: (k, j)), _spec((None, t_m, n_ffn), lambda i, j, k: (i, k, 0)),
                       _spec((None, t_d, n_ffn), lambda i, j, k: (i, j, 0)), (N_DEV, d, n_ffn), GRAD_WIRE,
                       (N_DEV, d // t_d, n_m), TN, reread=(True, False))
        ins_a = [rows_in(sv["x_in"]), rows_in(sv["y_mix"])] + [vec_in(v) for v in sv["vecs_a"]]
        wrt_a = [row_wrt(0, d, F32), row_wrt(1, d, BF16)] + [sum_wrt(2 + j, d) for j in range(5)]
        res = _rowwise_vjp(f"resid_mix_bwd_{l}", resid_ln_mod, ins_a, [rows_in(d_x_mid), rows_in(d_h2)], wrt_a, (n_r,))
        d_x_in, d_y_mix = res[0], res[1]
        d_mod[l][2], grads["ln1_g"][l], grads["ln1_b"][l], d_mod[l][4], d_mod[l][3] = res[2:7]
        d_merged = _mm(f"out_proj_dx_{l}", d_y_mix, wg["w_out"],
                       _spec((t_m, d), lambda i, j, k: (i, 0)), _spec((None, t_d, d), lambda i, j, k, l=l: (l, j, 0)),
                       _spec((t_m, t_d), lambda i, j, k: (i, j)), (seq, d), F32, (n_m, d // t_d, 1), NT)
        g_out = _mm(f"out_proj_dw_{l}", sv["merged"], d_y_mix,
                    _spec((t_m, t_d), lambda i, j, k: (k, i)), _spec((t_m, t_d), lambda i, j, k: (k, j)),
                    _spec((t_d, t_d), lambda i, j, k: (i, j)), (d, d), GRAD_WIRE, (d // t_d, d // t_d, n_m), TN, reread=(d > t_d, d > t_d))
        gates = (sv["proj"],) + col_spec(2 * d, gates_cb)
        d_y_sb, d_y_ssm, d_gates = _rowwise_vjp(
            f"merge_bwd_{l}", _merge_fn, [rows_in(sv["y_sb"]), rows_in(sv["y_ssm"]), gates], [rows_in(d_merged)],
            [row_wrt(0, d, BF16), row_wrt(1, d, BF16), row_wrt(2, 2 * d, BF16)], (n_r,))

        def up_bwd(name, act, d_y, w, dx_dtype, l=l):
            k_w = act.shape[1]
            dx = _mm(name + "_dx", d_y, w, _spec((t_m, d), lambda i, j, k: (i, 0)),
                     _spec((None, k_w, d), lambda i, j, k: (l, 0, 0)),
                     _spec((t_m, k_w), lambda i, j, k: (i, 0)), (seq, k_w), dx_dtype, (n_m, 1, 1), NT)
            dw = _mm(name + "_dw", act, d_y, _spec((t_m, k_w), lambda i, j, k: (k, 0)),
                     _spec((t_m, t_d), lambda i, j, k: (k, j)),
                     _spec((k_w, t_d), lambda i, j, k: (0, j)), (k_w, d), GRAD_WIRE, (1, d // t_d, n_m), TN,
                     reread=(d > t_d, False))
            return dx, jnp.swapaxes(dw.reshape(k_w, N_DEV, n_up), 0, 1)

        d_o_sb, g_sb_up = up_bwd(f"sb_up_{l}", sv["o_sb"], d_y_sb, wg["w_sb_up"], BF16)
        d_s5_out, g_ssm_up = up_bwd(f"ssm_up_{l}", sv["s5_out"], d_y_ssm, wg["w_ssm_up"], F32)
        waiting += [("w_ffn_in", g_ffn_in), ("w_ffn_out", g_ffn_out.reshape(N_DEV, -1, d)),
                    ("w_out", g_out.reshape(N_DEV, -1, d)), ("w_sb_up", g_sb_up), ("w_ssm_up", g_ssm_up)]
        levels = [l + 1] * (len(waiting) - 5) + [l] * 5
        (d_q, d_k, d_v), got = _sb_attention_bwd(
            sv["proj"], sv["o_sb32"], d_o_sb, sb_w,
            beside=_Exchange(layered=[(g, lv, depth, land.get(n)) for (n, g), lv in zip(waiting, levels)]))
        land.update({n: buf for (n, _), buf in zip(waiting, got)})
        u_in = (sv["proj"],) + col_spec(ssm_w, 3 * sb_w // ssm_w)
        ins_s5 = [rows_in(sv["yc"]), u_in, rows_in(sv["t_glu"]), vec_in(ssm_d[l]), vec_in(b_glu[l])]
        d_t = _rowwise_vjp(f"s5_glu_bwd_{l}", _s5_glu_fn, ins_s5, [rows_in(d_s5_out)],
                           [row_wrt(2, ssm_w, BF16)], (n_r,))[0]
        d_y1 = _mm(f"s5_glu_mm_dx_{l}", d_t, wg["w_glu"],
                   _spec((t_m, ssm_w), lambda i, j, k: (i, 0)), _spec((None, ssm_w, ssm_w), lambda i, j, k, l=l: (l, 0, 0)),
                   _spec((t_m, ssm_w), lambda i, j, k: (i, 0)), (seq, ssm_w), F32, (n_m, 1, 1), NT)
        g_glu = _mm(f"s5_glu_mm_dw_{l}", sv["y1"], d_t,
                    _spec((t_m, ssm_w), lambda i, j, k: (k, 0)), _spec((t_m, ssm_w), lambda i, j, k: (k, 0)),
                    _spec((ssm_w, ssm_w), lambda i, j, k: (0, 0)), (ssm_w, ssm_w), GRAD_WIRE, (1, 1, n_m), TN, reread=(False, False))
        d_yc, d_u_skip, grads["ssm_d"][l], grads["b_glu"][l] = _rowwise_vjp(
            f"s5_post_bwd_{l}", _s5_post_fn, ins_s5, [rows_in(d_y1), rows_in(d_s5_out)],
            [row_wrt(0, ssm_w, F32), row_wrt(1, ssm_w, F32), sum_wrt(3, ssm_w), sum_wrt(4, ssm_w)], (n_r,))
        bs16, cs16, lam = s5_b16[l]
        d_u, d_bs, d_cs, d_lam = _s5_scan_bwd(sv["proj"], u_col, sv["states"], d_yc, d_u_skip, bs16, cs16, lam,
                                              s5_pw[l][1], t_scan)
        raw = [p[n][l] for n in ("ssm_a_re", "ssm_a_im", "ssm_log_dt", "ssm_b_re", "ssm_b_im", "ssm_c_re", "ssm_c_im")]
        _, pull = jax.vjp(_s5_discretize, *raw)
        (grads["ssm_a_re"][l], grads["ssm_a_im"][l], grads["ssm_log_dt"][l], grads["ssm_b_re"][l],
         grads["ssm_b_im"][l], grads["ssm_c_re"][l], grads["ssm_c_im"][l]) = pull((d_bs, d_cs, d_lam))
        d_proj = jnp.concatenate([d_q, d_k.astype(BF16), d_v.astype(BF16), d_u.astype(BF16), d_gates], axis=1)
        t_n = _tile(n_in)
        g_in = _mm(f"proj_dw_{l}", sv["h"], d_proj,
                   _spec((t_m, t_d), lambda i, j, k: (k, j)), _spec((t_m, n_in), lambda i, j, k: (k, i)),
                   _spec((None, t_d, n_in), lambda i, j, k: (i, j, 0)), (N_DEV, d, n_in), GRAD_WIRE,
                   (N_DEV, d // t_d, n_m), TN, reread=(True, False))
        waiting = [("w_in", g_in), ("w_glu", g_glu.reshape(N_DEV, -1, ssm_w))]
        closing = _Exchange(layered=[(g, 0, depth, land.get(n)) for n, g in waiting]) if l == 0 else None
        d_h = _mm(f"proj_dx_{l}", d_proj, wg_in[l],
                  _spec((t_m, n_in), lambda i, j, k: (i, k)), _spec((None, t_d, n_in), lambda i, j, k: (k, j, 0)),
                  _spec((t_m, t_d), lambda i, j, k: (i, j)), (seq, d), F32, (n_m, d // t_d, N_DEV), NT, beside=closing)
        if l == 0:
            d_h, got = d_h
            land.update({n: buf for (n, _), buf in zip(waiting, got)})
        d_x, d_h_next = d_x_in, d_h
    res = _rowwise_vjp("modulate_in_bwd", lambda v, sc, sh: (v, _modulate(v, sc, sh)),
                       [rows_in(x0), vec_in(mods[0][1]), vec_in(mods[0][0])], [rows_in(d_x), rows_in(d_h_next)],
                       [row_wrt(0, d, F32), sum_wrt(1, d), sum_wrt(2, d)], (n_r,))
    grad_x, d_mod[0][1], d_mod[0][0] = res

    d_mod_rows = jnp.concatenate([jnp.concatenate(d_mod[l], axis=1) for l in range(depth)], axis=0)
    grads["b_ada"] = [d_mod_rows[l] for l in range(depth)]
    small_local = [jnp.stack([g.reshape(p[n].shape[1:]) for g in grads[n]]) for n in SMALL_PARAMS]
    d_mod_send = jnp.swapaxes(d_mod_rows.reshape(depth, N_DEV, n_ada), 0, 1)
    small_sum, (d_mod_cols,) = _reduce_packed("exchange_last", _pack(small_local), [d_mod_send])
    d_mod_pad = jnp.pad(jnp.swapaxes(d_mod_cols, 0, 1), ((0, 0), (0, rows_c - N_DEV), (0, 0)))
    g_ada = [
        _mm(f"mod_dw_{l}", c_act, d_mod_pad,
            _spec((rows_c, d), lambda i, j, k: (0, 0)), _spec((None, rows_c, n_ada), lambda i, j, k, l=l: (l, 0, 0)),
            _spec((d, n_ada), lambda i, j, k: (0, 0)), (d, n_ada), F32, (1, 1, 1), TN)
        for l in range(depth)]

    out = {}

    def update(name, partials):
        shape = p[name].shape
        two_d = lambda a: a.reshape(-1, shape[-1])
        res = _adamw("adamw_" + name, two_d(p[name]), two_d(p["m_" + name]), two_d(p["v_" + name]),
                     partials.reshape(partials.shape[0], -1, shape[-1]))
        out[name] = [r.reshape(shape) for r in res]

    update("w_ada", jnp.stack(g_ada)[None])
    for n in ("w_in", "w_sb_up", "w_ssm_up", "w_ffn_in", "w_glu", "w_out", "w_ffn_out"):
        update(n, land[n])
    small_w = [p[n] for n in SMALL_PARAMS]
    res = _adamw("adamw_small", _pack(small_w), _pack([p["m_" + n] for n in SMALL_PARAMS]),
                 _pack([p["v_" + n] for n in SMALL_PARAMS]), small_sum[None])
    for kind, packed in enumerate(res):
        for n, a in zip(SMALL_PARAMS, _unpack(packed, small_w)):
            out.setdefault(n, [None] * 4)[kind] = a

    return ((loss, grad_x[None]) + tuple(out[n][0] for n in WEIGHTS) + tuple(out[n][1] for n in WEIGHTS)
            + tuple(out[n][2] for n in WEIGHTS) + tuple(out[n][3] for n in WEIGHTS))
```

```python
import jax
import jax.numpy as jnp
from jax import lax
from jax.experimental import pallas as pl
from jax.experimental.pallas import tpu as pltpu

F32 = jnp.float32
BF16 = jnp.bfloat16
GRAD_WIRE = BF16
FFN_ACT = BF16

N_DEV = 8
LANES = 128
SUBLANES = 8
VMEM_BYTES = 64 * 1024 * 1024
HEAD_DIM = 64
SB_BLOCK = 256
SLAB_GROUPS = 8
LN_EPS = 1e-5
ADAM_LR, ADAM_B1, ADAM_B2, ADAM_EPS, ADAM_WD, ADAM_STEP = 0.001, 0.9, 0.999, 1e-08, 0.01, 10
SB_UNDERFLOW = -120.0

PACK_ROWS = 256
MESH_AXES = ("x", "y", "c")


def _vmem_limit(block_bytes):
    return int(min(max(3 * block_bytes + (8 << 20), 24 << 20), VMEM_BYTES - (8 << 20)))


def _nbytes(shape, dtype):
    n = 1
    for d in shape:
        if d is not None:
            n *= d
    return n * jnp.dtype(dtype).itemsize


def _spec(shape, fn):
    return pl.BlockSpec(shape, fn)


class _Exchange:
    def __init__(self, scatter=(), gather=(), layered=()):
        self.arrs = list(scatter) + [a for a, _, _, _ in layered] + list(gather)
        self.n = len(self.arrs)
        self.n_sc = len(scatter) + len(layered)
        self.layer = [None] * len(scatter) + [l for _, l, _, _ in layered] + [None] * len(gather)
        self.shapes = ([a.shape for a in scatter] + [(N_DEV, dp) + a.shape[1:] for a, _, dp, _ in layered]
                       + [(N_DEV,) + a.shape for a in gather])
        self.held = [(len(scatter) + i, b) for i, (_, _, _, b) in enumerate(layered) if b is not None]
        self.operands = self.arrs + [b for _, b in self.held]
        hbm = pl.BlockSpec(memory_space=pltpu.HBM)
        self.in_specs = [hbm] * len(self.operands)
        self.out_specs = [hbm] * self.n
        self.out_shape = [jax.ShapeDtypeStruct(s, a.dtype) for s, a in zip(self.shapes, self.arrs)]
        self.scratch = [pltpu.SemaphoreType.DMA((self.n, N_DEV - 1)), pltpu.SemaphoreType.DMA((self.n, N_DEV - 1)),
                        pltpu.SemaphoreType.DMA((self.n,))]

    def aliases(self, first_in, first_out):
        return {first_in + self.n + i: first_out + a for i, (a, _) in enumerate(self.held)}

    def copies(self, ins, outs, sems):
        send_sems, recv_sems, own_sems = sems
        x, y, c = lax.axis_index("x"), lax.axis_index("y"), lax.axis_index("c")
        me = 4 * x + 2 * y + c
        landing = [outs[a].at[me] if self.layer[a] is None else outs[a].at[me, self.layer[a]] for a in range(self.n)]
        out = [pltpu.make_async_copy(ins[a].at[me] if a < self.n_sc else ins[a], landing[a], own_sems.at[a])
               for a in range(self.n)]
        for k in range(1, N_DEV):
            px = 1 - x if k & 4 else x
            py = 1 - y if k & 2 else y
            pc = 1 - c if k & 1 else c
            peer = 4 * px + 2 * py + pc
            for a in range(self.n):
                out.append(pltpu.make_async_remote_copy(
                    src_ref=ins[a].at[peer] if a < self.n_sc else ins[a], dst_ref=landing[a],
                    send_sem=send_sems.at[a, k - 1], recv_sem=recv_sems.at[a, k - 1],
                    device_id=(px, py, pc), device_id_type=pl.DeviceIdType.MESH))
        return out


def _exchange(name, scatter, gather, layered=()):
    ex = _Exchange(scatter, gather, layered)

    def body(*refs):
        copies = ex.copies(refs[:ex.n], refs[len(ex.operands):len(ex.operands) + ex.n], refs[-3:])
        for cp in copies:
            cp.start()
        for cp in copies:
            cp.wait()

    return pl.pallas_call(body, name=name, in_specs=ex.in_specs, out_specs=ex.out_specs, out_shape=ex.out_shape,
                          input_output_aliases=ex.aliases(0, 0), scratch_shapes=ex.scratch)(*ex.operands)


def _reduce_packed(name, packed, scatter):
    rows = packed.shape[0]
    blk = rows // N_DEV
    ex = _Exchange(scatter=[packed.reshape(N_DEV, blk, LANES)] + list(scatter))
    n_in = len(ex.operands)

    def body(*refs):
        ins, outs = refs[:ex.n], refs[n_in:n_in + ex.n]
        total_ref = refs[n_in + ex.n]
        sems, (send2, recv2, own2, load_sem) = refs[n_in + ex.n + 1:n_in + ex.n + 4], refs[n_in + ex.n + 4:-2]
        land_v, sum_v = refs[-2:]
        copies = ex.copies(ins, outs, sems)
        for cp in copies:
            cp.start()
        for cp in copies:
            cp.wait()
        load = pltpu.make_async_copy(outs[0], land_v, load_sem)
        load.start()
        load.wait()
        acc = land_v[0]
        for i in range(1, N_DEV):
            acc = acc + land_v[i]
        sum_v[...] = acc
        x, y, c = lax.axis_index("x"), lax.axis_index("y"), lax.axis_index("c")
        me = 4 * x + 2 * y + c
        back = [pltpu.make_async_copy(sum_v, total_ref.at[me], own2)]
        for k in range(1, N_DEV):
            peer = (1 - x if k & 4 else x, 1 - y if k & 2 else y, 1 - c if k & 1 else c)
            back.append(pltpu.make_async_remote_copy(
                src_ref=sum_v, dst_ref=total_ref.at[me], send_sem=send2.at[k - 1], recv_sem=recv2.at[k - 1],
                device_id=peer, device_id_type=pl.DeviceIdType.MESH))
        for cp in back:
            cp.start()
        for cp in back:
            cp.wait()

    hbm = pl.BlockSpec(memory_space=pltpu.HBM)
    res = pl.pallas_call(
        body, name=name, in_specs=ex.in_specs, out_specs=ex.out_specs + [hbm],
        out_shape=ex.out_shape + [jax.ShapeDtypeStruct((N_DEV, blk, LANES), F32)],
        scratch_shapes=ex.scratch + [pltpu.SemaphoreType.DMA((N_DEV - 1,)), pltpu.SemaphoreType.DMA((N_DEV - 1,)),
                                     pltpu.SemaphoreType.DMA, pltpu.SemaphoreType.DMA,
                                     pltpu.VMEM((N_DEV, blk, LANES), F32), pltpu.VMEM((blk, LANES), F32)],
    )(*ex.operands)
    return res[-1].reshape(rows, LANES), res[1:-1]


def _call_beside(ex, body, name, grid, in_specs, out_specs, out_shape, scratch_shapes, vmem_bytes, operands,
                 semantics, in_hbm=True):
    if in_hbm:
        operands = [_in_hbm(a) for a in operands]
    if ex is None:
        res = pl.pallas_call(
            body, name=name, grid=grid, in_specs=in_specs, out_specs=out_specs, out_shape=out_shape,
            scratch_shapes=scratch_shapes,
            compiler_params=pltpu.CompilerParams(dimension_semantics=semantics, vmem_limit_bytes=vmem_bytes),
        )(*operands)
        return res, None
    n_in, n_out, n_scr = len(in_specs), len(out_specs), len(scratch_shapes)
    n_xin = len(ex.operands)

    def fused(*refs):
        mine = refs[:n_in] + refs[n_in + n_xin:n_in + n_xin + n_out]
        mine += refs[n_in + n_xin + n_out + ex.n:n_in + n_xin + n_out + ex.n + n_scr]
        first = pl.program_id(0) == 0
        last = pl.program_id(0) == grid[0] - 1
        for dim in range(1, len(grid)):
            first = jnp.logical_and(first, pl.program_id(dim) == 0)
            last = jnp.logical_and(last, pl.program_id(dim) == grid[dim] - 1)
        x_ins = refs[n_in:n_in + ex.n]
        x_outs = refs[n_in + n_xin + n_out:n_in + n_xin + n_out + ex.n]

        @pl.when(first)
        def _():
            for cp in ex.copies(x_ins, x_outs, refs[-3:]):
                cp.start()

        body(*mine)

        @pl.when(last)
        def _():
            for cp in ex.copies(x_ins, x_outs, refs[-3:]):
                cp.wait()

    res = pl.pallas_call(
        fused, name=name, grid=grid, in_specs=list(in_specs) + ex.in_specs, out_specs=list(out_specs) + ex.out_specs,
        out_shape=list(out_shape) + ex.out_shape, input_output_aliases=ex.aliases(n_in, n_out),
        scratch_shapes=list(scratch_shapes) + ex.scratch,
        compiler_params=pltpu.CompilerParams(dimension_semantics=("arbitrary",) * len(grid),
                                             vmem_limit_bytes=vmem_bytes),
    )(*operands, *ex.operands)
    return res[:n_out], res[n_out:]


NN = (((1,), (0,)), ((), ()))
NT = (((1,), (1,)), ((), ()))
TN = (((0,), (0,)), ((), ()))


def _in_hbm(a):
    return pltpu.with_memory_space_constraint(a, pltpu.HBM)


def _mm(name, a, b, a_spec, b_spec, o_spec, o_shape, o_dtype, grid, dims, beside=None, reread=(False, True)):
    nk = grid[2]
    a, b = (x if again else _in_hbm(x) for x, again in zip((a, b), reread))
    acc_shape = tuple(d for d in o_spec.block_shape if d is not None)

    def product(a_ref, b_ref):
        return lax.dot_general(a_ref[...].astype(BF16), b_ref[...].astype(BF16), dims, preferred_element_type=F32)

    def body_once(a_ref, b_ref, o_ref):
        o_ref[...] = product(a_ref, b_ref).astype(o_ref.dtype)

    def body(a_ref, b_ref, o_ref, acc_ref):
        k = pl.program_id(2)

        @pl.when(k == 0)
        def _():
            acc_ref[...] = product(a_ref, b_ref)

        @pl.when(k > 0)
        def _():
            acc_ref[...] += product(a_ref, b_ref)

        @pl.when(k == nk - 1)
        def _():
            o_ref[...] = acc_ref[...].astype(o_ref.dtype)

    blk = (_nbytes(a_spec.block_shape, a.dtype) + _nbytes(b_spec.block_shape, b.dtype)
           + _nbytes(acc_shape, o_dtype) + _nbytes(acc_shape, F32))
    res, got = _call_beside(
        beside, body_once if nk == 1 else body, name, grid, [a_spec, b_spec], [o_spec],
        [jax.ShapeDtypeStruct(o_shape, o_dtype)], [] if nk == 1 else [pltpu.VMEM(acc_shape, F32)],
        _vmem_limit(blk), (a, b), ("parallel", "parallel", "arbitrary"), in_hbm=False)
    return res[0] if beside is None else (res[0], got)


def _mm_pieces(name, pieces, starts, width, step_block, other, other_spec, pieces_first, piece_rows, o_spec, o_shape,
               o_dtype, grid, dims, beside=None):
    n_p, nk = len(pieces), grid[2]
    acc_shape = tuple(s for s in o_spec.block_shape if s is not None)

    def which(i, j, k):
        blk = step_block(i, j, k)
        idx = 0
        for s in starts[1:]:
            idx = idx + (blk >= s).astype(jnp.int32)
        return idx, blk

    def piece_spec(p, t_rows):
        def index(i, j, k):
            idx, blk = which(i, j, k)
            mine = idx == p
            return jnp.where(mine, piece_rows(i, j, k), 0), jnp.where(mine, blk - starts[p], 0)
        return _spec((t_rows, width), index)

    def body(*refs):
        p_refs = refs[:n_p] if pieces_first else refs[1:1 + n_p]
        other_ref = refs[n_p] if pieces_first else refs[0]
        o_ref, acc_ref = refs[n_p + 1], refs[n_p + 2]
        i, j, k = pl.program_id(0), pl.program_id(1), pl.program_id(2)

        @pl.when(k == 0)
        def _():
            acc_ref[...] = jnp.zeros_like(acc_ref)

        idx, _ = which(i, j, k)
        for p in range(n_p):
            @pl.when(idx == p)
            def _(p=p):
                mine, fixed = p_refs[p][...].astype(BF16), other_ref[...].astype(BF16)
                pair = (mine, fixed) if pieces_first else (fixed, mine)
                acc_ref[...] += lax.dot_general(pair[0], pair[1], dims, preferred_element_type=F32)

        @pl.when(k == nk - 1)
        def _():
            o_ref[...] = acc_ref[...].astype(o_ref.dtype)

    t_rows = other_spec.block_shape[-2] if not pieces_first else o_spec.block_shape[-2]
    specs = [piece_spec(p, t_rows) for p in range(n_p)]
    in_specs = specs + [other_spec] if pieces_first else [other_spec] + specs
    operands = list(pieces) + [other] if pieces_first else [other] + list(pieces)
    blk = (n_p * 4 * t_rows * width + _nbytes(other_spec.block_shape, other.dtype)
           + _nbytes(acc_shape, o_dtype) + _nbytes(acc_shape, F32))
    res, got = _call_beside(
        beside, body, name, grid, in_specs, [o_spec], [jax.ShapeDtypeStruct(o_shape, o_dtype)],
        [pltpu.VMEM(acc_shape, F32)], _vmem_limit(blk), operands, ("parallel", "parallel", "arbitrary"), in_hbm=False)
    return res[0] if beside is None else (res[0], got)


def _swiglu_fn(gate_up):
    gate, up = gate_up[0], gate_up[1]
    return gate * jax.nn.sigmoid(gate) * up


def _ffn_in_swiglu(name, h, w, t_m):
    seq, d = h.shape
    n_half, n = w.shape[0] // 2, w.shape[2]

    def body(h_ref, wg_ref, wu_ref, a_ref, f_ref):
        hb = h_ref[...]
        a_ref[0] = lax.dot_general(hb, wg_ref[...], NN, preferred_element_type=F32).astype(a_ref.dtype)
        a_ref[1] = lax.dot_general(hb, wu_ref[...], NN, preferred_element_type=F32).astype(a_ref.dtype)
        f_ref[...] = _swiglu_fn(a_ref[...].astype(F32)).astype(f_ref.dtype)

    blk = 2 * t_m * d + 4 * d * n + 6 * t_m * n + 12 * t_m * n
    return pl.pallas_call(
        body, name=name, grid=(seq // t_m, n_half),
        in_specs=[_spec((t_m, d), lambda i, j: (i, 0)), _spec((None, d, n), lambda i, j: (j, 0, 0)),
                  _spec((None, d, n), lambda i, j: (j + n_half, 0, 0))],
        out_specs=[_spec((2, None, t_m, n), lambda i, j: (0, j, i, 0)), _spec((None, t_m, n), lambda i, j: (j, i, 0))],
        out_shape=[jax.ShapeDtypeStruct((2, n_half, seq, n), FFN_ACT), jax.ShapeDtypeStruct((n_half, seq, n), BF16)],
        compiler_params=pltpu.CompilerParams(dimension_semantics=("parallel", "parallel"),
                                             vmem_limit_bytes=_vmem_limit(blk)),
    )(h, w, w)


def _ffn_out_dx_swiglu(name, d_y, w, a, t_m):
    seq, d = d_y.shape
    n_half, n = w.shape[0], w.shape[1]

    def body(dy_ref, w_ref, a_ref, da_ref):
        d_f = lax.dot_general(dy_ref[...], w_ref[...], NT, preferred_element_type=F32)
        _, pull = jax.vjp(_swiglu_fn, a_ref[...].astype(F32))
        da_ref[...] = pull(d_f)[0].astype(da_ref.dtype)

    blk = 2 * t_m * d + 2 * d * n + 8 * t_m * n + 24 * t_m * n
    return pl.pallas_call(
        body, name=name, grid=(seq // t_m, n_half),
        in_specs=[_spec((t_m, d), lambda i, j: (i, 0)), _spec((None, n, d), lambda i, j: (j, 0, 0)),
                  _spec((2, None, t_m, n), lambda i, j: (0, j, i, 0))],
        out_specs=_spec((2, None, t_m, n), lambda i, j: (0, j, i, 0)),
        out_shape=jax.ShapeDtypeStruct((2, n_half, seq, n), BF16),
        compiler_params=pltpu.CompilerParams(dimension_semantics=("parallel", "parallel"),
                                             vmem_limit_bytes=_vmem_limit(blk)),
    )(d_y, w, a)


def _merge_fn(y_sb, y_ssm, gates):
    half = gates.shape[-1] // 2
    return jax.nn.sigmoid(gates[:, :half]) * y_sb + jax.nn.sigmoid(gates[:, half:]) * y_ssm


def _up_merge(name, o_sb, s5_out, proj, gates_cb, w_sb, w_ssm, layer, t_rows):
    seq = o_sb.shape[0]
    d = w_sb.shape[2]

    def body(o_ref, s_ref, g_ref, w1_ref, w2_ref, m_ref, y1_ref, y2_ref):
        y_sb = lax.dot_general(o_ref[...], w1_ref[...], NN, preferred_element_type=F32)
        y_ssm = lax.dot_general(s_ref[...], w2_ref[...], NN, preferred_element_type=F32)
        m_ref[...] = _merge_fn(y_sb, y_ssm, g_ref[...]).astype(m_ref.dtype)
        y1_ref[...] = y_sb.astype(y1_ref.dtype)
        y2_ref[...] = y_ssm.astype(y2_ref.dtype)

    row = lambda width: _spec((t_rows, width), lambda i: (i, 0))
    whole = lambda w: _spec((None,) + w.shape[1:], lambda i: (layer, 0, 0))
    blk = t_rows * (2 * o_sb.shape[1] + 2 * s5_out.shape[1] + 8 * d + 6 * d + 24 * d) + 4 * d * (o_sb.shape[1] + s5_out.shape[1])
    return pl.pallas_call(
        body, name=name, grid=(seq // t_rows,),
        in_specs=[row(o_sb.shape[1]), row(s5_out.shape[1]), _spec((t_rows, 2 * d), lambda i: (i, gates_cb)),
                  whole(w_sb), whole(w_ssm)],
        out_specs=[row(d)] * 3, out_shape=[jax.ShapeDtypeStruct((seq, d), BF16)] * 3,
        compiler_params=pltpu.CompilerParams(dimension_semantics=("parallel",), vmem_limit_bytes=_vmem_limit(blk)),
    )(_in_hbm(o_sb), _in_hbm(s5_out), _in_hbm(proj), w_sb, w_ssm)


def _tile(n, pref=1024):
    t = pref
    while t >= LANES:
        if n % t == 0:
            return t
        t -= LANES
    return n


def _rowwise(name, fn, ins, outs, grid):
    n_in = len(ins)

    def body(*refs):
        vals = fn(*[r[...].astype(F32) for r in refs[:n_in]])
        if not isinstance(vals, (tuple, list)):
            vals = (vals,)
        for r, v in zip(refs[n_in:], vals):
            r[...] = v.astype(r.dtype)

    blk = sum(_nbytes(bs, a.dtype) for a, bs, _ in ins) + sum(_nbytes(bs, d) + _nbytes(bs, F32) for _, d, bs, _ in outs)
    return pl.pallas_call(
        body, name=name, grid=grid,
        in_specs=[_spec(bs, im) for _, bs, im in ins],
        out_specs=[_spec(bs, im) for _, _, bs, im in outs],
        out_shape=[jax.ShapeDtypeStruct(s, d) for s, d, _, _ in outs],
        compiler_params=pltpu.CompilerParams(dimension_semantics=("parallel",) * len(grid),
                                             vmem_limit_bytes=_vmem_limit(2 * blk)),
    )(*[_in_hbm(a) for a, _, _ in ins])


def _rowwise_vjp(name, fn, ins, cts, wrt, grid):
    n_in, n_ct = len(ins), len(cts)
    idx = [w[0] for w in wrt]

    def body(*refs):
        prim = [r[...].astype(F32) for r in refs[:n_in]]
        ct = tuple(r[...].astype(F32) for r in refs[n_in:n_in + n_ct])
        o_refs = refs[n_in + n_ct:]

        def g(*sel):
            full = list(prim)
            for i, s in zip(idx, sel):
                full[i] = s
            out = fn(*full)
            return tuple(out) if isinstance(out, (tuple, list)) else (out,)

        _, pull = jax.vjp(g, *[prim[i] for i in idx])
        grads = pull(ct)
        first = pl.program_id(0) == 0
        for d in range(1, len(grid)):
            first = jnp.logical_and(first, pl.program_id(d) == 0)
        for w, o_ref, gr in zip(wrt, o_refs, grads):
            if w[1] == "row":
                o_ref[...] = gr.astype(o_ref.dtype)
            else:
                @pl.when(first)
                def _(o_ref=o_ref):
                    o_ref[...] = jnp.zeros_like(o_ref)

                o_ref[...] += gr.astype(o_ref.dtype)

    blk = (sum(_nbytes(bs, a.dtype) + _nbytes(bs, F32) for a, bs, _ in list(ins) + list(cts))
           + sum(_nbytes(w[4], w[3]) + _nbytes(w[4], F32) for w in wrt))
    return pl.pallas_call(
        body, name=name, grid=grid,
        in_specs=[_spec(bs, im) for _, bs, im in list(ins) + list(cts)],
        out_specs=[_spec(w[4], w[5]) for w in wrt],
        out_shape=[jax.ShapeDtypeStruct(w[2], w[3]) for w in wrt],
        compiler_params=pltpu.CompilerParams(dimension_semantics=("arbitrary",) * len(grid),
                                             vmem_limit_bytes=_vmem_limit(2 * blk)),
    )(*[_in_hbm(a) for a, _, _ in list(ins) + list(cts)])


def _normalize(x):
    mu = jnp.mean(x, axis=-1, keepdims=True)
    xc = x - mu
    var = jnp.mean(xc * xc, axis=-1, keepdims=True)
    return xc * lax.rsqrt(var + LN_EPS)


def _modulate(x, sc, sh):
    return _normalize(x) * (1.0 + sc) + sh


def _make_resid_fns(alpha):
    def resid_ln(x, y, gate, g, b):
        return _normalize(alpha * x + (1.0 + gate) * y) * g + b

    def resid_ln_mod(x, y, gate, g, b, sc, sh):
        xn = resid_ln(x, y, gate, g, b)
        return xn, _modulate(xn, sc, sh)

    return resid_ln, resid_ln_mod


def _s5_act_fn(yc, u, d_skip):
    return jax.nn.gelu(yc + d_skip * u)


def _s5_glu_fn(yc, u, t, d_skip, b_glu):
    return _s5_act_fn(yc, u, d_skip) * jax.nn.sigmoid(t + b_glu)


def _s5_post_fn(yc, u, t, d_skip, b_glu):
    y1 = _s5_act_fn(yc, u, d_skip)
    return y1, y1 * jax.nn.sigmoid(t + b_glu)


def _sb_tri(kind):
    row = lax.broadcasted_iota(jnp.int32, (SB_BLOCK, SB_BLOCK), 0)
    col = lax.broadcasted_iota(jnp.int32, (SB_BLOCK, SB_BLOCK), 1)
    if kind == "after":
        return (row > col).astype(BF16)
    if kind == "from":
        return (row >= col).astype(BF16)
    return col < row


def _split_dot(x, m):
    hi = x.astype(BF16)
    lo = (x - hi.astype(F32)).astype(BF16)
    return (lax.dot_general(hi, m, NN, preferred_element_type=F32)
            + lax.dot_general(lo, m, NN, preferred_element_type=F32))


def _sb_scores(qh, k2):
    z = lax.dot_general(qh, k2, NT, preferred_element_type=F32)
    log_beta = jnp.minimum(z, 0.0) - jnp.log(1.0 + jnp.exp(-jnp.abs(z)))
    return log_beta, log_beta - z


def _sb_attention_fwd(proj, sb_width, beside=None):
    seq = proj.shape[0]
    n_pair, n_q = sb_width // LANES, seq // SB_BLOCK
    scale = 1.0 / (HEAD_DIM ** 0.5)

    def body(q_ref, k_ref, v_ref, o_ref, o32_ref):
        qi = pl.program_id(1)
        q2 = q_ref[...]
        lane = lax.broadcasted_iota(jnp.int32, (SB_BLOCK, LANES), 1)
        m_after, causal = _sb_tri("after"), _sb_tri("mask")
        heads = [lane < HEAD_DIM, lane >= HEAD_DIM]
        qh = [(jnp.where(m, q2, 0.0) * scale).astype(BF16) for m in heads]

        def scores(kb, diag):
            ks = pl.multiple_of(kb * SB_BLOCK, SB_BLOCK)
            k2 = k_ref[pl.ds(ks, SB_BLOCK), :].astype(BF16)
            out = []
            for h in range(2):
                log_beta, log_1m = _sb_scores(qh[h], k2)
                if diag:
                    log_1m = jnp.where(causal, log_1m, 0.0)
                out += [log_beta + _split_dot(log_1m, m_after), jnp.sum(log_1m, axis=1, keepdims=True)]
            return tuple(out)

        def weigh(kb, sc, carry, acc, diag):
            ks = pl.multiple_of(kb * SB_BLOCK, SB_BLOCK)
            v2 = v_ref[pl.ds(ks, SB_BLOCK), :].astype(BF16)
            out = []
            for h in range(2):
                w = jnp.exp(sc[2 * h] + carry[h])
                if diag:
                    w = jnp.where(causal, w, 0.0)
                out.append(acc[h] + lax.dot_general(w.astype(BF16), v2, NN, preferred_element_type=F32))
            return tuple(out)

        zero = jnp.zeros((SB_BLOCK, LANES), F32)
        zcol = jnp.zeros((SB_BLOCK, 1), F32)
        sc = scores(qi, True)
        acc = weigh(qi, sc, (zcol, zcol), (zero, zero), True)
        carry = (sc[1], sc[3])

        def loop(st):
            kb, carry, acc = st
            sc = scores(kb, False)
            after = (carry[0] + sc[1], carry[1] + sc[3])
            done = jnp.maximum(jnp.max(after[0]), jnp.max(after[1])) < SB_UNDERFLOW
            acc = weigh(kb, sc, carry, acc, False)
            return jnp.where(done, -1, kb - 1), after, acc

        _, _, acc = lax.while_loop(lambda st: st[0] >= 0, loop, (qi - 1, carry, acc))
        out = jnp.where(heads[0], acc[0], acc[1])
        o_ref[...] = out.astype(o_ref.dtype)
        o32_ref[...] = out

    q_spec = _spec((SB_BLOCK, LANES), lambda h, i: (i, h))
    kv = [_spec((seq, LANES), lambda h, i, o=o: (0, o + h)) for o in (n_pair, 2 * n_pair)]
    o_spec = _spec((SB_BLOCK, LANES), lambda h, i: (i, h))
    return _call_beside(
        beside, body, "sb_attention_fwd", (n_pair, n_q), [q_spec] + kv, [o_spec, o_spec],
        [jax.ShapeDtypeStruct((seq, sb_width), BF16), jax.ShapeDtypeStruct((seq, sb_width), F32)], [],
        _vmem_limit(2 * seq * LANES * 4), (proj, proj, proj), ("parallel", "arbitrary"))


def _sb_attention_bwd(proj, o32, do, sb_width, beside=None):
    seq = proj.shape[0]
    n_pair, n_q = sb_width // LANES, seq // SB_BLOCK
    scale = 1.0 / (HEAD_DIM ** 0.5)

    def body(q_ref, k_ref, v_ref, o_ref, do_ref, dq_ref, dk_ref, dv_ref):
        qi = pl.program_id(1)

        @pl.when(qi == 0)
        def _():
            dk_ref[...] = jnp.zeros_like(dk_ref)
            dv_ref[...] = jnp.zeros_like(dv_ref)

        q2 = q_ref[...]
        do2 = do_ref[...].astype(F32)
        o2 = o_ref[...]
        lane = lax.broadcasted_iota(jnp.int32, (SB_BLOCK, LANES), 1)
        m_after, m_from, causal = _sb_tri("after"), _sb_tri("from"), _sb_tri("mask")
        heads = [lane < HEAD_DIM, lane >= HEAD_DIM]
        qh = [(jnp.where(m, q2, 0.0) * scale).astype(BF16) for m in heads]
        doh = [jnp.where(m, do2, 0.0) for m in heads]
        doh_b = [v.astype(BF16) for v in doh]
        total = [jnp.sum(v * o2, axis=1, keepdims=True) for v in doh]

        def scores(kb, diag):
            ks = pl.multiple_of(kb * SB_BLOCK, SB_BLOCK)
            k2 = k_ref[pl.ds(ks, SB_BLOCK), :].astype(BF16)
            v2 = v_ref[pl.ds(ks, SB_BLOCK), :].astype(BF16)
            out = []
            for h in range(2):
                log_beta, log_1m = _sb_scores(qh[h], k2)
                if diag:
                    log_1m = jnp.where(causal, log_1m, 0.0)
                out += [log_beta + _split_dot(log_1m, m_after), jnp.sum(log_1m, axis=1, keepdims=True),
                        lax.dot_general(doh_b[h], v2, NT, preferred_element_type=F32), log_beta]
            return tuple(out)

        def pull(kb, sc, carry, right, dq, diag):
            ks = pl.multiple_of(kb * SB_BLOCK, SB_BLOCK)
            k2 = k_ref[pl.ds(ks, SB_BLOCK), :].astype(BF16)
            dv_blk, dk_blk, right_out, dq_out = None, None, [], []
            for h in range(2):
                arg, _, d_w, log_beta = sc[4 * h:4 * h + 4]
                w = jnp.exp(arg + carry[h])
                if diag:
                    w = jnp.where(causal, w, 0.0)
                w_b = w.astype(BF16)
                d_arg = d_w * w_b.astype(F32)
                dv_h = lax.dot_general(w_b, doh_b[h], TN, preferred_element_type=F32)
                d_log_1m = total[h] - right[h] - _split_dot(d_arg, m_from)
                beta = jnp.exp(log_beta)
                dz = d_arg * (1.0 - beta) - beta * d_log_1m
                if diag:
                    dz = jnp.where(causal, dz, 0.0)
                dz_b = dz.astype(BF16)
                dk_h = lax.dot_general(dz_b, qh[h], TN, preferred_element_type=F32)
                dv_blk = dv_h if h == 0 else dv_blk + dv_h
                dk_blk = dk_h if h == 0 else dk_blk + dk_h
                dq_out.append(dq[h] + lax.dot_general(dz_b, k2, NN, preferred_element_type=F32))
                right_out.append(right[h] + jnp.sum(d_arg, axis=1, keepdims=True))
            dv_ref[pl.ds(ks, SB_BLOCK), :] += dv_blk
            dk_ref[pl.ds(ks, SB_BLOCK), :] += dk_blk
            return tuple(right_out), tuple(dq_out)

        zero = jnp.zeros((SB_BLOCK, LANES), F32)
        zcol = jnp.zeros((SB_BLOCK, 1), F32)
        sc = scores(qi, True)
        right, dq = pull(qi, sc, (zcol, zcol), (zcol, zcol), (zero, zero), True)
        carry = (sc[1], sc[5])

        def loop(st):
            kb, carry, right, dq = st
            sc = scores(kb, False)
            after = (carry[0] + sc[1], carry[1] + sc[5])
            done = jnp.maximum(jnp.max(after[0]), jnp.max(after[1])) < SB_UNDERFLOW
            right, dq = pull(kb, sc, carry, right, dq, False)
            return jnp.where(done, -1, kb - 1), after, right, dq

        _, _, _, dq = lax.while_loop(lambda st: st[0] >= 0, loop, (qi - 1, carry, right, dq))
        dq_ref[...] = (jnp.where(heads[0], dq[0], dq[1]) * scale).astype(dq_ref.dtype)

    q_spec = _spec((SB_BLOCK, LANES), lambda h, i: (i, h))
    kv = [_spec((seq, LANES), lambda h, i, o=o: (0, o + h)) for o in (n_pair, 2 * n_pair)]
    full = _spec((seq, LANES), lambda h, i: (0, h))
    return _call_beside(
        beside, body, "sb_attention_bwd", (n_pair, n_q), [q_spec] + kv + [q_spec, q_spec], [q_spec, full, full],
        [jax.ShapeDtypeStruct((seq, sb_width), BF16), jax.ShapeDtypeStruct((seq, sb_width), F32),
         jax.ShapeDtypeStruct((seq, sb_width), F32)], [],
        _vmem_limit(4 * seq * LANES * 4), (proj, proj, proj, o32, do), ("parallel", "arbitrary"))


def _s5_discretize(a_re, a_im, log_dt, b_re, b_im, c_re, c_im):
    n_g, n_p = a_re.shape
    c_g = b_re.shape[-1]
    ns = n_g // SLAB_GROUPS
    dt = jnp.exp(log_dt)[:, None]
    xr, xi = a_re * dt, a_im * dt
    mag = jnp.exp(xr)
    lr, li = mag * jnp.cos(xi), mag * jnp.sin(xi)
    den = a_re * a_re + a_im * a_im
    fr = ((lr - 1.0) * a_re + li * a_im) / den
    fi = (li * a_re - (lr - 1.0) * a_im) / den
    bb_re = fr[..., None] * b_re - fi[..., None] * b_im
    bb_im = fr[..., None] * b_im + fi[..., None] * b_re
    eye = jnp.eye(SLAB_GROUPS, dtype=F32)

    def diag_b(m):
        m = jnp.transpose(m.reshape(ns, SLAB_GROUPS, n_p, c_g), (0, 1, 3, 2))
        m = m[:, :, :, None, :] * eye[None, :, None, :, None]
        return m.reshape(ns, SLAB_GROUPS * c_g, SLAB_GROUPS * n_p)

    def diag_c(m):
        m = jnp.transpose(m.reshape(ns, SLAB_GROUPS, c_g, n_p), (0, 1, 3, 2))
        m = m[:, :, :, None, :] * eye[None, :, None, :, None]
        return m.reshape(ns, SLAB_GROUPS * n_p, SLAB_GROUPS * c_g)

    bs = jnp.concatenate([diag_b(bb_re), diag_b(bb_im)], axis=-1)
    cs = jnp.concatenate([diag_c(c_re), -diag_c(c_im)], axis=1)
    lam = jnp.concatenate([lr.reshape(ns, 1, -1), li.reshape(ns, 1, -1)], axis=-1)
    return bs, cs, lam


def _s5_powers(a_re, a_im, log_dt, n):
    n_g, n_p = a_re.shape
    ns = n_g // SLAB_GROUPS
    dt = jnp.exp(log_dt)[:, None]
    mag = jnp.exp(a_re * dt)
    base_r, base_i = mag * jnp.cos(a_im * dt), mag * jnp.sin(a_im * dt)
    steps = jnp.arange(1, n + 1, dtype=jnp.int32)[:, None, None]
    pr, pi = jnp.ones((n, n_g, n_p), F32), jnp.zeros((n, n_g, n_p), F32)
    for b in range(n.bit_length()):
        take = ((steps >> b) & 1) == 1
        pr, pi = (jnp.where(take, pr * base_r - pi * base_i, pr), jnp.where(take, pr * base_i + pi * base_r, pi))
        base_r, base_i = base_r * base_r - base_i * base_i, 2.0 * base_r * base_i

    def slabs(re, im):
        one = lambda m: jnp.transpose(m.reshape(n, ns, SLAB_GROUPS * n_p), (1, 0, 2))
        return jnp.concatenate([one(re), one(im)], axis=-1)

    return slabs(pr, pi), slabs(pr[::-1], -pi[::-1])


def _lanes(j):
    return slice(j * LANES, (j + 1) * LANES)


def _tile8(k):
    return pl.ds(pl.multiple_of(k * SUBLANES, SUBLANES), SUBLANES)


def _s5_interleave(dst_ref, src_ref, t_seg):
    def body(k, _):
        dst_ref[_tile8(k), :] = src_ref[pl.ds(k, SUBLANES, stride=t_seg), :]
        return 0

    lax.fori_loop(0, t_seg, body, 0, unroll=4)


def _s5_join_segments(st_ref, end_ref, car_ref, tab_ref, row, order, n_pair):
    for j in range(n_pair):
        re, im = _lanes(j), _lanes(n_pair + j)
        cr, ci = st_ref[:, re], st_ref[:, im]
        tr, ti = tab_ref[row:row + 1, re], tab_ref[row:row + 1, im]
        for s in order:
            car_ref[s:s + 1, re] = cr
            car_ref[s:s + 1, im] = ci
            er, ei = end_ref[s:s + 1, re], end_ref[s:s + 1, im]
            cr, ci = er + tr * cr - ti * ci, ei + tr * ci + ti * cr
        st_ref[:, re] = cr
        st_ref[:, im] = ci


def _s5_add_carries(buf_ref, car_ref, tab_ref, t_seg, n_pair):
    def fix(k, _):
        rows = _tile8(k)
        tab = tab_ref[pl.ds(k, 1), :]
        for j in range(n_pair):
            re, im = _lanes(j), _lanes(n_pair + j)
            cr, ci = car_ref[:, re], car_ref[:, im]
            tr, ti = tab[:, re], tab[:, im]
            buf_ref[rows, re] += tr * cr - ti * ci
            buf_ref[rows, im] += tr * ci + ti * cr
        return 0

    lax.fori_loop(0, t_seg, fix, 0, unroll=2)


def _s5_scan_fwd(proj, u_col, bs, cs, lam, pw, t_blk, beside=None):
    seq = proj.shape[0]
    ns, _, w2 = bs.shape
    n_pair = w2 // (2 * LANES)
    t_seg, n_t = t_blk // SUBLANES, seq // t_blk

    def body(u_ref, bs_ref, cs_ref, lam_ref, pw_ref, yc_ref, h_ref, st_ref, end_ref, car_ref, ui_ref, bu_ref, yi_ref):
        @pl.when(pl.program_id(1) == 0)
        def _():
            st_ref[...] = jnp.zeros_like(st_ref)

        _s5_interleave(ui_ref, u_ref, t_seg)
        bu_ref[...] = lax.dot_general(ui_ref[...].astype(BF16), bs_ref[...], NN, preferred_element_type=F32)
        lam_r = [jnp.broadcast_to(lam_ref[:, _lanes(j)], (SUBLANES, LANES)) for j in range(n_pair)]
        lam_i = [jnp.broadcast_to(lam_ref[:, _lanes(n_pair + j)], (SUBLANES, LANES)) for j in range(n_pair)]

        def step(k, c):
            rows = _tile8(k)
            out = []
            for j in range(n_pair):
                hr, hi = c[2 * j], c[2 * j + 1]
                nr = lam_r[j] * hr - lam_i[j] * hi + bu_ref[rows, _lanes(j)]
                ni = lam_i[j] * hr + lam_r[j] * hi + bu_ref[rows, _lanes(n_pair + j)]
                h_ref[rows, _lanes(j)] = nr
                h_ref[rows, _lanes(n_pair + j)] = ni
                out += [nr, ni]
            return tuple(out)

        ends = lax.fori_loop(0, t_seg, step, (jnp.zeros((SUBLANES, LANES), F32),) * (2 * n_pair), unroll=4)
        for j in range(n_pair):
            end_ref[:, _lanes(j)] = ends[2 * j]
            end_ref[:, _lanes(n_pair + j)] = ends[2 * j + 1]
        _s5_join_segments(st_ref, end_ref, car_ref, pw_ref, t_seg - 1, list(range(SUBLANES)), n_pair)
        _s5_add_carries(h_ref, car_ref, pw_ref, t_seg, n_pair)
        yi_ref[...] = lax.dot_general(h_ref[...].astype(BF16), cs_ref[...], NN, preferred_element_type=F32)

        def scatter(k, _):
            yc_ref[pl.ds(k, SUBLANES, stride=t_seg), :] = yi_ref[_tile8(k), :]
            return 0

        lax.fori_loop(0, t_seg, scatter, 0, unroll=4)

    return _call_beside(
        beside, body, "s5_scan_fwd", (ns, n_t),
        [_spec((t_blk, LANES), lambda s, i: (i, u_col + s)),
         _spec((None, LANES, w2), lambda s, i: (s, 0, 0)),
         _spec((None, w2, LANES), lambda s, i: (s, 0, 0)),
         _spec((None, 1, w2), lambda s, i: (s, 0, 0)),
         _spec((None, t_seg, w2), lambda s, i: (s, 0, 0))],
        [_spec((t_blk, LANES), lambda s, i: (i, s)),
         _spec((None, t_blk, w2), lambda s, i: (s, i, 0))],
        [jax.ShapeDtypeStruct((seq, ns * LANES), F32), jax.ShapeDtypeStruct((ns, seq, w2), F32)],
        [pltpu.VMEM((1, w2), F32), pltpu.VMEM((SUBLANES, w2), F32), pltpu.VMEM((SUBLANES, w2), F32),
         pltpu.VMEM((t_blk, LANES), F32), pltpu.VMEM((t_blk, w2), F32), pltpu.VMEM((t_blk, LANES), F32)],
        _vmem_limit(3 * t_blk * w2 * 4), (proj, bs, cs, lam, pw), ("parallel", "arbitrary"))


def _s5_scan_bwd(proj, u_col, states, d_yc, du_extra, bs, cs, lam, qw, t_blk):
    seq = proj.shape[0]
    ns, _, w2 = bs.shape
    n_pair = w2 // (2 * LANES)
    t_seg, n_t = t_blk // SUBLANES, seq // t_blk

    def body(u_ref, h_ref, hp_ref, dyc_ref, dux_ref, bs_ref, cs_ref, lam_ref, qw_ref,
             du_ref, dbs_ref, dcs_ref, dlam_ref, g_ref, gd_ref, st_ref, end_ref, car_ref, ui_ref, dyi_ref, dui_ref):
        i = pl.program_id(1)

        @pl.when(i == 0)
        def _():
            st_ref[...] = jnp.zeros_like(st_ref)
            dbs_ref[...] = jnp.zeros_like(dbs_ref)
            dcs_ref[...] = jnp.zeros_like(dcs_ref)
            dlam_ref[...] = jnp.zeros_like(dlam_ref)

        _s5_interleave(ui_ref, u_ref, t_seg)
        _s5_interleave(dyi_ref, dyc_ref, t_seg)
        dyc_b = dyi_ref[...].astype(BF16)
        gd_ref[...] = lax.dot_general(dyc_b, cs_ref[...], NT, preferred_element_type=F32)
        lam_r = [jnp.broadcast_to(lam_ref[:, _lanes(j)], (SUBLANES, LANES)) for j in range(n_pair)]
        lam_i = [jnp.broadcast_to(lam_ref[:, _lanes(n_pair + j)], (SUBLANES, LANES)) for j in range(n_pair)]

        def step(kk, c):
            rows = _tile8(t_seg - 1 - kk)
            out = []
            for j in range(n_pair):
                gr_n, gi_n = c[2 * j], c[2 * j + 1]
                gr = gd_ref[rows, _lanes(j)] + lam_r[j] * gr_n + lam_i[j] * gi_n
                gi = gd_ref[rows, _lanes(n_pair + j)] + lam_r[j] * gi_n - lam_i[j] * gr_n
                g_ref[rows, _lanes(j)] = gr
                g_ref[rows, _lanes(n_pair + j)] = gi
                out += [gr, gi]
            return tuple(out)

        zero = jnp.zeros((SUBLANES, LANES), F32)
        firsts = lax.fori_loop(0, t_seg, step, (zero,) * (2 * n_pair), unroll=4)
        for j in range(n_pair):
            end_ref[:, _lanes(j)] = firsts[2 * j]
            end_ref[:, _lanes(n_pair + j)] = firsts[2 * j + 1]
        _s5_join_segments(st_ref, end_ref, car_ref, qw_ref, 0, list(range(SUBLANES))[::-1], n_pair)
        _s5_add_carries(g_ref, car_ref, qw_ref, t_seg, n_pair)

        def pair_up(k, c):
            rows, prev = _tile8(k), _tile8(k - 1)
            out = []
            for j in range(n_pair):
                re, im = _lanes(j), _lanes(n_pair + j)
                gr, gi, hr, hi = g_ref[rows, re], g_ref[rows, im], h_ref[prev, re], h_ref[prev, im]
                out += [c[2 * j] + gr * hr + gi * hi, c[2 * j + 1] + gi * hr - gr * hi]
            return tuple(out)

        acc = lax.fori_loop(1, t_seg, pair_up, (zero,) * (2 * n_pair), unroll=4)
        has_prev = (i < n_t - 1).astype(F32)
        first_seg = lax.broadcasted_iota(jnp.int32, (SUBLANES, LANES), 0) == 0
        last = _tile8(t_seg - 1)
        for j in range(n_pair):
            re, im = _lanes(j), _lanes(n_pair + j)
            gr, gi = g_ref[0:SUBLANES, re], g_ref[0:SUBLANES, im]
            hr = jnp.where(first_seg, hp_ref[SUBLANES - 1:, re] * has_prev, pltpu.roll(h_ref[last, re], 1, 0))
            hi = jnp.where(first_seg, hp_ref[SUBLANES - 1:, im] * has_prev, pltpu.roll(h_ref[last, im], 1, 0))
            dlam_ref[:, re] += jnp.sum(acc[2 * j] + gr * hr + gi * hi, axis=0, keepdims=True)
            dlam_ref[:, im] += jnp.sum(acc[2 * j + 1] + gi * hr - gr * hi, axis=0, keepdims=True)

        g_b = g_ref[...].astype(BF16)
        dui_ref[...] = lax.dot_general(g_b, bs_ref[...], NT, preferred_element_type=F32)
        dbs_ref[...] += lax.dot_general(ui_ref[...].astype(BF16), g_b, TN, preferred_element_type=F32)
        dcs_ref[...] += lax.dot_general(h_ref[...].astype(BF16), dyc_b, TN, preferred_element_type=F32)

        def scatter(k, _):
            rows = pl.ds(k, SUBLANES, stride=t_seg)
            du_ref[rows, :] = (dui_ref[_tile8(k), :] + dux_ref[rows, :]).astype(du_ref.dtype)
            return 0

        lax.fori_loop(0, t_seg, scatter, 0, unroll=4)

    rev = lambda i: n_t - 1 - i
    return pl.pallas_call(
        body, name="s5_scan_bwd", grid=(ns, n_t),
        in_specs=[_spec((t_blk, LANES), lambda s, i: (rev(i), u_col + s)),
                  _spec((None, t_blk, w2), lambda s, i: (s, rev(i), 0)),
                  _spec((None, SUBLANES, w2), lambda s, i: (s, jnp.maximum(rev(i) * t_seg - 1, 0), 0)),
                  _spec((t_blk, LANES), lambda s, i: (rev(i), s)),
                  _spec((t_blk, LANES), lambda s, i: (rev(i), s)),
                  _spec((None, LANES, w2), lambda s, i: (s, 0, 0)),
                  _spec((None, w2, LANES), lambda s, i: (s, 0, 0)),
                  _spec((None, 1, w2), lambda s, i: (s, 0, 0)),
                  _spec((None, t_seg, w2), lambda s, i: (s, 0, 0))],
        out_specs=[_spec((t_blk, LANES), lambda s, i: (rev(i), s)),
                   _spec((None, LANES, w2), lambda s, i: (s, 0, 0)),
                   _spec((None, w2, LANES), lambda s, i: (s, 0, 0)),
                   _spec((None, 1, w2), lambda s, i: (s, 0, 0))],
        out_shape=[jax.ShapeDtypeStruct((seq, ns * LANES), F32), jax.ShapeDtypeStruct(bs.shape, F32),
                   jax.ShapeDtypeStruct(cs.shape, F32), jax.ShapeDtypeStruct(lam.shape, F32)],
        scratch_shapes=[pltpu.VMEM((t_blk, w2), F32), pltpu.VMEM((t_blk, w2), F32), pltpu.VMEM((1, w2), F32),
                        pltpu.VMEM((SUBLANES, w2), F32), pltpu.VMEM((SUBLANES, w2), F32),
                        pltpu.VMEM((t_blk, LANES), F32), pltpu.VMEM((t_blk, LANES), F32), pltpu.VMEM((t_blk, LANES), F32)],
        compiler_params=pltpu.CompilerParams(dimension_semantics=("parallel", "arbitrary"),
                                             vmem_limit_bytes=_vmem_limit(5 * t_blk * w2 * 4)),
    )(*[_in_hbm(a) for a in (proj, states, states, d_yc, du_extra, bs, cs, lam, qw)])


def _loss_head(y, target, t_m):
    seq, d = y.shape

    def body(y_ref, t_ref, loss_ref, dy_ref):
        @pl.when(pl.program_id(0) == 0)
        def _():
            loss_ref[...] = jnp.zeros_like(loss_ref)

        diff = y_ref[...] - t_ref[...]
        dy_ref[...] = diff / d
        loss_ref[...] += 0.5 * jnp.sum(diff * diff) / d

    row = _spec((t_m, d), lambda i: (i, 0))
    return pl.pallas_call(
        body, name="loss_head", grid=(seq // t_m,), in_specs=[row, row],
        out_specs=[_spec((SUBLANES, LANES), lambda i: (0, 0)), row],
        out_shape=[jax.ShapeDtypeStruct((SUBLANES, LANES), F32), jax.ShapeDtypeStruct((seq, d), F32)],
        compiler_params=pltpu.CompilerParams(dimension_semantics=("arbitrary",),
                                             vmem_limit_bytes=_vmem_limit(6 * t_m * d * 4)),
    )(_in_hbm(y), _in_hbm(target))


def _adamw_fn(w, m, v, *partials):
    g = partials[0]
    for p in partials[1:]:
        g = g + p
    m2 = ADAM_B1 * m + (1.0 - ADAM_B1) * g
    v2 = ADAM_B2 * v + (1.0 - ADAM_B2) * (g * g)
    m_hat = m2 / (1.0 - ADAM_B1 ** ADAM_STEP)
    v_hat = v2 / (1.0 - ADAM_B2 ** ADAM_STEP)
    delta = -ADAM_LR * (m_hat / (jnp.sqrt(v_hat) + ADAM_EPS) + ADAM_WD * w)
    return g, delta, m2, v2


def _adamw(name, w, m, v, partials):
    rows, cols = w.shape
    t_r = rows
    for cand in (512, 256, 128, 64, 32, 16, 8):
        if rows % cand == 0 and cand * cols * 4 <= (1 << 20):
            t_r = cand
            break
    n_p = partials.shape[0]
    row = lambda i: (i, 0)
    ins = [(a, (t_r, cols), row) for a in (w, m, v)]
    ins += [(partials, (None, t_r, cols), (lambda i, j=j: (j, i, 0))) for j in range(n_p)]
    outs = [((rows, cols), F32, (t_r, cols), row)] * 4
    return _rowwise(name, _adamw_fn, ins, outs, (rows // t_r,))


SMALL_PARAMS = ("b_ada", "ssm_a_re", "ssm_a_im", "ssm_log_dt", "ssm_b_re", "ssm_b_im", "ssm_c_re", "ssm_c_im",
                "ssm_d", "b_glu", "ln1_g", "ln1_b", "ln2_g", "ln2_b")
WEIGHTS = ("w_ada", "b_ada", "w_in", "w_sb_up", "ssm_a_re", "ssm_a_im", "ssm_log_dt", "ssm_b_re", "ssm_b_im",
           "ssm_c_re", "ssm_c_im", "ssm_d", "w_glu", "b_glu", "w_ssm_up", "w_out", "ln1_g", "ln1_b", "w_ffn_in",
           "w_ffn_out", "ln2_g", "ln2_b")
ARG_NAMES = (("x", "c") + WEIGHTS + ("loss_target",) + tuple("m_" + n for n in WEIGHTS)
             + tuple("v_" + n for n in WEIGHTS))


def _pack(arrs):
    flat = jnp.concatenate([a.reshape(-1) for a in arrs])
    pad = (-flat.shape[0]) % (PACK_ROWS * LANES)
    return jnp.pad(flat, (0, pad)).reshape(-1, LANES)


def _unpack(packed, like):
    lead = packed.shape[:-2]
    flat = packed.reshape(lead + (-1,))
    out, off = [], 0
    for a in like:
        out.append(flat[..., off:off + a.size].reshape(lead + a.shape))
        off += a.size
    return out


def kernel(x, c, w_ada, b_ada, w_in, w_sb_up, ssm_a_re, ssm_a_im, ssm_log_dt, ssm_b_re, ssm_b_im, ssm_c_re,
           ssm_c_im, ssm_d, w_glu, b_glu, w_ssm_up, w_out, ln1_g, ln1_b, w_ffn_in, w_ffn_out, ln2_g, ln2_b,
           loss_target, m_w_ada, m_b_ada, m_w_in, m_w_sb_up, m_ssm_a_re, m_ssm_a_im, m_ssm_log_dt, m_ssm_b_re,
           m_ssm_b_im, m_ssm_c_re, m_ssm_c_im, m_ssm_d, m_w_glu, m_b_glu, m_w_ssm_up, m_w_out, m_ln1_g, m_ln1_b,
           m_w_ffn_in, m_w_ffn_out, m_ln2_g, m_ln2_b, v_w_ada, v_b_ada, v_w_in, v_w_sb_up, v_ssm_a_re, v_ssm_a_im,
           v_ssm_log_dt, v_ssm_b_re, v_ssm_b_im, v_ssm_c_re, v_ssm_c_im, v_ssm_d, v_w_glu, v_b_glu, v_w_ssm_up,
           v_w_out, v_ln1_g, v_ln1_b, v_w_ffn_in, v_w_ffn_out, v_ln2_g, v_ln2_b):
    given = locals()
    return _train_step({n: given[n] for n in ARG_NAMES})


def _train_step(p):
    x0 = p["x"][0]
    target = p["loss_target"][0]
    seq, d = x0.shape
    depth = p["w_ada"].shape[0]
    n_ada = p["w_ada"].shape[2]
    n_in = p["w_in"].shape[2]
    sb_w = p["w_sb_up"].shape[1]
    ssm_w = p["w_ssm_up"].shape[1]
    n_up = p["w_sb_up"].shape[2]
    n_ffn = p["w_ffn_in"].shape[2]
    ffn = N_DEV * p["w_ffn_out"].shape[1]
    in_cols = N_DEV * n_in
    alpha = (2 * depth) ** 0.25
    resid_ln, resid_ln_mod = _make_resid_fns(alpha)
    t_r = min(512, seq)
    n_r = seq // t_r
    t_m = min(1024, seq)
    n_m = seq // t_m
    t_d = _tile(d)
    assert n_ffn * (N_DEV // 2) == ffn and sb_w % LANES == 0 and ssm_w % LANES == 0 and d % LANES == 0
    assert n_in % LANES == 0 and n_up % LANES == 0 and seq % t_m == 0 and in_cols == 3 * sb_w + ssm_w + 2 * d
    assert (3 * sb_w) % ssm_w == 0 and (3 * sb_w + ssm_w) % (2 * d) == 0
    assert sb_w % n_in == 0 and ssm_w % n_in == 0 and d % n_in == 0
    proj_starts = [c // n_in for c in (0, sb_w, 2 * sb_w, 3 * sb_w, 3 * sb_w + ssm_w)]

    bf = lambda a: a.astype(BF16)
    got = _exchange("gather_first", [], [bf(p["w_in"][0]), p["c"]])
    wg_in = [got[0]] + [None] * (depth - 1)
    c_all = got[1].reshape(N_DEV, d)
    small_names = ("w_sb_up", "w_ssm_up", "w_glu", "w_out")
    wg_ffn_in, wg_ffn_out, wg = [None] * depth, [None] * depth, {}

    c_pad = jnp.pad(c_all, ((0, 2 * SUBLANES - N_DEV), (0, 0)))
    c_act = _rowwise("silu_c", lambda v: v * jax.nn.sigmoid(v), [(c_pad, c_pad.shape, lambda i: (0, 0))],
                     [(c_pad.shape, F32, c_pad.shape, lambda i: (0, 0))], (1,))[0]
    rows_c = c_pad.shape[0]
    mod_cols = [
        _mm(f"mod_{l}", c_act, p["w_ada"],
            _spec((rows_c, d), lambda i, j, k: (0, 0)), _spec((None, d, n_ada), lambda i, j, k, l=l: (l, 0, 0)),
            _spec((rows_c, n_ada), lambda i, j, k: (0, 0)), (rows_c, n_ada), F32, (1, 1, 1), NN)
        for l in range(depth)]
    mod_send = jnp.stack([m[:N_DEV] for m in mod_cols], axis=1)
    mod_recv = _exchange("exchange_mod", [mod_send], [])[0]
    mod_nobias = jnp.swapaxes(mod_recv, 0, 1).reshape(depth, N_DEV * n_ada)
    full2 = lambda a: (a, a.shape, lambda i: (0, 0))
    mod = _rowwise("mod_bias", lambda a, b: a + b, [full2(mod_nobias), full2(p["b_ada"])],
                   [(mod_nobias.shape, F32, mod_nobias.shape, lambda i: (0, 0))], (1,))[0]
    vec = lambda a: a.reshape(1, -1)
    mods = [[vec(mod[l, j * d:(j + 1) * d]) for j in range(6)] for l in range(depth)]
    ln = {n: [vec(p[n][l]) for l in range(depth)] for n in ("ln1_g", "ln1_b", "ln2_g", "ln2_b")}

    row_spec = lambda width: ((t_r, width), lambda i: (i, 0))
    col_spec = lambda width, cb: ((t_r, width), lambda i, cb=cb: (i, cb))
    vec_spec = lambda width: ((1, width), lambda i: (0, 0))
    rows_in = lambda a: (a,) + row_spec(a.shape[1])
    vec_in = lambda a: (a,) + vec_spec(a.shape[1])
    row_out = lambda width, dt: ((seq, width), dt) + row_spec(width)

    s5 = [_s5_discretize(*[p[n][l] for n in ("ssm_a_re", "ssm_a_im", "ssm_log_dt", "ssm_b_re", "ssm_b_im",
                                               "ssm_c_re", "ssm_c_im")]) for l in range(depth)]
    s5_b16 = [(bs.astype(BF16), cs.astype(BF16), lam) for bs, cs, lam in s5]
    t_scan = min(512, seq)
    s5_pw = [_s5_powers(p["ssm_a_re"][l], p["ssm_a_im"][l], p["ssm_log_dt"][l], t_scan // SUBLANES)
             for l in range(depth)]
    u_col = 3 * sb_w // LANES
    gates_cb = (3 * sb_w + ssm_w) // (2 * d)
    ssm_d = [vec(p["ssm_d"][l]) for l in range(depth)]
    b_glu = [vec(p["b_glu"][l]) for l in range(depth)]
    n_half = N_DEV // 2

    h = _rowwise("modulate_in", _modulate, [rows_in(x0), vec_in(mods[0][1]), vec_in(mods[0][0])],
                 [row_out(d, BF16)], (n_r,))[0]
    saved = []
    x_cur = x0
    for l in range(depth):
        sv = {"x_in": x_cur, "h": h}
        last = l == depth - 1
        t_n = _tile(n_in)
        r_n = n_in // t_n
        proj = _mm(f"proj_{l}", h, wg_in[l],
                   _spec((t_m, d), lambda i, j, k: (i, 0)),
                   _spec((None, d, t_n), lambda i, j, k, r=r_n: (j // r, 0, j % r)),
                   _spec((t_m, t_n), lambda i, j, k: (i, j)), (seq, in_cols), F32, (n_m, N_DEV * r_n, 1), NN)
        arriving = [bf(p["w_ffn_in"][l]), bf(p["w_ffn_out"][l])] + ([bf(p[n]) for n in small_names] if l == 0 else [])
        (o_sb, o_sb32), got = _sb_attention_fwd(proj, sb_w, beside=_Exchange(gather=arriving))
        wg_ffn_in[l] = got[0]
        wg_ffn_out[l] = got[1].reshape(n_half, n_ffn, d)
        if l == 0:
            wg = dict(zip(small_names, got[2:]))
            for n in ("w_glu", "w_out"):
                wg[n] = jnp.swapaxes(wg[n], 0, 1).reshape(depth, -1, wg[n].shape[-1])
            for n in ("w_sb_up", "w_ssm_up"):
                wg[n] = jnp.transpose(wg[n], (1, 2, 0, 3)).reshape(depth, wg[n].shape[2], d)
        bs16, cs16, lam = s5_b16[l]
        (yc, states), got = _s5_scan_fwd(proj, u_col, bs16, cs16, lam, s5_pw[l][0], t_scan,
                                         beside=None if last else _Exchange(gather=[bf(p["w_in"][l + 1])]))
        if not last:
            wg_in[l + 1] = got[0]
        u_in = (proj,) + col_spec(ssm_w, 3 * sb_w // ssm_w)
        y1 = _rowwise(f"s5_act_{l}", _s5_act_fn, [rows_in(yc), u_in, vec_in(ssm_d[l])],
                      [row_out(ssm_w, BF16)], (n_r,))[0]
        t_glu = _mm(f"s5_glu_mm_{l}", y1, wg["w_glu"],
                    _spec((t_m, ssm_w), lambda i, j, k: (i, 0)), _spec((None, ssm_w, ssm_w), lambda i, j, k, l=l: (l, 0, 0)),
                    _spec((t_m, ssm_w), lambda i, j, k: (i, 0)), (seq, ssm_w), F32, (n_m, 1, 1), NN)
        s5_out = _rowwise(f"s5_glu_{l}", _s5_glu_fn,
                          [rows_in(yc), u_in, rows_in(t_glu), vec_in(ssm_d[l]), vec_in(b_glu[l])],
                          [row_out(ssm_w, BF16)], (n_r,))[0]

        merged, y_sb, y_ssm = _up_merge(f"up_merge_{l}", o_sb, s5_out, proj, gates_cb, wg["w_sb_up"], wg["w_ssm_up"],
                                        l, t_r)
        y_mix = _mm(f"out_proj_{l}", merged, wg["w_out"],
                    _spec((t_m, d), lambda i, j, k: (i, 0)), _spec((None, d, t_d), lambda i, j, k, l=l: (l, 0, j)),
                    _spec((t_m, t_d), lambda i, j, k: (i, j)), (seq, d), F32, (n_m, d // t_d, 1), NN)
        vecs_a = [mods[l][2], ln["ln1_g"][l], ln["ln1_b"][l], mods[l][4], mods[l][3]]
        x_mid, h2 = _rowwise(f"resid_mix_{l}", resid_ln_mod, [rows_in(x_cur), rows_in(y_mix)] + [vec_in(v) for v in vecs_a],
                             [row_out(d, F32), row_out(d, BF16)], (n_r,))
        a_ffn, f_act = _ffn_in_swiglu(f"ffn_in_{l}", h2, wg_ffn_in[l], t_r)
        y_ffn = _mm(f"ffn_out_{l}", f_act, wg_ffn_out[l],
                    _spec((None, t_m, n_ffn), lambda i, j, k: (k, i, 0)),
                    _spec((None, n_ffn, t_d), lambda i, j, k: (k, 0, j)),
                    _spec((t_m, t_d), lambda i, j, k: (i, j)), (seq, d), F32, (n_m, d // t_d, n_half), NN)
        vecs_b = [mods[l][5], ln["ln2_g"][l], ln["ln2_b"][l]] + ([] if last else [mods[l + 1][1], mods[l + 1][0]])
        outs_b = [row_out(d, F32)] + ([] if last else [row_out(d, BF16)])
        res = _rowwise(f"resid_ffn_{l}", resid_ln if last else resid_ln_mod,
                       [rows_in(x_mid), rows_in(y_ffn)] + [vec_in(v) for v in vecs_b], outs_b, (n_r,))
        sv.update(proj=proj, o_sb=o_sb, o_sb32=o_sb32, yc=yc, states=states, y1=y1, t_glu=t_glu, s5_out=s5_out,
                  y_sb=y_sb, y_ssm=y_ssm, merged=merged, y_mix=y_mix, x_mid=x_mid, h2=h2, a_ffn=a_ffn, f_act=f_act,
                  y_ffn=y_ffn, vecs_a=vecs_a, vecs_b=vecs_b)
        saved.append(sv)
        x_cur = res[0]
        h = None if last else res[1]

    loss_part, d_x = _loss_head(x_cur, target, t_r)
    loss = lax.psum(loss_part[0, 0], MESH_AXES)

    d_h_next = None
    grads = {n: [None] * depth for n in WEIGHTS}
    d_mod = [[None] * 6 for _ in range(depth)]
    land = {}
    waiting = []
    row_wrt = lambda i, width, dt: (i, "row", (seq, width), dt) + row_spec(width)
    sum_wrt = lambda i, width: (i, "sum", (1, width), F32) + vec_spec(width)
    for l in reversed(range(depth)):
        sv = saved[l]
        last = l == depth - 1
        ins_b = [rows_in(sv["x_mid"]), rows_in(sv["y_ffn"])] + [vec_in(v) for v in sv["vecs_b"]]
        cts_b = [rows_in(d_x)] + ([] if last else [rows_in(d_h_next)])
        wrt_b = [row_wrt(0, d, F32), row_wrt(1, d, BF16)] + [sum_wrt(2 + j, d) for j in range(len(sv["vecs_b"]))]
        res = _rowwise_vjp(f"resid_ffn_bwd_{l}", resid_ln if last else resid_ln_mod, ins_b, cts_b, wrt_b, (n_r,))
        d_x_mid, d_y_ffn = res[0], res[1]
        d_mod[l][5], grads["ln2_g"][l], grads["ln2_b"][l] = res[2], res[3], res[4]
        if not last:
            d_mod[l + 1][1], d_mod[l + 1][0] = res[5], res[6]
        d_a = _ffn_out_dx_swiglu(f"ffn_out_dx_{l}", d_y_ffn, wg_ffn_out[l], sv["a_ffn"], t_r).reshape(N_DEV, seq, n_ffn)
        g_ffn_out = _mm(f"ffn_out_dw_{l}", sv["f_act"], d_y_ffn,
                        _spec((None, t_m, n_ffn), lambda i, j, k: (i, k, 0)), _spec((t_m, t_d), lambda i, j, k: (k, j)),
                        _spec((None, n_ffn, t_d), lambda i, j, k: (i, 0, j)), (n_half, n_ffn, d), GRAD_WIRE,
                        (n_half, d // t_d, n_m), TN, reread=(False, True))
        d_h2 = _mm(f"ffn_in_dx_{l}", d_a, wg_ffn_in[l],
                   _spec((None, t_m, n_ffn), lambda i, j, k: (k, i, 0)),
                   _spec((None, t_d, n_ffn), lambda i, j, k: (k, j, 0)),
                   _spec((t_m, t_d), lambda i, j, k: (i, j)), (seq, d), F32, (n_m, d // t_d, N_DEV), NT)
        g_ffn_in = _mm(f"ffn_in_dw_{l}", sv["h2"], d_a,
                       _spec((t_m, t_d), lambda i, j, k: (k, j)), _spec((None, t_m, n_ffn), lambda i, j, k: (i, k, 0)),
                       _spec((None, t_d, n_ffn), lambda i, j, k: (i, j, 0)), (N_DEV, d, n_ffn), GRAD_WIRE,
                       (N_DEV, d // t_d, n_m), TN, reread=(True, False))
        ins_a = [rows_in(sv["x_in"]), rows_in(sv["y_mix"])] + [vec_in(v) for v in sv["vecs_a"]]
        wrt_a = [row_wrt(0, d, F32), row_wrt(1, d, BF16)] + [sum_wrt(2 + j, d) for j in range(5)]
        res = _rowwise_vjp(f"resid_mix_bwd_{l}", resid_ln_mod, ins_a, [rows_in(d_x_mid), rows_in(d_h2)], wrt_a, (n_r,))
        d_x_in, d_y_mix = res[0], res[1]
        d_mod[l][2], grads["ln1_g"][l], grads["ln1_b"][l], d_mod[l][4], d_mod[l][3] = res[2:7]
        d_merged = _mm(f"out_proj_dx_{l}", d_y_mix, wg["w_out"],
                       _spec((t_m, d), lambda i, j, k: (i, 0)), _spec((None, t_d, d), lambda i, j, k, l=l: (l, j, 0)),
                       _spec((t_m, t_d), lambda i, j, k: (i, j)), (seq, d), F32, (n_m, d // t_d, 1), NT)
        g_out = _mm(f"out_proj_dw_{l}", sv["merged"], d_y_mix,
                    _spec((t_m, t_d), lambda i, j, k: (k, i)), _spec((t_m, t_d), lambda i, j, k: (k, j)),
                    _spec((t_d, t_d), lambda i, j, k: (i, j)), (d, d), GRAD_WIRE, (d // t_d, d // t_d, n_m), TN, reread=(d > t_d, d > t_d))
        gates = (sv["proj"],) + col_spec(2 * d, gates_cb)
        d_y_sb, d_y_ssm, d_gates = _rowwise_vjp(
            f"merge_bwd_{l}", _merge_fn, [rows_in(sv["y_sb"]), rows_in(sv["y_ssm"]), gates], [rows_in(d_merged)],
            [row_wrt(0, d, BF16), row_wrt(1, d, BF16), row_wrt(2, 2 * d, BF16)], (n_r,))

        def up_bwd(name, act, d_y, w, dx_dtype, l=l):
            k_w = act.shape[1]
            dx = _mm(name + "_dx", d_y, w, _spec((t_m, d), lambda i, j, k: (i, 0)),
                     _spec((None, k_w, d), lambda i, j, k: (l, 0, 0)),
                     _spec((t_m, k_w), lambda i, j, k: (i, 0)), (seq, k_w), dx_dtype, (n_m, 1, 1), NT)
            dw = _mm(name + "_dw", act, d_y, _spec((t_m, k_w), lambda i, j, k: (k, 0)),
                     _spec((t_m, t_d), lambda i, j, k: (k, j)),
                     _spec((k_w, t_d), lambda i, j, k: (0, j)), (k_w, d), GRAD_WIRE, (1, d // t_d, n_m), TN,
                     reread=(d > t_d, False))
            return dx, jnp.swapaxes(dw.reshape(k_w, N_DEV, n_up), 0, 1)

        d_o_sb, g_sb_up = up_bwd(f"sb_up_{l}", sv["o_sb"], d_y_sb, wg["w_sb_up"], BF16)
        d_s5_out, g_ssm_up = up_bwd(f"ssm_up_{l}", sv["s5_out"], d_y_ssm, wg["w_ssm_up"], F32)
        waiting += [("w_ffn_in", g_ffn_in), ("w_ffn_out", g_ffn_out.reshape(N_DEV, -1, d)),
                    ("w_out", g_out.reshape(N_DEV, -1, d)), ("w_sb_up", g_sb_up), ("w_ssm_up", g_ssm_up)]
        levels = [l + 1] * (len(waiting) - 5) + [l] * 5
        (d_q, d_k, d_v), got = _sb_attention_bwd(
            sv["proj"], sv["o_sb32"], d_o_sb, sb_w,
            beside=_Exchange(layered=[(g, lv, depth, land.get(n)) for (n, g), lv in zip(waiting, levels)]))
        land.update({n: buf for (n, _), buf in zip(waiting, got)})
        u_in = (sv["proj"],) + col_spec(ssm_w, 3 * sb_w // ssm_w)
        ins_s5 = [rows_in(sv["yc"]), u_in, rows_in(sv["t_glu"]), vec_in(ssm_d[l]), vec_in(b_glu[l])]
        d_t = _rowwise_vjp(f"s5_glu_bwd_{l}", _s5_glu_fn, ins_s5, [rows_in(d_s5_out)],
                           [row_wrt(2, ssm_w, BF16)], (n_r,))[0]
        d_y1 = _mm(f"s5_glu_mm_dx_{l}", d_t, wg["w_glu"],
                   _spec((t_m, ssm_w), lambda i, j, k: (i, 0)), _spec((None, ssm_w, ssm_w), lambda i, j, k, l=l: (l, 0, 0)),
                   _spec((t_m, ssm_w), lambda i, j, k: (i, 0)), (seq, ssm_w), F32, (n_m, 1, 1), NT)
        g_glu = _mm(f"s5_glu_mm_dw_{l}", sv["y1"], d_t,
                    _spec((t_m, ssm_w), lambda i, j, k: (k, 0)), _spec((t_m, ssm_w), lambda i, j, k: (k, 0)),
                    _spec((ssm_w, ssm_w), lambda i, j, k: (0, 0)), (ssm_w, ssm_w), GRAD_WIRE, (1, 1, n_m), TN, reread=(False, False))
        d_yc, d_u_skip, grads["ssm_d"][l], grads["b_glu"][l] = _rowwise_vjp(
            f"s5_post_bwd_{l}", _s5_post_fn, ins_s5, [rows_in(d_y1), rows_in(d_s5_out)],
            [row_wrt(0, ssm_w, F32), row_wrt(1, ssm_w, F32), sum_wrt(3, ssm_w), sum_wrt(4, ssm_w)], (n_r,))
        bs16, cs16, lam = s5_b16[l]
        d_u, d_bs, d_cs, d_lam = _s5_scan_bwd(sv["proj"], u_col, sv["states"], d_yc, d_u_skip, bs16, cs16, lam,
                                              s5_pw[l][1], t_scan)
        raw = [p[n][l] for n in ("ssm_a_re", "ssm_a_im", "ssm_log_dt", "ssm_b_re", "ssm_b_im", "ssm_c_re", "ssm_c_im")]
        _, pull = jax.vjp(_s5_discretize, *raw)
        (grads["ssm_a_re"][l], grads["ssm_a_im"][l], grads["ssm_log_dt"][l], grads["ssm_b_re"][l],
         grads["ssm_b_im"][l], grads["ssm_c_re"][l], grads["ssm_c_im"][l]) = pull((d_bs, d_cs, d_lam))
        d_proj = [d_q, d_k, d_v, d_u, d_gates]
        g_in = _mm_pieces(f"proj_dw_{l}", d_proj, proj_starts, n_in, lambda i, j, k: i, sv["h"],
                          _spec((t_m, t_d), lambda i, j, k: (k, j)), False, lambda i, j, k: k,
                          _spec((None, t_d, n_in), lambda i, j, k: (i, j, 0)), (N_DEV, d, n_in), GRAD_WIRE,
                          (N_DEV, d // t_d, n_m), TN)
        waiting = [("w_in", g_in), ("w_glu", g_glu.reshape(N_DEV, -1, ssm_w))]
        closing = _Exchange(layered=[(g, 0, depth, land.get(n)) for n, g in waiting]) if l == 0 else None
        d_h = _mm_pieces(f"proj_dx_{l}", d_proj, proj_starts, n_in, lambda i, j, k: k, wg_in[l],
                         _spec((None, t_d, n_in), lambda i, j, k: (k, j, 0)), True, lambda i, j, k: i,
                         _spec((t_m, t_d), lambda i, j, k: (i, j)), (seq, d), F32, (n_m, d // t_d, N_DEV), NT,
                         beside=closing)
        if l == 0:
            d_h, got = d_h
            land.update({n: buf for (n, _), buf in zip(waiting, got)})
        d_x, d_h_next = d_x_in, d_h
    res = _rowwise_vjp("modulate_in_bwd", lambda v, sc, sh: (v, _modulate(v, sc, sh)),
                       [rows_in(x0), vec_in(mods[0][1]), vec_in(mods[0][0])], [rows_in(d_x), rows_in(d_h_next)],
                       [row_wrt(0, d, F32), sum_wrt(1, d), sum_wrt(2, d)], (n_r,))
    grad_x, d_mod[0][1], d_mod[0][0] = res

    d_mod_rows = jnp.concatenate([jnp.concatenate(d_mod[l], axis=1) for l in range(depth)], axis=0)
    grads["b_ada"] = [d_mod_rows[l] for l in range(depth)]
    small_local = [jnp.stack([g.reshape(p[n].shape[1:]) for g in grads[n]]) for n in SMALL_PARAMS]
    d_mod_send = jnp.swapaxes(d_mod_rows.reshape(depth, N_DEV, n_ada), 0, 1)
    small_sum, (d_mod_cols,) = _reduce_packed("exchange_last", _pack(small_local), [d_mod_send])
    d_mod_pad = jnp.pad(jnp.swapaxes(d_mod_cols, 0, 1), ((0, 0), (0, rows_c - N_DEV), (0, 0)))
    g_ada = [
        _mm(f"mod_dw_{l}", c_act, d_mod_pad,
            _spec((rows_c, d), lambda i, j, k: (0, 0)), _spec((None, rows_c, n_ada), lambda i, j, k, l=l: (l, 0, 0)),
            _spec((d, n_ada), lambda i, j, k: (0, 0)), (d, n_ada), F32, (1, 1, 1), TN)
        for l in range(depth)]

    out = {}

    def update(name, partials):
        shape = p[name].shape
        two_d = lambda a: a.reshape(-1, shape[-1])
        res = _adamw("adamw_" + name, two_d(p[name]), two_d(p["m_" + name]), two_d(p["v_" + name]),
                     partials.reshape(partials.shape[0], -1, shape[-1]))
        out[name] = [r.reshape(shape) for r in res]

    update("w_ada", jnp.stack(g_ada)[None])
    for n in ("w_in", "w_sb_up", "w_ssm_up", "w_ffn_in", "w_glu", "w_out", "w_ffn_out"):
        update(n, land[n])
    small_w = [p[n] for n in SMALL_PARAMS]
    res = _adamw("adamw_small", _pack(small_w), _pack([p["m_" + n] for n in SMALL_PARAMS]),
                 _pack([p["v_" + n] for n in SMALL_PARAMS]), small_sum[None])
    for kind, packed in enumerate(res):
        for n, a in zip(SMALL_PARAMS, _unpack(packed, small_w)):
            out.setdefault(n, [None] * 4)[kind] = a

    return ((loss, grad_x[None]) + tuple(out[n][0] for n in WEIGHTS) + tuple(out[n][1] for n in WEIGHTS)
            + tuple(out[n][2] for n in WEIGHTS) + tuple(out[n][3] for n in WEIGHTS))
```

```python
import jax
import jax.numpy as jnp
from jax import lax
from jax.experimental import pallas as pl
from jax.experimental.pallas import tpu as pltpu

F32 = jnp.float32
BF16 = jnp.bfloat16
GRAD_WIRE = BF16
FFN_ACT = BF16
BRANCH_CT = BF16

N_DEV = 8
LANES = 128
SUBLANES = 8
VMEM_BYTES = 64 * 1024 * 1024
HEAD_DIM = 64
SB_BLOCK = 256
SLAB_GROUPS = 8
LN_EPS = 1e-5
ADAM_LR, ADAM_B1, ADAM_B2, ADAM_EPS, ADAM_WD, ADAM_STEP = 0.001, 0.9, 0.999, 1e-08, 0.01, 10
SB_UNDERFLOW = -120.0

PACK_ROWS = 256
MESH_AXES = ("x", "y", "c")


def _vmem_limit(block_bytes):
    return int(min(max(3 * block_bytes + (8 << 20), 24 << 20), VMEM_BYTES - (8 << 20)))


def _nbytes(shape, dtype):
    n = 1
    for d in shape:
        if d is not None:
            n *= d
    return n * jnp.dtype(dtype).itemsize


def _spec(shape, fn):
    return pl.BlockSpec(shape, fn)


class _Exchange:
    def __init__(self, scatter=(), gather=(), layered=()):
        self.arrs = list(scatter) + [a for a, _, _, _ in layered] + list(gather)
        self.n = len(self.arrs)
        self.n_sc = len(scatter) + len(layered)
        self.layer = [None] * len(scatter) + [l for _, l, _, _ in layered] + [None] * len(gather)
        self.shapes = ([a.shape for a in scatter] + [(N_DEV, dp) + a.shape[1:] for a, _, dp, _ in layered]
                       + [(N_DEV,) + a.shape for a in gather])
        self.held = [(len(scatter) + i, b) for i, (_, _, _, b) in enumerate(layered) if b is not None]
        self.operands = self.arrs + [b for _, b in self.held]
        hbm = pl.BlockSpec(memory_space=pltpu.HBM)
        self.in_specs = [hbm] * len(self.operands)
        self.out_specs = [hbm] * self.n
        self.out_shape = [jax.ShapeDtypeStruct(s, a.dtype) for s, a in zip(self.shapes, self.arrs)]
        self.scratch = [pltpu.SemaphoreType.DMA((self.n, N_DEV - 1)), pltpu.SemaphoreType.DMA((self.n, N_DEV - 1)),
                        pltpu.SemaphoreType.DMA((self.n,))]

    def aliases(self, first_in, first_out):
        return {first_in + self.n + i: first_out + a for i, (a, _) in enumerate(self.held)}

    def copies(self, ins, outs, sems):
        send_sems, recv_sems, own_sems = sems
        x, y, c = lax.axis_index("x"), lax.axis_index("y"), lax.axis_index("c")
        me = 4 * x + 2 * y + c
        landing = [outs[a].at[me] if self.layer[a] is None else outs[a].at[me, self.layer[a]] for a in range(self.n)]
        out = [pltpu.make_async_copy(ins[a].at[me] if a < self.n_sc else ins[a], landing[a], own_sems.at[a])
               for a in range(self.n)]
        for k in range(1, N_DEV):
            px = 1 - x if k & 4 else x
            py = 1 - y if k & 2 else y
            pc = 1 - c if k & 1 else c
            peer = 4 * px + 2 * py + pc
            for a in range(self.n):
                out.append(pltpu.make_async_remote_copy(
                    src_ref=ins[a].at[peer] if a < self.n_sc else ins[a], dst_ref=landing[a],
                    send_sem=send_sems.at[a, k - 1], recv_sem=recv_sems.at[a, k - 1],
                    device_id=(px, py, pc), device_id_type=pl.DeviceIdType.MESH))
        return out


def _exchange(name, scatter, gather, layered=()):
    ex = _Exchange(scatter, gather, layered)

    def body(*refs):
        copies = ex.copies(refs[:ex.n], refs[len(ex.operands):len(ex.operands) + ex.n], refs[-3:])
        for cp in copies:
            cp.start()
        for cp in copies:
            cp.wait()

    return pl.pallas_call(body, name=name, in_specs=ex.in_specs, out_specs=ex.out_specs, out_shape=ex.out_shape,
                          input_output_aliases=ex.aliases(0, 0), scratch_shapes=ex.scratch)(*ex.operands)


def _reduce_packed(name, packed, scatter):
    rows = packed.shape[0]
    blk = rows // N_DEV
    ex = _Exchange(scatter=[packed.reshape(N_DEV, blk, LANES)] + list(scatter))
    n_in = len(ex.operands)

    def body(*refs):
        ins, outs = refs[:ex.n], refs[n_in:n_in + ex.n]
        total_ref = refs[n_in + ex.n]
        sems, (send2, recv2, own2, load_sem) = refs[n_in + ex.n + 1:n_in + ex.n + 4], refs[n_in + ex.n + 4:-2]
        land_v, sum_v = refs[-2:]
        copies = ex.copies(ins, outs, sems)
        for cp in copies:
            cp.start()
        for cp in copies:
            cp.wait()
        load = pltpu.make_async_copy(outs[0], land_v, load_sem)
        load.start()
        load.wait()
        acc = land_v[0]
        for i in range(1, N_DEV):
            acc = acc + land_v[i]
        sum_v[...] = acc
        x, y, c = lax.axis_index("x"), lax.axis_index("y"), lax.axis_index("c")
        me = 4 * x + 2 * y + c
        back = [pltpu.make_async_copy(sum_v, total_ref.at[me], own2)]
        for k in range(1, N_DEV):
            peer = (1 - x if k & 4 else x, 1 - y if k & 2 else y, 1 - c if k & 1 else c)
            back.append(pltpu.make_async_remote_copy(
                src_ref=sum_v, dst_ref=total_ref.at[me], send_sem=send2.at[k - 1], recv_sem=recv2.at[k - 1],
                device_id=peer, device_id_type=pl.DeviceIdType.MESH))
        for cp in back:
            cp.start()
        for cp in back:
            cp.wait()

    hbm = pl.BlockSpec(memory_space=pltpu.HBM)
    res = pl.pallas_call(
        body, name=name, in_specs=ex.in_specs, out_specs=ex.out_specs + [hbm],
        out_shape=ex.out_shape + [jax.ShapeDtypeStruct((N_DEV, blk, LANES), F32)],
        scratch_shapes=ex.scratch + [pltpu.SemaphoreType.DMA((N_DEV - 1,)), pltpu.SemaphoreType.DMA((N_DEV - 1,)),
                                     pltpu.SemaphoreType.DMA, pltpu.SemaphoreType.DMA,
                                     pltpu.VMEM((N_DEV, blk, LANES), F32), pltpu.VMEM((blk, LANES), F32)],
    )(*ex.operands)
    return res[-1].reshape(rows, LANES), res[1:-1]


def _call_beside(ex, body, name, grid, in_specs, out_specs, out_shape, scratch_shapes, vmem_bytes, operands,
                 semantics, in_hbm=True):
    if in_hbm:
        operands = [_in_hbm(a) for a in operands]
    if ex is None:
        res = pl.pallas_call(
            body, name=name, grid=grid, in_specs=in_specs, out_specs=out_specs, out_shape=out_shape,
            scratch_shapes=scratch_shapes,
            compiler_params=pltpu.CompilerParams(dimension_semantics=semantics, vmem_limit_bytes=vmem_bytes),
        )(*operands)
        return res, None
    n_in, n_out, n_scr = len(in_specs), len(out_specs), len(scratch_shapes)
    n_xin = len(ex.operands)

    def fused(*refs):
        mine = refs[:n_in] + refs[n_in + n_xin:n_in + n_xin + n_out]
        mine += refs[n_in + n_xin + n_out + ex.n:n_in + n_xin + n_out + ex.n + n_scr]
        first = pl.program_id(0) == 0
        last = pl.program_id(0) == grid[0] - 1
        for dim in range(1, len(grid)):
            first = jnp.logical_and(first, pl.program_id(dim) == 0)
            last = jnp.logical_and(last, pl.program_id(dim) == grid[dim] - 1)
        x_ins = refs[n_in:n_in + ex.n]
        x_outs = refs[n_in + n_xin + n_out:n_in + n_xin + n_out + ex.n]

        @pl.when(first)
        def _():
            for cp in ex.copies(x_ins, x_outs, refs[-3:]):
                cp.start()

        body(*mine)

        @pl.when(last)
        def _():
            for cp in ex.copies(x_ins, x_outs, refs[-3:]):
                cp.wait()

    res = pl.pallas_call(
        fused, name=name, grid=grid, in_specs=list(in_specs) + ex.in_specs, out_specs=list(out_specs) + ex.out_specs,
        out_shape=list(out_shape) + ex.out_shape, input_output_aliases=ex.aliases(n_in, n_out),
        scratch_shapes=list(scratch_shapes) + ex.scratch,
        compiler_params=pltpu.CompilerParams(dimension_semantics=("arbitrary",) * len(grid),
                                             vmem_limit_bytes=vmem_bytes),
    )(*operands, *ex.operands)
    return res[:n_out], res[n_out:]


NN = (((1,), (0,)), ((), ()))
NT = (((1,), (1,)), ((), ()))
TN = (((0,), (0,)), ((), ()))


def _in_hbm(a):
    return pltpu.with_memory_space_constraint(a, pltpu.HBM)


def _mm(name, a, b, a_spec, b_spec, o_spec, o_shape, o_dtype, grid, dims, beside=None, reread=(False, True)):
    nk = grid[2]
    a, b = (x if again else _in_hbm(x) for x, again in zip((a, b), reread))
    acc_shape = tuple(d for d in o_spec.block_shape if d is not None)

    def product(a_ref, b_ref):
        return lax.dot_general(a_ref[...].astype(BF16), b_ref[...].astype(BF16), dims, preferred_element_type=F32)

    def body_once(a_ref, b_ref, o_ref):
        o_ref[...] = product(a_ref, b_ref).astype(o_ref.dtype)

    def body(a_ref, b_ref, o_ref, acc_ref):
        k = pl.program_id(2)

        @pl.when(k == 0)
        def _():
            acc_ref[...] = product(a_ref, b_ref)

        @pl.when(k > 0)
        def _():
            acc_ref[...] += product(a_ref, b_ref)

        @pl.when(k == nk - 1)
        def _():
            o_ref[...] = acc_ref[...].astype(o_ref.dtype)

    blk = (_nbytes(a_spec.block_shape, a.dtype) + _nbytes(b_spec.block_shape, b.dtype)
           + _nbytes(acc_shape, o_dtype) + _nbytes(acc_shape, F32))
    res, got = _call_beside(
        beside, body_once if nk == 1 else body, name, grid, [a_spec, b_spec], [o_spec],
        [jax.ShapeDtypeStruct(o_shape, o_dtype)], [] if nk == 1 else [pltpu.VMEM(acc_shape, F32)],
        _vmem_limit(blk), (a, b), ("parallel", "parallel", "arbitrary"), in_hbm=False)
    return res[0] if beside is None else (res[0], got)


def _mm_pieces(name, pieces, starts, width, step_block, other, other_spec, pieces_first, piece_rows, o_spec, o_shape,
               o_dtype, grid, dims, beside=None):
    n_p, nk = len(pieces), grid[2]
    acc_shape = tuple(s for s in o_spec.block_shape if s is not None)

    def which(i, j, k):
        blk = step_block(i, j, k)
        idx = 0
        for s in starts[1:]:
            idx = idx + (blk >= s).astype(jnp.int32)
        return idx, blk

    def piece_spec(p, t_rows):
        def index(i, j, k):
            idx, blk = which(i, j, k)
            mine = idx == p
            return jnp.where(mine, piece_rows(i, j, k), 0), jnp.where(mine, blk - starts[p], 0)
        return _spec((t_rows, width), index)

    def body(*refs):
        p_refs = refs[:n_p] if pieces_first else refs[1:1 + n_p]
        other_ref = refs[n_p] if pieces_first else refs[0]
        o_ref, acc_ref = refs[n_p + 1], refs[n_p + 2]
        i, j, k = pl.program_id(0), pl.program_id(1), pl.program_id(2)

        @pl.when(k == 0)
        def _():
            acc_ref[...] = jnp.zeros_like(acc_ref)

        idx, _ = which(i, j, k)
        for p in range(n_p):
            @pl.when(idx == p)
            def _(p=p):
                mine, fixed = p_refs[p][...].astype(BF16), other_ref[...].astype(BF16)
                pair = (mine, fixed) if pieces_first else (fixed, mine)
                acc_ref[...] += lax.dot_general(pair[0], pair[1], dims, preferred_element_type=F32)

        @pl.when(k == nk - 1)
        def _():
            o_ref[...] = acc_ref[...].astype(o_ref.dtype)

    t_rows = other_spec.block_shape[-2] if not pieces_first else o_spec.block_shape[-2]
    specs = [piece_spec(p, t_rows) for p in range(n_p)]
    in_specs = specs + [other_spec] if pieces_first else [other_spec] + specs
    operands = list(pieces) + [other] if pieces_first else [other] + list(pieces)
    blk = (n_p * 4 * t_rows * width + _nbytes(other_spec.block_shape, other.dtype)
           + _nbytes(acc_shape, o_dtype) + _nbytes(acc_shape, F32))
    res, got = _call_beside(
        beside, body, name, grid, in_specs, [o_spec], [jax.ShapeDtypeStruct(o_shape, o_dtype)],
        [pltpu.VMEM(acc_shape, F32)], _vmem_limit(blk), operands, ("parallel", "parallel", "arbitrary"), in_hbm=False)
    return res[0] if beside is None else (res[0], got)


def _swiglu_fn(gate_up):
    gate, up = gate_up[0], gate_up[1]
    return gate * jax.nn.sigmoid(gate) * up


def _ffn_in_swiglu(name, h, w, t_m):
    seq, d = h.shape
    n_half, n = w.shape[0] // 2, w.shape[2]

    def body(h_ref, wg_ref, wu_ref, a_ref, f_ref):
        hb = h_ref[...]
        a_ref[0] = lax.dot_general(hb, wg_ref[...], NN, preferred_element_type=F32).astype(a_ref.dtype)
        a_ref[1] = lax.dot_general(hb, wu_ref[...], NN, preferred_element_type=F32).astype(a_ref.dtype)
        f_ref[...] = _swiglu_fn(a_ref[...].astype(F32)).astype(f_ref.dtype)

    blk = 2 * t_m * d + 4 * d * n + 6 * t_m * n + 12 * t_m * n
    return pl.pallas_call(
        body, name=name, grid=(seq // t_m, n_half),
        in_specs=[_spec((t_m, d), lambda i, j: (i, 0)), _spec((None, d, n), lambda i, j: (j, 0, 0)),
                  _spec((None, d, n), lambda i, j: (j + n_half, 0, 0))],
        out_specs=[_spec((2, None, t_m, n), lambda i, j: (0, j, i, 0)), _spec((None, t_m, n), lambda i, j: (j, i, 0))],
        out_shape=[jax.ShapeDtypeStruct((2, n_half, seq, n), FFN_ACT), jax.ShapeDtypeStruct((n_half, seq, n), BF16)],
        compiler_params=pltpu.CompilerParams(dimension_semantics=("parallel", "parallel"),
                                             vmem_limit_bytes=_vmem_limit(blk)),
    )(h, w, w)


def _ffn_out_dx_swiglu(name, d_y, w, a, t_m):
    seq, d = d_y.shape
    n_half, n = w.shape[0], w.shape[1]

    def body(dy_ref, w_ref, a_ref, da_ref):
        d_f = lax.dot_general(dy_ref[...], w_ref[...], NT, preferred_element_type=F32)
        _, pull = jax.vjp(_swiglu_fn, a_ref[...].astype(F32))
        da_ref[...] = pull(d_f)[0].astype(da_ref.dtype)

    blk = 2 * t_m * d + 2 * d * n + 8 * t_m * n + 24 * t_m * n
    return pl.pallas_call(
        body, name=name, grid=(seq // t_m, n_half),
        in_specs=[_spec((t_m, d), lambda i, j: (i, 0)), _spec((None, n, d), lambda i, j: (j, 0, 0)),
                  _spec((2, None, t_m, n), lambda i, j: (0, j, i, 0))],
        out_specs=_spec((2, None, t_m, n), lambda i, j: (0, j, i, 0)),
        out_shape=jax.ShapeDtypeStruct((2, n_half, seq, n), BF16),
        compiler_params=pltpu.CompilerParams(dimension_semantics=("parallel", "parallel"),
                                             vmem_limit_bytes=_vmem_limit(blk)),
    )(d_y, w, a)


def _merge_fn(y_sb, y_ssm, gates):
    half = gates.shape[-1] // 2
    return jax.nn.sigmoid(gates[:, :half]) * y_sb + jax.nn.sigmoid(gates[:, half:]) * y_ssm


def _up_merge(name, o_sb, s5_out, proj, gates_cb, w_sb, w_ssm, layer, t_rows):
    seq = o_sb.shape[0]
    d = w_sb.shape[2]

    def body(o_ref, s_ref, g_ref, w1_ref, w2_ref, m_ref, y1_ref, y2_ref):
        y_sb = lax.dot_general(o_ref[...], w1_ref[...], NN, preferred_element_type=F32)
        y_ssm = lax.dot_general(s_ref[...], w2_ref[...], NN, preferred_element_type=F32)
        m_ref[...] = _merge_fn(y_sb, y_ssm, g_ref[...]).astype(m_ref.dtype)
        y1_ref[...] = y_sb.astype(y1_ref.dtype)
        y2_ref[...] = y_ssm.astype(y2_ref.dtype)

    row = lambda width: _spec((t_rows, width), lambda i: (i, 0))
    whole = lambda w: _spec((None,) + w.shape[1:], lambda i: (layer, 0, 0))
    blk = t_rows * (2 * o_sb.shape[1] + 2 * s5_out.shape[1] + 8 * d + 6 * d + 24 * d) + 4 * d * (o_sb.shape[1] + s5_out.shape[1])
    return pl.pallas_call(
        body, name=name, grid=(seq // t_rows,),
        in_specs=[row(o_sb.shape[1]), row(s5_out.shape[1]), _spec((t_rows, 2 * d), lambda i: (i, gates_cb)),
                  whole(w_sb), whole(w_ssm)],
        out_specs=[row(d)] * 3, out_shape=[jax.ShapeDtypeStruct((seq, d), BF16)] * 3,
        compiler_params=pltpu.CompilerParams(dimension_semantics=("parallel",), vmem_limit_bytes=_vmem_limit(blk)),
    )(_in_hbm(o_sb), _in_hbm(s5_out), _in_hbm(proj), w_sb, w_ssm)


def _tile(n, pref=1024):
    t = pref
    while t >= LANES:
        if n % t == 0:
            return t
        t -= LANES
    return n


def _rowwise(name, fn, ins, outs, grid):
    n_in = len(ins)

    def body(*refs):
        vals = fn(*[r[...].astype(F32) for r in refs[:n_in]])
        if not isinstance(vals, (tuple, list)):
            vals = (vals,)
        for r, v in zip(refs[n_in:], vals):
            r[...] = v.astype(r.dtype)

    blk = sum(_nbytes(bs, a.dtype) for a, bs, _ in ins) + sum(_nbytes(bs, d) + _nbytes(bs, F32) for _, d, bs, _ in outs)
    return pl.pallas_call(
        body, name=name, grid=grid,
        in_specs=[_spec(bs, im) for _, bs, im in ins],
        out_specs=[_spec(bs, im) for _, _, bs, im in outs],
        out_shape=[jax.ShapeDtypeStruct(s, d) for s, d, _, _ in outs],
        compiler_params=pltpu.CompilerParams(dimension_semantics=("parallel",) * len(grid),
                                             vmem_limit_bytes=_vmem_limit(2 * blk)),
    )(*[_in_hbm(a) for a, _, _ in ins])


def _rowwise_vjp(name, fn, ins, cts, wrt, grid):
    n_in, n_ct = len(ins), len(cts)
    idx = [w[0] for w in wrt]

    def body(*refs):
        prim = [r[...].astype(F32) for r in refs[:n_in]]
        ct = tuple(r[...].astype(F32) for r in refs[n_in:n_in + n_ct])
        o_refs = refs[n_in + n_ct:]

        def g(*sel):
            full = list(prim)
            for i, s in zip(idx, sel):
                full[i] = s
            out = fn(*full)
            return tuple(out) if isinstance(out, (tuple, list)) else (out,)

        _, pull = jax.vjp(g, *[prim[i] for i in idx])
        grads = pull(ct)
        first = pl.program_id(0) == 0
        for d in range(1, len(grid)):
            first = jnp.logical_and(first, pl.program_id(d) == 0)
        for w, o_ref, gr in zip(wrt, o_refs, grads):
            if w[1] == "row":
                o_ref[...] = gr.astype(o_ref.dtype)
            else:
                @pl.when(first)
                def _(o_ref=o_ref):
                    o_ref[...] = jnp.zeros_like(o_ref)

                o_ref[...] += gr.astype(o_ref.dtype)

    blk = (sum(_nbytes(bs, a.dtype) + _nbytes(bs, F32) for a, bs, _ in list(ins) + list(cts))
           + sum(_nbytes(w[4], w[3]) + _nbytes(w[4], F32) for w in wrt))
    return pl.pallas_call(
        body, name=name, grid=grid,
        in_specs=[_spec(bs, im) for _, bs, im in list(ins) + list(cts)],
        out_specs=[_spec(w[4], w[5]) for w in wrt],
        out_shape=[jax.ShapeDtypeStruct(w[2], w[3]) for w in wrt],
        compiler_params=pltpu.CompilerParams(dimension_semantics=("arbitrary",) * len(grid),
                                             vmem_limit_bytes=_vmem_limit(2 * blk)),
    )(*[_in_hbm(a) for a, _, _ in list(ins) + list(cts)])


def _normalize(x):
    mu = jnp.mean(x, axis=-1, keepdims=True)
    xc = x - mu
    var = jnp.mean(xc * xc, axis=-1, keepdims=True)
    return xc * lax.rsqrt(var + LN_EPS)


def _modulate(x, sc, sh):
    return _normalize(x) * (1.0 + sc) + sh


def _make_resid_fns(alpha):
    def resid_ln(x, y, gate, g, b):
        return _normalize(alpha * x + (1.0 + gate) * y) * g + b

    def resid_ln_mod(x, y, gate, g, b, sc, sh):
        xn = resid_ln(x, y, gate, g, b)
        return xn, _modulate(xn, sc, sh)

    return resid_ln, resid_ln_mod


def _s5_act_fn(yc, u, d_skip):
    return jax.nn.gelu(yc + d_skip * u)


def _s5_glu_fn(yc, u, t, d_skip, b_glu):
    return _s5_act_fn(yc, u, d_skip) * jax.nn.sigmoid(t + b_glu)


def _s5_post_fn(yc, u, t, d_skip, b_glu):
    y1 = _s5_act_fn(yc, u, d_skip)
    return y1, y1 * jax.nn.sigmoid(t + b_glu)


def _sb_tri(kind):
    row = lax.broadcasted_iota(jnp.int32, (SB_BLOCK, SB_BLOCK), 0)
    col = lax.broadcasted_iota(jnp.int32, (SB_BLOCK, SB_BLOCK), 1)
    if kind == "after":
        return (row > col).astype(BF16)
    if kind == "from":
        return (row >= col).astype(BF16)
    return col < row


def _split_dot(x, m):
    hi = x.astype(BF16)
    lo = (x - hi.astype(F32)).astype(BF16)
    return (lax.dot_general(hi, m, NN, preferred_element_type=F32)
            + lax.dot_general(lo, m, NN, preferred_element_type=F32))


def _sb_scores(qh, k2):
    z = lax.dot_general(qh, k2, NT, preferred_element_type=F32)
    log_beta = jnp.minimum(z, 0.0) - jnp.log(1.0 + jnp.exp(-jnp.abs(z)))
    return log_beta, log_beta - z


def _sb_attention_fwd(proj, sb_width, beside=None):
    seq = proj.shape[0]
    n_pair, n_q = sb_width // LANES, seq // SB_BLOCK
    scale = 1.0 / (HEAD_DIM ** 0.5)

    def body(q_ref, k_ref, v_ref, o_ref, o32_ref):
        qi = pl.program_id(1)
        q2 = q_ref[...]
        lane = lax.broadcasted_iota(jnp.int32, (SB_BLOCK, LANES), 1)
        m_after, causal = _sb_tri("after"), _sb_tri("mask")
        heads = [lane < HEAD_DIM, lane >= HEAD_DIM]
        qh = [(jnp.where(m, q2, 0.0) * scale).astype(BF16) for m in heads]

        def scores(kb, diag):
            ks = pl.multiple_of(kb * SB_BLOCK, SB_BLOCK)
            k2 = k_ref[pl.ds(ks, SB_BLOCK), :].astype(BF16)
            out = []
            for h in range(2):
                log_beta, log_1m = _sb_scores(qh[h], k2)
                if diag:
                    log_1m = jnp.where(causal, log_1m, 0.0)
                out += [log_beta + _split_dot(log_1m, m_after), jnp.sum(log_1m, axis=1, keepdims=True)]
            return tuple(out)

        def weigh(kb, sc, carry, acc, diag):
            ks = pl.multiple_of(kb * SB_BLOCK, SB_BLOCK)
            v2 = v_ref[pl.ds(ks, SB_BLOCK), :].astype(BF16)
            out = []
            for h in range(2):
                w = jnp.exp(sc[2 * h] + carry[h])
                if diag:
                    w = jnp.where(causal, w, 0.0)
                out.append(acc[h] + lax.dot_general(w.astype(BF16), v2, NN, preferred_element_type=F32))
            return tuple(out)

        zero = jnp.zeros((SB_BLOCK, LANES), F32)
        zcol = jnp.zeros((SB_BLOCK, 1), F32)
        sc = scores(qi, True)
        acc = weigh(qi, sc, (zcol, zcol), (zero, zero), True)
        carry = (sc[1], sc[3])

        def loop(st):
            kb, carry, acc = st
            sc = scores(kb, False)
            after = (carry[0] + sc[1], carry[1] + sc[3])
            done = jnp.maximum(jnp.max(after[0]), jnp.max(after[1])) < SB_UNDERFLOW
            acc = weigh(kb, sc, carry, acc, False)
            return jnp.where(done, -1, kb - 1), after, acc

        _, _, acc = lax.while_loop(lambda st: st[0] >= 0, loop, (qi - 1, carry, acc))
        out = jnp.where(heads[0], acc[0], acc[1])
        o_ref[...] = out.astype(o_ref.dtype)
        o32_ref[...] = out

    q_spec = _spec((SB_BLOCK, LANES), lambda h, i: (i, h))
    kv = [_spec((seq, LANES), lambda h, i, o=o: (0, o + h)) for o in (n_pair, 2 * n_pair)]
    o_spec = _spec((SB_BLOCK, LANES), lambda h, i: (i, h))
    return _call_beside(
        beside, body, "sb_attention_fwd", (n_pair, n_q), [q_spec] + kv, [o_spec, o_spec],
        [jax.ShapeDtypeStruct((seq, sb_width), BF16), jax.ShapeDtypeStruct((seq, sb_width), F32)], [],
        _vmem_limit(2 * seq * LANES * 4), (proj, proj, proj), ("parallel", "arbitrary"))


def _sb_attention_bwd(proj, o32, do, sb_width, beside=None):
    seq = proj.shape[0]
    n_pair, n_q = sb_width // LANES, seq // SB_BLOCK
    scale = 1.0 / (HEAD_DIM ** 0.5)

    def body(q_ref, k_ref, v_ref, o_ref, do_ref, dq_ref, dk_ref, dv_ref):
        qi = pl.program_id(1)

        @pl.when(qi == 0)
        def _():
            dk_ref[...] = jnp.zeros_like(dk_ref)
            dv_ref[...] = jnp.zeros_like(dv_ref)

        q2 = q_ref[...]
        do2 = do_ref[...].astype(F32)
        o2 = o_ref[...]
        lane = lax.broadcasted_iota(jnp.int32, (SB_BLOCK, LANES), 1)
        m_after, m_from, causal = _sb_tri("after"), _sb_tri("from"), _sb_tri("mask")
        heads = [lane < HEAD_DIM, lane >= HEAD_DIM]
        qh = [(jnp.where(m, q2, 0.0) * scale).astype(BF16) for m in heads]
        doh = [jnp.where(m, do2, 0.0) for m in heads]
        doh_b = [v.astype(BF16) for v in doh]
        total = [jnp.sum(v * o2, axis=1, keepdims=True) for v in doh]

        def scores(kb, diag):
            ks = pl.multiple_of(kb * SB_BLOCK, SB_BLOCK)
            k2 = k_ref[pl.ds(ks, SB_BLOCK), :].astype(BF16)
            v2 = v_ref[pl.ds(ks, SB_BLOCK), :].astype(BF16)
            out = []
            for h in range(2):
                log_beta, log_1m = _sb_scores(qh[h], k2)
                if diag:
                    log_1m = jnp.where(causal, log_1m, 0.0)
                out += [log_beta + _split_dot(log_1m, m_after), jnp.sum(log_1m, axis=1, keepdims=True),
                        lax.dot_general(doh_b[h], v2, NT, preferred_element_type=F32), log_beta]
            return tuple(out)

        def pull(kb, sc, carry, right, dq, diag):
            ks = pl.multiple_of(kb * SB_BLOCK, SB_BLOCK)
            k2 = k_ref[pl.ds(ks, SB_BLOCK), :].astype(BF16)
            dv_blk, dk_blk, right_out, dq_out = None, None, [], []
            for h in range(2):
                arg, _, d_w, log_beta = sc[4 * h:4 * h + 4]
                w = jnp.exp(arg + carry[h])
                if diag:
                    w = jnp.where(causal, w, 0.0)
                w_b = w.astype(BF16)
                d_arg = d_w * w_b.astype(F32)
                dv_h = lax.dot_general(w_b, doh_b[h], TN, preferred_element_type=F32)
                d_log_1m = total[h] - right[h] - _split_dot(d_arg, m_from)
                beta = jnp.exp(log_beta)
                dz = d_arg * (1.0 - beta) - beta * d_log_1m
                if diag:
                    dz = jnp.where(causal, dz, 0.0)
                dz_b = dz.astype(BF16)
                dk_h = lax.dot_general(dz_b, qh[h], TN, preferred_element_type=F32)
                dv_blk = dv_h if h == 0 else dv_blk + dv_h
                dk_blk = dk_h if h == 0 else dk_blk + dk_h
                dq_out.append(dq[h] + lax.dot_general(dz_b, k2, NN, preferred_element_type=F32))
                right_out.append(right[h] + jnp.sum(d_arg, axis=1, keepdims=True))
            dv_ref[pl.ds(ks, SB_BLOCK), :] += dv_blk
            dk_ref[pl.ds(ks, SB_BLOCK), :] += dk_blk
            return tuple(right_out), tuple(dq_out)

        zero = jnp.zeros((SB_BLOCK, LANES), F32)
        zcol = jnp.zeros((SB_BLOCK, 1), F32)
        sc = scores(qi, True)
        right, dq = pull(qi, sc, (zcol, zcol), (zcol, zcol), (zero, zero), True)
        carry = (sc[1], sc[5])

        def loop(st):
            kb, carry, right, dq = st
            sc = scores(kb, False)
            after = (carry[0] + sc[1], carry[1] + sc[5])
            done = jnp.maximum(jnp.max(after[0]), jnp.max(after[1])) < SB_UNDERFLOW
            right, dq = pull(kb, sc, carry, right, dq, False)
            return jnp.where(done, -1, kb - 1), after, right, dq

        _, _, _, dq = lax.while_loop(lambda st: st[0] >= 0, loop, (qi - 1, carry, right, dq))
        dq_ref[...] = (jnp.where(heads[0], dq[0], dq[1]) * scale).astype(dq_ref.dtype)

    q_spec = _spec((SB_BLOCK, LANES), lambda h, i: (i, h))
    kv = [_spec((seq, LANES), lambda h, i, o=o: (0, o + h)) for o in (n_pair, 2 * n_pair)]
    full = _spec((seq, LANES), lambda h, i: (0, h))
    return _call_beside(
        beside, body, "sb_attention_bwd", (n_pair, n_q), [q_spec] + kv + [q_spec, q_spec], [q_spec, full, full],
        [jax.ShapeDtypeStruct((seq, sb_width), BF16), jax.ShapeDtypeStruct((seq, sb_width), F32),
         jax.ShapeDtypeStruct((seq, sb_width), F32)], [],
        _vmem_limit(4 * seq * LANES * 4), (proj, proj, proj, o32, do), ("parallel", "arbitrary"))


def _s5_discretize(a_re, a_im, log_dt, b_re, b_im, c_re, c_im):
    n_g, n_p = a_re.shape
    c_g = b_re.shape[-1]
    ns = n_g // SLAB_GROUPS
    dt = jnp.exp(log_dt)[:, None]
    xr, xi = a_re * dt, a_im * dt
    mag = jnp.exp(xr)
    lr, li = mag * jnp.cos(xi), mag * jnp.sin(xi)
    den = a_re * a_re + a_im * a_im
    fr = ((lr - 1.0) * a_re + li * a_im) / den
    fi = (li * a_re - (lr - 1.0) * a_im) / den
    bb_re = fr[..., None] * b_re - fi[..., None] * b_im
    bb_im = fr[..., None] * b_im + fi[..., None] * b_re
    eye = jnp.eye(SLAB_GROUPS, dtype=F32)

    def diag_b(m):
        m = jnp.transpose(m.reshape(ns, SLAB_GROUPS, n_p, c_g), (0, 1, 3, 2))
        m = m[:, :, :, None, :] * eye[None, :, None, :, None]
        return m.reshape(ns, SLAB_GROUPS * c_g, SLAB_GROUPS * n_p)

    def diag_c(m):
        m = jnp.transpose(m.reshape(ns, SLAB_GROUPS, c_g, n_p), (0, 1, 3, 2))
        m = m[:, :, :, None, :] * eye[None, :, None, :, None]
        return m.reshape(ns, SLAB_GROUPS * n_p, SLAB_GROUPS * c_g)

    bs = jnp.concatenate([diag_b(bb_re), diag_b(bb_im)], axis=-1)
    cs = jnp.concatenate([diag_c(c_re), -diag_c(c_im)], axis=1)
    lam = jnp.concatenate([lr.reshape(ns, 1, -1), li.reshape(ns, 1, -1)], axis=-1)
    return bs, cs, lam


def _s5_powers(a_re, a_im, log_dt, n):
    n_g, n_p = a_re.shape
    ns = n_g // SLAB_GROUPS
    dt = jnp.exp(log_dt)[:, None]
    mag = jnp.exp(a_re * dt)
    base_r, base_i = mag * jnp.cos(a_im * dt), mag * jnp.sin(a_im * dt)
    steps = jnp.arange(1, n + 1, dtype=jnp.int32)[:, None, None]
    pr, pi = jnp.ones((n, n_g, n_p), F32), jnp.zeros((n, n_g, n_p), F32)
    for b in range(n.bit_length()):
        take = ((steps >> b) & 1) == 1
        pr, pi = (jnp.where(take, pr * base_r - pi * base_i, pr), jnp.where(take, pr * base_i + pi * base_r, pi))
        base_r, base_i = base_r * base_r - base_i * base_i, 2.0 * base_r * base_i

    def slabs(re, im):
        one = lambda m: jnp.transpose(m.reshape(n, ns, SLAB_GROUPS * n_p), (1, 0, 2))
        return jnp.concatenate([one(re), one(im)], axis=-1)

    return slabs(pr, pi), slabs(pr[::-1], -pi[::-1])


def _lanes(j):
    return slice(j * LANES, (j + 1) * LANES)


def _tile8(k):
    return pl.ds(pl.multiple_of(k * SUBLANES, SUBLANES), SUBLANES)


def _s5_interleave(dst_ref, src_ref, t_seg):
    def body(k, _):
        dst_ref[_tile8(k), :] = src_ref[pl.ds(k, SUBLANES, stride=t_seg), :]
        return 0

    lax.fori_loop(0, t_seg, body, 0, unroll=4)


def _s5_join_segments(st_ref, end_ref, car_ref, tab_ref, row, order, n_pair):
    for j in range(n_pair):
        re, im = _lanes(j), _lanes(n_pair + j)
        cr, ci = st_ref[:, re], st_ref[:, im]
        tr, ti = tab_ref[row:row + 1, re], tab_ref[row:row + 1, im]
        for s in order:
            car_ref[s:s + 1, re] = cr
            car_ref[s:s + 1, im] = ci
            er, ei = end_ref[s:s + 1, re], end_ref[s:s + 1, im]
            cr, ci = er + tr * cr - ti * ci, ei + tr * ci + ti * cr
        st_ref[:, re] = cr
        st_ref[:, im] = ci


def _s5_add_carries(buf_ref, car_ref, tab_ref, t_seg, n_pair):
    def fix(k, _):
        rows = _tile8(k)
        tab = tab_ref[pl.ds(k, 1), :]
        for j in range(n_pair):
            re, im = _lanes(j), _lanes(n_pair + j)
            cr, ci = car_ref[:, re], car_ref[:, im]
            tr, ti = tab[:, re], tab[:, im]
            buf_ref[rows, re] += tr * cr - ti * ci
            buf_ref[rows, im] += tr * ci + ti * cr
        return 0

    lax.fori_loop(0, t_seg, fix, 0, unroll=2)


def _s5_scan_fwd(proj, u_col, bs, cs, lam, pw, t_blk, beside=None):
    seq = proj.shape[0]
    ns, _, w2 = bs.shape
    n_pair = w2 // (2 * LANES)
    t_seg, n_t = t_blk // SUBLANES, seq // t_blk

    def body(u_ref, bs_ref, cs_ref, lam_ref, pw_ref, yc_ref, h_ref, st_ref, end_ref, car_ref, ui_ref, bu_ref, yi_ref):
        @pl.when(pl.program_id(1) == 0)
        def _():
            st_ref[...] = jnp.zeros_like(st_ref)

        _s5_interleave(ui_ref, u_ref, t_seg)
        bu_ref[...] = lax.dot_general(ui_ref[...].astype(BF16), bs_ref[...], NN, preferred_element_type=F32)
        lam_r = [jnp.broadcast_to(lam_ref[:, _lanes(j)], (SUBLANES, LANES)) for j in range(n_pair)]
        lam_i = [jnp.broadcast_to(lam_ref[:, _lanes(n_pair + j)], (SUBLANES, LANES)) for j in range(n_pair)]

        def step(k, c):
            rows = _tile8(k)
            out = []
            for j in range(n_pair):
                hr, hi = c[2 * j], c[2 * j + 1]
                nr = lam_r[j] * hr - lam_i[j] * hi + bu_ref[rows, _lanes(j)]
                ni = lam_i[j] * hr + lam_r[j] * hi + bu_ref[rows, _lanes(n_pair + j)]
                h_ref[rows, _lanes(j)] = nr
                h_ref[rows, _lanes(n_pair + j)] = ni
                out += [nr, ni]
            return tuple(out)

        ends = lax.fori_loop(0, t_seg, step, (jnp.zeros((SUBLANES, LANES), F32),) * (2 * n_pair), unroll=4)
        for j in range(n_pair):
            end_ref[:, _lanes(j)] = ends[2 * j]
            end_ref[:, _lanes(n_pair + j)] = ends[2 * j + 1]
        _s5_join_segments(st_ref, end_ref, car_ref, pw_ref, t_seg - 1, list(range(SUBLANES)), n_pair)
        _s5_add_carries(h_ref, car_ref, pw_ref, t_seg, n_pair)
        yi_ref[...] = lax.dot_general(h_ref[...].astype(BF16), cs_ref[...], NN, preferred_element_type=F32)

        def scatter(k, _):
            yc_ref[pl.ds(k, SUBLANES, stride=t_seg), :] = yi_ref[_tile8(k), :]
            return 0

        lax.fori_loop(0, t_seg, scatter, 0, unroll=4)

    return _call_beside(
        beside, body, "s5_scan_fwd", (ns, n_t),
        [_spec((t_blk, LANES), lambda s, i: (i, u_col + s)),
         _spec((None, LANES, w2), lambda s, i: (s, 0, 0)),
         _spec((None, w2, LANES), lambda s, i: (s, 0, 0)),
         _spec((None, 1, w2), lambda s, i: (s, 0, 0)),
         _spec((None, t_seg, w2), lambda s, i: (s, 0, 0))],
        [_spec((t_blk, LANES), lambda s, i: (i, s)),
         _spec((None, t_blk, w2), lambda s, i: (s, i, 0))],
        [jax.ShapeDtypeStruct((seq, ns * LANES), F32), jax.ShapeDtypeStruct((ns, seq, w2), F32)],
        [pltpu.VMEM((1, w2), F32), pltpu.VMEM((SUBLANES, w2), F32), pltpu.VMEM((SUBLANES, w2), F32),
         pltpu.VMEM((t_blk, LANES), F32), pltpu.VMEM((t_blk, w2), F32), pltpu.VMEM((t_blk, LANES), F32)],
        _vmem_limit(3 * t_blk * w2 * 4), (proj, bs, cs, lam, pw), ("parallel", "arbitrary"))


def _s5_scan_bwd(proj, u_col, states, d_yc, du_extra, bs, cs, lam, qw, t_blk):
    seq = proj.shape[0]
    ns, _, w2 = bs.shape
    n_pair = w2 // (2 * LANES)
    t_seg, n_t = t_blk // SUBLANES, seq // t_blk

    def body(u_ref, h_ref, hp_ref, dyc_ref, dux_ref, bs_ref, cs_ref, lam_ref, qw_ref,
             du_ref, dbs_ref, dcs_ref, dlam_ref, g_ref, gd_ref, st_ref, end_ref, car_ref, ui_ref, dyi_ref, dui_ref):
        i = pl.program_id(1)

        @pl.when(i == 0)
        def _():
            st_ref[...] = jnp.zeros_like(st_ref)
            dbs_ref[...] = jnp.zeros_like(dbs_ref)
            dcs_ref[...] = jnp.zeros_like(dcs_ref)
            dlam_ref[...] = jnp.zeros_like(dlam_ref)

        _s5_interleave(ui_ref, u_ref, t_seg)
        _s5_interleave(dyi_ref, dyc_ref, t_seg)
        dyc_b = dyi_ref[...].astype(BF16)
        gd_ref[...] = lax.dot_general(dyc_b, cs_ref[...], NT, preferred_element_type=F32)
        lam_r = [jnp.broadcast_to(lam_ref[:, _lanes(j)], (SUBLANES, LANES)) for j in range(n_pair)]
        lam_i = [jnp.broadcast_to(lam_ref[:, _lanes(n_pair + j)], (SUBLANES, LANES)) for j in range(n_pair)]

        def step(kk, c):
            rows = _tile8(t_seg - 1 - kk)
            out = []
            for j in range(n_pair):
                gr_n, gi_n = c[2 * j], c[2 * j + 1]
                gr = gd_ref[rows, _lanes(j)] + lam_r[j] * gr_n + lam_i[j] * gi_n
                gi = gd_ref[rows, _lanes(n_pair + j)] + lam_r[j] * gi_n - lam_i[j] * gr_n
                g_ref[rows, _lanes(j)] = gr
                g_ref[rows, _lanes(n_pair + j)] = gi
                out += [gr, gi]
            return tuple(out)

        zero = jnp.zeros((SUBLANES, LANES), F32)
        firsts = lax.fori_loop(0, t_seg, step, (zero,) * (2 * n_pair), unroll=4)
        for j in range(n_pair):
            end_ref[:, _lanes(j)] = firsts[2 * j]
            end_ref[:, _lanes(n_pair + j)] = firsts[2 * j + 1]
        _s5_join_segments(st_ref, end_ref, car_ref, qw_ref, 0, list(range(SUBLANES))[::-1], n_pair)
        _s5_add_carries(g_ref, car_ref, qw_ref, t_seg, n_pair)

        def pair_up(k, c):
            rows, prev = _tile8(k), _tile8(k - 1)
            out = []
            for j in range(n_pair):
                re, im = _lanes(j), _lanes(n_pair + j)
                gr, gi, hr, hi = g_ref[rows, re], g_ref[rows, im], h_ref[prev, re], h_ref[prev, im]
                out += [c[2 * j] + gr * hr + gi * hi, c[2 * j + 1] + gi * hr - gr * hi]
            return tuple(out)

        acc = lax.fori_loop(1, t_seg, pair_up, (zero,) * (2 * n_pair), unroll=4)
        has_prev = (i < n_t - 1).astype(F32)
        first_seg = lax.broadcasted_iota(jnp.int32, (SUBLANES, LANES), 0) == 0
        last = _tile8(t_seg - 1)
        for j in range(n_pair):
            re, im = _lanes(j), _lanes(n_pair + j)
            gr, gi = g_ref[0:SUBLANES, re], g_ref[0:SUBLANES, im]
            hr = jnp.where(first_seg, hp_ref[SUBLANES - 1:, re] * has_prev, pltpu.roll(h_ref[last, re], 1, 0))
            hi = jnp.where(first_seg, hp_ref[SUBLANES - 1:, im] * has_prev, pltpu.roll(h_ref[last, im], 1, 0))
            dlam_ref[:, re] += jnp.sum(acc[2 * j] + gr * hr + gi * hi, axis=0, keepdims=True)
            dlam_ref[:, im] += jnp.sum(acc[2 * j + 1] + gi * hr - gr * hi, axis=0, keepdims=True)

        g_b = g_ref[...].astype(BF16)
        dui_ref[...] = lax.dot_general(g_b, bs_ref[...], NT, preferred_element_type=F32)
        dbs_ref[...] += lax.dot_general(ui_ref[...].astype(BF16), g_b, TN, preferred_element_type=F32)
        dcs_ref[...] += lax.dot_general(h_ref[...].astype(BF16), dyc_b, TN, preferred_element_type=F32)

        def scatter(k, _):
            rows = pl.ds(k, SUBLANES, stride=t_seg)
            du_ref[rows, :] = (dui_ref[_tile8(k), :] + dux_ref[rows, :]).astype(du_ref.dtype)
            return 0

        lax.fori_loop(0, t_seg, scatter, 0, unroll=4)

    rev = lambda i: n_t - 1 - i
    return pl.pallas_call(
        body, name="s5_scan_bwd", grid=(ns, n_t),
        in_specs=[_spec((t_blk, LANES), lambda s, i: (rev(i), u_col + s)),
                  _spec((None, t_blk, w2), lambda s, i: (s, rev(i), 0)),
                  _spec((None, SUBLANES, w2), lambda s, i: (s, jnp.maximum(rev(i) * t_seg - 1, 0), 0)),
                  _spec((t_blk, LANES), lambda s, i: (rev(i), s)),
                  _spec((t_blk, LANES), lambda s, i: (rev(i), s)),
                  _spec((None, LANES, w2), lambda s, i: (s, 0, 0)),
                  _spec((None, w2, LANES), lambda s, i: (s, 0, 0)),
                  _spec((None, 1, w2), lambda s, i: (s, 0, 0)),
                  _spec((None, t_seg, w2), lambda s, i: (s, 0, 0))],
        out_specs=[_spec((t_blk, LANES), lambda s, i: (rev(i), s)),
                   _spec((None, LANES, w2), lambda s, i: (s, 0, 0)),
                   _spec((None, w2, LANES), lambda s, i: (s, 0, 0)),
                   _spec((None, 1, w2), lambda s, i: (s, 0, 0))],
        out_shape=[jax.ShapeDtypeStruct((seq, ns * LANES), F32), jax.ShapeDtypeStruct(bs.shape, F32),
                   jax.ShapeDtypeStruct(cs.shape, F32), jax.ShapeDtypeStruct(lam.shape, F32)],
        scratch_shapes=[pltpu.VMEM((t_blk, w2), F32), pltpu.VMEM((t_blk, w2), F32), pltpu.VMEM((1, w2), F32),
                        pltpu.VMEM((SUBLANES, w2), F32), pltpu.VMEM((SUBLANES, w2), F32),
                        pltpu.VMEM((t_blk, LANES), F32), pltpu.VMEM((t_blk, LANES), F32), pltpu.VMEM((t_blk, LANES), F32)],
        compiler_params=pltpu.CompilerParams(dimension_semantics=("parallel", "arbitrary"),
                                             vmem_limit_bytes=_vmem_limit(5 * t_blk * w2 * 4)),
    )(*[_in_hbm(a) for a in (proj, states, states, d_yc, du_extra, bs, cs, lam, qw)])


def _loss_head(y, target, t_m):
    seq, d = y.shape

    def body(y_ref, t_ref, loss_ref, dy_ref):
        @pl.when(pl.program_id(0) == 0)
        def _():
            loss_ref[...] = jnp.zeros_like(loss_ref)

        diff = y_ref[...] - t_ref[...]
        dy_ref[...] = diff / d
        loss_ref[...] += 0.5 * jnp.sum(diff * diff) / d

    row = _spec((t_m, d), lambda i: (i, 0))
    return pl.pallas_call(
        body, name="loss_head", grid=(seq // t_m,), in_specs=[row, row],
        out_specs=[_spec((SUBLANES, LANES), lambda i: (0, 0)), row],
        out_shape=[jax.ShapeDtypeStruct((SUBLANES, LANES), F32), jax.ShapeDtypeStruct((seq, d), F32)],
        compiler_params=pltpu.CompilerParams(dimension_semantics=("arbitrary",),
                                             vmem_limit_bytes=_vmem_limit(6 * t_m * d * 4)),
    )(_in_hbm(y), _in_hbm(target))


def _adamw_fn(w, m, v, *partials):
    g = partials[0]
    for p in partials[1:]:
        g = g + p
    m2 = ADAM_B1 * m + (1.0 - ADAM_B1) * g
    v2 = ADAM_B2 * v + (1.0 - ADAM_B2) * (g * g)
    m_hat = m2 / (1.0 - ADAM_B1 ** ADAM_STEP)
    v_hat = v2 / (1.0 - ADAM_B2 ** ADAM_STEP)
    delta = -ADAM_LR * (m_hat / (jnp.sqrt(v_hat) + ADAM_EPS) + ADAM_WD * w)
    return g, delta, m2, v2


def _adamw(name, w, m, v, partials):
    rows, cols = w.shape
    t_r = rows
    for cand in (512, 256, 128, 64, 32, 16, 8):
        if rows % cand == 0 and cand * cols * 4 <= (1 << 20):
            t_r = cand
            break
    n_p = partials.shape[0]
    row = lambda i: (i, 0)
    ins = [(a, (t_r, cols), row) for a in (w, m, v)]
    ins += [(partials, (None, t_r, cols), (lambda i, j=j: (j, i, 0))) for j in range(n_p)]
    outs = [((rows, cols), F32, (t_r, cols), row)] * 4
    return _rowwise(name, _adamw_fn, ins, outs, (rows // t_r,))


SMALL_PARAMS = ("b_ada", "ssm_a_re", "ssm_a_im", "ssm_log_dt", "ssm_b_re", "ssm_b_im", "ssm_c_re", "ssm_c_im",
                "ssm_d", "b_glu", "ln1_g", "ln1_b", "ln2_g", "ln2_b")
WEIGHTS = ("w_ada", "b_ada", "w_in", "w_sb_up", "ssm_a_re", "ssm_a_im", "ssm_log_dt", "ssm_b_re", "ssm_b_im",
           "ssm_c_re", "ssm_c_im", "ssm_d", "w_glu", "b_glu", "w_ssm_up", "w_out", "ln1_g", "ln1_b", "w_ffn_in",
           "w_ffn_out", "ln2_g", "ln2_b")
ARG_NAMES = (("x", "c") + WEIGHTS + ("loss_target",) + tuple("m_" + n for n in WEIGHTS)
             + tuple("v_" + n for n in WEIGHTS))


def _pack(arrs):
    flat = jnp.concatenate([a.reshape(-1) for a in arrs])
    pad = (-flat.shape[0]) % (PACK_ROWS * LANES)
    return jnp.pad(flat, (0, pad)).reshape(-1, LANES)


def _unpack(packed, like):
    lead = packed.shape[:-2]
    flat = packed.reshape(lead + (-1,))
    out, off = [], 0
    for a in like:
        out.append(flat[..., off:off + a.size].reshape(lead + a.shape))
        off += a.size
    return out


def kernel(x, c, w_ada, b_ada, w_in, w_sb_up, ssm_a_re, ssm_a_im, ssm_log_dt, ssm_b_re, ssm_b_im, ssm_c_re,
           ssm_c_im, ssm_d, w_glu, b_glu, w_ssm_up, w_out, ln1_g, ln1_b, w_ffn_in, w_ffn_out, ln2_g, ln2_b,
           loss_target, m_w_ada, m_b_ada, m_w_in, m_w_sb_up, m_ssm_a_re, m_ssm_a_im, m_ssm_log_dt, m_ssm_b_re,
           m_ssm_b_im, m_ssm_c_re, m_ssm_c_im, m_ssm_d, m_w_glu, m_b_glu, m_w_ssm_up, m_w_out, m_ln1_g, m_ln1_b,
           m_w_ffn_in, m_w_ffn_out, m_ln2_g, m_ln2_b, v_w_ada, v_b_ada, v_w_in, v_w_sb_up, v_ssm_a_re, v_ssm_a_im,
           v_ssm_log_dt, v_ssm_b_re, v_ssm_b_im, v_ssm_c_re, v_ssm_c_im, v_ssm_d, v_w_glu, v_b_glu, v_w_ssm_up,
           v_w_out, v_ln1_g, v_ln1_b, v_w_ffn_in, v_w_ffn_out, v_ln2_g, v_ln2_b):
    given = locals()
    return _train_step({n: given[n] for n in ARG_NAMES})


def _train_step(p):
    x0 = p["x"][0]
    target = p["loss_target"][0]
    seq, d = x0.shape
    depth = p["w_ada"].shape[0]
    n_ada = p["w_ada"].shape[2]
    n_in = p["w_in"].shape[2]
    sb_w = p["w_sb_up"].shape[1]
    ssm_w = p["w_ssm_up"].shape[1]
    n_up = p["w_sb_up"].shape[2]
    n_ffn = p["w_ffn_in"].shape[2]
    ffn = N_DEV * p["w_ffn_out"].shape[1]
    in_cols = N_DEV * n_in
    alpha = (2 * depth) ** 0.25
    resid_ln, resid_ln_mod = _make_resid_fns(alpha)
    t_r = min(512, seq)
    n_r = seq // t_r
    t_m = min(1024, seq)
    n_m = seq // t_m
    t_d = _tile(d)
    assert n_ffn * (N_DEV // 2) == ffn and sb_w % LANES == 0 and ssm_w % LANES == 0 and d % LANES == 0
    assert n_in % LANES == 0 and n_up % LANES == 0 and seq % t_m == 0 and in_cols == 3 * sb_w + ssm_w + 2 * d
    assert (3 * sb_w) % ssm_w == 0 and (3 * sb_w + ssm_w) % (2 * d) == 0
    assert sb_w % n_in == 0 and ssm_w % n_in == 0 and d % n_in == 0
    proj_starts = [c // n_in for c in (0, sb_w, 2 * sb_w, 3 * sb_w, 3 * sb_w + ssm_w)]

    bf = lambda a: a.astype(BF16)
    got = _exchange("gather_first", [], [bf(p["w_in"][0]), p["c"]])
    wg_in = [got[0]] + [None] * (depth - 1)
    c_all = got[1].reshape(N_DEV, d)
    small_names = ("w_sb_up", "w_ssm_up", "w_glu", "w_out")
    wg_ffn_in, wg_ffn_out, wg = [None] * depth, [None] * depth, {}

    c_pad = jnp.pad(c_all, ((0, 2 * SUBLANES - N_DEV), (0, 0)))
    c_act = _rowwise("silu_c", lambda v: v * jax.nn.sigmoid(v), [(c_pad, c_pad.shape, lambda i: (0, 0))],
                     [(c_pad.shape, F32, c_pad.shape, lambda i: (0, 0))], (1,))[0]
    rows_c = c_pad.shape[0]
    mod_cols = [
        _mm(f"mod_{l}", c_act, p["w_ada"],
            _spec((rows_c, d), lambda i, j, k: (0, 0)), _spec((None, d, n_ada), lambda i, j, k, l=l: (l, 0, 0)),
            _spec((rows_c, n_ada), lambda i, j, k: (0, 0)), (rows_c, n_ada), F32, (1, 1, 1), NN)
        for l in range(depth)]
    mod_send = jnp.stack([m[:N_DEV] for m in mod_cols], axis=1)
    mod_recv = _exchange("exchange_mod", [mod_send], [])[0]
    mod_nobias = jnp.swapaxes(mod_recv, 0, 1).reshape(depth, N_DEV * n_ada)
    full2 = lambda a: (a, a.shape, lambda i: (0, 0))
    mod = _rowwise("mod_bias", lambda a, b: a + b, [full2(mod_nobias), full2(p["b_ada"])],
                   [(mod_nobias.shape, F32, mod_nobias.shape, lambda i: (0, 0))], (1,))[0]
    vec = lambda a: a.reshape(1, -1)
    mods = [[vec(mod[l, j * d:(j + 1) * d]) for j in range(6)] for l in range(depth)]
    ln = {n: [vec(p[n][l]) for l in range(depth)] for n in ("ln1_g", "ln1_b", "ln2_g", "ln2_b")}

    row_spec = lambda width: ((t_r, width), lambda i: (i, 0))
    col_spec = lambda width, cb: ((t_r, width), lambda i, cb=cb: (i, cb))
    vec_spec = lambda width: ((1, width), lambda i: (0, 0))
    rows_in = lambda a: (a,) + row_spec(a.shape[1])
    vec_in = lambda a: (a,) + vec_spec(a.shape[1])
    row_out = lambda width, dt: ((seq, width), dt) + row_spec(width)

    s5 = [_s5_discretize(*[p[n][l] for n in ("ssm_a_re", "ssm_a_im", "ssm_log_dt", "ssm_b_re", "ssm_b_im",
                                               "ssm_c_re", "ssm_c_im")]) for l in range(depth)]
    s5_b16 = [(bs.astype(BF16), cs.astype(BF16), lam) for bs, cs, lam in s5]
    t_scan = min(1024, seq)
    s5_pw = [_s5_powers(p["ssm_a_re"][l], p["ssm_a_im"][l], p["ssm_log_dt"][l], t_scan // SUBLANES)
             for l in range(depth)]
    u_col = 3 * sb_w // LANES
    gates_cb = (3 * sb_w + ssm_w) // (2 * d)
    ssm_d = [vec(p["ssm_d"][l]) for l in range(depth)]
    b_glu = [vec(p["b_glu"][l]) for l in range(depth)]
    n_half = N_DEV // 2

    h = _rowwise("modulate_in", _modulate, [rows_in(x0), vec_in(mods[0][1]), vec_in(mods[0][0])],
                 [row_out(d, BF16)], (n_r,))[0]
    saved = []
    x_cur = x0
    for l in range(depth):
        sv = {"x_in": x_cur, "h": h}
        last = l == depth - 1
        t_n = _tile(n_in)
        r_n = n_in // t_n
        proj = _mm(f"proj_{l}", h, wg_in[l],
                   _spec((t_m, d), lambda i, j, k: (i, 0)),
                   _spec((None, d, t_n), lambda i, j, k, r=r_n: (j // r, 0, j % r)),
                   _spec((t_m, t_n), lambda i, j, k: (i, j)), (seq, in_cols), F32, (n_m, N_DEV * r_n, 1), NN)
        arriving = [bf(p["w_ffn_in"][l]), bf(p["w_ffn_out"][l])] + ([bf(p[n]) for n in small_names] if l == 0 else [])
        (o_sb, o_sb32), got = _sb_attention_fwd(proj, sb_w, beside=_Exchange(gather=arriving))
        wg_ffn_in[l] = got[0]
        wg_ffn_out[l] = got[1].reshape(n_half, n_ffn, d)
        if l == 0:
            wg = dict(zip(small_names, got[2:]))
            for n in ("w_glu", "w_out"):
                wg[n] = jnp.swapaxes(wg[n], 0, 1).reshape(depth, -1, wg[n].shape[-1])
            for n in ("w_sb_up", "w_ssm_up"):
                wg[n] = jnp.transpose(wg[n], (1, 2, 0, 3)).reshape(depth, wg[n].shape[2], d)
        bs16, cs16, lam = s5_b16[l]
        (yc, states), got = _s5_scan_fwd(proj, u_col, bs16, cs16, lam, s5_pw[l][0], t_scan,
                                         beside=None if last else _Exchange(gather=[bf(p["w_in"][l + 1])]))
        if not last:
            wg_in[l + 1] = got[0]
        u_in = (proj,) + col_spec(ssm_w, 3 * sb_w // ssm_w)
        y1 = _rowwise(f"s5_act_{l}", _s5_act_fn, [rows_in(yc), u_in, vec_in(ssm_d[l])],
                      [row_out(ssm_w, BF16)], (n_r,))[0]
        t_glu = _mm(f"s5_glu_mm_{l}", y1, wg["w_glu"],
                    _spec((t_m, ssm_w), lambda i, j, k: (i, 0)), _spec((None, ssm_w, ssm_w), lambda i, j, k, l=l: (l, 0, 0)),
                    _spec((t_m, ssm_w), lambda i, j, k: (i, 0)), (seq, ssm_w), F32, (n_m, 1, 1), NN)
        s5_out = _rowwise(f"s5_glu_{l}", _s5_glu_fn,
                          [rows_in(yc), u_in, rows_in(t_glu), vec_in(ssm_d[l]), vec_in(b_glu[l])],
                          [row_out(ssm_w, BF16)], (n_r,))[0]

        merged, y_sb, y_ssm = _up_merge(f"up_merge_{l}", o_sb, s5_out, proj, gates_cb, wg["w_sb_up"], wg["w_ssm_up"],
                                        l, t_r)
        y_mix = _mm(f"out_proj_{l}", merged, wg["w_out"],
                    _spec((t_m, d), lambda i, j, k: (i, 0)), _spec((None, d, t_d), lambda i, j, k, l=l: (l, 0, j)),
                    _spec((t_m, t_d), lambda i, j, k: (i, j)), (seq, d), F32, (n_m, d // t_d, 1), NN)
        vecs_a = [mods[l][2], ln["ln1_g"][l], ln["ln1_b"][l], mods[l][4], mods[l][3]]
        x_mid, h2 = _rowwise(f"resid_mix_{l}", resid_ln_mod, [rows_in(x_cur), rows_in(y_mix)] + [vec_in(v) for v in vecs_a],
                             [row_out(d, F32), row_out(d, BF16)], (n_r,))
        a_ffn, f_act = _ffn_in_swiglu(f"ffn_in_{l}", h2, wg_ffn_in[l], t_r)
        y_ffn = _mm(f"ffn_out_{l}", f_act, wg_ffn_out[l],
                    _spec((None, t_m, n_ffn), lambda i, j, k: (k, i, 0)),
                    _spec((None, n_ffn, t_d), lambda i, j, k: (k, 0, j)),
                    _spec((t_m, t_d), lambda i, j, k: (i, j)), (seq, d), F32, (n_m, d // t_d, n_half), NN)
        vecs_b = [mods[l][5], ln["ln2_g"][l], ln["ln2_b"][l]] + ([] if last else [mods[l + 1][1], mods[l + 1][0]])
        outs_b = [row_out(d, F32)] + ([] if last else [row_out(d, BF16)])
        res = _rowwise(f"resid_ffn_{l}", resid_ln if last else resid_ln_mod,
                       [rows_in(x_mid), rows_in(y_ffn)] + [vec_in(v) for v in vecs_b], outs_b, (n_r,))
        sv.update(proj=proj, o_sb=o_sb, o_sb32=o_sb32, yc=yc, states=states, y1=y1, t_glu=t_glu, s5_out=s5_out,
                  y_sb=y_sb, y_ssm=y_ssm, merged=merged, y_mix=y_mix, x_mid=x_mid, h2=h2, a_ffn=a_ffn, f_act=f_act,
                  y_ffn=y_ffn, vecs_a=vecs_a, vecs_b=vecs_b)
        saved.append(sv)
        x_cur = res[0]
        h = None if last else res[1]

    loss_part, d_x = _loss_head(x_cur, target, t_r)
    loss = lax.psum(loss_part[0, 0], MESH_AXES)

    d_h_next = None
    grads = {n: [None] * depth for n in WEIGHTS}
    d_mod = [[None] * 6 for _ in range(depth)]
    land = {}
    waiting = []
    row_wrt = lambda i, width, dt: (i, "row", (seq, width), dt) + row_spec(width)
    sum_wrt = lambda i, width: (i, "sum", (1, width), F32) + vec_spec(width)
    for l in reversed(range(depth)):
        sv = saved[l]
        last = l == depth - 1
        ins_b = [rows_in(sv["x_mid"]), rows_in(sv["y_ffn"])] + [vec_in(v) for v in sv["vecs_b"]]
        cts_b = [rows_in(d_x)] + ([] if last else [rows_in(d_h_next)])
        wrt_b = [row_wrt(0, d, F32), row_wrt(1, d, BF16)] + [sum_wrt(2 + j, d) for j in range(len(sv["vecs_b"]))]
        res = _rowwise_vjp(f"resid_ffn_bwd_{l}", resid_ln if last else resid_ln_mod, ins_b, cts_b, wrt_b, (n_r,))
        d_x_mid, d_y_ffn = res[0], res[1]
        d_mod[l][5], grads["ln2_g"][l], grads["ln2_b"][l] = res[2], res[3], res[4]
        if not last:
            d_mod[l + 1][1], d_mod[l + 1][0] = res[5], res[6]
        d_a = _ffn_out_dx_swiglu(f"ffn_out_dx_{l}", d_y_ffn, wg_ffn_out[l], sv["a_ffn"], t_r).reshape(N_DEV, seq, n_ffn)
        g_ffn_out = _mm(f"ffn_out_dw_{l}", sv["f_act"], d_y_ffn,
                        _spec((None, t_m, n_ffn), lambda i, j, k: (i, k, 0)), _spec((t_m, t_d), lambda i, j, k: (k, j)),
                        _spec((None, n_ffn, t_d), lambda i, j, k: (i, 0, j)), (n_half, n_ffn, d), GRAD_WIRE,
                        (n_half, d // t_d, n_m), TN, reread=(False, True))
        d_h2 = _mm(f"ffn_in_dx_{l}", d_a, wg_ffn_in[l],
                   _spec((None, t_m, n_ffn), lambda i, j, k: (k, i, 0)),
                   _spec((None, t_d, n_ffn), lambda i, j, k: (k, j, 0)),
                   _spec((t_m, t_d), lambda i, j, k: (i, j)), (seq, d), BRANCH_CT, (n_m, d // t_d, N_DEV), NT)
        g_ffn_in = _mm(f"ffn_in_dw_{l}", sv["h2"], d_a,
                       _spec((t_m, t_d), lambda i, j, k: (k, j)), _spec((None, t_m, n_ffn), lambda i, j, k: (i, k, 0)),
                       _spec((None, t_d, n_ffn), lambda i, j, k: (i, j, 0)), (N_DEV, d, n_ffn), GRAD_WIRE,
                       (N_DEV, d // t_d, n_m), TN, reread=(True, False))
        ins_a = [rows_in(sv["x_in"]), rows_in(sv["y_mix"])] + [vec_in(v) for v in sv["vecs_a"]]
        wrt_a = [row_wrt(0, d, F32), row_wrt(1, d, BF16)] + [sum_wrt(2 + j, d) for j in range(5)]
        res = _rowwise_vjp(f"resid_mix_bwd_{l}", resid_ln_mod, ins_a, [rows_in(d_x_mid), rows_in(d_h2)], wrt_a, (n_r,))
        d_x_in, d_y_mix = res[0], res[1]
        d_mod[l][2], grads["ln1_g"][l], grads["ln1_b"][l], d_mod[l][4], d_mod[l][3] = res[2:7]
        d_merged = _mm(f"out_proj_dx_{l}", d_y_mix, wg["w_out"],
                       _spec((t_m, d), lambda i, j, k: (i, 0)), _spec((None, t_d, d), lambda i, j, k, l=l: (l, j, 0)),
                       _spec((t_m, t_d), lambda i, j, k: (i, j)), (seq, d), BRANCH_CT, (n_m, d // t_d, 1), NT)
        g_out = _mm(f"out_proj_dw_{l}", sv["merged"], d_y_mix,
                    _spec((t_m, t_d), lambda i, j, k: (k, i)), _spec((t_m, t_d), lambda i, j, k: (k, j)),
                    _spec((t_d, t_d), lambda i, j, k: (i, j)), (d, d), GRAD_WIRE, (d // t_d, d // t_d, n_m), TN, reread=(d > t_d, d > t_d))
        gates = (sv["proj"],) + col_spec(2 * d, gates_cb)
        d_y_sb, d_y_ssm, d_gates = _rowwise_vjp(
            f"merge_bwd_{l}", _merge_fn, [rows_in(sv["y_sb"]), rows_in(sv["y_ssm"]), gates], [rows_in(d_merged)],
            [row_wrt(0, d, BF16), row_wrt(1, d, BF16), row_wrt(2, 2 * d, BF16)], (n_r,))

        def up_bwd(name, act, d_y, w, dx_dtype, l=l):
            k_w = act.shape[1]
            dx = _mm(name + "_dx", d_y, w, _spec((t_m, d), lambda i, j, k: (i, 0)),
                     _spec((None, k_w, d), lambda i, j, k: (l, 0, 0)),
                     _spec((t_m, k_w), lambda i, j, k: (i, 0)), (seq, k_w), dx_dtype, (n_m, 1, 1), NT)
            dw = _mm(name + "_dw", act, d_y, _spec((t_m, k_w), lambda i, j, k: (k, 0)),
                     _spec((t_m, t_d), lambda i, j, k: (k, j)),
                     _spec((k_w, t_d), lambda i, j, k: (0, j)), (k_w, d), GRAD_WIRE, (1, d // t_d, n_m), TN,
                     reread=(d > t_d, False))
            return dx, jnp.swapaxes(dw.reshape(k_w, N_DEV, n_up), 0, 1)

        d_o_sb, g_sb_up = up_bwd(f"sb_up_{l}", sv["o_sb"], d_y_sb, wg["w_sb_up"], BF16)
        d_s5_out, g_ssm_up = up_bwd(f"ssm_up_{l}", sv["s5_out"], d_y_ssm, wg["w_ssm_up"], BRANCH_CT)
        waiting += [("w_ffn_in", g_ffn_in), ("w_ffn_out", g_ffn_out.reshape(N_DEV, -1, d)),
                    ("w_out", g_out.reshape(N_DEV, -1, d)), ("w_sb_up", g_sb_up), ("w_ssm_up", g_ssm_up)]
        levels = [l + 1] * (len(waiting) - 5) + [l] * 5
        (d_q, d_k, d_v), got = _sb_attention_bwd(
            sv["proj"], sv["o_sb32"], d_o_sb, sb_w,
            beside=_Exchange(layered=[(g, lv, depth, land.get(n)) for (n, g), lv in zip(waiting, levels)]))
        land.update({n: buf for (n, _), buf in zip(waiting, got)})
        u_in = (sv["proj"],) + col_spec(ssm_w, 3 * sb_w // ssm_w)
        ins_s5 = [rows_in(sv["yc"]), u_in, rows_in(sv["t_glu"]), vec_in(ssm_d[l]), vec_in(b_glu[l])]
        d_t = _rowwise_vjp(f"s5_glu_bwd_{l}", _s5_glu_fn, ins_s5, [rows_in(d_s5_out)],
                           [row_wrt(2, ssm_w, BF16)], (n_r,))[0]
        d_y1 = _mm(f"s5_glu_mm_dx_{l}", d_t, wg["w_glu"],
                   _spec((t_m, ssm_w), lambda i, j, k: (i, 0)), _spec((None, ssm_w, ssm_w), lambda i, j, k, l=l: (l, 0, 0)),
                   _spec((t_m, ssm_w), lambda i, j, k: (i, 0)), (seq, ssm_w), BRANCH_CT, (n_m, 1, 1), NT)
        g_glu = _mm(f"s5_glu_mm_dw_{l}", sv["y1"], d_t,
                    _spec((t_m, ssm_w), lambda i, j, k: (k, 0)), _spec((t_m, ssm_w), lambda i, j, k: (k, 0)),
                    _spec((ssm_w, ssm_w), lambda i, j, k: (0, 0)), (ssm_w, ssm_w), GRAD_WIRE, (1, 1, n_m), TN, reread=(False, False))
        d_yc, d_u_skip, grads["ssm_d"][l], grads["b_glu"][l] = _rowwise_vjp(
            f"s5_post_bwd_{l}", _s5_post_fn, ins_s5, [rows_in(d_y1), rows_in(d_s5_out)],
            [row_wrt(0, ssm_w, F32), row_wrt(1, ssm_w, F32), sum_wrt(3, ssm_w), sum_wrt(4, ssm_w)], (n_r,))
        bs16, cs16, lam = s5_b16[l]
        d_u, d_bs, d_cs, d_lam = _s5_scan_bwd(sv["proj"], u_col, sv["states"], d_yc, d_u_skip, bs16, cs16, lam,
                                              s5_pw[l][1], t_scan)
        raw = [p[n][l] for n in ("ssm_a_re", "ssm_a_im", "ssm_log_dt", "ssm_b_re", "ssm_b_im", "ssm_c_re", "ssm_c_im")]
        _, pull = jax.vjp(_s5_discretize, *raw)
        (grads["ssm_a_re"][l], grads["ssm_a_im"][l], grads["ssm_log_dt"][l], grads["ssm_b_re"][l],
         grads["ssm_b_im"][l], grads["ssm_c_re"][l], grads["ssm_c_im"][l]) = pull((d_bs, d_cs, d_lam))
        d_proj = [d_q, d_k, d_v, d_u, d_gates]
        g_in = _mm_pieces(f"proj_dw_{l}", d_proj, proj_starts, n_in, lambda i, j, k: i, sv["h"],
                          _spec((t_m, t_d), lambda i, j, k: (k, j)), False, lambda i, j, k: k,
                          _spec((None, t_d, n_in), lambda i, j, k: (i, j, 0)), (N_DEV, d, n_in), GRAD_WIRE,
                          (N_DEV, d // t_d, n_m), TN)
        waiting = [("w_in", g_in), ("w_glu", g_glu.reshape(N_DEV, -1, ssm_w))]
        closing = _Exchange(layered=[(g, 0, depth, land.get(n)) for n, g in waiting]) if l == 0 else None
        d_h = _mm_pieces(f"proj_dx_{l}", d_proj, proj_starts, n_in, lambda i, j, k: k, wg_in[l],
                         _spec((None, t_d, n_in), lambda i, j, k: (k, j, 0)), True, lambda i, j, k: i,
                         _spec((t_m, t_d), lambda i, j, k: (i, j)), (seq, d), BRANCH_CT, (n_m, d // t_d, N_DEV), NT,
                         beside=closing)
        if l == 0:
            d_h, got = d_h
            land.update({n: buf for (n, _), buf in zip(waiting, got)})
        d_x, d_h_next = d_x_in, d_h
    res = _rowwise_vjp("modulate_in_bwd", lambda v, sc, sh: (v, _modulate(v, sc, sh)),
                       [rows_in(x0), vec_in(mods[0][1]), vec_in(mods[0][0])], [rows_in(d_x), rows_in(d_h_next)],
                       [row_wrt(0, d, F32), sum_wrt(1, d), sum_wrt(2, d)], (n_r,))
    grad_x, d_mod[0][1], d_mod[0][0] = res

    d_mod_rows = jnp.concatenate([jnp.concatenate(d_mod[l], axis=1) for l in range(depth)], axis=0)
    grads["b_ada"] = [d_mod_rows[l] for l in range(depth)]
    small_local = [jnp.stack([g.reshape(p[n].shape[1:]) for g in grads[n]]) for n in SMALL_PARAMS]
    d_mod_send = jnp.swapaxes(d_mod_rows.reshape(depth, N_DEV, n_ada), 0, 1)
    small_sum, (d_mod_cols,) = _reduce_packed("exchange_last", _pack(small_local), [d_mod_send])
    d_mod_pad = jnp.pad(jnp.swapaxes(d_mod_cols, 0, 1), ((0, 0), (0, rows_c - N_DEV), (0, 0)))
    g_ada = [
        _mm(f"mod_dw_{l}", c_act, d_mod_pad,
            _spec((rows_c, d), lambda i, j, k: (0, 0)), _spec((None, rows_c, n_ada), lambda i, j, k, l=l: (l, 0, 0)),
            _spec((d, n_ada), lambda i, j, k: (0, 0)), (d, n_ada), F32, (1, 1, 1), TN)
        for l in range(depth)]

    out = {}

    def update(name, partials):
        shape = p[name].shape
        two_d = lambda a: a.reshape(-1, shape[-1])
        res = _adamw("adamw_" + name, two_d(p[name]), two_d(p["m_" + name]), two_d(p["v_" + name]),
                     partials.reshape(partials.shape[0], -1, shape[-1]))
        out[name] = [r.reshape(shape) for r in res]

    update("w_ada", jnp.stack(g_ada)[None])
    for n in ("w_in", "w_sb_up", "w_ssm_up", "w_ffn_in", "w_glu", "w_out", "w_ffn_out"):
        update(n, land[n])
    small_w = [p[n] for n in SMALL_PARAMS]
    res = _adamw("adamw_small", _pack(small_w), _pack([p["m_" + n] for n in SMALL_PARAMS]),
                 _pack([p["v_" + n] for n in SMALL_PARAMS]), small_sum[None])
    for kind, packed in enumerate(res):
        for n, a in zip(SMALL_PARAMS, _unpack(packed, small_w)):
            out.setdefault(n, [None] * 4)[kind] = a

    return ((loss, grad_x[None]) + tuple(out[n][0] for n in WEIGHTS) + tuple(out[n][1] for n in WEIGHTS)
            + tuple(out[n][2] for n in WEIGHTS) + tuple(out[n][3] for n in WEIGHTS))
```

```python
import jax
import jax.numpy as jnp
from jax import lax
from jax.experimental import pallas as pl
from jax.experimental.pallas import tpu as pltpu

F32 = jnp.float32
BF16 = jnp.bfloat16
GRAD_WIRE = BF16
FFN_ACT = BF16
BRANCH_CT = BF16

N_DEV = 8
LANES = 128
SUBLANES = 8
VMEM_BYTES = 64 * 1024 * 1024
HEAD_DIM = 64
SB_BLOCK = 256
SLAB_GROUPS = 8
LN_EPS = 1e-5
ADAM_LR, ADAM_B1, ADAM_B2, ADAM_EPS, ADAM_WD, ADAM_STEP = 0.001, 0.9, 0.999, 1e-08, 0.01, 10
SB_UNDERFLOW = -120.0

PACK_ROWS = 256
MESH_AXES = ("x", "y", "c")


def _vmem_limit(block_bytes):
    return int(min(max(3 * block_bytes + (8 << 20), 24 << 20), VMEM_BYTES - (8 << 20)))


def _nbytes(shape, dtype):
    n = 1
    for d in shape:
        if d is not None:
            n *= d
    return n * jnp.dtype(dtype).itemsize


def _spec(shape, fn):
    return pl.BlockSpec(shape, fn)


class _Exchange:
    def __init__(self, scatter=(), gather=(), layered=()):
        self.arrs = list(scatter) + [a for a, _, _, _ in layered] + list(gather)
        self.n = len(self.arrs)
        self.n_sc = len(scatter) + len(layered)
        self.layer = [None] * len(scatter) + [l for _, l, _, _ in layered] + [None] * len(gather)
        self.shapes = ([a.shape for a in scatter] + [(N_DEV, dp) + a.shape[1:] for a, _, dp, _ in layered]
                       + [(N_DEV,) + a.shape for a in gather])
        self.held = [(len(scatter) + i, b) for i, (_, _, _, b) in enumerate(layered) if b is not None]
        self.operands = self.arrs + [b for _, b in self.held]
        hbm = pl.BlockSpec(memory_space=pltpu.HBM)
        self.in_specs = [hbm] * len(self.operands)
        self.out_specs = [hbm] * self.n
        self.out_shape = [jax.ShapeDtypeStruct(s, a.dtype) for s, a in zip(self.shapes, self.arrs)]
        self.scratch = [pltpu.SemaphoreType.DMA((self.n, N_DEV - 1)), pltpu.SemaphoreType.DMA((self.n, N_DEV - 1)),
                        pltpu.SemaphoreType.DMA((self.n,))]

    def aliases(self, first_in, first_out):
        return {first_in + self.n + i: first_out + a for i, (a, _) in enumerate(self.held)}

    def copies(self, ins, outs, sems):
        send_sems, recv_sems, own_sems = sems
        x, y, c = lax.axis_index("x"), lax.axis_index("y"), lax.axis_index("c")
        me = 4 * x + 2 * y + c
        landing = [outs[a].at[me] if self.layer[a] is None else outs[a].at[me, self.layer[a]] for a in range(self.n)]
        out = [pltpu.make_async_copy(ins[a].at[me] if a < self.n_sc else ins[a], landing[a], own_sems.at[a])
               for a in range(self.n)]
        for k in range(1, N_DEV):
            px = 1 - x if k & 4 else x
            py = 1 - y if k & 2 else y
            pc = 1 - c if k & 1 else c
            peer = 4 * px + 2 * py + pc
            for a in range(self.n):
                out.append(pltpu.make_async_remote_copy(
                    src_ref=ins[a].at[peer] if a < self.n_sc else ins[a], dst_ref=landing[a],
                    send_sem=send_sems.at[a, k - 1], recv_sem=recv_sems.at[a, k - 1],
                    device_id=(px, py, pc), device_id_type=pl.DeviceIdType.MESH))
        return out


def _exchange(name, scatter, gather, layered=()):
    ex = _Exchange(scatter, gather, layered)

    def body(*refs):
        copies = ex.copies(refs[:ex.n], refs[len(ex.operands):len(ex.operands) + ex.n], refs[-3:])
        for cp in copies:
            cp.start()
        for cp in copies:
            cp.wait()

    return pl.pallas_call(body, name=name, in_specs=ex.in_specs, out_specs=ex.out_specs, out_shape=ex.out_shape,
                          input_output_aliases=ex.aliases(0, 0), scratch_shapes=ex.scratch)(*ex.operands)


def _reduce_packed(name, packed, scatter):
    rows = packed.shape[0]
    blk = rows // N_DEV
    ex = _Exchange(scatter=[packed.reshape(N_DEV, blk, LANES)] + list(scatter))
    n_in = len(ex.operands)

    def body(*refs):
        ins, outs = refs[:ex.n], refs[n_in:n_in + ex.n]
        total_ref = refs[n_in + ex.n]
        sems, (send2, recv2, own2, load_sem) = refs[n_in + ex.n + 1:n_in + ex.n + 4], refs[n_in + ex.n + 4:-2]
        land_v, sum_v = refs[-2:]
        copies = ex.copies(ins, outs, sems)
        for cp in copies:
            cp.start()
        for cp in copies:
            cp.wait()
        load = pltpu.make_async_copy(outs[0], land_v, load_sem)
        load.start()
        load.wait()
        acc = land_v[0]
        for i in range(1, N_DEV):
            acc = acc + land_v[i]
        sum_v[...] = acc
        x, y, c = lax.axis_index("x"), lax.axis_index("y"), lax.axis_index("c")
        me = 4 * x + 2 * y + c
        back = [pltpu.make_async_copy(sum_v, total_ref.at[me], own2)]
        for k in range(1, N_DEV):
            peer = (1 - x if k & 4 else x, 1 - y if k & 2 else y, 1 - c if k & 1 else c)
            back.append(pltpu.make_async_remote_copy(
                src_ref=sum_v, dst_ref=total_ref.at[me], send_sem=send2.at[k - 1], recv_sem=recv2.at[k - 1],
                device_id=peer, device_id_type=pl.DeviceIdType.MESH))
        for cp in back:
            cp.start()
        for cp in back:
            cp.wait()

    hbm = pl.BlockSpec(memory_space=pltpu.HBM)
    res = pl.pallas_call(
        body, name=name, in_specs=ex.in_specs, out_specs=ex.out_specs + [hbm],
        out_shape=ex.out_shape + [jax.ShapeDtypeStruct((N_DEV, blk, LANES), F32)],
        scratch_shapes=ex.scratch + [pltpu.SemaphoreType.DMA((N_DEV - 1,)), pltpu.SemaphoreType.DMA((N_DEV - 1,)),
                                     pltpu.SemaphoreType.DMA, pltpu.SemaphoreType.DMA,
                                     pltpu.VMEM((N_DEV, blk, LANES), F32), pltpu.VMEM((blk, LANES), F32)],
    )(*ex.operands)
    return res[-1].reshape(rows, LANES), res[1:-1]


def _call_beside(ex, body, name, grid, in_specs, out_specs, out_shape, scratch_shapes, vmem_bytes, operands,
                 semantics, in_hbm=True):
    if in_hbm:
        operands = [_in_hbm(a) for a in operands]
    if ex is None:
        res = pl.pallas_call(
            body, name=name, grid=grid, in_specs=in_specs, out_specs=out_specs, out_shape=out_shape,
            scratch_shapes=scratch_shapes,
            compiler_params=pltpu.CompilerParams(dimension_semantics=semantics, vmem_limit_bytes=vmem_bytes),
        )(*operands)
        return res, None
    n_in, n_out, n_scr = len(in_specs), len(out_specs), len(scratch_shapes)
    n_xin = len(ex.operands)

    def fused(*refs):
        mine = refs[:n_in] + refs[n_in + n_xin:n_in + n_xin + n_out]
        mine += refs[n_in + n_xin + n_out + ex.n:n_in + n_xin + n_out + ex.n + n_scr]
        first = pl.program_id(0) == 0
        last = pl.program_id(0) == grid[0] - 1
        for dim in range(1, len(grid)):
            first = jnp.logical_and(first, pl.program_id(dim) == 0)
            last = jnp.logical_and(last, pl.program_id(dim) == grid[dim] - 1)
        x_ins = refs[n_in:n_in + ex.n]
        x_outs = refs[n_in + n_xin + n_out:n_in + n_xin + n_out + ex.n]

        @pl.when(first)
        def _():
            for cp in ex.copies(x_ins, x_outs, refs[-3:]):
                cp.start()

        body(*mine)

        @pl.when(last)
        def _():
            for cp in ex.copies(x_ins, x_outs, refs[-3:]):
                cp.wait()

    res = pl.pallas_call(
        fused, name=name, grid=grid, in_specs=list(in_specs) + ex.in_specs, out_specs=list(out_specs) + ex.out_specs,
        out_shape=list(out_shape) + ex.out_shape, input_output_aliases=ex.aliases(n_in, n_out),
        scratch_shapes=list(scratch_shapes) + ex.scratch,
        compiler_params=pltpu.CompilerParams(dimension_semantics=("arbitrary",) * len(grid),
                                             vmem_limit_bytes=vmem_bytes),
    )(*operands, *ex.operands)
    return res[:n_out], res[n_out:]


NN = (((1,), (0,)), ((), ()))
NT = (((1,), (1,)), ((), ()))
TN = (((0,), (0,)), ((), ()))


def _in_hbm(a):
    return pltpu.with_memory_space_constraint(a, pltpu.HBM)


def _mm(name, a, b, a_spec, b_spec, o_spec, o_shape, o_dtype, grid, dims, beside=None, reread=(False, True)):
    nk = grid[2]
    a, b = (x if again else _in_hbm(x) for x, again in zip((a, b), reread))
    acc_shape = tuple(d for d in o_spec.block_shape if d is not None)

    def product(a_ref, b_ref):
        return lax.dot_general(a_ref[...].astype(BF16), b_ref[...].astype(BF16), dims, preferred_element_type=F32)

    def body_once(a_ref, b_ref, o_ref):
        o_ref[...] = product(a_ref, b_ref).astype(o_ref.dtype)

    def body(a_ref, b_ref, o_ref, acc_ref):
        k = pl.program_id(2)

        @pl.when(k == 0)
        def _():
            acc_ref[...] = product(a_ref, b_ref)

        @pl.when(k > 0)
        def _():
            acc_ref[...] += product(a_ref, b_ref)

        @pl.when(k == nk - 1)
        def _():
            o_ref[...] = acc_ref[...].astype(o_ref.dtype)

    blk = (_nbytes(a_spec.block_shape, a.dtype) + _nbytes(b_spec.block_shape, b.dtype)
           + _nbytes(acc_shape, o_dtype) + _nbytes(acc_shape, F32))
    res, got = _call_beside(
        beside, body_once if nk == 1 else body, name, grid, [a_spec, b_spec], [o_spec],
        [jax.ShapeDtypeStruct(o_shape, o_dtype)], [] if nk == 1 else [pltpu.VMEM(acc_shape, F32)],
        _vmem_limit(blk), (a, b), ("parallel", "parallel", "arbitrary"), in_hbm=False)
    return res[0] if beside is None else (res[0], got)


def _mm_pieces(name, pieces, starts, width, step_block, other, other_spec, pieces_first, piece_rows, o_spec, o_shape,
               o_dtype, grid, dims, beside=None):
    n_p, nk = len(pieces), grid[2]
    acc_shape = tuple(s for s in o_spec.block_shape if s is not None)

    def which(i, j, k):
        blk = step_block(i, j, k)
        idx = 0
        for s in starts[1:]:
            idx = idx + (blk >= s).astype(jnp.int32)
        return idx, blk

    def piece_spec(p, t_rows):
        def index(i, j, k):
            idx, blk = which(i, j, k)
            mine = idx == p
            return jnp.where(mine, piece_rows(i, j, k), 0), jnp.where(mine, blk - starts[p], 0)
        return _spec((t_rows, width), index)

    def body(*refs):
        p_refs = refs[:n_p] if pieces_first else refs[1:1 + n_p]
        other_ref = refs[n_p] if pieces_first else refs[0]
        o_ref, acc_ref = refs[n_p + 1], refs[n_p + 2]
        i, j, k = pl.program_id(0), pl.program_id(1), pl.program_id(2)

        @pl.when(k == 0)
        def _():
            acc_ref[...] = jnp.zeros_like(acc_ref)

        idx, _ = which(i, j, k)
        for p in range(n_p):
            @pl.when(idx == p)
            def _(p=p):
                mine, fixed = p_refs[p][...].astype(BF16), other_ref[...].astype(BF16)
                pair = (mine, fixed) if pieces_first else (fixed, mine)
                acc_ref[...] += lax.dot_general(pair[0], pair[1], dims, preferred_element_type=F32)

        @pl.when(k == nk - 1)
        def _():
            o_ref[...] = acc_ref[...].astype(o_ref.dtype)

    t_rows = other_spec.block_shape[-2] if not pieces_first else o_spec.block_shape[-2]
    specs = [piece_spec(p, t_rows) for p in range(n_p)]
    in_specs = specs + [other_spec] if pieces_first else [other_spec] + specs
    operands = list(pieces) + [other] if pieces_first else [other] + list(pieces)
    blk = (n_p * 4 * t_rows * width + _nbytes(other_spec.block_shape, other.dtype)
           + _nbytes(acc_shape, o_dtype) + _nbytes(acc_shape, F32))
    res, got = _call_beside(
        beside, body, name, grid, in_specs, [o_spec], [jax.ShapeDtypeStruct(o_shape, o_dtype)],
        [pltpu.VMEM(acc_shape, F32)], _vmem_limit(blk), operands, ("parallel", "parallel", "arbitrary"), in_hbm=False)
    return res[0] if beside is None else (res[0], got)


def _swiglu_fn(gate_up):
    gate, up = gate_up[0], gate_up[1]
    return gate * jax.nn.sigmoid(gate) * up


def _ffn_in_swiglu(name, h, w, t_m):
    seq, d = h.shape
    n_half, n = w.shape[0] // 2, w.shape[2]

    def body(h_ref, wg_ref, wu_ref, a_ref, f_ref):
        hb = h_ref[...]
        a_ref[0] = lax.dot_general(hb, wg_ref[...], NN, preferred_element_type=F32).astype(a_ref.dtype)
        a_ref[1] = lax.dot_general(hb, wu_ref[...], NN, preferred_element_type=F32).astype(a_ref.dtype)
        f_ref[...] = _swiglu_fn(a_ref[...].astype(F32)).astype(f_ref.dtype)

    blk = 2 * t_m * d + 4 * d * n + 6 * t_m * n + 12 * t_m * n
    return pl.pallas_call(
        body, name=name, grid=(seq // t_m, n_half),
        in_specs=[_spec((t_m, d), lambda i, j: (i, 0)), _spec((None, d, n), lambda i, j: (j, 0, 0)),
                  _spec((None, d, n), lambda i, j: (j + n_half, 0, 0))],
        out_specs=[_spec((2, None, t_m, n), lambda i, j: (0, j, i, 0)), _spec((None, t_m, n), lambda i, j: (j, i, 0))],
        out_shape=[jax.ShapeDtypeStruct((2, n_half, seq, n), FFN_ACT), jax.ShapeDtypeStruct((n_half, seq, n), BF16)],
        compiler_params=pltpu.CompilerParams(dimension_semantics=("parallel", "parallel"),
                                             vmem_limit_bytes=_vmem_limit(blk)),
    )(h, w, w)


def _ffn_out_dx_swiglu(name, d_y, w, a, t_m):
    seq, d = d_y.shape
    n_half, n = w.shape[0], w.shape[1]

    def body(dy_ref, w_ref, a_ref, da_ref):
        d_f = lax.dot_general(dy_ref[...], w_ref[...], NT, preferred_element_type=F32)
        _, pull = jax.vjp(_swiglu_fn, a_ref[...].astype(F32))
        da_ref[...] = pull(d_f)[0].astype(da_ref.dtype)

    blk = 2 * t_m * d + 2 * d * n + 8 * t_m * n + 24 * t_m * n
    return pl.pallas_call(
        body, name=name, grid=(seq // t_m, n_half),
        in_specs=[_spec((t_m, d), lambda i, j: (i, 0)), _spec((None, n, d), lambda i, j: (j, 0, 0)),
                  _spec((2, None, t_m, n), lambda i, j: (0, j, i, 0))],
        out_specs=_spec((2, None, t_m, n), lambda i, j: (0, j, i, 0)),
        out_shape=jax.ShapeDtypeStruct((2, n_half, seq, n), BF16),
        compiler_params=pltpu.CompilerParams(dimension_semantics=("parallel", "parallel"),
                                             vmem_limit_bytes=_vmem_limit(blk)),
    )(d_y, w, a)


def _merge_fn(y_sb, y_ssm, gates):
    half = gates.shape[-1] // 2
    return jax.nn.sigmoid(gates[:, :half]) * y_sb + jax.nn.sigmoid(gates[:, half:]) * y_ssm


def _up_merge(name, o_sb, s5_out, proj, gates_cb, w_sb, w_ssm, layer, t_rows):
    seq = o_sb.shape[0]
    d = w_sb.shape[2]

    def body(o_ref, s_ref, g_ref, w1_ref, w2_ref, m_ref, y1_ref, y2_ref):
        y_sb = lax.dot_general(o_ref[...], w1_ref[...], NN, preferred_element_type=F32)
        y_ssm = lax.dot_general(s_ref[...], w2_ref[...], NN, preferred_element_type=F32)
        m_ref[...] = _merge_fn(y_sb, y_ssm, g_ref[...]).astype(m_ref.dtype)
        y1_ref[...] = y_sb.astype(y1_ref.dtype)
        y2_ref[...] = y_ssm.astype(y2_ref.dtype)

    row = lambda width: _spec((t_rows, width), lambda i: (i, 0))
    whole = lambda w: _spec((None,) + w.shape[1:], lambda i: (layer, 0, 0))
    blk = t_rows * (2 * o_sb.shape[1] + 2 * s5_out.shape[1] + 8 * d + 6 * d + 24 * d) + 4 * d * (o_sb.shape[1] + s5_out.shape[1])
    return pl.pallas_call(
        body, name=name, grid=(seq // t_rows,),
        in_specs=[row(o_sb.shape[1]), row(s5_out.shape[1]), _spec((t_rows, 2 * d), lambda i: (i, gates_cb)),
                  whole(w_sb), whole(w_ssm)],
        out_specs=[row(d)] * 3, out_shape=[jax.ShapeDtypeStruct((seq, d), BF16)] * 3,
        compiler_params=pltpu.CompilerParams(dimension_semantics=("parallel",), vmem_limit_bytes=_vmem_limit(blk)),
    )(_in_hbm(o_sb), _in_hbm(s5_out), _in_hbm(proj), w_sb, w_ssm)


def _tile(n, pref=1024):
    t = pref
    while t >= LANES:
        if n % t == 0:
            return t
        t -= LANES
    return n


def _rowwise(name, fn, ins, outs, grid):
    n_in = len(ins)

    def body(*refs):
        vals = fn(*[r[...].astype(F32) for r in refs[:n_in]])
        if not isinstance(vals, (tuple, list)):
            vals = (vals,)
        for r, v in zip(refs[n_in:], vals):
            r[...] = v.astype(r.dtype)

    blk = sum(_nbytes(bs, a.dtype) for a, bs, _ in ins) + sum(_nbytes(bs, d) + _nbytes(bs, F32) for _, d, bs, _ in outs)
    return pl.pallas_call(
        body, name=name, grid=grid,
        in_specs=[_spec(bs, im) for _, bs, im in ins],
        out_specs=[_spec(bs, im) for _, _, bs, im in outs],
        out_shape=[jax.ShapeDtypeStruct(s, d) for s, d, _, _ in outs],
        compiler_params=pltpu.CompilerParams(dimension_semantics=("parallel",) * len(grid),
                                             vmem_limit_bytes=_vmem_limit(2 * blk)),
    )(*[_in_hbm(a) for a, _, _ in ins])


def _rowwise_vjp(name, fn, ins, cts, wrt, grid):
    n_in, n_ct = len(ins), len(cts)
    idx = [w[0] for w in wrt]

    def body(*refs):
        prim = [r[...].astype(F32) for r in refs[:n_in]]
        ct = tuple(r[...].astype(F32) for r in refs[n_in:n_in + n_ct])
        o_refs = refs[n_in + n_ct:]

        def g(*sel):
            full = list(prim)
            for i, s in zip(idx, sel):
                full[i] = s
            out = fn(*full)
            return tuple(out) if isinstance(out, (tuple, list)) else (out,)

        _, pull = jax.vjp(g, *[prim[i] for i in idx])
        grads = pull(ct)
        first = pl.program_id(0) == 0
        for d in range(1, len(grid)):
            first = jnp.logical_and(first, pl.program_id(d) == 0)
        for w, o_ref, gr in zip(wrt, o_refs, grads):
            if w[1] == "row":
                o_ref[...] = gr.astype(o_ref.dtype)
            else:
                @pl.when(first)
                def _(o_ref=o_ref):
                    o_ref[...] = jnp.zeros_like(o_ref)

                o_ref[...] += gr.astype(o_ref.dtype)

    blk = (sum(_nbytes(bs, a.dtype) + _nbytes(bs, F32) for a, bs, _ in list(ins) + list(cts))
           + sum(_nbytes(w[4], w[3]) + _nbytes(w[4], F32) for w in wrt))
    return pl.pallas_call(
        body, name=name, grid=grid,
        in_specs=[_spec(bs, im) for _, bs, im in list(ins) + list(cts)],
        out_specs=[_spec(w[4], w[5]) for w in wrt],
        out_shape=[jax.ShapeDtypeStruct(w[2], w[3]) for w in wrt],
        compiler_params=pltpu.CompilerParams(dimension_semantics=("arbitrary",) * len(grid),
                                             vmem_limit_bytes=_vmem_limit(2 * blk)),
    )(*[_in_hbm(a) for a, _, _ in list(ins) + list(cts)])


def _normalize(x):
    mu = jnp.mean(x, axis=-1, keepdims=True)
    xc = x - mu
    var = jnp.mean(xc * xc, axis=-1, keepdims=True)
    return xc * lax.rsqrt(var + LN_EPS)


def _modulate(x, sc, sh):
    return _normalize(x) * (1.0 + sc) + sh


def _make_resid_fns(alpha):
    def resid_ln(x, y, gate, g, b):
        return _normalize(alpha * x + (1.0 + gate) * y) * g + b

    def resid_ln_mod(x, y, gate, g, b, sc, sh):
        xn = resid_ln(x, y, gate, g, b)
        return xn, _modulate(xn, sc, sh)

    return resid_ln, resid_ln_mod


def _s5_act_fn(yc, u, d_skip):
    return jax.nn.gelu(yc + d_skip * u)


def _s5_gate_fn(y1, t):
    return y1 * jax.nn.sigmoid(t)


def _s5_head_specs(yc, proj, u_cb, w_glu, layer, t_rows):
    width = yc.shape[1]
    row = _spec((t_rows, width), lambda i: (i, 0))
    u_spec = _spec((t_rows, width), lambda i: (i, u_cb))
    vec = _spec((1, width), lambda i: (0, 0))
    w_spec = _spec((None,) + w_glu.shape[1:], lambda i: (layer, 0, 0))
    return row, u_spec, vec, w_spec


def _s5_head(name, yc, proj, u_cb, d_skip, b_glu, w_glu, layer, t_rows):
    seq, width = yc.shape
    row, u_spec, vec, w_spec = _s5_head_specs(yc, proj, u_cb, w_glu, layer, t_rows)

    def body(yc_ref, u_ref, d_ref, b_ref, w_ref, o_ref):
        y1 = _s5_act_fn(yc_ref[...], u_ref[...], d_ref[...])
        t = lax.dot_general(y1.astype(BF16), w_ref[...], NN, preferred_element_type=F32) + b_ref[...]
        o_ref[...] = _s5_gate_fn(y1, t).astype(o_ref.dtype)

    return pl.pallas_call(
        body, name=name, grid=(seq // t_rows,), in_specs=[row, u_spec, vec, vec, w_spec], out_specs=row,
        out_shape=jax.ShapeDtypeStruct((seq, width), BF16),
        compiler_params=pltpu.CompilerParams(dimension_semantics=("parallel",),
                                             vmem_limit_bytes=_vmem_limit(40 * t_rows * width)),
    )(_in_hbm(yc), _in_hbm(proj), d_skip, b_glu, w_glu)


def _s5_head_bwd(name, yc, proj, u_cb, d_out, d_skip, b_glu, w_glu, layer, t_rows):
    seq, width = yc.shape
    row, u_spec, vec, w_spec = _s5_head_specs(yc, proj, u_cb, w_glu, layer, t_rows)
    n_t = seq // t_rows

    def body(yc_ref, u_ref, do_ref, d_ref, b_ref, w_ref, dyc_ref, du_ref, dw_ref, dd_ref, db_ref, acc_ref):
        i = pl.program_id(0)

        @pl.when(i == 0)
        def _():
            acc_ref[...] = jnp.zeros_like(acc_ref)
            dd_ref[...] = jnp.zeros_like(dd_ref)
            db_ref[...] = jnp.zeros_like(db_ref)

        y1, pull_act = jax.vjp(_s5_act_fn, yc_ref[...], u_ref[...], d_ref[...])
        y1_b = y1.astype(BF16)
        t = lax.dot_general(y1_b, w_ref[...], NN, preferred_element_type=F32) + b_ref[...]
        _, pull_gate = jax.vjp(_s5_gate_fn, y1, t)
        d_y1, d_t = pull_gate(do_ref[...].astype(F32))
        d_t_b = d_t.astype(BF16)
        d_y1 = d_y1 + lax.dot_general(d_t_b, w_ref[...], NT, preferred_element_type=F32)
        acc_ref[...] += lax.dot_general(y1_b, d_t_b, TN, preferred_element_type=F32)
        db_ref[...] += jnp.sum(d_t, axis=0, keepdims=True)
        d_yc, d_u, d_d = pull_act(d_y1)
        dyc_ref[...] = d_yc
        du_ref[...] = d_u
        dd_ref[...] += d_d

        @pl.when(i == n_t - 1)
        def _():
            dw_ref[...] = acc_ref[...].astype(dw_ref.dtype)

    whole = _spec((width, width), lambda i: (0, 0))
    return pl.pallas_call(
        body, name=name, grid=(n_t,), in_specs=[row, u_spec, row, vec, vec, w_spec],
        out_specs=[row, row, whole, vec, vec],
        out_shape=[jax.ShapeDtypeStruct((seq, width), F32), jax.ShapeDtypeStruct((seq, width), F32),
                   jax.ShapeDtypeStruct((width, width), GRAD_WIRE), jax.ShapeDtypeStruct((1, width), F32),
                   jax.ShapeDtypeStruct((1, width), F32)],
        scratch_shapes=[pltpu.VMEM((width, width), F32)],
        compiler_params=pltpu.CompilerParams(dimension_semantics=("arbitrary",),
                                             vmem_limit_bytes=_vmem_limit(80 * t_rows * width)),
    )(_in_hbm(yc), _in_hbm(proj), _in_hbm(d_out), d_skip, b_glu, w_glu)


def _sb_tri(kind):
    row = lax.broadcasted_iota(jnp.int32, (SB_BLOCK, SB_BLOCK), 0)
    col = lax.broadcasted_iota(jnp.int32, (SB_BLOCK, SB_BLOCK), 1)
    if kind == "after":
        return (row > col).astype(BF16)
    if kind == "from":
        return (row >= col).astype(BF16)
    return col < row


def _split_dot(x, m):
    hi = x.astype(BF16)
    lo = (x - hi.astype(F32)).astype(BF16)
    return (lax.dot_general(hi, m, NN, preferred_element_type=F32)
            + lax.dot_general(lo, m, NN, preferred_element_type=F32))


def _sb_scores(qh, k2):
    z = lax.dot_general(qh, k2, NT, preferred_element_type=F32)
    log_beta = jnp.minimum(z, 0.0) - jnp.log(1.0 + jnp.exp(-jnp.abs(z)))
    return log_beta, log_beta - z


def _sb_attention_fwd(proj, sb_width, beside=None):
    seq = proj.shape[0]
    n_pair, n_q = sb_width // LANES, seq // SB_BLOCK
    scale = 1.0 / (HEAD_DIM ** 0.5)

    def body(q_ref, k_ref, v_ref, o_ref, o32_ref):
        qi = pl.program_id(1)
        q2 = q_ref[...]
        lane = lax.broadcasted_iota(jnp.int32, (SB_BLOCK, LANES), 1)
        m_after, causal = _sb_tri("after"), _sb_tri("mask")
        heads = [lane < HEAD_DIM, lane >= HEAD_DIM]
        qh = [(jnp.where(m, q2, 0.0) * scale).astype(BF16) for m in heads]

        def scores(kb, diag):
            ks = pl.multiple_of(kb * SB_BLOCK, SB_BLOCK)
            k2 = k_ref[pl.ds(ks, SB_BLOCK), :].astype(BF16)
            out = []
            for h in range(2):
                log_beta, log_1m = _sb_scores(qh[h], k2)
                if diag:
                    log_1m = jnp.where(causal, log_1m, 0.0)
                out += [log_beta + _split_dot(log_1m, m_after), jnp.sum(log_1m, axis=1, keepdims=True)]
            return tuple(out)

        def weigh(kb, sc, carry, acc, diag):
            ks = pl.multiple_of(kb * SB_BLOCK, SB_BLOCK)
            v2 = v_ref[pl.ds(ks, SB_BLOCK), :].astype(BF16)
            out = []
            for h in range(2):
                w = jnp.exp(sc[2 * h] + carry[h])
                if diag:
                    w = jnp.where(causal, w, 0.0)
                out.append(acc[h] + lax.dot_general(w.astype(BF16), v2, NN, preferred_element_type=F32))
            return tuple(out)

        zero = jnp.zeros((SB_BLOCK, LANES), F32)
        zcol = jnp.zeros((SB_BLOCK, 1), F32)
        sc = scores(qi, True)
        acc = weigh(qi, sc, (zcol, zcol), (zero, zero), True)
        carry = (sc[1], sc[3])

        def loop(st):
            kb, carry, acc = st
            sc = scores(kb, False)
            after = (carry[0] + sc[1], carry[1] + sc[3])
            done = jnp.maximum(jnp.max(after[0]), jnp.max(after[1])) < SB_UNDERFLOW
            acc = weigh(kb, sc, carry, acc, False)
            return jnp.where(done, -1, kb - 1), after, acc

        _, _, acc = lax.while_loop(lambda st: st[0] >= 0, loop, (qi - 1, carry, acc))
        out = jnp.where(heads[0], acc[0], acc[1])
        o_ref[...] = out.astype(o_ref.dtype)
        o32_ref[...] = out

    q_spec = _spec((SB_BLOCK, LANES), lambda h, i: (i, h))
    kv = [_spec((seq, LANES), lambda h, i, o=o: (0, o + h)) for o in (n_pair, 2 * n_pair)]
    o_spec = _spec((SB_BLOCK, LANES), lambda h, i: (i, h))
    return _call_beside(
        beside, body, "sb_attention_fwd", (n_pair, n_q), [q_spec] + kv, [o_spec, o_spec],
        [jax.ShapeDtypeStruct((seq, sb_width), BF16), jax.ShapeDtypeStruct((seq, sb_width), F32)], [],
        _vmem_limit(2 * seq * LANES * 4), (proj, proj, proj), ("parallel", "arbitrary"))


def _sb_attention_bwd(proj, o32, do, sb_width, beside=None):
    seq = proj.shape[0]
    n_pair, n_q = sb_width // LANES, seq // SB_BLOCK
    scale = 1.0 / (HEAD_DIM ** 0.5)

    def body(q_ref, k_ref, v_ref, o_ref, do_ref, dq_ref, dk_ref, dv_ref):
        qi = pl.program_id(1)

        @pl.when(qi == 0)
        def _():
            dk_ref[...] = jnp.zeros_like(dk_ref)
            dv_ref[...] = jnp.zeros_like(dv_ref)

        q2 = q_ref[...]
        do2 = do_ref[...].astype(F32)
        o2 = o_ref[...]
        lane = lax.broadcasted_iota(jnp.int32, (SB_BLOCK, LANES), 1)
        m_after, m_from, causal = _sb_tri("after"), _sb_tri("from"), _sb_tri("mask")
        heads = [lane < HEAD_DIM, lane >= HEAD_DIM]
        qh = [(jnp.where(m, q2, 0.0) * scale).astype(BF16) for m in heads]
        doh = [jnp.where(m, do2, 0.0) for m in heads]
        doh_b = [v.astype(BF16) for v in doh]
        total = [jnp.sum(v * o2, axis=1, keepdims=True) for v in doh]

        def scores(kb, diag):
            ks = pl.multiple_of(kb * SB_BLOCK, SB_BLOCK)
            k2 = k_ref[pl.ds(ks, SB_BLOCK), :].astype(BF16)
            v2 = v_ref[pl.ds(ks, SB_BLOCK), :].astype(BF16)
            out = []
            for h in range(2):
                log_beta, log_1m = _sb_scores(qh[h], k2)
                if diag:
                    log_1m = jnp.where(causal, log_1m, 0.0)
                out += [log_beta + _split_dot(log_1m, m_after), jnp.sum(log_1m, axis=1, keepdims=True),
                        lax.dot_general(doh_b[h], v2, NT, preferred_element_type=F32), log_beta]
            return tuple(out)

        def pull(kb, sc, carry, right, dq, diag):
            ks = pl.multiple_of(kb * SB_BLOCK, SB_BLOCK)
            k2 = k_ref[pl.ds(ks, SB_BLOCK), :].astype(BF16)
            dv_blk, dk_blk, right_out, dq_out = None, None, [], []
            for h in range(2):
                arg, _, d_w, log_beta = sc[4 * h:4 * h + 4]
                w = jnp.exp(arg + carry[h])
                if diag:
                    w = jnp.where(causal, w, 0.0)
                w_b = w.astype(BF16)
                d_arg = d_w * w_b.astype(F32)
                dv_h = lax.dot_general(w_b, doh_b[h], TN, preferred_element_type=F32)
                d_log_1m = total[h] - right[h] - _split_dot(d_arg, m_from)
                beta = jnp.exp(log_beta)
                dz = d_arg * (1.0 - beta) - beta * d_log_1m
                if diag:
                    dz = jnp.where(causal, dz, 0.0)
                dz_b = dz.astype(BF16)
                dk_h = lax.dot_general(dz_b, qh[h], TN, preferred_element_type=F32)
                dv_blk = dv_h if h == 0 else dv_blk + dv_h
                dk_blk = dk_h if h == 0 else dk_blk + dk_h
                dq_out.append(dq[h] + lax.dot_general(dz_b, k2, NN, preferred_element_type=F32))
                right_out.append(right[h] + jnp.sum(d_arg, axis=1, keepdims=True))
            dv_ref[pl.ds(ks, SB_BLOCK), :] += dv_blk
            dk_ref[pl.ds(ks, SB_BLOCK), :] += dk_blk
            return tuple(right_out), tuple(dq_out)

        zero = jnp.zeros((SB_BLOCK, LANES), F32)
        zcol = jnp.zeros((SB_BLOCK, 1), F32)
        sc = scores(qi, True)
        right, dq = pull(qi, sc, (zcol, zcol), (zcol, zcol), (zero, zero), True)
        carry = (sc[1], sc[5])

        def loop(st):
            kb, carry, right, dq = st
            sc = scores(kb, False)
            after = (carry[0] + sc[1], carry[1] + sc[5])
            done = jnp.maximum(jnp.max(after[0]), jnp.max(after[1])) < SB_UNDERFLOW
            right, dq = pull(kb, sc, carry, right, dq, False)
            return jnp.where(done, -1, kb - 1), after, right, dq

        _, _, _, dq = lax.while_loop(lambda st: st[0] >= 0, loop, (qi - 1, carry, right, dq))
        dq_ref[...] = (jnp.where(heads[0], dq[0], dq[1]) * scale).astype(dq_ref.dtype)

    q_spec = _spec((SB_BLOCK, LANES), lambda h, i: (i, h))
    kv = [_spec((seq, LANES), lambda h, i, o=o: (0, o + h)) for o in (n_pair, 2 * n_pair)]
    full = _spec((seq, LANES), lambda h, i: (0, h))
    return _call_beside(
        beside, body, "sb_attention_bwd", (n_pair, n_q), [q_spec] + kv + [q_spec, q_spec], [q_spec, full, full],
        [jax.ShapeDtypeStruct((seq, sb_width), BF16), jax.ShapeDtypeStruct((seq, sb_width), F32),
         jax.ShapeDtypeStruct((seq, sb_width), F32)], [],
        _vmem_limit(4 * seq * LANES * 4), (proj, proj, proj, o32, do), ("parallel", "arbitrary"))


def _s5_discretize(a_re, a_im, log_dt, b_re, b_im, c_re, c_im):
    n_g, n_p = a_re.shape
    c_g = b_re.shape[-1]
    ns = n_g // SLAB_GROUPS
    dt = jnp.exp(log_dt)[:, None]
    xr, xi = a_re * dt, a_im * dt
    mag = jnp.exp(xr)
    lr, li = mag * jnp.cos(xi), mag * jnp.sin(xi)
    den = a_re * a_re + a_im * a_im
    fr = ((lr - 1.0) * a_re + li * a_im) / den
    fi = (li * a_re - (lr - 1.0) * a_im) / den
    bb_re = fr[..., None] * b_re - fi[..., None] * b_im
    bb_im = fr[..., None] * b_im + fi[..., None] * b_re
    eye = jnp.eye(SLAB_GROUPS, dtype=F32)

    def diag_b(m):
        m = jnp.transpose(m.reshape(ns, SLAB_GROUPS, n_p, c_g), (0, 1, 3, 2))
        m = m[:, :, :, None, :] * eye[None, :, None, :, None]
        return m.reshape(ns, SLAB_GROUPS * c_g, SLAB_GROUPS * n_p)

    def diag_c(m):
        m = jnp.transpose(m.reshape(ns, SLAB_GROUPS, c_g, n_p), (0, 1, 3, 2))
        m = m[:, :, :, None, :] * eye[None, :, None, :, None]
        return m.reshape(ns, SLAB_GROUPS * n_p, SLAB_GROUPS * c_g)

    bs = jnp.concatenate([diag_b(bb_re), diag_b(bb_im)], axis=-1)
    cs = jnp.concatenate([diag_c(c_re), -diag_c(c_im)], axis=1)
    lam = jnp.concatenate([lr.reshape(ns, 1, -1), li.reshape(ns, 1, -1)], axis=-1)
    return bs, cs, lam


def _s5_powers(a_re, a_im, log_dt, n):
    n_g, n_p = a_re.shape
    ns = n_g // SLAB_GROUPS
    dt = jnp.exp(log_dt)[:, None]
    mag = jnp.exp(a_re * dt)
    base_r, base_i = mag * jnp.cos(a_im * dt), mag * jnp.sin(a_im * dt)
    steps = jnp.arange(1, n + 1, dtype=jnp.int32)[:, None, None]
    pr, pi = jnp.ones((n, n_g, n_p), F32), jnp.zeros((n, n_g, n_p), F32)
    for b in range(n.bit_length()):
        take = ((steps >> b) & 1) == 1
        pr, pi = (jnp.where(take, pr * base_r - pi * base_i, pr), jnp.where(take, pr * base_i + pi * base_r, pi))
        base_r, base_i = base_r * base_r - base_i * base_i, 2.0 * base_r * base_i

    def slabs(re, im):
        one = lambda m: jnp.transpose(m.reshape(n, ns, SLAB_GROUPS * n_p), (1, 0, 2))
        return jnp.concatenate([one(re), one(im)], axis=-1)

    return slabs(pr, pi), slabs(pr[::-1], -pi[::-1])


def _lanes(j):
    return slice(j * LANES, (j + 1) * LANES)


def _tile8(k):
    return pl.ds(pl.multiple_of(k * SUBLANES, SUBLANES), SUBLANES)


def _s5_interleave(dst_ref, src_ref, t_seg):
    def body(k, _):
        dst_ref[_tile8(k), :] = src_ref[pl.ds(k, SUBLANES, stride=t_seg), :]
        return 0

    lax.fori_loop(0, t_seg, body, 0, unroll=4)


def _s5_join_segments(st_ref, end_ref, car_ref, tab_ref, row, order, n_pair):
    for j in range(n_pair):
        re, im = _lanes(j), _lanes(n_pair + j)
        cr, ci = st_ref[:, re], st_ref[:, im]
        tr, ti = tab_ref[row:row + 1, re], tab_ref[row:row + 1, im]
        for s in order:
            car_ref[s:s + 1, re] = cr
            car_ref[s:s + 1, im] = ci
            er, ei = end_ref[s:s + 1, re], end_ref[s:s + 1, im]
            cr, ci = er + tr * cr - ti * ci, ei + tr * ci + ti * cr
        st_ref[:, re] = cr
        st_ref[:, im] = ci


def _s5_add_carries(buf_ref, car_ref, tab_ref, t_seg, n_pair):
    def fix(k, _):
        rows = _tile8(k)
        tab = tab_ref[pl.ds(k, 1), :]
        for j in range(n_pair):
            re, im = _lanes(j), _lanes(n_pair + j)
            cr, ci = car_ref[:, re], car_ref[:, im]
            tr, ti = tab[:, re], tab[:, im]
            buf_ref[rows, re] += tr * cr - ti * ci
            buf_ref[rows, im] += tr * ci + ti * cr
        return 0

    lax.fori_loop(0, t_seg, fix, 0, unroll=2)


def _s5_scan_fwd(proj, u_col, bs, cs, lam, pw, t_blk, beside=None):
    seq = proj.shape[0]
    ns, _, w2 = bs.shape
    n_pair = w2 // (2 * LANES)
    t_seg, n_t = t_blk // SUBLANES, seq // t_blk

    def body(u_ref, bs_ref, cs_ref, lam_ref, pw_ref, yc_ref, h_ref, st_ref, end_ref, car_ref, ui_ref, bu_ref, yi_ref):
        @pl.when(pl.program_id(1) == 0)
        def _():
            st_ref[...] = jnp.zeros_like(st_ref)

        _s5_interleave(ui_ref, u_ref, t_seg)
        bu_ref[...] = lax.dot_general(ui_ref[...].astype(BF16), bs_ref[...], NN, preferred_element_type=F32)
        lam_r = [jnp.broadcast_to(lam_ref[:, _lanes(j)], (SUBLANES, LANES)) for j in range(n_pair)]
        lam_i = [jnp.broadcast_to(lam_ref[:, _lanes(n_pair + j)], (SUBLANES, LANES)) for j in range(n_pair)]

        def step(k, c):
            rows = _tile8(k)
            out = []
            for j in range(n_pair):
                hr, hi = c[2 * j], c[2 * j + 1]
                nr = lam_r[j] * hr - lam_i[j] * hi + bu_ref[rows, _lanes(j)]
                ni = lam_i[j] * hr + lam_r[j] * hi + bu_ref[rows, _lanes(n_pair + j)]
                h_ref[rows, _lanes(j)] = nr
                h_ref[rows, _lanes(n_pair + j)] = ni
                out += [nr, ni]
            return tuple(out)

        ends = lax.fori_loop(0, t_seg, step, (jnp.zeros((SUBLANES, LANES), F32),) * (2 * n_pair), unroll=4)
        for j in range(n_pair):
            end_ref[:, _lanes(j)] = ends[2 * j]
            end_ref[:, _lanes(n_pair + j)] = ends[2 * j + 1]
        _s5_join_segments(st_ref, end_ref, car_ref, pw_ref, t_seg - 1, list(range(SUBLANES)), n_pair)
        _s5_add_carries(h_ref, car_ref, pw_ref, t_seg, n_pair)
        yi_ref[...] = lax.dot_general(h_ref[...].astype(BF16), cs_ref[...], NN, preferred_element_type=F32)

        def scatter(k, _):
            yc_ref[pl.ds(k, SUBLANES, stride=t_seg), :] = yi_ref[_tile8(k), :]
            return 0

        lax.fori_loop(0, t_seg, scatter, 0, unroll=4)

    return _call_beside(
        beside, body, "s5_scan_fwd", (ns, n_t),
        [_spec((t_blk, LANES), lambda s, i: (i, u_col + s)),
         _spec((None, LANES, w2), lambda s, i: (s, 0, 0)),
         _spec((None, w2, LANES), lambda s, i: (s, 0, 0)),
         _spec((None, 1, w2), lambda s, i: (s, 0, 0)),
         _spec((None, t_seg, w2), lambda s, i: (s, 0, 0))],
        [_spec((t_blk, LANES), lambda s, i: (i, s)),
         _spec((None, t_blk, w2), lambda s, i: (s, i, 0))],
        [jax.ShapeDtypeStruct((seq, ns * LANES), F32), jax.ShapeDtypeStruct((ns, seq, w2), F32)],
        [pltpu.VMEM((1, w2), F32), pltpu.VMEM((SUBLANES, w2), F32), pltpu.VMEM((SUBLANES, w2), F32),
         pltpu.VMEM((t_blk, LANES), F32), pltpu.VMEM((t_blk, w2), F32), pltpu.VMEM((t_blk, LANES), F32)],
        _vmem_limit(3 * t_blk * w2 * 4), (proj, bs, cs, lam, pw), ("parallel", "arbitrary"))


def _s5_scan_bwd(proj, u_col, states, d_yc, du_extra, bs, cs, lam, qw, t_blk):
    seq = proj.shape[0]
    ns, _, w2 = bs.shape
    n_pair = w2 // (2 * LANES)
    t_seg, n_t = t_blk // SUBLANES, seq // t_blk

    def body(u_ref, h_ref, hp_ref, dyc_ref, dux_ref, bs_ref, cs_ref, lam_ref, qw_ref,
             du_ref, dbs_ref, dcs_ref, dlam_ref, g_ref, gd_ref, st_ref, end_ref, car_ref, ui_ref, dyi_ref, dui_ref):
        i = pl.program_id(1)

        @pl.when(i == 0)
        def _():
            st_ref[...] = jnp.zeros_like(st_ref)
            dbs_ref[...] = jnp.zeros_like(dbs_ref)
            dcs_ref[...] = jnp.zeros_like(dcs_ref)
            dlam_ref[...] = jnp.zeros_like(dlam_ref)

        _s5_interleave(ui_ref, u_ref, t_seg)
        _s5_interleave(dyi_ref, dyc_ref, t_seg)
        dyc_b = dyi_ref[...].astype(BF16)
        gd_ref[...] = lax.dot_general(dyc_b, cs_ref[...], NT, preferred_element_type=F32)
        lam_r = [jnp.broadcast_to(lam_ref[:, _lanes(j)], (SUBLANES, LANES)) for j in range(n_pair)]
        lam_i = [jnp.broadcast_to(lam_ref[:, _lanes(n_pair + j)], (SUBLANES, LANES)) for j in range(n_pair)]

        def step(kk, c):
            rows = _tile8(t_seg - 1 - kk)
            out = []
            for j in range(n_pair):
                gr_n, gi_n = c[2 * j], c[2 * j + 1]
                gr = gd_ref[rows, _lanes(j)] + lam_r[j] * gr_n + lam_i[j] * gi_n
                gi = gd_ref[rows, _lanes(n_pair + j)] + lam_r[j] * gi_n - lam_i[j] * gr_n
                g_ref[rows, _lanes(j)] = gr
                g_ref[rows, _lanes(n_pair + j)] = gi
                out += [gr, gi]
            return tuple(out)

        zero = jnp.zeros((SUBLANES, LANES), F32)
        firsts = lax.fori_loop(0, t_seg, step, (zero,) * (2 * n_pair), unroll=4)
        for j in range(n_pair):
            end_ref[:, _lanes(j)] = firsts[2 * j]
            end_ref[:, _lanes(n_pair + j)] = firsts[2 * j + 1]
        _s5_join_segments(st_ref, end_ref, car_ref, qw_ref, 0, list(range(SUBLANES))[::-1], n_pair)
        _s5_add_carries(g_ref, car_ref, qw_ref, t_seg, n_pair)

        def pair_up(k, c):
            rows, prev = _tile8(k), _tile8(k - 1)
            out = []
            for j in range(n_pair):
                re, im = _lanes(j), _lanes(n_pair + j)
                gr, gi, hr, hi = g_ref[rows, re], g_ref[rows, im], h_ref[prev, re], h_ref[prev, im]
                out += [c[2 * j] + gr * hr + gi * hi, c[2 * j + 1] + gi * hr - gr * hi]
            return tuple(out)

        acc = lax.fori_loop(1, t_seg, pair_up, (zero,) * (2 * n_pair), unroll=4)
        has_prev = (i < n_t - 1).astype(F32)
        first_seg = lax.broadcasted_iota(jnp.int32, (SUBLANES, LANES), 0) == 0
        last = _tile8(t_seg - 1)
        for j in range(n_pair):
            re, im = _lanes(j), _lanes(n_pair + j)
            gr, gi = g_ref[0:SUBLANES, re], g_ref[0:SUBLANES, im]
            hr = jnp.where(first_seg, hp_ref[SUBLANES - 1:, re] * has_prev, pltpu.roll(h_ref[last, re], 1, 0))
            hi = jnp.where(first_seg, hp_ref[SUBLANES - 1:, im] * has_prev, pltpu.roll(h_ref[last, im], 1, 0))
            dlam_ref[:, re] += jnp.sum(acc[2 * j] + gr * hr + gi * hi, axis=0, keepdims=True)
            dlam_ref[:, im] += jnp.sum(acc[2 * j + 1] + gi * hr - gr * hi, axis=0, keepdims=True)

        g_b = g_ref[...].astype(BF16)
        dui_ref[...] = lax.dot_general(g_b, bs_ref[...], NT, preferred_element_type=F32)
        dbs_ref[...] += lax.dot_general(ui_ref[...].astype(BF16), g_b, TN, preferred_element_type=F32)
        dcs_ref[...] += lax.dot_general(h_ref[...].astype(BF16), dyc_b, TN, preferred_element_type=F32)

        def scatter(k, _):
            rows = pl.ds(k, SUBLANES, stride=t_seg)
            du_ref[rows, :] = (dui_ref[_tile8(k), :] + dux_ref[rows, :]).astype(du_ref.dtype)
            return 0

        lax.fori_loop(0, t_seg, scatter, 0, unroll=4)

    rev = lambda i: n_t - 1 - i
    return pl.pallas_call(
        body, name="s5_scan_bwd", grid=(ns, n_t),
        in_specs=[_spec((t_blk, LANES), lambda s, i: (rev(i), u_col + s)),
                  _spec((None, t_blk, w2), lambda s, i: (s, rev(i), 0)),
                  _spec((None, SUBLANES, w2), lambda s, i: (s, jnp.maximum(rev(i) * t_seg - 1, 0), 0)),
                  _spec((t_blk, LANES), lambda s, i: (rev(i), s)),
                  _spec((t_blk, LANES), lambda s, i: (rev(i), s)),
                  _spec((None, LANES, w2), lambda s, i: (s, 0, 0)),
                  _spec((None, w2, LANES), lambda s, i: (s, 0, 0)),
                  _spec((None, 1, w2), lambda s, i: (s, 0, 0)),
                  _spec((None, t_seg, w2), lambda s, i: (s, 0, 0))],
        out_specs=[_spec((t_blk, LANES), lambda s, i: (rev(i), s)),
                   _spec((None, LANES, w2), lambda s, i: (s, 0, 0)),
                   _spec((None, w2, LANES), lambda s, i: (s, 0, 0)),
                   _spec((None, 1, w2), lambda s, i: (s, 0, 0))],
        out_shape=[jax.ShapeDtypeStruct((seq, ns * LANES), F32), jax.ShapeDtypeStruct(bs.shape, F32),
                   jax.ShapeDtypeStruct(cs.shape, F32), jax.ShapeDtypeStruct(lam.shape, F32)],
        scratch_shapes=[pltpu.VMEM((t_blk, w2), F32), pltpu.VMEM((t_blk, w2), F32), pltpu.VMEM((1, w2), F32),
                        pltpu.VMEM((SUBLANES, w2), F32), pltpu.VMEM((SUBLANES, w2), F32),
                        pltpu.VMEM((t_blk, LANES), F32), pltpu.VMEM((t_blk, LANES), F32), pltpu.VMEM((t_blk, LANES), F32)],
        compiler_params=pltpu.CompilerParams(dimension_semantics=("parallel", "arbitrary"),
                                             vmem_limit_bytes=_vmem_limit(5 * t_blk * w2 * 4)),
    )(*[_in_hbm(a) for a in (proj, states, states, d_yc, du_extra, bs, cs, lam, qw)])


def _loss_head(y, target, t_m):
    seq, d = y.shape

    def body(y_ref, t_ref, loss_ref, dy_ref):
        @pl.when(pl.program_id(0) == 0)
        def _():
            loss_ref[...] = jnp.zeros_like(loss_ref)

        diff = y_ref[...] - t_ref[...]
        dy_ref[...] = diff / d
        loss_ref[...] += 0.5 * jnp.sum(diff * diff) / d

    row = _spec((t_m, d), lambda i: (i, 0))
    return pl.pallas_call(
        body, name="loss_head", grid=(seq // t_m,), in_specs=[row, row],
        out_specs=[_spec((SUBLANES, LANES), lambda i: (0, 0)), row],
        out_shape=[jax.ShapeDtypeStruct((SUBLANES, LANES), F32), jax.ShapeDtypeStruct((seq, d), F32)],
        compiler_params=pltpu.CompilerParams(dimension_semantics=("arbitrary",),
                                             vmem_limit_bytes=_vmem_limit(6 * t_m * d * 4)),
    )(_in_hbm(y), _in_hbm(target))


def _adamw_fn(w, m, v, *partials):
    g = partials[0]
    for p in partials[1:]:
        g = g + p
    m2 = ADAM_B1 * m + (1.0 - ADAM_B1) * g
    v2 = ADAM_B2 * v + (1.0 - ADAM_B2) * (g * g)
    m_hat = m2 / (1.0 - ADAM_B1 ** ADAM_STEP)
    v_hat = v2 / (1.0 - ADAM_B2 ** ADAM_STEP)
    delta = -ADAM_LR * (m_hat / (jnp.sqrt(v_hat) + ADAM_EPS) + ADAM_WD * w)
    return g, delta, m2, v2


def _adamw(name, w, m, v, partials):
    rows, cols = w.shape
    t_r = rows
    for cand in (512, 256, 128, 64, 32, 16, 8):
        if rows % cand == 0 and cand * cols * 4 <= (1 << 20):
            t_r = cand
            break
    n_p = partials.shape[0]
    row = lambda i: (i, 0)
    ins = [(a, (t_r, cols), row) for a in (w, m, v)]
    ins += [(partials, (None, t_r, cols), (lambda i, j=j: (j, i, 0))) for j in range(n_p)]
    outs = [((rows, cols), F32, (t_r, cols), row)] * 4
    return _rowwise(name, _adamw_fn, ins, outs, (rows // t_r,))


SMALL_PARAMS = ("b_ada", "ssm_a_re", "ssm_a_im", "ssm_log_dt", "ssm_b_re", "ssm_b_im", "ssm_c_re", "ssm_c_im",
                "ssm_d", "b_glu", "ln1_g", "ln1_b", "ln2_g", "ln2_b")
WEIGHTS = ("w_ada", "b_ada", "w_in", "w_sb_up", "ssm_a_re", "ssm_a_im", "ssm_log_dt", "ssm_b_re", "ssm_b_im",
           "ssm_c_re", "ssm_c_im", "ssm_d", "w_glu", "b_glu", "w_ssm_up", "w_out", "ln1_g", "ln1_b", "w_ffn_in",
           "w_ffn_out", "ln2_g", "ln2_b")
ARG_NAMES = (("x", "c") + WEIGHTS + ("loss_target",) + tuple("m_" + n for n in WEIGHTS)
             + tuple("v_" + n for n in WEIGHTS))


def _pack(arrs):
    flat = jnp.concatenate([a.reshape(-1) for a in arrs])
    pad = (-flat.shape[0]) % (PACK_ROWS * LANES)
    return jnp.pad(flat, (0, pad)).reshape(-1, LANES)


def _unpack(packed, like):
    lead = packed.shape[:-2]
    flat = packed.reshape(lead + (-1,))
    out, off = [], 0
    for a in like:
        out.append(flat[..., off:off + a.size].reshape(lead + a.shape))
        off += a.size
    return out


def kernel(x, c, w_ada, b_ada, w_in, w_sb_up, ssm_a_re, ssm_a_im, ssm_log_dt, ssm_b_re, ssm_b_im, ssm_c_re,
           ssm_c_im, ssm_d, w_glu, b_glu, w_ssm_up, w_out, ln1_g, ln1_b, w_ffn_in, w_ffn_out, ln2_g, ln2_b,
           loss_target, m_w_ada, m_b_ada, m_w_in, m_w_sb_up, m_ssm_a_re, m_ssm_a_im, m_ssm_log_dt, m_ssm_b_re,
           m_ssm_b_im, m_ssm_c_re, m_ssm_c_im, m_ssm_d, m_w_glu, m_b_glu, m_w_ssm_up, m_w_out, m_ln1_g, m_ln1_b,
           m_w_ffn_in, m_w_ffn_out, m_ln2_g, m_ln2_b, v_w_ada, v_b_ada, v_w_in, v_w_sb_up, v_ssm_a_re, v_ssm_a_im,
           v_ssm_log_dt, v_ssm_b_re, v_ssm_b_im, v_ssm_c_re, v_ssm_c_im, v_ssm_d, v_w_glu, v_b_glu, v_w_ssm_up,
           v_w_out, v_ln1_g, v_ln1_b, v_w_ffn_in, v_w_ffn_out, v_ln2_g, v_ln2_b):
    given = locals()
    return _train_step({n: given[n] for n in ARG_NAMES})


def _train_step(p):
    x0 = p["x"][0]
    target = p["loss_target"][0]
    seq, d = x0.shape
    depth = p["w_ada"].shape[0]
    n_ada = p["w_ada"].shape[2]
    n_in = p["w_in"].shape[2]
    sb_w = p["w_sb_up"].shape[1]
    ssm_w = p["w_ssm_up"].shape[1]
    n_up = p["w_sb_up"].shape[2]
    n_ffn = p["w_ffn_in"].shape[2]
    ffn = N_DEV * p["w_ffn_out"].shape[1]
    in_cols = N_DEV * n_in
    alpha = (2 * depth) ** 0.25
    resid_ln, resid_ln_mod = _make_resid_fns(alpha)
    t_r = min(512, seq)
    n_r = seq // t_r
    t_m = min(1024, seq)
    n_m = seq // t_m
    t_d = _tile(d)
    assert n_ffn * (N_DEV // 2) == ffn and sb_w % LANES == 0 and ssm_w % LANES == 0 and d % LANES == 0
    assert n_in % LANES == 0 and n_up % LANES == 0 and seq % t_m == 0 and in_cols == 3 * sb_w + ssm_w + 2 * d
    assert (3 * sb_w) % ssm_w == 0 and (3 * sb_w + ssm_w) % (2 * d) == 0
    assert sb_w % n_in == 0 and ssm_w % n_in == 0 and d % n_in == 0
    proj_starts = [c // n_in for c in (0, sb_w, 2 * sb_w, 3 * sb_w, 3 * sb_w + ssm_w)]

    bf = lambda a: a.astype(BF16)
    got = _exchange("gather_first", [], [bf(p["w_in"][0]), p["c"]])
    wg_in = [got[0]] + [None] * (depth - 1)
    c_all = got[1].reshape(N_DEV, d)
    small_names = ("w_sb_up", "w_ssm_up", "w_glu", "w_out")
    wg_ffn_in, wg_ffn_out, wg = [None] * depth, [None] * depth, {}

    c_pad = jnp.pad(c_all, ((0, 2 * SUBLANES - N_DEV), (0, 0)))
    c_act = _rowwise("silu_c", lambda v: v * jax.nn.sigmoid(v), [(c_pad, c_pad.shape, lambda i: (0, 0))],
                     [(c_pad.shape, F32, c_pad.shape, lambda i: (0, 0))], (1,))[0]
    rows_c = c_pad.shape[0]
    mod_cols = [
        _mm(f"mod_{l}", c_act, p["w_ada"],
            _spec((rows_c, d), lambda i, j, k: (0, 0)), _spec((None, d, n_ada), lambda i, j, k, l=l: (l, 0, 0)),
            _spec((rows_c, n_ada), lambda i, j, k: (0, 0)), (rows_c, n_ada), F32, (1, 1, 1), NN)
        for l in range(depth)]
    mod_send = jnp.stack([m[:N_DEV] for m in mod_cols], axis=1)
    mod_recv = _exchange("exchange_mod", [mod_send], [])[0]
    mod_nobias = jnp.swapaxes(mod_recv, 0, 1).reshape(depth, N_DEV * n_ada)
    full2 = lambda a: (a, a.shape, lambda i: (0, 0))
    mod = _rowwise("mod_bias", lambda a, b: a + b, [full2(mod_nobias), full2(p["b_ada"])],
                   [(mod_nobias.shape, F32, mod_nobias.shape, lambda i: (0, 0))], (1,))[0]
    vec = lambda a: a.reshape(1, -1)
    mods = [[vec(mod[l, j * d:(j + 1) * d]) for j in range(6)] for l in range(depth)]
    ln = {n: [vec(p[n][l]) for l in range(depth)] for n in ("ln1_g", "ln1_b", "ln2_g", "ln2_b")}

    row_spec = lambda width: ((t_r, width), lambda i: (i, 0))
    col_spec = lambda width, cb: ((t_r, width), lambda i, cb=cb: (i, cb))
    vec_spec = lambda width: ((1, width), lambda i: (0, 0))
    rows_in = lambda a: (a,) + row_spec(a.shape[1])
    vec_in = lambda a: (a,) + vec_spec(a.shape[1])
    row_out = lambda width, dt: ((seq, width), dt) + row_spec(width)

    s5 = [_s5_discretize(*[p[n][l] for n in ("ssm_a_re", "ssm_a_im", "ssm_log_dt", "ssm_b_re", "ssm_b_im",
                                               "ssm_c_re", "ssm_c_im")]) for l in range(depth)]
    s5_b16 = [(bs.astype(BF16), cs.astype(BF16), lam) for bs, cs, lam in s5]
    t_scan = min(1024, seq)
    s5_pw = [_s5_powers(p["ssm_a_re"][l], p["ssm_a_im"][l], p["ssm_log_dt"][l], t_scan // SUBLANES)
             for l in range(depth)]
    u_col = 3 * sb_w // LANES
    gates_cb = (3 * sb_w + ssm_w) // (2 * d)
    ssm_d = [vec(p["ssm_d"][l]) for l in range(depth)]
    b_glu = [vec(p["b_glu"][l]) for l in range(depth)]
    n_half = N_DEV // 2

    h = _rowwise("modulate_in", _modulate, [rows_in(x0), vec_in(mods[0][1]), vec_in(mods[0][0])],
                 [row_out(d, BF16)], (n_r,))[0]
    saved = []
    x_cur = x0
    for l in range(depth):
        sv = {"x_in": x_cur, "h": h}
        last = l == depth - 1
        t_n = _tile(n_in)
        r_n = n_in // t_n
        proj = _mm(f"proj_{l}", h, wg_in[l],
                   _spec((t_m, d), lambda i, j, k: (i, 0)),
                   _spec((None, d, t_n), lambda i, j, k, r=r_n: (j // r, 0, j % r)),
                   _spec((t_m, t_n), lambda i, j, k: (i, j)), (seq, in_cols), F32, (n_m, N_DEV * r_n, 1), NN,
                   reread=(True, True))
        arriving = [bf(p["w_ffn_in"][l]), bf(p["w_ffn_out"][l])] + ([bf(p[n]) for n in small_names] if l == 0 else [])
        (o_sb, o_sb32), got = _sb_attention_fwd(proj, sb_w, beside=_Exchange(gather=arriving))
        wg_ffn_in[l] = got[0]
        wg_ffn_out[l] = got[1].reshape(n_half, n_ffn, d)
        if l == 0:
            wg = dict(zip(small_names, got[2:]))
            for n in ("w_glu", "w_out"):
                wg[n] = jnp.swapaxes(wg[n], 0, 1).reshape(depth, -1, wg[n].shape[-1])
            for n in ("w_sb_up", "w_ssm_up"):
                wg[n] = jnp.transpose(wg[n], (1, 2, 0, 3)).reshape(depth, wg[n].shape[2], d)
        bs16, cs16, lam = s5_b16[l]
        (yc, states), got = _s5_scan_fwd(proj, u_col, bs16, cs16, lam, s5_pw[l][0], t_scan,
                                         beside=None if last else _Exchange(gather=[bf(p["w_in"][l + 1])]))
        if not last:
            wg_in[l + 1] = got[0]
        s5_out = _s5_head(f"s5_head_{l}", yc, proj, 3 * sb_w // ssm_w, ssm_d[l], b_glu[l], wg["w_glu"], l, t_r)

        merged, y_sb, y_ssm = _up_merge(f"up_merge_{l}", o_sb, s5_out, proj, gates_cb, wg["w_sb_up"], wg["w_ssm_up"],
                                        l, t_r)
        y_mix = _mm(f"out_proj_{l}", merged, wg["w_out"],
                    _spec((t_m, d), lambda i, j, k: (i, 0)), _spec((None, d, t_d), lambda i, j, k, l=l: (l, 0, j)),
                    _spec((t_m, t_d), lambda i, j, k: (i, j)), (seq, d), F32, (n_m, d // t_d, 1), NN)
        vecs_a = [mods[l][2], ln["ln1_g"][l], ln["ln1_b"][l], mods[l][4], mods[l][3]]
        x_mid, h2 = _rowwise(f"resid_mix_{l}", resid_ln_mod, [rows_in(x_cur), rows_in(y_mix)] + [vec_in(v) for v in vecs_a],
                             [row_out(d, F32), row_out(d, BF16)], (n_r,))
        a_ffn, f_act = _ffn_in_swiglu(f"ffn_in_{l}", h2, wg_ffn_in[l], t_r)
        y_ffn = _mm(f"ffn_out_{l}", f_act, wg_ffn_out[l],
                    _spec((None, t_m, n_ffn), lambda i, j, k: (k, i, 0)),
                    _spec((None, n_ffn, t_d), lambda i, j, k: (k, 0, j)),
                    _spec((t_m, t_d), lambda i, j, k: (i, j)), (seq, d), F32, (n_m, d // t_d, n_half), NN)
        vecs_b = [mods[l][5], ln["ln2_g"][l], ln["ln2_b"][l]] + ([] if last else [mods[l + 1][1], mods[l + 1][0]])
        outs_b = [row_out(d, F32)] + ([] if last else [row_out(d, BF16)])
        res = _rowwise(f"resid_ffn_{l}", resid_ln if last else resid_ln_mod,
                       [rows_in(x_mid), rows_in(y_ffn)] + [vec_in(v) for v in vecs_b], outs_b, (n_r,))
        sv.update(proj=proj, o_sb=o_sb, o_sb32=o_sb32, yc=yc, states=states, s5_out=s5_out,
                  y_sb=y_sb, y_ssm=y_ssm, merged=merged, y_mix=y_mix, x_mid=x_mid, h2=h2, a_ffn=a_ffn, f_act=f_act,
                  y_ffn=y_ffn, vecs_a=vecs_a, vecs_b=vecs_b)
        saved.append(sv)
        x_cur = res[0]
        h = None if last else res[1]

    loss_part, d_x = _loss_head(x_cur, target, t_r)
    loss = lax.psum(loss_part[0, 0], MESH_AXES)

    d_h_next = None
    grads = {n: [None] * depth for n in WEIGHTS}
    d_mod = [[None] * 6 for _ in range(depth)]
    land = {}
    waiting = []
    row_wrt = lambda i, width, dt: (i, "row", (seq, width), dt) + row_spec(width)
    sum_wrt = lambda i, width: (i, "sum", (1, width), F32) + vec_spec(width)
    for l in reversed(range(depth)):
        sv = saved[l]
        last = l == depth - 1
        ins_b = [rows_in(sv["x_mid"]), rows_in(sv["y_ffn"])] + [vec_in(v) for v in sv["vecs_b"]]
        cts_b = [rows_in(d_x)] + ([] if last else [rows_in(d_h_next)])
        wrt_b = [row_wrt(0, d, F32), row_wrt(1, d, BF16)] + [sum_wrt(2 + j, d) for j in range(len(sv["vecs_b"]))]
        res = _rowwise_vjp(f"resid_ffn_bwd_{l}", resid_ln if last else resid_ln_mod, ins_b, cts_b, wrt_b, (n_r,))
        d_x_mid, d_y_ffn = res[0], res[1]
        d_mod[l][5], grads["ln2_g"][l], grads["ln2_b"][l] = res[2], res[3], res[4]
        if not last:
            d_mod[l + 1][1], d_mod[l + 1][0] = res[5], res[6]
        d_a = _ffn_out_dx_swiglu(f"ffn_out_dx_{l}", d_y_ffn, wg_ffn_out[l], sv["a_ffn"], t_r).reshape(N_DEV, seq, n_ffn)
        g_ffn_out = _mm(f"ffn_out_dw_{l}", sv["f_act"], d_y_ffn,
                        _spec((None, t_m, n_ffn), lambda i, j, k: (i, k, 0)), _spec((t_m, t_d), lambda i, j, k: (k, j)),
                        _spec((None, n_ffn, t_d), lambda i, j, k: (i, 0, j)), (n_half, n_ffn, d), GRAD_WIRE,
                        (n_half, d // t_d, n_m), TN, reread=(False, True))
        d_h2 = _mm(f"ffn_in_dx_{l}", d_a, wg_ffn_in[l],
                   _spec((None, t_m, n_ffn), lambda i, j, k: (k, i, 0)),
                   _spec((None, t_d, n_ffn), lambda i, j, k: (k, j, 0)),
                   _spec((t_m, t_d), lambda i, j, k: (i, j)), (seq, d), BRANCH_CT, (n_m, d // t_d, N_DEV), NT)
        g_ffn_in = _mm(f"ffn_in_dw_{l}", sv["h2"], d_a,
                       _spec((t_m, t_d), lambda i, j, k: (k, j)), _spec((None, t_m, n_ffn), lambda i, j, k: (i, k, 0)),
                       _spec((None, t_d, n_ffn), lambda i, j, k: (i, j, 0)), (N_DEV, d, n_ffn), GRAD_WIRE,
                       (N_DEV, d // t_d, n_m), TN, reread=(True, False))
        ins_a = [rows_in(sv["x_in"]), rows_in(sv["y_mix"])] + [vec_in(v) for v in sv["vecs_a"]]
        wrt_a = [row_wrt(0, d, F32), row_wrt(1, d, BF16)] + [sum_wrt(2 + j, d) for j in range(5)]
        res = _rowwise_vjp(f"resid_mix_bwd_{l}", resid_ln_mod, ins_a, [rows_in(d_x_mid), rows_in(d_h2)], wrt_a, (n_r,))
        d_x_in, d_y_mix = res[0], res[1]
        d_mod[l][2], grads["ln1_g"][l], grads["ln1_b"][l], d_mod[l][4], d_mod[l][3] = res[2:7]
        d_merged = _mm(f"out_proj_dx_{l}", d_y_mix, wg["w_out"],
                       _spec((t_m, d), lambda i, j, k: (i, 0)), _spec((None, t_d, d), lambda i, j, k, l=l: (l, j, 0)),
                       _spec((t_m, t_d), lambda i, j, k: (i, j)), (seq, d), BRANCH_CT, (n_m, d // t_d, 1), NT)
        g_out = _mm(f"out_proj_dw_{l}", sv["merged"], d_y_mix,
                    _spec((t_m, t_d), lambda i, j, k: (k, i)), _spec((t_m, t_d), lambda i, j, k: (k, j)),
                    _spec((t_d, t_d), lambda i, j, k: (i, j)), (d, d), GRAD_WIRE, (d // t_d, d // t_d, n_m), TN, reread=(d > t_d, d > t_d))
        gates = (sv["proj"],) + col_spec(2 * d, gates_cb)
        d_y_sb, d_y_ssm, d_gates = _rowwise_vjp(
            f"merge_bwd_{l}", _merge_fn, [rows_in(sv["y_sb"]), rows_in(sv["y_ssm"]), gates], [rows_in(d_merged)],
            [row_wrt(0, d, BF16), row_wrt(1, d, BF16), row_wrt(2, 2 * d, BF16)], (n_r,))

        def up_bwd(name, act, d_y, w, dx_dtype, l=l):
            k_w = act.shape[1]
            dx = _mm(name + "_dx", d_y, w, _spec((t_m, d), lambda i, j, k: (i, 0)),
                     _spec((None, k_w, d), lambda i, j, k: (l, 0, 0)),
                     _spec((t_m, k_w), lambda i, j, k: (i, 0)), (seq, k_w), dx_dtype, (n_m, 1, 1), NT)
            dw = _mm(name + "_dw", act, d_y, _spec((t_m, k_w), lambda i, j, k: (k, 0)),
                     _spec((t_m, t_d), lambda i, j, k: (k, j)),
                     _spec((k_w, t_d), lambda i, j, k: (0, j)), (k_w, d), GRAD_WIRE, (1, d // t_d, n_m), TN,
                     reread=(d > t_d, False))
            return dx, jnp.swapaxes(dw.reshape(k_w, N_DEV, n_up), 0, 1)

        d_o_sb, g_sb_up = up_bwd(f"sb_up_{l}", sv["o_sb"], d_y_sb, wg["w_sb_up"], BF16)
        d_s5_out, g_ssm_up = up_bwd(f"ssm_up_{l}", sv["s5_out"], d_y_ssm, wg["w_ssm_up"], BRANCH_CT)
        waiting += [("w_ffn_in", g_ffn_in), ("w_ffn_out", g_ffn_out.reshape(N_DEV, -1, d)),
                    ("w_out", g_out.reshape(N_DEV, -1, d)), ("w_sb_up", g_sb_up), ("w_ssm_up", g_ssm_up)]
        levels = [l + 1] * (len(waiting) - 5) + [l] * 5
        (d_q, d_k, d_v), got = _sb_attention_bwd(
            sv["proj"], sv["o_sb32"], d_o_sb, sb_w,
            beside=_Exchange(layered=[(g, lv, depth, land.get(n)) for (n, g), lv in zip(waiting, levels)]))
        land.update({n: buf for (n, _), buf in zip(waiting, got)})
        d_yc, d_u_skip, g_glu, grads["ssm_d"][l], grads["b_glu"][l] = _s5_head_bwd(
            f"s5_head_bwd_{l}", sv["yc"], sv["proj"], 3 * sb_w // ssm_w, d_s5_out, ssm_d[l], b_glu[l], wg["w_glu"], l, t_r)
        bs16, cs16, lam = s5_b16[l]
        d_u, d_bs, d_cs, d_lam = _s5_scan_bwd(sv["proj"], u_col, sv["states"], d_yc, d_u_skip, bs16, cs16, lam,
                                              s5_pw[l][1], t_scan)
        raw = [p[n][l] for n in ("ssm_a_re", "ssm_a_im", "ssm_log_dt", "ssm_b_re", "ssm_b_im", "ssm_c_re", "ssm_c_im")]
        _, pull = jax.vjp(_s5_discretize, *raw)
        (grads["ssm_a_re"][l], grads["ssm_a_im"][l], grads["ssm_log_dt"][l], grads["ssm_b_re"][l],
         grads["ssm_b_im"][l], grads["ssm_c_re"][l], grads["ssm_c_im"][l]) = pull((d_bs, d_cs, d_lam))
        d_proj = [d_q, d_k, d_v, d_u, d_gates]
        g_in = _mm_pieces(f"proj_dw_{l}", d_proj, proj_starts, n_in, lambda i, j, k: i, sv["h"],
                          _spec((t_m, t_d), lambda i, j, k: (k, j)), False, lambda i, j, k: k,
                          _spec((None, t_d, n_in), lambda i, j, k: (i, j, 0)), (N_DEV, d, n_in), GRAD_WIRE,
                          (N_DEV, d // t_d, n_m), TN)
        waiting = [("w_in", g_in), ("w_glu", g_glu.reshape(N_DEV, -1, ssm_w))]
        closing = _Exchange(layered=[(g, 0, depth, land.get(n)) for n, g in waiting]) if l == 0 else None
        d_h = _mm_pieces(f"proj_dx_{l}", d_proj, proj_starts, n_in, lambda i, j, k: k, wg_in[l],
                         _spec((None, t_d, n_in), lambda i, j, k: (k, j, 0)), True, lambda i, j, k: i,
                         _spec((t_m, t_d), lambda i, j, k: (i, j)), (seq, d), BRANCH_CT, (n_m, d // t_d, N_DEV), NT,
                         beside=closing)
        if l == 0:
            d_h, got = d_h
            land.update({n: buf for (n, _), buf in zip(waiting, got)})
        d_x, d_h_next = d_x_in, d_h
    res = _rowwise_vjp("modulate_in_bwd", lambda v, sc, sh: (v, _modulate(v, sc, sh)),
                       [rows_in(x0), vec_in(mods[0][1]), vec_in(mods[0][0])], [rows_in(d_x), rows_in(d_h_next)],
                       [row_wrt(0, d, F32), sum_wrt(1, d), sum_wrt(2, d)], (n_r,))
    grad_x, d_mod[0][1], d_mod[0][0] = res

    d_mod_rows = jnp.concatenate([jnp.concatenate(d_mod[l], axis=1) for l in range(depth)], axis=0)
    grads["b_ada"] = [d_mod_rows[l] for l in range(depth)]
    small_local = [jnp.stack([g.reshape(p[n].shape[1:]) for g in grads[n]]) for n in SMALL_PARAMS]
    d_mod_send = jnp.swapaxes(d_mod_rows.reshape(depth, N_DEV, n_ada), 0, 1)
    small_sum, (d_mod_cols,) = _reduce_packed("exchange_last", _pack(small_local), [d_mod_send])
    d_mod_pad = jnp.pad(jnp.swapaxes(d_mod_cols, 0, 1), ((0, 0), (0, rows_c - N_DEV), (0, 0)))
    g_ada = [
        _mm(f"mod_dw_{l}", c_act, d_mod_pad,
            _spec((rows_c, d), lambda i, j, k: (0, 0)), _spec((None, rows_c, n_ada), lambda i, j, k, l=l: (l, 0, 0)),
            _spec((d, n_ada), lambda i, j, k: (0, 0)), (d, n_ada), F32, (1, 1, 1), TN)
        for l in range(depth)]

    out = {}

    def update(name, partials):
        shape = p[name].shape
        two_d = lambda a: a.reshape(-1, shape[-1])
        res = _adamw("adamw_" + name, two_d(p[name]), two_d(p["m_" + name]), two_d(p["v_" + name]),
                     partials.reshape(partials.shape[0], -1, shape[-1]))
        out[name] = [r.reshape(shape) for r in res]

    update("w_ada", jnp.stack(g_ada)[None])
    for n in ("w_in", "w_sb_up", "w_ssm_up", "w_ffn_in", "w_glu", "w_out", "w_ffn_out"):
        update(n, land[n])
    small_w = [p[n] for n in SMALL_PARAMS]
    res = _adamw("adamw_small", _pack(small_w), _pack([p["m_" + n] for n in SMALL_PARAMS]),
                 _pack([p["v_" + n] for n in SMALL_PARAMS]), small_sum[None])
    for kind, packed in enumerate(res):
        for n, a in zip(SMALL_PARAMS, _unpack(packed, small_w)):
            out.setdefault(n, [None] * 4)[kind] = a

    return ((loss, grad_x[None]) + tuple(out[n][0] for n in WEIGHTS) + tuple(out[n][1] for n in WEIGHTS)
            + tuple(out[n][2] for n in WEIGHTS) + tuple(out[n][3] for n in WEIGHTS))
```

```python
import jax
import jax.numpy as jnp
from jax import lax
from jax.experimental import pallas as pl
from jax.experimental.pallas import tpu as pltpu

F32 = jnp.float32
BF16 = jnp.bfloat16
GRAD_WIRE = BF16
FFN_ACT = BF16
BRANCH_CT = BF16

N_DEV = 8
LANES = 128
SUBLANES = 8
VMEM_BYTES = 64 * 1024 * 1024
HEAD_DIM = 64
SB_BLOCK = 256
SLAB_GROUPS = 8
LN_EPS = 1e-5
ADAM_LR, ADAM_B1, ADAM_B2, ADAM_EPS, ADAM_WD, ADAM_STEP = 0.001, 0.9, 0.999, 1e-08, 0.01, 10
SB_UNDERFLOW = -120.0

PACK_ROWS = 256
MESH_AXES = ("x", "y", "c")


def _vmem_limit(block_bytes):
    return int(min(max(3 * block_bytes + (8 << 20), 24 << 20), VMEM_BYTES - (8 << 20)))


def _nbytes(shape, dtype):
    n = 1
    for d in shape:
        if d is not None:
            n *= d
    return n * jnp.dtype(dtype).itemsize


def _spec(shape, fn):
    return pl.BlockSpec(shape, fn)


class _Exchange:
    def __init__(self, scatter=(), gather=(), layered=()):
        self.arrs = list(scatter) + [a for a, _, _, _ in layered] + list(gather)
        self.n = len(self.arrs)
        self.n_sc = len(scatter) + len(layered)
        self.layer = [None] * len(scatter) + [l for _, l, _, _ in layered] + [None] * len(gather)
        self.shapes = ([a.shape for a in scatter] + [(N_DEV, dp) + a.shape[1:] for a, _, dp, _ in layered]
                       + [(N_DEV,) + a.shape for a in gather])
        self.held = [(len(scatter) + i, b) for i, (_, _, _, b) in enumerate(layered) if b is not None]
        self.operands = self.arrs + [b for _, b in self.held]
        hbm = pl.BlockSpec(memory_space=pltpu.HBM)
        self.in_specs = [hbm] * len(self.operands)
        self.out_specs = [hbm] * self.n
        self.out_shape = [jax.ShapeDtypeStruct(s, a.dtype) for s, a in zip(self.shapes, self.arrs)]
        self.scratch = [pltpu.SemaphoreType.DMA((self.n, N_DEV - 1)), pltpu.SemaphoreType.DMA((self.n, N_DEV - 1)),
                        pltpu.SemaphoreType.DMA((self.n,))]

    def aliases(self, first_in, first_out):
        return {first_in + self.n + i: first_out + a for i, (a, _) in enumerate(self.held)}

    def copies(self, ins, outs, sems):
        send_sems, recv_sems, own_sems = sems
        x, y, c = lax.axis_index("x"), lax.axis_index("y"), lax.axis_index("c")
        me = 4 * x + 2 * y + c
        landing = [outs[a].at[me] if self.layer[a] is None else outs[a].at[me, self.layer[a]] for a in range(self.n)]
        out = [pltpu.make_async_copy(ins[a].at[me] if a < self.n_sc else ins[a], landing[a], own_sems.at[a])
               for a in range(self.n)]
        for k in range(1, N_DEV):
            px = 1 - x if k & 4 else x
            py = 1 - y if k & 2 else y
            pc = 1 - c if k & 1 else c
            peer = 4 * px + 2 * py + pc
            for a in range(self.n):
                out.append(pltpu.make_async_remote_copy(
                    src_ref=ins[a].at[peer] if a < self.n_sc else ins[a], dst_ref=landing[a],
                    send_sem=send_sems.at[a, k - 1], recv_sem=recv_sems.at[a, k - 1],
                    device_id=(px, py, pc), device_id_type=pl.DeviceIdType.MESH))
        return out


def _exchange(name, scatter, gather, layered=()):
    ex = _Exchange(scatter, gather, layered)

    def body(*refs):
        copies = ex.copies(refs[:ex.n], refs[len(ex.operands):len(ex.operands) + ex.n], refs[-3:])
        for cp in copies:
            cp.start()
        for cp in copies:
            cp.wait()

    return pl.pallas_call(body, name=name, in_specs=ex.in_specs, out_specs=ex.out_specs, out_shape=ex.out_shape,
                          input_output_aliases=ex.aliases(0, 0), scratch_shapes=ex.scratch)(*ex.operands)


def _reduce_packed(name, packed, scatter):
    rows = packed.shape[0]
    blk = rows // N_DEV
    ex = _Exchange(scatter=[packed.reshape(N_DEV, blk, LANES)] + list(scatter))
    n_in = len(ex.operands)

    def body(*refs):
        ins, outs = refs[:ex.n], refs[n_in:n_in + ex.n]
        total_ref = refs[n_in + ex.n]
        sems, (send2, recv2, own2, load_sem) = refs[n_in + ex.n + 1:n_in + ex.n + 4], refs[n_in + ex.n + 4:-2]
        land_v, sum_v = refs[-2:]
        copies = ex.copies(ins, outs, sems)
        for cp in copies:
            cp.start()
        for cp in copies:
            cp.wait()
        load = pltpu.make_async_copy(outs[0], land_v, load_sem)
        load.start()
        load.wait()
        acc = land_v[0]
        for i in range(1, N_DEV):
            acc = acc + land_v[i]
        sum_v[...] = acc
        x, y, c = lax.axis_index("x"), lax.axis_index("y"), lax.axis_index("c")
        me = 4 * x + 2 * y + c
        back = [pltpu.make_async_copy(sum_v, total_ref.at[me], own2)]
        for k in range(1, N_DEV):
            peer = (1 - x if k & 4 else x, 1 - y if k & 2 else y, 1 - c if k & 1 else c)
            back.append(pltpu.make_async_remote_copy(
                src_ref=sum_v, dst_ref=total_ref.at[me], send_sem=send2.at[k - 1], recv_sem=recv2.at[k - 1],
                device_id=peer, device_id_type=pl.DeviceIdType.MESH))
        for cp in back:
            cp.start()
        for cp in back:
            cp.wait()

    hbm = pl.BlockSpec(memory_space=pltpu.HBM)
    res = pl.pallas_call(
        body, name=name, in_specs=ex.in_specs, out_specs=ex.out_specs + [hbm],
        out_shape=ex.out_shape + [jax.ShapeDtypeStruct((N_DEV, blk, LANES), F32)],
        scratch_shapes=ex.scratch + [pltpu.SemaphoreType.DMA((N_DEV - 1,)), pltpu.SemaphoreType.DMA((N_DEV - 1,)),
                                     pltpu.SemaphoreType.DMA, pltpu.SemaphoreType.DMA,
                                     pltpu.VMEM((N_DEV, blk, LANES), F32), pltpu.VMEM((blk, LANES), F32)],
    )(*ex.operands)
    return res[-1].reshape(rows, LANES), res[1:-1]


def _call_beside(ex, body, name, grid, in_specs, out_specs, out_shape, scratch_shapes, vmem_bytes, operands,
                 semantics, in_hbm=True):
    if in_hbm:
        operands = [_in_hbm(a) for a in operands]
    if ex is None:
        res = pl.pallas_call(
            body, name=name, grid=grid, in_specs=in_specs, out_specs=out_specs, out_shape=out_shape,
            scratch_shapes=scratch_shapes,
            compiler_params=pltpu.CompilerParams(dimension_semantics=semantics, vmem_limit_bytes=vmem_bytes),
        )(*operands)
        return res, None
    n_in, n_out, n_scr = len(in_specs), len(out_specs), len(scratch_shapes)
    n_xin = len(ex.operands)

    def fused(*refs):
        mine = refs[:n_in] + refs[n_in + n_xin:n_in + n_xin + n_out]
        mine += refs[n_in + n_xin + n_out + ex.n:n_in + n_xin + n_out + ex.n + n_scr]
        first = pl.program_id(0) == 0
        last = pl.program_id(0) == grid[0] - 1
        for dim in range(1, len(grid)):
            first = jnp.logical_and(first, pl.program_id(dim) == 0)
            last = jnp.logical_and(last, pl.program_id(dim) == grid[dim] - 1)
        x_ins = refs[n_in:n_in + ex.n]
        x_outs = refs[n_in + n_xin + n_out:n_in + n_xin + n_out + ex.n]

        @pl.when(first)
        def _():
            for cp in ex.copies(x_ins, x_outs, refs[-3:]):
                cp.start()

        body(*mine)

        @pl.when(last)
        def _():
            for cp in ex.copies(x_ins, x_outs, refs[-3:]):
                cp.wait()

    res = pl.pallas_call(
        fused, name=name, grid=grid, in_specs=list(in_specs) + ex.in_specs, out_specs=list(out_specs) + ex.out_specs,
        out_shape=list(out_shape) + ex.out_shape, input_output_aliases=ex.aliases(n_in, n_out),
        scratch_shapes=list(scratch_shapes) + ex.scratch,
        compiler_params=pltpu.CompilerParams(dimension_semantics=("arbitrary",) * len(grid),
                                             vmem_limit_bytes=vmem_bytes),
    )(*operands, *ex.operands)
    return res[:n_out], res[n_out:]


NN = (((1,), (0,)), ((), ()))
NT = (((1,), (1,)), ((), ()))
TN = (((0,), (0,)), ((), ()))


def _in_hbm(a):
    return pltpu.with_memory_space_constraint(a, pltpu.HBM)


def _mm(name, a, b, a_spec, b_spec, o_spec, o_shape, o_dtype, grid, dims, beside=None, reread=(False, True)):
    nk = grid[2]
    a, b = (x if again else _in_hbm(x) for x, again in zip((a, b), reread))
    acc_shape = tuple(d for d in o_spec.block_shape if d is not None)

    def product(a_ref, b_ref):
        return lax.dot_general(a_ref[...].astype(BF16), b_ref[...].astype(BF16), dims, preferred_element_type=F32)

    def body_once(a_ref, b_ref, o_ref):
        o_ref[...] = product(a_ref, b_ref).astype(o_ref.dtype)

    def body(a_ref, b_ref, o_ref, acc_ref):
        k = pl.program_id(2)

        @pl.when(k == 0)
        def _():
            acc_ref[...] = product(a_ref, b_ref)

        @pl.when(k > 0)
        def _():
            acc_ref[...] += product(a_ref, b_ref)

        @pl.when(k == nk - 1)
        def _():
            o_ref[...] = acc_ref[...].astype(o_ref.dtype)

    blk = (_nbytes(a_spec.block_shape, a.dtype) + _nbytes(b_spec.block_shape, b.dtype)
           + _nbytes(acc_shape, o_dtype) + _nbytes(acc_shape, F32))
    res, got = _call_beside(
        beside, body_once if nk == 1 else body, name, grid, [a_spec, b_spec], [o_spec],
        [jax.ShapeDtypeStruct(o_shape, o_dtype)], [] if nk == 1 else [pltpu.VMEM(acc_shape, F32)],
        _vmem_limit(blk), (a, b), ("parallel", "parallel", "arbitrary"), in_hbm=False)
    return res[0] if beside is None else (res[0], got)


def _mm_pieces(name, pieces, starts, width, step_block, other, other_spec, pieces_first, piece_rows, o_spec, o_shape,
               o_dtype, grid, dims, beside=None):
    n_p, nk = len(pieces), grid[2]
    acc_shape = tuple(s for s in o_spec.block_shape if s is not None)

    def which(i, j, k):
        blk = step_block(i, j, k)
        idx = 0
        for s in starts[1:]:
            idx = idx + (blk >= s).astype(jnp.int32)
        return idx, blk

    def piece_spec(p, t_rows):
        def index(i, j, k):
            idx, blk = which(i, j, k)
            mine = idx == p
            return jnp.where(mine, piece_rows(i, j, k), 0), jnp.where(mine, blk - starts[p], 0)
        return _spec((t_rows, width), index)

    def body(*refs):
        p_refs = refs[:n_p] if pieces_first else refs[1:1 + n_p]
        other_ref = refs[n_p] if pieces_first else refs[0]
        o_ref, acc_ref = refs[n_p + 1], refs[n_p + 2]
        i, j, k = pl.program_id(0), pl.program_id(1), pl.program_id(2)

        @pl.when(k == 0)
        def _():
            acc_ref[...] = jnp.zeros_like(acc_ref)

        idx, _ = which(i, j, k)
        for p in range(n_p):
            @pl.when(idx == p)
            def _(p=p):
                mine, fixed = p_refs[p][...].astype(BF16), other_ref[...].astype(BF16)
                pair = (mine, fixed) if pieces_first else (fixed, mine)
                acc_ref[...] += lax.dot_general(pair[0], pair[1], dims, preferred_element_type=F32)

        @pl.when(k == nk - 1)
        def _():
            o_ref[...] = acc_ref[...].astype(o_ref.dtype)

    t_rows = other_spec.block_shape[-2] if not pieces_first else o_spec.block_shape[-2]
    specs = [piece_spec(p, t_rows) for p in range(n_p)]
    in_specs = specs + [other_spec] if pieces_first else [other_spec] + specs
    operands = list(pieces) + [other] if pieces_first else [other] + list(pieces)
    blk = (n_p * 4 * t_rows * width + _nbytes(other_spec.block_shape, other.dtype)
           + _nbytes(acc_shape, o_dtype) + _nbytes(acc_shape, F32))
    res, got = _call_beside(
        beside, body, name, grid, in_specs, [o_spec], [jax.ShapeDtypeStruct(o_shape, o_dtype)],
        [pltpu.VMEM(acc_shape, F32)], _vmem_limit(blk), operands, ("parallel", "parallel", "arbitrary"), in_hbm=False)
    return res[0] if beside is None else (res[0], got)


def _swiglu_fn(gate_up):
    gate, up = gate_up[0], gate_up[1]
    return gate * jax.nn.sigmoid(gate) * up


def _ffn_in_swiglu(name, h, w, t_m):
    seq, d = h.shape
    n_half, n = w.shape[0] // 2, w.shape[2]

    def body(h_ref, wg_ref, wu_ref, a_ref, f_ref):
        hb = h_ref[...]
        a_ref[0] = lax.dot_general(hb, wg_ref[...], NN, preferred_element_type=F32).astype(a_ref.dtype)
        a_ref[1] = lax.dot_general(hb, wu_ref[...], NN, preferred_element_type=F32).astype(a_ref.dtype)
        f_ref[...] = _swiglu_fn(a_ref[...].astype(F32)).astype(f_ref.dtype)

    blk = 2 * t_m * d + 4 * d * n + 6 * t_m * n + 12 * t_m * n
    return pl.pallas_call(
        body, name=name, grid=(seq // t_m, n_half),
        in_specs=[_spec((t_m, d), lambda i, j: (i, 0)), _spec((None, d, n), lambda i, j: (j, 0, 0)),
                  _spec((None, d, n), lambda i, j: (j + n_half, 0, 0))],
        out_specs=[_spec((2, None, t_m, n), lambda i, j: (0, j, i, 0)), _spec((None, t_m, n), lambda i, j: (j, i, 0))],
        out_shape=[jax.ShapeDtypeStruct((2, n_half, seq, n), FFN_ACT), jax.ShapeDtypeStruct((n_half, seq, n), BF16)],
        compiler_params=pltpu.CompilerParams(dimension_semantics=("parallel", "parallel"),
                                             vmem_limit_bytes=_vmem_limit(blk)),
    )(h, w, w)


def _ffn_out_dx_swiglu(name, d_y, w, a, t_m):
    seq, d = d_y.shape
    n_half, n = w.shape[0], w.shape[1]

    def body(dy_ref, w_ref, a_ref, da_ref):
        d_f = lax.dot_general(dy_ref[...], w_ref[...], NT, preferred_element_type=F32)
        gate, up = a_ref[0].astype(F32), a_ref[1].astype(F32)
        s = jax.nn.sigmoid(gate)
        gs = gate * s
        da_ref[0] = (d_f * up * (s + gs * (1.0 - s))).astype(da_ref.dtype)
        da_ref[1] = (d_f * gs).astype(da_ref.dtype)

    blk = 2 * t_m * d + 2 * d * n + 8 * t_m * n + 24 * t_m * n
    return pl.pallas_call(
        body, name=name, grid=(seq // t_m, n_half),
        in_specs=[_spec((t_m, d), lambda i, j: (i, 0)), _spec((None, n, d), lambda i, j: (j, 0, 0)),
                  _spec((2, None, t_m, n), lambda i, j: (0, j, i, 0))],
        out_specs=_spec((2, None, t_m, n), lambda i, j: (0, j, i, 0)),
        out_shape=jax.ShapeDtypeStruct((2, n_half, seq, n), BF16),
        compiler_params=pltpu.CompilerParams(dimension_semantics=("parallel", "parallel"),
                                             vmem_limit_bytes=_vmem_limit(blk)),
    )(d_y, w, a)


def _merge_fn(y_sb, y_ssm, gates):
    half = gates.shape[-1] // 2
    return jax.nn.sigmoid(gates[:, :half]) * y_sb + jax.nn.sigmoid(gates[:, half:]) * y_ssm


def _up_merge(name, o_sb, s5_out, proj, gates_cb, w_sb, w_ssm, layer, t_rows):
    seq = o_sb.shape[0]
    d = w_sb.shape[2]

    def body(o_ref, s_ref, g_ref, w1_ref, w2_ref, m_ref, y1_ref, y2_ref):
        y_sb = lax.dot_general(o_ref[...], w1_ref[...], NN, preferred_element_type=F32)
        y_ssm = lax.dot_general(s_ref[...], w2_ref[...], NN, preferred_element_type=F32)
        m_ref[...] = _merge_fn(y_sb, y_ssm, g_ref[...]).astype(m_ref.dtype)
        y1_ref[...] = y_sb.astype(y1_ref.dtype)
        y2_ref[...] = y_ssm.astype(y2_ref.dtype)

    row = lambda width: _spec((t_rows, width), lambda i: (i, 0))
    whole = lambda w: _spec((None,) + w.shape[1:], lambda i: (layer, 0, 0))
    blk = t_rows * (2 * o_sb.shape[1] + 2 * s5_out.shape[1] + 8 * d + 6 * d + 24 * d) + 4 * d * (o_sb.shape[1] + s5_out.shape[1])
    return pl.pallas_call(
        body, name=name, grid=(seq // t_rows,),
        in_specs=[row(o_sb.shape[1]), row(s5_out.shape[1]), _spec((t_rows, 2 * d), lambda i: (i, gates_cb)),
                  whole(w_sb), whole(w_ssm)],
        out_specs=[row(d)] * 3, out_shape=[jax.ShapeDtypeStruct((seq, d), BF16)] * 3,
        compiler_params=pltpu.CompilerParams(dimension_semantics=("parallel",), vmem_limit_bytes=_vmem_limit(blk)),
    )(_in_hbm(o_sb), _in_hbm(s5_out), _in_hbm(proj), w_sb, w_ssm)


def _tile(n, pref=1024):
    t = pref
    while t >= LANES:
        if n % t == 0:
            return t
        t -= LANES
    return n


def _rowwise(name, fn, ins, outs, grid):
    n_in = len(ins)

    def body(*refs):
        vals = fn(*[r[...].astype(F32) for r in refs[:n_in]])
        if not isinstance(vals, (tuple, list)):
            vals = (vals,)
        for r, v in zip(refs[n_in:], vals):
            r[...] = v.astype(r.dtype)

    blk = sum(_nbytes(bs, a.dtype) for a, bs, _ in ins) + sum(_nbytes(bs, d) + _nbytes(bs, F32) for _, d, bs, _ in outs)
    return pl.pallas_call(
        body, name=name, grid=grid,
        in_specs=[_spec(bs, im) for _, bs, im in ins],
        out_specs=[_spec(bs, im) for _, _, bs, im in outs],
        out_shape=[jax.ShapeDtypeStruct(s, d) for s, d, _, _ in outs],
        compiler_params=pltpu.CompilerParams(dimension_semantics=("parallel",) * len(grid),
                                             vmem_limit_bytes=_vmem_limit(2 * blk)),
    )(*[_in_hbm(a) for a, _, _ in ins])


def _rowwise_vjp(name, fn, ins, cts, wrt, grid):
    n_in, n_ct = len(ins), len(cts)
    idx = [w[0] for w in wrt]

    def body(*refs):
        prim = [r[...].astype(F32) for r in refs[:n_in]]
        ct = tuple(r[...].astype(F32) for r in refs[n_in:n_in + n_ct])
        o_refs = refs[n_in + n_ct:]

        def g(*sel):
            full = list(prim)
            for i, s in zip(idx, sel):
                full[i] = s
            out = fn(*full)
            return tuple(out) if isinstance(out, (tuple, list)) else (out,)

        _, pull = jax.vjp(g, *[prim[i] for i in idx])
        grads = pull(ct)
        first = pl.program_id(0) == 0
        for d in range(1, len(grid)):
            first = jnp.logical_and(first, pl.program_id(d) == 0)
        for w, o_ref, gr in zip(wrt, o_refs, grads):
            if w[1] == "row":
                o_ref[...] = gr.astype(o_ref.dtype)
            else:
                @pl.when(first)
                def _(o_ref=o_ref):
                    o_ref[...] = jnp.zeros_like(o_ref)

                o_ref[...] += gr.astype(o_ref.dtype)

    blk = (sum(_nbytes(bs, a.dtype) + _nbytes(bs, F32) for a, bs, _ in list(ins) + list(cts))
           + sum(_nbytes(w[4], w[3]) + _nbytes(w[4], F32) for w in wrt))
    return pl.pallas_call(
        body, name=name, grid=grid,
        in_specs=[_spec(bs, im) for _, bs, im in list(ins) + list(cts)],
        out_specs=[_spec(w[4], w[5]) for w in wrt],
        out_shape=[jax.ShapeDtypeStruct(w[2], w[3]) for w in wrt],
        compiler_params=pltpu.CompilerParams(dimension_semantics=("arbitrary",) * len(grid),
                                             vmem_limit_bytes=_vmem_limit(2 * blk)),
    )(*[_in_hbm(a) for a, _, _ in list(ins) + list(cts)])


def _normalize(x):
    mu = jnp.mean(x, axis=-1, keepdims=True)
    xc = x - mu
    var = jnp.mean(xc * xc, axis=-1, keepdims=True)
    return xc * lax.rsqrt(var + LN_EPS)


def _modulate(x, sc, sh):
    return _normalize(x) * (1.0 + sc) + sh


def _make_resid_fns(alpha):
    def resid_ln(x, y, gate, g, b):
        return _normalize(alpha * x + (1.0 + gate) * y) * g + b

    def resid_ln_mod(x, y, gate, g, b, sc, sh):
        xn = resid_ln(x, y, gate, g, b)
        return xn, _modulate(xn, sc, sh)

    return resid_ln, resid_ln_mod


def _s5_act_fn(yc, u, d_skip):
    return jax.nn.gelu(yc + d_skip * u)


def _s5_gate_fn(y1, t):
    return y1 * jax.nn.sigmoid(t)


def _s5_head_specs(yc, proj, u_cb, w_glu, layer, t_rows):
    width = yc.shape[1]
    row = _spec((t_rows, width), lambda i: (i, 0))
    u_spec = _spec((t_rows, width), lambda i: (i, u_cb))
    vec = _spec((1, width), lambda i: (0, 0))
    w_spec = _spec((None,) + w_glu.shape[1:], lambda i: (layer, 0, 0))
    return row, u_spec, vec, w_spec


def _s5_head(name, yc, proj, u_cb, d_skip, b_glu, w_glu, layer, t_rows):
    seq, width = yc.shape
    row, u_spec, vec, w_spec = _s5_head_specs(yc, proj, u_cb, w_glu, layer, t_rows)

    def body(yc_ref, u_ref, d_ref, b_ref, w_ref, o_ref):
        y1 = _s5_act_fn(yc_ref[...], u_ref[...], d_ref[...])
        t = lax.dot_general(y1.astype(BF16), w_ref[...], NN, preferred_element_type=F32) + b_ref[...]
        o_ref[...] = _s5_gate_fn(y1, t).astype(o_ref.dtype)

    return pl.pallas_call(
        body, name=name, grid=(seq // t_rows,), in_specs=[row, u_spec, vec, vec, w_spec], out_specs=row,
        out_shape=jax.ShapeDtypeStruct((seq, width), BF16),
        compiler_params=pltpu.CompilerParams(dimension_semantics=("parallel",),
                                             vmem_limit_bytes=_vmem_limit(40 * t_rows * width)),
    )(_in_hbm(yc), _in_hbm(proj), d_skip, b_glu, w_glu)


def _s5_head_bwd(name, yc, proj, u_cb, d_out, d_skip, b_glu, w_glu, layer, t_rows):
    seq, width = yc.shape
    row, u_spec, vec, w_spec = _s5_head_specs(yc, proj, u_cb, w_glu, layer, t_rows)
    n_t = seq // t_rows

    def body(yc_ref, u_ref, do_ref, d_ref, b_ref, w_ref, dyc_ref, du_ref, dw_ref, dd_ref, db_ref, acc_ref):
        i = pl.program_id(0)

        @pl.when(i == 0)
        def _():
            acc_ref[...] = jnp.zeros_like(acc_ref)
            dd_ref[...] = jnp.zeros_like(dd_ref)
            db_ref[...] = jnp.zeros_like(db_ref)

        y1, pull_act = jax.vjp(_s5_act_fn, yc_ref[...], u_ref[...], d_ref[...])
        y1_b = y1.astype(BF16)
        t = lax.dot_general(y1_b, w_ref[...], NN, preferred_element_type=F32) + b_ref[...]
        _, pull_gate = jax.vjp(_s5_gate_fn, y1, t)
        d_y1, d_t = pull_gate(do_ref[...].astype(F32))
        d_t_b = d_t.astype(BF16)
        d_y1 = d_y1 + lax.dot_general(d_t_b, w_ref[...], NT, preferred_element_type=F32)
        acc_ref[...] += lax.dot_general(y1_b, d_t_b, TN, preferred_element_type=F32)
        db_ref[...] += jnp.sum(d_t, axis=0, keepdims=True)
        d_yc, d_u, d_d = pull_act(d_y1)
        dyc_ref[...] = d_yc
        du_ref[...] = d_u
        dd_ref[...] += d_d

        @pl.when(i == n_t - 1)
        def _():
            dw_ref[...] = acc_ref[...].astype(dw_ref.dtype)

    whole = _spec((width, width), lambda i: (0, 0))
    return pl.pallas_call(
        body, name=name, grid=(n_t,), in_specs=[row, u_spec, row, vec, vec, w_spec],
        out_specs=[row, row, whole, vec, vec],
        out_shape=[jax.ShapeDtypeStruct((seq, width), F32), jax.ShapeDtypeStruct((seq, width), F32),
                   jax.ShapeDtypeStruct((width, width), GRAD_WIRE), jax.ShapeDtypeStruct((1, width), F32),
                   jax.ShapeDtypeStruct((1, width), F32)],
        scratch_shapes=[pltpu.VMEM((width, width), F32)],
        compiler_params=pltpu.CompilerParams(dimension_semantics=("arbitrary",),
                                             vmem_limit_bytes=_vmem_limit(80 * t_rows * width)),
    )(_in_hbm(yc), _in_hbm(proj), _in_hbm(d_out), d_skip, b_glu, w_glu)


def _sb_tri(kind):
    row = lax.broadcasted_iota(jnp.int32, (SB_BLOCK, SB_BLOCK), 0)
    col = lax.broadcasted_iota(jnp.int32, (SB_BLOCK, SB_BLOCK), 1)
    if kind == "after":
        return (row > col).astype(BF16)
    if kind == "from":
        return (row >= col).astype(BF16)
    return col < row


def _split_dot(x, m):
    hi = x.astype(BF16)
    lo = (x - hi.astype(F32)).astype(BF16)
    return (lax.dot_general(hi, m, NN, preferred_element_type=F32)
            + lax.dot_general(lo, m, NN, preferred_element_type=F32))


def _sb_scores(qh, k2):
    z = lax.dot_general(qh, k2, NT, preferred_element_type=F32)
    log_beta = jnp.minimum(z, 0.0) - jnp.log(1.0 + jnp.exp(-jnp.abs(z)))
    return log_beta, log_beta - z


def _sb_attention_fwd(proj, sb_width, beside=None):
    seq = proj.shape[0]
    n_pair, n_q = sb_width // LANES, seq // SB_BLOCK
    scale = 1.0 / (HEAD_DIM ** 0.5)

    def body(q_ref, k_ref, v_ref, o_ref, o32_ref):
        qi = pl.program_id(1)
        q2 = q_ref[...]
        lane = lax.broadcasted_iota(jnp.int32, (SB_BLOCK, LANES), 1)
        m_after, causal = _sb_tri("after"), _sb_tri("mask")
        heads = [lane < HEAD_DIM, lane >= HEAD_DIM]
        qh = [(jnp.where(m, q2, 0.0) * scale).astype(BF16) for m in heads]

        def scores(kb, diag):
            ks = pl.multiple_of(kb * SB_BLOCK, SB_BLOCK)
            k2 = k_ref[pl.ds(ks, SB_BLOCK), :].astype(BF16)
            out = []
            for h in range(2):
                log_beta, log_1m = _sb_scores(qh[h], k2)
                if diag:
                    log_1m = jnp.where(causal, log_1m, 0.0)
                out += [log_beta + _split_dot(log_1m, m_after), jnp.sum(log_1m, axis=1, keepdims=True)]
            return tuple(out)

        def weigh(kb, sc, carry, acc, diag):
            ks = pl.multiple_of(kb * SB_BLOCK, SB_BLOCK)
            v2 = v_ref[pl.ds(ks, SB_BLOCK), :].astype(BF16)
            out = []
            for h in range(2):
                w = jnp.exp(sc[2 * h] + carry[h])
                if diag:
                    w = jnp.where(causal, w, 0.0)
                out.append(acc[h] + lax.dot_general(w.astype(BF16), v2, NN, preferred_element_type=F32))
            return tuple(out)

        zero = jnp.zeros((SB_BLOCK, LANES), F32)
        zcol = jnp.zeros((SB_BLOCK, 1), F32)
        sc = scores(qi, True)
        acc = weigh(qi, sc, (zcol, zcol), (zero, zero), True)
        carry = (sc[1], sc[3])

        def loop(st):
            kb, carry, acc = st
            sc = scores(kb, False)
            after = (carry[0] + sc[1], carry[1] + sc[3])
            done = jnp.maximum(jnp.max(after[0]), jnp.max(after[1])) < SB_UNDERFLOW
            acc = weigh(kb, sc, carry, acc, False)
            return jnp.where(done, -1, kb - 1), after, acc

        _, _, acc = lax.while_loop(lambda st: st[0] >= 0, loop, (qi - 1, carry, acc))
        out = jnp.where(heads[0], acc[0], acc[1])
        o_ref[...] = out.astype(o_ref.dtype)
        o32_ref[...] = out

    q_spec = _spec((SB_BLOCK, LANES), lambda h, i: (i, h))
    kv = [_spec((seq, LANES), lambda h, i, o=o: (0, o + h)) for o in (n_pair, 2 * n_pair)]
    o_spec = _spec((SB_BLOCK, LANES), lambda h, i: (i, h))
    return _call_beside(
        beside, body, "sb_attention_fwd", (n_pair, n_q), [q_spec] + kv, [o_spec, o_spec],
        [jax.ShapeDtypeStruct((seq, sb_width), BF16), jax.ShapeDtypeStruct((seq, sb_width), F32)], [],
        _vmem_limit(2 * seq * LANES * 4), (proj, proj, proj), ("parallel", "arbitrary"))


def _sb_attention_bwd(proj, o32, do, sb_width, beside=None):
    seq = proj.shape[0]
    n_pair, n_q = sb_width // LANES, seq // SB_BLOCK
    scale = 1.0 / (HEAD_DIM ** 0.5)

    def body(q_ref, k_ref, v_ref, o_ref, do_ref, dq_ref, dk_out_ref, dv_out_ref, dk_ref, dv_ref):
        qi = pl.program_id(1)

        @pl.when(qi == 0)
        def _():
            dk_ref[...] = jnp.zeros_like(dk_ref)
            dv_ref[...] = jnp.zeros_like(dv_ref)

        q2 = q_ref[...]
        do2 = do_ref[...].astype(F32)
        o2 = o_ref[...]
        lane = lax.broadcasted_iota(jnp.int32, (SB_BLOCK, LANES), 1)
        m_after, m_from, causal = _sb_tri("after"), _sb_tri("from"), _sb_tri("mask")
        heads = [lane < HEAD_DIM, lane >= HEAD_DIM]
        qh = [(jnp.where(m, q2, 0.0) * scale).astype(BF16) for m in heads]
        doh = [jnp.where(m, do2, 0.0) for m in heads]
        doh_b = [v.astype(BF16) for v in doh]
        total = [jnp.sum(v * o2, axis=1, keepdims=True) for v in doh]

        def scores(kb, diag):
            ks = pl.multiple_of(kb * SB_BLOCK, SB_BLOCK)
            k2 = k_ref[pl.ds(ks, SB_BLOCK), :].astype(BF16)
            v2 = v_ref[pl.ds(ks, SB_BLOCK), :].astype(BF16)
            out = []
            for h in range(2):
                log_beta, log_1m = _sb_scores(qh[h], k2)
                if diag:
                    log_1m = jnp.where(causal, log_1m, 0.0)
                out += [log_beta + _split_dot(log_1m, m_after), jnp.sum(log_1m, axis=1, keepdims=True),
                        lax.dot_general(doh_b[h], v2, NT, preferred_element_type=F32), log_beta]
            return tuple(out)

        def pull(kb, sc, carry, right, dq, diag):
            ks = pl.multiple_of(kb * SB_BLOCK, SB_BLOCK)
            k2 = k_ref[pl.ds(ks, SB_BLOCK), :].astype(BF16)
            dv_blk, dk_blk, right_out, dq_out = None, None, [], []
            for h in range(2):
                arg, _, d_w, log_beta = sc[4 * h:4 * h + 4]
                w = jnp.exp(arg + carry[h])
                if diag:
                    w = jnp.where(causal, w, 0.0)
                w_b = w.astype(BF16)
                d_arg = d_w * w_b.astype(F32)
                dv_h = lax.dot_general(w_b, doh_b[h], TN, preferred_element_type=F32)
                d_log_1m = total[h] - right[h] - _split_dot(d_arg, m_from)
                beta = jnp.exp(log_beta)
                dz = d_arg * (1.0 - beta) - beta * d_log_1m
                if diag:
                    dz = jnp.where(causal, dz, 0.0)
                dz_b = dz.astype(BF16)
                dk_h = lax.dot_general(dz_b, qh[h], TN, preferred_element_type=F32)
                dv_blk = dv_h if h == 0 else dv_blk + dv_h
                dk_blk = dk_h if h == 0 else dk_blk + dk_h
                dq_out.append(dq[h] + lax.dot_general(dz_b, k2, NN, preferred_element_type=F32))
                right_out.append(right[h] + jnp.sum(d_arg, axis=1, keepdims=True))
            dv_ref[pl.ds(ks, SB_BLOCK), :] += dv_blk
            dk_ref[pl.ds(ks, SB_BLOCK), :] += dk_blk
            return tuple(right_out), tuple(dq_out)

        zero = jnp.zeros((SB_BLOCK, LANES), F32)
        zcol = jnp.zeros((SB_BLOCK, 1), F32)
        sc = scores(qi, True)
        right, dq = pull(qi, sc, (zcol, zcol), (zcol, zcol), (zero, zero), True)
        carry = (sc[1], sc[5])

        def loop(st):
            kb, carry, right, dq = st
            sc = scores(kb, False)
            after = (carry[0] + sc[1], carry[1] + sc[5])
            done = jnp.maximum(jnp.max(after[0]), jnp.max(after[1])) < SB_UNDERFLOW
            right, dq = pull(kb, sc, carry, right, dq, False)
            return jnp.where(done, -1, kb - 1), after, right, dq

        _, _, _, dq = lax.while_loop(lambda st: st[0] >= 0, loop, (qi - 1, carry, right, dq))
        dq_ref[...] = (jnp.where(heads[0], dq[0], dq[1]) * scale).astype(dq_ref.dtype)

        @pl.when(qi == n_q - 1)
        def _():
            dk_out_ref[...] = dk_ref[...].astype(dk_out_ref.dtype)
            dv_out_ref[...] = dv_ref[...].astype(dv_out_ref.dtype)

    q_spec = _spec((SB_BLOCK, LANES), lambda h, i: (i, h))
    kv = [_spec((seq, LANES), lambda h, i, o=o: (0, o + h)) for o in (n_pair, 2 * n_pair)]
    full = _spec((seq, LANES), lambda h, i: (0, h))
    return _call_beside(
        beside, body, "sb_attention_bwd", (n_pair, n_q), [q_spec] + kv + [q_spec, q_spec], [q_spec, full, full],
        [jax.ShapeDtypeStruct((seq, sb_width), BF16)] * 3,
        [pltpu.VMEM((seq, LANES), F32), pltpu.VMEM((seq, LANES), F32)],
        _vmem_limit(4 * seq * LANES * 4), (proj, proj, proj, o32, do), ("parallel", "arbitrary"))


def _s5_discretize(a_re, a_im, log_dt, b_re, b_im, c_re, c_im):
    n_g, n_p = a_re.shape
    c_g = b_re.shape[-1]
    ns = n_g // SLAB_GROUPS
    dt = jnp.exp(log_dt)[:, None]
    xr, xi = a_re * dt, a_im * dt
    mag = jnp.exp(xr)
    lr, li = mag * jnp.cos(xi), mag * jnp.sin(xi)
    den = a_re * a_re + a_im * a_im
    fr = ((lr - 1.0) * a_re + li * a_im) / den
    fi = (li * a_re - (lr - 1.0) * a_im) / den
    bb_re = fr[..., None] * b_re - fi[..., None] * b_im
    bb_im = fr[..., None] * b_im + fi[..., None] * b_re
    eye = jnp.eye(SLAB_GROUPS, dtype=F32)

    def diag_b(m):
        m = jnp.transpose(m.reshape(ns, SLAB_GROUPS, n_p, c_g), (0, 1, 3, 2))
        m = m[:, :, :, None, :] * eye[None, :, None, :, None]
        return m.reshape(ns, SLAB_GROUPS * c_g, SLAB_GROUPS * n_p)

    def diag_c(m):
        m = jnp.transpose(m.reshape(ns, SLAB_GROUPS, c_g, n_p), (0, 1, 3, 2))
        m = m[:, :, :, None, :] * eye[None, :, None, :, None]
        return m.reshape(ns, SLAB_GROUPS * n_p, SLAB_GROUPS * c_g)

    bs = jnp.concatenate([diag_b(bb_re), diag_b(bb_im)], axis=-1)
    cs = jnp.concatenate([diag_c(c_re), -diag_c(c_im)], axis=1)
    lam = jnp.concatenate([lr.reshape(ns, 1, -1), li.reshape(ns, 1, -1)], axis=-1)
    return bs, cs, lam


def _s5_powers(a_re, a_im, log_dt, n):
    n_g, n_p = a_re.shape
    ns = n_g // SLAB_GROUPS
    dt = jnp.exp(log_dt)[:, None]
    mag = jnp.exp(a_re * dt)
    base_r, base_i = mag * jnp.cos(a_im * dt), mag * jnp.sin(a_im * dt)
    steps = jnp.arange(1, n + 1, dtype=jnp.int32)[:, None, None]
    pr, pi = jnp.ones((n, n_g, n_p), F32), jnp.zeros((n, n_g, n_p), F32)
    for b in range(n.bit_length()):
        take = ((steps >> b) & 1) == 1
        pr, pi = (jnp.where(take, pr * base_r - pi * base_i, pr), jnp.where(take, pr * base_i + pi * base_r, pi))
        base_r, base_i = base_r * base_r - base_i * base_i, 2.0 * base_r * base_i

    def slabs(re, im):
        one = lambda m: jnp.transpose(m.reshape(n, ns, SLAB_GROUPS * n_p), (1, 0, 2))
        return jnp.concatenate([one(re), one(im)], axis=-1)

    return slabs(pr, pi), slabs(pr[::-1], -pi[::-1])


def _lanes(j):
    return slice(j * LANES, (j + 1) * LANES)


def _tile8(k):
    return pl.ds(pl.multiple_of(k * SUBLANES, SUBLANES), SUBLANES)


def _s5_interleave(dst_ref, src_ref, t_seg):
    def body(k, _):
        dst_ref[_tile8(k), :] = src_ref[pl.ds(k, SUBLANES, stride=t_seg), :]
        return 0

    lax.fori_loop(0, t_seg, body, 0, unroll=4)


def _s5_join_segments(st_ref, end_ref, car_ref, tab_ref, row, order, n_pair):
    for j in range(n_pair):
        re, im = _lanes(j), _lanes(n_pair + j)
        cr, ci = st_ref[:, re], st_ref[:, im]
        tr, ti = tab_ref[row:row + 1, re], tab_ref[row:row + 1, im]
        for s in order:
            car_ref[s:s + 1, re] = cr
            car_ref[s:s + 1, im] = ci
            er, ei = end_ref[s:s + 1, re], end_ref[s:s + 1, im]
            cr, ci = er + tr * cr - ti * ci, ei + tr * ci + ti * cr
        st_ref[:, re] = cr
        st_ref[:, im] = ci


def _s5_add_carries(buf_ref, car_ref, tab_ref, t_seg, n_pair):
    def fix(k, _):
        rows = _tile8(k)
        tab = tab_ref[pl.ds(k, 1), :]
        for j in range(n_pair):
            re, im = _lanes(j), _lanes(n_pair + j)
            cr, ci = car_ref[:, re], car_ref[:, im]
            tr, ti = tab[:, re], tab[:, im]
            buf_ref[rows, re] += tr * cr - ti * ci
            buf_ref[rows, im] += tr * ci + ti * cr
        return 0

    lax.fori_loop(0, t_seg, fix, 0, unroll=2)


def _s5_scan_fwd(proj, u_col, bs, cs, lam, pw, t_blk, beside=None):
    seq = proj.shape[0]
    ns, _, w2 = bs.shape
    n_pair = w2 // (2 * LANES)
    t_seg, n_t = t_blk // SUBLANES, seq // t_blk

    def body(u_ref, bs_ref, cs_ref, lam_ref, pw_ref, yc_ref, h_ref, st_ref, end_ref, car_ref, ui_ref, bu_ref, yi_ref):
        @pl.when(pl.program_id(1) == 0)
        def _():
            st_ref[...] = jnp.zeros_like(st_ref)

        _s5_interleave(ui_ref, u_ref, t_seg)
        bu_ref[...] = lax.dot_general(ui_ref[...].astype(BF16), bs_ref[...], NN, preferred_element_type=F32)
        lam_r = [jnp.broadcast_to(lam_ref[:, _lanes(j)], (SUBLANES, LANES)) for j in range(n_pair)]
        lam_i = [jnp.broadcast_to(lam_ref[:, _lanes(n_pair + j)], (SUBLANES, LANES)) for j in range(n_pair)]

        def step(k, c):
            rows = _tile8(k)
            out = []
            for j in range(n_pair):
                hr, hi = c[2 * j], c[2 * j + 1]
                nr = lam_r[j] * hr - lam_i[j] * hi + bu_ref[rows, _lanes(j)]
                ni = lam_i[j] * hr + lam_r[j] * hi + bu_ref[rows, _lanes(n_pair + j)]
                h_ref[rows, _lanes(j)] = nr
                h_ref[rows, _lanes(n_pair + j)] = ni
                out += [nr, ni]
            return tuple(out)

        ends = lax.fori_loop(0, t_seg, step, (jnp.zeros((SUBLANES, LANES), F32),) * (2 * n_pair), unroll=4)
        for j in range(n_pair):
            end_ref[:, _lanes(j)] = ends[2 * j]
            end_ref[:, _lanes(n_pair + j)] = ends[2 * j + 1]
        _s5_join_segments(st_ref, end_ref, car_ref, pw_ref, t_seg - 1, list(range(SUBLANES)), n_pair)
        _s5_add_carries(h_ref, car_ref, pw_ref, t_seg, n_pair)
        yi_ref[...] = lax.dot_general(h_ref[...].astype(BF16), cs_ref[...], NN, preferred_element_type=F32)

        def scatter(k, _):
            yc_ref[pl.ds(k, SUBLANES, stride=t_seg), :] = yi_ref[_tile8(k), :]
            return 0

        lax.fori_loop(0, t_seg, scatter, 0, unroll=4)

    return _call_beside(
        beside, body, "s5_scan_fwd", (ns, n_t),
        [_spec((t_blk, LANES), lambda s, i: (i, u_col + s)),
         _spec((None, LANES, w2), lambda s, i: (s, 0, 0)),
         _spec((None, w2, LANES), lambda s, i: (s, 0, 0)),
         _spec((None, 1, w2), lambda s, i: (s, 0, 0)),
         _spec((None, t_seg, w2), lambda s, i: (s, 0, 0))],
        [_spec((t_blk, LANES), lambda s, i: (i, s)),
         _spec((None, t_blk, w2), lambda s, i: (s, i, 0))],
        [jax.ShapeDtypeStruct((seq, ns * LANES), F32), jax.ShapeDtypeStruct((ns, seq, w2), F32)],
        [pltpu.VMEM((1, w2), F32), pltpu.VMEM((SUBLANES, w2), F32), pltpu.VMEM((SUBLANES, w2), F32),
         pltpu.VMEM((t_blk, LANES), F32), pltpu.VMEM((t_blk, w2), F32), pltpu.VMEM((t_blk, LANES), F32)],
        _vmem_limit(3 * t_blk * w2 * 4), (proj, bs, cs, lam, pw), ("parallel", "arbitrary"))


def _s5_scan_bwd(proj, u_col, states, d_yc, du_extra, bs, cs, lam, qw, t_blk):
    seq = proj.shape[0]
    ns, _, w2 = bs.shape
    n_pair = w2 // (2 * LANES)
    t_seg, n_t = t_blk // SUBLANES, seq // t_blk

    def body(u_ref, h_ref, hp_ref, dyc_ref, dux_ref, bs_ref, cs_ref, lam_ref, qw_ref,
             du_ref, dbs_ref, dcs_ref, dlam_ref, g_ref, gd_ref, st_ref, end_ref, car_ref, ui_ref, dyi_ref, dui_ref):
        i = pl.program_id(1)

        @pl.when(i == 0)
        def _():
            st_ref[...] = jnp.zeros_like(st_ref)
            dbs_ref[...] = jnp.zeros_like(dbs_ref)
            dcs_ref[...] = jnp.zeros_like(dcs_ref)
            dlam_ref[...] = jnp.zeros_like(dlam_ref)

        _s5_interleave(ui_ref, u_ref, t_seg)
        _s5_interleave(dyi_ref, dyc_ref, t_seg)
        dyc_b = dyi_ref[...].astype(BF16)
        gd_ref[...] = lax.dot_general(dyc_b, cs_ref[...], NT, preferred_element_type=F32)
        lam_r = [jnp.broadcast_to(lam_ref[:, _lanes(j)], (SUBLANES, LANES)) for j in range(n_pair)]
        lam_i = [jnp.broadcast_to(lam_ref[:, _lanes(n_pair + j)], (SUBLANES, LANES)) for j in range(n_pair)]

        def step(kk, c):
            rows = _tile8(t_seg - 1 - kk)
            out = []
            for j in range(n_pair):
                gr_n, gi_n = c[2 * j], c[2 * j + 1]
                gr = gd_ref[rows, _lanes(j)] + lam_r[j] * gr_n + lam_i[j] * gi_n
                gi = gd_ref[rows, _lanes(n_pair + j)] + lam_r[j] * gi_n - lam_i[j] * gr_n
                g_ref[rows, _lanes(j)] = gr
                g_ref[rows, _lanes(n_pair + j)] = gi
                out += [gr, gi]
            return tuple(out)

        zero = jnp.zeros((SUBLANES, LANES), F32)
        firsts = lax.fori_loop(0, t_seg, step, (zero,) * (2 * n_pair), unroll=4)
        for j in range(n_pair):
            end_ref[:, _lanes(j)] = firsts[2 * j]
            end_ref[:, _lanes(n_pair + j)] = firsts[2 * j + 1]
        _s5_join_segments(st_ref, end_ref, car_ref, qw_ref, 0, list(range(SUBLANES))[::-1], n_pair)
        _s5_add_carries(g_ref, car_ref, qw_ref, t_seg, n_pair)

        def pair_up(k, c):
            rows, prev = _tile8(k), _tile8(k - 1)
            out = []
            for j in range(n_pair):
                re, im = _lanes(j), _lanes(n_pair + j)
                gr, gi, hr, hi = g_ref[rows, re], g_ref[rows, im], h_ref[prev, re], h_ref[prev, im]
                out += [c[2 * j] + gr * hr + gi * hi, c[2 * j + 1] + gi * hr - gr * hi]
            return tuple(out)

        acc = lax.fori_loop(1, t_seg, pair_up, (zero,) * (2 * n_pair), unroll=4)
        has_prev = (i < n_t - 1).astype(F32)
        first_seg = lax.broadcasted_iota(jnp.int32, (SUBLANES, LANES), 0) == 0
        last = _tile8(t_seg - 1)
        for j in range(n_pair):
            re, im = _lanes(j), _lanes(n_pair + j)
            gr, gi = g_ref[0:SUBLANES, re], g_ref[0:SUBLANES, im]
            hr = jnp.where(first_seg, hp_ref[SUBLANES - 1:, re] * has_prev, pltpu.roll(h_ref[last, re], 1, 0))
            hi = jnp.where(first_seg, hp_ref[SUBLANES - 1:, im] * has_prev, pltpu.roll(h_ref[last, im], 1, 0))
            dlam_ref[:, re] += jnp.sum(acc[2 * j] + gr * hr + gi * hi, axis=0, keepdims=True)
            dlam_ref[:, im] += jnp.sum(acc[2 * j + 1] + gi * hr - gr * hi, axis=0, keepdims=True)

        g_b = g_ref[...].astype(BF16)
        dui_ref[...] = lax.dot_general(g_b, bs_ref[...], NT, preferred_element_type=F32)
        dbs_ref[...] += lax.dot_general(ui_ref[...].astype(BF16), g_b, TN, preferred_element_type=F32)
        dcs_ref[...] += lax.dot_general(h_ref[...].astype(BF16), dyc_b, TN, preferred_element_type=F32)

        def scatter(k, _):
            rows = pl.ds(k, SUBLANES, stride=t_seg)
            du_ref[rows, :] = (dui_ref[_tile8(k), :] + dux_ref[rows, :]).astype(du_ref.dtype)
            return 0

        lax.fori_loop(0, t_seg, scatter, 0, unroll=4)

    rev = lambda i: n_t - 1 - i
    return pl.pallas_call(
        body, name="s5_scan_bwd", grid=(ns, n_t),
        in_specs=[_spec((t_blk, LANES), lambda s, i: (rev(i), u_col + s)),
                  _spec((None, t_blk, w2), lambda s, i: (s, rev(i), 0)),
                  _spec((None, SUBLANES, w2), lambda s, i: (s, jnp.maximum(rev(i) * t_seg - 1, 0), 0)),
                  _spec((t_blk, LANES), lambda s, i: (rev(i), s)),
                  _spec((t_blk, LANES), lambda s, i: (rev(i), s)),
                  _spec((None, LANES, w2), lambda s, i: (s, 0, 0)),
                  _spec((None, w2, LANES), lambda s, i: (s, 0, 0)),
                  _spec((None, 1, w2), lambda s, i: (s, 0, 0)),
                  _spec((None, t_seg, w2), lambda s, i: (s, 0, 0))],
        out_specs=[_spec((t_blk, LANES), lambda s, i: (rev(i), s)),
                   _spec((None, LANES, w2), lambda s, i: (s, 0, 0)),
                   _spec((None, w2, LANES), lambda s, i: (s, 0, 0)),
                   _spec((None, 1, w2), lambda s, i: (s, 0, 0))],
        out_shape=[jax.ShapeDtypeStruct((seq, ns * LANES), F32), jax.ShapeDtypeStruct(bs.shape, F32),
                   jax.ShapeDtypeStruct(cs.shape, F32), jax.ShapeDtypeStruct(lam.shape, F32)],
        scratch_shapes=[pltpu.VMEM((t_blk, w2), F32), pltpu.VMEM((t_blk, w2), F32), pltpu.VMEM((1, w2), F32),
                        pltpu.VMEM((SUBLANES, w2), F32), pltpu.VMEM((SUBLANES, w2), F32),
                        pltpu.VMEM((t_blk, LANES), F32), pltpu.VMEM((t_blk, LANES), F32), pltpu.VMEM((t_blk, LANES), F32)],
        compiler_params=pltpu.CompilerParams(dimension_semantics=("parallel", "arbitrary"),
                                             vmem_limit_bytes=_vmem_limit(5 * t_blk * w2 * 4)),
    )(*[_in_hbm(a) for a in (proj, states, states, d_yc, du_extra, bs, cs, lam, qw)])


def _loss_head(y, target, t_m):
    seq, d = y.shape

    def body(y_ref, t_ref, loss_ref, dy_ref):
        @pl.when(pl.program_id(0) == 0)
        def _():
            loss_ref[...] = jnp.zeros_like(loss_ref)

        diff = y_ref[...] - t_ref[...]
        dy_ref[...] = diff / d
        loss_ref[...] += 0.5 * jnp.sum(diff * diff) / d

    row = _spec((t_m, d), lambda i: (i, 0))
    return pl.pallas_call(
        body, name="loss_head", grid=(seq // t_m,), in_specs=[row, row],
        out_specs=[_spec((SUBLANES, LANES), lambda i: (0, 0)), row],
        out_shape=[jax.ShapeDtypeStruct((SUBLANES, LANES), F32), jax.ShapeDtypeStruct((seq, d), F32)],
        compiler_params=pltpu.CompilerParams(dimension_semantics=("arbitrary",),
                                             vmem_limit_bytes=_vmem_limit(6 * t_m * d * 4)),
    )(_in_hbm(y), _in_hbm(target))


def _adamw_fn(w, m, v, *partials):
    g = partials[0]
    for p in partials[1:]:
        g = g + p
    m2 = ADAM_B1 * m + (1.0 - ADAM_B1) * g
    v2 = ADAM_B2 * v + (1.0 - ADAM_B2) * (g * g)
    m_hat = m2 / (1.0 - ADAM_B1 ** ADAM_STEP)
    v_hat = v2 / (1.0 - ADAM_B2 ** ADAM_STEP)
    delta = -ADAM_LR * (m_hat / (jnp.sqrt(v_hat) + ADAM_EPS) + ADAM_WD * w)
    return g, delta, m2, v2


def _adamw(name, w, m, v, partials):
    rows, cols = w.shape
    t_r = rows
    for cand in (512, 256, 128, 64, 32, 16, 8):
        if rows % cand == 0 and cand * cols * 4 <= (1 << 20):
            t_r = cand
            break
    n_p = partials.shape[0]
    row = lambda i: (i, 0)
    ins = [(a, (t_r, cols), row) for a in (w, m, v)]
    ins += [(partials, (None, t_r, cols), (lambda i, j=j: (j, i, 0))) for j in range(n_p)]
    outs = [((rows, cols), F32, (t_r, cols), row)] * 4
    return _rowwise(name, _adamw_fn, ins, outs, (rows // t_r,))


SMALL_PARAMS = ("b_ada", "ssm_a_re", "ssm_a_im", "ssm_log_dt", "ssm_b_re", "ssm_b_im", "ssm_c_re", "ssm_c_im",
                "ssm_d", "b_glu", "ln1_g", "ln1_b", "ln2_g", "ln2_b")
WEIGHTS = ("w_ada", "b_ada", "w_in", "w_sb_up", "ssm_a_re", "ssm_a_im", "ssm_log_dt", "ssm_b_re", "ssm_b_im",
           "ssm_c_re", "ssm_c_im", "ssm_d", "w_glu", "b_glu", "w_ssm_up", "w_out", "ln1_g", "ln1_b", "w_ffn_in",
           "w_ffn_out", "ln2_g", "ln2_b")
ARG_NAMES = (("x", "c") + WEIGHTS + ("loss_target",) + tuple("m_" + n for n in WEIGHTS)
             + tuple("v_" + n for n in WEIGHTS))


def _pack(arrs):
    flat = jnp.concatenate([a.reshape(-1) for a in arrs])
    pad = (-flat.shape[0]) % (PACK_ROWS * LANES)
    return jnp.pad(flat, (0, pad)).reshape(-1, LANES)


def _unpack(packed, like):
    lead = packed.shape[:-2]
    flat = packed.reshape(lead + (-1,))
    out, off = [], 0
    for a in like:
        out.append(flat[..., off:off + a.size].reshape(lead + a.shape))
        off += a.size
    return out


def kernel(x, c, w_ada, b_ada, w_in, w_sb_up, ssm_a_re, ssm_a_im, ssm_log_dt, ssm_b_re, ssm_b_im, ssm_c_re,
           ssm_c_im, ssm_d, w_glu, b_glu, w_ssm_up, w_out, ln1_g, ln1_b, w_ffn_in, w_ffn_out, ln2_g, ln2_b,
           loss_target, m_w_ada, m_b_ada, m_w_in, m_w_sb_up, m_ssm_a_re, m_ssm_a_im, m_ssm_log_dt, m_ssm_b_re,
           m_ssm_b_im, m_ssm_c_re, m_ssm_c_im, m_ssm_d, m_w_glu, m_b_glu, m_w_ssm_up, m_w_out, m_ln1_g, m_ln1_b,
           m_w_ffn_in, m_w_ffn_out, m_ln2_g, m_ln2_b, v_w_ada, v_b_ada, v_w_in, v_w_sb_up, v_ssm_a_re, v_ssm_a_im,
           v_ssm_log_dt, v_ssm_b_re, v_ssm_b_im, v_ssm_c_re, v_ssm_c_im, v_ssm_d, v_w_glu, v_b_glu, v_w_ssm_up,
           v_w_out, v_ln1_g, v_ln1_b, v_w_ffn_in, v_w_ffn_out, v_ln2_g, v_ln2_b):
    given = locals()
    return _train_step({n: given[n] for n in ARG_NAMES})


def _train_step(p):
    x0 = p["x"][0]
    target = p["loss_target"][0]
    seq, d = x0.shape
    depth = p["w_ada"].shape[0]
    n_ada = p["w_ada"].shape[2]
    n_in = p["w_in"].shape[2]
    sb_w = p["w_sb_up"].shape[1]
    ssm_w = p["w_ssm_up"].shape[1]
    n_up = p["w_sb_up"].shape[2]
    n_ffn = p["w_ffn_in"].shape[2]
    ffn = N_DEV * p["w_ffn_out"].shape[1]
    in_cols = N_DEV * n_in
    alpha = (2 * depth) ** 0.25
    resid_ln, resid_ln_mod = _make_resid_fns(alpha)
    t_r = min(512, seq)
    n_r = seq // t_r
    t_m = min(1024, seq)
    n_m = seq // t_m
    t_d = _tile(d)
    assert n_ffn * (N_DEV // 2) == ffn and sb_w % LANES == 0 and ssm_w % LANES == 0 and d % LANES == 0
    assert n_in % LANES == 0 and n_up % LANES == 0 and seq % t_m == 0 and in_cols == 3 * sb_w + ssm_w + 2 * d
    assert (3 * sb_w) % ssm_w == 0 and (3 * sb_w + ssm_w) % (2 * d) == 0
    assert sb_w % n_in == 0 and ssm_w % n_in == 0 and d % n_in == 0
    proj_starts = [c // n_in for c in (0, sb_w, 2 * sb_w, 3 * sb_w, 3 * sb_w + ssm_w)]

    bf = lambda a: a.astype(BF16)
    got = _exchange("gather_first", [], [bf(p["w_in"][0]), p["c"]])
    wg_in = [got[0]] + [None] * (depth - 1)
    c_all = got[1].reshape(N_DEV, d)
    small_names = ("w_sb_up", "w_ssm_up", "w_glu", "w_out")
    wg_ffn_in, wg_ffn_out, wg = [None] * depth, [None] * depth, {}

    c_pad = jnp.pad(c_all, ((0, 2 * SUBLANES - N_DEV), (0, 0)))
    c_act = _rowwise("silu_c", lambda v: v * jax.nn.sigmoid(v), [(c_pad, c_pad.shape, lambda i: (0, 0))],
                     [(c_pad.shape, F32, c_pad.shape, lambda i: (0, 0))], (1,))[0]
    rows_c = c_pad.shape[0]
    mod_cols = [
        _mm(f"mod_{l}", c_act, p["w_ada"],
            _spec((rows_c, d), lambda i, j, k: (0, 0)), _spec((None, d, n_ada), lambda i, j, k, l=l: (l, 0, 0)),
            _spec((rows_c, n_ada), lambda i, j, k: (0, 0)), (rows_c, n_ada), F32, (1, 1, 1), NN)
        for l in range(depth)]
    mod_send = jnp.stack([m[:N_DEV] for m in mod_cols], axis=1)
    mod_recv = _exchange("exchange_mod", [mod_send], [])[0]
    mod_nobias = jnp.swapaxes(mod_recv, 0, 1).reshape(depth, N_DEV * n_ada)
    full2 = lambda a: (a, a.shape, lambda i: (0, 0))
    mod = _rowwise("mod_bias", lambda a, b: a + b, [full2(mod_nobias), full2(p["b_ada"])],
                   [(mod_nobias.shape, F32, mod_nobias.shape, lambda i: (0, 0))], (1,))[0]
    vec = lambda a: a.reshape(1, -1)
    mods = [[vec(mod[l, j * d:(j + 1) * d]) for j in range(6)] for l in range(depth)]
    ln = {n: [vec(p[n][l]) for l in range(depth)] for n in ("ln1_g", "ln1_b", "ln2_g", "ln2_b")}

    row_spec = lambda width: ((t_r, width), lambda i: (i, 0))
    col_spec = lambda width, cb: ((t_r, width), lambda i, cb=cb: (i, cb))
    vec_spec = lambda width: ((1, width), lambda i: (0, 0))
    rows_in = lambda a: (a,) + row_spec(a.shape[1])
    vec_in = lambda a: (a,) + vec_spec(a.shape[1])
    row_out = lambda width, dt: ((seq, width), dt) + row_spec(width)

    s5 = [_s5_discretize(*[p[n][l] for n in ("ssm_a_re", "ssm_a_im", "ssm_log_dt", "ssm_b_re", "ssm_b_im",
                                               "ssm_c_re", "ssm_c_im")]) for l in range(depth)]
    s5_b16 = [(bs.astype(BF16), cs.astype(BF16), lam) for bs, cs, lam in s5]
    t_scan = min(1024, seq)
    s5_pw = [_s5_powers(p["ssm_a_re"][l], p["ssm_a_im"][l], p["ssm_log_dt"][l], t_scan // SUBLANES)
             for l in range(depth)]
    u_col = 3 * sb_w // LANES
    gates_cb = (3 * sb_w + ssm_w) // (2 * d)
    ssm_d = [vec(p["ssm_d"][l]) for l in range(depth)]
    b_glu = [vec(p["b_glu"][l]) for l in range(depth)]
    n_half = N_DEV // 2

    h = _rowwise("modulate_in", _modulate, [rows_in(x0), vec_in(mods[0][1]), vec_in(mods[0][0])],
                 [row_out(d, BF16)], (n_r,))[0]
    saved = []
    x_cur = x0
    for l in range(depth):
        sv = {"x_in": x_cur, "h": h}
        last = l == depth - 1
        t_n = _tile(n_in)
        r_n = n_in // t_n
        proj = _mm(f"proj_{l}", h, wg_in[l],
                   _spec((t_m, d), lambda i, j, k: (i, 0)),
                   _spec((None, d, t_n), lambda i, j, k, r=r_n: (j // r, 0, j % r)),
                   _spec((t_m, t_n), lambda i, j, k: (i, j)), (seq, in_cols), F32, (n_m, N_DEV * r_n, 1), NN,
                   reread=(True, True))
        arriving = [bf(p["w_ffn_in"][l]), bf(p["w_ffn_out"][l])] + ([bf(p[n]) for n in small_names] if l == 0 else [])
        (o_sb, o_sb32), got = _sb_attention_fwd(proj, sb_w, beside=_Exchange(gather=arriving))
        wg_ffn_in[l] = got[0]
        wg_ffn_out[l] = got[1].reshape(n_half, n_ffn, d)
        if l == 0:
            wg = dict(zip(small_names, got[2:]))
            for n in ("w_glu", "w_out"):
                wg[n] = jnp.swapaxes(wg[n], 0, 1).reshape(depth, -1, wg[n].shape[-1])
            for n in ("w_sb_up", "w_ssm_up"):
                wg[n] = jnp.transpose(wg[n], (1, 2, 0, 3)).reshape(depth, wg[n].shape[2], d)
        bs16, cs16, lam = s5_b16[l]
        (yc, states), got = _s5_scan_fwd(proj, u_col, bs16, cs16, lam, s5_pw[l][0], t_scan,
                                         beside=None if last else _Exchange(gather=[bf(p["w_in"][l + 1])]))
        if not last:
            wg_in[l + 1] = got[0]
        s5_out = _s5_head(f"s5_head_{l}", yc, proj, 3 * sb_w // ssm_w, ssm_d[l], b_glu[l], wg["w_glu"], l, t_r)

        merged, y_sb, y_ssm = _up_merge(f"up_merge_{l}", o_sb, s5_out, proj, gates_cb, wg["w_sb_up"], wg["w_ssm_up"],
                                        l, t_r)
        y_mix = _mm(f"out_proj_{l}", merged, wg["w_out"],
                    _spec((t_m, d), lambda i, j, k: (i, 0)), _spec((None, d, t_d), lambda i, j, k, l=l: (l, 0, j)),
                    _spec((t_m, t_d), lambda i, j, k: (i, j)), (seq, d), F32, (n_m, d // t_d, 1), NN)
        vecs_a = [mods[l][2], ln["ln1_g"][l], ln["ln1_b"][l], mods[l][4], mods[l][3]]
        x_mid, h2 = _rowwise(f"resid_mix_{l}", resid_ln_mod, [rows_in(x_cur), rows_in(y_mix)] + [vec_in(v) for v in vecs_a],
                             [row_out(d, F32), row_out(d, BF16)], (n_r,))
        a_ffn, f_act = _ffn_in_swiglu(f"ffn_in_{l}", h2, wg_ffn_in[l], t_r)
        y_ffn = _mm(f"ffn_out_{l}", f_act, wg_ffn_out[l],
                    _spec((None, t_m, n_ffn), lambda i, j, k: (k, i, 0)),
                    _spec((None, n_ffn, t_d), lambda i, j, k: (k, 0, j)),
                    _spec((t_m, t_d), lambda i, j, k: (i, j)), (seq, d), F32, (n_m, d // t_d, n_half), NN)
        vecs_b = [mods[l][5], ln["ln2_g"][l], ln["ln2_b"][l]] + ([] if last else [mods[l + 1][1], mods[l + 1][0]])
        outs_b = [row_out(d, F32)] + ([] if last else [row_out(d, BF16)])
        res = _rowwise(f"resid_ffn_{l}", resid_ln if last else resid_ln_mod,
                       [rows_in(x_mid), rows_in(y_ffn)] + [vec_in(v) for v in vecs_b], outs_b, (n_r,))
        sv.update(proj=proj, o_sb=o_sb, o_sb32=o_sb32, yc=yc, states=states, s5_out=s5_out,
                  y_sb=y_sb, y_ssm=y_ssm, merged=merged, y_mix=y_mix, x_mid=x_mid, h2=h2, a_ffn=a_ffn, f_act=f_act,
                  y_ffn=y_ffn, vecs_a=vecs_a, vecs_b=vecs_b)
        saved.append(sv)
        x_cur = res[0]
        h = None if last else res[1]

    loss_part, d_x = _loss_head(x_cur, target, t_r)
    loss = lax.psum(loss_part[0, 0], MESH_AXES)

    d_h_next = None
    grads = {n: [None] * depth for n in WEIGHTS}
    d_mod = [[None] * 6 for _ in range(depth)]
    land = {}
    waiting = []
    row_wrt = lambda i, width, dt: (i, "row", (seq, width), dt) + row_spec(width)
    sum_wrt = lambda i, width: (i, "sum", (1, width), F32) + vec_spec(width)
    for l in reversed(range(depth)):
        sv = saved[l]
        last = l == depth - 1
        ins_b = [rows_in(sv["x_mid"]), rows_in(sv["y_ffn"])] + [vec_in(v) for v in sv["vecs_b"]]
        cts_b = [rows_in(d_x)] + ([] if last else [rows_in(d_h_next)])
        wrt_b = [row_wrt(0, d, F32), row_wrt(1, d, BF16)] + [sum_wrt(2 + j, d) for j in range(len(sv["vecs_b"]))]
        res = _rowwise_vjp(f"resid_ffn_bwd_{l}", resid_ln if last else resid_ln_mod, ins_b, cts_b, wrt_b, (n_r,))
        d_x_mid, d_y_ffn = res[0], res[1]
        d_mod[l][5], grads["ln2_g"][l], grads["ln2_b"][l] = res[2], res[3], res[4]
        if not last:
            d_mod[l + 1][1], d_mod[l + 1][0] = res[5], res[6]
        d_a = _ffn_out_dx_swiglu(f"ffn_out_dx_{l}", d_y_ffn, wg_ffn_out[l], sv["a_ffn"], t_r).reshape(N_DEV, seq, n_ffn)
        g_ffn_out = _mm(f"ffn_out_dw_{l}", sv["f_act"], d_y_ffn,
                        _spec((None, t_m, n_ffn), lambda i, j, k: (i, k, 0)), _spec((t_m, t_d), lambda i, j, k: (k, j)),
                        _spec((None, n_ffn, t_d), lambda i, j, k: (i, 0, j)), (n_half, n_ffn, d), GRAD_WIRE,
                        (n_half, d // t_d, n_m), TN, reread=(False, True))
        d_h2 = _mm(f"ffn_in_dx_{l}", d_a, wg_ffn_in[l],
                   _spec((None, t_m, n_ffn), lambda i, j, k: (k, i, 0)),
                   _spec((None, t_d, n_ffn), lambda i, j, k: (k, j, 0)),
                   _spec((t_m, t_d), lambda i, j, k: (i, j)), (seq, d), BRANCH_CT, (n_m, d // t_d, N_DEV), NT)
        g_ffn_in = _mm(f"ffn_in_dw_{l}", sv["h2"], d_a,
                       _spec((t_m, t_d), lambda i, j, k: (k, j)), _spec((None, t_m, n_ffn), lambda i, j, k: (i, k, 0)),
                       _spec((None, t_d, n_ffn), lambda i, j, k: (i, j, 0)), (N_DEV, d, n_ffn), GRAD_WIRE,
                       (N_DEV, d // t_d, n_m), TN, reread=(True, False))
        ins_a = [rows_in(sv["x_in"]), rows_in(sv["y_mix"])] + [vec_in(v) for v in sv["vecs_a"]]
        wrt_a = [row_wrt(0, d, F32), row_wrt(1, d, BF16)] + [sum_wrt(2 + j, d) for j in range(5)]
        res = _rowwise_vjp(f"resid_mix_bwd_{l}", resid_ln_mod, ins_a, [rows_in(d_x_mid), rows_in(d_h2)], wrt_a, (n_r,))
        d_x_in, d_y_mix = res[0], res[1]
        d_mod[l][2], grads["ln1_g"][l], grads["ln1_b"][l], d_mod[l][4], d_mod[l][3] = res[2:7]
        d_merged = _mm(f"out_proj_dx_{l}", d_y_mix, wg["w_out"],
                       _spec((t_m, d), lambda i, j, k: (i, 0)), _spec((None, t_d, d), lambda i, j, k, l=l: (l, j, 0)),
                       _spec((t_m, t_d), lambda i, j, k: (i, j)), (seq, d), BRANCH_CT, (n_m, d // t_d, 1), NT)
        g_out = _mm(f"out_proj_dw_{l}", sv["merged"], d_y_mix,
                    _spec((t_m, t_d), lambda i, j, k: (k, i)), _spec((t_m, t_d), lambda i, j, k: (k, j)),
                    _spec((t_d, t_d), lambda i, j, k: (i, j)), (d, d), GRAD_WIRE, (d // t_d, d // t_d, n_m), TN, reread=(d > t_d, d > t_d))
        gates = (sv["proj"],) + col_spec(2 * d, gates_cb)
        d_y_sb, d_y_ssm, d_gates = _rowwise_vjp(
            f"merge_bwd_{l}", _merge_fn, [rows_in(sv["y_sb"]), rows_in(sv["y_ssm"]), gates], [rows_in(d_merged)],
            [row_wrt(0, d, BF16), row_wrt(1, d, BF16), row_wrt(2, 2 * d, BF16)], (n_r,))

        def up_bwd(name, act, d_y, w, dx_dtype, l=l):
            k_w = act.shape[1]
            dx = _mm(name + "_dx", d_y, w, _spec((t_m, d), lambda i, j, k: (i, 0)),
                     _spec((None, k_w, d), lambda i, j, k: (l, 0, 0)),
                     _spec((t_m, k_w), lambda i, j, k: (i, 0)), (seq, k_w), dx_dtype, (n_m, 1, 1), NT)
            dw = _mm(name + "_dw", act, d_y, _spec((t_m, k_w), lambda i, j, k: (k, 0)),
                     _spec((t_m, t_d), lambda i, j, k: (k, j)),
                     _spec((k_w, t_d), lambda i, j, k: (0, j)), (k_w, d), GRAD_WIRE, (1, d // t_d, n_m), TN,
                     reread=(d > t_d, False))
            return dx, jnp.swapaxes(dw.reshape(k_w, N_DEV, n_up), 0, 1)

        d_o_sb, g_sb_up = up_bwd(f"sb_up_{l}", sv["o_sb"], d_y_sb, wg["w_sb_up"], BF16)
        d_s5_out, g_ssm_up = up_bwd(f"ssm_up_{l}", sv["s5_out"], d_y_ssm, wg["w_ssm_up"], BRANCH_CT)
        waiting += [("w_ffn_in", g_ffn_in), ("w_ffn_out", g_ffn_out.reshape(N_DEV, -1, d)),
                    ("w_out", g_out.reshape(N_DEV, -1, d)), ("w_sb_up", g_sb_up), ("w_ssm_up", g_ssm_up)]
        levels = [l + 1] * (len(waiting) - 5) + [l] * 5
        (d_q, d_k, d_v), got = _sb_attention_bwd(
            sv["proj"], sv["o_sb32"], d_o_sb, sb_w,
            beside=_Exchange(layered=[(g, lv, depth, land.get(n)) for (n, g), lv in zip(waiting, levels)]))
        land.update({n: buf for (n, _), buf in zip(waiting, got)})
        d_yc, d_u_skip, g_glu, grads["ssm_d"][l], grads["b_glu"][l] = _s5_head_bwd(
            f"s5_head_bwd_{l}", sv["yc"], sv["proj"], 3 * sb_w // ssm_w, d_s5_out, ssm_d[l], b_glu[l], wg["w_glu"], l, t_r)
        bs16, cs16, lam = s5_b16[l]
        d_u, d_bs, d_cs, d_lam = _s5_scan_bwd(sv["proj"], u_col, sv["states"], d_yc, d_u_skip, bs16, cs16, lam,
                                              s5_pw[l][1], t_scan)
        raw = [p[n][l] for n in ("ssm_a_re", "ssm_a_im", "ssm_log_dt", "ssm_b_re", "ssm_b_im", "ssm_c_re", "ssm_c_im")]
        _, pull = jax.vjp(_s5_discretize, *raw)
        (grads["ssm_a_re"][l], grads["ssm_a_im"][l], grads["ssm_log_dt"][l], grads["ssm_b_re"][l],
         grads["ssm_b_im"][l], grads["ssm_c_re"][l], grads["ssm_c_im"][l]) = pull((d_bs, d_cs, d_lam))
        d_proj = [d_q, d_k, d_v, d_u, d_gates]
        g_in = _mm_pieces(f"proj_dw_{l}", d_proj, proj_starts, n_in, lambda i, j, k: i, sv["h"],
                          _spec((t_m, t_d), lambda i, j, k: (k, j)), False, lambda i, j, k: k,
                          _spec((None, t_d, n_in), lambda i, j, k: (i, j, 0)), (N_DEV, d, n_in), GRAD_WIRE,
                          (N_DEV, d // t_d, n_m), TN)
        waiting = [("w_in", g_in), ("w_glu", g_glu.reshape(N_DEV, -1, ssm_w))]
        closing = _Exchange(layered=[(g, 0, depth, land.get(n)) for n, g in waiting]) if l == 0 else None
        d_h = _mm_pieces(f"proj_dx_{l}", d_proj, proj_starts, n_in, lambda i, j, k: k, wg_in[l],
                         _spec((None, t_d, n_in), lambda i, j, k: (k, j, 0)), True, lambda i, j, k: i,
                         _spec((t_m, t_d), lambda i, j, k: (i, j)), (seq, d), BRANCH_CT, (n_m, d // t_d, N_DEV), NT,
                         beside=closing)
        if l == 0:
            d_h, got = d_h
            land.update({n: buf for (n, _), buf in zip(waiting, got)})
        d_x, d_h_next = d_x_in, d_h
    res = _rowwise_vjp("modulate_in_bwd", lambda v, sc, sh: (v, _modulate(v, sc, sh)),
                       [rows_in(x0), vec_in(mods[0][1]), vec_in(mods[0][0])], [rows_in(d_x), rows_in(d_h_next)],
                       [row_wrt(0, d, F32), sum_wrt(1, d), sum_wrt(2, d)], (n_r,))
    grad_x, d_mod[0][1], d_mod[0][0] = res

    d_mod_rows = jnp.concatenate([jnp.concatenate(d_mod[l], axis=1) for l in range(depth)], axis=0)
    grads["b_ada"] = [d_mod_rows[l] for l in range(depth)]
    small_local = [jnp.stack([g.reshape(p[n].shape[1:]) for g in grads[n]]) for n in SMALL_PARAMS]
    d_mod_send = jnp.swapaxes(d_mod_rows.reshape(depth, N_DEV, n_ada), 0, 1)
    small_sum, (d_mod_cols,) = _reduce_packed("exchange_last", _pack(small_local), [d_mod_send])
    d_mod_pad = jnp.pad(jnp.swapaxes(d_mod_cols, 0, 1), ((0, 0), (0, rows_c - N_DEV), (0, 0)))
    g_ada = [
        _mm(f"mod_dw_{l}", c_act, d_mod_pad,
            _spec((rows_c, d), lambda i, j, k: (0, 0)), _spec((None, rows_c, n_ada), lambda i, j, k, l=l: (l, 0, 0)),
            _spec((d, n_ada), lambda i, j, k: (0, 0)), (d, n_ada), F32, (1, 1, 1), TN)
        for l in range(depth)]

    out = {}

    def update(name, partials):
        shape = p[name].shape
        two_d = lambda a: a.reshape(-1, shape[-1])
        res = _adamw("adamw_" + name, two_d(p[name]), two_d(p["m_" + name]), two_d(p["v_" + name]),
                     partials.reshape(partials.shape[0], -1, shape[-1]))
        out[name] = [r.reshape(shape) for r in res]

    update("w_ada", jnp.stack(g_ada)[None])
    for n in ("w_in", "w_sb_up", "w_ssm_up", "w_ffn_in", "w_glu", "w_out", "w_ffn_out"):
        update(n, land[n])
    small_w = [p[n] for n in SMALL_PARAMS]
    res = _adamw("adamw_small", _pack(small_w), _pack([p["m_" + n] for n in SMALL_PARAMS]),
                 _pack([p["v_" + n] for n in SMALL_PARAMS]), small_sum[None])
    for kind, packed in enumerate(res):
        for n, a in zip(SMALL_PARAMS, _unpack(packed, small_w)):
            out.setdefault(n, [None] * 4)[kind] = a

    return ((loss, grad_x[None]) + tuple(out[n][0] for n in WEIGHTS) + tuple(out[n][1] for n in WEIGHTS)
            + tuple(out[n][2] for n in WEIGHTS) + tuple(out[n][3] for n in WEIGHTS))
```

```python
import jax
import jax.numpy as jnp
from jax import lax
from jax.experimental import pallas as pl
from jax.experimental.pallas import tpu as pltpu

F32 = jnp.float32
BF16 = jnp.bfloat16
GRAD_WIRE = BF16
FFN_ACT = BF16
BRANCH_CT = BF16

N_DEV = 8
LANES = 128
SUBLANES = 8
VMEM_BYTES = 64 * 1024 * 1024
HEAD_DIM = 64
SB_BLOCK = 256
SB_GROUP = 2
SLAB_GROUPS = 8
LN_EPS = 1e-5
ADAM_LR, ADAM_B1, ADAM_B2, ADAM_EPS, ADAM_WD, ADAM_STEP = 0.001, 0.9, 0.999, 1e-08, 0.01, 10
SB_UNDERFLOW = -120.0

PACK_ROWS = 256
MESH_AXES = ("x", "y", "c")


def _vmem_limit(block_bytes):
    return int(min(max(3 * block_bytes + (8 << 20), 24 << 20), VMEM_BYTES - (8 << 20)))


def _nbytes(shape, dtype):
    n = 1
    for d in shape:
        if d is not None:
            n *= d
    return n * jnp.dtype(dtype).itemsize


def _spec(shape, fn):
    return pl.BlockSpec(shape, fn)


class _Exchange:
    def __init__(self, scatter=(), gather=(), layered=()):
        self.arrs = list(scatter) + [a for a, _, _, _ in layered] + list(gather)
        self.n = len(self.arrs)
        self.n_sc = len(scatter) + len(layered)
        self.layer = [None] * len(scatter) + [l for _, l, _, _ in layered] + [None] * len(gather)
        self.shapes = ([a.shape for a in scatter] + [(N_DEV, dp) + a.shape[1:] for a, _, dp, _ in layered]
                       + [(N_DEV,) + a.shape for a in gather])
        self.held = [(len(scatter) + i, b) for i, (_, _, _, b) in enumerate(layered) if b is not None]
        self.operands = self.arrs + [b for _, b in self.held]
        hbm = pl.BlockSpec(memory_space=pltpu.HBM)
        self.in_specs = [hbm] * len(self.operands)
        self.out_specs = [hbm] * self.n
        self.out_shape = [jax.ShapeDtypeStruct(s, a.dtype) for s, a in zip(self.shapes, self.arrs)]
        self.scratch = [pltpu.SemaphoreType.DMA((self.n, N_DEV - 1)), pltpu.SemaphoreType.DMA((self.n, N_DEV - 1)),
                        pltpu.SemaphoreType.DMA((self.n,))]

    def aliases(self, first_in, first_out):
        return {first_in + self.n + i: first_out + a for i, (a, _) in enumerate(self.held)}

    def copies(self, ins, outs, sems):
        send_sems, recv_sems, own_sems = sems
        x, y, c = lax.axis_index("x"), lax.axis_index("y"), lax.axis_index("c")
        me = 4 * x + 2 * y + c
        landing = [outs[a].at[me] if self.layer[a] is None else outs[a].at[me, self.layer[a]] for a in range(self.n)]
        out = [pltpu.make_async_copy(ins[a].at[me] if a < self.n_sc else ins[a], landing[a], own_sems.at[a])
               for a in range(self.n)]
        for k in range(1, N_DEV):
            px = 1 - x if k & 4 else x
            py = 1 - y if k & 2 else y
            pc = 1 - c if k & 1 else c
            peer = 4 * px + 2 * py + pc
            for a in range(self.n):
                out.append(pltpu.make_async_remote_copy(
                    src_ref=ins[a].at[peer] if a < self.n_sc else ins[a], dst_ref=landing[a],
                    send_sem=send_sems.at[a, k - 1], recv_sem=recv_sems.at[a, k - 1],
                    device_id=(px, py, pc), device_id_type=pl.DeviceIdType.MESH))
        return out


def _exchange(name, scatter, gather, layered=()):
    ex = _Exchange(scatter, gather, layered)

    def body(*refs):
        copies = ex.copies(refs[:ex.n], refs[len(ex.operands):len(ex.operands) + ex.n], refs[-3:])
        for cp in copies:
            cp.start()
        for cp in copies:
            cp.wait()

    return pl.pallas_call(body, name=name, in_specs=ex.in_specs, out_specs=ex.out_specs, out_shape=ex.out_shape,
                          input_output_aliases=ex.aliases(0, 0), scratch_shapes=ex.scratch)(*ex.operands)


def _reduce_packed(name, packed, scatter):
    rows = packed.shape[0]
    blk = rows // N_DEV
    ex = _Exchange(scatter=[packed.reshape(N_DEV, blk, LANES)] + list(scatter))
    n_in = len(ex.operands)

    def body(*refs):
        ins, outs = refs[:ex.n], refs[n_in:n_in + ex.n]
        total_ref = refs[n_in + ex.n]
        sems, (send2, recv2, own2, load_sem) = refs[n_in + ex.n + 1:n_in + ex.n + 4], refs[n_in + ex.n + 4:-2]
        land_v, sum_v = refs[-2:]
        copies = ex.copies(ins, outs, sems)
        for cp in copies:
            cp.start()
        for cp in copies:
            cp.wait()
        load = pltpu.make_async_copy(outs[0], land_v, load_sem)
        load.start()
        load.wait()
        acc = land_v[0]
        for i in range(1, N_DEV):
            acc = acc + land_v[i]
        sum_v[...] = acc
        x, y, c = lax.axis_index("x"), lax.axis_index("y"), lax.axis_index("c")
        me = 4 * x + 2 * y + c
        back = [pltpu.make_async_copy(sum_v, total_ref.at[me], own2)]
        for k in range(1, N_DEV):
            peer = (1 - x if k & 4 else x, 1 - y if k & 2 else y, 1 - c if k & 1 else c)
            back.append(pltpu.make_async_remote_copy(
                src_ref=sum_v, dst_ref=total_ref.at[me], send_sem=send2.at[k - 1], recv_sem=recv2.at[k - 1],
                device_id=peer, device_id_type=pl.DeviceIdType.MESH))
        for cp in back:
            cp.start()
        for cp in back:
            cp.wait()

    hbm = pl.BlockSpec(memory_space=pltpu.HBM)
    res = pl.pallas_call(
        body, name=name, in_specs=ex.in_specs, out_specs=ex.out_specs + [hbm],
        out_shape=ex.out_shape + [jax.ShapeDtypeStruct((N_DEV, blk, LANES), F32)],
        scratch_shapes=ex.scratch + [pltpu.SemaphoreType.DMA((N_DEV - 1,)), pltpu.SemaphoreType.DMA((N_DEV - 1,)),
                                     pltpu.SemaphoreType.DMA, pltpu.SemaphoreType.DMA,
                                     pltpu.VMEM((N_DEV, blk, LANES), F32), pltpu.VMEM((blk, LANES), F32)],
    )(*ex.operands)
    return res[-1].reshape(rows, LANES), res[1:-1]


def _call_beside(ex, body, name, grid, in_specs, out_specs, out_shape, scratch_shapes, vmem_bytes, operands,
                 semantics, in_hbm=True):
    if in_hbm:
        operands = [_in_hbm(a) for a in operands]
    if ex is None:
        res = pl.pallas_call(
            body, name=name, grid=grid, in_specs=in_specs, out_specs=out_specs, out_shape=out_shape,
            scratch_shapes=scratch_shapes,
            compiler_params=pltpu.CompilerParams(dimension_semantics=semantics, vmem_limit_bytes=vmem_bytes),
        )(*operands)
        return res, None
    n_in, n_out, n_scr = len(in_specs), len(out_specs), len(scratch_shapes)
    n_xin = len(ex.operands)

    def fused(*refs):
        mine = refs[:n_in] + refs[n_in + n_xin:n_in + n_xin + n_out]
        mine += refs[n_in + n_xin + n_out + ex.n:n_in + n_xin + n_out + ex.n + n_scr]
        first = pl.program_id(0) == 0
        last = pl.program_id(0) == grid[0] - 1
        for dim in range(1, len(grid)):
            first = jnp.logical_and(first, pl.program_id(dim) == 0)
            last = jnp.logical_and(last, pl.program_id(dim) == grid[dim] - 1)
        x_ins = refs[n_in:n_in + ex.n]
        x_outs = refs[n_in + n_xin + n_out:n_in + n_xin + n_out + ex.n]

        @pl.when(first)
        def _():
            for cp in ex.copies(x_ins, x_outs, refs[-3:]):
                cp.start()

        body(*mine)

        @pl.when(last)
        def _():
            for cp in ex.copies(x_ins, x_outs, refs[-3:]):
                cp.wait()

    res = pl.pallas_call(
        fused, name=name, grid=grid, in_specs=list(in_specs) + ex.in_specs, out_specs=list(out_specs) + ex.out_specs,
        out_shape=list(out_shape) + ex.out_shape, input_output_aliases=ex.aliases(n_in, n_out),
        scratch_shapes=list(scratch_shapes) + ex.scratch,
        compiler_params=pltpu.CompilerParams(dimension_semantics=("arbitrary",) * len(grid),
                                             vmem_limit_bytes=vmem_bytes),
    )(*operands, *ex.operands)
    return res[:n_out], res[n_out:]


NN = (((1,), (0,)), ((), ()))
NT = (((1,), (1,)), ((), ()))
TN = (((0,), (0,)), ((), ()))


def _in_hbm(a):
    return pltpu.with_memory_space_constraint(a, pltpu.HBM)


def _mm(name, a, b, a_spec, b_spec, o_spec, o_shape, o_dtype, grid, dims, beside=None, reread=(False, True)):
    nk = grid[2]
    a, b = (x if again else _in_hbm(x) for x, again in zip((a, b), reread))
    acc_shape = tuple(d for d in o_spec.block_shape if d is not None)

    def product(a_ref, b_ref):
        return lax.dot_general(a_ref[...].astype(BF16), b_ref[...].astype(BF16), dims, preferred_element_type=F32)

    def body_once(a_ref, b_ref, o_ref):
        o_ref[...] = product(a_ref, b_ref).astype(o_ref.dtype)

    def body(a_ref, b_ref, o_ref, acc_ref):
        k = pl.program_id(2)

        @pl.when(k == 0)
        def _():
            acc_ref[...] = product(a_ref, b_ref)

        @pl.when(k > 0)
        def _():
            acc_ref[...] += product(a_ref, b_ref)

        @pl.when(k == nk - 1)
        def _():
            o_ref[...] = acc_ref[...].astype(o_ref.dtype)

    blk = (_nbytes(a_spec.block_shape, a.dtype) + _nbytes(b_spec.block_shape, b.dtype)
           + _nbytes(acc_shape, o_dtype) + _nbytes(acc_shape, F32))
    res, got = _call_beside(
        beside, body_once if nk == 1 else body, name, grid, [a_spec, b_spec], [o_spec],
        [jax.ShapeDtypeStruct(o_shape, o_dtype)], [] if nk == 1 else [pltpu.VMEM(acc_shape, F32)],
        _vmem_limit(blk), (a, b), ("parallel", "parallel", "arbitrary"), in_hbm=False)
    return res[0] if beside is None else (res[0], got)


def _mm_pieces(name, pieces, starts, width, step_block, other, other_spec, pieces_first, piece_rows, o_spec, o_shape,
               o_dtype, grid, dims, beside=None):
    n_p, nk = len(pieces), grid[2]
    acc_shape = tuple(s for s in o_spec.block_shape if s is not None)

    def which(i, j, k):
        blk = step_block(i, j, k)
        idx = 0
        for s in starts[1:]:
            idx = idx + (blk >= s).astype(jnp.int32)
        return idx, blk

    def piece_spec(p, t_rows):
        def index(i, j, k):
            idx, blk = which(i, j, k)
            mine = idx == p
            return jnp.where(mine, piece_rows(i, j, k), 0), jnp.where(mine, blk - starts[p], 0)
        return _spec((t_rows, width), index)

    def body(*refs):
        p_refs = refs[:n_p] if pieces_first else refs[1:1 + n_p]
        other_ref = refs[n_p] if pieces_first else refs[0]
        o_ref, acc_ref = refs[n_p + 1], refs[n_p + 2]
        i, j, k = pl.program_id(0), pl.program_id(1), pl.program_id(2)

        @pl.when(k == 0)
        def _():
            acc_ref[...] = jnp.zeros_like(acc_ref)

        idx, _ = which(i, j, k)
        for p in range(n_p):
            @pl.when(idx == p)
            def _(p=p):
                mine, fixed = p_refs[p][...].astype(BF16), other_ref[...].astype(BF16)
                pair = (mine, fixed) if pieces_first else (fixed, mine)
                acc_ref[...] += lax.dot_general(pair[0], pair[1], dims, preferred_element_type=F32)

        @pl.when(k == nk - 1)
        def _():
            o_ref[...] = acc_ref[...].astype(o_ref.dtype)

    t_rows = other_spec.block_shape[-2] if not pieces_first else o_spec.block_shape[-2]
    specs = [piece_spec(p, t_rows) for p in range(n_p)]
    in_specs = specs + [other_spec] if pieces_first else [other_spec] + specs
    operands = list(pieces) + [other] if pieces_first else [other] + list(pieces)
    blk = (n_p * 4 * t_rows * width + _nbytes(other_spec.block_shape, other.dtype)
           + _nbytes(acc_shape, o_dtype) + _nbytes(acc_shape, F32))
    res, got = _call_beside(
        beside, body, name, grid, in_specs, [o_spec], [jax.ShapeDtypeStruct(o_shape, o_dtype)],
        [pltpu.VMEM(acc_shape, F32)], _vmem_limit(blk), operands, ("parallel", "parallel", "arbitrary"), in_hbm=False)
    return res[0] if beside is None else (res[0], got)


def _swiglu_fn(gate_up):
    gate, up = gate_up[0], gate_up[1]
    return gate * jax.nn.sigmoid(gate) * up


def _ffn_in_swiglu(name, h, w, t_m):
    seq, d = h.shape
    n_half, n = w.shape[0] // 2, w.shape[2]

    def body(h_ref, wg_ref, wu_ref, a_ref, f_ref):
        hb = h_ref[...]
        a_ref[0] = lax.dot_general(hb, wg_ref[...], NN, preferred_element_type=F32).astype(a_ref.dtype)
        a_ref[1] = lax.dot_general(hb, wu_ref[...], NN, preferred_element_type=F32).astype(a_ref.dtype)
        f_ref[...] = _swiglu_fn(a_ref[...].astype(F32)).astype(f_ref.dtype)

    blk = 2 * t_m * d + 4 * d * n + 6 * t_m * n + 12 * t_m * n
    return pl.pallas_call(
        body, name=name, grid=(seq // t_m, n_half),
        in_specs=[_spec((t_m, d), lambda i, j: (i, 0)), _spec((None, d, n), lambda i, j: (j, 0, 0)),
                  _spec((None, d, n), lambda i, j: (j + n_half, 0, 0))],
        out_specs=[_spec((2, None, t_m, n), lambda i, j: (0, j, i, 0)), _spec((None, t_m, n), lambda i, j: (j, i, 0))],
        out_shape=[jax.ShapeDtypeStruct((2, n_half, seq, n), FFN_ACT), jax.ShapeDtypeStruct((n_half, seq, n), BF16)],
        compiler_params=pltpu.CompilerParams(dimension_semantics=("parallel", "parallel"),
                                             vmem_limit_bytes=_vmem_limit(blk)),
    )(h, w, w)


def _ffn_out_dx_swiglu(name, d_y, w, a, t_m):
    seq, d = d_y.shape
    n_half, n = w.shape[0], w.shape[1]

    def body(dy_ref, w_ref, a_ref, da_ref):
        d_f = lax.dot_general(dy_ref[...], w_ref[...], NT, preferred_element_type=F32)
        gate, up = a_ref[0].astype(F32), a_ref[1].astype(F32)
        s = jax.nn.sigmoid(gate)
        gs = gate * s
        da_ref[0] = (d_f * up * (s + gs * (1.0 - s))).astype(da_ref.dtype)
        da_ref[1] = (d_f * gs).astype(da_ref.dtype)

    blk = 2 * t_m * d + 2 * d * n + 8 * t_m * n + 24 * t_m * n
    return pl.pallas_call(
        body, name=name, grid=(seq // t_m, n_half),
        in_specs=[_spec((t_m, d), lambda i, j: (i, 0)), _spec((None, n, d), lambda i, j: (j, 0, 0)),
                  _spec((2, None, t_m, n), lambda i, j: (0, j, i, 0))],
        out_specs=_spec((2, None, t_m, n), lambda i, j: (0, j, i, 0)),
        out_shape=jax.ShapeDtypeStruct((2, n_half, seq, n), BF16),
        compiler_params=pltpu.CompilerParams(dimension_semantics=("parallel", "parallel"),
                                             vmem_limit_bytes=_vmem_limit(blk)),
    )(d_y, w, a)


def _merge_fn(y_sb, y_ssm, gates):
    half = gates.shape[-1] // 2
    return jax.nn.sigmoid(gates[:, :half]) * y_sb + jax.nn.sigmoid(gates[:, half:]) * y_ssm


def _up_merge(name, o_sb, s5_out, proj, gates_cb, w_sb, w_ssm, layer, t_rows):
    seq = o_sb.shape[0]
    d = w_sb.shape[2]

    def body(o_ref, s_ref, g_ref, w1_ref, w2_ref, m_ref, y1_ref, y2_ref):
        y_sb = lax.dot_general(o_ref[...], w1_ref[...], NN, preferred_element_type=F32)
        y_ssm = lax.dot_general(s_ref[...], w2_ref[...], NN, preferred_element_type=F32)
        m_ref[...] = _merge_fn(y_sb, y_ssm, g_ref[...]).astype(m_ref.dtype)
        y1_ref[...] = y_sb.astype(y1_ref.dtype)
        y2_ref[...] = y_ssm.astype(y2_ref.dtype)

    row = lambda width: _spec((t_rows, width), lambda i: (i, 0))
    whole = lambda w: _spec((None,) + w.shape[1:], lambda i: (layer, 0, 0))
    blk = t_rows * (2 * o_sb.shape[1] + 2 * s5_out.shape[1] + 8 * d + 6 * d + 24 * d) + 4 * d * (o_sb.shape[1] + s5_out.shape[1])
    return pl.pallas_call(
        body, name=name, grid=(seq // t_rows,),
        in_specs=[row(o_sb.shape[1]), row(s5_out.shape[1]), _spec((t_rows, 2 * d), lambda i: (i, gates_cb)),
                  whole(w_sb), whole(w_ssm)],
        out_specs=[row(d)] * 3, out_shape=[jax.ShapeDtypeStruct((seq, d), BF16)] * 3,
        compiler_params=pltpu.CompilerParams(dimension_semantics=("parallel",), vmem_limit_bytes=_vmem_limit(blk)),
    )(_in_hbm(o_sb), _in_hbm(s5_out), _in_hbm(proj), w_sb, w_ssm)


def _tile(n, pref=1024):
    t = pref
    while t >= LANES:
        if n % t == 0:
            return t
        t -= LANES
    return n


def _rowwise(name, fn, ins, outs, grid):
    n_in = len(ins)

    def body(*refs):
        vals = fn(*[r[...].astype(F32) for r in refs[:n_in]])
        if not isinstance(vals, (tuple, list)):
            vals = (vals,)
        for r, v in zip(refs[n_in:], vals):
            r[...] = v.astype(r.dtype)

    blk = sum(_nbytes(bs, a.dtype) for a, bs, _ in ins) + sum(_nbytes(bs, d) + _nbytes(bs, F32) for _, d, bs, _ in outs)
    return pl.pallas_call(
        body, name=name, grid=grid,
        in_specs=[_spec(bs, im) for _, bs, im in ins],
        out_specs=[_spec(bs, im) for _, _, bs, im in outs],
        out_shape=[jax.ShapeDtypeStruct(s, d) for s, d, _, _ in outs],
        compiler_params=pltpu.CompilerParams(dimension_semantics=("parallel",) * len(grid),
                                             vmem_limit_bytes=_vmem_limit(2 * blk)),
    )(*[_in_hbm(a) for a, _, _ in ins])


def _rowwise_vjp(name, fn, ins, cts, wrt, grid):
    n_in, n_ct = len(ins), len(cts)
    idx = [w[0] for w in wrt]

    def body(*refs):
        prim = [r[...].astype(F32) for r in refs[:n_in]]
        ct = tuple(r[...].astype(F32) for r in refs[n_in:n_in + n_ct])
        o_refs = refs[n_in + n_ct:]

        def g(*sel):
            full = list(prim)
            for i, s in zip(idx, sel):
                full[i] = s
            out = fn(*full)
            return tuple(out) if isinstance(out, (tuple, list)) else (out,)

        _, pull = jax.vjp(g, *[prim[i] for i in idx])
        grads = pull(ct)
        first = pl.program_id(0) == 0
        for d in range(1, len(grid)):
            first = jnp.logical_and(first, pl.program_id(d) == 0)
        for w, o_ref, gr in zip(wrt, o_refs, grads):
            if w[1] == "row":
                o_ref[...] = gr.astype(o_ref.dtype)
            else:
                @pl.when(first)
                def _(o_ref=o_ref):
                    o_ref[...] = jnp.zeros_like(o_ref)

                o_ref[...] += gr.astype(o_ref.dtype)

    blk = (sum(_nbytes(bs, a.dtype) + _nbytes(bs, F32) for a, bs, _ in list(ins) + list(cts))
           + sum(_nbytes(w[4], w[3]) + _nbytes(w[4], F32) for w in wrt))
    return pl.pallas_call(
        body, name=name, grid=grid,
        in_specs=[_spec(bs, im) for _, bs, im in list(ins) + list(cts)],
        out_specs=[_spec(w[4], w[5]) for w in wrt],
        out_shape=[jax.ShapeDtypeStruct(w[2], w[3]) for w in wrt],
        compiler_params=pltpu.CompilerParams(dimension_semantics=("arbitrary",) * len(grid),
                                             vmem_limit_bytes=_vmem_limit(2 * blk)),
    )(*[_in_hbm(a) for a, _, _ in list(ins) + list(cts)])


def _normalize(x):
    mu = jnp.mean(x, axis=-1, keepdims=True)
    xc = x - mu
    var = jnp.mean(xc * xc, axis=-1, keepdims=True)
    return xc * lax.rsqrt(var + LN_EPS)


def _modulate(x, sc, sh):
    return _normalize(x) * (1.0 + sc) + sh


def _make_resid_fns(alpha):
    def resid_ln(x, y, gate, g, b):
        return _normalize(alpha * x + (1.0 + gate) * y) * g + b

    def resid_ln_mod(x, y, gate, g, b, sc, sh):
        xn = resid_ln(x, y, gate, g, b)
        return xn, _modulate(xn, sc, sh)

    return resid_ln, resid_ln_mod


def _s5_act_fn(yc, u, d_skip):
    return jax.nn.gelu(yc + d_skip * u)


def _s5_gate_fn(y1, t):
    return y1 * jax.nn.sigmoid(t)


def _s5_head_specs(yc, proj, u_cb, w_glu, layer, t_rows):
    width = yc.shape[1]
    row = _spec((t_rows, width), lambda i: (i, 0))
    u_spec = _spec((t_rows, width), lambda i: (i, u_cb))
    vec = _spec((1, width), lambda i: (0, 0))
    w_spec = _spec((None,) + w_glu.shape[1:], lambda i: (layer, 0, 0))
    return row, u_spec, vec, w_spec


def _s5_head(name, yc, proj, u_cb, d_skip, b_glu, w_glu, layer, t_rows):
    seq, width = yc.shape
    row, u_spec, vec, w_spec = _s5_head_specs(yc, proj, u_cb, w_glu, layer, t_rows)

    def body(yc_ref, u_ref, d_ref, b_ref, w_ref, o_ref):
        y1 = _s5_act_fn(yc_ref[...], u_ref[...], d_ref[...])
        t = lax.dot_general(y1.astype(BF16), w_ref[...], NN, preferred_element_type=F32) + b_ref[...]
        o_ref[...] = _s5_gate_fn(y1, t).astype(o_ref.dtype)

    return pl.pallas_call(
        body, name=name, grid=(seq // t_rows,), in_specs=[row, u_spec, vec, vec, w_spec], out_specs=row,
        out_shape=jax.ShapeDtypeStruct((seq, width), BF16),
        compiler_params=pltpu.CompilerParams(dimension_semantics=("parallel",),
                                             vmem_limit_bytes=_vmem_limit(40 * t_rows * width)),
    )(_in_hbm(yc), _in_hbm(proj), d_skip, b_glu, w_glu)


def _s5_head_bwd(name, yc, proj, u_cb, d_out, d_skip, b_glu, w_glu, layer, t_rows):
    seq, width = yc.shape
    row, u_spec, vec, w_spec = _s5_head_specs(yc, proj, u_cb, w_glu, layer, t_rows)
    n_t = seq // t_rows

    def body(yc_ref, u_ref, do_ref, d_ref, b_ref, w_ref, dyc_ref, du_ref, dw_ref, dd_ref, db_ref, acc_ref):
        i = pl.program_id(0)

        @pl.when(i == 0)
        def _():
            acc_ref[...] = jnp.zeros_like(acc_ref)
            dd_ref[...] = jnp.zeros_like(dd_ref)
            db_ref[...] = jnp.zeros_like(db_ref)

        y1, pull_act = jax.vjp(_s5_act_fn, yc_ref[...], u_ref[...], d_ref[...])
        y1_b = y1.astype(BF16)
        t = lax.dot_general(y1_b, w_ref[...], NN, preferred_element_type=F32) + b_ref[...]
        _, pull_gate = jax.vjp(_s5_gate_fn, y1, t)
        d_y1, d_t = pull_gate(do_ref[...].astype(F32))
        d_t_b = d_t.astype(BF16)
        d_y1 = d_y1 + lax.dot_general(d_t_b, w_ref[...], NT, preferred_element_type=F32)
        acc_ref[...] += lax.dot_general(y1_b, d_t_b, TN, preferred_element_type=F32)
        db_ref[...] += jnp.sum(d_t, axis=0, keepdims=True)
        d_yc, d_u, d_d = pull_act(d_y1)
        dyc_ref[...] = d_yc
        du_ref[...] = d_u
        dd_ref[...] += d_d

        @pl.when(i == n_t - 1)
        def _():
            dw_ref[...] = acc_ref[...].astype(dw_ref.dtype)

    whole = _spec((width, width), lambda i: (0, 0))
    return pl.pallas_call(
        body, name=name, grid=(n_t,), in_specs=[row, u_spec, row, vec, vec, w_spec],
        out_specs=[row, row, whole, vec, vec],
        out_shape=[jax.ShapeDtypeStruct((seq, width), F32), jax.ShapeDtypeStruct((seq, width), F32),
                   jax.ShapeDtypeStruct((width, width), GRAD_WIRE), jax.ShapeDtypeStruct((1, width), F32),
                   jax.ShapeDtypeStruct((1, width), F32)],
        scratch_shapes=[pltpu.VMEM((width, width), F32)],
        compiler_params=pltpu.CompilerParams(dimension_semantics=("arbitrary",),
                                             vmem_limit_bytes=_vmem_limit(80 * t_rows * width)),
    )(_in_hbm(yc), _in_hbm(proj), _in_hbm(d_out), d_skip, b_glu, w_glu)


def _sb_tri(kind):
    row = lax.broadcasted_iota(jnp.int32, (SB_BLOCK, SB_BLOCK), 0)
    col = lax.broadcasted_iota(jnp.int32, (SB_BLOCK, SB_BLOCK), 1)
    if kind == "after":
        return (row > col).astype(BF16)
    if kind == "from":
        return (row >= col).astype(BF16)
    return col < row


def _split_dot(x, m):
    hi = x.astype(BF16)
    lo = (x - hi.astype(F32)).astype(BF16)
    return (lax.dot_general(hi, m, NN, preferred_element_type=F32)
            + lax.dot_general(lo, m, NN, preferred_element_type=F32))


def _sb_scores(qh, k2):
    z = lax.dot_general(qh, k2, NT, preferred_element_type=F32)
    log_beta = jnp.minimum(z, 0.0) - jnp.log(1.0 + jnp.exp(-jnp.abs(z)))
    return log_beta, log_beta - z


def _sb_attention_fwd(proj, sb_width, beside=None):
    seq = proj.shape[0]
    n_pair, n_q = sb_width // LANES, seq // (SB_BLOCK * SB_GROUP)
    scale = 1.0 / (HEAD_DIM ** 0.5)
    chains = [(s, h) for s in range(SB_GROUP) for h in range(2)]

    def body(q_ref, k_ref, v_ref, o_ref, o32_ref):
        first = pl.program_id(1) * SB_GROUP
        lane = lax.broadcasted_iota(jnp.int32, (SB_BLOCK, LANES), 1)
        m_after, causal = _sb_tri("after"), _sb_tri("mask")
        heads = [lane < HEAD_DIM, lane >= HEAD_DIM]
        rows = [pl.ds(s * SB_BLOCK, SB_BLOCK) for s in range(SB_GROUP)]
        qh = {(s, h): (jnp.where(heads[h], q_ref[rows[s], :], 0.0) * scale).astype(BF16) for s, h in chains}

        def key_rows(s, r):
            kb = first + s - r
            return kb >= 0, pl.ds(pl.multiple_of(jnp.maximum(kb, 0) * SB_BLOCK, SB_BLOCK), SB_BLOCK)

        def scores(r, diag):
            out = []
            for s in range(SB_GROUP):
                live, ks = key_rows(s, r)
                k2 = k_ref[ks, :].astype(BF16)
                for h in range(2):
                    log_beta, log_1m = _sb_scores(qh[s, h], k2)
                    if diag:
                        log_1m = jnp.where(causal, log_1m, 0.0)
                    else:
                        log_1m = jnp.where(live, log_1m, 0.0)
                    out += [log_beta + _split_dot(log_1m, m_after), jnp.sum(log_1m, axis=1, keepdims=True)]
            return tuple(out)

        def weigh(r, sc, carry, acc, diag):
            out = []
            for c, (s, h) in enumerate(chains):
                live, ks = key_rows(s, r)
                v2 = v_ref[ks, :].astype(BF16)
                w = jnp.exp(sc[2 * c] + carry[c])
                w = jnp.where(causal, w, 0.0) if diag else jnp.where(live, w, 0.0)
                out.append(acc[c] + lax.dot_general(w.astype(BF16), v2, NN, preferred_element_type=F32))
            return tuple(out)

        zero = jnp.zeros((SB_BLOCK, LANES), F32)
        zcol = jnp.zeros((SB_BLOCK, 1), F32)
        sc = scores(0, True)
        acc = weigh(0, sc, (zcol,) * len(chains), (zero,) * len(chains), True)
        carry = tuple(sc[2 * c + 1] for c in range(len(chains)))
        last = first + SB_GROUP - 1

        def loop(st):
            r, carry, acc = st
            sc = scores(r, False)
            after = tuple(carry[c] + sc[2 * c + 1] for c in range(len(chains)))
            top = jnp.max(after[0])
            for c in range(1, len(chains)):
                top = jnp.maximum(top, jnp.max(after[c]))
            acc = weigh(r, sc, carry, acc, False)
            return jnp.where(top < SB_UNDERFLOW, last + 1, r + 1), after, acc

        _, _, acc = lax.while_loop(lambda st: st[0] <= last, loop, (1, carry, acc))
        for s in range(SB_GROUP):
            out = jnp.where(heads[0], acc[2 * s], acc[2 * s + 1])
            o_ref[rows[s], :] = out.astype(o_ref.dtype)
            o32_ref[rows[s], :] = out

    q_spec = _spec((SB_BLOCK * SB_GROUP, LANES), lambda h, i: (i, h))
    kv = [_spec((seq, LANES), lambda h, i, o=o: (0, o + h)) for o in (n_pair, 2 * n_pair)]
    return _call_beside(
        beside, body, "sb_attention_fwd", (n_pair, n_q), [q_spec] + kv, [q_spec, q_spec],
        [jax.ShapeDtypeStruct((seq, sb_width), BF16), jax.ShapeDtypeStruct((seq, sb_width), F32)], [],
        _vmem_limit(2 * seq * LANES * 4), (proj, proj, proj), ("parallel", "arbitrary"))


def _sb_attention_bwd(proj, o32, do, sb_width, beside=None):
    seq = proj.shape[0]
    n_pair, n_q = sb_width // LANES, seq // SB_BLOCK
    scale = 1.0 / (HEAD_DIM ** 0.5)

    def body(q_ref, k_ref, v_ref, o_ref, do_ref, dq_ref, dk_out_ref, dv_out_ref, dk_ref, dv_ref):
        qi = pl.program_id(1)

        @pl.when(qi == 0)
        def _():
            dk_ref[...] = jnp.zeros_like(dk_ref)
            dv_ref[...] = jnp.zeros_like(dv_ref)

        q2 = q_ref[...]
        do2 = do_ref[...].astype(F32)
        o2 = o_ref[...]
        lane = lax.broadcasted_iota(jnp.int32, (SB_BLOCK, LANES), 1)
        m_after, m_from, causal = _sb_tri("after"), _sb_tri("from"), _sb_tri("mask")
        heads = [lane < HEAD_DIM, lane >= HEAD_DIM]
        qh = [(jnp.where(m, q2, 0.0) * scale).astype(BF16) for m in heads]
        doh = [jnp.where(m, do2, 0.0) for m in heads]
        doh_b = [v.astype(BF16) for v in doh]
        total = [jnp.sum(v * o2, axis=1, keepdims=True) for v in doh]

        def scores(kb, diag):
            ks = pl.multiple_of(kb * SB_BLOCK, SB_BLOCK)
            k2 = k_ref[pl.ds(ks, SB_BLOCK), :].astype(BF16)
            v2 = v_ref[pl.ds(ks, SB_BLOCK), :].astype(BF16)
            out = []
            for h in range(2):
                log_beta, log_1m = _sb_scores(qh[h], k2)
                if diag:
                    log_1m = jnp.where(causal, log_1m, 0.0)
                out += [log_beta + _split_dot(log_1m, m_after), jnp.sum(log_1m, axis=1, keepdims=True),
                        lax.dot_general(doh_b[h], v2, NT, preferred_element_type=F32), log_beta]
            return tuple(out)

        def pull(kb, sc, carry, right, dq, diag):
            ks = pl.multiple_of(kb * SB_BLOCK, SB_BLOCK)
            k2 = k_ref[pl.ds(ks, SB_BLOCK), :].astype(BF16)
            dv_blk, dk_blk, right_out, dq_out = None, None, [], []
            for h in range(2):
                arg, _, d_w, log_beta = sc[4 * h:4 * h + 4]
                w = jnp.exp(arg + carry[h])
                if diag:
                    w = jnp.where(causal, w, 0.0)
                w_b = w.astype(BF16)
                d_arg = d_w * w_b.astype(F32)
                dv_h = lax.dot_general(w_b, doh_b[h], TN, preferred_element_type=F32)
                d_log_1m = total[h] - right[h] - _split_dot(d_arg, m_from)
                beta = jnp.exp(log_beta)
                dz = d_arg * (1.0 - beta) - beta * d_log_1m
                if diag:
                    dz = jnp.where(causal, dz, 0.0)
                dz_b = dz.astype(BF16)
                dk_h = lax.dot_general(dz_b, qh[h], TN, preferred_element_type=F32)
                dv_blk = dv_h if h == 0 else dv_blk + dv_h
                dk_blk = dk_h if h == 0 else dk_blk + dk_h
                dq_out.append(dq[h] + lax.dot_general(dz_b, k2, NN, preferred_element_type=F32))
                right_out.append(right[h] + jnp.sum(d_arg, axis=1, keepdims=True))
            dv_ref[pl.ds(ks, SB_BLOCK), :] += dv_blk
            dk_ref[pl.ds(ks, SB_BLOCK), :] += dk_blk
            return tuple(right_out), tuple(dq_out)

        zero = jnp.zeros((SB_BLOCK, LANES), F32)
        zcol = jnp.zeros((SB_BLOCK, 1), F32)
        sc = scores(qi, True)
        right, dq = pull(qi, sc, (zcol, zcol), (zcol, zcol), (zero, zero), True)
        carry = (sc[1], sc[5])

        def loop(st):
            kb, carry, right, dq = st
            sc = scores(kb, False)
            after = (carry[0] + sc[1], carry[1] + sc[5])
            done = jnp.maximum(jnp.max(after[0]), jnp.max(after[1])) < SB_UNDERFLOW
            right, dq = pull(kb, sc, carry, right, dq, False)
            return jnp.where(done, -1, kb - 1), after, right, dq

        _, _, _, dq = lax.while_loop(lambda st: st[0] >= 0, loop, (qi - 1, carry, right, dq))
        dq_ref[...] = (jnp.where(heads[0], dq[0], dq[1]) * scale).astype(dq_ref.dtype)

        @pl.when(qi == n_q - 1)
        def _():
            dk_out_ref[...] = dk_ref[...].astype(dk_out_ref.dtype)
            dv_out_ref[...] = dv_ref[...].astype(dv_out_ref.dtype)

    q_spec = _spec((SB_BLOCK, LANES), lambda h, i: (i, h))
    kv = [_spec((seq, LANES), lambda h, i, o=o: (0, o + h)) for o in (n_pair, 2 * n_pair)]
    full = _spec((seq, LANES), lambda h, i: (0, h))
    return _call_beside(
        beside, body, "sb_attention_bwd", (n_pair, n_q), [q_spec] + kv + [q_spec, q_spec], [q_spec, full, full],
        [jax.ShapeDtypeStruct((seq, sb_width), BF16)] * 3,
        [pltpu.VMEM((seq, LANES), F32), pltpu.VMEM((seq, LANES), F32)],
        _vmem_limit(4 * seq * LANES * 4), (proj, proj, proj, o32, do), ("parallel", "arbitrary"))


def _s5_discretize(a_re, a_im, log_dt, b_re, b_im, c_re, c_im):
    n_g, n_p = a_re.shape
    c_g = b_re.shape[-1]
    ns = n_g // SLAB_GROUPS
    dt = jnp.exp(log_dt)[:, None]
    xr, xi = a_re * dt, a_im * dt
    mag = jnp.exp(xr)
    lr, li = mag * jnp.cos(xi), mag * jnp.sin(xi)
    den = a_re * a_re + a_im * a_im
    fr = ((lr - 1.0) * a_re + li * a_im) / den
    fi = (li * a_re - (lr - 1.0) * a_im) / den
    bb_re = fr[..., None] * b_re - fi[..., None] * b_im
    bb_im = fr[..., None] * b_im + fi[..., None] * b_re
    eye = jnp.eye(SLAB_GROUPS, dtype=F32)

    def diag_b(m):
        m = jnp.transpose(m.reshape(ns, SLAB_GROUPS, n_p, c_g), (0, 1, 3, 2))
        m = m[:, :, :, None, :] * eye[None, :, None, :, None]
        return m.reshape(ns, SLAB_GROUPS * c_g, SLAB_GROUPS * n_p)

    def diag_c(m):
        m = jnp.transpose(m.reshape(ns, SLAB_GROUPS, c_g, n_p), (0, 1, 3, 2))
        m = m[:, :, :, None, :] * eye[None, :, None, :, None]
        return m.reshape(ns, SLAB_GROUPS * n_p, SLAB_GROUPS * c_g)

    bs = jnp.concatenate([diag_b(bb_re), diag_b(bb_im)], axis=-1)
    cs = jnp.concatenate([diag_c(c_re), -diag_c(c_im)], axis=1)
    lam = jnp.concatenate([lr.reshape(ns, 1, -1), li.reshape(ns, 1, -1)], axis=-1)
    return bs, cs, lam


def _s5_powers(a_re, a_im, log_dt, n):
    n_g, n_p = a_re.shape
    ns = n_g // SLAB_GROUPS
    dt = jnp.exp(log_dt)[:, None]
    mag = jnp.exp(a_re * dt)
    base_r, base_i = mag * jnp.cos(a_im * dt), mag * jnp.sin(a_im * dt)
    steps = jnp.arange(1, n + 1, dtype=jnp.int32)[:, None, None]
    pr, pi = jnp.ones((n, n_g, n_p), F32), jnp.zeros((n, n_g, n_p), F32)
    for b in range(n.bit_length()):
        take = ((steps >> b) & 1) == 1
        pr, pi = (jnp.where(take, pr * base_r - pi * base_i, pr), jnp.where(take, pr * base_i + pi * base_r, pi))
        base_r, base_i = base_r * base_r - base_i * base_i, 2.0 * base_r * base_i

    def slabs(re, im):
        one = lambda m: jnp.transpose(m.reshape(n, ns, SLAB_GROUPS * n_p), (1, 0, 2))
        return jnp.concatenate([one(re), one(im)], axis=-1)

    return slabs(pr, pi), slabs(pr[::-1], -pi[::-1])


def _lanes(j):
    return slice(j * LANES, (j + 1) * LANES)


def _tile8(k):
    return pl.ds(pl.multiple_of(k * SUBLANES, SUBLANES), SUBLANES)


def _s5_interleave(dst_ref, src_ref, t_seg):
    def body(k, _):
        dst_ref[_tile8(k), :] = src_ref[pl.ds(k, SUBLANES, stride=t_seg), :]
        return 0

    lax.fori_loop(0, t_seg, body, 0, unroll=4)


def _s5_join_segments(st_ref, end_ref, car_ref, tab_ref, row, order, n_pair):
    for j in range(n_pair):
        re, im = _lanes(j), _lanes(n_pair + j)
        cr, ci = st_ref[:, re], st_ref[:, im]
        tr, ti = tab_ref[row:row + 1, re], tab_ref[row:row + 1, im]
        for s in order:
            car_ref[s:s + 1, re] = cr
            car_ref[s:s + 1, im] = ci
            er, ei = end_ref[s:s + 1, re], end_ref[s:s + 1, im]
            cr, ci = er + tr * cr - ti * ci, ei + tr * ci + ti * cr
        st_ref[:, re] = cr
        st_ref[:, im] = ci


def _s5_add_carries(buf_ref, car_ref, tab_ref, t_seg, n_pair):
    def fix(k, _):
        rows = _tile8(k)
        tab = tab_ref[pl.ds(k, 1), :]
        for j in range(n_pair):
            re, im = _lanes(j), _lanes(n_pair + j)
            cr, ci = car_ref[:, re], car_ref[:, im]
            tr, ti = tab[:, re], tab[:, im]
            buf_ref[rows, re] += tr * cr - ti * ci
            buf_ref[rows, im] += tr * ci + ti * cr
        return 0

    lax.fori_loop(0, t_seg, fix, 0, unroll=2)


def _s5_scan_fwd(proj, u_col, bs, cs, lam, pw, t_blk, beside=None):
    seq = proj.shape[0]
    ns, _, w2 = bs.shape
    n_pair = w2 // (2 * LANES)
    t_seg, n_t = t_blk // SUBLANES, seq // t_blk

    def body(u_ref, bs_ref, cs_ref, lam_ref, pw_ref, yc_ref, h_ref, st_ref, end_ref, car_ref, ui_ref, bu_ref, yi_ref):
        @pl.when(pl.program_id(1) == 0)
        def _():
            st_ref[...] = jnp.zeros_like(st_ref)

        _s5_interleave(ui_ref, u_ref, t_seg)
        bu_ref[...] = lax.dot_general(ui_ref[...].astype(BF16), bs_ref[...], NN, preferred_element_type=F32)
        lam_r = [jnp.broadcast_to(lam_ref[:, _lanes(j)], (SUBLANES, LANES)) for j in range(n_pair)]
        lam_i = [jnp.broadcast_to(lam_ref[:, _lanes(n_pair + j)], (SUBLANES, LANES)) for j in range(n_pair)]

        def step(k, c):
            rows = _tile8(k)
            out = []
            for j in range(n_pair):
                hr, hi = c[2 * j], c[2 * j + 1]
                nr = lam_r[j] * hr - lam_i[j] * hi + bu_ref[rows, _lanes(j)]
                ni = lam_i[j] * hr + lam_r[j] * hi + bu_ref[rows, _lanes(n_pair + j)]
                h_ref[rows, _lanes(j)] = nr
                h_ref[rows, _lanes(n_pair + j)] = ni
                out += [nr, ni]
            return tuple(out)

        ends = lax.fori_loop(0, t_seg, step, (jnp.zeros((SUBLANES, LANES), F32),) * (2 * n_pair), unroll=4)
        for j in range(n_pair):
            end_ref[:, _lanes(j)] = ends[2 * j]
            end_ref[:, _lanes(n_pair + j)] = ends[2 * j + 1]
        _s5_join_segments(st_ref, end_ref, car_ref, pw_ref, t_seg - 1, list(range(SUBLANES)), n_pair)
        _s5_add_carries(h_ref, car_ref, pw_ref, t_seg, n_pair)
        yi_ref[...] = lax.dot_general(h_ref[...].astype(BF16), cs_ref[...], NN, preferred_element_type=F32)

        def scatter(k, _):
            yc_ref[pl.ds(k, SUBLANES, stride=t_seg), :] = yi_ref[_tile8(k), :]
            return 0

        lax.fori_loop(0, t_seg, scatter, 0, unroll=4)

    return _call_beside(
        beside, body, "s5_scan_fwd", (ns, n_t),
        [_spec((t_blk, LANES), lambda s, i: (i, u_col + s)),
         _spec((None, LANES, w2), lambda s, i: (s, 0, 0)),
         _spec((None, w2, LANES), lambda s, i: (s, 0, 0)),
         _spec((None, 1, w2), lambda s, i: (s, 0, 0)),
         _spec((None, t_seg, w2), lambda s, i: (s, 0, 0))],
        [_spec((t_blk, LANES), lambda s, i: (i, s)),
         _spec((None, t_blk, w2), lambda s, i: (s, i, 0))],
        [jax.ShapeDtypeStruct((seq, ns * LANES), F32), jax.ShapeDtypeStruct((ns, seq, w2), F32)],
        [pltpu.VMEM((1, w2), F32), pltpu.VMEM((SUBLANES, w2), F32), pltpu.VMEM((SUBLANES, w2), F32),
         pltpu.VMEM((t_blk, LANES), F32), pltpu.VMEM((t_blk, w2), F32), pltpu.VMEM((t_blk, LANES), F32)],
        _vmem_limit(3 * t_blk * w2 * 4), (proj, bs, cs, lam, pw), ("parallel", "arbitrary"))


def _s5_scan_bwd(proj, u_col, states, d_yc, du_extra, bs, cs, lam, qw, t_blk):
    seq = proj.shape[0]
    ns, _, w2 = bs.shape
    n_pair = w2 // (2 * LANES)
    t_seg, n_t = t_blk // SUBLANES, seq // t_blk

    def body(u_ref, h_ref, hp_ref, dyc_ref, dux_ref, bs_ref, cs_ref, lam_ref, qw_ref,
             du_ref, dbs_ref, dcs_ref, dlam_ref, g_ref, gd_ref, st_ref, end_ref, car_ref, ui_ref, dyi_ref, dui_ref):
        i = pl.program_id(1)

        @pl.when(i == 0)
        def _():
            st_ref[...] = jnp.zeros_like(st_ref)
            dbs_ref[...] = jnp.zeros_like(dbs_ref)
            dcs_ref[...] = jnp.zeros_like(dcs_ref)
            dlam_ref[...] = jnp.zeros_like(dlam_ref)

        _s5_interleave(ui_ref, u_ref, t_seg)
        _s5_interleave(dyi_ref, dyc_ref, t_seg)
        dyc_b = dyi_ref[...].astype(BF16)
        gd_ref[...] = lax.dot_general(dyc_b, cs_ref[...], NT, preferred_element_type=F32)
        lam_r = [jnp.broadcast_to(lam_ref[:, _lanes(j)], (SUBLANES, LANES)) for j in range(n_pair)]
        lam_i = [jnp.broadcast_to(lam_ref[:, _lanes(n_pair + j)], (SUBLANES, LANES)) for j in range(n_pair)]

        def step(kk, c):
            rows = _tile8(t_seg - 1 - kk)
            out = []
            for j in range(n_pair):
                gr_n, gi_n = c[2 * j], c[2 * j + 1]
                gr = gd_ref[rows, _lanes(j)] + lam_r[j] * gr_n + lam_i[j] * gi_n
                gi = gd_ref[rows, _lanes(n_pair + j)] + lam_r[j] * gi_n - lam_i[j] * gr_n
                g_ref[rows, _lanes(j)] = gr
                g_ref[rows, _lanes(n_pair + j)] = gi
                out += [gr, gi]
            return tuple(out)

        zero = jnp.zeros((SUBLANES, LANES), F32)
        firsts = lax.fori_loop(0, t_seg, step, (zero,) * (2 * n_pair), unroll=4)
        for j in range(n_pair):
            end_ref[:, _lanes(j)] = firsts[2 * j]
            end_ref[:, _lanes(n_pair + j)] = firsts[2 * j + 1]
        _s5_join_segments(st_ref, end_ref, car_ref, qw_ref, 0, list(range(SUBLANES))[::-1], n_pair)
        _s5_add_carries(g_ref, car_ref, qw_ref, t_seg, n_pair)

        def pair_up(k, c):
            rows, prev = _tile8(k), _tile8(k - 1)
            out = []
            for j in range(n_pair):
                re, im = _lanes(j), _lanes(n_pair + j)
                gr, gi, hr, hi = g_ref[rows, re], g_ref[rows, im], h_ref[prev, re], h_ref[prev, im]
                out += [c[2 * j] + gr * hr + gi * hi, c[2 * j + 1] + gi * hr - gr * hi]
            return tuple(out)

        acc = lax.fori_loop(1, t_seg, pair_up, (zero,) * (2 * n_pair), unroll=4)
        has_prev = (i < n_t - 1).astype(F32)
        first_seg = lax.broadcasted_iota(jnp.int32, (SUBLANES, LANES), 0) == 0
        last = _tile8(t_seg - 1)
        for j in range(n_pair):
            re, im = _lanes(j), _lanes(n_pair + j)
            gr, gi = g_ref[0:SUBLANES, re], g_ref[0:SUBLANES, im]
            hr = jnp.where(first_seg, hp_ref[SUBLANES - 1:, re] * has_prev, pltpu.roll(h_ref[last, re], 1, 0))
            hi = jnp.where(first_seg, hp_ref[SUBLANES - 1:, im] * has_prev, pltpu.roll(h_ref[last, im], 1, 0))
            dlam_ref[:, re] += jnp.sum(acc[2 * j] + gr * hr + gi * hi, axis=0, keepdims=True)
            dlam_ref[:, im] += jnp.sum(acc[2 * j + 1] + gi * hr - gr * hi, axis=0, keepdims=True)

        g_b = g_ref[...].astype(BF16)
        dui_ref[...] = lax.dot_general(g_b, bs_ref[...], NT, preferred_element_type=F32)
        dbs_ref[...] += lax.dot_general(ui_ref[...].astype(BF16), g_b, TN, preferred_element_type=F32)
        dcs_ref[...] += lax.dot_general(h_ref[...].astype(BF16), dyc_b, TN, preferred_element_type=F32)

        def scatter(k, _):
            rows = pl.ds(k, SUBLANES, stride=t_seg)
            du_ref[rows, :] = (dui_ref[_tile8(k), :] + dux_ref[rows, :]).astype(du_ref.dtype)
            return 0

        lax.fori_loop(0, t_seg, scatter, 0, unroll=4)

    rev = lambda i: n_t - 1 - i
    return pl.pallas_call(
        body, name="s5_scan_bwd", grid=(ns, n_t),
        in_specs=[_spec((t_blk, LANES), lambda s, i: (rev(i), u_col + s)),
                  _spec((None, t_blk, w2), lambda s, i: (s, rev(i), 0)),
                  _spec((None, SUBLANES, w2), lambda s, i: (s, jnp.maximum(rev(i) * t_seg - 1, 0), 0)),
                  _spec((t_blk, LANES), lambda s, i: (rev(i), s)),
                  _spec((t_blk, LANES), lambda s, i: (rev(i), s)),
                  _spec((None, LANES, w2), lambda s, i: (s, 0, 0)),
                  _spec((None, w2, LANES), lambda s, i: (s, 0, 0)),
                  _spec((None, 1, w2), lambda s, i: (s, 0, 0)),
                  _spec((None, t_seg, w2), lambda s, i: (s, 0, 0))],
        out_specs=[_spec((t_blk, LANES), lambda s, i: (rev(i), s)),
                   _spec((None, LANES, w2), lambda s, i: (s, 0, 0)),
                   _spec((None, w2, LANES), lambda s, i: (s, 0, 0)),
                   _spec((None, 1, w2), lambda s, i: (s, 0, 0))],
        out_shape=[jax.ShapeDtypeStruct((seq, ns * LANES), F32), jax.ShapeDtypeStruct(bs.shape, F32),
                   jax.ShapeDtypeStruct(cs.shape, F32), jax.ShapeDtypeStruct(lam.shape, F32)],
        scratch_shapes=[pltpu.VMEM((t_blk, w2), F32), pltpu.VMEM((t_blk, w2), F32), pltpu.VMEM((1, w2), F32),
                        pltpu.VMEM((SUBLANES, w2), F32), pltpu.VMEM((SUBLANES, w2), F32),
                        pltpu.VMEM((t_blk, LANES), F32), pltpu.VMEM((t_blk, LANES), F32), pltpu.VMEM((t_blk, LANES), F32)],
        compiler_params=pltpu.CompilerParams(dimension_semantics=("parallel", "arbitrary"),
                                             vmem_limit_bytes=_vmem_limit(5 * t_blk * w2 * 4)),
    )(*[_in_hbm(a) for a in (proj, states, states, d_yc, du_extra, bs, cs, lam, qw)])


def _loss_head(y, target, t_m):
    seq, d = y.shape

    def body(y_ref, t_ref, loss_ref, dy_ref):
        @pl.when(pl.program_id(0) == 0)
        def _():
            loss_ref[...] = jnp.zeros_like(loss_ref)

        diff = y_ref[...] - t_ref[...]
        dy_ref[...] = diff / d
        loss_ref[...] += 0.5 * jnp.sum(diff * diff) / d

    row = _spec((t_m, d), lambda i: (i, 0))
    return pl.pallas_call(
        body, name="loss_head", grid=(seq // t_m,), in_specs=[row, row],
        out_specs=[_spec((SUBLANES, LANES), lambda i: (0, 0)), row],
        out_shape=[jax.ShapeDtypeStruct((SUBLANES, LANES), F32), jax.ShapeDtypeStruct((seq, d), F32)],
        compiler_params=pltpu.CompilerParams(dimension_semantics=("arbitrary",),
                                             vmem_limit_bytes=_vmem_limit(6 * t_m * d * 4)),
    )(_in_hbm(y), _in_hbm(target))


def _adamw_fn(w, m, v, *partials):
    g = partials[0]
    for p in partials[1:]:
        g = g + p
    m2 = ADAM_B1 * m + (1.0 - ADAM_B1) * g
    v2 = ADAM_B2 * v + (1.0 - ADAM_B2) * (g * g)
    m_hat = m2 / (1.0 - ADAM_B1 ** ADAM_STEP)
    v_hat = v2 / (1.0 - ADAM_B2 ** ADAM_STEP)
    delta = -ADAM_LR * (m_hat / (jnp.sqrt(v_hat) + ADAM_EPS) + ADAM_WD * w)
    return g, delta, m2, v2


def _adamw(name, w, m, v, partials):
    rows, cols = w.shape
    t_r = rows
    for cand in (512, 256, 128, 64, 32, 16, 8):
        if rows % cand == 0 and cand * cols * 4 <= (1 << 20):
            t_r = cand
            break
    n_p = partials.shape[0]
    row = lambda i: (i, 0)
    ins = [(a, (t_r, cols), row) for a in (w, m, v)]
    ins += [(partials, (None, t_r, cols), (lambda i, j=j: (j, i, 0))) for j in range(n_p)]
    outs = [((rows, cols), F32, (t_r, cols), row)] * 4
    return _rowwise(name, _adamw_fn, ins, outs, (rows // t_r,))


SMALL_PARAMS = ("b_ada", "ssm_a_re", "ssm_a_im", "ssm_log_dt", "ssm_b_re", "ssm_b_im", "ssm_c_re", "ssm_c_im",
                "ssm_d", "b_glu", "ln1_g", "ln1_b", "ln2_g", "ln2_b")
WEIGHTS = ("w_ada", "b_ada", "w_in", "w_sb_up", "ssm_a_re", "ssm_a_im", "ssm_log_dt", "ssm_b_re", "ssm_b_im",
           "ssm_c_re", "ssm_c_im", "ssm_d", "w_glu", "b_glu", "w_ssm_up", "w_out", "ln1_g", "ln1_b", "w_ffn_in",
           "w_ffn_out", "ln2_g", "ln2_b")
ARG_NAMES = (("x", "c") + WEIGHTS + ("loss_target",) + tuple("m_" + n for n in WEIGHTS)
             + tuple("v_" + n for n in WEIGHTS))


def _pack(arrs):
    flat = jnp.concatenate([a.reshape(-1) for a in arrs])
    pad = (-flat.shape[0]) % (PACK_ROWS * LANES)
    return jnp.pad(flat, (0, pad)).reshape(-1, LANES)


def _unpack(packed, like):
    lead = packed.shape[:-2]
    flat = packed.reshape(lead + (-1,))
    out, off = [], 0
    for a in like:
        out.append(flat[..., off:off + a.size].reshape(lead + a.shape))
        off += a.size
    return out


def kernel(x, c, w_ada, b_ada, w_in, w_sb_up, ssm_a_re, ssm_a_im, ssm_log_dt, ssm_b_re, ssm_b_im, ssm_c_re,
           ssm_c_im, ssm_d, w_glu, b_glu, w_ssm_up, w_out, ln1_g, ln1_b, w_ffn_in, w_ffn_out, ln2_g, ln2_b,
           loss_target, m_w_ada, m_b_ada, m_w_in, m_w_sb_up, m_ssm_a_re, m_ssm_a_im, m_ssm_log_dt, m_ssm_b_re,
           m_ssm_b_im, m_ssm_c_re, m_ssm_c_im, m_ssm_d, m_w_glu, m_b_glu, m_w_ssm_up, m_w_out, m_ln1_g, m_ln1_b,
           m_w_ffn_in, m_w_ffn_out, m_ln2_g, m_ln2_b, v_w_ada, v_b_ada, v_w_in, v_w_sb_up, v_ssm_a_re, v_ssm_a_im,
           v_ssm_log_dt, v_ssm_b_re, v_ssm_b_im, v_ssm_c_re, v_ssm_c_im, v_ssm_d, v_w_glu, v_b_glu, v_w_ssm_up,
           v_w_out, v_ln1_g, v_ln1_b, v_w_ffn_in, v_w_ffn_out, v_ln2_g, v_ln2_b):
    given = locals()
    return _train_step({n: given[n] for n in ARG_NAMES})


def _train_step(p):
    x0 = p["x"][0]
    target = p["loss_target"][0]
    seq, d = x0.shape
    depth = p["w_ada"].shape[0]
    n_ada = p["w_ada"].shape[2]
    n_in = p["w_in"].shape[2]
    sb_w = p["w_sb_up"].shape[1]
    ssm_w = p["w_ssm_up"].shape[1]
    n_up = p["w_sb_up"].shape[2]
    n_ffn = p["w_ffn_in"].shape[2]
    ffn = N_DEV * p["w_ffn_out"].shape[1]
    in_cols = N_DEV * n_in
    alpha = (2 * depth) ** 0.25
    resid_ln, resid_ln_mod = _make_resid_fns(alpha)
    t_r = min(512, seq)
    n_r = seq // t_r
    t_m = min(1024, seq)
    n_m = seq // t_m
    t_d = _tile(d)
    assert n_ffn * (N_DEV // 2) == ffn and sb_w % LANES == 0 and ssm_w % LANES == 0 and d % LANES == 0
    assert n_in % LANES == 0 and n_up % LANES == 0 and seq % t_m == 0 and in_cols == 3 * sb_w + ssm_w + 2 * d
    assert (3 * sb_w) % ssm_w == 0 and (3 * sb_w + ssm_w) % (2 * d) == 0
    assert sb_w % n_in == 0 and ssm_w % n_in == 0 and d % n_in == 0 and seq % (SB_BLOCK * SB_GROUP) == 0
    proj_starts = [c // n_in for c in (0, sb_w, 2 * sb_w, 3 * sb_w, 3 * sb_w + ssm_w)]

    bf = lambda a: a.astype(BF16)
    got = _exchange("gather_first", [], [bf(p["w_in"][0]), p["c"]])
    wg_in = [got[0]] + [None] * (depth - 1)
    c_all = got[1].reshape(N_DEV, d)
    small_names = ("w_sb_up", "w_ssm_up", "w_glu", "w_out")
    wg_ffn_in, wg_ffn_out, wg = [None] * depth, [None] * depth, {}

    c_pad = jnp.pad(c_all, ((0, 2 * SUBLANES - N_DEV), (0, 0)))
    c_act = _rowwise("silu_c", lambda v: v * jax.nn.sigmoid(v), [(c_pad, c_pad.shape, lambda i: (0, 0))],
                     [(c_pad.shape, F32, c_pad.shape, lambda i: (0, 0))], (1,))[0]
    rows_c = c_pad.shape[0]
    mod_cols = [
        _mm(f"mod_{l}", c_act, p["w_ada"],
            _spec((rows_c, d), lambda i, j, k: (0, 0)), _spec((None, d, n_ada), lambda i, j, k, l=l: (l, 0, 0)),
            _spec((rows_c, n_ada), lambda i, j, k: (0, 0)), (rows_c, n_ada), F32, (1, 1, 1), NN)
        for l in range(depth)]
    mod_send = jnp.stack([m[:N_DEV] for m in mod_cols], axis=1)
    mod_recv = _exchange("exchange_mod", [mod_send], [])[0]
    mod_nobias = jnp.swapaxes(mod_recv, 0, 1).reshape(depth, N_DEV * n_ada)
    full2 = lambda a: (a, a.shape, lambda i: (0, 0))
    mod = _rowwise("mod_bias", lambda a, b: a + b, [full2(mod_nobias), full2(p["b_ada"])],
                   [(mod_nobias.shape, F32, mod_nobias.shape, lambda i: (0, 0))], (1,))[0]
    vec = lambda a: a.reshape(1, -1)
    mods = [[vec(mod[l, j * d:(j + 1) * d]) for j in range(6)] for l in range(depth)]
    ln = {n: [vec(p[n][l]) for l in range(depth)] for n in ("ln1_g", "ln1_b", "ln2_g", "ln2_b")}

    row_spec = lambda width: ((t_r, width), lambda i: (i, 0))
    col_spec = lambda width, cb: ((t_r, width), lambda i, cb=cb: (i, cb))
    vec_spec = lambda width: ((1, width), lambda i: (0, 0))
    rows_in = lambda a: (a,) + row_spec(a.shape[1])
    vec_in = lambda a: (a,) + vec_spec(a.shape[1])
    row_out = lambda width, dt: ((seq, width), dt) + row_spec(width)

    s5 = [_s5_discretize(*[p[n][l] for n in ("ssm_a_re", "ssm_a_im", "ssm_log_dt", "ssm_b_re", "ssm_b_im",
                                               "ssm_c_re", "ssm_c_im")]) for l in range(depth)]
    s5_b16 = [(bs.astype(BF16), cs.astype(BF16), lam) for bs, cs, lam in s5]
    t_scan = min(1024, seq)
    s5_pw = [_s5_powers(p["ssm_a_re"][l], p["ssm_a_im"][l], p["ssm_log_dt"][l], t_scan // SUBLANES)
             for l in range(depth)]
    u_col = 3 * sb_w // LANES
    gates_cb = (3 * sb_w + ssm_w) // (2 * d)
    ssm_d = [vec(p["ssm_d"][l]) for l in range(depth)]
    b_glu = [vec(p["b_glu"][l]) for l in range(depth)]
    n_half = N_DEV // 2

    h = _rowwise("modulate_in", _modulate, [rows_in(x0), vec_in(mods[0][1]), vec_in(mods[0][0])],
                 [row_out(d, BF16)], (n_r,))[0]
    saved = []
    x_cur = x0
    for l in range(depth):
        sv = {"x_in": x_cur, "h": h}
        last = l == depth - 1
        t_n = _tile(n_in)
        r_n = n_in // t_n
        proj = _mm(f"proj_{l}", h, wg_in[l],
                   _spec((t_m, d), lambda i, j, k: (i, 0)),
                   _spec((None, d, t_n), lambda i, j, k, r=r_n: (j // r, 0, j % r)),
                   _spec((t_m, t_n), lambda i, j, k: (i, j)), (seq, in_cols), F32, (n_m, N_DEV * r_n, 1), NN,
                   reread=(True, True))
        arriving = [bf(p["w_ffn_in"][l]), bf(p["w_ffn_out"][l])] + ([bf(p[n]) for n in small_names] if l == 0 else [])
        (o_sb, o_sb32), got = _sb_attention_fwd(proj, sb_w, beside=_Exchange(gather=arriving))
        wg_ffn_in[l] = got[0]
        wg_ffn_out[l] = got[1].reshape(n_half, n_ffn, d)
        if l == 0:
            wg = dict(zip(small_names, got[2:]))
            for n in ("w_glu", "w_out"):
                wg[n] = jnp.swapaxes(wg[n], 0, 1).reshape(depth, -1, wg[n].shape[-1])
            for n in ("w_sb_up", "w_ssm_up"):
                wg[n] = jnp.transpose(wg[n], (1, 2, 0, 3)).reshape(depth, wg[n].shape[2], d)
        bs16, cs16, lam = s5_b16[l]
        (yc, states), got = _s5_scan_fwd(proj, u_col, bs16, cs16, lam, s5_pw[l][0], t_scan,
                                         beside=None if last else _Exchange(gather=[bf(p["w_in"][l + 1])]))
        if not last:
            wg_in[l + 1] = got[0]
        s5_out = _s5_head(f"s5_head_{l}", yc, proj, 3 * sb_w // ssm_w, ssm_d[l], b_glu[l], wg["w_glu"], l, t_r)

        merged, y_sb, y_ssm = _up_merge(f"up_merge_{l}", o_sb, s5_out, proj, gates_cb, wg["w_sb_up"], wg["w_ssm_up"],
                                        l, t_r)
        y_mix = _mm(f"out_proj_{l}", merged, wg["w_out"],
                    _spec((t_m, d), lambda i, j, k: (i, 0)), _spec((None, d, t_d), lambda i, j, k, l=l: (l, 0, j)),
                    _spec((t_m, t_d), lambda i, j, k: (i, j)), (seq, d), F32, (n_m, d // t_d, 1), NN)
        vecs_a = [mods[l][2], ln["ln1_g"][l], ln["ln1_b"][l], mods[l][4], mods[l][3]]
        x_mid, h2 = _rowwise(f"resid_mix_{l}", resid_ln_mod, [rows_in(x_cur), rows_in(y_mix)] + [vec_in(v) for v in vecs_a],
                             [row_out(d, F32), row_out(d, BF16)], (n_r,))
        a_ffn, f_act = _ffn_in_swiglu(f"ffn_in_{l}", h2, wg_ffn_in[l], t_r)
        y_ffn = _mm(f"ffn_out_{l}", f_act, wg_ffn_out[l],
                    _spec((None, t_m, n_ffn), lambda i, j, k: (k, i, 0)),
                    _spec((None, n_ffn, t_d), lambda i, j, k: (k, 0, j)),
                    _spec((t_m, t_d), lambda i, j, k: (i, j)), (seq, d), F32, (n_m, d // t_d, n_half), NN)
        vecs_b = [mods[l][5], ln["ln2_g"][l], ln["ln2_b"][l]] + ([] if last else [mods[l + 1][1], mods[l + 1][0]])
        outs_b = [row_out(d, F32)] + ([] if last else [row_out(d, BF16)])
        res = _rowwise(f"resid_ffn_{l}", resid_ln if last else resid_ln_mod,
                       [rows_in(x_mid), rows_in(y_ffn)] + [vec_in(v) for v in vecs_b], outs_b, (n_r,))
        sv.update(proj=proj, o_sb=o_sb, o_sb32=o_sb32, yc=yc, states=states, s5_out=s5_out,
                  y_sb=y_sb, y_ssm=y_ssm, merged=merged, y_mix=y_mix, x_mid=x_mid, h2=h2, a_ffn=a_ffn, f_act=f_act,
                  y_ffn=y_ffn, vecs_a=vecs_a, vecs_b=vecs_b)
        saved.append(sv)
        x_cur = res[0]
        h = None if last else res[1]

    loss_part, d_x = _loss_head(x_cur, target, t_r)
    loss = lax.psum(loss_part[0, 0], MESH_AXES)

    d_h_next = None
    grads = {n: [None] * depth for n in WEIGHTS}
    d_mod = [[None] * 6 for _ in range(depth)]
    land = {}
    waiting = []
    row_wrt = lambda i, width, dt: (i, "row", (seq, width), dt) + row_spec(width)
    sum_wrt = lambda i, width: (i, "sum", (1, width), F32) + vec_spec(width)
    for l in reversed(range(depth)):
        sv = saved[l]
        last = l == depth - 1
        ins_b = [rows_in(sv["x_mid"]), rows_in(sv["y_ffn"])] + [vec_in(v) for v in sv["vecs_b"]]
        cts_b = [rows_in(d_x)] + ([] if last else [rows_in(d_h_next)])
        wrt_b = [row_wrt(0, d, F32), row_wrt(1, d, BF16)] + [sum_wrt(2 + j, d) for j in range(len(sv["vecs_b"]))]
        res = _rowwise_vjp(f"resid_ffn_bwd_{l}", resid_ln if last else resid_ln_mod, ins_b, cts_b, wrt_b, (n_r,))
        d_x_mid, d_y_ffn = res[0], res[1]
        d_mod[l][5], grads["ln2_g"][l], grads["ln2_b"][l] = res[2], res[3], res[4]
        if not last:
            d_mod[l + 1][1], d_mod[l + 1][0] = res[5], res[6]
        d_a = _ffn_out_dx_swiglu(f"ffn_out_dx_{l}", d_y_ffn, wg_ffn_out[l], sv["a_ffn"], t_r).reshape(N_DEV, seq, n_ffn)
        g_ffn_out = _mm(f"ffn_out_dw_{l}", sv["f_act"], d_y_ffn,
                        _spec((None, t_m, n_ffn), lambda i, j, k: (i, k, 0)), _spec((t_m, t_d), lambda i, j, k: (k, j)),
                        _spec((None, n_ffn, t_d), lambda i, j, k: (i, 0, j)), (n_half, n_ffn, d), GRAD_WIRE,
                        (n_half, d // t_d, n_m), TN, reread=(False, True))
        d_h2 = _mm(f"ffn_in_dx_{l}", d_a, wg_ffn_in[l],
                   _spec((None, t_m, n_ffn), lambda i, j, k: (k, i, 0)),
                   _spec((None, t_d, n_ffn), lambda i, j, k: (k, j, 0)),
                   _spec((t_m, t_d), lambda i, j, k: (i, j)), (seq, d), BRANCH_CT, (n_m, d // t_d, N_DEV), NT)
        g_ffn_in = _mm(f"ffn_in_dw_{l}", sv["h2"], d_a,
                       _spec((t_m, t_d), lambda i, j, k: (k, j)), _spec((None, t_m, n_ffn), lambda i, j, k: (i, k, 0)),
                       _spec((None, t_d, n_ffn), lambda i, j, k: (i, j, 0)), (N_DEV, d, n_ffn), GRAD_WIRE,
                       (N_DEV, d // t_d, n_m), TN, reread=(True, False))
        ins_a = [rows_in(sv["x_in"]), rows_in(sv["y_mix"])] + [vec_in(v) for v in sv["vecs_a"]]
        wrt_a = [row_wrt(0, d, F32), row_wrt(1, d, BF16)] + [sum_wrt(2 + j, d) for j in range(5)]
        res = _rowwise_vjp(f"resid_mix_bwd_{l}", resid_ln_mod, ins_a, [rows_in(d_x_mid), rows_in(d_h2)], wrt_a, (n_r,))
        d_x_in, d_y_mix = res[0], res[1]
        d_mod[l][2], grads["ln1_g"][l], grads["ln1_b"][l], d_mod[l][4], d_mod[l][3] = res[2:7]
        d_merged = _mm(f"out_proj_dx_{l}", d_y_mix, wg["w_out"],
                       _spec((t_m, d), lambda i, j, k: (i, 0)), _spec((None, t_d, d), lambda i, j, k, l=l: (l, j, 0)),
                       _spec((t_m, t_d), lambda i, j, k: (i, j)), (seq, d), BRANCH_CT, (n_m, d // t_d, 1), NT)
        g_out = _mm(f"out_proj_dw_{l}", sv["merged"], d_y_mix,
                    _spec((t_m, t_d), lambda i, j, k: (k, i)), _spec((t_m, t_d), lambda i, j, k: (k, j)),
                    _spec((t_d, t_d), lambda i, j, k: (i, j)), (d, d), GRAD_WIRE, (d // t_d, d // t_d, n_m), TN, reread=(d > t_d, d > t_d))
        gates = (sv["proj"],) + col_spec(2 * d, gates_cb)
        d_y_sb, d_y_ssm, d_gates = _rowwise_vjp(
            f"merge_bwd_{l}", _merge_fn, [rows_in(sv["y_sb"]), rows_in(sv["y_ssm"]), gates], [rows_in(d_merged)],
            [row_wrt(0, d, BF16), row_wrt(1, d, BF16), row_wrt(2, 2 * d, BF16)], (n_r,))

        def up_bwd(name, act, d_y, w, dx_dtype, l=l):
            k_w = act.shape[1]
            dx = _mm(name + "_dx", d_y, w, _spec((t_m, d), lambda i, j, k: (i, 0)),
                     _spec((None, k_w, d), lambda i, j, k: (l, 0, 0)),
                     _spec((t_m, k_w), lambda i, j, k: (i, 0)), (seq, k_w), dx_dtype, (n_m, 1, 1), NT)
            dw = _mm(name + "_dw", act, d_y, _spec((t_m, k_w), lambda i, j, k: (k, 0)),
                     _spec((t_m, t_d), lambda i, j, k: (k, j)),
                     _spec((k_w, t_d), lambda i, j, k: (0, j)), (k_w, d), GRAD_WIRE, (1, d // t_d, n_m), TN,
                     reread=(d > t_d, False))
            return dx, jnp.swapaxes(dw.reshape(k_w, N_DEV, n_up), 0, 1)

        d_o_sb, g_sb_up = up_bwd(f"sb_up_{l}", sv["o_sb"], d_y_sb, wg["w_sb_up"], BF16)
        d_s5_out, g_ssm_up = up_bwd(f"ssm_up_{l}", sv["s5_out"], d_y_ssm, wg["w_ssm_up"], BRANCH_CT)
        waiting += [("w_ffn_in", g_ffn_in), ("w_ffn_out", g_ffn_out.reshape(N_DEV, -1, d)),
                    ("w_out", g_out.reshape(N_DEV, -1, d)), ("w_sb_up", g_sb_up), ("w_ssm_up", g_ssm_up)]
        levels = [l + 1] * (len(waiting) - 5) + [l] * 5
        (d_q, d_k, d_v), got = _sb_attention_bwd(
            sv["proj"], sv["o_sb32"], d_o_sb, sb_w,
            beside=_Exchange(layered=[(g, lv, depth, land.get(n)) for (n, g), lv in zip(waiting, levels)]))
        land.update({n: buf for (n, _), buf in zip(waiting, got)})
        d_yc, d_u_skip, g_glu, grads["ssm_d"][l], grads["b_glu"][l] = _s5_head_bwd(
            f"s5_head_bwd_{l}", sv["yc"], sv["proj"], 3 * sb_w // ssm_w, d_s5_out, ssm_d[l], b_glu[l], wg["w_glu"], l, t_r)
        bs16, cs16, lam = s5_b16[l]
        d_u, d_bs, d_cs, d_lam = _s5_scan_bwd(sv["proj"], u_col, sv["states"], d_yc, d_u_skip, bs16, cs16, lam,
                                              s5_pw[l][1], t_scan)
        raw = [p[n][l] for n in ("ssm_a_re", "ssm_a_im", "ssm_log_dt", "ssm_b_re", "ssm_b_im", "ssm_c_re", "ssm_c_im")]
        _, pull = jax.vjp(_s5_discretize, *raw)
        (grads["ssm_a_re"][l], grads["ssm_a_im"][l], grads["ssm_log_dt"][l], grads["ssm_b_re"][l],
         grads["ssm_b_im"][l], grads["ssm_c_re"][l], grads["ssm_c_im"][l]) = pull((d_bs, d_cs, d_lam))
        d_proj = [d_q, d_k, d_v, d_u, d_gates]
        g_in = _mm_pieces(f"proj_dw_{l}", d_proj, proj_starts, n_in, lambda i, j, k: i, sv["h"],
                          _spec((t_m, t_d), lambda i, j, k: (k, j)), False, lambda i, j, k: k,
                          _spec((None, t_d, n_in), lambda i, j, k: (i, j, 0)), (N_DEV, d, n_in), GRAD_WIRE,
                          (N_DEV, d // t_d, n_m), TN)
        waiting = [("w_in", g_in), ("w_glu", g_glu.reshape(N_DEV, -1, ssm_w))]
        closing = _Exchange(layered=[(g, 0, depth, land.get(n)) for n, g in waiting]) if l == 0 else None
        d_h = _mm_pieces(f"proj_dx_{l}", d_proj, proj_starts, n_in, lambda i, j, k: k, wg_in[l],
                         _spec((None, t_d, n_in), lambda i, j, k: (k, j, 0)), True, lambda i, j, k: i,
                         _spec((t_m, t_d), lambda i, j, k: (i, j)), (seq, d), BRANCH_CT, (n_m, d // t_d, N_DEV), NT,
                         beside=closing)
        if l == 0:
            d_h, got = d_h
            land.update({n: buf for (n, _), buf in zip(waiting, got)})
        d_x, d_h_next = d_x_in, d_h
    res = _rowwise_vjp("modulate_in_bwd", lambda v, sc, sh: (v, _modulate(v, sc, sh)),
                       [rows_in(x0), vec_in(mods[0][1]), vec_in(mods[0][0])], [rows_in(d_x), rows_in(d_h_next)],
                       [row_wrt(0, d, F32), sum_wrt(1, d), sum_wrt(2, d)], (n_r,))
    grad_x, d_mod[0][1], d_mod[0][0] = res

    d_mod_rows = jnp.concatenate([jnp.concatenate(d_mod[l], axis=1) for l in range(depth)], axis=0)
    grads["b_ada"] = [d_mod_rows[l] for l in range(depth)]
    small_local = [jnp.stack([g.reshape(p[n].shape[1:]) for g in grads[n]]) for n in SMALL_PARAMS]
    d_mod_send = jnp.swapaxes(d_mod_rows.reshape(depth, N_DEV, n_ada), 0, 1)
    small_sum, (d_mod_cols,) = _reduce_packed("exchange_last", _pack(small_local), [d_mod_send])
    d_mod_pad = jnp.pad(jnp.swapaxes(d_mod_cols, 0, 1), ((0, 0), (0, rows_c - N_DEV), (0, 0)))
    g_ada = [
        _mm(f"mod_dw_{l}", c_act, d_mod_pad,
            _spec((rows_c, d), lambda i, j, k: (0, 0)), _spec((None, rows_c, n_ada), lambda i, j, k, l=l: (l, 0, 0)),
            _spec((d, n_ada), lambda i, j, k: (0, 0)), (d, n_ada), F32, (1, 1, 1), TN)
        for l in range(depth)]

    out = {}

    def update(name, partials):
        shape = p[name].shape
        two_d = lambda a: a.reshape(-1, shape[-1])
        res = _adamw("adamw_" + name, two_d(p[name]), two_d(p["m_" + name]), two_d(p["v_" + name]),
                     partials.reshape(partials.shape[0], -1, shape[-1]))
        out[name] = [r.reshape(shape) for r in res]

    update("w_ada", jnp.stack(g_ada)[None])
    for n in ("w_in", "w_sb_up", "w_ssm_up", "w_ffn_in", "w_glu", "w_out", "w_ffn_out"):
        update(n, land[n])
    small_w = [p[n] for n in SMALL_PARAMS]
    res = _adamw("adamw_small", _pack(small_w), _pack([p["m_" + n] for n in SMALL_PARAMS]),
                 _pack([p["v_" + n] for n in SMALL_PARAMS]), small_sum[None])
    for kind, packed in enumerate(res):
        for n, a in zip(SMALL_PARAMS, _unpack(packed, small_w)):
            out.setdefault(n, [None] * 4)[kind] = a

    return ((loss, grad_x[None]) + tuple(out[n][0] for n in WEIGHTS) + tuple(out[n][1] for n in WEIGHTS)
            + tuple(out[n][2] for n in WEIGHTS) + tuple(out[n][3] for n in WEIGHTS))
```

```python
import jax
import jax.numpy as jnp
from jax import lax
from jax.experimental import pallas as pl
from jax.experimental.pallas import tpu as pltpu

F32 = jnp.float32
BF16 = jnp.bfloat16
GRAD_WIRE = BF16
FFN_ACT = BF16
BRANCH_CT = BF16

N_DEV = 8
LANES = 128
SUBLANES = 8
VMEM_BYTES = 64 * 1024 * 1024
HEAD_DIM = 64
SB_BLOCK = 256
SB_GROUP = 2
SLAB_GROUPS = 8
LN_EPS = 1e-5
ADAM_LR, ADAM_B1, ADAM_B2, ADAM_EPS, ADAM_WD, ADAM_STEP = 0.001, 0.9, 0.999, 1e-08, 0.01, 10
SB_UNDERFLOW = -120.0

PACK_ROWS = 256
MESH_AXES = ("x", "y", "c")


def _vmem_limit(block_bytes):
    return int(min(max(3 * block_bytes + (8 << 20), 24 << 20), VMEM_BYTES - (8 << 20)))


def _nbytes(shape, dtype):
    n = 1
    for d in shape:
        if d is not None:
            n *= d
    return n * jnp.dtype(dtype).itemsize


def _spec(shape, fn):
    return pl.BlockSpec(shape, fn)


class _Exchange:
    def __init__(self, scatter=(), gather=(), layered=()):
        self.arrs = list(scatter) + [a for a, _, _, _ in layered] + list(gather)
        self.n = len(self.arrs)
        self.n_sc = len(scatter) + len(layered)
        self.layer = [None] * len(scatter) + [l for _, l, _, _ in layered] + [None] * len(gather)
        self.shapes = ([a.shape for a in scatter] + [(N_DEV, dp) + a.shape[1:] for a, _, dp, _ in layered]
                       + [(N_DEV,) + a.shape for a in gather])
        self.held = [(len(scatter) + i, b) for i, (_, _, _, b) in enumerate(layered) if b is not None]
        self.operands = self.arrs + [b for _, b in self.held]
        hbm = pl.BlockSpec(memory_space=pltpu.HBM)
        self.in_specs = [hbm] * len(self.operands)
        self.out_specs = [hbm] * self.n
        self.out_shape = [jax.ShapeDtypeStruct(s, a.dtype) for s, a in zip(self.shapes, self.arrs)]
        self.scratch = [pltpu.SemaphoreType.DMA((self.n, N_DEV - 1)), pltpu.SemaphoreType.DMA((self.n, N_DEV - 1)),
                        pltpu.SemaphoreType.DMA((self.n,))]

    def aliases(self, first_in, first_out):
        return {first_in + self.n + i: first_out + a for i, (a, _) in enumerate(self.held)}

    def copies(self, ins, outs, sems):
        send_sems, recv_sems, own_sems = sems
        x, y, c = lax.axis_index("x"), lax.axis_index("y"), lax.axis_index("c")
        me = 4 * x + 2 * y + c
        landing = [outs[a].at[me] if self.layer[a] is None else outs[a].at[me, self.layer[a]] for a in range(self.n)]
        out = [pltpu.make_async_copy(ins[a].at[me] if a < self.n_sc else ins[a], landing[a], own_sems.at[a])
               for a in range(self.n)]
        for k in range(1, N_DEV):
            px = 1 - x if k & 4 else x
            py = 1 - y if k & 2 else y
            pc = 1 - c if k & 1 else c
            peer = 4 * px + 2 * py + pc
            for a in range(self.n):
                out.append(pltpu.make_async_remote_copy(
                    src_ref=ins[a].at[peer] if a < self.n_sc else ins[a], dst_ref=landing[a],
                    send_sem=send_sems.at[a, k - 1], recv_sem=recv_sems.at[a, k - 1],
                    device_id=(px, py, pc), device_id_type=pl.DeviceIdType.MESH))
        return out


def _exchange(name, scatter, gather, layered=()):
    ex = _Exchange(scatter, gather, layered)

    def body(*refs):
        copies = ex.copies(refs[:ex.n], refs[len(ex.operands):len(ex.operands) + ex.n], refs[-3:])
        for cp in copies:
            cp.start()
        for cp in copies:
            cp.wait()

    return pl.pallas_call(body, name=name, in_specs=ex.in_specs, out_specs=ex.out_specs, out_shape=ex.out_shape,
                          input_output_aliases=ex.aliases(0, 0), scratch_shapes=ex.scratch)(*ex.operands)


def _reduce_packed(name, packed, scatter):
    rows = packed.shape[0]
    blk = rows // N_DEV
    ex = _Exchange(scatter=[packed.reshape(N_DEV, blk, LANES)] + list(scatter))
    n_in = len(ex.operands)

    def body(*refs):
        ins, outs = refs[:ex.n], refs[n_in:n_in + ex.n]
        total_ref = refs[n_in + ex.n]
        sems, (send2, recv2, own2, load_sem) = refs[n_in + ex.n + 1:n_in + ex.n + 4], refs[n_in + ex.n + 4:-2]
        land_v, sum_v = refs[-2:]
        copies = ex.copies(ins, outs, sems)
        for cp in copies:
            cp.start()
        for cp in copies:
            cp.wait()
        load = pltpu.make_async_copy(outs[0], land_v, load_sem)
        load.start()
        load.wait()
        acc = land_v[0]
        for i in range(1, N_DEV):
            acc = acc + land_v[i]
        sum_v[...] = acc
        x, y, c = lax.axis_index("x"), lax.axis_index("y"), lax.axis_index("c")
        me = 4 * x + 2 * y + c
        back = [pltpu.make_async_copy(sum_v, total_ref.at[me], own2)]
        for k in range(1, N_DEV):
            peer = (1 - x if k & 4 else x, 1 - y if k & 2 else y, 1 - c if k & 1 else c)
            back.append(pltpu.make_async_remote_copy(
                src_ref=sum_v, dst_ref=total_ref.at[me], send_sem=send2.at[k - 1], recv_sem=recv2.at[k - 1],
                device_id=peer, device_id_type=pl.DeviceIdType.MESH))
        for cp in back:
            cp.start()
        for cp in back:
            cp.wait()

    hbm = pl.BlockSpec(memory_space=pltpu.HBM)
    res = pl.pallas_call(
        body, name=name, in_specs=ex.in_specs, out_specs=ex.out_specs + [hbm],
        out_shape=ex.out_shape + [jax.ShapeDtypeStruct((N_DEV, blk, LANES), F32)],
        scratch_shapes=ex.scratch + [pltpu.SemaphoreType.DMA((N_DEV - 1,)), pltpu.SemaphoreType.DMA((N_DEV - 1,)),
                                     pltpu.SemaphoreType.DMA, pltpu.SemaphoreType.DMA,
                                     pltpu.VMEM((N_DEV, blk, LANES), F32), pltpu.VMEM((blk, LANES), F32)],
    )(*ex.operands)
    return res[-1].reshape(rows, LANES), res[1:-1]


def _call_beside(ex, body, name, grid, in_specs, out_specs, out_shape, scratch_shapes, vmem_bytes, operands,
                 semantics, in_hbm=True):
    if in_hbm:
        operands = [_in_hbm(a) for a in operands]
    if ex is None:
        res = pl.pallas_call(
            body, name=name, grid=grid, in_specs=in_specs, out_specs=out_specs, out_shape=out_shape,
            scratch_shapes=scratch_shapes,
            compiler_params=pltpu.CompilerParams(dimension_semantics=semantics, vmem_limit_bytes=vmem_bytes),
        )(*operands)
        return res, None
    n_in, n_out, n_scr = len(in_specs), len(out_specs), len(scratch_shapes)
    n_xin = len(ex.operands)

    def fused(*refs):
        mine = refs[:n_in] + refs[n_in + n_xin:n_in + n_xin + n_out]
        mine += refs[n_in + n_xin + n_out + ex.n:n_in + n_xin + n_out + ex.n + n_scr]
        first = pl.program_id(0) == 0
        last = pl.program_id(0) == grid[0] - 1
        for dim in range(1, len(grid)):
            first = jnp.logical_and(first, pl.program_id(dim) == 0)
            last = jnp.logical_and(last, pl.program_id(dim) == grid[dim] - 1)
        x_ins = refs[n_in:n_in + ex.n]
        x_outs = refs[n_in + n_xin + n_out:n_in + n_xin + n_out + ex.n]

        @pl.when(first)
        def _():
            for cp in ex.copies(x_ins, x_outs, refs[-3:]):
                cp.start()

        body(*mine)

        @pl.when(last)
        def _():
            for cp in ex.copies(x_ins, x_outs, refs[-3:]):
                cp.wait()

    res = pl.pallas_call(
        fused, name=name, grid=grid, in_specs=list(in_specs) + ex.in_specs, out_specs=list(out_specs) + ex.out_specs,
        out_shape=list(out_shape) + ex.out_shape, input_output_aliases=ex.aliases(n_in, n_out),
        scratch_shapes=list(scratch_shapes) + ex.scratch,
        compiler_params=pltpu.CompilerParams(dimension_semantics=("arbitrary",) * len(grid),
                                             vmem_limit_bytes=vmem_bytes),
    )(*operands, *ex.operands)
    return res[:n_out], res[n_out:]


NN = (((1,), (0,)), ((), ()))
NT = (((1,), (1,)), ((), ()))
TN = (((0,), (0,)), ((), ()))


def _in_hbm(a):
    return pltpu.with_memory_space_constraint(a, pltpu.HBM)


def _mm(name, a, b, a_spec, b_spec, o_spec, o_shape, o_dtype, grid, dims, beside=None, reread=(False, True)):
    nk = grid[2]
    a, b = (x if again else _in_hbm(x) for x, again in zip((a, b), reread))
    acc_shape = tuple(d for d in o_spec.block_shape if d is not None)

    def product(a_ref, b_ref):
        return lax.dot_general(a_ref[...].astype(BF16), b_ref[...].astype(BF16), dims, preferred_element_type=F32)

    def body_once(a_ref, b_ref, o_ref):
        o_ref[...] = product(a_ref, b_ref).astype(o_ref.dtype)

    def body(a_ref, b_ref, o_ref, acc_ref):
        k = pl.program_id(2)

        @pl.when(k == 0)
        def _():
            acc_ref[...] = product(a_ref, b_ref)

        @pl.when(k > 0)
        def _():
            acc_ref[...] += product(a_ref, b_ref)

        @pl.when(k == nk - 1)
        def _():
            o_ref[...] = acc_ref[...].astype(o_ref.dtype)

    blk = (_nbytes(a_spec.block_shape, a.dtype) + _nbytes(b_spec.block_shape, b.dtype)
           + _nbytes(acc_shape, o_dtype) + _nbytes(acc_shape, F32))
    res, got = _call_beside(
        beside, body_once if nk == 1 else body, name, grid, [a_spec, b_spec], [o_spec],
        [jax.ShapeDtypeStruct(o_shape, o_dtype)], [] if nk == 1 else [pltpu.VMEM(acc_shape, F32)],
        _vmem_limit(blk), (a, b), ("parallel", "parallel", "arbitrary"), in_hbm=False)
    return res[0] if beside is None else (res[0], got)


def _mm_pieces(name, pieces, starts, width, step_block, other, other_spec, pieces_first, piece_rows, o_spec, o_shape,
               o_dtype, grid, dims, beside=None):
    n_p, nk = len(pieces), grid[2]
    acc_shape = tuple(s for s in o_spec.block_shape if s is not None)

    def which(i, j, k):
        blk = step_block(i, j, k)
        idx = 0
        for s in starts[1:]:
            idx = idx + (blk >= s).astype(jnp.int32)
        return idx, blk

    def piece_spec(p, t_rows):
        def index(i, j, k):
            idx, blk = which(i, j, k)
            mine = idx == p
            return jnp.where(mine, piece_rows(i, j, k), 0), jnp.where(mine, blk - starts[p], 0)
        return _spec((t_rows, width), index)

    def body(*refs):
        p_refs = refs[:n_p] if pieces_first else refs[1:1 + n_p]
        other_ref = refs[n_p] if pieces_first else refs[0]
        o_ref, acc_ref = refs[n_p + 1], refs[n_p + 2]
        i, j, k = pl.program_id(0), pl.program_id(1), pl.program_id(2)

        @pl.when(k == 0)
        def _():
            acc_ref[...] = jnp.zeros_like(acc_ref)

        idx, _ = which(i, j, k)
        for p in range(n_p):
            @pl.when(idx == p)
            def _(p=p):
                mine, fixed = p_refs[p][...].astype(BF16), other_ref[...].astype(BF16)
                pair = (mine, fixed) if pieces_first else (fixed, mine)
                acc_ref[...] += lax.dot_general(pair[0], pair[1], dims, preferred_element_type=F32)

        @pl.when(k == nk - 1)
        def _():
            o_ref[...] = acc_ref[...].astype(o_ref.dtype)

    t_rows = other_spec.block_shape[-2] if not pieces_first else o_spec.block_shape[-2]
    specs = [piece_spec(p, t_rows) for p in range(n_p)]
    in_specs = specs + [other_spec] if pieces_first else [other_spec] + specs
    operands = list(pieces) + [other] if pieces_first else [other] + list(pieces)
    blk = (n_p * 4 * t_rows * width + _nbytes(other_spec.block_shape, other.dtype)
           + _nbytes(acc_shape, o_dtype) + _nbytes(acc_shape, F32))
    res, got = _call_beside(
        beside, body, name, grid, in_specs, [o_spec], [jax.ShapeDtypeStruct(o_shape, o_dtype)],
        [pltpu.VMEM(acc_shape, F32)], _vmem_limit(blk), operands, ("parallel", "parallel", "arbitrary"), in_hbm=False)
    return res[0] if beside is None else (res[0], got)


def _swiglu_fn(gate_up):
    gate, up = gate_up[0], gate_up[1]
    return gate * jax.nn.sigmoid(gate) * up


def _ffn_in_swiglu(name, h, w, t_m):
    seq, d = h.shape
    n_half, n = w.shape[0] // 2, w.shape[2]

    def body(h_ref, wg_ref, wu_ref, a_ref, f_ref):
        hb = h_ref[...]
        a_ref[0] = lax.dot_general(hb, wg_ref[...], NN, preferred_element_type=F32).astype(a_ref.dtype)
        a_ref[1] = lax.dot_general(hb, wu_ref[...], NN, preferred_element_type=F32).astype(a_ref.dtype)
        f_ref[...] = _swiglu_fn(a_ref[...].astype(F32)).astype(f_ref.dtype)

    blk = 2 * t_m * d + 4 * d * n + 6 * t_m * n + 12 * t_m * n
    return pl.pallas_call(
        body, name=name, grid=(seq // t_m, n_half),
        in_specs=[_spec((t_m, d), lambda i, j: (i, 0)), _spec((None, d, n), lambda i, j: (j, 0, 0)),
                  _spec((None, d, n), lambda i, j: (j + n_half, 0, 0))],
        out_specs=[_spec((2, None, t_m, n), lambda i, j: (0, j, i, 0)), _spec((None, t_m, n), lambda i, j: (j, i, 0))],
        out_shape=[jax.ShapeDtypeStruct((2, n_half, seq, n), FFN_ACT), jax.ShapeDtypeStruct((n_half, seq, n), BF16)],
        compiler_params=pltpu.CompilerParams(dimension_semantics=("parallel", "parallel"),
                                             vmem_limit_bytes=_vmem_limit(blk)),
    )(h, w, w)


def _ffn_out_dx_swiglu(name, d_y, w, a, t_m):
    seq, d = d_y.shape
    n_half, n = w.shape[0], w.shape[1]

    def body(dy_ref, w_ref, a_ref, da_ref):
        d_f = lax.dot_general(dy_ref[...], w_ref[...], NT, preferred_element_type=F32)
        gate, up = a_ref[0].astype(F32), a_ref[1].astype(F32)
        s = jax.nn.sigmoid(gate)
        gs = gate * s
        da_ref[0] = (d_f * up * (s + gs * (1.0 - s))).astype(da_ref.dtype)
        da_ref[1] = (d_f * gs).astype(da_ref.dtype)

    blk = 2 * t_m * d + 2 * d * n + 8 * t_m * n + 24 * t_m * n
    return pl.pallas_call(
        body, name=name, grid=(seq // t_m, n_half),
        in_specs=[_spec((t_m, d), lambda i, j: (i, 0)), _spec((None, n, d), lambda i, j: (j, 0, 0)),
                  _spec((2, None, t_m, n), lambda i, j: (0, j, i, 0))],
        out_specs=_spec((2, None, t_m, n), lambda i, j: (0, j, i, 0)),
        out_shape=jax.ShapeDtypeStruct((2, n_half, seq, n), BF16),
        compiler_params=pltpu.CompilerParams(dimension_semantics=("parallel", "parallel"),
                                             vmem_limit_bytes=_vmem_limit(blk)),
    )(d_y, w, a)


def _merge_fn(y_sb, y_ssm, gates):
    half = gates.shape[-1] // 2
    return jax.nn.sigmoid(gates[:, :half]) * y_sb + jax.nn.sigmoid(gates[:, half:]) * y_ssm


def _up_merge(name, o_sb, s5_out, proj, gates_cb, w_sb, w_ssm, layer, t_rows):
    seq = o_sb.shape[0]
    d = w_sb.shape[2]

    def body(o_ref, s_ref, g_ref, w1_ref, w2_ref, m_ref, y1_ref, y2_ref):
        y_sb = lax.dot_general(o_ref[...], w1_ref[...], NN, preferred_element_type=F32)
        y_ssm = lax.dot_general(s_ref[...], w2_ref[...], NN, preferred_element_type=F32)
        m_ref[...] = _merge_fn(y_sb, y_ssm, g_ref[...]).astype(m_ref.dtype)
        y1_ref[...] = y_sb.astype(y1_ref.dtype)
        y2_ref[...] = y_ssm.astype(y2_ref.dtype)

    row = lambda width: _spec((t_rows, width), lambda i: (i, 0))
    whole = lambda w: _spec((None,) + w.shape[1:], lambda i: (layer, 0, 0))
    blk = t_rows * (2 * o_sb.shape[1] + 2 * s5_out.shape[1] + 8 * d + 6 * d + 24 * d) + 4 * d * (o_sb.shape[1] + s5_out.shape[1])
    return pl.pallas_call(
        body, name=name, grid=(seq // t_rows,),
        in_specs=[row(o_sb.shape[1]), row(s5_out.shape[1]), _spec((t_rows, 2 * d), lambda i: (i, gates_cb)),
                  whole(w_sb), whole(w_ssm)],
        out_specs=[row(d)] * 3, out_shape=[jax.ShapeDtypeStruct((seq, d), BF16)] * 3,
        compiler_params=pltpu.CompilerParams(dimension_semantics=("parallel",), vmem_limit_bytes=_vmem_limit(blk)),
    )(_in_hbm(o_sb), _in_hbm(s5_out), _in_hbm(proj), w_sb, w_ssm)


def _tile(n, pref=1024):
    t = pref
    while t >= LANES:
        if n % t == 0:
            return t
        t -= LANES
    return n


def _rowwise(name, fn, ins, outs, grid):
    n_in = len(ins)

    def body(*refs):
        vals = fn(*[r[...].astype(F32) for r in refs[:n_in]])
        if not isinstance(vals, (tuple, list)):
            vals = (vals,)
        for r, v in zip(refs[n_in:], vals):
            r[...] = v.astype(r.dtype)

    blk = sum(_nbytes(bs, a.dtype) for a, bs, _ in ins) + sum(_nbytes(bs, d) + _nbytes(bs, F32) for _, d, bs, _ in outs)
    return pl.pallas_call(
        body, name=name, grid=grid,
        in_specs=[_spec(bs, im) for _, bs, im in ins],
        out_specs=[_spec(bs, im) for _, _, bs, im in outs],
        out_shape=[jax.ShapeDtypeStruct(s, d) for s, d, _, _ in outs],
        compiler_params=pltpu.CompilerParams(dimension_semantics=("parallel",) * len(grid),
                                             vmem_limit_bytes=_vmem_limit(2 * blk)),
    )(*[_in_hbm(a) for a, _, _ in ins])


def _rowwise_vjp(name, fn, ins, cts, wrt, grid):
    n_in, n_ct = len(ins), len(cts)
    idx = [w[0] for w in wrt]

    def body(*refs):
        prim = [r[...].astype(F32) for r in refs[:n_in]]
        ct = tuple(r[...].astype(F32) for r in refs[n_in:n_in + n_ct])
        o_refs = refs[n_in + n_ct:]

        def g(*sel):
            full = list(prim)
            for i, s in zip(idx, sel):
                full[i] = s
            out = fn(*full)
            return tuple(out) if isinstance(out, (tuple, list)) else (out,)

        _, pull = jax.vjp(g, *[prim[i] for i in idx])
        grads = pull(ct)
        first = pl.program_id(0) == 0
        for d in range(1, len(grid)):
            first = jnp.logical_and(first, pl.program_id(d) == 0)
        for w, o_ref, gr in zip(wrt, o_refs, grads):
            if w[1] == "row":
                o_ref[...] = gr.astype(o_ref.dtype)
            else:
                @pl.when(first)
                def _(o_ref=o_ref):
                    o_ref[...] = jnp.zeros_like(o_ref)

                o_ref[...] += gr.astype(o_ref.dtype)

    blk = (sum(_nbytes(bs, a.dtype) + _nbytes(bs, F32) for a, bs, _ in list(ins) + list(cts))
           + sum(_nbytes(w[4], w[3]) + _nbytes(w[4], F32) for w in wrt))
    return pl.pallas_call(
        body, name=name, grid=grid,
        in_specs=[_spec(bs, im) for _, bs, im in list(ins) + list(cts)],
        out_specs=[_spec(w[4], w[5]) for w in wrt],
        out_shape=[jax.ShapeDtypeStruct(w[2], w[3]) for w in wrt],
        compiler_params=pltpu.CompilerParams(dimension_semantics=("arbitrary",) * len(grid),
                                             vmem_limit_bytes=_vmem_limit(2 * blk)),
    )(*[_in_hbm(a) for a, _, _ in list(ins) + list(cts)])


def _normalize(x):
    mu = jnp.mean(x, axis=-1, keepdims=True)
    xc = x - mu
    var = jnp.mean(xc * xc, axis=-1, keepdims=True)
    return xc * lax.rsqrt(var + LN_EPS)


def _modulate(x, sc, sh):
    return _normalize(x) * (1.0 + sc) + sh


def _make_resid_fns(alpha):
    def resid_ln(x, y, gate, g, b):
        return _normalize(alpha * x + (1.0 + gate) * y) * g + b

    def resid_ln_mod(x, y, gate, g, b, sc, sh):
        xn = resid_ln(x, y, gate, g, b)
        return xn, _modulate(xn, sc, sh)

    return resid_ln, resid_ln_mod


def _s5_act_fn(yc, u, d_skip):
    return jax.nn.gelu(yc + d_skip * u)


def _s5_gate_fn(y1, t):
    return y1 * jax.nn.sigmoid(t)


def _s5_head_specs(yc, proj, u_cb, w_glu, layer, t_rows):
    width = yc.shape[1]
    row = _spec((t_rows, width), lambda i: (i, 0))
    u_spec = _spec((t_rows, width), lambda i: (i, u_cb))
    vec = _spec((1, width), lambda i: (0, 0))
    w_spec = _spec((None,) + w_glu.shape[1:], lambda i: (layer, 0, 0))
    return row, u_spec, vec, w_spec


def _s5_head(name, yc, proj, u_cb, d_skip, b_glu, w_glu, layer, t_rows):
    seq, width = yc.shape
    row, u_spec, vec, w_spec = _s5_head_specs(yc, proj, u_cb, w_glu, layer, t_rows)

    def body(yc_ref, u_ref, d_ref, b_ref, w_ref, o_ref):
        y1 = _s5_act_fn(yc_ref[...], u_ref[...], d_ref[...])
        t = lax.dot_general(y1.astype(BF16), w_ref[...], NN, preferred_element_type=F32) + b_ref[...]
        o_ref[...] = _s5_gate_fn(y1, t).astype(o_ref.dtype)

    return pl.pallas_call(
        body, name=name, grid=(seq // t_rows,), in_specs=[row, u_spec, vec, vec, w_spec], out_specs=row,
        out_shape=jax.ShapeDtypeStruct((seq, width), BF16),
        compiler_params=pltpu.CompilerParams(dimension_semantics=("parallel",),
                                             vmem_limit_bytes=_vmem_limit(40 * t_rows * width)),
    )(_in_hbm(yc), _in_hbm(proj), d_skip, b_glu, w_glu)


def _s5_head_bwd(name, yc, proj, u_cb, d_out, d_skip, b_glu, w_glu, layer, t_rows):
    seq, width = yc.shape
    row, u_spec, vec, w_spec = _s5_head_specs(yc, proj, u_cb, w_glu, layer, t_rows)
    n_t = seq // t_rows

    def body(yc_ref, u_ref, do_ref, d_ref, b_ref, w_ref, dyc_ref, du_ref, dw_ref, dd_ref, db_ref, acc_ref):
        i = pl.program_id(0)

        @pl.when(i == 0)
        def _():
            acc_ref[...] = jnp.zeros_like(acc_ref)
            dd_ref[...] = jnp.zeros_like(dd_ref)
            db_ref[...] = jnp.zeros_like(db_ref)

        y1, pull_act = jax.vjp(_s5_act_fn, yc_ref[...], u_ref[...], d_ref[...])
        y1_b = y1.astype(BF16)
        t = lax.dot_general(y1_b, w_ref[...], NN, preferred_element_type=F32) + b_ref[...]
        _, pull_gate = jax.vjp(_s5_gate_fn, y1, t)
        d_y1, d_t = pull_gate(do_ref[...].astype(F32))
        d_t_b = d_t.astype(BF16)
        d_y1 = d_y1 + lax.dot_general(d_t_b, w_ref[...], NT, preferred_element_type=F32)
        acc_ref[...] += lax.dot_general(y1_b, d_t_b, TN, preferred_element_type=F32)
        db_ref[...] += jnp.sum(d_t, axis=0, keepdims=True)
        d_yc, d_u, d_d = pull_act(d_y1)
        dyc_ref[...] = d_yc
        du_ref[...] = d_u
        dd_ref[...] += d_d

        @pl.when(i == n_t - 1)
        def _():
            dw_ref[...] = acc_ref[...].astype(dw_ref.dtype)

    whole = _spec((width, width), lambda i: (0, 0))
    return pl.pallas_call(
        body, name=name, grid=(n_t,), in_specs=[row, u_spec, row, vec, vec, w_spec],
        out_specs=[row, row, whole, vec, vec],
        out_shape=[jax.ShapeDtypeStruct((seq, width), F32), jax.ShapeDtypeStruct((seq, width), F32),
                   jax.ShapeDtypeStruct((width, width), GRAD_WIRE), jax.ShapeDtypeStruct((1, width), F32),
                   jax.ShapeDtypeStruct((1, width), F32)],
        scratch_shapes=[pltpu.VMEM((width, width), F32)],
        compiler_params=pltpu.CompilerParams(dimension_semantics=("arbitrary",),
                                             vmem_limit_bytes=_vmem_limit(80 * t_rows * width)),
    )(_in_hbm(yc), _in_hbm(proj), _in_hbm(d_out), d_skip, b_glu, w_glu)


def _sb_tri(kind):
    row = lax.broadcasted_iota(jnp.int32, (SB_BLOCK, SB_BLOCK), 0)
    col = lax.broadcasted_iota(jnp.int32, (SB_BLOCK, SB_BLOCK), 1)
    if kind == "after":
        return (row > col).astype(BF16)
    if kind == "from":
        return (row >= col).astype(BF16)
    return col < row


def _split_dot(x, m):
    hi = x.astype(BF16)
    lo = (x - hi.astype(F32)).astype(BF16)
    return (lax.dot_general(hi, m, NN, preferred_element_type=F32)
            + lax.dot_general(lo, m, NN, preferred_element_type=F32))


def _sb_scores(qh, k2):
    z = lax.dot_general(qh, k2, NT, preferred_element_type=F32)
    log_beta = jnp.minimum(z, 0.0) - jnp.log(1.0 + jnp.exp(-jnp.abs(z)))
    return log_beta, log_beta - z


def _sb_attention_fwd(proj, sb_width, beside=None):
    seq = proj.shape[0]
    n_pair, n_q = sb_width // LANES, seq // (SB_BLOCK * SB_GROUP)
    scale = 1.0 / (HEAD_DIM ** 0.5)
    chains = [(s, h) for s in range(SB_GROUP) for h in range(2)]

    def body(q_ref, k_ref, v_ref, o_ref, o32_ref):
        first = pl.program_id(1) * SB_GROUP
        lane = lax.broadcasted_iota(jnp.int32, (SB_BLOCK, LANES), 1)
        m_after, causal = _sb_tri("after"), _sb_tri("mask")
        heads = [lane < HEAD_DIM, lane >= HEAD_DIM]
        rows = [pl.ds(s * SB_BLOCK, SB_BLOCK) for s in range(SB_GROUP)]
        qh = {(s, h): (jnp.where(heads[h], q_ref[rows[s], :], 0.0) * scale).astype(BF16) for s, h in chains}

        def key_rows(s, r):
            kb = first + s - r
            return kb >= 0, pl.ds(pl.multiple_of(jnp.maximum(kb, 0) * SB_BLOCK, SB_BLOCK), SB_BLOCK)

        def scores(r, diag):
            out = []
            for s in range(SB_GROUP):
                live, ks = key_rows(s, r)
                k2 = k_ref[ks, :].astype(BF16)
                for h in range(2):
                    log_beta, log_1m = _sb_scores(qh[s, h], k2)
                    if diag:
                        log_1m = jnp.where(causal, log_1m, 0.0)
                    else:
                        log_1m = jnp.where(live, log_1m, 0.0)
                    out += [log_beta + _split_dot(log_1m, m_after), jnp.sum(log_1m, axis=1, keepdims=True)]
            return tuple(out)

        def weigh(r, sc, carry, acc, diag):
            out = []
            for c, (s, h) in enumerate(chains):
                live, ks = key_rows(s, r)
                v2 = v_ref[ks, :].astype(BF16)
                w = jnp.exp(sc[2 * c] + carry[c])
                w = jnp.where(causal, w, 0.0) if diag else jnp.where(live, w, 0.0)
                out.append(acc[c] + lax.dot_general(w.astype(BF16), v2, NN, preferred_element_type=F32))
            return tuple(out)

        zero = jnp.zeros((SB_BLOCK, LANES), F32)
        zcol = jnp.zeros((SB_BLOCK, 1), F32)
        sc = scores(0, True)
        acc = weigh(0, sc, (zcol,) * len(chains), (zero,) * len(chains), True)
        carry = tuple(sc[2 * c + 1] for c in range(len(chains)))
        last = first + SB_GROUP - 1

        def loop(st):
            r, carry, acc = st
            sc = scores(r, False)
            after = tuple(carry[c] + sc[2 * c + 1] for c in range(len(chains)))
            top = jnp.max(after[0])
            for c in range(1, len(chains)):
                top = jnp.maximum(top, jnp.max(after[c]))
            acc = weigh(r, sc, carry, acc, False)
            return jnp.where(top < SB_UNDERFLOW, last + 1, r + 1), after, acc

        _, _, acc = lax.while_loop(lambda st: st[0] <= last, loop, (1, carry, acc))
        for s in range(SB_GROUP):
            out = jnp.where(heads[0], acc[2 * s], acc[2 * s + 1])
            o_ref[rows[s], :] = out.astype(o_ref.dtype)
            o32_ref[rows[s], :] = out

    q_spec = _spec((SB_BLOCK * SB_GROUP, LANES), lambda h, i: (i, h))
    kv = [_spec((seq, LANES), lambda h, i, o=o: (0, o + h)) for o in (n_pair, 2 * n_pair)]
    return _call_beside(
        beside, body, "sb_attention_fwd", (n_pair, n_q), [q_spec] + kv, [q_spec, q_spec],
        [jax.ShapeDtypeStruct((seq, sb_width), BF16), jax.ShapeDtypeStruct((seq, sb_width), F32)], [],
        _vmem_limit(2 * seq * LANES * 4), (proj, proj, proj), ("parallel", "arbitrary"))


def _sb_attention_bwd(proj, o32, do, sb_width, beside=None):
    seq = proj.shape[0]
    n_pair, n_q = sb_width // LANES, seq // SB_BLOCK
    scale = 1.0 / (HEAD_DIM ** 0.5)

    def body(q_ref, k_ref, v_ref, o_ref, do_ref, dq_ref, dk_out_ref, dv_out_ref, dk_ref, dv_ref):
        qi = pl.program_id(1)

        @pl.when(qi == 0)
        def _():
            dk_ref[...] = jnp.zeros_like(dk_ref)
            dv_ref[...] = jnp.zeros_like(dv_ref)

        q2 = q_ref[...]
        do2 = do_ref[...].astype(F32)
        o2 = o_ref[...]
        lane = lax.broadcasted_iota(jnp.int32, (SB_BLOCK, LANES), 1)
        m_after, m_from, causal = _sb_tri("after"), _sb_tri("from"), _sb_tri("mask")
        heads = [lane < HEAD_DIM, lane >= HEAD_DIM]
        qh = [(jnp.where(m, q2, 0.0) * scale).astype(BF16) for m in heads]
        doh = [jnp.where(m, do2, 0.0) for m in heads]
        doh_b = [v.astype(BF16) for v in doh]
        total = [jnp.sum(v * o2, axis=1, keepdims=True) for v in doh]

        def scores(kb, diag):
            ks = pl.multiple_of(kb * SB_BLOCK, SB_BLOCK)
            k2 = k_ref[pl.ds(ks, SB_BLOCK), :].astype(BF16)
            v2 = v_ref[pl.ds(ks, SB_BLOCK), :].astype(BF16)
            out = []
            for h in range(2):
                log_beta, log_1m = _sb_scores(qh[h], k2)
                if diag:
                    log_1m = jnp.where(causal, log_1m, 0.0)
                out += [log_beta + _split_dot(log_1m, m_after), jnp.sum(log_1m, axis=1, keepdims=True),
                        lax.dot_general(doh_b[h], v2, NT, preferred_element_type=F32), log_beta]
            return tuple(out)

        def pull(kb, sc, carry, right, dq, diag):
            ks = pl.multiple_of(kb * SB_BLOCK, SB_BLOCK)
            k2 = k_ref[pl.ds(ks, SB_BLOCK), :].astype(BF16)
            dv_blk, dk_blk, right_out, dq_out = None, None, [], []
            for h in range(2):
                arg, _, d_w, log_beta = sc[4 * h:4 * h + 4]
                w = jnp.exp(arg + carry[h])
                if diag:
                    w = jnp.where(causal, w, 0.0)
                w_b = w.astype(BF16)
                d_arg = d_w * w_b.astype(F32)
                dv_h = lax.dot_general(w_b, doh_b[h], TN, preferred_element_type=F32)
                d_log_1m = total[h] - right[h] - _split_dot(d_arg, m_from)
                beta = jnp.exp(log_beta)
                dz = d_arg * (1.0 - beta) - beta * d_log_1m
                if diag:
                    dz = jnp.where(causal, dz, 0.0)
                dz_b = dz.astype(BF16)
                dk_h = lax.dot_general(dz_b, qh[h], TN, preferred_element_type=F32)
                dv_blk = dv_h if h == 0 else dv_blk + dv_h
                dk_blk = dk_h if h == 0 else dk_blk + dk_h
                dq_out.append(dq[h] + lax.dot_general(dz_b, k2, NN, preferred_element_type=F32))
                right_out.append(right[h] + jnp.sum(d_arg, axis=1, keepdims=True))
            dv_ref[pl.ds(ks, SB_BLOCK), :] += dv_blk
            dk_ref[pl.ds(ks, SB_BLOCK), :] += dk_blk
            return tuple(right_out), tuple(dq_out)

        zero = jnp.zeros((SB_BLOCK, LANES), F32)
        zcol = jnp.zeros((SB_BLOCK, 1), F32)
        sc = scores(qi, True)
        right, dq = pull(qi, sc, (zcol, zcol), (zcol, zcol), (zero, zero), True)
        carry = (sc[1], sc[5])

        def loop(st):
            kb, carry, right, dq = st
            sc = scores(kb, False)
            after = (carry[0] + sc[1], carry[1] + sc[5])
            done = jnp.maximum(jnp.max(after[0]), jnp.max(after[1])) < SB_UNDERFLOW
            right, dq = pull(kb, sc, carry, right, dq, False)
            return jnp.where(done, -1, kb - 1), after, right, dq

        _, _, _, dq = lax.while_loop(lambda st: st[0] >= 0, loop, (qi - 1, carry, right, dq))
        dq_ref[...] = (jnp.where(heads[0], dq[0], dq[1]) * scale).astype(dq_ref.dtype)

        @pl.when(qi == n_q - 1)
        def _():
            dk_out_ref[...] = dk_ref[...].astype(dk_out_ref.dtype)
            dv_out_ref[...] = dv_ref[...].astype(dv_out_ref.dtype)

    q_spec = _spec((SB_BLOCK, LANES), lambda h, i: (i, h))
    kv = [_spec((seq, LANES), lambda h, i, o=o: (0, o + h)) for o in (n_pair, 2 * n_pair)]
    full = _spec((seq, LANES), lambda h, i: (0, h))
    return _call_beside(
        beside, body, "sb_attention_bwd", (n_pair, n_q), [q_spec] + kv + [q_spec, q_spec], [q_spec, full, full],
        [jax.ShapeDtypeStruct((seq, sb_width), BF16)] * 3,
        [pltpu.VMEM((seq, LANES), F32), pltpu.VMEM((seq, LANES), F32)],
        _vmem_limit(4 * seq * LANES * 4), (proj, proj, proj, o32, do), ("parallel", "arbitrary"))


def _s5_discretize(a_re, a_im, log_dt, b_re, b_im, c_re, c_im):
    n_g, n_p = a_re.shape
    c_g = b_re.shape[-1]
    ns = n_g // SLAB_GROUPS
    dt = jnp.exp(log_dt)[:, None]
    xr, xi = a_re * dt, a_im * dt
    mag = jnp.exp(xr)
    lr, li = mag * jnp.cos(xi), mag * jnp.sin(xi)
    den = a_re * a_re + a_im * a_im
    fr = ((lr - 1.0) * a_re + li * a_im) / den
    fi = (li * a_re - (lr - 1.0) * a_im) / den
    bb_re = fr[..., None] * b_re - fi[..., None] * b_im
    bb_im = fr[..., None] * b_im + fi[..., None] * b_re
    eye = jnp.eye(SLAB_GROUPS, dtype=F32)

    def diag_b(m):
        m = jnp.transpose(m.reshape(ns, SLAB_GROUPS, n_p, c_g), (0, 1, 3, 2))
        m = m[:, :, :, None, :] * eye[None, :, None, :, None]
        return m.reshape(ns, SLAB_GROUPS * c_g, SLAB_GROUPS * n_p)

    def diag_c(m):
        m = jnp.transpose(m.reshape(ns, SLAB_GROUPS, c_g, n_p), (0, 1, 3, 2))
        m = m[:, :, :, None, :] * eye[None, :, None, :, None]
        return m.reshape(ns, SLAB_GROUPS * n_p, SLAB_GROUPS * c_g)

    bs = jnp.concatenate([diag_b(bb_re), diag_b(bb_im)], axis=-1)
    cs = jnp.concatenate([diag_c(c_re), -diag_c(c_im)], axis=1)
    lam = jnp.concatenate([lr.reshape(ns, 1, -1), li.reshape(ns, 1, -1)], axis=-1)
    return bs, cs, lam


def _s5_powers(a_re, a_im, log_dt, n):
    n_g, n_p = a_re.shape
    ns = n_g // SLAB_GROUPS
    dt = jnp.exp(log_dt)[:, None]
    mag = jnp.exp(a_re * dt)
    base_r, base_i = mag * jnp.cos(a_im * dt), mag * jnp.sin(a_im * dt)
    steps = jnp.arange(1, n + 1, dtype=jnp.int32)[:, None, None]
    pr, pi = jnp.ones((n, n_g, n_p), F32), jnp.zeros((n, n_g, n_p), F32)
    for b in range(n.bit_length()):
        take = ((steps >> b) & 1) == 1
        pr, pi = (jnp.where(take, pr * base_r - pi * base_i, pr), jnp.where(take, pr * base_i + pi * base_r, pi))
        base_r, base_i = base_r * base_r - base_i * base_i, 2.0 * base_r * base_i

    def slabs(re, im):
        one = lambda m: jnp.transpose(m.reshape(n, ns, SLAB_GROUPS * n_p), (1, 0, 2))
        return jnp.concatenate([one(re), one(im)], axis=-1)

    return slabs(pr, pi), slabs(pr[::-1], -pi[::-1])


def _lanes(j):
    return slice(j * LANES, (j + 1) * LANES)


def _tile8(k):
    return pl.ds(pl.multiple_of(k * SUBLANES, SUBLANES), SUBLANES)


def _s5_interleave(dst_ref, src_ref, t_seg):
    def body(k, _):
        dst_ref[_tile8(k), :] = src_ref[pl.ds(k, SUBLANES, stride=t_seg), :]
        return 0

    lax.fori_loop(0, t_seg, body, 0, unroll=4)


def _s5_join_segments(st_ref, end_ref, car_ref, tab_ref, row, order, n_pair):
    for j in range(n_pair):
        re, im = _lanes(j), _lanes(n_pair + j)
        cr, ci = st_ref[:, re], st_ref[:, im]
        tr, ti = tab_ref[row:row + 1, re], tab_ref[row:row + 1, im]
        for s in order:
            car_ref[s:s + 1, re] = cr
            car_ref[s:s + 1, im] = ci
            er, ei = end_ref[s:s + 1, re], end_ref[s:s + 1, im]
            cr, ci = er + tr * cr - ti * ci, ei + tr * ci + ti * cr
        st_ref[:, re] = cr
        st_ref[:, im] = ci


def _s5_add_carries(buf_ref, car_ref, tab_ref, t_seg, n_pair):
    def fix(k, _):
        rows = _tile8(k)
        tab = tab_ref[pl.ds(k, 1), :]
        for j in range(n_pair):
            re, im = _lanes(j), _lanes(n_pair + j)
            cr, ci = car_ref[:, re], car_ref[:, im]
            tr, ti = tab[:, re], tab[:, im]
            buf_ref[rows, re] += tr * cr - ti * ci
            buf_ref[rows, im] += tr * ci + ti * cr
        return 0

    lax.fori_loop(0, t_seg, fix, 0, unroll=2)


def _s5_scan_fwd(proj, u_col, bs, cs, lam, pw, t_blk, beside=None):
    seq = proj.shape[0]
    ns, _, w2 = bs.shape
    n_pair = w2 // (2 * LANES)
    t_seg, n_t = t_blk // SUBLANES, seq // t_blk

    def body(u_ref, bs_ref, cs_ref, lam_ref, pw_ref, yc_ref, h_ref, st_ref, end_ref, car_ref, ui_ref, bu_ref, yi_ref):
        @pl.when(pl.program_id(1) == 0)
        def _():
            st_ref[...] = jnp.zeros_like(st_ref)

        _s5_interleave(ui_ref, u_ref, t_seg)
        bu_ref[...] = lax.dot_general(ui_ref[...].astype(BF16), bs_ref[...], NN, preferred_element_type=F32)
        lam_r = [jnp.broadcast_to(lam_ref[:, _lanes(j)], (SUBLANES, LANES)) for j in range(n_pair)]
        lam_i = [jnp.broadcast_to(lam_ref[:, _lanes(n_pair + j)], (SUBLANES, LANES)) for j in range(n_pair)]

        def step(k, c):
            rows = _tile8(k)
            out = []
            for j in range(n_pair):
                hr, hi = c[2 * j], c[2 * j + 1]
                nr = lam_r[j] * hr - lam_i[j] * hi + bu_ref[rows, _lanes(j)]
                ni = lam_i[j] * hr + lam_r[j] * hi + bu_ref[rows, _lanes(n_pair + j)]
                h_ref[rows, _lanes(j)] = nr
                h_ref[rows, _lanes(n_pair + j)] = ni
                out += [nr, ni]
            return tuple(out)

        ends = lax.fori_loop(0, t_seg, step, (jnp.zeros((SUBLANES, LANES), F32),) * (2 * n_pair), unroll=4)
        for j in range(n_pair):
            end_ref[:, _lanes(j)] = ends[2 * j]
            end_ref[:, _lanes(n_pair + j)] = ends[2 * j + 1]
        _s5_join_segments(st_ref, end_ref, car_ref, pw_ref, t_seg - 1, list(range(SUBLANES)), n_pair)
        _s5_add_carries(h_ref, car_ref, pw_ref, t_seg, n_pair)
        yi_ref[...] = lax.dot_general(h_ref[...].astype(BF16), cs_ref[...], NN, preferred_element_type=F32)

        def scatter(k, _):
            yc_ref[pl.ds(k, SUBLANES, stride=t_seg), :] = yi_ref[_tile8(k), :]
            return 0

        lax.fori_loop(0, t_seg, scatter, 0, unroll=4)

    return _call_beside(
        beside, body, "s5_scan_fwd", (ns, n_t),
        [_spec((t_blk, LANES), lambda s, i: (i, u_col + s)),
         _spec((None, LANES, w2), lambda s, i: (s, 0, 0)),
         _spec((None, w2, LANES), lambda s, i: (s, 0, 0)),
         _spec((None, 1, w2), lambda s, i: (s, 0, 0)),
         _spec((None, t_seg, w2), lambda s, i: (s, 0, 0))],
        [_spec((t_blk, LANES), lambda s, i: (i, s)),
         _spec((None, t_blk, w2), lambda s, i: (s, i, 0))],
        [jax.ShapeDtypeStruct((seq, ns * LANES), F32), jax.ShapeDtypeStruct((ns, seq, w2), F32)],
        [pltpu.VMEM((1, w2), F32), pltpu.VMEM((SUBLANES, w2), F32), pltpu.VMEM((SUBLANES, w2), F32),
         pltpu.VMEM((t_blk, LANES), F32), pltpu.VMEM((t_blk, w2), F32), pltpu.VMEM((t_blk, LANES), F32)],
        _vmem_limit(3 * t_blk * w2 * 4), (proj, bs, cs, lam, pw), ("parallel", "arbitrary"))


def _s5_scan_bwd(proj, u_col, states, d_yc, du_extra, bs, cs, lam, qw, t_blk):
    seq = proj.shape[0]
    ns, _, w2 = bs.shape
    n_pair = w2 // (2 * LANES)
    t_seg, n_t = t_blk // SUBLANES, seq // t_blk

    def body(u_ref, h_ref, hp_ref, dyc_ref, dux_ref, bs_ref, cs_ref, lam_ref, qw_ref,
             du_ref, dbs_ref, dcs_ref, dlam_ref, g_ref, gd_ref, st_ref, end_ref, car_ref, ui_ref, dyi_ref, dui_ref):
        i = pl.program_id(1)

        @pl.when(i == 0)
        def _():
            st_ref[...] = jnp.zeros_like(st_ref)
            dbs_ref[...] = jnp.zeros_like(dbs_ref)
            dcs_ref[...] = jnp.zeros_like(dcs_ref)
            dlam_ref[...] = jnp.zeros_like(dlam_ref)

        _s5_interleave(ui_ref, u_ref, t_seg)
        _s5_interleave(dyi_ref, dyc_ref, t_seg)
        dyc_b = dyi_ref[...].astype(BF16)
        gd_ref[...] = lax.dot_general(dyc_b, cs_ref[...], NT, preferred_element_type=F32)
        lam_r = [jnp.broadcast_to(lam_ref[:, _lanes(j)], (SUBLANES, LANES)) for j in range(n_pair)]
        lam_i = [jnp.broadcast_to(lam_ref[:, _lanes(n_pair + j)], (SUBLANES, LANES)) for j in range(n_pair)]

        def step(kk, c):
            rows = _tile8(t_seg - 1 - kk)
            out = []
            for j in range(n_pair):
                gr_n, gi_n = c[2 * j], c[2 * j + 1]
                gr = gd_ref[rows, _lanes(j)] + lam_r[j] * gr_n + lam_i[j] * gi_n
                gi = gd_ref[rows, _lanes(n_pair + j)] + lam_r[j] * gi_n - lam_i[j] * gr_n
                g_ref[rows, _lanes(j)] = gr
                g_ref[rows, _lanes(n_pair + j)] = gi
                out += [gr, gi]
            return tuple(out)

        zero = jnp.zeros((SUBLANES, LANES), F32)
        firsts = lax.fori_loop(0, t_seg, step, (zero,) * (2 * n_pair), unroll=4)
        for j in range(n_pair):
            end_ref[:, _lanes(j)] = firsts[2 * j]
            end_ref[:, _lanes(n_pair + j)] = firsts[2 * j + 1]
        _s5_join_segments(st_ref, end_ref, car_ref, qw_ref, 0, list(range(SUBLANES))[::-1], n_pair)
        _s5_add_carries(g_ref, car_ref, qw_ref, t_seg, n_pair)

        def pair_up(k, c):
            rows, prev = _tile8(k), _tile8(k - 1)
            out = []
            for j in range(n_pair):
                re, im = _lanes(j), _lanes(n_pair + j)
                gr, gi, hr, hi = g_ref[rows, re], g_ref[rows, im], h_ref[prev, re], h_ref[prev, im]
                out += [c[2 * j] + gr * hr + gi * hi, c[2 * j + 1] + gi * hr - gr * hi]
            return tuple(out)

        acc = lax.fori_loop(1, t_seg, pair_up, (zero,) * (2 * n_pair), unroll=4)
        has_prev = (i < n_t - 1).astype(F32)
        first_seg = lax.broadcasted_iota(jnp.int32, (SUBLANES, LANES), 0) == 0
        last = _tile8(t_seg - 1)
        for j in range(n_pair):
            re, im = _lanes(j), _lanes(n_pair + j)
            gr, gi = g_ref[0:SUBLANES, re], g_ref[0:SUBLANES, im]
            hr = jnp.where(first_seg, hp_ref[SUBLANES - 1:, re] * has_prev, pltpu.roll(h_ref[last, re], 1, 0))
            hi = jnp.where(first_seg, hp_ref[SUBLANES - 1:, im] * has_prev, pltpu.roll(h_ref[last, im], 1, 0))
            dlam_ref[:, re] += jnp.sum(acc[2 * j] + gr * hr + gi * hi, axis=0, keepdims=True)
            dlam_ref[:, im] += jnp.sum(acc[2 * j + 1] + gi * hr - gr * hi, axis=0, keepdims=True)

        g_b = g_ref[...].astype(BF16)
        dui_ref[...] = lax.dot_general(g_b, bs_ref[...], NT, preferred_element_type=F32)
        dbs_ref[...] += lax.dot_general(ui_ref[...].astype(BF16), g_b, TN, preferred_element_type=F32)
        dcs_ref[...] += lax.dot_general(h_ref[...].astype(BF16), dyc_b, TN, preferred_element_type=F32)

        def scatter(k, _):
            rows = pl.ds(k, SUBLANES, stride=t_seg)
            du_ref[rows, :] = (dui_ref[_tile8(k), :] + dux_ref[rows, :]).astype(du_ref.dtype)
            return 0

        lax.fori_loop(0, t_seg, scatter, 0, unroll=4)

    rev = lambda i: n_t - 1 - i
    return pl.pallas_call(
        body, name="s5_scan_bwd", grid=(ns, n_t),
        in_specs=[_spec((t_blk, LANES), lambda s, i: (rev(i), u_col + s)),
                  _spec((None, t_blk, w2), lambda s, i: (s, rev(i), 0)),
                  _spec((None, SUBLANES, w2), lambda s, i: (s, jnp.maximum(rev(i) * t_seg - 1, 0), 0)),
                  _spec((t_blk, LANES), lambda s, i: (rev(i), s)),
                  _spec((t_blk, LANES), lambda s, i: (rev(i), s)),
                  _spec((None, LANES, w2), lambda s, i: (s, 0, 0)),
                  _spec((None, w2, LANES), lambda s, i: (s, 0, 0)),
                  _spec((None, 1, w2), lambda s, i: (s, 0, 0)),
                  _spec((None, t_seg, w2), lambda s, i: (s, 0, 0))],
        out_specs=[_spec((t_blk, LANES), lambda s, i: (rev(i), s)),
                   _spec((None, LANES, w2), lambda s, i: (s, 0, 0)),
                   _spec((None, w2, LANES), lambda s, i: (s, 0, 0)),
                   _spec((None, 1, w2), lambda s, i: (s, 0, 0))],
        out_shape=[jax.ShapeDtypeStruct((seq, ns * LANES), F32), jax.ShapeDtypeStruct(bs.shape, F32),
                   jax.ShapeDtypeStruct(cs.shape, F32), jax.ShapeDtypeStruct(lam.shape, F32)],
        scratch_shapes=[pltpu.VMEM((t_blk, w2), F32), pltpu.VMEM((t_blk, w2), F32), pltpu.VMEM((1, w2), F32),
                        pltpu.VMEM((SUBLANES, w2), F32), pltpu.VMEM((SUBLANES, w2), F32),
                        pltpu.VMEM((t_blk, LANES), F32), pltpu.VMEM((t_blk, LANES), F32), pltpu.VMEM((t_blk, LANES), F32)],
        compiler_params=pltpu.CompilerParams(dimension_semantics=("parallel", "arbitrary"),
                                             vmem_limit_bytes=_vmem_limit(5 * t_blk * w2 * 4)),
    )(*[_in_hbm(a) for a in (proj, states, states, d_yc, du_extra, bs, cs, lam, qw)])


def _loss_head(y, target, t_m):
    seq, d = y.shape

    def body(y_ref, t_ref, loss_ref, dy_ref):
        @pl.when(pl.program_id(0) == 0)
        def _():
            loss_ref[...] = jnp.zeros_like(loss_ref)

        diff = y_ref[...] - t_ref[...]
        dy_ref[...] = diff / d
        loss_ref[...] += 0.5 * jnp.sum(diff * diff) / d

    row = _spec((t_m, d), lambda i: (i, 0))
    return pl.pallas_call(
        body, name="loss_head", grid=(seq // t_m,), in_specs=[row, row],
        out_specs=[_spec((SUBLANES, LANES), lambda i: (0, 0)), row],
        out_shape=[jax.ShapeDtypeStruct((SUBLANES, LANES), F32), jax.ShapeDtypeStruct((seq, d), F32)],
        compiler_params=pltpu.CompilerParams(dimension_semantics=("arbitrary",),
                                             vmem_limit_bytes=_vmem_limit(6 * t_m * d * 4)),
    )(_in_hbm(y), _in_hbm(target))


def _adamw_fn(w, m, v, *partials):
    g = partials[0]
    for p in partials[1:]:
        g = g + p
    m2 = ADAM_B1 * m + (1.0 - ADAM_B1) * g
    v2 = ADAM_B2 * v + (1.0 - ADAM_B2) * (g * g)
    m_hat = m2 / (1.0 - ADAM_B1 ** ADAM_STEP)
    v_hat = v2 / (1.0 - ADAM_B2 ** ADAM_STEP)
    delta = -ADAM_LR * (m_hat / (jnp.sqrt(v_hat) + ADAM_EPS) + ADAM_WD * w)
    return g, delta, m2, v2


def _adamw(name, w, m, v, partials):
    rows, cols = w.shape
    t_r = rows
    for cand in (512, 256, 128, 64, 32, 16, 8):
        if rows % cand == 0 and cand * cols * 4 <= (1 << 20):
            t_r = cand
            break
    n_p = partials.shape[0]
    row = lambda i: (i, 0)
    ins = [(a, (t_r, cols), row) for a in (w, m, v)]
    ins += [(partials, (None, t_r, cols), (lambda i, j=j: (j, i, 0))) for j in range(n_p)]
    outs = [((rows, cols), F32, (t_r, cols), row)] * 4
    return _rowwise(name, _adamw_fn, ins, outs, (rows // t_r,))


SMALL_PARAMS = ("b_ada", "ssm_a_re", "ssm_a_im", "ssm_log_dt", "ssm_b_re", "ssm_b_im", "ssm_c_re", "ssm_c_im",
                "ssm_d", "b_glu", "ln1_g", "ln1_b", "ln2_g", "ln2_b")
WEIGHTS = ("w_ada", "b_ada", "w_in", "w_sb_up", "ssm_a_re", "ssm_a_im", "ssm_log_dt", "ssm_b_re", "ssm_b_im",
           "ssm_c_re", "ssm_c_im", "ssm_d", "w_glu", "b_glu", "w_ssm_up", "w_out", "ln1_g", "ln1_b", "w_ffn_in",
           "w_ffn_out", "ln2_g", "ln2_b")
ARG_NAMES = (("x", "c") + WEIGHTS + ("loss_target",) + tuple("m_" + n for n in WEIGHTS)
             + tuple("v_" + n for n in WEIGHTS))


def _pack(arrs):
    flat = jnp.concatenate([a.reshape(-1) for a in arrs])
    pad = (-flat.shape[0]) % (PACK_ROWS * LANES)
    return jnp.pad(flat, (0, pad)).reshape(-1, LANES)


def _unpack(packed, like):
    lead = packed.shape[:-2]
    flat = packed.reshape(lead + (-1,))
    out, off = [], 0
    for a in like:
        out.append(flat[..., off:off + a.size].reshape(lead + a.shape))
        off += a.size
    return out


def kernel(x, c, w_ada, b_ada, w_in, w_sb_up, ssm_a_re, ssm_a_im, ssm_log_dt, ssm_b_re, ssm_b_im, ssm_c_re,
           ssm_c_im, ssm_d, w_glu, b_glu, w_ssm_up, w_out, ln1_g, ln1_b, w_ffn_in, w_ffn_out, ln2_g, ln2_b,
           loss_target, m_w_ada, m_b_ada, m_w_in, m_w_sb_up, m_ssm_a_re, m_ssm_a_im, m_ssm_log_dt, m_ssm_b_re,
           m_ssm_b_im, m_ssm_c_re, m_ssm_c_im, m_ssm_d, m_w_glu, m_b_glu, m_w_ssm_up, m_w_out, m_ln1_g, m_ln1_b,
           m_w_ffn_in, m_w_ffn_out, m_ln2_g, m_ln2_b, v_w_ada, v_b_ada, v_w_in, v_w_sb_up, v_ssm_a_re, v_ssm_a_im,
           v_ssm_log_dt, v_ssm_b_re, v_ssm_b_im, v_ssm_c_re, v_ssm_c_im, v_ssm_d, v_w_glu, v_b_glu, v_w_ssm_up,
           v_w_out, v_ln1_g, v_ln1_b, v_w_ffn_in, v_w_ffn_out, v_ln2_g, v_ln2_b):
    given = locals()
    return _train_step({n: given[n] for n in ARG_NAMES})


def _train_step(p):
    x0 = p["x"][0]
    target = p["loss_target"][0]
    seq, d = x0.shape
    depth = p["w_ada"].shape[0]
    n_ada = p["w_ada"].shape[2]
    n_in = p["w_in"].shape[2]
    sb_w = p["w_sb_up"].shape[1]
    ssm_w = p["w_ssm_up"].shape[1]
    n_up = p["w_sb_up"].shape[2]
    n_ffn = p["w_ffn_in"].shape[2]
    ffn = N_DEV * p["w_ffn_out"].shape[1]
    in_cols = N_DEV * n_in
    alpha = (2 * depth) ** 0.25
    resid_ln, resid_ln_mod = _make_resid_fns(alpha)
    t_r = min(512, seq)
    n_r = seq // t_r
    t_m = min(1024, seq)
    n_m = seq // t_m
    t_d = _tile(d)
    assert n_ffn * (N_DEV // 2) == ffn and sb_w % LANES == 0 and ssm_w % LANES == 0 and d % LANES == 0
    assert n_in % LANES == 0 and n_up % LANES == 0 and seq % t_m == 0 and in_cols == 3 * sb_w + ssm_w + 2 * d
    assert (3 * sb_w) % ssm_w == 0 and (3 * sb_w + ssm_w) % (2 * d) == 0
    assert sb_w % n_in == 0 and ssm_w % n_in == 0 and d % n_in == 0 and seq % (SB_BLOCK * SB_GROUP) == 0
    proj_starts = [c // n_in for c in (0, sb_w, 2 * sb_w, 3 * sb_w, 3 * sb_w + ssm_w)]

    bf = lambda a: a.astype(BF16)
    got = _exchange("gather_first", [], [bf(p["w_in"][0]), p["c"]])
    wg_in = [got[0]] + [None] * (depth - 1)
    c_all = got[1].reshape(N_DEV, d)
    small_names = ("w_sb_up", "w_ssm_up", "w_glu", "w_out")
    wg_ffn_in, wg_ffn_out, wg = [None] * depth, [None] * depth, {}

    c_pad = jnp.pad(c_all, ((0, 2 * SUBLANES - N_DEV), (0, 0)))
    c_act = _rowwise("silu_c", lambda v: v * jax.nn.sigmoid(v), [(c_pad, c_pad.shape, lambda i: (0, 0))],
                     [(c_pad.shape, F32, c_pad.shape, lambda i: (0, 0))], (1,))[0]
    rows_c = c_pad.shape[0]
    mod_cols = [
        _mm(f"mod_{l}", c_act, p["w_ada"],
            _spec((rows_c, d), lambda i, j, k: (0, 0)), _spec((None, d, n_ada), lambda i, j, k, l=l: (l, 0, 0)),
            _spec((rows_c, n_ada), lambda i, j, k: (0, 0)), (rows_c, n_ada), F32, (1, 1, 1), NN)
        for l in range(depth)]
    mod_send = jnp.stack([m[:N_DEV] for m in mod_cols], axis=1)
    mod_recv = _exchange("exchange_mod", [mod_send], [])[0]
    mod_nobias = jnp.swapaxes(mod_recv, 0, 1).reshape(depth, N_DEV * n_ada)
    full2 = lambda a: (a, a.shape, lambda i: (0, 0))
    mod = _rowwise("mod_bias", lambda a, b: a + b, [full2(mod_nobias), full2(p["b_ada"])],
                   [(mod_nobias.shape, F32, mod_nobias.shape, lambda i: (0, 0))], (1,))[0]
    vec = lambda a: a.reshape(1, -1)
    mods = [[vec(mod[l, j * d:(j + 1) * d]) for j in range(6)] for l in range(depth)]
    ln = {n: [vec(p[n][l]) for l in range(depth)] for n in ("ln1_g", "ln1_b", "ln2_g", "ln2_b")}

    row_spec = lambda width: ((t_r, width), lambda i: (i, 0))
    col_spec = lambda width, cb: ((t_r, width), lambda i, cb=cb: (i, cb))
    vec_spec = lambda width: ((1, width), lambda i: (0, 0))
    rows_in = lambda a: (a,) + row_spec(a.shape[1])
    vec_in = lambda a: (a,) + vec_spec(a.shape[1])
    row_out = lambda width, dt: ((seq, width), dt) + row_spec(width)

    s5 = [_s5_discretize(*[p[n][l] for n in ("ssm_a_re", "ssm_a_im", "ssm_log_dt", "ssm_b_re", "ssm_b_im",
                                               "ssm_c_re", "ssm_c_im")]) for l in range(depth)]
    s5_b16 = [(bs.astype(BF16), cs.astype(BF16), lam) for bs, cs, lam in s5]
    t_scan = min(1024, seq)
    s5_pw = [_s5_powers(p["ssm_a_re"][l], p["ssm_a_im"][l], p["ssm_log_dt"][l], t_scan // SUBLANES)
             for l in range(depth)]
    u_col = 3 * sb_w // LANES
    gates_cb = (3 * sb_w + ssm_w) // (2 * d)
    ssm_d = [vec(p["ssm_d"][l]) for l in range(depth)]
    b_glu = [vec(p["b_glu"][l]) for l in range(depth)]
    n_half = N_DEV // 2

    h = _rowwise("modulate_in", _modulate, [rows_in(x0), vec_in(mods[0][1]), vec_in(mods[0][0])],
                 [row_out(d, BF16)], (n_r,))[0]
    saved = []
    x_cur = x0
    for l in range(depth):
        sv = {"x_in": x_cur, "h": h}
        last = l == depth - 1
        t_n = _tile(n_in)
        r_n = n_in // t_n
        proj = _mm(f"proj_{l}", h, wg_in[l],
                   _spec((t_m, d), lambda i, j, k: (i, 0)),
                   _spec((None, d, t_n), lambda i, j, k, r=r_n: (j // r, 0, j % r)),
                   _spec((t_m, t_n), lambda i, j, k: (i, j)), (seq, in_cols), F32, (n_m, N_DEV * r_n, 1), NN,
                   reread=(True, True))
        arriving = [bf(p["w_ffn_in"][l])] + ([bf(p[n]) for n in small_names] if l == 0 else [])
        (o_sb, o_sb32), got = _sb_attention_fwd(proj, sb_w, beside=_Exchange(gather=arriving))
        wg_ffn_in[l] = got[0]
        if l == 0:
            wg = dict(zip(small_names, got[1:]))
            for n in ("w_glu", "w_out"):
                wg[n] = jnp.swapaxes(wg[n], 0, 1).reshape(depth, -1, wg[n].shape[-1])
            for n in ("w_sb_up", "w_ssm_up"):
                wg[n] = jnp.transpose(wg[n], (1, 2, 0, 3)).reshape(depth, wg[n].shape[2], d)
        bs16, cs16, lam = s5_b16[l]
        arriving = [bf(p["w_ffn_out"][l])] + ([] if last else [bf(p["w_in"][l + 1])])
        (yc, states), got = _s5_scan_fwd(proj, u_col, bs16, cs16, lam, s5_pw[l][0], t_scan,
                                         beside=_Exchange(gather=arriving))
        wg_ffn_out[l] = got[0].reshape(n_half, n_ffn, d)
        if not last:
            wg_in[l + 1] = got[1]
        s5_out = _s5_head(f"s5_head_{l}", yc, proj, 3 * sb_w // ssm_w, ssm_d[l], b_glu[l], wg["w_glu"], l, t_r)

        merged, y_sb, y_ssm = _up_merge(f"up_merge_{l}", o_sb, s5_out, proj, gates_cb, wg["w_sb_up"], wg["w_ssm_up"],
                                        l, t_r)
        y_mix = _mm(f"out_proj_{l}", merged, wg["w_out"],
                    _spec((t_m, d), lambda i, j, k: (i, 0)), _spec((None, d, t_d), lambda i, j, k, l=l: (l, 0, j)),
                    _spec((t_m, t_d), lambda i, j, k: (i, j)), (seq, d), F32, (n_m, d // t_d, 1), NN)
        vecs_a = [mods[l][2], ln["ln1_g"][l], ln["ln1_b"][l], mods[l][4], mods[l][3]]
        x_mid, h2 = _rowwise(f"resid_mix_{l}", resid_ln_mod, [rows_in(x_cur), rows_in(y_mix)] + [vec_in(v) for v in vecs_a],
                             [row_out(d, F32), row_out(d, BF16)], (n_r,))
        a_ffn, f_act = _ffn_in_swiglu(f"ffn_in_{l}", h2, wg_ffn_in[l], t_r)
        y_ffn = _mm(f"ffn_out_{l}", f_act, wg_ffn_out[l],
                    _spec((None, t_m, n_ffn), lambda i, j, k: (k, i, 0)),
                    _spec((None, n_ffn, t_d), lambda i, j, k: (k, 0, j)),
                    _spec((t_m, t_d), lambda i, j, k: (i, j)), (seq, d), F32, (n_m, d // t_d, n_half), NN)
        vecs_b = [mods[l][5], ln["ln2_g"][l], ln["ln2_b"][l]] + ([] if last else [mods[l + 1][1], mods[l + 1][0]])
        outs_b = [row_out(d, F32)] + ([] if last else [row_out(d, BF16)])
        res = _rowwise(f"resid_ffn_{l}", resid_ln if last else resid_ln_mod,
                       [rows_in(x_mid), rows_in(y_ffn)] + [vec_in(v) for v in vecs_b], outs_b, (n_r,))
        sv.update(proj=proj, o_sb=o_sb, o_sb32=o_sb32, yc=yc, states=states, s5_out=s5_out,
                  y_sb=y_sb, y_ssm=y_ssm, merged=merged, y_mix=y_mix, x_mid=x_mid, h2=h2, a_ffn=a_ffn, f_act=f_act,
                  y_ffn=y_ffn, vecs_a=vecs_a, vecs_b=vecs_b)
        saved.append(sv)
        x_cur = res[0]
        h = None if last else res[1]

    loss_part, d_x = _loss_head(x_cur, target, t_r)
    loss = lax.psum(loss_part[0, 0], MESH_AXES)

    d_h_next = None
    grads = {n: [None] * depth for n in WEIGHTS}
    d_mod = [[None] * 6 for _ in range(depth)]
    land = {}
    waiting = []
    row_wrt = lambda i, width, dt: (i, "row", (seq, width), dt) + row_spec(width)
    sum_wrt = lambda i, width: (i, "sum", (1, width), F32) + vec_spec(width)
    for l in reversed(range(depth)):
        sv = saved[l]
        last = l == depth - 1
        ins_b = [rows_in(sv["x_mid"]), rows_in(sv["y_ffn"])] + [vec_in(v) for v in sv["vecs_b"]]
        cts_b = [rows_in(d_x)] + ([] if last else [rows_in(d_h_next)])
        wrt_b = [row_wrt(0, d, F32), row_wrt(1, d, BF16)] + [sum_wrt(2 + j, d) for j in range(len(sv["vecs_b"]))]
        res = _rowwise_vjp(f"resid_ffn_bwd_{l}", resid_ln if last else resid_ln_mod, ins_b, cts_b, wrt_b, (n_r,))
        d_x_mid, d_y_ffn = res[0], res[1]
        d_mod[l][5], grads["ln2_g"][l], grads["ln2_b"][l] = res[2], res[3], res[4]
        if not last:
            d_mod[l + 1][1], d_mod[l + 1][0] = res[5], res[6]
        d_a = _ffn_out_dx_swiglu(f"ffn_out_dx_{l}", d_y_ffn, wg_ffn_out[l], sv["a_ffn"], t_r).reshape(N_DEV, seq, n_ffn)
        g_ffn_out = _mm(f"ffn_out_dw_{l}", sv["f_act"], d_y_ffn,
                        _spec((None, t_m, n_ffn), lambda i, j, k: (i, k, 0)), _spec((t_m, t_d), lambda i, j, k: (k, j)),
                        _spec((None, n_ffn, t_d), lambda i, j, k: (i, 0, j)), (n_half, n_ffn, d), GRAD_WIRE,
                        (n_half, d // t_d, n_m), TN, reread=(False, True))
        d_h2 = _mm(f"ffn_in_dx_{l}", d_a, wg_ffn_in[l],
                   _spec((None, t_m, n_ffn), lambda i, j, k: (k, i, 0)),
                   _spec((None, t_d, n_ffn), lambda i, j, k: (k, j, 0)),
                   _spec((t_m, t_d), lambda i, j, k: (i, j)), (seq, d), BRANCH_CT, (n_m, d // t_d, N_DEV), NT)
        g_ffn_in = _mm(f"ffn_in_dw_{l}", sv["h2"], d_a,
                       _spec((t_m, t_d), lambda i, j, k: (k, j)), _spec((None, t_m, n_ffn), lambda i, j, k: (i, k, 0)),
                       _spec((None, t_d, n_ffn), lambda i, j, k: (i, j, 0)), (N_DEV, d, n_ffn), GRAD_WIRE,
                       (N_DEV, d // t_d, n_m), TN, reread=(True, False))
        ins_a = [rows_in(sv["x_in"]), rows_in(sv["y_mix"])] + [vec_in(v) for v in sv["vecs_a"]]
        wrt_a = [row_wrt(0, d, F32), row_wrt(1, d, BF16)] + [sum_wrt(2 + j, d) for j in range(5)]
        res = _rowwise_vjp(f"resid_mix_bwd_{l}", resid_ln_mod, ins_a, [rows_in(d_x_mid), rows_in(d_h2)], wrt_a, (n_r,))
        d_x_in, d_y_mix = res[0], res[1]
        d_mod[l][2], grads["ln1_g"][l], grads["ln1_b"][l], d_mod[l][4], d_mod[l][3] = res[2:7]
        d_merged = _mm(f"out_proj_dx_{l}", d_y_mix, wg["w_out"],
                       _spec((t_m, d), lambda i, j, k: (i, 0)), _spec((None, t_d, d), lambda i, j, k, l=l: (l, j, 0)),
                       _spec((t_m, t_d), lambda i, j, k: (i, j)), (seq, d), BRANCH_CT, (n_m, d // t_d, 1), NT)
        g_out = _mm(f"out_proj_dw_{l}", sv["merged"], d_y_mix,
                    _spec((t_m, t_d), lambda i, j, k: (k, i)), _spec((t_m, t_d), lambda i, j, k: (k, j)),
                    _spec((t_d, t_d), lambda i, j, k: (i, j)), (d, d), GRAD_WIRE, (d // t_d, d // t_d, n_m), TN, reread=(d > t_d, d > t_d))
        gates = (sv["proj"],) + col_spec(2 * d, gates_cb)
        d_y_sb, d_y_ssm, d_gates = _rowwise_vjp(
            f"merge_bwd_{l}", _merge_fn, [rows_in(sv["y_sb"]), rows_in(sv["y_ssm"]), gates], [rows_in(d_merged)],
            [row_wrt(0, d, BF16), row_wrt(1, d, BF16), row_wrt(2, 2 * d, BF16)], (n_r,))

        def up_bwd(name, act, d_y, w, dx_dtype, l=l):
            k_w = act.shape[1]
            dx = _mm(name + "_dx", d_y, w, _spec((t_m, d), lambda i, j, k: (i, 0)),
                     _spec((None, k_w, d), lambda i, j, k: (l, 0, 0)),
                     _spec((t_m, k_w), lambda i, j, k: (i, 0)), (seq, k_w), dx_dtype, (n_m, 1, 1), NT)
            dw = _mm(name + "_dw", act, d_y, _spec((t_m, k_w), lambda i, j, k: (k, 0)),
                     _spec((t_m, t_d), lambda i, j, k: (k, j)),
                     _spec((k_w, t_d), lambda i, j, k: (0, j)), (k_w, d), GRAD_WIRE, (1, d // t_d, n_m), TN,
                     reread=(d > t_d, False))
            return dx, jnp.swapaxes(dw.reshape(k_w, N_DEV, n_up), 0, 1)

        d_o_sb, g_sb_up = up_bwd(f"sb_up_{l}", sv["o_sb"], d_y_sb, wg["w_sb_up"], BF16)
        d_s5_out, g_ssm_up = up_bwd(f"ssm_up_{l}", sv["s5_out"], d_y_ssm, wg["w_ssm_up"], BRANCH_CT)
        waiting += [("w_ffn_in", g_ffn_in), ("w_ffn_out", g_ffn_out.reshape(N_DEV, -1, d)),
                    ("w_out", g_out.reshape(N_DEV, -1, d)), ("w_sb_up", g_sb_up), ("w_ssm_up", g_ssm_up)]
        levels = [l + 1] * (len(waiting) - 5) + [l] * 5
        (d_q, d_k, d_v), got = _sb_attention_bwd(
            sv["proj"], sv["o_sb32"], d_o_sb, sb_w,
            beside=_Exchange(layered=[(g, lv, depth, land.get(n)) for (n, g), lv in zip(waiting, levels)]))
        land.update({n: buf for (n, _), buf in zip(waiting, got)})
        d_yc, d_u_skip, g_glu, grads["ssm_d"][l], grads["b_glu"][l] = _s5_head_bwd(
            f"s5_head_bwd_{l}", sv["yc"], sv["proj"], 3 * sb_w // ssm_w, d_s5_out, ssm_d[l], b_glu[l], wg["w_glu"], l, t_r)
        bs16, cs16, lam = s5_b16[l]
        d_u, d_bs, d_cs, d_lam = _s5_scan_bwd(sv["proj"], u_col, sv["states"], d_yc, d_u_skip, bs16, cs16, lam,
                                              s5_pw[l][1], t_scan)
        raw = [p[n][l] for n in ("ssm_a_re", "ssm_a_im", "ssm_log_dt", "ssm_b_re", "ssm_b_im", "ssm_c_re", "ssm_c_im")]
        _, pull = jax.vjp(_s5_discretize, *raw)
        (grads["ssm_a_re"][l], grads["ssm_a_im"][l], grads["ssm_log_dt"][l], grads["ssm_b_re"][l],
         grads["ssm_b_im"][l], grads["ssm_c_re"][l], grads["ssm_c_im"][l]) = pull((d_bs, d_cs, d_lam))
        d_proj = [d_q, d_k, d_v, d_u, d_gates]
        g_in = _mm_pieces(f"proj_dw_{l}", d_proj, proj_starts, n_in, lambda i, j, k: i, sv["h"],
                          _spec((t_m, t_d), lambda i, j, k: (k, j)), False, lambda i, j, k: k,
                          _spec((None, t_d, n_in), lambda i, j, k: (i, j, 0)), (N_DEV, d, n_in), GRAD_WIRE,
                          (N_DEV, d // t_d, n_m), TN)
        waiting = [("w_in", g_in), ("w_glu", g_glu.reshape(N_DEV, -1, ssm_w))]
        closing = _Exchange(layered=[(g, 0, depth, land.get(n)) for n, g in waiting]) if l == 0 else None
        d_h = _mm_pieces(f"proj_dx_{l}", d_proj, proj_starts, n_in, lambda i, j, k: k, wg_in[l],
                         _spec((None, t_d, n_in), lambda i, j, k: (k, j, 0)), True, lambda i, j, k: i,
                         _spec((t_m, t_d), lambda i, j, k: (i, j)), (seq, d), BRANCH_CT, (n_m, d // t_d, N_DEV), NT,
                         beside=closing)
        if l == 0:
            d_h, got = d_h
            land.update({n: buf for (n, _), buf in zip(waiting, got)})
        d_x, d_h_next = d_x_in, d_h
    res = _rowwise_vjp("modulate_in_bwd", lambda v, sc, sh: (v, _modulate(v, sc, sh)),
                       [rows_in(x0), vec_in(mods[0][1]), vec_in(mods[0][0])], [rows_in(d_x), rows_in(d_h_next)],
                       [row_wrt(0, d, F32), sum_wrt(1, d), sum_wrt(2, d)], (n_r,))
    grad_x, d_mod[0][1], d_mod[0][0] = res

    d_mod_rows = jnp.concatenate([jnp.concatenate(d_mod[l], axis=1) for l in range(depth)], axis=0)
    grads["b_ada"] = [d_mod_rows[l] for l in range(depth)]
    small_local = [jnp.stack([g.reshape(p[n].shape[1:]) for g in grads[n]]) for n in SMALL_PARAMS]
    d_mod_send = jnp.swapaxes(d_mod_rows.reshape(depth, N_DEV, n_ada), 0, 1)
    small_sum, (d_mod_cols,) = _reduce_packed("exchange_last", _pack(small_local), [d_mod_send])
    d_mod_pad = jnp.pad(jnp.swapaxes(d_mod_cols, 0, 1), ((0, 0), (0, rows_c - N_DEV), (0, 0)))
    g_ada = [
        _mm(f"mod_dw_{l}", c_act, d_mod_pad,
            _spec((rows_c, d), lambda i, j, k: (0, 0)), _spec((None, rows_c, n_ada), lambda i, j, k, l=l: (l, 0, 0)),
            _spec((d, n_ada), lambda i, j, k: (0, 0)), (d, n_ada), F32, (1, 1, 1), TN)
        for l in range(depth)]

    out = {}

    def update(name, partials):
        shape = p[name].shape
        two_d = lambda a: a.reshape(-1, shape[-1])
        res = _adamw("adamw_" + name, two_d(p[name]), two_d(p["m_" + name]), two_d(p["v_" + name]),
                     partials.reshape(partials.shape[0], -1, shape[-1]))
        out[name] = [r.reshape(shape) for r in res]

    update("w_ada", jnp.stack(g_ada)[None])
    for n in ("w_in", "w_sb_up", "w_ssm_up", "w_ffn_in", "w_glu", "w_out", "w_ffn_out"):
        update(n, land[n])
    small_w = [p[n] for n in SMALL_PARAMS]
    res = _adamw("adamw_small", _pack(small_w), _pack([p["m_" + n] for n in SMALL_PARAMS]),
                 _pack([p["v_" + n] for n in SMALL_PARAMS]), small_sum[None])
    for kind, packed in enumerate(res):
        for n, a in zip(SMALL_PARAMS, _unpack(packed, small_w)):
            out.setdefault(n, [None] * 4)[kind] = a

    return ((loss, grad_x[None]) + tuple(out[n][0] for n in WEIGHTS) + tuple(out[n][1] for n in WEIGHTS)
            + tuple(out[n][2] for n in WEIGHTS) + tuple(out[n][3] for n in WEIGHTS))
```

```python
import jax
import jax.numpy as jnp
from jax import lax
from jax.experimental import pallas as pl
from jax.experimental.pallas import tpu as pltpu

F32 = jnp.float32
BF16 = jnp.bfloat16
GRAD_WIRE = BF16
FFN_ACT = BF16
BRANCH_CT = BF16

N_DEV = 8
LANES = 128
SUBLANES = 8
VMEM_BYTES = 64 * 1024 * 1024
HEAD_DIM = 64
SB_BLOCK = 256
SB_GROUP = 2
SLAB_GROUPS = 8
LN_EPS = 1e-5
ADAM_LR, ADAM_B1, ADAM_B2, ADAM_EPS, ADAM_WD, ADAM_STEP = 0.001, 0.9, 0.999, 1e-08, 0.01, 10
SB_UNDERFLOW = -120.0

PACK_ROWS = 256
MESH_AXES = ("x", "y", "c")


def _vmem_limit(block_bytes):
    return int(min(max(3 * block_bytes + (8 << 20), 24 << 20), VMEM_BYTES - (8 << 20)))


def _nbytes(shape, dtype):
    n = 1
    for d in shape:
        if d is not None:
            n *= d
    return n * jnp.dtype(dtype).itemsize


def _spec(shape, fn):
    return pl.BlockSpec(shape, fn)


class _Exchange:
    def __init__(self, scatter=(), gather=(), layered=()):
        self.arrs = list(scatter) + [a for a, _, _, _ in layered] + list(gather)
        self.n = len(self.arrs)
        self.n_sc = len(scatter) + len(layered)
        self.layer = [None] * len(scatter) + [l for _, l, _, _ in layered] + [None] * len(gather)
        self.shapes = ([a.shape for a in scatter] + [(N_DEV, dp) + a.shape[1:] for a, _, dp, _ in layered]
                       + [(N_DEV,) + a.shape for a in gather])
        self.held = [(len(scatter) + i, b) for i, (_, _, _, b) in enumerate(layered) if b is not None]
        self.operands = self.arrs + [b for _, b in self.held]
        hbm = pl.BlockSpec(memory_space=pltpu.HBM)
        self.in_specs = [hbm] * len(self.operands)
        self.out_specs = [hbm] * self.n
        self.out_shape = [jax.ShapeDtypeStruct(s, a.dtype) for s, a in zip(self.shapes, self.arrs)]
        self.scratch = [pltpu.SemaphoreType.DMA((self.n, N_DEV - 1)), pltpu.SemaphoreType.DMA((self.n, N_DEV - 1)),
                        pltpu.SemaphoreType.DMA((self.n,))]

    def aliases(self, first_in, first_out):
        return {first_in + self.n + i: first_out + a for i, (a, _) in enumerate(self.held)}

    def copies(self, ins, outs, sems):
        send_sems, recv_sems, own_sems = sems
        x, y, c = lax.axis_index("x"), lax.axis_index("y"), lax.axis_index("c")
        me = 4 * x + 2 * y + c
        landing = [outs[a].at[me] if self.layer[a] is None else outs[a].at[me, self.layer[a]] for a in range(self.n)]
        out = [pltpu.make_async_copy(ins[a].at[me] if a < self.n_sc else ins[a], landing[a], own_sems.at[a])
               for a in range(self.n)]
        for k in range(1, N_DEV):
            px = 1 - x if k & 4 else x
            py = 1 - y if k & 2 else y
            pc = 1 - c if k & 1 else c
            peer = 4 * px + 2 * py + pc
            for a in range(self.n):
                out.append(pltpu.make_async_remote_copy(
                    src_ref=ins[a].at[peer] if a < self.n_sc else ins[a], dst_ref=landing[a],
                    send_sem=send_sems.at[a, k - 1], recv_sem=recv_sems.at[a, k - 1],
                    device_id=(px, py, pc), device_id_type=pl.DeviceIdType.MESH))
        return out


def _exchange(name, scatter, gather, layered=()):
    ex = _Exchange(scatter, gather, layered)

    def body(*refs):
        copies = ex.copies(refs[:ex.n], refs[len(ex.operands):len(ex.operands) + ex.n], refs[-3:])
        for cp in copies:
            cp.start()
        for cp in copies:
            cp.wait()

    return pl.pallas_call(body, name=name, in_specs=ex.in_specs, out_specs=ex.out_specs, out_shape=ex.out_shape,
                          input_output_aliases=ex.aliases(0, 0), scratch_shapes=ex.scratch)(*ex.operands)


def _reduce_packed(name, packed, scatter):
    rows = packed.shape[0]
    blk = rows // N_DEV
    ex = _Exchange(scatter=[packed.reshape(N_DEV, blk, LANES)] + list(scatter))
    n_in = len(ex.operands)

    def body(*refs):
        ins, outs = refs[:ex.n], refs[n_in:n_in + ex.n]
        total_ref = refs[n_in + ex.n]
        sems, (send2, recv2, own2, load_sem) = refs[n_in + ex.n + 1:n_in + ex.n + 4], refs[n_in + ex.n + 4:-2]
        land_v, sum_v = refs[-2:]
        copies = ex.copies(ins, outs, sems)
        for cp in copies:
            cp.start()
        for cp in copies:
            cp.wait()
        load = pltpu.make_async_copy(outs[0], land_v, load_sem)
        load.start()
        load.wait()
        acc = land_v[0]
        for i in range(1, N_DEV):
            acc = acc + land_v[i]
        sum_v[...] = acc
        x, y, c = lax.axis_index("x"), lax.axis_index("y"), lax.axis_index("c")
        me = 4 * x + 2 * y + c
        back = [pltpu.make_async_copy(sum_v, total_ref.at[me], own2)]
        for k in range(1, N_DEV):
            peer = (1 - x if k & 4 else x, 1 - y if k & 2 else y, 1 - c if k & 1 else c)
            back.append(pltpu.make_async_remote_copy(
                src_ref=sum_v, dst_ref=total_ref.at[me], send_sem=send2.at[k - 1], recv_sem=recv2.at[k - 1],
                device_id=peer, device_id_type=pl.DeviceIdType.MESH))
        for cp in back:
            cp.start()
        for cp in back:
            cp.wait()

    hbm = pl.BlockSpec(memory_space=pltpu.HBM)
    res = pl.pallas_call(
        body, name=name, in_specs=ex.in_specs, out_specs=ex.out_specs + [hbm],
        out_shape=ex.out_shape + [jax.ShapeDtypeStruct((N_DEV, blk, LANES), F32)],
        scratch_shapes=ex.scratch + [pltpu.SemaphoreType.DMA((N_DEV - 1,)), pltpu.SemaphoreType.DMA((N_DEV - 1,)),
                                     pltpu.SemaphoreType.DMA, pltpu.SemaphoreType.DMA,
                                     pltpu.VMEM((N_DEV, blk, LANES), F32), pltpu.VMEM((blk, LANES), F32)],
    )(*ex.operands)
    return res[-1].reshape(rows, LANES), res[1:-1]


def _call_beside(ex, body, name, grid, in_specs, out_specs, out_shape, scratch_shapes, vmem_bytes, operands,
                 semantics, in_hbm=True):
    if in_hbm:
        operands = [_in_hbm(a) for a in operands]
    if ex is None:
        res = pl.pallas_call(
            body, name=name, grid=grid, in_specs=in_specs, out_specs=out_specs, out_shape=out_shape,
            scratch_shapes=scratch_shapes,
            compiler_params=pltpu.CompilerParams(dimension_semantics=semantics, vmem_limit_bytes=vmem_bytes),
        )(*operands)
        return res, None
    n_in, n_out, n_scr = len(in_specs), len(out_specs), len(scratch_shapes)
    n_xin = len(ex.operands)

    def fused(*refs):
        mine = refs[:n_in] + refs[n_in + n_xin:n_in + n_xin + n_out]
        mine += refs[n_in + n_xin + n_out + ex.n:n_in + n_xin + n_out + ex.n + n_scr]
        first = pl.program_id(0) == 0
        last = pl.program_id(0) == grid[0] - 1
        for dim in range(1, len(grid)):
            first = jnp.logical_and(first, pl.program_id(dim) == 0)
            last = jnp.logical_and(last, pl.program_id(dim) == grid[dim] - 1)
        x_ins = refs[n_in:n_in + ex.n]
        x_outs = refs[n_in + n_xin + n_out:n_in + n_xin + n_out + ex.n]

        @pl.when(first)
        def _():
            for cp in ex.copies(x_ins, x_outs, refs[-3:]):
                cp.start()

        body(*mine)

        @pl.when(last)
        def _():
            for cp in ex.copies(x_ins, x_outs, refs[-3:]):
                cp.wait()

    res = pl.pallas_call(
        fused, name=name, grid=grid, in_specs=list(in_specs) + ex.in_specs, out_specs=list(out_specs) + ex.out_specs,
        out_shape=list(out_shape) + ex.out_shape, input_output_aliases=ex.aliases(n_in, n_out),
        scratch_shapes=list(scratch_shapes) + ex.scratch,
        compiler_params=pltpu.CompilerParams(dimension_semantics=("arbitrary",) * len(grid),
                                             vmem_limit_bytes=vmem_bytes),
    )(*operands, *ex.operands)
    return res[:n_out], res[n_out:]


NN = (((1,), (0,)), ((), ()))
NT = (((1,), (1,)), ((), ()))
TN = (((0,), (0,)), ((), ()))


def _in_hbm(a):
    return pltpu.with_memory_space_constraint(a, pltpu.HBM)


def _mm(name, a, b, a_spec, b_spec, o_spec, o_shape, o_dtype, grid, dims, beside=None, reread=(False, True)):
    nk = grid[2]
    a, b = (x if again else _in_hbm(x) for x, again in zip((a, b), reread))
    acc_shape = tuple(d for d in o_spec.block_shape if d is not None)

    def product(a_ref, b_ref):
        return lax.dot_general(a_ref[...].astype(BF16), b_ref[...].astype(BF16), dims, preferred_element_type=F32)

    def body_once(a_ref, b_ref, o_ref):
        o_ref[...] = product(a_ref, b_ref).astype(o_ref.dtype)

    def body(a_ref, b_ref, o_ref, acc_ref):
        k = pl.program_id(2)

        @pl.when(k == 0)
        def _():
            acc_ref[...] = product(a_ref, b_ref)

        @pl.when(k > 0)
        def _():
            acc_ref[...] += product(a_ref, b_ref)

        @pl.when(k == nk - 1)
        def _():
            o_ref[...] = acc_ref[...].astype(o_ref.dtype)

    blk = (_nbytes(a_spec.block_shape, a.dtype) + _nbytes(b_spec.block_shape, b.dtype)
           + _nbytes(acc_shape, o_dtype) + _nbytes(acc_shape, F32))
    res, got = _call_beside(
        beside, body_once if nk == 1 else body, name, grid, [a_spec, b_spec], [o_spec],
        [jax.ShapeDtypeStruct(o_shape, o_dtype)], [] if nk == 1 else [pltpu.VMEM(acc_shape, F32)],
        _vmem_limit(blk), (a, b), ("parallel", "parallel", "arbitrary"), in_hbm=False)
    return res[0] if beside is None else (res[0], got)


def _mm_pieces(name, pieces, starts, width, step_block, other, other_spec, pieces_first, piece_rows, o_spec, o_shape,
               o_dtype, grid, dims, beside=None):
    n_p, nk = len(pieces), grid[2]
    acc_shape = tuple(s for s in o_spec.block_shape if s is not None)

    def which(i, j, k):
        blk = step_block(i, j, k)
        idx = 0
        for s in starts[1:]:
            idx = idx + (blk >= s).astype(jnp.int32)
        return idx, blk

    def piece_spec(p, t_rows):
        def index(i, j, k):
            idx, blk = which(i, j, k)
            mine = idx == p
            return jnp.where(mine, piece_rows(i, j, k), 0), jnp.where(mine, blk - starts[p], 0)
        return _spec((t_rows, width), index)

    def body(*refs):
        p_refs = refs[:n_p] if pieces_first else refs[1:1 + n_p]
        other_ref = refs[n_p] if pieces_first else refs[0]
        o_ref, acc_ref = refs[n_p + 1], refs[n_p + 2]
        i, j, k = pl.program_id(0), pl.program_id(1), pl.program_id(2)

        @pl.when(k == 0)
        def _():
            acc_ref[...] = jnp.zeros_like(acc_ref)

        idx, _ = which(i, j, k)
        for p in range(n_p):
            @pl.when(idx == p)
            def _(p=p):
                mine, fixed = p_refs[p][...].astype(BF16), other_ref[...].astype(BF16)
                pair = (mine, fixed) if pieces_first else (fixed, mine)
                acc_ref[...] += lax.dot_general(pair[0], pair[1], dims, preferred_element_type=F32)

        @pl.when(k == nk - 1)
        def _():
            o_ref[...] = acc_ref[...].astype(o_ref.dtype)

    t_rows = other_spec.block_shape[-2] if not pieces_first else o_spec.block_shape[-2]
    specs = [piece_spec(p, t_rows) for p in range(n_p)]
    in_specs = specs + [other_spec] if pieces_first else [other_spec] + specs
    operands = list(pieces) + [other] if pieces_first else [other] + list(pieces)
    blk = (n_p * 4 * t_rows * width + _nbytes(other_spec.block_shape, other.dtype)
           + _nbytes(acc_shape, o_dtype) + _nbytes(acc_shape, F32))
    res, got = _call_beside(
        beside, body, name, grid, in_specs, [o_spec], [jax.ShapeDtypeStruct(o_shape, o_dtype)],
        [pltpu.VMEM(acc_shape, F32)], _vmem_limit(blk), operands, ("parallel", "parallel", "arbitrary"), in_hbm=False)
    return res[0] if beside is None else (res[0], got)


def _swiglu_fn(gate_up):
    gate, up = gate_up[0], gate_up[1]
    return gate * jax.nn.sigmoid(gate) * up


def _ffn_in_swiglu(name, h, w, t_m):
    seq, d = h.shape
    n_half, n = w.shape[0] // 2, w.shape[2]

    def body(h_ref, wg_ref, wu_ref, a_ref, f_ref):
        hb = h_ref[...]
        a_ref[0] = lax.dot_general(hb, wg_ref[...], NN, preferred_element_type=F32).astype(a_ref.dtype)
        a_ref[1] = lax.dot_general(hb, wu_ref[...], NN, preferred_element_type=F32).astype(a_ref.dtype)
        f_ref[...] = _swiglu_fn(a_ref[...].astype(F32)).astype(f_ref.dtype)

    blk = 2 * t_m * d + 4 * d * n + 6 * t_m * n + 12 * t_m * n
    return pl.pallas_call(
        body, name=name, grid=(seq // t_m, n_half),
        in_specs=[_spec((t_m, d), lambda i, j: (i, 0)), _spec((None, d, n), lambda i, j: (j, 0, 0)),
                  _spec((None, d, n), lambda i, j: (j + n_half, 0, 0))],
        out_specs=[_spec((2, None, t_m, n), lambda i, j: (0, j, i, 0)), _spec((None, t_m, n), lambda i, j: (j, i, 0))],
        out_shape=[jax.ShapeDtypeStruct((2, n_half, seq, n), FFN_ACT), jax.ShapeDtypeStruct((n_half, seq, n), BF16)],
        compiler_params=pltpu.CompilerParams(dimension_semantics=("parallel", "parallel"),
                                             vmem_limit_bytes=_vmem_limit(blk)),
    )(h, w, w)


def _ffn_out_dx_swiglu(name, d_y, w, a, t_m):
    seq, d = d_y.shape
    n_half, n = w.shape[0], w.shape[1]

    def body(dy_ref, w_ref, a_ref, da_ref):
        d_f = lax.dot_general(dy_ref[...], w_ref[...], NT, preferred_element_type=F32)
        gate, up = a_ref[0].astype(F32), a_ref[1].astype(F32)
        s = jax.nn.sigmoid(gate)
        gs = gate * s
        da_ref[0] = (d_f * up * (s + gs * (1.0 - s))).astype(da_ref.dtype)
        da_ref[1] = (d_f * gs).astype(da_ref.dtype)

    blk = 2 * t_m * d + 2 * d * n + 8 * t_m * n + 24 * t_m * n
    return pl.pallas_call(
        body, name=name, grid=(seq // t_m, n_half),
        in_specs=[_spec((t_m, d), lambda i, j: (i, 0)), _spec((None, n, d), lambda i, j: (j, 0, 0)),
                  _spec((2, None, t_m, n), lambda i, j: (0, j, i, 0))],
        out_specs=_spec((2, None, t_m, n), lambda i, j: (0, j, i, 0)),
        out_shape=jax.ShapeDtypeStruct((2, n_half, seq, n), BF16),
        compiler_params=pltpu.CompilerParams(dimension_semantics=("parallel", "parallel"),
                                             vmem_limit_bytes=_vmem_limit(blk)),
    )(d_y, w, a)


def _merge_fn(y_sb, y_ssm, gates):
    half = gates.shape[-1] // 2
    return jax.nn.sigmoid(gates[:, :half]) * y_sb + jax.nn.sigmoid(gates[:, half:]) * y_ssm


def _up_merge(name, o_sb, s5_out, proj, gates_cb, w_sb, w_ssm, layer, t_rows):
    seq = o_sb.shape[0]
    d = w_sb.shape[2]

    def body(o_ref, s_ref, g_ref, w1_ref, w2_ref, m_ref, y1_ref, y2_ref):
        y_sb = lax.dot_general(o_ref[...], w1_ref[...], NN, preferred_element_type=F32)
        y_ssm = lax.dot_general(s_ref[...], w2_ref[...], NN, preferred_element_type=F32)
        m_ref[...] = _merge_fn(y_sb, y_ssm, g_ref[...]).astype(m_ref.dtype)
        y1_ref[...] = y_sb.astype(y1_ref.dtype)
        y2_ref[...] = y_ssm.astype(y2_ref.dtype)

    row = lambda width: _spec((t_rows, width), lambda i: (i, 0))
    whole = lambda w: _spec((None,) + w.shape[1:], lambda i: (layer, 0, 0))
    blk = t_rows * (2 * o_sb.shape[1] + 2 * s5_out.shape[1] + 8 * d + 6 * d + 24 * d) + 4 * d * (o_sb.shape[1] + s5_out.shape[1])
    return pl.pallas_call(
        body, name=name, grid=(seq // t_rows,),
        in_specs=[row(o_sb.shape[1]), row(s5_out.shape[1]), _spec((t_rows, 2 * d), lambda i: (i, gates_cb)),
                  whole(w_sb), whole(w_ssm)],
        out_specs=[row(d)] * 3, out_shape=[jax.ShapeDtypeStruct((seq, d), BF16)] * 3,
        compiler_params=pltpu.CompilerParams(dimension_semantics=("parallel",), vmem_limit_bytes=_vmem_limit(blk)),
    )(_in_hbm(o_sb), _in_hbm(s5_out), _in_hbm(proj), w_sb, w_ssm)


def _tile(n, pref=1024):
    t = pref
    while t >= LANES:
        if n % t == 0:
            return t
        t -= LANES
    return n


def _rowwise(name, fn, ins, outs, grid):
    n_in = len(ins)

    def body(*refs):
        vals = fn(*[r[...].astype(F32) for r in refs[:n_in]])
        if not isinstance(vals, (tuple, list)):
            vals = (vals,)
        for r, v in zip(refs[n_in:], vals):
            r[...] = v.astype(r.dtype)

    blk = sum(_nbytes(bs, a.dtype) for a, bs, _ in ins) + sum(_nbytes(bs, d) + _nbytes(bs, F32) for _, d, bs, _ in outs)
    return pl.pallas_call(
        body, name=name, grid=grid,
        in_specs=[_spec(bs, im) for _, bs, im in ins],
        out_specs=[_spec(bs, im) for _, _, bs, im in outs],
        out_shape=[jax.ShapeDtypeStruct(s, d) for s, d, _, _ in outs],
        compiler_params=pltpu.CompilerParams(dimension_semantics=("parallel",) * len(grid),
                                             vmem_limit_bytes=_vmem_limit(2 * blk)),
    )(*[_in_hbm(a) for a, _, _ in ins])


def _rowwise_vjp(name, fn, ins, cts, wrt, grid):
    n_in, n_ct = len(ins), len(cts)
    idx = [w[0] for w in wrt]

    def body(*refs):
        prim = [r[...].astype(F32) for r in refs[:n_in]]
        ct = tuple(r[...].astype(F32) for r in refs[n_in:n_in + n_ct])
        o_refs = refs[n_in + n_ct:]

        def g(*sel):
            full = list(prim)
            for i, s in zip(idx, sel):
                full[i] = s
            out = fn(*full)
            return tuple(out) if isinstance(out, (tuple, list)) else (out,)

        _, pull = jax.vjp(g, *[prim[i] for i in idx])
        grads = pull(ct)
        first = pl.program_id(0) == 0
        for d in range(1, len(grid)):
            first = jnp.logical_and(first, pl.program_id(d) == 0)
        for w, o_ref, gr in zip(wrt, o_refs, grads):
            if w[1] == "row":
                o_ref[...] = gr.astype(o_ref.dtype)
            else:
                @pl.when(first)
                def _(o_ref=o_ref):
                    o_ref[...] = jnp.zeros_like(o_ref)

                o_ref[...] += gr.astype(o_ref.dtype)

    blk = (sum(_nbytes(bs, a.dtype) + _nbytes(bs, F32) for a, bs, _ in list(ins) + list(cts))
           + sum(_nbytes(w[4], w[3]) + _nbytes(w[4], F32) for w in wrt))
    return pl.pallas_call(
        body, name=name, grid=grid,
        in_specs=[_spec(bs, im) for _, bs, im in list(ins) + list(cts)],
        out_specs=[_spec(w[4], w[5]) for w in wrt],
        out_shape=[jax.ShapeDtypeStruct(w[2], w[3]) for w in wrt],
        compiler_params=pltpu.CompilerParams(dimension_semantics=("arbitrary",) * len(grid),
                                             vmem_limit_bytes=_vmem_limit(2 * blk)),
    )(*[_in_hbm(a) for a, _, _ in list(ins) + list(cts)])


def _normalize(x):
    mu = jnp.mean(x, axis=-1, keepdims=True)
    xc = x - mu
    var = jnp.mean(xc * xc, axis=-1, keepdims=True)
    return xc * lax.rsqrt(var + LN_EPS)


def _modulate(x, sc, sh):
    return _normalize(x) * (1.0 + sc) + sh


def _make_resid_fns(alpha):
    def resid_ln(x, y, gate, g, b):
        return _normalize(alpha * x + (1.0 + gate) * y) * g + b

    def resid_ln_mod(x, y, gate, g, b, sc, sh):
        xn = resid_ln(x, y, gate, g, b)
        return xn, _modulate(xn, sc, sh)

    return resid_ln, resid_ln_mod


def _s5_act_fn(yc, u, d_skip):
    return jax.nn.gelu(yc + d_skip * u)


def _s5_gate_fn(y1, t):
    return y1 * jax.nn.sigmoid(t)


def _s5_head_specs(yc, proj, u_cb, w_glu, layer, t_rows):
    width = yc.shape[1]
    row = _spec((t_rows, width), lambda i: (i, 0))
    u_spec = _spec((t_rows, width), lambda i: (i, u_cb))
    vec = _spec((1, width), lambda i: (0, 0))
    w_spec = _spec((None,) + w_glu.shape[1:], lambda i: (layer, 0, 0))
    return row, u_spec, vec, w_spec


def _s5_head(name, yc, proj, u_cb, d_skip, b_glu, w_glu, layer, t_rows):
    seq, width = yc.shape
    row, u_spec, vec, w_spec = _s5_head_specs(yc, proj, u_cb, w_glu, layer, t_rows)

    def body(yc_ref, u_ref, d_ref, b_ref, w_ref, o_ref):
        y1 = _s5_act_fn(yc_ref[...], u_ref[...], d_ref[...])
        t = lax.dot_general(y1.astype(BF16), w_ref[...], NN, preferred_element_type=F32) + b_ref[...]
        o_ref[...] = _s5_gate_fn(y1, t).astype(o_ref.dtype)

    return pl.pallas_call(
        body, name=name, grid=(seq // t_rows,), in_specs=[row, u_spec, vec, vec, w_spec], out_specs=row,
        out_shape=jax.ShapeDtypeStruct((seq, width), BF16),
        compiler_params=pltpu.CompilerParams(dimension_semantics=("parallel",),
                                             vmem_limit_bytes=_vmem_limit(40 * t_rows * width)),
    )(_in_hbm(yc), _in_hbm(proj), d_skip, b_glu, w_glu)


def _s5_head_bwd(name, yc, proj, u_cb, d_out, d_skip, b_glu, w_glu, layer, t_rows):
    seq, width = yc.shape
    row, u_spec, vec, w_spec = _s5_head_specs(yc, proj, u_cb, w_glu, layer, t_rows)
    n_t = seq // t_rows

    def body(yc_ref, u_ref, do_ref, d_ref, b_ref, w_ref, dyc_ref, du_ref, dw_ref, dd_ref, db_ref, acc_ref):
        i = pl.program_id(0)

        @pl.when(i == 0)
        def _():
            acc_ref[...] = jnp.zeros_like(acc_ref)
            dd_ref[...] = jnp.zeros_like(dd_ref)
            db_ref[...] = jnp.zeros_like(db_ref)

        y1, pull_act = jax.vjp(_s5_act_fn, yc_ref[...], u_ref[...], d_ref[...])
        y1_b = y1.astype(BF16)
        t = lax.dot_general(y1_b, w_ref[...], NN, preferred_element_type=F32) + b_ref[...]
        _, pull_gate = jax.vjp(_s5_gate_fn, y1, t)
        d_y1, d_t = pull_gate(do_ref[...].astype(F32))
        d_t_b = d_t.astype(BF16)
        d_y1 = d_y1 + lax.dot_general(d_t_b, w_ref[...], NT, preferred_element_type=F32)
        acc_ref[...] += lax.dot_general(y1_b, d_t_b, TN, preferred_element_type=F32)
        db_ref[...] += jnp.sum(d_t, axis=0, keepdims=True)
        d_yc, d_u, d_d = pull_act(d_y1)
        dyc_ref[...] = d_yc
        du_ref[...] = d_u
        dd_ref[...] += d_d

        @pl.when(i == n_t - 1)
        def _():
            dw_ref[...] = acc_ref[...].astype(dw_ref.dtype)

    whole = _spec((width, width), lambda i: (0, 0))
    return pl.pallas_call(
        body, name=name, grid=(n_t,), in_specs=[row, u_spec, row, vec, vec, w_spec],
        out_specs=[row, row, whole, vec, vec],
        out_shape=[jax.ShapeDtypeStruct((seq, width), F32), jax.ShapeDtypeStruct((seq, width), F32),
                   jax.ShapeDtypeStruct((width, width), GRAD_WIRE), jax.ShapeDtypeStruct((1, width), F32),
                   jax.ShapeDtypeStruct((1, width), F32)],
        scratch_shapes=[pltpu.VMEM((width, width), F32)],
        compiler_params=pltpu.CompilerParams(dimension_semantics=("arbitrary",),
                                             vmem_limit_bytes=_vmem_limit(80 * t_rows * width)),
    )(_in_hbm(yc), _in_hbm(proj), _in_hbm(d_out), d_skip, b_glu, w_glu)


def _sb_tri(kind):
    row = lax.broadcasted_iota(jnp.int32, (SB_BLOCK, SB_BLOCK), 0)
    col = lax.broadcasted_iota(jnp.int32, (SB_BLOCK, SB_BLOCK), 1)
    if kind == "after":
        return (row > col).astype(BF16)
    if kind == "from":
        return (row >= col).astype(BF16)
    return col < row


def _split_dot(x, m):
    hi = x.astype(BF16)
    lo = (x - hi.astype(F32)).astype(BF16)
    return (lax.dot_general(hi, m, NN, preferred_element_type=F32)
            + lax.dot_general(lo, m, NN, preferred_element_type=F32))


def _sb_scores(qh, k2):
    z = lax.dot_general(qh, k2, NT, preferred_element_type=F32)
    log_beta = jnp.minimum(z, 0.0) - jnp.log(1.0 + jnp.exp(-jnp.abs(z)))
    return log_beta, log_beta - z


def _sb_attention_fwd(proj, sb_width, beside=None):
    seq = proj.shape[0]
    n_pair, n_q = sb_width // LANES, seq // (SB_BLOCK * SB_GROUP)
    scale = 1.0 / (HEAD_DIM ** 0.5)
    chains = [(s, h) for s in range(SB_GROUP) for h in range(2)]

    def body(q_ref, k_ref, v_ref, o_ref, o32_ref):
        first = pl.program_id(1) * SB_GROUP
        lane = lax.broadcasted_iota(jnp.int32, (SB_BLOCK, LANES), 1)
        m_after, causal = _sb_tri("after"), _sb_tri("mask")
        heads = [lane < HEAD_DIM, lane >= HEAD_DIM]
        rows = [pl.ds(s * SB_BLOCK, SB_BLOCK) for s in range(SB_GROUP)]
        qh = {(s, h): (jnp.where(heads[h], q_ref[rows[s], :], 0.0) * scale).astype(BF16) for s, h in chains}

        def key_rows(s, r):
            kb = first + s - r
            return kb >= 0, pl.ds(pl.multiple_of(jnp.maximum(kb, 0) * SB_BLOCK, SB_BLOCK), SB_BLOCK)

        def scores(r, diag):
            out = []
            for s in range(SB_GROUP):
                live, ks = key_rows(s, r)
                k2 = k_ref[ks, :].astype(BF16)
                for h in range(2):
                    log_beta, log_1m = _sb_scores(qh[s, h], k2)
                    if diag:
                        log_1m = jnp.where(causal, log_1m, 0.0)
                    else:
                        log_1m = jnp.where(live, log_1m, 0.0)
                    out += [log_beta + _split_dot(log_1m, m_after), jnp.sum(log_1m, axis=1, keepdims=True)]
            return tuple(out)

        def weigh(r, sc, carry, acc, diag):
            out = []
            for c, (s, h) in enumerate(chains):
                live, ks = key_rows(s, r)
                v2 = v_ref[ks, :].astype(BF16)
                w = jnp.exp(sc[2 * c] + carry[c])
                w = jnp.where(causal, w, 0.0) if diag else jnp.where(live, w, 0.0)
                out.append(acc[c] + lax.dot_general(w.astype(BF16), v2, NN, preferred_element_type=F32))
            return tuple(out)

        zero = jnp.zeros((SB_BLOCK, LANES), F32)
        zcol = jnp.zeros((SB_BLOCK, 1), F32)
        sc = scores(0, True)
        acc = weigh(0, sc, (zcol,) * len(chains), (zero,) * len(chains), True)
        carry = tuple(sc[2 * c + 1] for c in range(len(chains)))
        last = first + SB_GROUP - 1

        def loop(st):
            r, carry, acc = st
            sc = scores(r, False)
            after = tuple(carry[c] + sc[2 * c + 1] for c in range(len(chains)))
            top = jnp.max(after[0])
            for c in range(1, len(chains)):
                top = jnp.maximum(top, jnp.max(after[c]))
            acc = weigh(r, sc, carry, acc, False)
            return jnp.where(top < SB_UNDERFLOW, last + 1, r + 1), after, acc

        _, _, acc = lax.while_loop(lambda st: st[0] <= last, loop, (1, carry, acc))
        for s in range(SB_GROUP):
            out = jnp.where(heads[0], acc[2 * s], acc[2 * s + 1])
            o_ref[rows[s], :] = out.astype(o_ref.dtype)
            o32_ref[rows[s], :] = out

    q_spec = _spec((SB_BLOCK * SB_GROUP, LANES), lambda h, i: (i, h))
    kv = [_spec((seq, LANES), lambda h, i, o=o: (0, o + h)) for o in (n_pair, 2 * n_pair)]
    return _call_beside(
        beside, body, "sb_attention_fwd", (n_pair, n_q), [q_spec] + kv, [q_spec, q_spec],
        [jax.ShapeDtypeStruct((seq, sb_width), BF16), jax.ShapeDtypeStruct((seq, sb_width), F32)], [],
        _vmem_limit(2 * seq * LANES * 4), (proj, proj, proj), ("parallel", "arbitrary"))


def _sb_attention_bwd(proj, o32, do, sb_width, beside=None):
    seq = proj.shape[0]
    n_pair, n_q = sb_width // LANES, seq // (SB_BLOCK * SB_GROUP)
    scale = 1.0 / (HEAD_DIM ** 0.5)
    chains = [(s, h) for s in range(SB_GROUP) for h in range(2)]
    n_c = len(chains)

    def body(q_ref, k_ref, v_ref, o_ref, do_ref, dq_ref, dk_out_ref, dv_out_ref, dk_ref, dv_ref):
        qi = pl.program_id(1)
        first = qi * SB_GROUP

        @pl.when(qi == 0)
        def _():
            dk_ref[...] = jnp.zeros_like(dk_ref)
            dv_ref[...] = jnp.zeros_like(dv_ref)

        lane = lax.broadcasted_iota(jnp.int32, (SB_BLOCK, LANES), 1)
        m_after, m_from, causal = _sb_tri("after"), _sb_tri("from"), _sb_tri("mask")
        heads = [lane < HEAD_DIM, lane >= HEAD_DIM]
        rows = [pl.ds(s * SB_BLOCK, SB_BLOCK) for s in range(SB_GROUP)]
        qh, doh_b, total = {}, {}, {}
        for s, h in chains:
            qh[s, h] = (jnp.where(heads[h], q_ref[rows[s], :], 0.0) * scale).astype(BF16)
            doh = jnp.where(heads[h], do_ref[rows[s], :].astype(F32), 0.0)
            doh_b[s, h] = doh.astype(BF16)
            total[s, h] = jnp.sum(doh * o_ref[rows[s], :], axis=1, keepdims=True)

        def key_rows(s, r):
            kb = first + s - r
            return kb >= 0, pl.ds(pl.multiple_of(jnp.maximum(kb, 0) * SB_BLOCK, SB_BLOCK), SB_BLOCK)

        def scores(r, diag):
            out = []
            for s in range(SB_GROUP):
                live, ks = key_rows(s, r)
                k2 = k_ref[ks, :].astype(BF16)
                v2 = v_ref[ks, :].astype(BF16)
                for h in range(2):
                    log_beta, log_1m = _sb_scores(qh[s, h], k2)
                    log_1m = jnp.where(causal, log_1m, 0.0) if diag else jnp.where(live, log_1m, 0.0)
                    out += [log_beta + _split_dot(log_1m, m_after), jnp.sum(log_1m, axis=1, keepdims=True),
                            lax.dot_general(doh_b[s, h], v2, NT, preferred_element_type=F32), log_beta]
            return tuple(out)

        def pull(r, sc, carry, right, dq, diag):
            right_out, dq_out = [], []
            for s in range(SB_GROUP):
                live, ks = key_rows(s, r)
                k2 = k_ref[ks, :].astype(BF16)
                dv_blk, dk_blk = None, None
                for h in range(2):
                    c = 2 * s + h
                    arg, _, d_w, log_beta = sc[4 * c:4 * c + 4]
                    w = jnp.exp(arg + carry[c])
                    w = jnp.where(causal, w, 0.0) if diag else jnp.where(live, w, 0.0)
                    w_b = w.astype(BF16)
                    d_arg = d_w * w_b.astype(F32)
                    dv_h = lax.dot_general(w_b, doh_b[s, h], TN, preferred_element_type=F32)
                    d_log_1m = total[s, h] - right[c] - _split_dot(d_arg, m_from)
                    beta = jnp.exp(log_beta)
                    dz = d_arg * (1.0 - beta) - beta * d_log_1m
                    dz = jnp.where(causal, dz, 0.0) if diag else jnp.where(live, dz, 0.0)
                    dz_b = dz.astype(BF16)
                    dk_h = lax.dot_general(dz_b, qh[s, h], TN, preferred_element_type=F32)
                    dv_blk = dv_h if h == 0 else dv_blk + dv_h
                    dk_blk = dk_h if h == 0 else dk_blk + dk_h
                    dq_out.append(dq[c] + lax.dot_general(dz_b, k2, NN, preferred_element_type=F32))
                    right_out.append(right[c] + jnp.sum(d_arg, axis=1, keepdims=True))
                dv_ref[ks, :] += dv_blk
                dk_ref[ks, :] += dk_blk
            return tuple(right_out), tuple(dq_out)

        zero = jnp.zeros((SB_BLOCK, LANES), F32)
        zcol = jnp.zeros((SB_BLOCK, 1), F32)
        sc = scores(0, True)
        right, dq = pull(0, sc, (zcol,) * n_c, (zcol,) * n_c, (zero,) * n_c, True)
        carry = tuple(sc[4 * c + 1] for c in range(n_c))
        last = first + SB_GROUP - 1

        def loop(st):
            r, carry, right, dq = st
            sc = scores(r, False)
            after = tuple(carry[c] + sc[4 * c + 1] for c in range(n_c))
            top = jnp.max(after[0])
            for c in range(1, n_c):
                top = jnp.maximum(top, jnp.max(after[c]))
            right, dq = pull(r, sc, carry, right, dq, False)
            return jnp.where(top < SB_UNDERFLOW, last + 1, r + 1), after, right, dq

        _, _, _, dq = lax.while_loop(lambda st: st[0] <= last, loop, (1, carry, right, dq))
        for s in range(SB_GROUP):
            dq_ref[rows[s], :] = (jnp.where(heads[0], dq[2 * s], dq[2 * s + 1]) * scale).astype(dq_ref.dtype)

        @pl.when(qi == n_q - 1)
        def _():
            dk_out_ref[...] = dk_ref[...].astype(dk_out_ref.dtype)
            dv_out_ref[...] = dv_ref[...].astype(dv_out_ref.dtype)

    q_spec = _spec((SB_BLOCK * SB_GROUP, LANES), lambda h, i: (i, h))
    kv = [_spec((seq, LANES), lambda h, i, o=o: (0, o + h)) for o in (n_pair, 2 * n_pair)]
    full = _spec((seq, LANES), lambda h, i: (0, h))
    return _call_beside(
        beside, body, "sb_attention_bwd", (n_pair, n_q), [q_spec] + kv + [q_spec, q_spec], [q_spec, full, full],
        [jax.ShapeDtypeStruct((seq, sb_width), BF16)] * 3,
        [pltpu.VMEM((seq, LANES), F32), pltpu.VMEM((seq, LANES), F32)],
        _vmem_limit(4 * seq * LANES * 4), (proj, proj, proj, o32, do), ("parallel", "arbitrary"))


def _s5_discretize(a_re, a_im, log_dt, b_re, b_im, c_re, c_im):
    n_g, n_p = a_re.shape
    c_g = b_re.shape[-1]
    ns = n_g // SLAB_GROUPS
    dt = jnp.exp(log_dt)[:, None]
    xr, xi = a_re * dt, a_im * dt
    mag = jnp.exp(xr)
    lr, li = mag * jnp.cos(xi), mag * jnp.sin(xi)
    den = a_re * a_re + a_im * a_im
    fr = ((lr - 1.0) * a_re + li * a_im) / den
    fi = (li * a_re - (lr - 1.0) * a_im) / den
    bb_re = fr[..., None] * b_re - fi[..., None] * b_im
    bb_im = fr[..., None] * b_im + fi[..., None] * b_re
    eye = jnp.eye(SLAB_GROUPS, dtype=F32)

    def diag_b(m):
        m = jnp.transpose(m.reshape(ns, SLAB_GROUPS, n_p, c_g), (0, 1, 3, 2))
        m = m[:, :, :, None, :] * eye[None, :, None, :, None]
        return m.reshape(ns, SLAB_GROUPS * c_g, SLAB_GROUPS * n_p)

    def diag_c(m):
        m = jnp.transpose(m.reshape(ns, SLAB_GROUPS, c_g, n_p), (0, 1, 3, 2))
        m = m[:, :, :, None, :] * eye[None, :, None, :, None]
        return m.reshape(ns, SLAB_GROUPS * n_p, SLAB_GROUPS * c_g)

    bs = jnp.concatenate([diag_b(bb_re), diag_b(bb_im)], axis=-1)
    cs = jnp.concatenate([diag_c(c_re), -diag_c(c_im)], axis=1)
    lam = jnp.concatenate([lr.reshape(ns, 1, -1), li.reshape(ns, 1, -1)], axis=-1)
    return bs, cs, lam


def _s5_powers(a_re, a_im, log_dt, n):
    n_g, n_p = a_re.shape
    ns = n_g // SLAB_GROUPS
    dt = jnp.exp(log_dt)[:, None]
    mag = jnp.exp(a_re * dt)
    base_r, base_i = mag * jnp.cos(a_im * dt), mag * jnp.sin(a_im * dt)
    steps = jnp.arange(1, n + 1, dtype=jnp.int32)[:, None, None]
    pr, pi = jnp.ones((n, n_g, n_p), F32), jnp.zeros((n, n_g, n_p), F32)
    for b in range(n.bit_length()):
        take = ((steps >> b) & 1) == 1
        pr, pi = (jnp.where(take, pr * base_r - pi * base_i, pr), jnp.where(take, pr * base_i + pi * base_r, pi))
        base_r, base_i = base_r * base_r - base_i * base_i, 2.0 * base_r * base_i

    def slabs(re, im):
        one = lambda m: jnp.transpose(m.reshape(n, ns, SLAB_GROUPS * n_p), (1, 0, 2))
        return jnp.concatenate([one(re), one(im)], axis=-1)

    return slabs(pr, pi), slabs(pr[::-1], -pi[::-1])


def _lanes(j):
    return slice(j * LANES, (j + 1) * LANES)


def _tile8(k):
    return pl.ds(pl.multiple_of(k * SUBLANES, SUBLANES), SUBLANES)


def _s5_interleave(dst_ref, src_ref, t_seg):
    def body(k, _):
        dst_ref[_tile8(k), :] = src_ref[pl.ds(k, SUBLANES, stride=t_seg), :]
        return 0

    lax.fori_loop(0, t_seg, body, 0, unroll=4)


def _s5_join_segments(st_ref, end_ref, car_ref, tab_ref, row, order, n_pair):
    for j in range(n_pair):
        re, im = _lanes(j), _lanes(n_pair + j)
        cr, ci = st_ref[:, re], st_ref[:, im]
        tr, ti = tab_ref[row:row + 1, re], tab_ref[row:row + 1, im]
        for s in order:
            car_ref[s:s + 1, re] = cr
            car_ref[s:s + 1, im] = ci
            er, ei = end_ref[s:s + 1, re], end_ref[s:s + 1, im]
            cr, ci = er + tr * cr - ti * ci, ei + tr * ci + ti * cr
        st_ref[:, re] = cr
        st_ref[:, im] = ci


def _s5_add_carries(buf_ref, car_ref, tab_ref, t_seg, n_pair):
    def fix(k, _):
        rows = _tile8(k)
        tab = tab_ref[pl.ds(k, 1), :]
        for j in range(n_pair):
            re, im = _lanes(j), _lanes(n_pair + j)
            cr, ci = car_ref[:, re], car_ref[:, im]
            tr, ti = tab[:, re], tab[:, im]
            buf_ref[rows, re] += tr * cr - ti * ci
            buf_ref[rows, im] += tr * ci + ti * cr
        return 0

    lax.fori_loop(0, t_seg, fix, 0, unroll=2)


def _s5_scan_fwd(proj, u_col, bs, cs, lam, pw, t_blk, beside=None):
    seq = proj.shape[0]
    ns, _, w2 = bs.shape
    n_pair = w2 // (2 * LANES)
    t_seg, n_t = t_blk // SUBLANES, seq // t_blk

    def body(u_ref, bs_ref, cs_ref, lam_ref, pw_ref, yc_ref, h_ref, st_ref, end_ref, car_ref, ui_ref, bu_ref, yi_ref):
        @pl.when(pl.program_id(1) == 0)
        def _():
            st_ref[...] = jnp.zeros_like(st_ref)

        _s5_interleave(ui_ref, u_ref, t_seg)
        bu_ref[...] = lax.dot_general(ui_ref[...].astype(BF16), bs_ref[...], NN, preferred_element_type=F32)
        lam_r = [jnp.broadcast_to(lam_ref[:, _lanes(j)], (SUBLANES, LANES)) for j in range(n_pair)]
        lam_i = [jnp.broadcast_to(lam_ref[:, _lanes(n_pair + j)], (SUBLANES, LANES)) for j in range(n_pair)]

        def step(k, c):
            rows = _tile8(k)
            out = []
            for j in range(n_pair):
                hr, hi = c[2 * j], c[2 * j + 1]
                nr = lam_r[j] * hr - lam_i[j] * hi + bu_ref[rows, _lanes(j)]
                ni = lam_i[j] * hr + lam_r[j] * hi + bu_ref[rows, _lanes(n_pair + j)]
                h_ref[rows, _lanes(j)] = nr
                h_ref[rows, _lanes(n_pair + j)] = ni
                out += [nr, ni]
            return tuple(out)

        ends = lax.fori_loop(0, t_seg, step, (jnp.zeros((SUBLANES, LANES), F32),) * (2 * n_pair), unroll=4)
        for j in range(n_pair):
            end_ref[:, _lanes(j)] = ends[2 * j]
            end_ref[:, _lanes(n_pair + j)] = ends[2 * j + 1]
        _s5_join_segments(st_ref, end_ref, car_ref, pw_ref, t_seg - 1, list(range(SUBLANES)), n_pair)
        _s5_add_carries(h_ref, car_ref, pw_ref, t_seg, n_pair)
        yi_ref[...] = lax.dot_general(h_ref[...].astype(BF16), cs_ref[...], NN, preferred_element_type=F32)

        def scatter(k, _):
            yc_ref[pl.ds(k, SUBLANES, stride=t_seg), :] = yi_ref[_tile8(k), :]
            return 0

        lax.fori_loop(0, t_seg, scatter, 0, unroll=4)

    return _call_beside(
        beside, body, "s5_scan_fwd", (ns, n_t),
        [_spec((t_blk, LANES), lambda s, i: (i, u_col + s)),
         _spec((None, LANES, w2), lambda s, i: (s, 0, 0)),
         _spec((None, w2, LANES), lambda s, i: (s, 0, 0)),
         _spec((None, 1, w2), lambda s, i: (s, 0, 0)),
         _spec((None, t_seg, w2), lambda s, i: (s, 0, 0))],
        [_spec((t_blk, LANES), lambda s, i: (i, s)),
         _spec((None, t_blk, w2), lambda s, i: (s, i, 0))],
        [jax.ShapeDtypeStruct((seq, ns * LANES), F32), jax.ShapeDtypeStruct((ns, seq, w2), F32)],
        [pltpu.VMEM((1, w2), F32), pltpu.VMEM((SUBLANES, w2), F32), pltpu.VMEM((SUBLANES, w2), F32),
         pltpu.VMEM((t_blk, LANES), F32), pltpu.VMEM((t_blk, w2), F32), pltpu.VMEM((t_blk, LANES), F32)],
        _vmem_limit(3 * t_blk * w2 * 4), (proj, bs, cs, lam, pw), ("parallel", "arbitrary"))


def _s5_scan_bwd(proj, u_col, states, d_yc, du_extra, bs, cs, lam, qw, t_blk):
    seq = proj.shape[0]
    ns, _, w2 = bs.shape
    n_pair = w2 // (2 * LANES)
    t_seg, n_t = t_blk // SUBLANES, seq // t_blk

    def body(u_ref, h_ref, hp_ref, dyc_ref, dux_ref, bs_ref, cs_ref, lam_ref, qw_ref,
             du_ref, dbs_ref, dcs_ref, dlam_ref, g_ref, gd_ref, st_ref, end_ref, car_ref, ui_ref, dyi_ref, dui_ref):
        i = pl.program_id(1)

        @pl.when(i == 0)
        def _():
            st_ref[...] = jnp.zeros_like(st_ref)
            dbs_ref[...] = jnp.zeros_like(dbs_ref)
            dcs_ref[...] = jnp.zeros_like(dcs_ref)
            dlam_ref[...] = jnp.zeros_like(dlam_ref)

        _s5_interleave(ui_ref, u_ref, t_seg)
        _s5_interleave(dyi_ref, dyc_ref, t_seg)
        dyc_b = dyi_ref[...].astype(BF16)
        gd_ref[...] = lax.dot_general(dyc_b, cs_ref[...], NT, preferred_element_type=F32)
        lam_r = [jnp.broadcast_to(lam_ref[:, _lanes(j)], (SUBLANES, LANES)) for j in range(n_pair)]
        lam_i = [jnp.broadcast_to(lam_ref[:, _lanes(n_pair + j)], (SUBLANES, LANES)) for j in range(n_pair)]

        def step(kk, c):
            rows = _tile8(t_seg - 1 - kk)
            out = []
            for j in range(n_pair):
                gr_n, gi_n = c[2 * j], c[2 * j + 1]
                gr = gd_ref[rows, _lanes(j)] + lam_r[j] * gr_n + lam_i[j] * gi_n
                gi = gd_ref[rows, _lanes(n_pair + j)] + lam_r[j] * gi_n - lam_i[j] * gr_n
                g_ref[rows, _lanes(j)] = gr
                g_ref[rows, _lanes(n_pair + j)] = gi
                out += [gr, gi]
            return tuple(out)

        zero = jnp.zeros((SUBLANES, LANES), F32)
        firsts = lax.fori_loop(0, t_seg, step, (zero,) * (2 * n_pair), unroll=4)
        for j in range(n_pair):
            end_ref[:, _lanes(j)] = firsts[2 * j]
            end_ref[:, _lanes(n_pair + j)] = firsts[2 * j + 1]
        _s5_join_segments(st_ref, end_ref, car_ref, qw_ref, 0, list(range(SUBLANES))[::-1], n_pair)
        _s5_add_carries(g_ref, car_ref, qw_ref, t_seg, n_pair)

        def pair_up(k, c):
            rows, prev = _tile8(k), _tile8(k - 1)
            out = []
            for j in range(n_pair):
                re, im = _lanes(j), _lanes(n_pair + j)
                gr, gi, hr, hi = g_ref[rows, re], g_ref[rows, im], h_ref[prev, re], h_ref[prev, im]
                out += [c[2 * j] + gr * hr + gi * hi, c[2 * j + 1] + gi * hr - gr * hi]
            return tuple(out)

        acc = lax.fori_loop(1, t_seg, pair_up, (zero,) * (2 * n_pair), unroll=4)
        has_prev = (i < n_t - 1).astype(F32)
        first_seg = lax.broadcasted_iota(jnp.int32, (SUBLANES, LANES), 0) == 0
        last = _tile8(t_seg - 1)
        for j in range(n_pair):
            re, im = _lanes(j), _lanes(n_pair + j)
            gr, gi = g_ref[0:SUBLANES, re], g_ref[0:SUBLANES, im]
            hr = jnp.where(first_seg, hp_ref[SUBLANES - 1:, re] * has_prev, pltpu.roll(h_ref[last, re], 1, 0))
            hi = jnp.where(first_seg, hp_ref[SUBLANES - 1:, im] * has_prev, pltpu.roll(h_ref[last, im], 1, 0))
            dlam_ref[:, re] += jnp.sum(acc[2 * j] + gr * hr + gi * hi, axis=0, keepdims=True)
            dlam_ref[:, im] += jnp.sum(acc[2 * j + 1] + gi * hr - gr * hi, axis=0, keepdims=True)

        g_b = g_ref[...].astype(BF16)
        dui_ref[...] = lax.dot_general(g_b, bs_ref[...], NT, preferred_element_type=F32)
        dbs_ref[...] += lax.dot_general(ui_ref[...].astype(BF16), g_b, TN, preferred_element_type=F32)
        dcs_ref[...] += lax.dot_general(h_ref[...].astype(BF16), dyc_b, TN, preferred_element_type=F32)

        def scatter(k, _):
            rows = pl.ds(k, SUBLANES, stride=t_seg)
            du_ref[rows, :] = (dui_ref[_tile8(k), :] + dux_ref[rows, :]).astype(du_ref.dtype)
            return 0

        lax.fori_loop(0, t_seg, scatter, 0, unroll=4)

    rev = lambda i: n_t - 1 - i
    return pl.pallas_call(
        body, name="s5_scan_bwd", grid=(ns, n_t),
        in_specs=[_spec((t_blk, LANES), lambda s, i: (rev(i), u_col + s)),
                  _spec((None, t_blk, w2), lambda s, i: (s, rev(i), 0)),
                  _spec((None, SUBLANES, w2), lambda s, i: (s, jnp.maximum(rev(i) * t_seg - 1, 0), 0)),
                  _spec((t_blk, LANES), lambda s, i: (rev(i), s)),
                  _spec((t_blk, LANES), lambda s, i: (rev(i), s)),
                  _spec((None, LANES, w2), lambda s, i: (s, 0, 0)),
                  _spec((None, w2, LANES), lambda s, i: (s, 0, 0)),
                  _spec((None, 1, w2), lambda s, i: (s, 0, 0)),
                  _spec((None, t_seg, w2), lambda s, i: (s, 0, 0))],
        out_specs=[_spec((t_blk, LANES), lambda s, i: (rev(i), s)),
                   _spec((None, LANES, w2), lambda s, i: (s, 0, 0)),
                   _spec((None, w2, LANES), lambda s, i: (s, 0, 0)),
                   _spec((None, 1, w2), lambda s, i: (s, 0, 0))],
        out_shape=[jax.ShapeDtypeStruct((seq, ns * LANES), F32), jax.ShapeDtypeStruct(bs.shape, F32),
                   jax.ShapeDtypeStruct(cs.shape, F32), jax.ShapeDtypeStruct(lam.shape, F32)],
        scratch_shapes=[pltpu.VMEM((t_blk, w2), F32), pltpu.VMEM((t_blk, w2), F32), pltpu.VMEM((1, w2), F32),
                        pltpu.VMEM((SUBLANES, w2), F32), pltpu.VMEM((SUBLANES, w2), F32),
                        pltpu.VMEM((t_blk, LANES), F32), pltpu.VMEM((t_blk, LANES), F32), pltpu.VMEM((t_blk, LANES), F32)],
        compiler_params=pltpu.CompilerParams(dimension_semantics=("parallel", "arbitrary"),
                                             vmem_limit_bytes=_vmem_limit(5 * t_blk * w2 * 4)),
    )(*[_in_hbm(a) for a in (proj, states, states, d_yc, du_extra, bs, cs, lam, qw)])


def _loss_head(y, target, t_m):
    seq, d = y.shape

    def body(y_ref, t_ref, loss_ref, dy_ref):
        @pl.when(pl.program_id(0) == 0)
        def _():
            loss_ref[...] = jnp.zeros_like(loss_ref)

        diff = y_ref[...] - t_ref[...]
        dy_ref[...] = diff / d
        loss_ref[...] += 0.5 * jnp.sum(diff * diff) / d

    row = _spec((t_m, d), lambda i: (i, 0))
    return pl.pallas_call(
        body, name="loss_head", grid=(seq // t_m,), in_specs=[row, row],
        out_specs=[_spec((SUBLANES, LANES), lambda i: (0, 0)), row],
        out_shape=[jax.ShapeDtypeStruct((SUBLANES, LANES), F32), jax.ShapeDtypeStruct((seq, d), F32)],
        compiler_params=pltpu.CompilerParams(dimension_semantics=("arbitrary",),
                                             vmem_limit_bytes=_vmem_limit(6 * t_m * d * 4)),
    )(_in_hbm(y), _in_hbm(target))


def _adamw_fn(w, m, v, *partials):
    g = partials[0]
    for p in partials[1:]:
        g = g + p
    m2 = ADAM_B1 * m + (1.0 - ADAM_B1) * g
    v2 = ADAM_B2 * v + (1.0 - ADAM_B2) * (g * g)
    m_hat = m2 / (1.0 - ADAM_B1 ** ADAM_STEP)
    v_hat = v2 / (1.0 - ADAM_B2 ** ADAM_STEP)
    delta = -ADAM_LR * (m_hat / (jnp.sqrt(v_hat) + ADAM_EPS) + ADAM_WD * w)
    return g, delta, m2, v2


def _adamw(name, w, m, v, partials):
    rows, cols = w.shape
    t_r = rows
    for cand in (512, 256, 128, 64, 32, 16, 8):
        if rows % cand == 0 and cand * cols * 4 <= (1 << 20):
            t_r = cand
            break
    n_p = partials.shape[0]
    row = lambda i: (i, 0)
    ins = [(a, (t_r, cols), row) for a in (w, m, v)]
    ins += [(partials, (None, t_r, cols), (lambda i, j=j: (j, i, 0))) for j in range(n_p)]
    outs = [((rows, cols), F32, (t_r, cols), row)] * 4
    return _rowwise(name, _adamw_fn, ins, outs, (rows // t_r,))


SMALL_PARAMS = ("b_ada", "ssm_a_re", "ssm_a_im", "ssm_log_dt", "ssm_b_re", "ssm_b_im", "ssm_c_re", "ssm_c_im",
                "ssm_d", "b_glu", "ln1_g", "ln1_b", "ln2_g", "ln2_b")
WEIGHTS = ("w_ada", "b_ada", "w_in", "w_sb_up", "ssm_a_re", "ssm_a_im", "ssm_log_dt", "ssm_b_re", "ssm_b_im",
           "ssm_c_re", "ssm_c_im", "ssm_d", "w_glu", "b_glu", "w_ssm_up", "w_out", "ln1_g", "ln1_b", "w_ffn_in",
           "w_ffn_out", "ln2_g", "ln2_b")
ARG_NAMES = (("x", "c") + WEIGHTS + ("loss_target",) + tuple("m_" + n for n in WEIGHTS)
             + tuple("v_" + n for n in WEIGHTS))


def _pack(arrs):
    flat = jnp.concatenate([a.reshape(-1) for a in arrs])
    pad = (-flat.shape[0]) % (PACK_ROWS * LANES)
    return jnp.pad(flat, (0, pad)).reshape(-1, LANES)


def _unpack(packed, like):
    lead = packed.shape[:-2]
    flat = packed.reshape(lead + (-1,))
    out, off = [], 0
    for a in like:
        out.append(flat[..., off:off + a.size].reshape(lead + a.shape))
        off += a.size
    return out


def kernel(x, c, w_ada, b_ada, w_in, w_sb_up, ssm_a_re, ssm_a_im, ssm_log_dt, ssm_b_re, ssm_b_im, ssm_c_re,
           ssm_c_im, ssm_d, w_glu, b_glu, w_ssm_up, w_out, ln1_g, ln1_b, w_ffn_in, w_ffn_out, ln2_g, ln2_b,
           loss_target, m_w_ada, m_b_ada, m_w_in, m_w_sb_up, m_ssm_a_re, m_ssm_a_im, m_ssm_log_dt, m_ssm_b_re,
           m_ssm_b_im, m_ssm_c_re, m_ssm_c_im, m_ssm_d, m_w_glu, m_b_glu, m_w_ssm_up, m_w_out, m_ln1_g, m_ln1_b,
           m_w_ffn_in, m_w_ffn_out, m_ln2_g, m_ln2_b, v_w_ada, v_b_ada, v_w_in, v_w_sb_up, v_ssm_a_re, v_ssm_a_im,
           v_ssm_log_dt, v_ssm_b_re, v_ssm_b_im, v_ssm_c_re, v_ssm_c_im, v_ssm_d, v_w_glu, v_b_glu, v_w_ssm_up,
           v_w_out, v_ln1_g, v_ln1_b, v_w_ffn_in, v_w_ffn_out, v_ln2_g, v_ln2_b):
    given = locals()
    return _train_step({n: given[n] for n in ARG_NAMES})


def _train_step(p):
    x0 = p["x"][0]
    target = p["loss_target"][0]
    seq, d = x0.shape
    depth = p["w_ada"].shape[0]
    n_ada = p["w_ada"].shape[2]
    n_in = p["w_in"].shape[2]
    sb_w = p["w_sb_up"].shape[1]
    ssm_w = p["w_ssm_up"].shape[1]
    n_up = p["w_sb_up"].shape[2]
    n_ffn = p["w_ffn_in"].shape[2]
    ffn = N_DEV * p["w_ffn_out"].shape[1]
    in_cols = N_DEV * n_in
    alpha = (2 * depth) ** 0.25
    resid_ln, resid_ln_mod = _make_resid_fns(alpha)
    t_r = min(512, seq)
    n_r = seq // t_r
    t_m = min(1024, seq)
    n_m = seq // t_m
    t_d = _tile(d)
    assert n_ffn * (N_DEV // 2) == ffn and sb_w % LANES == 0 and ssm_w % LANES == 0 and d % LANES == 0
    assert n_in % LANES == 0 and n_up % LANES == 0 and seq % t_m == 0 and in_cols == 3 * sb_w + ssm_w + 2 * d
    assert (3 * sb_w) % ssm_w == 0 and (3 * sb_w + ssm_w) % (2 * d) == 0
    assert sb_w % n_in == 0 and ssm_w % n_in == 0 and d % n_in == 0 and seq % (SB_BLOCK * SB_GROUP) == 0
    proj_starts = [c // n_in for c in (0, sb_w, 2 * sb_w, 3 * sb_w, 3 * sb_w + ssm_w)]

    bf = lambda a: a.astype(BF16)
    got = _exchange("gather_first", [], [bf(p["w_in"][0]), p["c"]])
    wg_in = [got[0]] + [None] * (depth - 1)
    c_all = got[1].reshape(N_DEV, d)
    small_names = ("w_sb_up", "w_ssm_up", "w_glu", "w_out")
    wg_ffn_in, wg_ffn_out, wg = [None] * depth, [None] * depth, {}

    c_pad = jnp.pad(c_all, ((0, 2 * SUBLANES - N_DEV), (0, 0)))
    c_act = _rowwise("silu_c", lambda v: v * jax.nn.sigmoid(v), [(c_pad, c_pad.shape, lambda i: (0, 0))],
                     [(c_pad.shape, F32, c_pad.shape, lambda i: (0, 0))], (1,))[0]
    rows_c = c_pad.shape[0]
    mod_cols = [
        _mm(f"mod_{l}", c_act, p["w_ada"],
            _spec((rows_c, d), lambda i, j, k: (0, 0)), _spec((None, d, n_ada), lambda i, j, k, l=l: (l, 0, 0)),
            _spec((rows_c, n_ada), lambda i, j, k: (0, 0)), (rows_c, n_ada), F32, (1, 1, 1), NN)
        for l in range(depth)]
    mod_send = jnp.stack([m[:N_DEV] for m in mod_cols], axis=1)
    mod_recv = _exchange("exchange_mod", [mod_send], [])[0]
    mod_nobias = jnp.swapaxes(mod_recv, 0, 1).reshape(depth, N_DEV * n_ada)
    full2 = lambda a: (a, a.shape, lambda i: (0, 0))
    mod = _rowwise("mod_bias", lambda a, b: a + b, [full2(mod_nobias), full2(p["b_ada"])],
                   [(mod_nobias.shape, F32, mod_nobias.shape, lambda i: (0, 0))], (1,))[0]
    vec = lambda a: a.reshape(1, -1)
    mods = [[vec(mod[l, j * d:(j + 1) * d]) for j in range(6)] for l in range(depth)]
    ln = {n: [vec(p[n][l]) for l in range(depth)] for n in ("ln1_g", "ln1_b", "ln2_g", "ln2_b")}

    row_spec = lambda width: ((t_r, width), lambda i: (i, 0))
    col_spec = lambda width, cb: ((t_r, width), lambda i, cb=cb: (i, cb))
    vec_spec = lambda width: ((1, width), lambda i: (0, 0))
    rows_in = lambda a: (a,) + row_spec(a.shape[1])
    vec_in = lambda a: (a,) + vec_spec(a.shape[1])
    row_out = lambda width, dt: ((seq, width), dt) + row_spec(width)

    s5 = [_s5_discretize(*[p[n][l] for n in ("ssm_a_re", "ssm_a_im", "ssm_log_dt", "ssm_b_re", "ssm_b_im",
                                               "ssm_c_re", "ssm_c_im")]) for l in range(depth)]
    s5_b16 = [(bs.astype(BF16), cs.astype(BF16), lam) for bs, cs, lam in s5]
    t_scan = min(1024, seq)
    s5_pw = [_s5_powers(p["ssm_a_re"][l], p["ssm_a_im"][l], p["ssm_log_dt"][l], t_scan // SUBLANES)
             for l in range(depth)]
    u_col = 3 * sb_w // LANES
    gates_cb = (3 * sb_w + ssm_w) // (2 * d)
    ssm_d = [vec(p["ssm_d"][l]) for l in range(depth)]
    b_glu = [vec(p["b_glu"][l]) for l in range(depth)]
    n_half = N_DEV // 2

    h = _rowwise("modulate_in", _modulate, [rows_in(x0), vec_in(mods[0][1]), vec_in(mods[0][0])],
                 [row_out(d, BF16)], (n_r,))[0]
    saved = []
    x_cur = x0
    for l in range(depth):
        sv = {"x_in": x_cur, "h": h}
        last = l == depth - 1
        t_n = _tile(n_in)
        r_n = n_in // t_n
        proj = _mm(f"proj_{l}", h, wg_in[l],
                   _spec((t_m, d), lambda i, j, k: (i, 0)),
                   _spec((None, d, t_n), lambda i, j, k, r=r_n: (j // r, 0, j % r)),
                   _spec((t_m, t_n), lambda i, j, k: (i, j)), (seq, in_cols), F32, (n_m, N_DEV * r_n, 1), NN,
                   reread=(True, True))
        arriving = [bf(p["w_ffn_in"][l])] + ([bf(p[n]) for n in small_names] if l == 0 else [])
        (o_sb, o_sb32), got = _sb_attention_fwd(proj, sb_w, beside=_Exchange(gather=arriving))
        wg_ffn_in[l] = got[0]
        if l == 0:
            wg = dict(zip(small_names, got[1:]))
            for n in ("w_glu", "w_out"):
                wg[n] = jnp.swapaxes(wg[n], 0, 1).reshape(depth, -1, wg[n].shape[-1])
            for n in ("w_sb_up", "w_ssm_up"):
                wg[n] = jnp.transpose(wg[n], (1, 2, 0, 3)).reshape(depth, wg[n].shape[2], d)
        bs16, cs16, lam = s5_b16[l]
        arriving = [bf(p["w_ffn_out"][l])] + ([] if last else [bf(p["w_in"][l + 1])])
        (yc, states), got = _s5_scan_fwd(proj, u_col, bs16, cs16, lam, s5_pw[l][0], t_scan,
                                         beside=_Exchange(gather=arriving))
        wg_ffn_out[l] = got[0].reshape(n_half, n_ffn, d)
        if not last:
            wg_in[l + 1] = got[1]
        s5_out = _s5_head(f"s5_head_{l}", yc, proj, 3 * sb_w // ssm_w, ssm_d[l], b_glu[l], wg["w_glu"], l, t_r)

        merged, y_sb, y_ssm = _up_merge(f"up_merge_{l}", o_sb, s5_out, proj, gates_cb, wg["w_sb_up"], wg["w_ssm_up"],
                                        l, t_r)
        y_mix = _mm(f"out_proj_{l}", merged, wg["w_out"],
                    _spec((t_m, d), lambda i, j, k: (i, 0)), _spec((None, d, t_d), lambda i, j, k, l=l: (l, 0, j)),
                    _spec((t_m, t_d), lambda i, j, k: (i, j)), (seq, d), F32, (n_m, d // t_d, 1), NN)
        vecs_a = [mods[l][2], ln["ln1_g"][l], ln["ln1_b"][l], mods[l][4], mods[l][3]]
        x_mid, h2 = _rowwise(f"resid_mix_{l}", resid_ln_mod, [rows_in(x_cur), rows_in(y_mix)] + [vec_in(v) for v in vecs_a],
                             [row_out(d, F32), row_out(d, BF16)], (n_r,))
        a_ffn, f_act = _ffn_in_swiglu(f"ffn_in_{l}", h2, wg_ffn_in[l], t_r)
        y_ffn = _mm(f"ffn_out_{l}", f_act, wg_ffn_out[l],
                    _spec((None, t_m, n_ffn), lambda i, j, k: (k, i, 0)),
                    _spec((None, n_ffn, t_d), lambda i, j, k: (k, 0, j)),
                    _spec((t_m, t_d), lambda i, j, k: (i, j)), (seq, d), F32, (n_m, d // t_d, n_half), NN)
        vecs_b = [mods[l][5], ln["ln2_g"][l], ln["ln2_b"][l]] + ([] if last else [mods[l + 1][1], mods[l + 1][0]])
        outs_b = [row_out(d, F32)] + ([] if last else [row_out(d, BF16)])
        res = _rowwise(f"resid_ffn_{l}", resid_ln if last else resid_ln_mod,
                       [rows_in(x_mid), rows_in(y_ffn)] + [vec_in(v) for v in vecs_b], outs_b, (n_r,))
        sv.update(proj=proj, o_sb=o_sb, o_sb32=o_sb32, yc=yc, states=states, s5_out=s5_out,
                  y_sb=y_sb, y_ssm=y_ssm, merged=merged, y_mix=y_mix, x_mid=x_mid, h2=h2, a_ffn=a_ffn, f_act=f_act,
                  y_ffn=y_ffn, vecs_a=vecs_a, vecs_b=vecs_b)
        saved.append(sv)
        x_cur = res[0]
        h = None if last else res[1]

    loss_part, d_x = _loss_head(x_cur, target, t_r)
    loss = lax.psum(loss_part[0, 0], MESH_AXES)

    d_h_next = None
    grads = {n: [None] * depth for n in WEIGHTS}
    d_mod = [[None] * 6 for _ in range(depth)]
    land = {}
    waiting = []
    row_wrt = lambda i, width, dt: (i, "row", (seq, width), dt) + row_spec(width)
    sum_wrt = lambda i, width: (i, "sum", (1, width), F32) + vec_spec(width)
    for l in reversed(range(depth)):
        sv = saved[l]
        last = l == depth - 1
        ins_b = [rows_in(sv["x_mid"]), rows_in(sv["y_ffn"])] + [vec_in(v) for v in sv["vecs_b"]]
        cts_b = [rows_in(d_x)] + ([] if last else [rows_in(d_h_next)])
        wrt_b = [row_wrt(0, d, F32), row_wrt(1, d, BF16)] + [sum_wrt(2 + j, d) for j in range(len(sv["vecs_b"]))]
        res = _rowwise_vjp(f"resid_ffn_bwd_{l}", resid_ln if last else resid_ln_mod, ins_b, cts_b, wrt_b, (n_r,))
        d_x_mid, d_y_ffn = res[0], res[1]
        d_mod[l][5], grads["ln2_g"][l], grads["ln2_b"][l] = res[2], res[3], res[4]
        if not last:
            d_mod[l + 1][1], d_mod[l + 1][0] = res[5], res[6]
        d_a = _ffn_out_dx_swiglu(f"ffn_out_dx_{l}", d_y_ffn, wg_ffn_out[l], sv["a_ffn"], t_r).reshape(N_DEV, seq, n_ffn)
        g_ffn_out = _mm(f"ffn_out_dw_{l}", sv["f_act"], d_y_ffn,
                        _spec((None, t_m, n_ffn), lambda i, j, k: (i, k, 0)), _spec((t_m, t_d), lambda i, j, k: (k, j)),
                        _spec((None, n_ffn, t_d), lambda i, j, k: (i, 0, j)), (n_half, n_ffn, d), GRAD_WIRE,
                        (n_half, d // t_d, n_m), TN, reread=(False, True))
        d_h2 = _mm(f"ffn_in_dx_{l}", d_a, wg_ffn_in[l],
                   _spec((None, t_m, n_ffn), lambda i, j, k: (k, i, 0)),
                   _spec((None, t_d, n_ffn), lambda i, j, k: (k, j, 0)),
                   _spec((t_m, t_d), lambda i, j, k: (i, j)), (seq, d), BRANCH_CT, (n_m, d // t_d, N_DEV), NT)
        g_ffn_in = _mm(f"ffn_in_dw_{l}", sv["h2"], d_a,
                       _spec((t_m, t_d), lambda i, j, k: (k, j)), _spec((None, t_m, n_ffn), lambda i, j, k: (i, k, 0)),
                       _spec((None, t_d, n_ffn), lambda i, j, k: (i, j, 0)), (N_DEV, d, n_ffn), GRAD_WIRE,
                       (N_DEV, d // t_d, n_m), TN, reread=(True, False))
        ins_a = [rows_in(sv["x_in"]), rows_in(sv["y_mix"])] + [vec_in(v) for v in sv["vecs_a"]]
        wrt_a = [row_wrt(0, d, F32), row_wrt(1, d, BF16)] + [sum_wrt(2 + j, d) for j in range(5)]
        res = _rowwise_vjp(f"resid_mix_bwd_{l}", resid_ln_mod, ins_a, [rows_in(d_x_mid), rows_in(d_h2)], wrt_a, (n_r,))
        d_x_in, d_y_mix = res[0], res[1]
        d_mod[l][2], grads["ln1_g"][l], grads["ln1_b"][l], d_mod[l][4], d_mod[l][3] = res[2:7]
        d_merged = _mm(f"out_proj_dx_{l}", d_y_mix, wg["w_out"],
                       _spec((t_m, d), lambda i, j, k: (i, 0)), _spec((None, t_d, d), lambda i, j, k, l=l: (l, j, 0)),
                       _spec((t_m, t_d), lambda i, j, k: (i, j)), (seq, d), BRANCH_CT, (n_m, d // t_d, 1), NT)
        g_out = _mm(f"out_proj_dw_{l}", sv["merged"], d_y_mix,
                    _spec((t_m, t_d), lambda i, j, k: (k, i)), _spec((t_m, t_d), lambda i, j, k: (k, j)),
                    _spec((t_d, t_d), lambda i, j, k: (i, j)), (d, d), GRAD_WIRE, (d // t_d, d // t_d, n_m), TN, reread=(d > t_d, d > t_d))
        gates = (sv["proj"],) + col_spec(2 * d, gates_cb)
        d_y_sb, d_y_ssm, d_gates = _rowwise_vjp(
            f"merge_bwd_{l}", _merge_fn, [rows_in(sv["y_sb"]), rows_in(sv["y_ssm"]), gates], [rows_in(d_merged)],
            [row_wrt(0, d, BF16), row_wrt(1, d, BF16), row_wrt(2, 2 * d, BF16)], (n_r,))

        def up_bwd(name, act, d_y, w, dx_dtype, l=l):
            k_w = act.shape[1]
            dx = _mm(name + "_dx", d_y, w, _spec((t_m, d), lambda i, j, k: (i, 0)),
                     _spec((None, k_w, d), lambda i, j, k: (l, 0, 0)),
                     _spec((t_m, k_w), lambda i, j, k: (i, 0)), (seq, k_w), dx_dtype, (n_m, 1, 1), NT)
            dw = _mm(name + "_dw", act, d_y, _spec((t_m, k_w), lambda i, j, k: (k, 0)),
                     _spec((t_m, t_d), lambda i, j, k: (k, j)),
                     _spec((k_w, t_d), lambda i, j, k: (0, j)), (k_w, d), GRAD_WIRE, (1, d // t_d, n_m), TN,
                     reread=(d > t_d, False))
            return dx, jnp.swapaxes(dw.reshape(k_w, N_DEV, n_up), 0, 1)

        d_o_sb, g_sb_up = up_bwd(f"sb_up_{l}", sv["o_sb"], d_y_sb, wg["w_sb_up"], BF16)
        d_s5_out, g_ssm_up = up_bwd(f"ssm_up_{l}", sv["s5_out"], d_y_ssm, wg["w_ssm_up"], BRANCH_CT)
        waiting += [("w_ffn_in", g_ffn_in), ("w_ffn_out", g_ffn_out.reshape(N_DEV, -1, d)),
                    ("w_out", g_out.reshape(N_DEV, -1, d)), ("w_sb_up", g_sb_up), ("w_ssm_up", g_ssm_up)]
        levels = [l + 1] * (len(waiting) - 5) + [l] * 5
        (d_q, d_k, d_v), got = _sb_attention_bwd(
            sv["proj"], sv["o_sb32"], d_o_sb, sb_w,
            beside=_Exchange(layered=[(g, lv, depth, land.get(n)) for (n, g), lv in zip(waiting, levels)]))
        land.update({n: buf for (n, _), buf in zip(waiting, got)})
        d_yc, d_u_skip, g_glu, grads["ssm_d"][l], grads["b_glu"][l] = _s5_head_bwd(
            f"s5_head_bwd_{l}", sv["yc"], sv["proj"], 3 * sb_w // ssm_w, d_s5_out, ssm_d[l], b_glu[l], wg["w_glu"], l, t_r)
        bs16, cs16, lam = s5_b16[l]
        d_u, d_bs, d_cs, d_lam = _s5_scan_bwd(sv["proj"], u_col, sv["states"], d_yc, d_u_skip, bs16, cs16, lam,
                                              s5_pw[l][1], t_scan)
        raw = [p[n][l] for n in ("ssm_a_re", "ssm_a_im", "ssm_log_dt", "ssm_b_re", "ssm_b_im", "ssm_c_re", "ssm_c_im")]
        _, pull = jax.vjp(_s5_discretize, *raw)
        (grads["ssm_a_re"][l], grads["ssm_a_im"][l], grads["ssm_log_dt"][l], grads["ssm_b_re"][l],
         grads["ssm_b_im"][l], grads["ssm_c_re"][l], grads["ssm_c_im"][l]) = pull((d_bs, d_cs, d_lam))
        d_proj = [d_q, d_k, d_v, d_u, d_gates]
        g_in = _mm_pieces(f"proj_dw_{l}", d_proj, proj_starts, n_in, lambda i, j, k: i, sv["h"],
                          _spec((t_m, t_d), lambda i, j, k: (k, j)), False, lambda i, j, k: k,
                          _spec((None, t_d, n_in), lambda i, j, k: (i, j, 0)), (N_DEV, d, n_in), GRAD_WIRE,
                          (N_DEV, d // t_d, n_m), TN)
        waiting = [("w_in", g_in), ("w_glu", g_glu.reshape(N_DEV, -1, ssm_w))]
        closing = _Exchange(layered=[(g, 0, depth, land.get(n)) for n, g in waiting]) if l == 0 else None
        d_h = _mm_pieces(f"proj_dx_{l}", d_proj, proj_starts, n_in, lambda i, j, k: k, wg_in[l],
                         _spec((None, t_d, n_in), lambda i, j, k: (k, j, 0)), True, lambda i, j, k: i,
                         _spec((t_m, t_d), lambda i, j, k: (i, j)), (seq, d), BRANCH_CT, (n_m, d // t_d, N_DEV), NT,
                         beside=closing)
        if l == 0:
            d_h, got = d_h
            land.update({n: buf for (n, _), buf in zip(waiting, got)})
        d_x, d_h_next = d_x_in, d_h
    res = _rowwise_vjp("modulate_in_bwd", lambda v, sc, sh: (v, _modulate(v, sc, sh)),
                       [rows_in(x0), vec_in(mods[0][1]), vec_in(mods[0][0])], [rows_in(d_x), rows_in(d_h_next)],
                       [row_wrt(0, d, F32), sum_wrt(1, d), sum_wrt(2, d)], (n_r,))
    grad_x, d_mod[0][1], d_mod[0][0] = res

    d_mod_rows = jnp.concatenate([jnp.concatenate(d_mod[l], axis=1) for l in range(depth)], axis=0)
    grads["b_ada"] = [d_mod_rows[l] for l in range(depth)]
    small_local = [jnp.stack([g.reshape(p[n].shape[1:]) for g in grads[n]]) for n in SMALL_PARAMS]
    d_mod_send = jnp.swapaxes(d_mod_rows.reshape(depth, N_DEV, n_ada), 0, 1)
    small_sum, (d_mod_cols,) = _reduce_packed("exchange_last", _pack(small_local), [d_mod_send])
    d_mod_pad = jnp.pad(jnp.swapaxes(d_mod_cols, 0, 1), ((0, 0), (0, rows_c - N_DEV), (0, 0)))
    g_ada = [
        _mm(f"mod_dw_{l}", c_act, d_mod_pad,
            _spec((rows_c, d), lambda i, j, k: (0, 0)), _spec((None, rows_c, n_ada), lambda i, j, k, l=l: (l, 0, 0)),
            _spec((d, n_ada), lambda i, j, k: (0, 0)), (d, n_ada), F32, (1, 1, 1), TN)
        for l in range(depth)]

    out = {}

    def update(name, partials):
        shape = p[name].shape
        two_d = lambda a: a.reshape(-1, shape[-1])
        res = _adamw("adamw_" + name, two_d(p[name]), two_d(p["m_" + name]), two_d(p["v_" + name]),
                     partials.reshape(partials.shape[0], -1, shape[-1]))
        out[name] = [r.reshape(shape) for r in res]

    update("w_ada", jnp.stack(g_ada)[None])
    for n in ("w_in", "w_sb_up", "w_ssm_up", "w_ffn_in", "w_glu", "w_out", "w_ffn_out"):
        update(n, land[n])
    small_w = [p[n] for n in SMALL_PARAMS]
    res = _adamw("adamw_small", _pack(small_w), _pack([p["m_" + n] for n in SMALL_PARAMS]),
                 _pack([p["v_" + n] for n in SMALL_PARAMS]), small_sum[None])
    for kind, packed in enumerate(res):
        for n, a in zip(SMALL_PARAMS, _unpack(packed, small_w)):
            out.setdefault(n, [None] * 4)[kind] = a

    return ((loss, grad_x[None]) + tuple(out[n][0] for n in WEIGHTS) + tuple(out[n][1] for n in WEIGHTS)
            + tuple(out[n][2] for n in WEIGHTS) + tuple(out[n][3] for n in WEIGHTS))
```

```python
import jax
import jax.numpy as jnp
from jax import lax
from jax.experimental import pallas as pl
from jax.experimental.pallas import tpu as pltpu

F32 = jnp.float32
BF16 = jnp.bfloat16
GRAD_WIRE = BF16
FFN_ACT = BF16
BRANCH_CT = BF16

N_DEV = 8
LANES = 128
SUBLANES = 8
VMEM_BYTES = 64 * 1024 * 1024
HEAD_DIM = 64
SB_BLOCK = 256
SB_GROUP = 4
SLAB_GROUPS = 8
LN_EPS = 1e-5
ADAM_LR, ADAM_B1, ADAM_B2, ADAM_EPS, ADAM_WD, ADAM_STEP = 0.001, 0.9, 0.999, 1e-08, 0.01, 10
SB_UNDERFLOW = -120.0

PACK_ROWS = 256
MESH_AXES = ("x", "y", "c")


def _vmem_limit(block_bytes):
    return int(min(max(3 * block_bytes + (8 << 20), 24 << 20), VMEM_BYTES - (8 << 20)))


def _nbytes(shape, dtype):
    n = 1
    for d in shape:
        if d is not None:
            n *= d
    return n * jnp.dtype(dtype).itemsize


def _spec(shape, fn):
    return pl.BlockSpec(shape, fn)


class _Exchange:
    def __init__(self, scatter=(), gather=(), layered=()):
        self.arrs = list(scatter) + [a for a, _, _, _ in layered] + list(gather)
        self.n = len(self.arrs)
        self.n_sc = len(scatter) + len(layered)
        self.layer = [None] * len(scatter) + [l for _, l, _, _ in layered] + [None] * len(gather)
        self.shapes = ([a.shape for a in scatter] + [(N_DEV, dp) + a.shape[1:] for a, _, dp, _ in layered]
                       + [(N_DEV,) + a.shape for a in gather])
        self.held = [(len(scatter) + i, b) for i, (_, _, _, b) in enumerate(layered) if b is not None]
        self.operands = self.arrs + [b for _, b in self.held]
        hbm = pl.BlockSpec(memory_space=pltpu.HBM)
        self.in_specs = [hbm] * len(self.operands)
        self.out_specs = [hbm] * self.n
        self.out_shape = [jax.ShapeDtypeStruct(s, a.dtype) for s, a in zip(self.shapes, self.arrs)]
        self.scratch = [pltpu.SemaphoreType.DMA((self.n, N_DEV - 1)), pltpu.SemaphoreType.DMA((self.n, N_DEV - 1)),
                        pltpu.SemaphoreType.DMA((self.n,))]

    def aliases(self, first_in, first_out):
        return {first_in + self.n + i: first_out + a for i, (a, _) in enumerate(self.held)}

    def copies(self, ins, outs, sems):
        send_sems, recv_sems, own_sems = sems
        x, y, c = lax.axis_index("x"), lax.axis_index("y"), lax.axis_index("c")
        me = 4 * x + 2 * y + c
        landing = [outs[a].at[me] if self.layer[a] is None else outs[a].at[me, self.layer[a]] for a in range(self.n)]
        out = [pltpu.make_async_copy(ins[a].at[me] if a < self.n_sc else ins[a], landing[a], own_sems.at[a])
               for a in range(self.n)]
        for k in range(1, N_DEV):
            px = 1 - x if k & 4 else x
            py = 1 - y if k & 2 else y
            pc = 1 - c if k & 1 else c
            peer = 4 * px + 2 * py + pc
            for a in range(self.n):
                out.append(pltpu.make_async_remote_copy(
                    src_ref=ins[a].at[peer] if a < self.n_sc else ins[a], dst_ref=landing[a],
                    send_sem=send_sems.at[a, k - 1], recv_sem=recv_sems.at[a, k - 1],
                    device_id=(px, py, pc), device_id_type=pl.DeviceIdType.MESH))
        return out


def _exchange(name, scatter, gather, layered=()):
    ex = _Exchange(scatter, gather, layered)

    def body(*refs):
        copies = ex.copies(refs[:ex.n], refs[len(ex.operands):len(ex.operands) + ex.n], refs[-3:])
        for cp in copies:
            cp.start()
        for cp in copies:
            cp.wait()

    return pl.pallas_call(body, name=name, in_specs=ex.in_specs, out_specs=ex.out_specs, out_shape=ex.out_shape,
                          input_output_aliases=ex.aliases(0, 0), scratch_shapes=ex.scratch)(*ex.operands)


def _reduce_packed(name, packed, scatter):
    rows = packed.shape[0]
    blk = rows // N_DEV
    ex = _Exchange(scatter=[packed.reshape(N_DEV, blk, LANES)] + list(scatter))
    n_in = len(ex.operands)

    def body(*refs):
        ins, outs = refs[:ex.n], refs[n_in:n_in + ex.n]
        total_ref = refs[n_in + ex.n]
        sems, (send2, recv2, own2, load_sem) = refs[n_in + ex.n + 1:n_in + ex.n + 4], refs[n_in + ex.n + 4:-2]
        land_v, sum_v = refs[-2:]
        copies = ex.copies(ins, outs, sems)
        for cp in copies:
            cp.start()
        for cp in copies:
            cp.wait()
        load = pltpu.make_async_copy(outs[0], land_v, load_sem)
        load.start()
        load.wait()
        acc = land_v[0]
        for i in range(1, N_DEV):
            acc = acc + land_v[i]
        sum_v[...] = acc
        x, y, c = lax.axis_index("x"), lax.axis_index("y"), lax.axis_index("c")
        me = 4 * x + 2 * y + c
        back = [pltpu.make_async_copy(sum_v, total_ref.at[me], own2)]
        for k in range(1, N_DEV):
            peer = (1 - x if k & 4 else x, 1 - y if k & 2 else y, 1 - c if k & 1 else c)
            back.append(pltpu.make_async_remote_copy(
                src_ref=sum_v, dst_ref=total_ref.at[me], send_sem=send2.at[k - 1], recv_sem=recv2.at[k - 1],
                device_id=peer, device_id_type=pl.DeviceIdType.MESH))
        for cp in back:
            cp.start()
        for cp in back:
            cp.wait()

    hbm = pl.BlockSpec(memory_space=pltpu.HBM)
    res = pl.pallas_call(
        body, name=name, in_specs=ex.in_specs, out_specs=ex.out_specs + [hbm],
        out_shape=ex.out_shape + [jax.ShapeDtypeStruct((N_DEV, blk, LANES), F32)],
        scratch_shapes=ex.scratch + [pltpu.SemaphoreType.DMA((N_DEV - 1,)), pltpu.SemaphoreType.DMA((N_DEV - 1,)),
                                     pltpu.SemaphoreType.DMA, pltpu.SemaphoreType.DMA,
                                     pltpu.VMEM((N_DEV, blk, LANES), F32), pltpu.VMEM((blk, LANES), F32)],
    )(*ex.operands)
    return res[-1].reshape(rows, LANES), res[1:-1]


def _call_beside(ex, body, name, grid, in_specs, out_specs, out_shape, scratch_shapes, vmem_bytes, operands,
                 semantics, in_hbm=True):
    if in_hbm:
        operands = [_in_hbm(a) for a in operands]
    if ex is None:
        res = pl.pallas_call(
            body, name=name, grid=grid, in_specs=in_specs, out_specs=out_specs, out_shape=out_shape,
            scratch_shapes=scratch_shapes,
            compiler_params=pltpu.CompilerParams(dimension_semantics=semantics, vmem_limit_bytes=vmem_bytes),
        )(*operands)
        return res, None
    n_in, n_out, n_scr = len(in_specs), len(out_specs), len(scratch_shapes)
    n_xin = len(ex.operands)

    def fused(*refs):
        mine = refs[:n_in] + refs[n_in + n_xin:n_in + n_xin + n_out]
        mine += refs[n_in + n_xin + n_out + ex.n:n_in + n_xin + n_out + ex.n + n_scr]
        first = pl.program_id(0) == 0
        last = pl.program_id(0) == grid[0] - 1
        for dim in range(1, len(grid)):
            first = jnp.logical_and(first, pl.program_id(dim) == 0)
            last = jnp.logical_and(last, pl.program_id(dim) == grid[dim] - 1)
        x_ins = refs[n_in:n_in + ex.n]
        x_outs = refs[n_in + n_xin + n_out:n_in + n_xin + n_out + ex.n]

        @pl.when(first)
        def _():
            for cp in ex.copies(x_ins, x_outs, refs[-3:]):
                cp.start()

        body(*mine)

        @pl.when(last)
        def _():
            for cp in ex.copies(x_ins, x_outs, refs[-3:]):
                cp.wait()

    res = pl.pallas_call(
        fused, name=name, grid=grid, in_specs=list(in_specs) + ex.in_specs, out_specs=list(out_specs) + ex.out_specs,
        out_shape=list(out_shape) + ex.out_shape, input_output_aliases=ex.aliases(n_in, n_out),
        scratch_shapes=list(scratch_shapes) + ex.scratch,
        compiler_params=pltpu.CompilerParams(dimension_semantics=("arbitrary",) * len(grid),
                                             vmem_limit_bytes=vmem_bytes),
    )(*operands, *ex.operands)
    return res[:n_out], res[n_out:]


NN = (((1,), (0,)), ((), ()))
NT = (((1,), (1,)), ((), ()))
TN = (((0,), (0,)), ((), ()))


def _in_hbm(a):
    return pltpu.with_memory_space_constraint(a, pltpu.HBM)


def _mm(name, a, b, a_spec, b_spec, o_spec, o_shape, o_dtype, grid, dims, beside=None, reread=(False, True)):
    nk = grid[2]
    a, b = (x if again else _in_hbm(x) for x, again in zip((a, b), reread))
    acc_shape = tuple(d for d in o_spec.block_shape if d is not None)

    def product(a_ref, b_ref):
        return lax.dot_general(a_ref[...].astype(BF16), b_ref[...].astype(BF16), dims, preferred_element_type=F32)

    def body_once(a_ref, b_ref, o_ref):
        o_ref[...] = product(a_ref, b_ref).astype(o_ref.dtype)

    def body(a_ref, b_ref, o_ref, acc_ref):
        k = pl.program_id(2)

        @pl.when(k == 0)
        def _():
            acc_ref[...] = product(a_ref, b_ref)

        @pl.when(k > 0)
        def _():
            acc_ref[...] += product(a_ref, b_ref)

        @pl.when(k == nk - 1)
        def _():
            o_ref[...] = acc_ref[...].astype(o_ref.dtype)

    blk = (_nbytes(a_spec.block_shape, a.dtype) + _nbytes(b_spec.block_shape, b.dtype)
           + _nbytes(acc_shape, o_dtype) + _nbytes(acc_shape, F32))
    res, got = _call_beside(
        beside, body_once if nk == 1 else body, name, grid, [a_spec, b_spec], [o_spec],
        [jax.ShapeDtypeStruct(o_shape, o_dtype)], [] if nk == 1 else [pltpu.VMEM(acc_shape, F32)],
        _vmem_limit(blk), (a, b), ("parallel", "parallel", "arbitrary"), in_hbm=False)
    return res[0] if beside is None else (res[0], got)


def _mm_pieces(name, pieces, starts, width, step_block, other, other_spec, pieces_first, piece_rows, o_spec, o_shape,
               o_dtype, grid, dims, beside=None):
    n_p, nk = len(pieces), grid[2]
    acc_shape = tuple(s for s in o_spec.block_shape if s is not None)

    def which(i, j, k):
        blk = step_block(i, j, k)
        idx = 0
        for s in starts[1:]:
            idx = idx + (blk >= s).astype(jnp.int32)
        return idx, blk

    def piece_spec(p, t_rows):
        def index(i, j, k):
            idx, blk = which(i, j, k)
            mine = idx == p
            return jnp.where(mine, piece_rows(i, j, k), 0), jnp.where(mine, blk - starts[p], 0)
        return _spec((t_rows, width), index)

    def body(*refs):
        p_refs = refs[:n_p] if pieces_first else refs[1:1 + n_p]
        other_ref = refs[n_p] if pieces_first else refs[0]
        o_ref, acc_ref = refs[n_p + 1], refs[n_p + 2]
        i, j, k = pl.program_id(0), pl.program_id(1), pl.program_id(2)

        @pl.when(k == 0)
        def _():
            acc_ref[...] = jnp.zeros_like(acc_ref)

        idx, _ = which(i, j, k)
        for p in range(n_p):
            @pl.when(idx == p)
            def _(p=p):
                mine, fixed = p_refs[p][...].astype(BF16), other_ref[...].astype(BF16)
                pair = (mine, fixed) if pieces_first else (fixed, mine)
                acc_ref[...] += lax.dot_general(pair[0], pair[1], dims, preferred_element_type=F32)

        @pl.when(k == nk - 1)
        def _():
            o_ref[...] = acc_ref[...].astype(o_ref.dtype)

    t_rows = other_spec.block_shape[-2] if not pieces_first else o_spec.block_shape[-2]
    specs = [piece_spec(p, t_rows) for p in range(n_p)]
    in_specs = specs + [other_spec] if pieces_first else [other_spec] + specs
    operands = list(pieces) + [other] if pieces_first else [other] + list(pieces)
    blk = (n_p * 4 * t_rows * width + _nbytes(other_spec.block_shape, other.dtype)
           + _nbytes(acc_shape, o_dtype) + _nbytes(acc_shape, F32))
    res, got = _call_beside(
        beside, body, name, grid, in_specs, [o_spec], [jax.ShapeDtypeStruct(o_shape, o_dtype)],
        [pltpu.VMEM(acc_shape, F32)], _vmem_limit(blk), operands, ("parallel", "parallel", "arbitrary"), in_hbm=False)
    return res[0] if beside is None else (res[0], got)


def _swiglu_fn(gate_up):
    gate, up = gate_up[0], gate_up[1]
    return gate * jax.nn.sigmoid(gate) * up


def _ffn_in_swiglu(name, h, w, t_m):
    seq, d = h.shape
    n_half, n = w.shape[0] // 2, w.shape[2]

    def body(h_ref, wg_ref, wu_ref, a_ref, f_ref):
        hb = h_ref[...]
        a_ref[0] = lax.dot_general(hb, wg_ref[...], NN, preferred_element_type=F32).astype(a_ref.dtype)
        a_ref[1] = lax.dot_general(hb, wu_ref[...], NN, preferred_element_type=F32).astype(a_ref.dtype)
        f_ref[...] = _swiglu_fn(a_ref[...].astype(F32)).astype(f_ref.dtype)

    blk = 2 * t_m * d + 4 * d * n + 6 * t_m * n + 12 * t_m * n
    return pl.pallas_call(
        body, name=name, grid=(seq // t_m, n_half),
        in_specs=[_spec((t_m, d), lambda i, j: (i, 0)), _spec((None, d, n), lambda i, j: (j, 0, 0)),
                  _spec((None, d, n), lambda i, j: (j + n_half, 0, 0))],
        out_specs=[_spec((2, None, t_m, n), lambda i, j: (0, j, i, 0)), _spec((None, t_m, n), lambda i, j: (j, i, 0))],
        out_shape=[jax.ShapeDtypeStruct((2, n_half, seq, n), FFN_ACT), jax.ShapeDtypeStruct((n_half, seq, n), BF16)],
        compiler_params=pltpu.CompilerParams(dimension_semantics=("parallel", "parallel"),
                                             vmem_limit_bytes=_vmem_limit(blk)),
    )(h, w, w)


def _ffn_out_dx_swiglu(name, d_y, w, a, t_m):
    seq, d = d_y.shape
    n_half, n = w.shape[0], w.shape[1]

    def body(dy_ref, w_ref, a_ref, da_ref):
        d_f = lax.dot_general(dy_ref[...], w_ref[...], NT, preferred_element_type=F32)
        gate, up = a_ref[0].astype(F32), a_ref[1].astype(F32)
        s = jax.nn.sigmoid(gate)
        gs = gate * s
        da_ref[0] = (d_f * up * (s + gs * (1.0 - s))).astype(da_ref.dtype)
        da_ref[1] = (d_f * gs).astype(da_ref.dtype)

    blk = 2 * t_m * d + 2 * d * n + 8 * t_m * n + 24 * t_m * n
    return pl.pallas_call(
        body, name=name, grid=(seq // t_m, n_half),
        in_specs=[_spec((t_m, d), lambda i, j: (i, 0)), _spec((None, n, d), lambda i, j: (j, 0, 0)),
                  _spec((2, None, t_m, n), lambda i, j: (0, j, i, 0))],
        out_specs=_spec((2, None, t_m, n), lambda i, j: (0, j, i, 0)),
        out_shape=jax.ShapeDtypeStruct((2, n_half, seq, n), BF16),
        compiler_params=pltpu.CompilerParams(dimension_semantics=("parallel", "parallel"),
                                             vmem_limit_bytes=_vmem_limit(blk)),
    )(d_y, w, a)


def _merge_fn(y_sb, y_ssm, gates):
    half = gates.shape[-1] // 2
    return jax.nn.sigmoid(gates[:, :half]) * y_sb + jax.nn.sigmoid(gates[:, half:]) * y_ssm


def _up_merge(name, o_sb, s5_out, proj, gates_cb, w_sb, w_ssm, layer, t_rows):
    seq = o_sb.shape[0]
    d = w_sb.shape[2]

    def body(o_ref, s_ref, g_ref, w1_ref, w2_ref, m_ref, y1_ref, y2_ref):
        y_sb = lax.dot_general(o_ref[...], w1_ref[...], NN, preferred_element_type=F32)
        y_ssm = lax.dot_general(s_ref[...], w2_ref[...], NN, preferred_element_type=F32)
        m_ref[...] = _merge_fn(y_sb, y_ssm, g_ref[...]).astype(m_ref.dtype)
        y1_ref[...] = y_sb.astype(y1_ref.dtype)
        y2_ref[...] = y_ssm.astype(y2_ref.dtype)

    row = lambda width: _spec((t_rows, width), lambda i: (i, 0))
    whole = lambda w: _spec((None,) + w.shape[1:], lambda i: (layer, 0, 0))
    blk = t_rows * (2 * o_sb.shape[1] + 2 * s5_out.shape[1] + 8 * d + 6 * d + 24 * d) + 4 * d * (o_sb.shape[1] + s5_out.shape[1])
    return pl.pallas_call(
        body, name=name, grid=(seq // t_rows,),
        in_specs=[row(o_sb.shape[1]), row(s5_out.shape[1]), _spec((t_rows, 2 * d), lambda i: (i, gates_cb)),
                  whole(w_sb), whole(w_ssm)],
        out_specs=[row(d)] * 3, out_shape=[jax.ShapeDtypeStruct((seq, d), BF16)] * 3,
        compiler_params=pltpu.CompilerParams(dimension_semantics=("parallel",), vmem_limit_bytes=_vmem_limit(blk)),
    )(_in_hbm(o_sb), _in_hbm(s5_out), _in_hbm(proj), w_sb, w_ssm)


def _tile(n, pref=1024):
    t = pref
    while t >= LANES:
        if n % t == 0:
            return t
        t -= LANES
    return n


def _rowwise(name, fn, ins, outs, grid):
    n_in = len(ins)

    def body(*refs):
        vals = fn(*[r[...].astype(F32) for r in refs[:n_in]])
        if not isinstance(vals, (tuple, list)):
            vals = (vals,)
        for r, v in zip(refs[n_in:], vals):
            r[...] = v.astype(r.dtype)

    blk = sum(_nbytes(bs, a.dtype) for a, bs, _ in ins) + sum(_nbytes(bs, d) + _nbytes(bs, F32) for _, d, bs, _ in outs)
    return pl.pallas_call(
        body, name=name, grid=grid,
        in_specs=[_spec(bs, im) for _, bs, im in ins],
        out_specs=[_spec(bs, im) for _, _, bs, im in outs],
        out_shape=[jax.ShapeDtypeStruct(s, d) for s, d, _, _ in outs],
        compiler_params=pltpu.CompilerParams(dimension_semantics=("parallel",) * len(grid),
                                             vmem_limit_bytes=_vmem_limit(2 * blk)),
    )(*[_in_hbm(a) for a, _, _ in ins])


def _rowwise_vjp(name, fn, ins, cts, wrt, grid):
    n_in, n_ct = len(ins), len(cts)
    idx = [w[0] for w in wrt]

    def body(*refs):
        prim = [r[...].astype(F32) for r in refs[:n_in]]
        ct = tuple(r[...].astype(F32) for r in refs[n_in:n_in + n_ct])
        o_refs = refs[n_in + n_ct:]

        def g(*sel):
            full = list(prim)
            for i, s in zip(idx, sel):
                full[i] = s
            out = fn(*full)
            return tuple(out) if isinstance(out, (tuple, list)) else (out,)

        _, pull = jax.vjp(g, *[prim[i] for i in idx])
        grads = pull(ct)
        first = pl.program_id(0) == 0
        for d in range(1, len(grid)):
            first = jnp.logical_and(first, pl.program_id(d) == 0)
        for w, o_ref, gr in zip(wrt, o_refs, grads):
            if w[1] == "row":
                o_ref[...] = gr.astype(o_ref.dtype)
            else:
                @pl.when(first)
                def _(o_ref=o_ref):
                    o_ref[...] = jnp.zeros_like(o_ref)

                o_ref[...] += gr.astype(o_ref.dtype)

    blk = (sum(_nbytes(bs, a.dtype) + _nbytes(bs, F32) for a, bs, _ in list(ins) + list(cts))
           + sum(_nbytes(w[4], w[3]) + _nbytes(w[4], F32) for w in wrt))
    return pl.pallas_call(
        body, name=name, grid=grid,
        in_specs=[_spec(bs, im) for _, bs, im in list(ins) + list(cts)],
        out_specs=[_spec(w[4], w[5]) for w in wrt],
        out_shape=[jax.ShapeDtypeStruct(w[2], w[3]) for w in wrt],
        compiler_params=pltpu.CompilerParams(dimension_semantics=("arbitrary",) * len(grid),
                                             vmem_limit_bytes=_vmem_limit(2 * blk)),
    )(*[_in_hbm(a) for a, _, _ in list(ins) + list(cts)])


def _normalize(x):
    mu = jnp.mean(x, axis=-1, keepdims=True)
    xc = x - mu
    var = jnp.mean(xc * xc, axis=-1, keepdims=True)
    return xc * lax.rsqrt(var + LN_EPS)


def _modulate(x, sc, sh):
    return _normalize(x) * (1.0 + sc) + sh


def _make_resid_fns(alpha):
    def resid_ln(x, y, gate, g, b):
        return _normalize(alpha * x + (1.0 + gate) * y) * g + b

    def resid_ln_mod(x, y, gate, g, b, sc, sh):
        xn = resid_ln(x, y, gate, g, b)
        return xn, _modulate(xn, sc, sh)

    return resid_ln, resid_ln_mod


def _s5_act_fn(yc, u, d_skip):
    return jax.nn.gelu(yc + d_skip * u)


def _s5_gate_fn(y1, t):
    return y1 * jax.nn.sigmoid(t)


def _s5_head_specs(yc, proj, u_cb, w_glu, layer, t_rows):
    width = yc.shape[1]
    row = _spec((t_rows, width), lambda i: (i, 0))
    u_spec = _spec((t_rows, width), lambda i: (i, u_cb))
    vec = _spec((1, width), lambda i: (0, 0))
    w_spec = _spec((None,) + w_glu.shape[1:], lambda i: (layer, 0, 0))
    return row, u_spec, vec, w_spec


def _s5_head(name, yc, proj, u_cb, d_skip, b_glu, w_glu, layer, t_rows):
    seq, width = yc.shape
    row, u_spec, vec, w_spec = _s5_head_specs(yc, proj, u_cb, w_glu, layer, t_rows)

    def body(yc_ref, u_ref, d_ref, b_ref, w_ref, o_ref):
        y1 = _s5_act_fn(yc_ref[...], u_ref[...], d_ref[...])
        t = lax.dot_general(y1.astype(BF16), w_ref[...], NN, preferred_element_type=F32) + b_ref[...]
        o_ref[...] = _s5_gate_fn(y1, t).astype(o_ref.dtype)

    return pl.pallas_call(
        body, name=name, grid=(seq // t_rows,), in_specs=[row, u_spec, vec, vec, w_spec], out_specs=row,
        out_shape=jax.ShapeDtypeStruct((seq, width), BF16),
        compiler_params=pltpu.CompilerParams(dimension_semantics=("parallel",),
                                             vmem_limit_bytes=_vmem_limit(40 * t_rows * width)),
    )(_in_hbm(yc), _in_hbm(proj), d_skip, b_glu, w_glu)


def _s5_head_bwd(name, yc, proj, u_cb, d_out, d_skip, b_glu, w_glu, layer, t_rows):
    seq, width = yc.shape
    row, u_spec, vec, w_spec = _s5_head_specs(yc, proj, u_cb, w_glu, layer, t_rows)
    n_t = seq // t_rows

    def body(yc_ref, u_ref, do_ref, d_ref, b_ref, w_ref, dyc_ref, du_ref, dw_ref, dd_ref, db_ref, acc_ref):
        i = pl.program_id(0)

        @pl.when(i == 0)
        def _():
            acc_ref[...] = jnp.zeros_like(acc_ref)
            dd_ref[...] = jnp.zeros_like(dd_ref)
            db_ref[...] = jnp.zeros_like(db_ref)

        y1, pull_act = jax.vjp(_s5_act_fn, yc_ref[...], u_ref[...], d_ref[...])
        y1_b = y1.astype(BF16)
        t = lax.dot_general(y1_b, w_ref[...], NN, preferred_element_type=F32) + b_ref[...]
        _, pull_gate = jax.vjp(_s5_gate_fn, y1, t)
        d_y1, d_t = pull_gate(do_ref[...].astype(F32))
        d_t_b = d_t.astype(BF16)
        d_y1 = d_y1 + lax.dot_general(d_t_b, w_ref[...], NT, preferred_element_type=F32)
        acc_ref[...] += lax.dot_general(y1_b, d_t_b, TN, preferred_element_type=F32)
        db_ref[...] += jnp.sum(d_t, axis=0, keepdims=True)
        d_yc, d_u, d_d = pull_act(d_y1)
        dyc_ref[...] = d_yc
        du_ref[...] = d_u
        dd_ref[...] += d_d

        @pl.when(i == n_t - 1)
        def _():
            dw_ref[...] = acc_ref[...].astype(dw_ref.dtype)

    whole = _spec((width, width), lambda i: (0, 0))
    return pl.pallas_call(
        body, name=name, grid=(n_t,), in_specs=[row, u_spec, row, vec, vec, w_spec],
        out_specs=[row, row, whole, vec, vec],
        out_shape=[jax.ShapeDtypeStruct((seq, width), F32), jax.ShapeDtypeStruct((seq, width), F32),
                   jax.ShapeDtypeStruct((width, width), GRAD_WIRE), jax.ShapeDtypeStruct((1, width), F32),
                   jax.ShapeDtypeStruct((1, width), F32)],
        scratch_shapes=[pltpu.VMEM((width, width), F32)],
        compiler_params=pltpu.CompilerParams(dimension_semantics=("arbitrary",),
                                             vmem_limit_bytes=_vmem_limit(80 * t_rows * width)),
    )(_in_hbm(yc), _in_hbm(proj), _in_hbm(d_out), d_skip, b_glu, w_glu)


def _sb_tri(kind):
    row = lax.broadcasted_iota(jnp.int32, (SB_BLOCK, SB_BLOCK), 0)
    col = lax.broadcasted_iota(jnp.int32, (SB_BLOCK, SB_BLOCK), 1)
    if kind == "after":
        return (row > col).astype(BF16)
    if kind == "from":
        return (row >= col).astype(BF16)
    return col < row


def _split_dot(x, m):
    hi = x.astype(BF16)
    lo = (x - hi.astype(F32)).astype(BF16)
    return (lax.dot_general(hi, m, NN, preferred_element_type=F32)
            + lax.dot_general(lo, m, NN, preferred_element_type=F32))


def _sb_scores(qh, k2):
    z = lax.dot_general(qh, k2, NT, preferred_element_type=F32)
    log_beta = jnp.minimum(z, 0.0) - jnp.log(1.0 + jnp.exp(-jnp.abs(z)))
    return log_beta, log_beta - z


def _sb_attention_fwd(proj, sb_width, beside=None):
    seq = proj.shape[0]
    n_pair, n_q = sb_width // LANES, seq // (SB_BLOCK * SB_GROUP)
    scale = 1.0 / (HEAD_DIM ** 0.5)
    chains = [(s, h) for s in range(SB_GROUP) for h in range(2)]

    def body(q_ref, k_ref, v_ref, o_ref, o32_ref):
        first = pl.program_id(1) * SB_GROUP
        lane = lax.broadcasted_iota(jnp.int32, (SB_BLOCK, LANES), 1)
        m_after, causal = _sb_tri("after"), _sb_tri("mask")
        heads = [lane < HEAD_DIM, lane >= HEAD_DIM]
        rows = [pl.ds(s * SB_BLOCK, SB_BLOCK) for s in range(SB_GROUP)]
        qh = {(s, h): (jnp.where(heads[h], q_ref[rows[s], :], 0.0) * scale).astype(BF16) for s, h in chains}

        def key_rows(s, r):
            kb = first + s - r
            return kb >= 0, pl.ds(pl.multiple_of(jnp.maximum(kb, 0) * SB_BLOCK, SB_BLOCK), SB_BLOCK)

        def scores(r, diag):
            out = []
            for s in range(SB_GROUP):
                live, ks = key_rows(s, r)
                k2 = k_ref[ks, :].astype(BF16)
                for h in range(2):
                    log_beta, log_1m = _sb_scores(qh[s, h], k2)
                    if diag:
                        log_1m = jnp.where(causal, log_1m, 0.0)
                    else:
                        log_1m = jnp.where(live, log_1m, 0.0)
                    out += [log_beta + _split_dot(log_1m, m_after), jnp.sum(log_1m, axis=1, keepdims=True)]
            return tuple(out)

        def weigh(r, sc, carry, acc, diag):
            out = []
            for c, (s, h) in enumerate(chains):
                live, ks = key_rows(s, r)
                v2 = v_ref[ks, :].astype(BF16)
                w = jnp.exp(sc[2 * c] + carry[c])
                w = jnp.where(causal, w, 0.0) if diag else jnp.where(live, w, 0.0)
                out.append(acc[c] + lax.dot_general(w.astype(BF16), v2, NN, preferred_element_type=F32))
            return tuple(out)

        zero = jnp.zeros((SB_BLOCK, LANES), F32)
        zcol = jnp.zeros((SB_BLOCK, 1), F32)
        sc = scores(0, True)
        acc = weigh(0, sc, (zcol,) * len(chains), (zero,) * len(chains), True)
        carry = tuple(sc[2 * c + 1] for c in range(len(chains)))
        last = first + SB_GROUP - 1

        def loop(st):
            r, carry, acc = st
            sc = scores(r, False)
            after = tuple(carry[c] + sc[2 * c + 1] for c in range(len(chains)))
            top = jnp.max(after[0])
            for c in range(1, len(chains)):
                top = jnp.maximum(top, jnp.max(after[c]))
            acc = weigh(r, sc, carry, acc, False)
            return jnp.where(top < SB_UNDERFLOW, last + 1, r + 1), after, acc

        _, _, acc = lax.while_loop(lambda st: st[0] <= last, loop, (1, carry, acc))
        for s in range(SB_GROUP):
            out = jnp.where(heads[0], acc[2 * s], acc[2 * s + 1])
            o_ref[rows[s], :] = out.astype(o_ref.dtype)
            o32_ref[rows[s], :] = out

    q_spec = _spec((SB_BLOCK * SB_GROUP, LANES), lambda h, i: (i, h))
    kv = [_spec((seq, LANES), lambda h, i, o=o: (0, o + h)) for o in (n_pair, 2 * n_pair)]
    return _call_beside(
        beside, body, "sb_attention_fwd", (n_pair, n_q), [q_spec] + kv, [q_spec, q_spec],
        [jax.ShapeDtypeStruct((seq, sb_width), BF16), jax.ShapeDtypeStruct((seq, sb_width), F32)], [],
        _vmem_limit(2 * seq * LANES * 4), (proj, proj, proj), ("parallel", "arbitrary"))


def _sb_attention_bwd(proj, o32, do, sb_width, beside=None):
    seq = proj.shape[0]
    n_pair, n_q = sb_width // LANES, seq // (SB_BLOCK * SB_GROUP)
    scale = 1.0 / (HEAD_DIM ** 0.5)
    chains = [(s, h) for s in range(SB_GROUP) for h in range(2)]
    n_c = len(chains)

    def body(q_ref, k_ref, v_ref, o_ref, do_ref, dq_ref, dk_out_ref, dv_out_ref, dk_ref, dv_ref):
        qi = pl.program_id(1)
        first = qi * SB_GROUP

        @pl.when(qi == 0)
        def _():
            dk_ref[...] = jnp.zeros_like(dk_ref)
            dv_ref[...] = jnp.zeros_like(dv_ref)

        lane = lax.broadcasted_iota(jnp.int32, (SB_BLOCK, LANES), 1)
        m_after, m_from, causal = _sb_tri("after"), _sb_tri("from"), _sb_tri("mask")
        heads = [lane < HEAD_DIM, lane >= HEAD_DIM]
        rows = [pl.ds(s * SB_BLOCK, SB_BLOCK) for s in range(SB_GROUP)]
        qh, doh_b, total = {}, {}, {}
        for s, h in chains:
            qh[s, h] = (jnp.where(heads[h], q_ref[rows[s], :], 0.0) * scale).astype(BF16)
            doh = jnp.where(heads[h], do_ref[rows[s], :].astype(F32), 0.0)
            doh_b[s, h] = doh.astype(BF16)
            total[s, h] = jnp.sum(doh * o_ref[rows[s], :], axis=1, keepdims=True)

        def key_rows(s, r):
            kb = first + s - r
            return kb >= 0, pl.ds(pl.multiple_of(jnp.maximum(kb, 0) * SB_BLOCK, SB_BLOCK), SB_BLOCK)

        def scores(r, diag):
            out = []
            for s in range(SB_GROUP):
                live, ks = key_rows(s, r)
                k2 = k_ref[ks, :].astype(BF16)
                v2 = v_ref[ks, :].astype(BF16)
                for h in range(2):
                    log_beta, log_1m = _sb_scores(qh[s, h], k2)
                    log_1m = jnp.where(causal, log_1m, 0.0) if diag else jnp.where(live, log_1m, 0.0)
                    out += [log_beta + _split_dot(log_1m, m_after), jnp.sum(log_1m, axis=1, keepdims=True),
                            lax.dot_general(doh_b[s, h], v2, NT, preferred_element_type=F32), log_beta]
            return tuple(out)

        def pull(r, sc, carry, right, dq, diag):
            right_out, dq_out = [], []
            for s in range(SB_GROUP):
                live, ks = key_rows(s, r)
                k2 = k_ref[ks, :].astype(BF16)
                dv_blk, dk_blk = None, None
                for h in range(2):
                    c = 2 * s + h
                    arg, _, d_w, log_beta = sc[4 * c:4 * c + 4]
                    w = jnp.exp(arg + carry[c])
                    w = jnp.where(causal, w, 0.0) if diag else jnp.where(live, w, 0.0)
                    w_b = w.astype(BF16)
                    d_arg = d_w * w_b.astype(F32)
                    dv_h = lax.dot_general(w_b, doh_b[s, h], TN, preferred_element_type=F32)
                    d_log_1m = total[s, h] - right[c] - _split_dot(d_arg, m_from)
                    beta = jnp.exp(log_beta)
                    dz = d_arg * (1.0 - beta) - beta * d_log_1m
                    dz = jnp.where(causal, dz, 0.0) if diag else jnp.where(live, dz, 0.0)
                    dz_b = dz.astype(BF16)
                    dk_h = lax.dot_general(dz_b, qh[s, h], TN, preferred_element_type=F32)
                    dv_blk = dv_h if h == 0 else dv_blk + dv_h
                    dk_blk = dk_h if h == 0 else dk_blk + dk_h
                    dq_out.append(dq[c] + lax.dot_general(dz_b, k2, NN, preferred_element_type=F32))
                    right_out.append(right[c] + jnp.sum(d_arg, axis=1, keepdims=True))
                dv_ref[ks, :] += dv_blk
                dk_ref[ks, :] += dk_blk
            return tuple(right_out), tuple(dq_out)

        zero = jnp.zeros((SB_BLOCK, LANES), F32)
        zcol = jnp.zeros((SB_BLOCK, 1), F32)
        sc = scores(0, True)
        right, dq = pull(0, sc, (zcol,) * n_c, (zcol,) * n_c, (zero,) * n_c, True)
        carry = tuple(sc[4 * c + 1] for c in range(n_c))
        last = first + SB_GROUP - 1

        def loop(st):
            r, carry, right, dq = st
            sc = scores(r, False)
            after = tuple(carry[c] + sc[4 * c + 1] for c in range(n_c))
            top = jnp.max(after[0])
            for c in range(1, n_c):
                top = jnp.maximum(top, jnp.max(after[c]))
            right, dq = pull(r, sc, carry, right, dq, False)
            return jnp.where(top < SB_UNDERFLOW, last + 1, r + 1), after, right, dq

        _, _, _, dq = lax.while_loop(lambda st: st[0] <= last, loop, (1, carry, right, dq))
        for s in range(SB_GROUP):
            dq_ref[rows[s], :] = (jnp.where(heads[0], dq[2 * s], dq[2 * s + 1]) * scale).astype(dq_ref.dtype)

        @pl.when(qi == n_q - 1)
        def _():
            dk_out_ref[...] = dk_ref[...].astype(dk_out_ref.dtype)
            dv_out_ref[...] = dv_ref[...].astype(dv_out_ref.dtype)

    q_spec = _spec((SB_BLOCK * SB_GROUP, LANES), lambda h, i: (i, h))
    kv = [_spec((seq, LANES), lambda h, i, o=o: (0, o + h)) for o in (n_pair, 2 * n_pair)]
    full = _spec((seq, LANES), lambda h, i: (0, h))
    return _call_beside(
        beside, body, "sb_attention_bwd", (n_pair, n_q), [q_spec] + kv + [q_spec, q_spec], [q_spec, full, full],
        [jax.ShapeDtypeStruct((seq, sb_width), BF16)] * 3,
        [pltpu.VMEM((seq, LANES), F32), pltpu.VMEM((seq, LANES), F32)],
        _vmem_limit(4 * seq * LANES * 4), (proj, proj, proj, o32, do), ("parallel", "arbitrary"))


def _s5_discretize(a_re, a_im, log_dt, b_re, b_im, c_re, c_im):
    n_g, n_p = a_re.shape
    c_g = b_re.shape[-1]
    ns = n_g // SLAB_GROUPS
    dt = jnp.exp(log_dt)[:, None]
    xr, xi = a_re * dt, a_im * dt
    mag = jnp.exp(xr)
    lr, li = mag * jnp.cos(xi), mag * jnp.sin(xi)
    den = a_re * a_re + a_im * a_im
    fr = ((lr - 1.0) * a_re + li * a_im) / den
    fi = (li * a_re - (lr - 1.0) * a_im) / den
    bb_re = fr[..., None] * b_re - fi[..., None] * b_im
    bb_im = fr[..., None] * b_im + fi[..., None] * b_re
    eye = jnp.eye(SLAB_GROUPS, dtype=F32)

    def diag_b(m):
        m = jnp.transpose(m.reshape(ns, SLAB_GROUPS, n_p, c_g), (0, 1, 3, 2))
        m = m[:, :, :, None, :] * eye[None, :, None, :, None]
        return m.reshape(ns, SLAB_GROUPS * c_g, SLAB_GROUPS * n_p)

    def diag_c(m):
        m = jnp.transpose(m.reshape(ns, SLAB_GROUPS, c_g, n_p), (0, 1, 3, 2))
        m = m[:, :, :, None, :] * eye[None, :, None, :, None]
        return m.reshape(ns, SLAB_GROUPS * n_p, SLAB_GROUPS * c_g)

    bs = jnp.concatenate([diag_b(bb_re), diag_b(bb_im)], axis=-1)
    cs = jnp.concatenate([diag_c(c_re), -diag_c(c_im)], axis=1)
    lam = jnp.concatenate([lr.reshape(ns, 1, -1), li.reshape(ns, 1, -1)], axis=-1)
    return bs, cs, lam


def _s5_powers(a_re, a_im, log_dt, n):
    n_g, n_p = a_re.shape
    ns = n_g // SLAB_GROUPS
    dt = jnp.exp(log_dt)[:, None]
    mag = jnp.exp(a_re * dt)
    base_r, base_i = mag * jnp.cos(a_im * dt), mag * jnp.sin(a_im * dt)
    steps = jnp.arange(1, n + 1, dtype=jnp.int32)[:, None, None]
    pr, pi = jnp.ones((n, n_g, n_p), F32), jnp.zeros((n, n_g, n_p), F32)
    for b in range(n.bit_length()):
        take = ((steps >> b) & 1) == 1
        pr, pi = (jnp.where(take, pr * base_r - pi * base_i, pr), jnp.where(take, pr * base_i + pi * base_r, pi))
        base_r, base_i = base_r * base_r - base_i * base_i, 2.0 * base_r * base_i

    def slabs(re, im):
        one = lambda m: jnp.transpose(m.reshape(n, ns, SLAB_GROUPS * n_p), (1, 0, 2))
        return jnp.concatenate([one(re), one(im)], axis=-1)

    return slabs(pr, pi), slabs(pr[::-1], -pi[::-1])


def _lanes(j):
    return slice(j * LANES, (j + 1) * LANES)


def _tile8(k):
    return pl.ds(pl.multiple_of(k * SUBLANES, SUBLANES), SUBLANES)


def _s5_interleave(dst_ref, src_ref, t_seg):
    def body(k, _):
        dst_ref[_tile8(k), :] = src_ref[pl.ds(k, SUBLANES, stride=t_seg), :]
        return 0

    lax.fori_loop(0, t_seg, body, 0, unroll=4)


def _s5_join_segments(st_ref, end_ref, car_ref, tab_ref, row, order, n_pair):
    for j in range(n_pair):
        re, im = _lanes(j), _lanes(n_pair + j)
        cr, ci = st_ref[:, re], st_ref[:, im]
        tr, ti = tab_ref[row:row + 1, re], tab_ref[row:row + 1, im]
        for s in order:
            car_ref[s:s + 1, re] = cr
            car_ref[s:s + 1, im] = ci
            er, ei = end_ref[s:s + 1, re], end_ref[s:s + 1, im]
            cr, ci = er + tr * cr - ti * ci, ei + tr * ci + ti * cr
        st_ref[:, re] = cr
        st_ref[:, im] = ci


def _s5_add_carries(buf_ref, car_ref, tab_ref, t_seg, n_pair):
    def fix(k, _):
        rows = _tile8(k)
        tab = tab_ref[pl.ds(k, 1), :]
        for j in range(n_pair):
            re, im = _lanes(j), _lanes(n_pair + j)
            cr, ci = car_ref[:, re], car_ref[:, im]
            tr, ti = tab[:, re], tab[:, im]
            buf_ref[rows, re] += tr * cr - ti * ci
            buf_ref[rows, im] += tr * ci + ti * cr
        return 0

    lax.fori_loop(0, t_seg, fix, 0, unroll=2)


def _s5_scan_fwd(proj, u_col, bs, cs, lam, pw, t_blk, beside=None):
    seq = proj.shape[0]
    ns, _, w2 = bs.shape
    n_pair = w2 // (2 * LANES)
    t_seg, n_t = t_blk // SUBLANES, seq // t_blk

    def body(u_ref, bs_ref, cs_ref, lam_ref, pw_ref, yc_ref, h_ref, st_ref, end_ref, car_ref, ui_ref, bu_ref, yi_ref):
        @pl.when(pl.program_id(1) == 0)
        def _():
            st_ref[...] = jnp.zeros_like(st_ref)

        _s5_interleave(ui_ref, u_ref, t_seg)
        bu_ref[...] = lax.dot_general(ui_ref[...].astype(BF16), bs_ref[...], NN, preferred_element_type=F32)
        lam_r = [jnp.broadcast_to(lam_ref[:, _lanes(j)], (SUBLANES, LANES)) for j in range(n_pair)]
        lam_i = [jnp.broadcast_to(lam_ref[:, _lanes(n_pair + j)], (SUBLANES, LANES)) for j in range(n_pair)]

        def step(k, c):
            rows = _tile8(k)
            out = []
            for j in range(n_pair):
                hr, hi = c[2 * j], c[2 * j + 1]
                nr = lam_r[j] * hr - lam_i[j] * hi + bu_ref[rows, _lanes(j)]
                ni = lam_i[j] * hr + lam_r[j] * hi + bu_ref[rows, _lanes(n_pair + j)]
                h_ref[rows, _lanes(j)] = nr
                h_ref[rows, _lanes(n_pair + j)] = ni
                out += [nr, ni]
            return tuple(out)

        ends = lax.fori_loop(0, t_seg, step, (jnp.zeros((SUBLANES, LANES), F32),) * (2 * n_pair), unroll=4)
        for j in range(n_pair):
            end_ref[:, _lanes(j)] = ends[2 * j]
            end_ref[:, _lanes(n_pair + j)] = ends[2 * j + 1]
        _s5_join_segments(st_ref, end_ref, car_ref, pw_ref, t_seg - 1, list(range(SUBLANES)), n_pair)
        _s5_add_carries(h_ref, car_ref, pw_ref, t_seg, n_pair)
        yi_ref[...] = lax.dot_general(h_ref[...].astype(BF16), cs_ref[...], NN, preferred_element_type=F32)

        def scatter(k, _):
            yc_ref[pl.ds(k, SUBLANES, stride=t_seg), :] = yi_ref[_tile8(k), :]
            return 0

        lax.fori_loop(0, t_seg, scatter, 0, unroll=4)

    return _call_beside(
        beside, body, "s5_scan_fwd", (ns, n_t),
        [_spec((t_blk, LANES), lambda s, i: (i, u_col + s)),
         _spec((None, LANES, w2), lambda s, i: (s, 0, 0)),
         _spec((None, w2, LANES), lambda s, i: (s, 0, 0)),
         _spec((None, 1, w2), lambda s, i: (s, 0, 0)),
         _spec((None, t_seg, w2), lambda s, i: (s, 0, 0))],
        [_spec((t_blk, LANES), lambda s, i: (i, s)),
         _spec((None, t_blk, w2), lambda s, i: (s, i, 0))],
        [jax.ShapeDtypeStruct((seq, ns * LANES), F32), jax.ShapeDtypeStruct((ns, seq, w2), F32)],
        [pltpu.VMEM((1, w2), F32), pltpu.VMEM((SUBLANES, w2), F32), pltpu.VMEM((SUBLANES, w2), F32),
         pltpu.VMEM((t_blk, LANES), F32), pltpu.VMEM((t_blk, w2), F32), pltpu.VMEM((t_blk, LANES), F32)],
        _vmem_limit(3 * t_blk * w2 * 4), (proj, bs, cs, lam, pw), ("parallel", "arbitrary"))


def _s5_scan_bwd(proj, u_col, states, d_yc, du_extra, bs, cs, lam, qw, t_blk, beside=None):
    seq = proj.shape[0]
    ns, _, w2 = bs.shape
    n_pair = w2 // (2 * LANES)
    t_seg, n_t = t_blk // SUBLANES, seq // t_blk

    def body(u_ref, h_ref, hp_ref, dyc_ref, dux_ref, bs_ref, cs_ref, lam_ref, qw_ref,
             du_ref, dbs_ref, dcs_ref, dlam_ref, g_ref, gd_ref, st_ref, end_ref, car_ref, ui_ref, dyi_ref, dui_ref):
        i = pl.program_id(1)

        @pl.when(i == 0)
        def _():
            st_ref[...] = jnp.zeros_like(st_ref)
            dbs_ref[...] = jnp.zeros_like(dbs_ref)
            dcs_ref[...] = jnp.zeros_like(dcs_ref)
            dlam_ref[...] = jnp.zeros_like(dlam_ref)

        _s5_interleave(ui_ref, u_ref, t_seg)
        _s5_interleave(dyi_ref, dyc_ref, t_seg)
        dyc_b = dyi_ref[...].astype(BF16)
        gd_ref[...] = lax.dot_general(dyc_b, cs_ref[...], NT, preferred_element_type=F32)
        lam_r = [jnp.broadcast_to(lam_ref[:, _lanes(j)], (SUBLANES, LANES)) for j in range(n_pair)]
        lam_i = [jnp.broadcast_to(lam_ref[:, _lanes(n_pair + j)], (SUBLANES, LANES)) for j in range(n_pair)]

        def step(kk, c):
            rows = _tile8(t_seg - 1 - kk)
            out = []
            for j in range(n_pair):
                gr_n, gi_n = c[2 * j], c[2 * j + 1]
                gr = gd_ref[rows, _lanes(j)] + lam_r[j] * gr_n + lam_i[j] * gi_n
                gi = gd_ref[rows, _lanes(n_pair + j)] + lam_r[j] * gi_n - lam_i[j] * gr_n
                g_ref[rows, _lanes(j)] = gr
                g_ref[rows, _lanes(n_pair + j)] = gi
                out += [gr, gi]
            return tuple(out)

        zero = jnp.zeros((SUBLANES, LANES), F32)
        firsts = lax.fori_loop(0, t_seg, step, (zero,) * (2 * n_pair), unroll=4)
        for j in range(n_pair):
            end_ref[:, _lanes(j)] = firsts[2 * j]
            end_ref[:, _lanes(n_pair + j)] = firsts[2 * j + 1]
        _s5_join_segments(st_ref, end_ref, car_ref, qw_ref, 0, list(range(SUBLANES))[::-1], n_pair)
        _s5_add_carries(g_ref, car_ref, qw_ref, t_seg, n_pair)

        def pair_up(k, c):
            rows, prev = _tile8(k), _tile8(k - 1)
            out = []
            for j in range(n_pair):
                re, im = _lanes(j), _lanes(n_pair + j)
                gr, gi, hr, hi = g_ref[rows, re], g_ref[rows, im], h_ref[prev, re], h_ref[prev, im]
                out += [c[2 * j] + gr * hr + gi * hi, c[2 * j + 1] + gi * hr - gr * hi]
            return tuple(out)

        acc = lax.fori_loop(1, t_seg, pair_up, (zero,) * (2 * n_pair), unroll=4)
        has_prev = (i < n_t - 1).astype(F32)
        first_seg = lax.broadcasted_iota(jnp.int32, (SUBLANES, LANES), 0) == 0
        last = _tile8(t_seg - 1)
        for j in range(n_pair):
            re, im = _lanes(j), _lanes(n_pair + j)
            gr, gi = g_ref[0:SUBLANES, re], g_ref[0:SUBLANES, im]
            hr = jnp.where(first_seg, hp_ref[SUBLANES - 1:, re] * has_prev, pltpu.roll(h_ref[last, re], 1, 0))
            hi = jnp.where(first_seg, hp_ref[SUBLANES - 1:, im] * has_prev, pltpu.roll(h_ref[last, im], 1, 0))
            dlam_ref[:, re] += jnp.sum(acc[2 * j] + gr * hr + gi * hi, axis=0, keepdims=True)
            dlam_ref[:, im] += jnp.sum(acc[2 * j + 1] + gi * hr - gr * hi, axis=0, keepdims=True)

        g_b = g_ref[...].astype(BF16)
        dui_ref[...] = lax.dot_general(g_b, bs_ref[...], NT, preferred_element_type=F32)
        dbs_ref[...] += lax.dot_general(ui_ref[...].astype(BF16), g_b, TN, preferred_element_type=F32)
        dcs_ref[...] += lax.dot_general(h_ref[...].astype(BF16), dyc_b, TN, preferred_element_type=F32)

        def scatter(k, _):
            rows = pl.ds(k, SUBLANES, stride=t_seg)
            du_ref[rows, :] = (dui_ref[_tile8(k), :] + dux_ref[rows, :]).astype(du_ref.dtype)
            return 0

        lax.fori_loop(0, t_seg, scatter, 0, unroll=4)

    rev = lambda i: n_t - 1 - i
    return _call_beside(
        beside, body, "s5_scan_bwd", (ns, n_t),
        [_spec((t_blk, LANES), lambda s, i: (rev(i), u_col + s)),
         _spec((None, t_blk, w2), lambda s, i: (s, rev(i), 0)),
         _spec((None, SUBLANES, w2), lambda s, i: (s, jnp.maximum(rev(i) * t_seg - 1, 0), 0)),
         _spec((t_blk, LANES), lambda s, i: (rev(i), s)),
         _spec((t_blk, LANES), lambda s, i: (rev(i), s)),
         _spec((None, LANES, w2), lambda s, i: (s, 0, 0)),
         _spec((None, w2, LANES), lambda s, i: (s, 0, 0)),
         _spec((None, 1, w2), lambda s, i: (s, 0, 0)),
         _spec((None, t_seg, w2), lambda s, i: (s, 0, 0))],
        [_spec((t_blk, LANES), lambda s, i: (rev(i), s)),
         _spec((None, LANES, w2), lambda s, i: (s, 0, 0)),
         _spec((None, w2, LANES), lambda s, i: (s, 0, 0)),
         _spec((None, 1, w2), lambda s, i: (s, 0, 0))],
        [jax.ShapeDtypeStruct((seq, ns * LANES), F32), jax.ShapeDtypeStruct(bs.shape, F32),
         jax.ShapeDtypeStruct(cs.shape, F32), jax.ShapeDtypeStruct(lam.shape, F32)],
        [pltpu.VMEM((t_blk, w2), F32), pltpu.VMEM((t_blk, w2), F32), pltpu.VMEM((1, w2), F32),
         pltpu.VMEM((SUBLANES, w2), F32), pltpu.VMEM((SUBLANES, w2), F32),
         pltpu.VMEM((t_blk, LANES), F32), pltpu.VMEM((t_blk, LANES), F32), pltpu.VMEM((t_blk, LANES), F32)],
        _vmem_limit(5 * t_blk * w2 * 4), (proj, states, states, d_yc, du_extra, bs, cs, lam, qw),
        ("parallel", "arbitrary"))


def _loss_head(y, target, t_m):
    seq, d = y.shape

    def body(y_ref, t_ref, loss_ref, dy_ref):
        @pl.when(pl.program_id(0) == 0)
        def _():
            loss_ref[...] = jnp.zeros_like(loss_ref)

        diff = y_ref[...] - t_ref[...]
        dy_ref[...] = diff / d
        loss_ref[...] += 0.5 * jnp.sum(diff * diff) / d

    row = _spec((t_m, d), lambda i: (i, 0))
    return pl.pallas_call(
        body, name="loss_head", grid=(seq // t_m,), in_specs=[row, row],
        out_specs=[_spec((SUBLANES, LANES), lambda i: (0, 0)), row],
        out_shape=[jax.ShapeDtypeStruct((SUBLANES, LANES), F32), jax.ShapeDtypeStruct((seq, d), F32)],
        compiler_params=pltpu.CompilerParams(dimension_semantics=("arbitrary",),
                                             vmem_limit_bytes=_vmem_limit(6 * t_m * d * 4)),
    )(_in_hbm(y), _in_hbm(target))


def _adamw_fn(w, m, v, *partials):
    g = partials[0]
    for p in partials[1:]:
        g = g + p
    m2 = ADAM_B1 * m + (1.0 - ADAM_B1) * g
    v2 = ADAM_B2 * v + (1.0 - ADAM_B2) * (g * g)
    m_hat = m2 / (1.0 - ADAM_B1 ** ADAM_STEP)
    v_hat = v2 / (1.0 - ADAM_B2 ** ADAM_STEP)
    delta = -ADAM_LR * (m_hat / (jnp.sqrt(v_hat) + ADAM_EPS) + ADAM_WD * w)
    return g, delta, m2, v2


def _adamw(name, w, m, v, partials):
    rows, cols = w.shape
    t_r = rows
    for cand in (512, 256, 128, 64, 32, 16, 8):
        if rows % cand == 0 and cand * cols * 4 <= (1 << 20):
            t_r = cand
            break
    n_p = partials.shape[0]
    row = lambda i: (i, 0)
    ins = [(a, (t_r, cols), row) for a in (w, m, v)]
    ins += [(partials, (None, t_r, cols), (lambda i, j=j: (j, i, 0))) for j in range(n_p)]
    outs = [((rows, cols), F32, (t_r, cols), row)] * 4
    return _rowwise(name, _adamw_fn, ins, outs, (rows // t_r,))


SMALL_PARAMS = ("b_ada", "ssm_a_re", "ssm_a_im", "ssm_log_dt", "ssm_b_re", "ssm_b_im", "ssm_c_re", "ssm_c_im",
                "ssm_d", "b_glu", "ln1_g", "ln1_b", "ln2_g", "ln2_b")
WEIGHTS = ("w_ada", "b_ada", "w_in", "w_sb_up", "ssm_a_re", "ssm_a_im", "ssm_log_dt", "ssm_b_re", "ssm_b_im",
           "ssm_c_re", "ssm_c_im", "ssm_d", "w_glu", "b_glu", "w_ssm_up", "w_out", "ln1_g", "ln1_b", "w_ffn_in",
           "w_ffn_out", "ln2_g", "ln2_b")
ARG_NAMES = (("x", "c") + WEIGHTS + ("loss_target",) + tuple("m_" + n for n in WEIGHTS)
             + tuple("v_" + n for n in WEIGHTS))


def _pack(arrs):
    flat = jnp.concatenate([a.reshape(-1) for a in arrs])
    pad = (-flat.shape[0]) % (PACK_ROWS * LANES)
    return jnp.pad(flat, (0, pad)).reshape(-1, LANES)


def _unpack(packed, like):
    lead = packed.shape[:-2]
    flat = packed.reshape(lead + (-1,))
    out, off = [], 0
    for a in like:
        out.append(flat[..., off:off + a.size].reshape(lead + a.shape))
        off += a.size
    return out


def kernel(x, c, w_ada, b_ada, w_in, w_sb_up, ssm_a_re, ssm_a_im, ssm_log_dt, ssm_b_re, ssm_b_im, ssm_c_re,
           ssm_c_im, ssm_d, w_glu, b_glu, w_ssm_up, w_out, ln1_g, ln1_b, w_ffn_in, w_ffn_out, ln2_g, ln2_b,
           loss_target, m_w_ada, m_b_ada, m_w_in, m_w_sb_up, m_ssm_a_re, m_ssm_a_im, m_ssm_log_dt, m_ssm_b_re,
           m_ssm_b_im, m_ssm_c_re, m_ssm_c_im, m_ssm_d, m_w_glu, m_b_glu, m_w_ssm_up, m_w_out, m_ln1_g, m_ln1_b,
           m_w_ffn_in, m_w_ffn_out, m_ln2_g, m_ln2_b, v_w_ada, v_b_ada, v_w_in, v_w_sb_up, v_ssm_a_re, v_ssm_a_im,
           v_ssm_log_dt, v_ssm_b_re, v_ssm_b_im, v_ssm_c_re, v_ssm_c_im, v_ssm_d, v_w_glu, v_b_glu, v_w_ssm_up,
           v_w_out, v_ln1_g, v_ln1_b, v_w_ffn_in, v_w_ffn_out, v_ln2_g, v_ln2_b):
    given = locals()
    return _train_step({n: given[n] for n in ARG_NAMES})


def _train_step(p):
    x0 = p["x"][0]
    target = p["loss_target"][0]
    seq, d = x0.shape
    depth = p["w_ada"].shape[0]
    n_ada = p["w_ada"].shape[2]
    n_in = p["w_in"].shape[2]
    sb_w = p["w_sb_up"].shape[1]
    ssm_w = p["w_ssm_up"].shape[1]
    n_up = p["w_sb_up"].shape[2]
    n_ffn = p["w_ffn_in"].shape[2]
    ffn = N_DEV * p["w_ffn_out"].shape[1]
    in_cols = N_DEV * n_in
    alpha = (2 * depth) ** 0.25
    resid_ln, resid_ln_mod = _make_resid_fns(alpha)
    t_r = min(512, seq)
    n_r = seq // t_r
    t_m = min(1024, seq)
    n_m = seq // t_m
    t_d = _tile(d)
    assert n_ffn * (N_DEV // 2) == ffn and sb_w % LANES == 0 and ssm_w % LANES == 0 and d % LANES == 0
    assert n_in % LANES == 0 and n_up % LANES == 0 and seq % t_m == 0 and in_cols == 3 * sb_w + ssm_w + 2 * d
    assert (3 * sb_w) % ssm_w == 0 and (3 * sb_w + ssm_w) % (2 * d) == 0
    assert sb_w % n_in == 0 and ssm_w % n_in == 0 and d % n_in == 0 and seq % (SB_BLOCK * SB_GROUP) == 0
    proj_starts = [c // n_in for c in (0, sb_w, 2 * sb_w, 3 * sb_w, 3 * sb_w + ssm_w)]

    bf = lambda a: a.astype(BF16)
    got = _exchange("gather_first", [], [bf(p["w_in"][0]), p["c"]])
    wg_in = [got[0]] + [None] * (depth - 1)
    c_all = got[1].reshape(N_DEV, d)
    small_names = ("w_sb_up", "w_ssm_up", "w_glu", "w_out")
    wg_ffn_in, wg_ffn_out, wg = [None] * depth, [None] * depth, {}

    c_pad = jnp.pad(c_all, ((0, 2 * SUBLANES - N_DEV), (0, 0)))
    c_act = _rowwise("silu_c", lambda v: v * jax.nn.sigmoid(v), [(c_pad, c_pad.shape, lambda i: (0, 0))],
                     [(c_pad.shape, F32, c_pad.shape, lambda i: (0, 0))], (1,))[0]
    rows_c = c_pad.shape[0]
    mod_cols = [
        _mm(f"mod_{l}", c_act, p["w_ada"],
            _spec((rows_c, d), lambda i, j, k: (0, 0)), _spec((None, d, n_ada), lambda i, j, k, l=l: (l, 0, 0)),
            _spec((rows_c, n_ada), lambda i, j, k: (0, 0)), (rows_c, n_ada), F32, (1, 1, 1), NN)
        for l in range(depth)]
    mod_send = jnp.stack([m[:N_DEV] for m in mod_cols], axis=1)
    mod_recv = _exchange("exchange_mod", [mod_send], [])[0]
    mod_nobias = jnp.swapaxes(mod_recv, 0, 1).reshape(depth, N_DEV * n_ada)
    full2 = lambda a: (a, a.shape, lambda i: (0, 0))
    mod = _rowwise("mod_bias", lambda a, b: a + b, [full2(mod_nobias), full2(p["b_ada"])],
                   [(mod_nobias.shape, F32, mod_nobias.shape, lambda i: (0, 0))], (1,))[0]
    vec = lambda a: a.reshape(1, -1)
    mods = [[vec(mod[l, j * d:(j + 1) * d]) for j in range(6)] for l in range(depth)]
    ln = {n: [vec(p[n][l]) for l in range(depth)] for n in ("ln1_g", "ln1_b", "ln2_g", "ln2_b")}

    row_spec = lambda width: ((t_r, width), lambda i: (i, 0))
    col_spec = lambda width, cb: ((t_r, width), lambda i, cb=cb: (i, cb))
    vec_spec = lambda width: ((1, width), lambda i: (0, 0))
    rows_in = lambda a: (a,) + row_spec(a.shape[1])
    vec_in = lambda a: (a,) + vec_spec(a.shape[1])
    row_out = lambda width, dt: ((seq, width), dt) + row_spec(width)

    s5 = [_s5_discretize(*[p[n][l] for n in ("ssm_a_re", "ssm_a_im", "ssm_log_dt", "ssm_b_re", "ssm_b_im",
                                               "ssm_c_re", "ssm_c_im")]) for l in range(depth)]
    s5_b16 = [(bs.astype(BF16), cs.astype(BF16), lam) for bs, cs, lam in s5]
    t_scan = min(1024, seq)
    s5_pw = [_s5_powers(p["ssm_a_re"][l], p["ssm_a_im"][l], p["ssm_log_dt"][l], t_scan // SUBLANES)
             for l in range(depth)]
    u_col = 3 * sb_w // LANES
    gates_cb = (3 * sb_w + ssm_w) // (2 * d)
    ssm_d = [vec(p["ssm_d"][l]) for l in range(depth)]
    b_glu = [vec(p["b_glu"][l]) for l in range(depth)]
    n_half = N_DEV // 2

    h = _rowwise("modulate_in", _modulate, [rows_in(x0), vec_in(mods[0][1]), vec_in(mods[0][0])],
                 [row_out(d, BF16)], (n_r,))[0]
    saved = []
    x_cur = x0
    for l in range(depth):
        sv = {"x_in": x_cur, "h": h}
        last = l == depth - 1
        t_n = _tile(n_in)
        r_n = n_in // t_n
        proj = _mm(f"proj_{l}", h, wg_in[l],
                   _spec((t_m, d), lambda i, j, k: (i, 0)),
                   _spec((None, d, t_n), lambda i, j, k, r=r_n: (j // r, 0, j % r)),
                   _spec((t_m, t_n), lambda i, j, k: (i, j)), (seq, in_cols), F32, (n_m, N_DEV * r_n, 1), NN,
                   reread=(True, True))
        arriving = [bf(p["w_ffn_in"][l])] + ([bf(p[n]) for n in small_names] if l == 0 else [])
        (o_sb, o_sb32), got = _sb_attention_fwd(proj, sb_w, beside=_Exchange(gather=arriving))
        wg_ffn_in[l] = got[0]
        if l == 0:
            wg = dict(zip(small_names, got[1:]))
            for n in ("w_glu", "w_out"):
                wg[n] = jnp.swapaxes(wg[n], 0, 1).reshape(depth, -1, wg[n].shape[-1])
            for n in ("w_sb_up", "w_ssm_up"):
                wg[n] = jnp.transpose(wg[n], (1, 2, 0, 3)).reshape(depth, wg[n].shape[2], d)
        bs16, cs16, lam = s5_b16[l]
        arriving = [bf(p["w_ffn_out"][l])] + ([] if last else [bf(p["w_in"][l + 1])])
        (yc, states), got = _s5_scan_fwd(proj, u_col, bs16, cs16, lam, s5_pw[l][0], t_scan,
                                         beside=_Exchange(gather=arriving))
        wg_ffn_out[l] = got[0].reshape(n_half, n_ffn, d)
        if not last:
            wg_in[l + 1] = got[1]
        s5_out = _s5_head(f"s5_head_{l}", yc, proj, 3 * sb_w // ssm_w, ssm_d[l], b_glu[l], wg["w_glu"], l, t_r)

        merged, y_sb, y_ssm = _up_merge(f"up_merge_{l}", o_sb, s5_out, proj, gates_cb, wg["w_sb_up"], wg["w_ssm_up"],
                                        l, t_r)
        y_mix = _mm(f"out_proj_{l}", merged, wg["w_out"],
                    _spec((t_m, d), lambda i, j, k: (i, 0)), _spec((None, d, t_d), lambda i, j, k, l=l: (l, 0, j)),
                    _spec((t_m, t_d), lambda i, j, k: (i, j)), (seq, d), F32, (n_m, d // t_d, 1), NN)
        vecs_a = [mods[l][2], ln["ln1_g"][l], ln["ln1_b"][l], mods[l][4], mods[l][3]]
        x_mid, h2 = _rowwise(f"resid_mix_{l}", resid_ln_mod, [rows_in(x_cur), rows_in(y_mix)] + [vec_in(v) for v in vecs_a],
                             [row_out(d, F32), row_out(d, BF16)], (n_r,))
        a_ffn, f_act = _ffn_in_swiglu(f"ffn_in_{l}", h2, wg_ffn_in[l], t_r)
        y_ffn = _mm(f"ffn_out_{l}", f_act, wg_ffn_out[l],
                    _spec((None, t_m, n_ffn), lambda i, j, k: (k, i, 0)),
                    _spec((None, n_ffn, t_d), lambda i, j, k: (k, 0, j)),
                    _spec((t_m, t_d), lambda i, j, k: (i, j)), (seq, d), F32, (n_m, d // t_d, n_half), NN)
        vecs_b = [mods[l][5], ln["ln2_g"][l], ln["ln2_b"][l]] + ([] if last else [mods[l + 1][1], mods[l + 1][0]])
        outs_b = [row_out(d, F32)] + ([] if last else [row_out(d, BF16)])
        res = _rowwise(f"resid_ffn_{l}", resid_ln if last else resid_ln_mod,
                       [rows_in(x_mid), rows_in(y_ffn)] + [vec_in(v) for v in vecs_b], outs_b, (n_r,))
        sv.update(proj=proj, o_sb=o_sb, o_sb32=o_sb32, yc=yc, states=states, s5_out=s5_out,
                  y_sb=y_sb, y_ssm=y_ssm, merged=merged, y_mix=y_mix, x_mid=x_mid, h2=h2, a_ffn=a_ffn, f_act=f_act,
                  y_ffn=y_ffn, vecs_a=vecs_a, vecs_b=vecs_b)
        saved.append(sv)
        x_cur = res[0]
        h = None if last else res[1]

    loss_part, d_x = _loss_head(x_cur, target, t_r)
    loss = lax.psum(loss_part[0, 0], MESH_AXES)

    d_h_next = None
    grads = {n: [None] * depth for n in WEIGHTS}
    d_mod = [[None] * 6 for _ in range(depth)]
    land = {}
    waiting = []
    row_wrt = lambda i, width, dt: (i, "row", (seq, width), dt) + row_spec(width)
    sum_wrt = lambda i, width: (i, "sum", (1, width), F32) + vec_spec(width)
    for l in reversed(range(depth)):
        sv = saved[l]
        last = l == depth - 1
        ins_b = [rows_in(sv["x_mid"]), rows_in(sv["y_ffn"])] + [vec_in(v) for v in sv["vecs_b"]]
        cts_b = [rows_in(d_x)] + ([] if last else [rows_in(d_h_next)])
        wrt_b = [row_wrt(0, d, F32), row_wrt(1, d, BF16)] + [sum_wrt(2 + j, d) for j in range(len(sv["vecs_b"]))]
        res = _rowwise_vjp(f"resid_ffn_bwd_{l}", resid_ln if last else resid_ln_mod, ins_b, cts_b, wrt_b, (n_r,))
        d_x_mid, d_y_ffn = res[0], res[1]
        d_mod[l][5], grads["ln2_g"][l], grads["ln2_b"][l] = res[2], res[3], res[4]
        if not last:
            d_mod[l + 1][1], d_mod[l + 1][0] = res[5], res[6]
        d_a = _ffn_out_dx_swiglu(f"ffn_out_dx_{l}", d_y_ffn, wg_ffn_out[l], sv["a_ffn"], t_r).reshape(N_DEV, seq, n_ffn)
        g_ffn_out = _mm(f"ffn_out_dw_{l}", sv["f_act"], d_y_ffn,
                        _spec((None, t_m, n_ffn), lambda i, j, k: (i, k, 0)), _spec((t_m, t_d), lambda i, j, k: (k, j)),
                        _spec((None, n_ffn, t_d), lambda i, j, k: (i, 0, j)), (n_half, n_ffn, d), GRAD_WIRE,
                        (n_half, d // t_d, n_m), TN, reread=(False, True))
        d_h2 = _mm(f"ffn_in_dx_{l}", d_a, wg_ffn_in[l],
                   _spec((None, t_m, n_ffn), lambda i, j, k: (k, i, 0)),
                   _spec((None, t_d, n_ffn), lambda i, j, k: (k, j, 0)),
                   _spec((t_m, t_d), lambda i, j, k: (i, j)), (seq, d), BRANCH_CT, (n_m, d // t_d, N_DEV), NT)
        g_ffn_in = _mm(f"ffn_in_dw_{l}", sv["h2"], d_a,
                       _spec((t_m, t_d), lambda i, j, k: (k, j)), _spec((None, t_m, n_ffn), lambda i, j, k: (i, k, 0)),
                       _spec((None, t_d, n_ffn), lambda i, j, k: (i, j, 0)), (N_DEV, d, n_ffn), GRAD_WIRE,
                       (N_DEV, d // t_d, n_m), TN, reread=(True, False))
        ins_a = [rows_in(sv["x_in"]), rows_in(sv["y_mix"])] + [vec_in(v) for v in sv["vecs_a"]]
        wrt_a = [row_wrt(0, d, F32), row_wrt(1, d, BF16)] + [sum_wrt(2 + j, d) for j in range(5)]
        res = _rowwise_vjp(f"resid_mix_bwd_{l}", resid_ln_mod, ins_a, [rows_in(d_x_mid), rows_in(d_h2)], wrt_a, (n_r,))
        d_x_in, d_y_mix = res[0], res[1]
        d_mod[l][2], grads["ln1_g"][l], grads["ln1_b"][l], d_mod[l][4], d_mod[l][3] = res[2:7]
        d_merged = _mm(f"out_proj_dx_{l}", d_y_mix, wg["w_out"],
                       _spec((t_m, d), lambda i, j, k: (i, 0)), _spec((None, t_d, d), lambda i, j, k, l=l: (l, j, 0)),
                       _spec((t_m, t_d), lambda i, j, k: (i, j)), (seq, d), BRANCH_CT, (n_m, d // t_d, 1), NT)
        g_out = _mm(f"out_proj_dw_{l}", sv["merged"], d_y_mix,
                    _spec((t_m, t_d), lambda i, j, k: (k, i)), _spec((t_m, t_d), lambda i, j, k: (k, j)),
                    _spec((t_d, t_d), lambda i, j, k: (i, j)), (d, d), GRAD_WIRE, (d // t_d, d // t_d, n_m), TN, reread=(d > t_d, d > t_d))
        gates = (sv["proj"],) + col_spec(2 * d, gates_cb)
        d_y_sb, d_y_ssm, d_gates = _rowwise_vjp(
            f"merge_bwd_{l}", _merge_fn, [rows_in(sv["y_sb"]), rows_in(sv["y_ssm"]), gates], [rows_in(d_merged)],
            [row_wrt(0, d, BF16), row_wrt(1, d, BF16), row_wrt(2, 2 * d, BF16)], (n_r,))

        def up_bwd(name, act, d_y, w, dx_dtype, l=l):
            k_w = act.shape[1]
            dx = _mm(name + "_dx", d_y, w, _spec((t_m, d), lambda i, j, k: (i, 0)),
                     _spec((None, k_w, d), lambda i, j, k: (l, 0, 0)),
                     _spec((t_m, k_w), lambda i, j, k: (i, 0)), (seq, k_w), dx_dtype, (n_m, 1, 1), NT)
            dw = _mm(name + "_dw", act, d_y, _spec((t_m, k_w), lambda i, j, k: (k, 0)),
                     _spec((t_m, t_d), lambda i, j, k: (k, j)),
                     _spec((k_w, t_d), lambda i, j, k: (0, j)), (k_w, d), GRAD_WIRE, (1, d // t_d, n_m), TN,
                     reread=(d > t_d, False))
            return dx, jnp.swapaxes(dw.reshape(k_w, N_DEV, n_up), 0, 1)

        d_o_sb, g_sb_up = up_bwd(f"sb_up_{l}", sv["o_sb"], d_y_sb, wg["w_sb_up"], BF16)
        d_s5_out, g_ssm_up = up_bwd(f"ssm_up_{l}", sv["s5_out"], d_y_ssm, wg["w_ssm_up"], BRANCH_CT)
        def round_of(entries):
            return _Exchange(layered=[(g, lv, depth, land.get(n)) for n, g, lv in entries])

        beside_attn = [e for e in waiting if e[0] == "w_glu"] + [
            ("w_ffn_in", g_ffn_in, l), ("w_ffn_out", g_ffn_out.reshape(N_DEV, -1, d), l)]
        beside_scan = [e for e in waiting if e[0] == "w_in"] + [
            ("w_out", g_out.reshape(N_DEV, -1, d), l), ("w_sb_up", g_sb_up, l), ("w_ssm_up", g_ssm_up, l)]
        (d_q, d_k, d_v), got = _sb_attention_bwd(sv["proj"], sv["o_sb32"], d_o_sb, sb_w, beside=round_of(beside_attn))
        land.update({e[0]: buf for e, buf in zip(beside_attn, got)})
        d_yc, d_u_skip, g_glu, grads["ssm_d"][l], grads["b_glu"][l] = _s5_head_bwd(
            f"s5_head_bwd_{l}", sv["yc"], sv["proj"], 3 * sb_w // ssm_w, d_s5_out, ssm_d[l], b_glu[l], wg["w_glu"], l, t_r)
        bs16, cs16, lam = s5_b16[l]
        (d_u, d_bs, d_cs, d_lam), got = _s5_scan_bwd(sv["proj"], u_col, sv["states"], d_yc, d_u_skip, bs16, cs16, lam,
                                                     s5_pw[l][1], t_scan, beside=round_of(beside_scan))
        land.update({e[0]: buf for e, buf in zip(beside_scan, got)})
        raw = [p[n][l] for n in ("ssm_a_re", "ssm_a_im", "ssm_log_dt", "ssm_b_re", "ssm_b_im", "ssm_c_re", "ssm_c_im")]
        _, pull = jax.vjp(_s5_discretize, *raw)
        (grads["ssm_a_re"][l], grads["ssm_a_im"][l], grads["ssm_log_dt"][l], grads["ssm_b_re"][l],
         grads["ssm_b_im"][l], grads["ssm_c_re"][l], grads["ssm_c_im"][l]) = pull((d_bs, d_cs, d_lam))
        d_proj = [d_q, d_k, d_v, d_u, d_gates]
        g_in = _mm_pieces(f"proj_dw_{l}", d_proj, proj_starts, n_in, lambda i, j, k: i, sv["h"],
                          _spec((t_m, t_d), lambda i, j, k: (k, j)), False, lambda i, j, k: k,
                          _spec((None, t_d, n_in), lambda i, j, k: (i, j, 0)), (N_DEV, d, n_in), GRAD_WIRE,
                          (N_DEV, d // t_d, n_m), TN)
        waiting = [("w_in", g_in, l), ("w_glu", g_glu.reshape(N_DEV, -1, ssm_w), l)]
        closing = round_of(waiting) if l == 0 else None
        d_h = _mm_pieces(f"proj_dx_{l}", d_proj, proj_starts, n_in, lambda i, j, k: k, wg_in[l],
                         _spec((None, t_d, n_in), lambda i, j, k: (k, j, 0)), True, lambda i, j, k: i,
                         _spec((t_m, t_d), lambda i, j, k: (i, j)), (seq, d), BRANCH_CT, (n_m, d // t_d, N_DEV), NT,
                         beside=closing)
        if l == 0:
            d_h, got = d_h
            land.update({e[0]: buf for e, buf in zip(waiting, got)})
        d_x, d_h_next = d_x_in, d_h
    res = _rowwise_vjp("modulate_in_bwd", lambda v, sc, sh: (v, _modulate(v, sc, sh)),
                       [rows_in(x0), vec_in(mods[0][1]), vec_in(mods[0][0])], [rows_in(d_x), rows_in(d_h_next)],
                       [row_wrt(0, d, F32), sum_wrt(1, d), sum_wrt(2, d)], (n_r,))
    grad_x, d_mod[0][1], d_mod[0][0] = res

    d_mod_rows = jnp.concatenate([jnp.concatenate(d_mod[l], axis=1) for l in range(depth)], axis=0)
    grads["b_ada"] = [d_mod_rows[l] for l in range(depth)]
    small_local = [jnp.stack([g.reshape(p[n].shape[1:]) for g in grads[n]]) for n in SMALL_PARAMS]
    d_mod_send = jnp.swapaxes(d_mod_rows.reshape(depth, N_DEV, n_ada), 0, 1)
    small_sum, (d_mod_cols,) = _reduce_packed("exchange_last", _pack(small_local), [d_mod_send])
    d_mod_pad = jnp.pad(jnp.swapaxes(d_mod_cols, 0, 1), ((0, 0), (0, rows_c - N_DEV), (0, 0)))
    g_ada = [
        _mm(f"mod_dw_{l}", c_act, d_mod_pad,
            _spec((rows_c, d), lambda i, j, k: (0, 0)), _spec((None, rows_c, n_ada), lambda i, j, k, l=l: (l, 0, 0)),
            _spec((d, n_ada), lambda i, j, k: (0, 0)), (d, n_ada), F32, (1, 1, 1), TN)
        for l in range(depth)]

    out = {}

    def update(name, partials):
        shape = p[name].shape
        two_d = lambda a: a.reshape(-1, shape[-1])
        res = _adamw("adamw_" + name, two_d(p[name]), two_d(p["m_" + name]), two_d(p["v_" + name]),
                     partials.reshape(partials.shape[0], -1, shape[-1]))
        out[name] = [r.reshape(shape) for r in res]

    update("w_ada", jnp.stack(g_ada)[None])
    for n in ("w_in", "w_sb_up", "w_ssm_up", "w_ffn_in", "w_glu", "w_out", "w_ffn_out"):
        update(n, land[n])
    small_w = [p[n] for n in SMALL_PARAMS]
    res = _adamw("adamw_small", _pack(small_w), _pack([p["m_" + n] for n in SMALL_PARAMS]),
                 _pack([p["v_" + n] for n in SMALL_PARAMS]), small_sum[None])
    for kind, packed in enumerate(res):
        for n, a in zip(SMALL_PARAMS, _unpack(packed, small_w)):
            out.setdefault(n, [None] * 4)[kind] = a

    return ((loss, grad_x[None]) + tuple(out[n][0] for n in WEIGHTS) + tuple(out[n][1] for n in WEIGHTS)
            + tuple(out[n][2] for n in WEIGHTS) + tuple(out[n][3] for n in WEIGHTS))
```

```python
import jax
import jax.numpy as jnp
from jax import lax
from jax.experimental import pallas as pl
from jax.experimental.pallas import tpu as pltpu

F32 = jnp.float32
BF16 = jnp.bfloat16
GRAD_WIRE = BF16
FFN_ACT = BF16
BRANCH_CT = BF16

N_DEV = 8
LANES = 128
SUBLANES = 8
VMEM_BYTES = 64 * 1024 * 1024
HEAD_DIM = 64
SB_BLOCK = 256
SB_GROUP = 2
SLAB_GROUPS = 8
LN_EPS = 1e-5
ADAM_LR, ADAM_B1, ADAM_B2, ADAM_EPS, ADAM_WD, ADAM_STEP = 0.001, 0.9, 0.999, 1e-08, 0.01, 10
SB_UNDERFLOW = -120.0

PACK_ROWS = 256
MESH_AXES = ("x", "y", "c")


def _vmem_limit(block_bytes):
    return int(min(max(3 * block_bytes + (8 << 20), 24 << 20), VMEM_BYTES - (8 << 20)))


def _nbytes(shape, dtype):
    n = 1
    for d in shape:
        if d is not None:
            n *= d
    return n * jnp.dtype(dtype).itemsize


def _spec(shape, fn):
    return pl.BlockSpec(shape, fn)


class _Exchange:
    def __init__(self, scatter=(), gather=(), layered=()):
        self.arrs = list(scatter) + [a for a, _, _, _ in layered] + list(gather)
        self.n = len(self.arrs)
        self.n_sc = len(scatter) + len(layered)
        self.layer = [None] * len(scatter) + [l for _, l, _, _ in layered] + [None] * len(gather)
        self.shapes = ([a.shape for a in scatter] + [(N_DEV, dp) + a.shape[1:] for a, _, dp, _ in layered]
                       + [(N_DEV,) + a.shape for a in gather])
        self.held = [(len(scatter) + i, b) for i, (_, _, _, b) in enumerate(layered) if b is not None]
        self.operands = self.arrs + [b for _, b in self.held]
        hbm = pl.BlockSpec(memory_space=pltpu.HBM)
        self.in_specs = [hbm] * len(self.operands)
        self.out_specs = [hbm] * self.n
        self.out_shape = [jax.ShapeDtypeStruct(s, a.dtype) for s, a in zip(self.shapes, self.arrs)]
        self.scratch = [pltpu.SemaphoreType.DMA((self.n, N_DEV - 1)), pltpu.SemaphoreType.DMA((self.n, N_DEV - 1)),
                        pltpu.SemaphoreType.DMA((self.n,))]

    def aliases(self, first_in, first_out):
        return {first_in + self.n + i: first_out + a for i, (a, _) in enumerate(self.held)}

    def copies(self, ins, outs, sems):
        send_sems, recv_sems, own_sems = sems
        x, y, c = lax.axis_index("x"), lax.axis_index("y"), lax.axis_index("c")
        me = 4 * x + 2 * y + c
        landing = [outs[a].at[me] if self.layer[a] is None else outs[a].at[me, self.layer[a]] for a in range(self.n)]
        out = [pltpu.make_async_copy(ins[a].at[me] if a < self.n_sc else ins[a], landing[a], own_sems.at[a])
               for a in range(self.n)]
        for k in range(1, N_DEV):
            px = 1 - x if k & 4 else x
            py = 1 - y if k & 2 else y
            pc = 1 - c if k & 1 else c
            peer = 4 * px + 2 * py + pc
            for a in range(self.n):
                out.append(pltpu.make_async_remote_copy(
                    src_ref=ins[a].at[peer] if a < self.n_sc else ins[a], dst_ref=landing[a],
                    send_sem=send_sems.at[a, k - 1], recv_sem=recv_sems.at[a, k - 1],
                    device_id=(px, py, pc), device_id_type=pl.DeviceIdType.MESH))
        return out


def _exchange(name, scatter, gather, layered=()):
    ex = _Exchange(scatter, gather, layered)

    def body(*refs):
        copies = ex.copies(refs[:ex.n], refs[len(ex.operands):len(ex.operands) + ex.n], refs[-3:])
        for cp in copies:
            cp.start()
        for cp in copies:
            cp.wait()

    return pl.pallas_call(body, name=name, in_specs=ex.in_specs, out_specs=ex.out_specs, out_shape=ex.out_shape,
                          input_output_aliases=ex.aliases(0, 0), scratch_shapes=ex.scratch)(*ex.operands)


def _reduce_packed(name, packed, scatter):
    rows = packed.shape[0]
    blk = rows // N_DEV
    ex = _Exchange(scatter=[packed.reshape(N_DEV, blk, LANES)] + list(scatter))
    n_in = len(ex.operands)

    def body(*refs):
        ins, outs = refs[:ex.n], refs[n_in:n_in + ex.n]
        total_ref = refs[n_in + ex.n]
        sems, (send2, recv2, own2, load_sem) = refs[n_in + ex.n + 1:n_in + ex.n + 4], refs[n_in + ex.n + 4:-2]
        land_v, sum_v = refs[-2:]
        copies = ex.copies(ins, outs, sems)
        for cp in copies:
            cp.start()
        for cp in copies:
            cp.wait()
        load = pltpu.make_async_copy(outs[0], land_v, load_sem)
        load.start()
        load.wait()
        acc = land_v[0]
        for i in range(1, N_DEV):
            acc = acc + land_v[i]
        sum_v[...] = acc
        x, y, c = lax.axis_index("x"), lax.axis_index("y"), lax.axis_index("c")
        me = 4 * x + 2 * y + c
        back = [pltpu.make_async_copy(sum_v, total_ref.at[me], own2)]
        for k in range(1, N_DEV):
            peer = (1 - x if k & 4 else x, 1 - y if k & 2 else y, 1 - c if k & 1 else c)
            back.append(pltpu.make_async_remote_copy(
                src_ref=sum_v, dst_ref=total_ref.at[me], send_sem=send2.at[k - 1], recv_sem=recv2.at[k - 1],
                device_id=peer, device_id_type=pl.DeviceIdType.MESH))
        for cp in back:
            cp.start()
        for cp in back:
            cp.wait()

    hbm = pl.BlockSpec(memory_space=pltpu.HBM)
    res = pl.pallas_call(
        body, name=name, in_specs=ex.in_specs, out_specs=ex.out_specs + [hbm],
        out_shape=ex.out_shape + [jax.ShapeDtypeStruct((N_DEV, blk, LANES), F32)],
        scratch_shapes=ex.scratch + [pltpu.SemaphoreType.DMA((N_DEV - 1,)), pltpu.SemaphoreType.DMA((N_DEV - 1,)),
                                     pltpu.SemaphoreType.DMA, pltpu.SemaphoreType.DMA,
                                     pltpu.VMEM((N_DEV, blk, LANES), F32), pltpu.VMEM((blk, LANES), F32)],
    )(*ex.operands)
    return res[-1].reshape(rows, LANES), res[1:-1]


def _call_beside(ex, body, name, grid, in_specs, out_specs, out_shape, scratch_shapes, vmem_bytes, operands,
                 semantics, in_hbm=True):
    if in_hbm:
        operands = [_in_hbm(a) for a in operands]
    if ex is None:
        res = pl.pallas_call(
            body, name=name, grid=grid, in_specs=in_specs, out_specs=out_specs, out_shape=out_shape,
            scratch_shapes=scratch_shapes,
            compiler_params=pltpu.CompilerParams(dimension_semantics=semantics, vmem_limit_bytes=vmem_bytes),
        )(*operands)
        return res, None
    n_in, n_out, n_scr = len(in_specs), len(out_specs), len(scratch_shapes)
    n_xin = len(ex.operands)

    def fused(*refs):
        mine = refs[:n_in] + refs[n_in + n_xin:n_in + n_xin + n_out]
        mine += refs[n_in + n_xin + n_out + ex.n:n_in + n_xin + n_out + ex.n + n_scr]
        first = pl.program_id(0) == 0
        last = pl.program_id(0) == grid[0] - 1
        for dim in range(1, len(grid)):
            first = jnp.logical_and(first, pl.program_id(dim) == 0)
            last = jnp.logical_and(last, pl.program_id(dim) == grid[dim] - 1)
        x_ins = refs[n_in:n_in + ex.n]
        x_outs = refs[n_in + n_xin + n_out:n_in + n_xin + n_out + ex.n]

        @pl.when(first)
        def _():
            for cp in ex.copies(x_ins, x_outs, refs[-3:]):
                cp.start()

        body(*mine)

        @pl.when(last)
        def _():
            for cp in ex.copies(x_ins, x_outs, refs[-3:]):
                cp.wait()

    res = pl.pallas_call(
        fused, name=name, grid=grid, in_specs=list(in_specs) + ex.in_specs, out_specs=list(out_specs) + ex.out_specs,
        out_shape=list(out_shape) + ex.out_shape, input_output_aliases=ex.aliases(n_in, n_out),
        scratch_shapes=list(scratch_shapes) + ex.scratch,
        compiler_params=pltpu.CompilerParams(dimension_semantics=("arbitrary",) * len(grid),
                                             vmem_limit_bytes=vmem_bytes),
    )(*operands, *ex.operands)
    return res[:n_out], res[n_out:]


NN = (((1,), (0,)), ((), ()))
NT = (((1,), (1,)), ((), ()))
TN = (((0,), (0,)), ((), ()))


def _in_hbm(a):
    return pltpu.with_memory_space_constraint(a, pltpu.HBM)


def _mm(name, a, b, a_spec, b_spec, o_spec, o_shape, o_dtype, grid, dims, beside=None, reread=(False, True)):
    nk = grid[2]
    a, b = (x if again else _in_hbm(x) for x, again in zip((a, b), reread))
    acc_shape = tuple(d for d in o_spec.block_shape if d is not None)

    def product(a_ref, b_ref):
        return lax.dot_general(a_ref[...].astype(BF16), b_ref[...].astype(BF16), dims, preferred_element_type=F32)

    def body_once(a_ref, b_ref, o_ref):
        o_ref[...] = product(a_ref, b_ref).astype(o_ref.dtype)

    def body(a_ref, b_ref, o_ref, acc_ref):
        k = pl.program_id(2)

        @pl.when(k == 0)
        def _():
            acc_ref[...] = product(a_ref, b_ref)

        @pl.when(k > 0)
        def _():
            acc_ref[...] += product(a_ref, b_ref)

        @pl.when(k == nk - 1)
        def _():
            o_ref[...] = acc_ref[...].astype(o_ref.dtype)

    blk = (_nbytes(a_spec.block_shape, a.dtype) + _nbytes(b_spec.block_shape, b.dtype)
           + _nbytes(acc_shape, o_dtype) + _nbytes(acc_shape, F32))
    res, got = _call_beside(
        beside, body_once if nk == 1 else body, name, grid, [a_spec, b_spec], [o_spec],
        [jax.ShapeDtypeStruct(o_shape, o_dtype)], [] if nk == 1 else [pltpu.VMEM(acc_shape, F32)],
        _vmem_limit(blk), (a, b), ("parallel", "parallel", "arbitrary"), in_hbm=False)
    return res[0] if beside is None else (res[0], got)


def _mm_pieces(name, pieces, starts, width, step_block, other, other_spec, pieces_first, piece_rows, o_spec, o_shape,
               o_dtype, grid, dims, beside=None):
    n_p, nk = len(pieces), grid[2]
    acc_shape = tuple(s for s in o_spec.block_shape if s is not None)

    def which(i, j, k):
        blk = step_block(i, j, k)
        idx = 0
        for s in starts[1:]:
            idx = idx + (blk >= s).astype(jnp.int32)
        return idx, blk

    def piece_spec(p, t_rows):
        def index(i, j, k):
            idx, blk = which(i, j, k)
            mine = idx == p
            return jnp.where(mine, piece_rows(i, j, k), 0), jnp.where(mine, blk - starts[p], 0)
        return _spec((t_rows, width), index)

    def body(*refs):
        p_refs = refs[:n_p] if pieces_first else refs[1:1 + n_p]
        other_ref = refs[n_p] if pieces_first else refs[0]
        o_ref, acc_ref = refs[n_p + 1], refs[n_p + 2]
        i, j, k = pl.program_id(0), pl.program_id(1), pl.program_id(2)

        @pl.when(k == 0)
        def _():
            acc_ref[...] = jnp.zeros_like(acc_ref)

        idx, _ = which(i, j, k)
        for p in range(n_p):
            @pl.when(idx == p)
            def _(p=p):
                mine, fixed = p_refs[p][...].astype(BF16), other_ref[...].astype(BF16)
                pair = (mine, fixed) if pieces_first else (fixed, mine)
                acc_ref[...] += lax.dot_general(pair[0], pair[1], dims, preferred_element_type=F32)

        @pl.when(k == nk - 1)
        def _():
            o_ref[...] = acc_ref[...].astype(o_ref.dtype)

    t_rows = other_spec.block_shape[-2] if not pieces_first else o_spec.block_shape[-2]
    specs = [piece_spec(p, t_rows) for p in range(n_p)]
    in_specs = specs + [other_spec] if pieces_first else [other_spec] + specs
    operands = list(pieces) + [other] if pieces_first else [other] + list(pieces)
    blk = (n_p * 4 * t_rows * width + _nbytes(other_spec.block_shape, other.dtype)
           + _nbytes(acc_shape, o_dtype) + _nbytes(acc_shape, F32))
    res, got = _call_beside(
        beside, body, name, grid, in_specs, [o_spec], [jax.ShapeDtypeStruct(o_shape, o_dtype)],
        [pltpu.VMEM(acc_shape, F32)], _vmem_limit(blk), operands, ("parallel", "parallel", "arbitrary"), in_hbm=False)
    return res[0] if beside is None else (res[0], got)


def _swiglu_fn(gate_up):
    gate, up = gate_up[0], gate_up[1]
    return gate * jax.nn.sigmoid(gate) * up


def _ffn_in_swiglu(name, h, w, t_m):
    seq, d = h.shape
    n_half, n = w.shape[0] // 2, w.shape[2]

    def body(h_ref, wg_ref, wu_ref, a_ref, f_ref):
        hb = h_ref[...]
        a_ref[0] = lax.dot_general(hb, wg_ref[...], NN, preferred_element_type=F32).astype(a_ref.dtype)
        a_ref[1] = lax.dot_general(hb, wu_ref[...], NN, preferred_element_type=F32).astype(a_ref.dtype)
        f_ref[...] = _swiglu_fn(a_ref[...].astype(F32)).astype(f_ref.dtype)

    blk = 2 * t_m * d + 4 * d * n + 6 * t_m * n + 12 * t_m * n
    return pl.pallas_call(
        body, name=name, grid=(seq // t_m, n_half),
        in_specs=[_spec((t_m, d), lambda i, j: (i, 0)), _spec((None, d, n), lambda i, j: (j, 0, 0)),
                  _spec((None, d, n), lambda i, j: (j + n_half, 0, 0))],
        out_specs=[_spec((2, None, t_m, n), lambda i, j: (0, j, i, 0)), _spec((None, t_m, n), lambda i, j: (j, i, 0))],
        out_shape=[jax.ShapeDtypeStruct((2, n_half, seq, n), FFN_ACT), jax.ShapeDtypeStruct((n_half, seq, n), BF16)],
        compiler_params=pltpu.CompilerParams(dimension_semantics=("parallel", "parallel"),
                                             vmem_limit_bytes=_vmem_limit(blk)),
    )(h, w, w)


def _ffn_out_dx_swiglu(name, d_y, w, a, t_m):
    seq, d = d_y.shape
    n_half, n = w.shape[0], w.shape[1]

    def body(dy_ref, w_ref, a_ref, da_ref):
        d_f = lax.dot_general(dy_ref[...], w_ref[...], NT, preferred_element_type=F32)
        gate, up = a_ref[0].astype(F32), a_ref[1].astype(F32)
        s = jax.nn.sigmoid(gate)
        gs = gate * s
        da_ref[0] = (d_f * up * (s + gs * (1.0 - s))).astype(da_ref.dtype)
        da_ref[1] = (d_f * gs).astype(da_ref.dtype)

    blk = 2 * t_m * d + 2 * d * n + 8 * t_m * n + 24 * t_m * n
    return pl.pallas_call(
        body, name=name, grid=(seq // t_m, n_half),
        in_specs=[_spec((t_m, d), lambda i, j: (i, 0)), _spec((None, n, d), lambda i, j: (j, 0, 0)),
                  _spec((2, None, t_m, n), lambda i, j: (0, j, i, 0))],
        out_specs=_spec((2, None, t_m, n), lambda i, j: (0, j, i, 0)),
        out_shape=jax.ShapeDtypeStruct((2, n_half, seq, n), BF16),
        compiler_params=pltpu.CompilerParams(dimension_semantics=("parallel", "parallel"),
                                             vmem_limit_bytes=_vmem_limit(blk)),
    )(d_y, w, a)


def _merge_fn(y_sb, y_ssm, gates):
    half = gates.shape[-1] // 2
    return jax.nn.sigmoid(gates[:, :half]) * y_sb + jax.nn.sigmoid(gates[:, half:]) * y_ssm


def _up_merge(name, o_sb, s5_out, proj, gates_cb, w_sb, w_ssm, layer, t_rows):
    seq = o_sb.shape[0]
    d = w_sb.shape[2]

    def body(o_ref, s_ref, g_ref, w1_ref, w2_ref, m_ref, y1_ref, y2_ref):
        y_sb = lax.dot_general(o_ref[...], w1_ref[...], NN, preferred_element_type=F32)
        y_ssm = lax.dot_general(s_ref[...], w2_ref[...], NN, preferred_element_type=F32)
        m_ref[...] = _merge_fn(y_sb, y_ssm, g_ref[...]).astype(m_ref.dtype)
        y1_ref[...] = y_sb.astype(y1_ref.dtype)
        y2_ref[...] = y_ssm.astype(y2_ref.dtype)

    row = lambda width: _spec((t_rows, width), lambda i: (i, 0))
    whole = lambda w: _spec((None,) + w.shape[1:], lambda i: (layer, 0, 0))
    blk = t_rows * (2 * o_sb.shape[1] + 2 * s5_out.shape[1] + 8 * d + 6 * d + 24 * d) + 4 * d * (o_sb.shape[1] + s5_out.shape[1])
    return pl.pallas_call(
        body, name=name, grid=(seq // t_rows,),
        in_specs=[row(o_sb.shape[1]), row(s5_out.shape[1]), _spec((t_rows, 2 * d), lambda i: (i, gates_cb)),
                  whole(w_sb), whole(w_ssm)],
        out_specs=[row(d)] * 3, out_shape=[jax.ShapeDtypeStruct((seq, d), BF16)] * 3,
        compiler_params=pltpu.CompilerParams(dimension_semantics=("parallel",), vmem_limit_bytes=_vmem_limit(blk)),
    )(_in_hbm(o_sb), _in_hbm(s5_out), _in_hbm(proj), w_sb, w_ssm)


def _tile(n, pref=1024):
    t = pref
    while t >= LANES:
        if n % t == 0:
            return t
        t -= LANES
    return n


def _rowwise(name, fn, ins, outs, grid, beside=None):
    n_in = len(ins)

    def body(*refs):
        vals = fn(*[r[...].astype(F32) for r in refs[:n_in]])
        if not isinstance(vals, (tuple, list)):
            vals = (vals,)
        for r, v in zip(refs[n_in:], vals):
            r[...] = v.astype(r.dtype)

    blk = sum(_nbytes(bs, a.dtype) for a, bs, _ in ins) + sum(_nbytes(bs, d) + _nbytes(bs, F32) for _, d, bs, _ in outs)
    res, got = _call_beside(
        beside, body, name, grid, [_spec(bs, im) for _, bs, im in ins], [_spec(bs, im) for _, _, bs, im in outs],
        [jax.ShapeDtypeStruct(s, d) for s, d, _, _ in outs], [], _vmem_limit(2 * blk), [a for a, _, _ in ins],
        ("parallel",) * len(grid))
    return list(res) if beside is None else (list(res), got)


def _rowwise_vjp(name, fn, ins, cts, wrt, grid):
    n_in, n_ct = len(ins), len(cts)
    idx = [w[0] for w in wrt]

    def body(*refs):
        prim = [r[...].astype(F32) for r in refs[:n_in]]
        ct = tuple(r[...].astype(F32) for r in refs[n_in:n_in + n_ct])
        o_refs = refs[n_in + n_ct:]

        def g(*sel):
            full = list(prim)
            for i, s in zip(idx, sel):
                full[i] = s
            out = fn(*full)
            return tuple(out) if isinstance(out, (tuple, list)) else (out,)

        _, pull = jax.vjp(g, *[prim[i] for i in idx])
        grads = pull(ct)
        first = pl.program_id(0) == 0
        for d in range(1, len(grid)):
            first = jnp.logical_and(first, pl.program_id(d) == 0)
        for w, o_ref, gr in zip(wrt, o_refs, grads):
            if w[1] == "row":
                o_ref[...] = gr.astype(o_ref.dtype)
            else:
                @pl.when(first)
                def _(o_ref=o_ref):
                    o_ref[...] = jnp.zeros_like(o_ref)

                o_ref[...] += gr.astype(o_ref.dtype)

    blk = (sum(_nbytes(bs, a.dtype) + _nbytes(bs, F32) for a, bs, _ in list(ins) + list(cts))
           + sum(_nbytes(w[4], w[3]) + _nbytes(w[4], F32) for w in wrt))
    return pl.pallas_call(
        body, name=name, grid=grid,
        in_specs=[_spec(bs, im) for _, bs, im in list(ins) + list(cts)],
        out_specs=[_spec(w[4], w[5]) for w in wrt],
        out_shape=[jax.ShapeDtypeStruct(w[2], w[3]) for w in wrt],
        compiler_params=pltpu.CompilerParams(dimension_semantics=("arbitrary",) * len(grid),
                                             vmem_limit_bytes=_vmem_limit(2 * blk)),
    )(*[_in_hbm(a) for a, _, _ in list(ins) + list(cts)])


def _normalize(x):
    mu = jnp.mean(x, axis=-1, keepdims=True)
    xc = x - mu
    var = jnp.mean(xc * xc, axis=-1, keepdims=True)
    return xc * lax.rsqrt(var + LN_EPS)


def _modulate(x, sc, sh):
    return _normalize(x) * (1.0 + sc) + sh


def _make_resid_fns(alpha):
    def resid_ln(x, y, gate, g, b):
        return _normalize(alpha * x + (1.0 + gate) * y) * g + b

    def resid_ln_mod(x, y, gate, g, b, sc, sh):
        xn = resid_ln(x, y, gate, g, b)
        return xn, _modulate(xn, sc, sh)

    return resid_ln, resid_ln_mod


def _s5_act_fn(yc, u, d_skip):
    return jax.nn.gelu(yc + d_skip * u)


def _s5_gate_fn(y1, t):
    return y1 * jax.nn.sigmoid(t)


def _s5_head_specs(yc, proj, u_cb, w_glu, layer, t_rows):
    width = yc.shape[1]
    row = _spec((t_rows, width), lambda i: (i, 0))
    u_spec = _spec((t_rows, width), lambda i: (i, u_cb))
    vec = _spec((1, width), lambda i: (0, 0))
    w_spec = _spec((None,) + w_glu.shape[1:], lambda i: (layer, 0, 0))
    return row, u_spec, vec, w_spec


def _s5_head(name, yc, proj, u_cb, d_skip, b_glu, w_glu, layer, t_rows):
    seq, width = yc.shape
    row, u_spec, vec, w_spec = _s5_head_specs(yc, proj, u_cb, w_glu, layer, t_rows)

    def body(yc_ref, u_ref, d_ref, b_ref, w_ref, o_ref):
        y1 = _s5_act_fn(yc_ref[...], u_ref[...], d_ref[...])
        t = lax.dot_general(y1.astype(BF16), w_ref[...], NN, preferred_element_type=F32) + b_ref[...]
        o_ref[...] = _s5_gate_fn(y1, t).astype(o_ref.dtype)

    return pl.pallas_call(
        body, name=name, grid=(seq // t_rows,), in_specs=[row, u_spec, vec, vec, w_spec], out_specs=row,
        out_shape=jax.ShapeDtypeStruct((seq, width), BF16),
        compiler_params=pltpu.CompilerParams(dimension_semantics=("parallel",),
                                             vmem_limit_bytes=_vmem_limit(40 * t_rows * width)),
    )(_in_hbm(yc), _in_hbm(proj), d_skip, b_glu, w_glu)


def _s5_head_bwd(name, yc, proj, u_cb, d_out, d_skip, b_glu, w_glu, layer, t_rows):
    seq, width = yc.shape
    row, u_spec, vec, w_spec = _s5_head_specs(yc, proj, u_cb, w_glu, layer, t_rows)
    n_t = seq // t_rows

    def body(yc_ref, u_ref, do_ref, d_ref, b_ref, w_ref, dyc_ref, du_ref, dw_ref, dd_ref, db_ref, acc_ref):
        i = pl.program_id(0)

        @pl.when(i == 0)
        def _():
            acc_ref[...] = jnp.zeros_like(acc_ref)
            dd_ref[...] = jnp.zeros_like(dd_ref)
            db_ref[...] = jnp.zeros_like(db_ref)

        y1, pull_act = jax.vjp(_s5_act_fn, yc_ref[...], u_ref[...], d_ref[...])
        y1_b = y1.astype(BF16)
        t = lax.dot_general(y1_b, w_ref[...], NN, preferred_element_type=F32) + b_ref[...]
        _, pull_gate = jax.vjp(_s5_gate_fn, y1, t)
        d_y1, d_t = pull_gate(do_ref[...].astype(F32))
        d_t_b = d_t.astype(BF16)
        d_y1 = d_y1 + lax.dot_general(d_t_b, w_ref[...], NT, preferred_element_type=F32)
        acc_ref[...] += lax.dot_general(y1_b, d_t_b, TN, preferred_element_type=F32)
        db_ref[...] += jnp.sum(d_t, axis=0, keepdims=True)
        d_yc, d_u, d_d = pull_act(d_y1)
        dyc_ref[...] = d_yc
        du_ref[...] = d_u
        dd_ref[...] += d_d

        @pl.when(i == n_t - 1)
        def _():
            dw_ref[...] = acc_ref[...].astype(dw_ref.dtype)

    whole = _spec((width, width), lambda i: (0, 0))
    return pl.pallas_call(
        body, name=name, grid=(n_t,), in_specs=[row, u_spec, row, vec, vec, w_spec],
        out_specs=[row, row, whole, vec, vec],
        out_shape=[jax.ShapeDtypeStruct((seq, width), F32), jax.ShapeDtypeStruct((seq, width), F32),
                   jax.ShapeDtypeStruct((width, width), GRAD_WIRE), jax.ShapeDtypeStruct((1, width), F32),
                   jax.ShapeDtypeStruct((1, width), F32)],
        scratch_shapes=[pltpu.VMEM((width, width), F32)],
        compiler_params=pltpu.CompilerParams(dimension_semantics=("arbitrary",),
                                             vmem_limit_bytes=_vmem_limit(80 * t_rows * width)),
    )(_in_hbm(yc), _in_hbm(proj), _in_hbm(d_out), d_skip, b_glu, w_glu)


def _sb_tri(kind):
    row = lax.broadcasted_iota(jnp.int32, (SB_BLOCK, SB_BLOCK), 0)
    col = lax.broadcasted_iota(jnp.int32, (SB_BLOCK, SB_BLOCK), 1)
    if kind == "after":
        return (row > col).astype(BF16)
    if kind == "from":
        return (row >= col).astype(BF16)
    return col < row


def _split_dot(x, m):
    hi = x.astype(BF16)
    lo = (x - hi.astype(F32)).astype(BF16)
    return (lax.dot_general(hi, m, NN, preferred_element_type=F32)
            + lax.dot_general(lo, m, NN, preferred_element_type=F32))


def _sb_scores(qh, k2):
    z = lax.dot_general(qh, k2, NT, preferred_element_type=F32)
    log_beta = jnp.minimum(z, 0.0) - jnp.log(1.0 + jnp.exp(-jnp.abs(z)))
    return log_beta, log_beta - z


def _sb_attention_fwd(proj, sb_width, beside=None):
    seq = proj.shape[0]
    n_pair, n_q = sb_width // LANES, seq // (SB_BLOCK * SB_GROUP)
    scale = 1.0 / (HEAD_DIM ** 0.5)
    chains = [(s, h) for s in range(SB_GROUP) for h in range(2)]

    def body(q_ref, k_ref, v_ref, o_ref, o32_ref):
        first = pl.program_id(1) * SB_GROUP
        lane = lax.broadcasted_iota(jnp.int32, (SB_BLOCK, LANES), 1)
        m_after, causal = _sb_tri("after"), _sb_tri("mask")
        heads = [lane < HEAD_DIM, lane >= HEAD_DIM]
        rows = [pl.ds(s * SB_BLOCK, SB_BLOCK) for s in range(SB_GROUP)]
        qh = {(s, h): (jnp.where(heads[h], q_ref[rows[s], :], 0.0) * scale).astype(BF16) for s, h in chains}

        def key_rows(s, r):
            kb = first + s - r
            return kb >= 0, pl.ds(pl.multiple_of(jnp.maximum(kb, 0) * SB_BLOCK, SB_BLOCK), SB_BLOCK)

        def scores(r, diag):
            out = []
            for s in range(SB_GROUP):
                live, ks = key_rows(s, r)
                k2 = k_ref[ks, :].astype(BF16)
                for h in range(2):
                    log_beta, log_1m = _sb_scores(qh[s, h], k2)
                    if diag:
                        log_1m = jnp.where(causal, log_1m, 0.0)
                    else:
                        log_1m = jnp.where(live, log_1m, 0.0)
                    out += [log_beta + _split_dot(log_1m, m_after), jnp.sum(log_1m, axis=1, keepdims=True)]
            return tuple(out)

        def weigh(r, sc, carry, acc, diag):
            out = []
            for c, (s, h) in enumerate(chains):
                live, ks = key_rows(s, r)
                v2 = v_ref[ks, :].astype(BF16)
                w = jnp.exp(sc[2 * c] + carry[c])
                w = jnp.where(causal, w, 0.0) if diag else jnp.where(live, w, 0.0)
                out.append(acc[c] + lax.dot_general(w.astype(BF16), v2, NN, preferred_element_type=F32))
            return tuple(out)

        zero = jnp.zeros((SB_BLOCK, LANES), F32)
        zcol = jnp.zeros((SB_BLOCK, 1), F32)
        sc = scores(0, True)
        acc = weigh(0, sc, (zcol,) * len(chains), (zero,) * len(chains), True)
        carry = tuple(sc[2 * c + 1] for c in range(len(chains)))
        last = first + SB_GROUP - 1

        def loop(st):
            r, carry, acc = st
            sc = scores(r, False)
            after = tuple(carry[c] + sc[2 * c + 1] for c in range(len(chains)))
            top = jnp.max(after[0])
            for c in range(1, len(chains)):
                top = jnp.maximum(top, jnp.max(after[c]))
            acc = weigh(r, sc, carry, acc, False)
            return jnp.where(top < SB_UNDERFLOW, last + 1, r + 1), after, acc

        _, _, acc = lax.while_loop(lambda st: st[0] <= last, loop, (1, carry, acc))
        for s in range(SB_GROUP):
            out = jnp.where(heads[0], acc[2 * s], acc[2 * s + 1])
            o_ref[rows[s], :] = out.astype(o_ref.dtype)
            o32_ref[rows[s], :] = out

    q_spec = _spec((SB_BLOCK * SB_GROUP, LANES), lambda h, i: (i, h))
    kv = [_spec((seq, LANES), lambda h, i, o=o: (0, o + h)) for o in (n_pair, 2 * n_pair)]
    return _call_beside(
        beside, body, "sb_attention_fwd", (n_pair, n_q), [q_spec] + kv, [q_spec, q_spec],
        [jax.ShapeDtypeStruct((seq, sb_width), BF16), jax.ShapeDtypeStruct((seq, sb_width), F32)], [],
        _vmem_limit(2 * seq * LANES * 4), (proj, proj, proj), ("parallel", "arbitrary"))


def _sb_attention_bwd(proj, o32, do, sb_width, beside=None):
    seq = proj.shape[0]
    n_pair, n_q = sb_width // LANES, seq // (SB_BLOCK * SB_GROUP)
    scale = 1.0 / (HEAD_DIM ** 0.5)
    chains = [(s, h) for s in range(SB_GROUP) for h in range(2)]
    n_c = len(chains)

    def body(q_ref, k_ref, v_ref, o_ref, do_ref, dq_ref, dk_out_ref, dv_out_ref, dk_ref, dv_ref):
        qi = pl.program_id(1)
        first = qi * SB_GROUP

        @pl.when(qi == 0)
        def _():
            dk_ref[...] = jnp.zeros_like(dk_ref)
            dv_ref[...] = jnp.zeros_like(dv_ref)

        lane = lax.broadcasted_iota(jnp.int32, (SB_BLOCK, LANES), 1)
        m_after, m_from, causal = _sb_tri("after"), _sb_tri("from"), _sb_tri("mask")
        heads = [lane < HEAD_DIM, lane >= HEAD_DIM]
        rows = [pl.ds(s * SB_BLOCK, SB_BLOCK) for s in range(SB_GROUP)]
        qh, doh_b, total = {}, {}, {}
        for s, h in chains:
            qh[s, h] = (jnp.where(heads[h], q_ref[rows[s], :], 0.0) * scale).astype(BF16)
            doh = jnp.where(heads[h], do_ref[rows[s], :].astype(F32), 0.0)
            doh_b[s, h] = doh.astype(BF16)
            total[s, h] = jnp.sum(doh * o_ref[rows[s], :], axis=1, keepdims=True)

        def key_rows(s, r):
            kb = first + s - r
            return kb >= 0, pl.ds(pl.multiple_of(jnp.maximum(kb, 0) * SB_BLOCK, SB_BLOCK), SB_BLOCK)

        def scores(r, diag):
            out = []
            for s in range(SB_GROUP):
                live, ks = key_rows(s, r)
                k2 = k_ref[ks, :].astype(BF16)
                v2 = v_ref[ks, :].astype(BF16)
                for h in range(2):
                    log_beta, log_1m = _sb_scores(qh[s, h], k2)
                    log_1m = jnp.where(causal, log_1m, 0.0) if diag else jnp.where(live, log_1m, 0.0)
                    out += [log_beta + _split_dot(log_1m, m_after), jnp.sum(log_1m, axis=1, keepdims=True),
                            lax.dot_general(doh_b[s, h], v2, NT, preferred_element_type=F32), log_beta]
            return tuple(out)

        def pull(r, sc, carry, right, dq, diag):
            right_out, dq_out = [], []
            for s in range(SB_GROUP):
                live, ks = key_rows(s, r)
                k2 = k_ref[ks, :].astype(BF16)
                dv_blk, dk_blk = None, None
                for h in range(2):
                    c = 2 * s + h
                    arg, _, d_w, log_beta = sc[4 * c:4 * c + 4]
                    w = jnp.exp(arg + carry[c])
                    w = jnp.where(causal, w, 0.0) if diag else jnp.where(live, w, 0.0)
                    w_b = w.astype(BF16)
                    d_arg = d_w * w_b.astype(F32)
                    dv_h = lax.dot_general(w_b, doh_b[s, h], TN, preferred_element_type=F32)
                    d_log_1m = total[s, h] - right[c] - _split_dot(d_arg, m_from)
                    beta = jnp.exp(log_beta)
                    dz = d_arg * (1.0 - beta) - beta * d_log_1m
                    dz = jnp.where(causal, dz, 0.0) if diag else jnp.where(live, dz, 0.0)
                    dz_b = dz.astype(BF16)
                    dk_h = lax.dot_general(dz_b, qh[s, h], TN, preferred_element_type=F32)
                    dv_blk = dv_h if h == 0 else dv_blk + dv_h
                    dk_blk = dk_h if h == 0 else dk_blk + dk_h
                    dq_out.append(dq[c] + lax.dot_general(dz_b, k2, NN, preferred_element_type=F32))
                    right_out.append(right[c] + jnp.sum(d_arg, axis=1, keepdims=True))
                dv_ref[ks, :] += dv_blk
                dk_ref[ks, :] += dk_blk
            return tuple(right_out), tuple(dq_out)

        zero = jnp.zeros((SB_BLOCK, LANES), F32)
        zcol = jnp.zeros((SB_BLOCK, 1), F32)
        sc = scores(0, True)
        right, dq = pull(0, sc, (zcol,) * n_c, (zcol,) * n_c, (zero,) * n_c, True)
        carry = tuple(sc[4 * c + 1] for c in range(n_c))
        last = first + SB_GROUP - 1

        def loop(st):
            r, carry, right, dq = st
            sc = scores(r, False)
            after = tuple(carry[c] + sc[4 * c + 1] for c in range(n_c))
            top = jnp.max(after[0])
            for c in range(1, n_c):
                top = jnp.maximum(top, jnp.max(after[c]))
            right, dq = pull(r, sc, carry, right, dq, False)
            return jnp.where(top < SB_UNDERFLOW, last + 1, r + 1), after, right, dq

        _, _, _, dq = lax.while_loop(lambda st: st[0] <= last, loop, (1, carry, right, dq))
        for s in range(SB_GROUP):
            dq_ref[rows[s], :] = (jnp.where(heads[0], dq[2 * s], dq[2 * s + 1]) * scale).astype(dq_ref.dtype)

        @pl.when(qi == n_q - 1)
        def _():
            dk_out_ref[...] = dk_ref[...].astype(dk_out_ref.dtype)
            dv_out_ref[...] = dv_ref[...].astype(dv_out_ref.dtype)

    q_spec = _spec((SB_BLOCK * SB_GROUP, LANES), lambda h, i: (i, h))
    kv = [_spec((seq, LANES), lambda h, i, o=o: (0, o + h)) for o in (n_pair, 2 * n_pair)]
    full = _spec((seq, LANES), lambda h, i: (0, h))
    return _call_beside(
        beside, body, "sb_attention_bwd", (n_pair, n_q), [q_spec] + kv + [q_spec, q_spec], [q_spec, full, full],
        [jax.ShapeDtypeStruct((seq, sb_width), BF16)] * 3,
        [pltpu.VMEM((seq, LANES), F32), pltpu.VMEM((seq, LANES), F32)],
        _vmem_limit(4 * seq * LANES * 4), (proj, proj, proj, o32, do), ("parallel", "arbitrary"))


def _s5_discretize(a_re, a_im, log_dt, b_re, b_im, c_re, c_im):
    n_g, n_p = a_re.shape
    c_g = b_re.shape[-1]
    ns = n_g // SLAB_GROUPS
    dt = jnp.exp(log_dt)[:, None]
    xr, xi = a_re * dt, a_im * dt
    mag = jnp.exp(xr)
    lr, li = mag * jnp.cos(xi), mag * jnp.sin(xi)
    den = a_re * a_re + a_im * a_im
    fr = ((lr - 1.0) * a_re + li * a_im) / den
    fi = (li * a_re - (lr - 1.0) * a_im) / den
    bb_re = fr[..., None] * b_re - fi[..., None] * b_im
    bb_im = fr[..., None] * b_im + fi[..., None] * b_re
    eye = jnp.eye(SLAB_GROUPS, dtype=F32)

    def diag_b(m):
        m = jnp.transpose(m.reshape(ns, SLAB_GROUPS, n_p, c_g), (0, 1, 3, 2))
        m = m[:, :, :, None, :] * eye[None, :, None, :, None]
        return m.reshape(ns, SLAB_GROUPS * c_g, SLAB_GROUPS * n_p)

    def diag_c(m):
        m = jnp.transpose(m.reshape(ns, SLAB_GROUPS, c_g, n_p), (0, 1, 3, 2))
        m = m[:, :, :, None, :] * eye[None, :, None, :, None]
        return m.reshape(ns, SLAB_GROUPS * n_p, SLAB_GROUPS * c_g)

    bs = jnp.concatenate([diag_b(bb_re), diag_b(bb_im)], axis=-1)
    cs = jnp.concatenate([diag_c(c_re), -diag_c(c_im)], axis=1)
    lam = jnp.concatenate([lr.reshape(ns, 1, -1), li.reshape(ns, 1, -1)], axis=-1)
    return bs, cs, lam


def _s5_powers(a_re, a_im, log_dt, n):
    n_g, n_p = a_re.shape
    ns = n_g // SLAB_GROUPS
    dt = jnp.exp(log_dt)[:, None]
    mag = jnp.exp(a_re * dt)
    base_r, base_i = mag * jnp.cos(a_im * dt), mag * jnp.sin(a_im * dt)
    steps = jnp.arange(1, n + 1, dtype=jnp.int32)[:, None, None]
    pr, pi = jnp.ones((n, n_g, n_p), F32), jnp.zeros((n, n_g, n_p), F32)
    for b in range(n.bit_length()):
        take = ((steps >> b) & 1) == 1
        pr, pi = (jnp.where(take, pr * base_r - pi * base_i, pr), jnp.where(take, pr * base_i + pi * base_r, pi))
        base_r, base_i = base_r * base_r - base_i * base_i, 2.0 * base_r * base_i

    def slabs(re, im):
        one = lambda m: jnp.transpose(m.reshape(n, ns, SLAB_GROUPS * n_p), (1, 0, 2))
        return jnp.concatenate([one(re), one(im)], axis=-1)

    return slabs(pr, pi), slabs(pr[::-1], -pi[::-1])


def _lanes(j):
    return slice(j * LANES, (j + 1) * LANES)


def _tile8(k):
    return pl.ds(pl.multiple_of(k * SUBLANES, SUBLANES), SUBLANES)


def _s5_interleave(dst_ref, src_ref, t_seg):
    def body(k, _):
        dst_ref[_tile8(k), :] = src_ref[pl.ds(k, SUBLANES, stride=t_seg), :]
        return 0

    lax.fori_loop(0, t_seg, body, 0, unroll=4)


def _s5_join_segments(st_ref, end_ref, car_ref, tab_ref, row, order, n_pair):
    for j in range(n_pair):
        re, im = _lanes(j), _lanes(n_pair + j)
        cr, ci = st_ref[:, re], st_ref[:, im]
        tr, ti = tab_ref[row:row + 1, re], tab_ref[row:row + 1, im]
        for s in order:
            car_ref[s:s + 1, re] = cr
            car_ref[s:s + 1, im] = ci
            er, ei = end_ref[s:s + 1, re], end_ref[s:s + 1, im]
            cr, ci = er + tr * cr - ti * ci, ei + tr * ci + ti * cr
        st_ref[:, re] = cr
        st_ref[:, im] = ci


def _s5_add_carries(buf_ref, car_ref, tab_ref, t_seg, n_pair):
    def fix(k, _):
        rows = _tile8(k)
        tab = tab_ref[pl.ds(k, 1), :]
        for j in range(n_pair):
            re, im = _lanes(j), _lanes(n_pair + j)
            cr, ci = car_ref[:, re], car_ref[:, im]
            tr, ti = tab[:, re], tab[:, im]
            buf_ref[rows, re] += tr * cr - ti * ci
            buf_ref[rows, im] += tr * ci + ti * cr
        return 0

    lax.fori_loop(0, t_seg, fix, 0, unroll=2)


def _s5_scan_fwd(proj, u_col, bs, cs, lam, pw, t_blk, beside=None):
    seq = proj.shape[0]
    ns, _, w2 = bs.shape
    n_pair = w2 // (2 * LANES)
    t_seg, n_t = t_blk // SUBLANES, seq // t_blk

    def body(u_ref, bs_ref, cs_ref, lam_ref, pw_ref, yc_ref, h_ref, st_ref, end_ref, car_ref, ui_ref, bu_ref, yi_ref):
        @pl.when(pl.program_id(1) == 0)
        def _():
            st_ref[...] = jnp.zeros_like(st_ref)

        _s5_interleave(ui_ref, u_ref, t_seg)
        bu_ref[...] = lax.dot_general(ui_ref[...].astype(BF16), bs_ref[...], NN, preferred_element_type=F32)
        lam_r = [jnp.broadcast_to(lam_ref[:, _lanes(j)], (SUBLANES, LANES)) for j in range(n_pair)]
        lam_i = [jnp.broadcast_to(lam_ref[:, _lanes(n_pair + j)], (SUBLANES, LANES)) for j in range(n_pair)]

        def step(k, c):
            rows = _tile8(k)
            out = []
            for j in range(n_pair):
                hr, hi = c[2 * j], c[2 * j + 1]
                nr = lam_r[j] * hr - lam_i[j] * hi + bu_ref[rows, _lanes(j)]
                ni = lam_i[j] * hr + lam_r[j] * hi + bu_ref[rows, _lanes(n_pair + j)]
                h_ref[rows, _lanes(j)] = nr
                h_ref[rows, _lanes(n_pair + j)] = ni
                out += [nr, ni]
            return tuple(out)

        ends = lax.fori_loop(0, t_seg, step, (jnp.zeros((SUBLANES, LANES), F32),) * (2 * n_pair), unroll=4)
        for j in range(n_pair):
            end_ref[:, _lanes(j)] = ends[2 * j]
            end_ref[:, _lanes(n_pair + j)] = ends[2 * j + 1]
        _s5_join_segments(st_ref, end_ref, car_ref, pw_ref, t_seg - 1, list(range(SUBLANES)), n_pair)
        _s5_add_carries(h_ref, car_ref, pw_ref, t_seg, n_pair)
        yi_ref[...] = lax.dot_general(h_ref[...].astype(BF16), cs_ref[...], NN, preferred_element_type=F32)

        def scatter(k, _):
            yc_ref[pl.ds(k, SUBLANES, stride=t_seg), :] = yi_ref[_tile8(k), :]
            return 0

        lax.fori_loop(0, t_seg, scatter, 0, unroll=4)

    return _call_beside(
        beside, body, "s5_scan_fwd", (ns, n_t),
        [_spec((t_blk, LANES), lambda s, i: (i, u_col + s)),
         _spec((None, LANES, w2), lambda s, i: (s, 0, 0)),
         _spec((None, w2, LANES), lambda s, i: (s, 0, 0)),
         _spec((None, 1, w2), lambda s, i: (s, 0, 0)),
         _spec((None, t_seg, w2), lambda s, i: (s, 0, 0))],
        [_spec((t_blk, LANES), lambda s, i: (i, s)),
         _spec((None, t_blk, w2), lambda s, i: (s, i, 0))],
        [jax.ShapeDtypeStruct((seq, ns * LANES), F32), jax.ShapeDtypeStruct((ns, seq, w2), F32)],
        [pltpu.VMEM((1, w2), F32), pltpu.VMEM((SUBLANES, w2), F32), pltpu.VMEM((SUBLANES, w2), F32),
         pltpu.VMEM((t_blk, LANES), F32), pltpu.VMEM((t_blk, w2), F32), pltpu.VMEM((t_blk, LANES), F32)],
        _vmem_limit(3 * t_blk * w2 * 4), (proj, bs, cs, lam, pw), ("parallel", "arbitrary"))


def _s5_scan_bwd(proj, u_col, states, d_yc, du_extra, bs, cs, lam, qw, t_blk):
    seq = proj.shape[0]
    ns, _, w2 = bs.shape
    n_pair = w2 // (2 * LANES)
    t_seg, n_t = t_blk // SUBLANES, seq // t_blk

    def body(u_ref, h_ref, hp_ref, dyc_ref, dux_ref, bs_ref, cs_ref, lam_ref, qw_ref,
             du_ref, dbs_ref, dcs_ref, dlam_ref, g_ref, gd_ref, st_ref, end_ref, car_ref, ui_ref, dyi_ref, dui_ref):
        i = pl.program_id(1)

        @pl.when(i == 0)
        def _():
            st_ref[...] = jnp.zeros_like(st_ref)
            dbs_ref[...] = jnp.zeros_like(dbs_ref)
            dcs_ref[...] = jnp.zeros_like(dcs_ref)
            dlam_ref[...] = jnp.zeros_like(dlam_ref)

        _s5_interleave(ui_ref, u_ref, t_seg)
        _s5_interleave(dyi_ref, dyc_ref, t_seg)
        dyc_b = dyi_ref[...].astype(BF16)
        gd_ref[...] = lax.dot_general(dyc_b, cs_ref[...], NT, preferred_element_type=F32)
        lam_r = [jnp.broadcast_to(lam_ref[:, _lanes(j)], (SUBLANES, LANES)) for j in range(n_pair)]
        lam_i = [jnp.broadcast_to(lam_ref[:, _lanes(n_pair + j)], (SUBLANES, LANES)) for j in range(n_pair)]

        def step(kk, c):
            rows = _tile8(t_seg - 1 - kk)
            out = []
            for j in range(n_pair):
                gr_n, gi_n = c[2 * j], c[2 * j + 1]
                gr = gd_ref[rows, _lanes(j)] + lam_r[j] * gr_n + lam_i[j] * gi_n
                gi = gd_ref[rows, _lanes(n_pair + j)] + lam_r[j] * gi_n - lam_i[j] * gr_n
                g_ref[rows, _lanes(j)] = gr
                g_ref[rows, _lanes(n_pair + j)] = gi
                out += [gr, gi]
            return tuple(out)

        zero = jnp.zeros((SUBLANES, LANES), F32)
        firsts = lax.fori_loop(0, t_seg, step, (zero,) * (2 * n_pair), unroll=4)
        for j in range(n_pair):
            end_ref[:, _lanes(j)] = firsts[2 * j]
            end_ref[:, _lanes(n_pair + j)] = firsts[2 * j + 1]
        _s5_join_segments(st_ref, end_ref, car_ref, qw_ref, 0, list(range(SUBLANES))[::-1], n_pair)
        _s5_add_carries(g_ref, car_ref, qw_ref, t_seg, n_pair)

        def pair_up(k, c):
            rows, prev = _tile8(k), _tile8(k - 1)
            out = []
            for j in range(n_pair):
                re, im = _lanes(j), _lanes(n_pair + j)
                gr, gi, hr, hi = g_ref[rows, re], g_ref[rows, im], h_ref[prev, re], h_ref[prev, im]
                out += [c[2 * j] + gr * hr + gi * hi, c[2 * j + 1] + gi * hr - gr * hi]
            return tuple(out)

        acc = lax.fori_loop(1, t_seg, pair_up, (zero,) * (2 * n_pair), unroll=4)
        has_prev = (i < n_t - 1).astype(F32)
        first_seg = lax.broadcasted_iota(jnp.int32, (SUBLANES, LANES), 0) == 0
        last = _tile8(t_seg - 1)
        for j in range(n_pair):
            re, im = _lanes(j), _lanes(n_pair + j)
            gr, gi = g_ref[0:SUBLANES, re], g_ref[0:SUBLANES, im]
            hr = jnp.where(first_seg, hp_ref[SUBLANES - 1:, re] * has_prev, pltpu.roll(h_ref[last, re], 1, 0))
            hi = jnp.where(first_seg, hp_ref[SUBLANES - 1:, im] * has_prev, pltpu.roll(h_ref[last, im], 1, 0))
            dlam_ref[:, re] += jnp.sum(acc[2 * j] + gr * hr + gi * hi, axis=0, keepdims=True)
            dlam_ref[:, im] += jnp.sum(acc[2 * j + 1] + gi * hr - gr * hi, axis=0, keepdims=True)

        g_b = g_ref[...].astype(BF16)
        dui_ref[...] = lax.dot_general(g_b, bs_ref[...], NT, preferred_element_type=F32)
        dbs_ref[...] += lax.dot_general(ui_ref[...].astype(BF16), g_b, TN, preferred_element_type=F32)
        dcs_ref[...] += lax.dot_general(h_ref[...].astype(BF16), dyc_b, TN, preferred_element_type=F32)

        def scatter(k, _):
            rows = pl.ds(k, SUBLANES, stride=t_seg)
            du_ref[rows, :] = (dui_ref[_tile8(k), :] + dux_ref[rows, :]).astype(du_ref.dtype)
            return 0

        lax.fori_loop(0, t_seg, scatter, 0, unroll=4)

    rev = lambda i: n_t - 1 - i
    return pl.pallas_call(
        body, name="s5_scan_bwd", grid=(ns, n_t),
        in_specs=[_spec((t_blk, LANES), lambda s, i: (rev(i), u_col + s)),
                  _spec((None, t_blk, w2), lambda s, i: (s, rev(i), 0)),
                  _spec((None, SUBLANES, w2), lambda s, i: (s, jnp.maximum(rev(i) * t_seg - 1, 0), 0)),
                  _spec((t_blk, LANES), lambda s, i: (rev(i), s)),
                  _spec((t_blk, LANES), lambda s, i: (rev(i), s)),
                  _spec((None, LANES, w2), lambda s, i: (s, 0, 0)),
                  _spec((None, w2, LANES), lambda s, i: (s, 0, 0)),
                  _spec((None, 1, w2), lambda s, i: (s, 0, 0)),
                  _spec((None, t_seg, w2), lambda s, i: (s, 0, 0))],
        out_specs=[_spec((t_blk, LANES), lambda s, i: (rev(i), s)),
                   _spec((None, LANES, w2), lambda s, i: (s, 0, 0)),
                   _spec((None, w2, LANES), lambda s, i: (s, 0, 0)),
                   _spec((None, 1, w2), lambda s, i: (s, 0, 0))],
        out_shape=[jax.ShapeDtypeStruct((seq, ns * LANES), F32), jax.ShapeDtypeStruct(bs.shape, F32),
                   jax.ShapeDtypeStruct(cs.shape, F32), jax.ShapeDtypeStruct(lam.shape, F32)],
        scratch_shapes=[pltpu.VMEM((t_blk, w2), F32), pltpu.VMEM((t_blk, w2), F32), pltpu.VMEM((1, w2), F32),
                        pltpu.VMEM((SUBLANES, w2), F32), pltpu.VMEM((SUBLANES, w2), F32),
                        pltpu.VMEM((t_blk, LANES), F32), pltpu.VMEM((t_blk, LANES), F32), pltpu.VMEM((t_blk, LANES), F32)],
        compiler_params=pltpu.CompilerParams(dimension_semantics=("parallel", "arbitrary"),
                                             vmem_limit_bytes=_vmem_limit(5 * t_blk * w2 * 4)),
    )(*[_in_hbm(a) for a in (proj, states, states, d_yc, du_extra, bs, cs, lam, qw)])


def _loss_head(y, target, t_m):
    seq, d = y.shape

    def body(y_ref, t_ref, loss_ref, dy_ref):
        @pl.when(pl.program_id(0) == 0)
        def _():
            loss_ref[...] = jnp.zeros_like(loss_ref)

        diff = y_ref[...] - t_ref[...]
        dy_ref[...] = diff / d
        loss_ref[...] += 0.5 * jnp.sum(diff * diff) / d

    row = _spec((t_m, d), lambda i: (i, 0))
    return pl.pallas_call(
        body, name="loss_head", grid=(seq // t_m,), in_specs=[row, row],
        out_specs=[_spec((SUBLANES, LANES), lambda i: (0, 0)), row],
        out_shape=[jax.ShapeDtypeStruct((SUBLANES, LANES), F32), jax.ShapeDtypeStruct((seq, d), F32)],
        compiler_params=pltpu.CompilerParams(dimension_semantics=("arbitrary",),
                                             vmem_limit_bytes=_vmem_limit(6 * t_m * d * 4)),
    )(_in_hbm(y), _in_hbm(target))


def _adamw_fn(w, m, v, *partials):
    g = partials[0]
    for p in partials[1:]:
        g = g + p
    m2 = ADAM_B1 * m + (1.0 - ADAM_B1) * g
    v2 = ADAM_B2 * v + (1.0 - ADAM_B2) * (g * g)
    m_hat = m2 / (1.0 - ADAM_B1 ** ADAM_STEP)
    v_hat = v2 / (1.0 - ADAM_B2 ** ADAM_STEP)
    delta = -ADAM_LR * (m_hat / (jnp.sqrt(v_hat) + ADAM_EPS) + ADAM_WD * w)
    return g, delta, m2, v2


def _adamw(name, w, m, v, partials):
    rows, cols = w.shape
    t_r = rows
    for cand in (512, 256, 128, 64, 32, 16, 8):
        if rows % cand == 0 and cand * cols * 4 <= (1 << 20):
            t_r = cand
            break
    n_p = partials.shape[0]
    row = lambda i: (i, 0)
    ins = [(a, (t_r, cols), row) for a in (w, m, v)]
    ins += [(partials, (None, t_r, cols), (lambda i, j=j: (j, i, 0))) for j in range(n_p)]
    outs = [((rows, cols), F32, (t_r, cols), row)] * 4
    return _rowwise(name, _adamw_fn, ins, outs, (rows // t_r,))


SMALL_PARAMS = ("b_ada", "ssm_a_re", "ssm_a_im", "ssm_log_dt", "ssm_b_re", "ssm_b_im", "ssm_c_re", "ssm_c_im",
                "ssm_d", "b_glu", "ln1_g", "ln1_b", "ln2_g", "ln2_b")
WEIGHTS = ("w_ada", "b_ada", "w_in", "w_sb_up", "ssm_a_re", "ssm_a_im", "ssm_log_dt", "ssm_b_re", "ssm_b_im",
           "ssm_c_re", "ssm_c_im", "ssm_d", "w_glu", "b_glu", "w_ssm_up", "w_out", "ln1_g", "ln1_b", "w_ffn_in",
           "w_ffn_out", "ln2_g", "ln2_b")
ARG_NAMES = (("x", "c") + WEIGHTS + ("loss_target",) + tuple("m_" + n for n in WEIGHTS)
             + tuple("v_" + n for n in WEIGHTS))


def _pack(arrs):
    flat = jnp.concatenate([a.reshape(-1) for a in arrs])
    pad = (-flat.shape[0]) % (PACK_ROWS * LANES)
    return jnp.pad(flat, (0, pad)).reshape(-1, LANES)


def _unpack(packed, like):
    lead = packed.shape[:-2]
    flat = packed.reshape(lead + (-1,))
    out, off = [], 0
    for a in like:
        out.append(flat[..., off:off + a.size].reshape(lead + a.shape))
        off += a.size
    return out


def kernel(x, c, w_ada, b_ada, w_in, w_sb_up, ssm_a_re, ssm_a_im, ssm_log_dt, ssm_b_re, ssm_b_im, ssm_c_re,
           ssm_c_im, ssm_d, w_glu, b_glu, w_ssm_up, w_out, ln1_g, ln1_b, w_ffn_in, w_ffn_out, ln2_g, ln2_b,
           loss_target, m_w_ada, m_b_ada, m_w_in, m_w_sb_up, m_ssm_a_re, m_ssm_a_im, m_ssm_log_dt, m_ssm_b_re,
           m_ssm_b_im, m_ssm_c_re, m_ssm_c_im, m_ssm_d, m_w_glu, m_b_glu, m_w_ssm_up, m_w_out, m_ln1_g, m_ln1_b,
           m_w_ffn_in, m_w_ffn_out, m_ln2_g, m_ln2_b, v_w_ada, v_b_ada, v_w_in, v_w_sb_up, v_ssm_a_re, v_ssm_a_im,
           v_ssm_log_dt, v_ssm_b_re, v_ssm_b_im, v_ssm_c_re, v_ssm_c_im, v_ssm_d, v_w_glu, v_b_glu, v_w_ssm_up,
           v_w_out, v_ln1_g, v_ln1_b, v_w_ffn_in, v_w_ffn_out, v_ln2_g, v_ln2_b):
    given = locals()
    return _train_step({n: given[n] for n in ARG_NAMES})


def _train_step(p):
    x0 = p["x"][0]
    target = p["loss_target"][0]
    seq, d = x0.shape
    depth = p["w_ada"].shape[0]
    n_ada = p["w_ada"].shape[2]
    n_in = p["w_in"].shape[2]
    sb_w = p["w_sb_up"].shape[1]
    ssm_w = p["w_ssm_up"].shape[1]
    n_up = p["w_sb_up"].shape[2]
    n_ffn = p["w_ffn_in"].shape[2]
    ffn = N_DEV * p["w_ffn_out"].shape[1]
    in_cols = N_DEV * n_in
    alpha = (2 * depth) ** 0.25
    resid_ln, resid_ln_mod = _make_resid_fns(alpha)
    t_r = min(512, seq)
    n_r = seq // t_r
    t_m = min(1024, seq)
    n_m = seq // t_m
    t_d = _tile(d)
    assert n_ffn * (N_DEV // 2) == ffn and sb_w % LANES == 0 and ssm_w % LANES == 0 and d % LANES == 0
    assert n_in % LANES == 0 and n_up % LANES == 0 and seq % t_m == 0 and in_cols == 3 * sb_w + ssm_w + 2 * d
    assert (3 * sb_w) % ssm_w == 0 and (3 * sb_w + ssm_w) % (2 * d) == 0
    assert sb_w % n_in == 0 and ssm_w % n_in == 0 and d % n_in == 0 and seq % (SB_BLOCK * SB_GROUP) == 0
    proj_starts = [c // n_in for c in (0, sb_w, 2 * sb_w, 3 * sb_w, 3 * sb_w + ssm_w)]

    bf = lambda a: a.astype(BF16)
    c_all = _exchange("gather_c", [], [p["c"]])[0].reshape(N_DEV, d)
    wg_in = [None] * depth
    small_names = ("w_sb_up", "w_ssm_up", "w_glu", "w_out")
    wg_ffn_in, wg_ffn_out, wg = [None] * depth, [None] * depth, {}

    c_pad = jnp.pad(c_all, ((0, 2 * SUBLANES - N_DEV), (0, 0)))
    c_act = _rowwise("silu_c", lambda v: v * jax.nn.sigmoid(v), [(c_pad, c_pad.shape, lambda i: (0, 0))],
                     [(c_pad.shape, F32, c_pad.shape, lambda i: (0, 0))], (1,))[0]
    rows_c = c_pad.shape[0]
    mod_cols = [
        _mm(f"mod_{l}", c_act, p["w_ada"],
            _spec((rows_c, d), lambda i, j, k: (0, 0)), _spec((None, d, n_ada), lambda i, j, k, l=l: (l, 0, 0)),
            _spec((rows_c, n_ada), lambda i, j, k: (0, 0)), (rows_c, n_ada), F32, (1, 1, 1), NN)
        for l in range(depth)]
    mod_send = jnp.stack([m[:N_DEV] for m in mod_cols], axis=1)
    mod_recv = _exchange("exchange_mod", [mod_send], [])[0]
    mod_nobias = jnp.swapaxes(mod_recv, 0, 1).reshape(depth, N_DEV * n_ada)
    full2 = lambda a: (a, a.shape, lambda i: (0, 0))
    mod = _rowwise("mod_bias", lambda a, b: a + b, [full2(mod_nobias), full2(p["b_ada"])],
                   [(mod_nobias.shape, F32, mod_nobias.shape, lambda i: (0, 0))], (1,))[0]
    vec = lambda a: a.reshape(1, -1)
    mods = [[vec(mod[l, j * d:(j + 1) * d]) for j in range(6)] for l in range(depth)]
    ln = {n: [vec(p[n][l]) for l in range(depth)] for n in ("ln1_g", "ln1_b", "ln2_g", "ln2_b")}

    row_spec = lambda width: ((t_r, width), lambda i: (i, 0))
    col_spec = lambda width, cb: ((t_r, width), lambda i, cb=cb: (i, cb))
    vec_spec = lambda width: ((1, width), lambda i: (0, 0))
    rows_in = lambda a: (a,) + row_spec(a.shape[1])
    vec_in = lambda a: (a,) + vec_spec(a.shape[1])
    row_out = lambda width, dt: ((seq, width), dt) + row_spec(width)

    s5 = [_s5_discretize(*[p[n][l] for n in ("ssm_a_re", "ssm_a_im", "ssm_log_dt", "ssm_b_re", "ssm_b_im",
                                               "ssm_c_re", "ssm_c_im")]) for l in range(depth)]
    s5_b16 = [(bs.astype(BF16), cs.astype(BF16), lam) for bs, cs, lam in s5]
    t_scan = min(1024, seq)
    s5_pw = [_s5_powers(p["ssm_a_re"][l], p["ssm_a_im"][l], p["ssm_log_dt"][l], t_scan // SUBLANES)
             for l in range(depth)]
    u_col = 3 * sb_w // LANES
    gates_cb = (3 * sb_w + ssm_w) // (2 * d)
    ssm_d = [vec(p["ssm_d"][l]) for l in range(depth)]
    b_glu = [vec(p["b_glu"][l]) for l in range(depth)]
    n_half = N_DEV // 2

    (h,), (wg_in[0],) = _rowwise("modulate_in", _modulate, [rows_in(x0), vec_in(mods[0][1]), vec_in(mods[0][0])],
                                 [row_out(d, BF16)], (n_r,), beside=_Exchange(gather=[bf(p["w_in"][0])]))
    saved = []
    x_cur = x0
    for l in range(depth):
        sv = {"x_in": x_cur, "h": h}
        last = l == depth - 1
        t_n = _tile(n_in)
        r_n = n_in // t_n
        proj = _mm(f"proj_{l}", h, wg_in[l],
                   _spec((t_m, d), lambda i, j, k: (i, 0)),
                   _spec((None, d, t_n), lambda i, j, k, r=r_n: (j // r, 0, j % r)),
                   _spec((t_m, t_n), lambda i, j, k: (i, j)), (seq, in_cols), F32, (n_m, N_DEV * r_n, 1), NN,
                   reread=(True, True))
        arriving = [bf(p["w_ffn_in"][l])] + ([bf(p[n]) for n in small_names] if l == 0 else [])
        (o_sb, o_sb32), got = _sb_attention_fwd(proj, sb_w, beside=_Exchange(gather=arriving))
        wg_ffn_in[l] = got[0]
        if l == 0:
            wg = dict(zip(small_names, got[1:]))
            for n in ("w_glu", "w_out"):
                wg[n] = jnp.swapaxes(wg[n], 0, 1).reshape(depth, -1, wg[n].shape[-1])
            for n in ("w_sb_up", "w_ssm_up"):
                wg[n] = jnp.transpose(wg[n], (1, 2, 0, 3)).reshape(depth, wg[n].shape[2], d)
        bs16, cs16, lam = s5_b16[l]
        arriving = [bf(p["w_ffn_out"][l])] + ([] if last else [bf(p["w_in"][l + 1])])
        (yc, states), got = _s5_scan_fwd(proj, u_col, bs16, cs16, lam, s5_pw[l][0], t_scan,
                                         beside=_Exchange(gather=arriving))
        wg_ffn_out[l] = got[0].reshape(n_half, n_ffn, d)
        if not last:
            wg_in[l + 1] = got[1]
        s5_out = _s5_head(f"s5_head_{l}", yc, proj, 3 * sb_w // ssm_w, ssm_d[l], b_glu[l], wg["w_glu"], l, t_r)

        merged, y_sb, y_ssm = _up_merge(f"up_merge_{l}", o_sb, s5_out, proj, gates_cb, wg["w_sb_up"], wg["w_ssm_up"],
                                        l, t_r)
        y_mix = _mm(f"out_proj_{l}", merged, wg["w_out"],
                    _spec((t_m, d), lambda i, j, k: (i, 0)), _spec((None, d, t_d), lambda i, j, k, l=l: (l, 0, j)),
                    _spec((t_m, t_d), lambda i, j, k: (i, j)), (seq, d), F32, (n_m, d // t_d, 1), NN)
        vecs_a = [mods[l][2], ln["ln1_g"][l], ln["ln1_b"][l], mods[l][4], mods[l][3]]
        x_mid, h2 = _rowwise(f"resid_mix_{l}", resid_ln_mod, [rows_in(x_cur), rows_in(y_mix)] + [vec_in(v) for v in vecs_a],
                             [row_out(d, F32), row_out(d, BF16)], (n_r,))
        a_ffn, f_act = _ffn_in_swiglu(f"ffn_in_{l}", h2, wg_ffn_in[l], t_r)
        y_ffn = _mm(f"ffn_out_{l}", f_act, wg_ffn_out[l],
                    _spec((None, t_m, n_ffn), lambda i, j, k: (k, i, 0)),
                    _spec((None, n_ffn, t_d), lambda i, j, k: (k, 0, j)),
                    _spec((t_m, t_d), lambda i, j, k: (i, j)), (seq, d), F32, (n_m, d // t_d, n_half), NN)
        vecs_b = [mods[l][5], ln["ln2_g"][l], ln["ln2_b"][l]] + ([] if last else [mods[l + 1][1], mods[l + 1][0]])
        outs_b = [row_out(d, F32)] + ([] if last else [row_out(d, BF16)])
        res = _rowwise(f"resid_ffn_{l}", resid_ln if last else resid_ln_mod,
                       [rows_in(x_mid), rows_in(y_ffn)] + [vec_in(v) for v in vecs_b], outs_b, (n_r,))
        sv.update(proj=proj, o_sb=o_sb, o_sb32=o_sb32, yc=yc, states=states, s5_out=s5_out,
                  y_sb=y_sb, y_ssm=y_ssm, merged=merged, y_mix=y_mix, x_mid=x_mid, h2=h2, a_ffn=a_ffn, f_act=f_act,
                  y_ffn=y_ffn, vecs_a=vecs_a, vecs_b=vecs_b)
        saved.append(sv)
        x_cur = res[0]
        h = None if last else res[1]

    loss_part, d_x = _loss_head(x_cur, target, t_r)
    loss = lax.psum(loss_part[0, 0], MESH_AXES)

    d_h_next = None
    grads = {n: [None] * depth for n in WEIGHTS}
    d_mod = [[None] * 6 for _ in range(depth)]
    land = {}
    waiting = []
    row_wrt = lambda i, width, dt: (i, "row", (seq, width), dt) + row_spec(width)
    sum_wrt = lambda i, width: (i, "sum", (1, width), F32) + vec_spec(width)
    for l in reversed(range(depth)):
        sv = saved[l]
        last = l == depth - 1
        ins_b = [rows_in(sv["x_mid"]), rows_in(sv["y_ffn"])] + [vec_in(v) for v in sv["vecs_b"]]
        cts_b = [rows_in(d_x)] + ([] if last else [rows_in(d_h_next)])
        wrt_b = [row_wrt(0, d, F32), row_wrt(1, d, BF16)] + [sum_wrt(2 + j, d) for j in range(len(sv["vecs_b"]))]
        res = _rowwise_vjp(f"resid_ffn_bwd_{l}", resid_ln if last else resid_ln_mod, ins_b, cts_b, wrt_b, (n_r,))
        d_x_mid, d_y_ffn = res[0], res[1]
        d_mod[l][5], grads["ln2_g"][l], grads["ln2_b"][l] = res[2], res[3], res[4]
        if not last:
            d_mod[l + 1][1], d_mod[l + 1][0] = res[5], res[6]
        d_a = _ffn_out_dx_swiglu(f"ffn_out_dx_{l}", d_y_ffn, wg_ffn_out[l], sv["a_ffn"], t_r).reshape(N_DEV, seq, n_ffn)
        g_ffn_out = _mm(f"ffn_out_dw_{l}", sv["f_act"], d_y_ffn,
                        _spec((None, t_m, n_ffn), lambda i, j, k: (i, k, 0)), _spec((t_m, t_d), lambda i, j, k: (k, j)),
                        _spec((None, n_ffn, t_d), lambda i, j, k: (i, 0, j)), (n_half, n_ffn, d), GRAD_WIRE,
                        (n_half, d // t_d, n_m), TN, reread=(False, True))
        d_h2 = _mm(f"ffn_in_dx_{l}", d_a, wg_ffn_in[l],
                   _spec((None, t_m, n_ffn), lambda i, j, k: (k, i, 0)),
                   _spec((None, t_d, n_ffn), lambda i, j, k: (k, j, 0)),
                   _spec((t_m, t_d), lambda i, j, k: (i, j)), (seq, d), BRANCH_CT, (n_m, d // t_d, N_DEV), NT)
        g_ffn_in = _mm(f"ffn_in_dw_{l}", sv["h2"], d_a,
                       _spec((t_m, t_d), lambda i, j, k: (k, j)), _spec((None, t_m, n_ffn), lambda i, j, k: (i, k, 0)),
                       _spec((None, t_d, n_ffn), lambda i, j, k: (i, j, 0)), (N_DEV, d, n_ffn), GRAD_WIRE,
                       (N_DEV, d // t_d, n_m), TN, reread=(True, False))
        ins_a = [rows_in(sv["x_in"]), rows_in(sv["y_mix"])] + [vec_in(v) for v in sv["vecs_a"]]
        wrt_a = [row_wrt(0, d, F32), row_wrt(1, d, BF16)] + [sum_wrt(2 + j, d) for j in range(5)]
        res = _rowwise_vjp(f"resid_mix_bwd_{l}", resid_ln_mod, ins_a, [rows_in(d_x_mid), rows_in(d_h2)], wrt_a, (n_r,))
        d_x_in, d_y_mix = res[0], res[1]
        d_mod[l][2], grads["ln1_g"][l], grads["ln1_b"][l], d_mod[l][4], d_mod[l][3] = res[2:7]
        d_merged = _mm(f"out_proj_dx_{l}", d_y_mix, wg["w_out"],
                       _spec((t_m, d), lambda i, j, k: (i, 0)), _spec((None, t_d, d), lambda i, j, k, l=l: (l, j, 0)),
                       _spec((t_m, t_d), lambda i, j, k: (i, j)), (seq, d), BRANCH_CT, (n_m, d // t_d, 1), NT)
        g_out = _mm(f"out_proj_dw_{l}", sv["merged"], d_y_mix,
                    _spec((t_m, t_d), lambda i, j, k: (k, i)), _spec((t_m, t_d), lambda i, j, k: (k, j)),
                    _spec((t_d, t_d), lambda i, j, k: (i, j)), (d, d), GRAD_WIRE, (d // t_d, d // t_d, n_m), TN, reread=(d > t_d, d > t_d))
        gates = (sv["proj"],) + col_spec(2 * d, gates_cb)
        d_y_sb, d_y_ssm, d_gates = _rowwise_vjp(
            f"merge_bwd_{l}", _merge_fn, [rows_in(sv["y_sb"]), rows_in(sv["y_ssm"]), gates], [rows_in(d_merged)],
            [row_wrt(0, d, BF16), row_wrt(1, d, BF16), row_wrt(2, 2 * d, BF16)], (n_r,))

        def up_bwd(name, act, d_y, w, dx_dtype, l=l):
            k_w = act.shape[1]
            dx = _mm(name + "_dx", d_y, w, _spec((t_m, d), lambda i, j, k: (i, 0)),
                     _spec((None, k_w, d), lambda i, j, k: (l, 0, 0)),
                     _spec((t_m, k_w), lambda i, j, k: (i, 0)), (seq, k_w), dx_dtype, (n_m, 1, 1), NT)
            dw = _mm(name + "_dw", act, d_y, _spec((t_m, k_w), lambda i, j, k: (k, 0)),
                     _spec((t_m, t_d), lambda i, j, k: (k, j)),
                     _spec((k_w, t_d), lambda i, j, k: (0, j)), (k_w, d), GRAD_WIRE, (1, d // t_d, n_m), TN,
                     reread=(d > t_d, False))
            return dx, jnp.swapaxes(dw.reshape(k_w, N_DEV, n_up), 0, 1)

        d_o_sb, g_sb_up = up_bwd(f"sb_up_{l}", sv["o_sb"], d_y_sb, wg["w_sb_up"], BF16)
        d_s5_out, g_ssm_up = up_bwd(f"ssm_up_{l}", sv["s5_out"], d_y_ssm, wg["w_ssm_up"], BRANCH_CT)
        waiting += [("w_ffn_in", g_ffn_in), ("w_ffn_out", g_ffn_out.reshape(N_DEV, -1, d)),
                    ("w_out", g_out.reshape(N_DEV, -1, d)), ("w_sb_up", g_sb_up), ("w_ssm_up", g_ssm_up)]
        levels = [l + 1] * (len(waiting) - 5) + [l] * 5
        (d_q, d_k, d_v), got = _sb_attention_bwd(
            sv["proj"], sv["o_sb32"], d_o_sb, sb_w,
            beside=_Exchange(layered=[(g, lv, depth, land.get(n)) for (n, g), lv in zip(waiting, levels)]))
        land.update({n: buf for (n, _), buf in zip(waiting, got)})
        d_yc, d_u_skip, g_glu, grads["ssm_d"][l], grads["b_glu"][l] = _s5_head_bwd(
            f"s5_head_bwd_{l}", sv["yc"], sv["proj"], 3 * sb_w // ssm_w, d_s5_out, ssm_d[l], b_glu[l], wg["w_glu"], l, t_r)
        bs16, cs16, lam = s5_b16[l]
        d_u, d_bs, d_cs, d_lam = _s5_scan_bwd(sv["proj"], u_col, sv["states"], d_yc, d_u_skip, bs16, cs16, lam,
                                              s5_pw[l][1], t_scan)
        raw = [p[n][l] for n in ("ssm_a_re", "ssm_a_im", "ssm_log_dt", "ssm_b_re", "ssm_b_im", "ssm_c_re", "ssm_c_im")]
        _, pull = jax.vjp(_s5_discretize, *raw)
        (grads["ssm_a_re"][l], grads["ssm_a_im"][l], grads["ssm_log_dt"][l], grads["ssm_b_re"][l],
         grads["ssm_b_im"][l], grads["ssm_c_re"][l], grads["ssm_c_im"][l]) = pull((d_bs, d_cs, d_lam))
        d_proj = [d_q, d_k, d_v, d_u, d_gates]
        g_in = _mm_pieces(f"proj_dw_{l}", d_proj, proj_starts, n_in, lambda i, j, k: i, sv["h"],
                          _spec((t_m, t_d), lambda i, j, k: (k, j)), False, lambda i, j, k: k,
                          _spec((None, t_d, n_in), lambda i, j, k: (i, j, 0)), (N_DEV, d, n_in), GRAD_WIRE,
                          (N_DEV, d // t_d, n_m), TN)
        waiting = [("w_in", g_in), ("w_glu", g_glu.reshape(N_DEV, -1, ssm_w))]
        closing = _Exchange(layered=[(g, 0, depth, land.get(n)) for n, g in waiting]) if l == 0 else None
        d_h = _mm_pieces(f"proj_dx_{l}", d_proj, proj_starts, n_in, lambda i, j, k: k, wg_in[l],
                         _spec((None, t_d, n_in), lambda i, j, k: (k, j, 0)), True, lambda i, j, k: i,
                         _spec((t_m, t_d), lambda i, j, k: (i, j)), (seq, d), BRANCH_CT, (n_m, d // t_d, N_DEV), NT,
                         beside=closing)
        if l == 0:
            d_h, got = d_h
            land.update({n: buf for (n, _), buf in zip(waiting, got)})
        d_x, d_h_next = d_x_in, d_h
    res = _rowwise_vjp("modulate_in_bwd", lambda v, sc, sh: (v, _modulate(v, sc, sh)),
                       [rows_in(x0), vec_in(mods[0][1]), vec_in(mods[0][0])], [rows_in(d_x), rows_in(d_h_next)],
                       [row_wrt(0, d, F32), sum_wrt(1, d), sum_wrt(2, d)], (n_r,))
    grad_x, d_mod[0][1], d_mod[0][0] = res

    d_mod_rows = jnp.concatenate([jnp.concatenate(d_mod[l], axis=1) for l in range(depth)], axis=0)
    grads["b_ada"] = [d_mod_rows[l] for l in range(depth)]
    small_local = [jnp.stack([g.reshape(p[n].shape[1:]) for g in grads[n]]) for n in SMALL_PARAMS]
    d_mod_send = jnp.swapaxes(d_mod_rows.reshape(depth, N_DEV, n_ada), 0, 1)
    small_sum, (d_mod_cols,) = _reduce_packed("exchange_last", _pack(small_local), [d_mod_send])
    d_mod_pad = jnp.pad(jnp.swapaxes(d_mod_cols, 0, 1), ((0, 0), (0, rows_c - N_DEV), (0, 0)))
    g_ada = [
        _mm(f"mod_dw_{l}", c_act, d_mod_pad,
            _spec((rows_c, d), lambda i, j, k: (0, 0)), _spec((None, rows_c, n_ada), lambda i, j, k, l=l: (l, 0, 0)),
            _spec((d, n_ada), lambda i, j, k: (0, 0)), (d, n_ada), F32, (1, 1, 1), TN)
        for l in range(depth)]

    out = {}

    def update(name, partials):
        shape = p[name].shape
        two_d = lambda a: a.reshape(-1, shape[-1])
        res = _adamw("adamw_" + name, two_d(p[name]), two_d(p["m_" + name]), two_d(p["v_" + name]),
                     partials.reshape(partials.shape[0], -1, shape[-1]))
        out[name] = [r.reshape(shape) for r in res]

    update("w_ada", jnp.stack(g_ada)[None])
    for n in ("w_in", "w_sb_up", "w_ssm_up", "w_ffn_in", "w_glu", "w_out", "w_ffn_out"):
        update(n, land[n])
    small_w = [p[n] for n in SMALL_PARAMS]
    res = _adamw("adamw_small", _pack(small_w), _pack([p["m_" + n] for n in SMALL_PARAMS]),
                 _pack([p["v_" + n] for n in SMALL_PARAMS]), small_sum[None])
    for kind, packed in enumerate(res):
        for n, a in zip(SMALL_PARAMS, _unpack(packed, small_w)):
            out.setdefault(n, [None] * 4)[kind] = a

    return ((loss, grad_x[None]) + tuple(out[n][0] for n in WEIGHTS) + tuple(out[n][1] for n in WEIGHTS)
            + tuple(out[n][2] for n in WEIGHTS) + tuple(out[n][3] for n in WEIGHTS))
```

```python
import jax
import jax.numpy as jnp
from jax import lax
from jax.experimental import pallas as pl
from jax.experimental.pallas import tpu as pltpu

F32 = jnp.float32
BF16 = jnp.bfloat16
GRAD_WIRE = BF16
FFN_ACT = BF16
BRANCH_CT = BF16

N_DEV = 8
LANES = 128
SUBLANES = 8
VMEM_BYTES = 64 * 1024 * 1024
HEAD_DIM = 64
SB_BLOCK = 256
SB_GROUP = 2
SLAB_GROUPS = 8
LN_EPS = 1e-5
ADAM_LR, ADAM_B1, ADAM_B2, ADAM_EPS, ADAM_WD, ADAM_STEP = 0.001, 0.9, 0.999, 1e-08, 0.01, 10
SB_UNDERFLOW = -120.0

PACK_ROWS = 256
MESH_AXES = ("x", "y", "c")


def _vmem_limit(block_bytes):
    return int(min(max(3 * block_bytes + (8 << 20), 24 << 20), VMEM_BYTES - (8 << 20)))


def _nbytes(shape, dtype):
    n = 1
    for d in shape:
        if d is not None:
            n *= d
    return n * jnp.dtype(dtype).itemsize


def _spec(shape, fn):
    return pl.BlockSpec(shape, fn)


class _Exchange:
    def __init__(self, scatter=(), gather=(), layered=()):
        self.arrs = list(scatter) + [a for a, _, _, _ in layered] + list(gather)
        self.n = len(self.arrs)
        self.n_sc = len(scatter) + len(layered)
        self.layer = [None] * len(scatter) + [l for _, l, _, _ in layered] + [None] * len(gather)
        self.shapes = ([a.shape for a in scatter] + [(N_DEV, dp) + a.shape[1:] for a, _, dp, _ in layered]
                       + [(N_DEV,) + a.shape for a in gather])
        self.held = [(len(scatter) + i, b) for i, (_, _, _, b) in enumerate(layered) if b is not None]
        self.operands = self.arrs + [b for _, b in self.held]
        hbm = pl.BlockSpec(memory_space=pltpu.HBM)
        self.in_specs = [hbm] * len(self.operands)
        self.out_specs = [hbm] * self.n
        self.out_shape = [jax.ShapeDtypeStruct(s, a.dtype) for s, a in zip(self.shapes, self.arrs)]
        self.scratch = [pltpu.SemaphoreType.DMA((self.n, N_DEV - 1)), pltpu.SemaphoreType.DMA((self.n, N_DEV - 1)),
                        pltpu.SemaphoreType.DMA((self.n,))]

    def aliases(self, first_in, first_out):
        return {first_in + self.n + i: first_out + a for i, (a, _) in enumerate(self.held)}

    def copies(self, ins, outs, sems):
        send_sems, recv_sems, own_sems = sems
        x, y, c = lax.axis_index("x"), lax.axis_index("y"), lax.axis_index("c")
        me = 4 * x + 2 * y + c
        landing = [outs[a].at[me] if self.layer[a] is None else outs[a].at[me, self.layer[a]] for a in range(self.n)]
        out = [pltpu.make_async_copy(ins[a].at[me] if a < self.n_sc else ins[a], landing[a], own_sems.at[a])
               for a in range(self.n)]
        for k in range(1, N_DEV):
            px = 1 - x if k & 4 else x
            py = 1 - y if k & 2 else y
            pc = 1 - c if k & 1 else c
            peer = 4 * px + 2 * py + pc
            for a in range(self.n):
                out.append(pltpu.make_async_remote_copy(
                    src_ref=ins[a].at[peer] if a < self.n_sc else ins[a], dst_ref=landing[a],
                    send_sem=send_sems.at[a, k - 1], recv_sem=recv_sems.at[a, k - 1],
                    device_id=(px, py, pc), device_id_type=pl.DeviceIdType.MESH))
        return out


def _exchange(name, scatter, gather, layered=()):
    ex = _Exchange(scatter, gather, layered)

    def body(*refs):
        copies = ex.copies(refs[:ex.n], refs[len(ex.operands):len(ex.operands) + ex.n], refs[-3:])
        for cp in copies:
            cp.start()
        for cp in copies:
            cp.wait()

    return pl.pallas_call(body, name=name, in_specs=ex.in_specs, out_specs=ex.out_specs, out_shape=ex.out_shape,
                          input_output_aliases=ex.aliases(0, 0), scratch_shapes=ex.scratch)(*ex.operands)


def _reduce_packed(name, packed, scatter):
    rows = packed.shape[0]
    blk = rows // N_DEV
    ex = _Exchange(scatter=[packed.reshape(N_DEV, blk, LANES)] + list(scatter))
    n_in = len(ex.operands)

    def body(*refs):
        ins, outs = refs[:ex.n], refs[n_in:n_in + ex.n]
        total_ref = refs[n_in + ex.n]
        sems, (send2, recv2, own2, load_sem) = refs[n_in + ex.n + 1:n_in + ex.n + 4], refs[n_in + ex.n + 4:-2]
        land_v, sum_v = refs[-2:]
        copies = ex.copies(ins, outs, sems)
        for cp in copies:
            cp.start()
        for cp in copies:
            cp.wait()
        load = pltpu.make_async_copy(outs[0], land_v, load_sem)
        load.start()
        load.wait()
        acc = land_v[0]
        for i in range(1, N_DEV):
            acc = acc + land_v[i]
        sum_v[...] = acc
        x, y, c = lax.axis_index("x"), lax.axis_index("y"), lax.axis_index("c")
        me = 4 * x + 2 * y + c
        back = [pltpu.make_async_copy(sum_v, total_ref.at[me], own2)]
        for k in range(1, N_DEV):
            peer = (1 - x if k & 4 else x, 1 - y if k & 2 else y, 1 - c if k & 1 else c)
            back.append(pltpu.make_async_remote_copy(
                src_ref=sum_v, dst_ref=total_ref.at[me], send_sem=send2.at[k - 1], recv_sem=recv2.at[k - 1],
                device_id=peer, device_id_type=pl.DeviceIdType.MESH))
        for cp in back:
            cp.start()
        for cp in back:
            cp.wait()

    hbm = pl.BlockSpec(memory_space=pltpu.HBM)
    res = pl.pallas_call(
        body, name=name, in_specs=ex.in_specs, out_specs=ex.out_specs + [hbm],
        out_shape=ex.out_shape + [jax.ShapeDtypeStruct((N_DEV, blk, LANES), F32)],
        scratch_shapes=ex.scratch + [pltpu.SemaphoreType.DMA((N_DEV - 1,)), pltpu.SemaphoreType.DMA((N_DEV - 1,)),
                                     pltpu.SemaphoreType.DMA, pltpu.SemaphoreType.DMA,
                                     pltpu.VMEM((N_DEV, blk, LANES), F32), pltpu.VMEM((blk, LANES), F32)],
    )(*ex.operands)
    return res[-1].reshape(rows, LANES), res[1:-1]


def _call_beside(ex, body, name, grid, in_specs, out_specs, out_shape, scratch_shapes, vmem_bytes, operands,
                 semantics, in_hbm=True):
    if in_hbm:
        operands = [_in_hbm(a) for a in operands]
    if ex is None:
        res = pl.pallas_call(
            body, name=name, grid=grid, in_specs=in_specs, out_specs=out_specs, out_shape=out_shape,
            scratch_shapes=scratch_shapes,
            compiler_params=pltpu.CompilerParams(dimension_semantics=semantics, vmem_limit_bytes=vmem_bytes),
        )(*operands)
        return res, None
    n_in, n_out, n_scr = len(in_specs), len(out_specs), len(scratch_shapes)
    n_xin = len(ex.operands)

    def fused(*refs):
        mine = refs[:n_in] + refs[n_in + n_xin:n_in + n_xin + n_out]
        mine += refs[n_in + n_xin + n_out + ex.n:n_in + n_xin + n_out + ex.n + n_scr]
        first = pl.program_id(0) == 0
        last = pl.program_id(0) == grid[0] - 1
        for dim in range(1, len(grid)):
            first = jnp.logical_and(first, pl.program_id(dim) == 0)
            last = jnp.logical_and(last, pl.program_id(dim) == grid[dim] - 1)
        x_ins = refs[n_in:n_in + ex.n]
        x_outs = refs[n_in + n_xin + n_out:n_in + n_xin + n_out + ex.n]

        @pl.when(first)
        def _():
            for cp in ex.copies(x_ins, x_outs, refs[-3:]):
                cp.start()

        body(*mine)

        @pl.when(last)
        def _():
            for cp in ex.copies(x_ins, x_outs, refs[-3:]):
                cp.wait()

    res = pl.pallas_call(
        fused, name=name, grid=grid, in_specs=list(in_specs) + ex.in_specs, out_specs=list(out_specs) + ex.out_specs,
        out_shape=list(out_shape) + ex.out_shape, input_output_aliases=ex.aliases(n_in, n_out),
        scratch_shapes=list(scratch_shapes) + ex.scratch,
        compiler_params=pltpu.CompilerParams(dimension_semantics=("arbitrary",) * len(grid),
                                             vmem_limit_bytes=vmem_bytes),
    )(*operands, *ex.operands)
    return res[:n_out], res[n_out:]


NN = (((1,), (0,)), ((), ()))
NT = (((1,), (1,)), ((), ()))
TN = (((0,), (0,)), ((), ()))


def _in_hbm(a):
    return pltpu.with_memory_space_constraint(a, pltpu.HBM)


def _mm(name, a, b, a_spec, b_spec, o_spec, o_shape, o_dtype, grid, dims, beside=None, reread=(False, True)):
    nk = grid[2]
    a, b = (x if again else _in_hbm(x) for x, again in zip((a, b), reread))
    acc_shape = tuple(d for d in o_spec.block_shape if d is not None)

    def product(a_ref, b_ref):
        return lax.dot_general(a_ref[...].astype(BF16), b_ref[...].astype(BF16), dims, preferred_element_type=F32)

    def body_once(a_ref, b_ref, o_ref):
        o_ref[...] = product(a_ref, b_ref).astype(o_ref.dtype)

    def body(a_ref, b_ref, o_ref, acc_ref):
        k = pl.program_id(2)

        @pl.when(k == 0)
        def _():
            acc_ref[...] = product(a_ref, b_ref)

        @pl.when(k > 0)
        def _():
            acc_ref[...] += product(a_ref, b_ref)

        @pl.when(k == nk - 1)
        def _():
            o_ref[...] = acc_ref[...].astype(o_ref.dtype)

    blk = (_nbytes(a_spec.block_shape, a.dtype) + _nbytes(b_spec.block_shape, b.dtype)
           + _nbytes(acc_shape, o_dtype) + _nbytes(acc_shape, F32))
    res, got = _call_beside(
        beside, body_once if nk == 1 else body, name, grid, [a_spec, b_spec], [o_spec],
        [jax.ShapeDtypeStruct(o_shape, o_dtype)], [] if nk == 1 else [pltpu.VMEM(acc_shape, F32)],
        _vmem_limit(blk), (a, b), ("parallel", "parallel", "arbitrary"), in_hbm=False)
    return res[0] if beside is None else (res[0], got)


def _mm_pieces(name, pieces, starts, width, step_block, other, other_spec, pieces_first, piece_rows, o_spec, o_shape,
               o_dtype, grid, dims, beside=None):
    n_p, nk = len(pieces), grid[2]
    acc_shape = tuple(s for s in o_spec.block_shape if s is not None)

    def which(i, j, k):
        blk = step_block(i, j, k)
        idx = 0
        for s in starts[1:]:
            idx = idx + (blk >= s).astype(jnp.int32)
        return idx, blk

    def piece_spec(p, t_rows):
        def index(i, j, k):
            idx, blk = which(i, j, k)
            mine = idx == p
            return jnp.where(mine, piece_rows(i, j, k), 0), jnp.where(mine, blk - starts[p], 0)
        return _spec((t_rows, width), index)

    def body(*refs):
        p_refs = refs[:n_p] if pieces_first else refs[1:1 + n_p]
        other_ref = refs[n_p] if pieces_first else refs[0]
        o_ref, acc_ref = refs[n_p + 1], refs[n_p + 2]
        i, j, k = pl.program_id(0), pl.program_id(1), pl.program_id(2)

        @pl.when(k == 0)
        def _():
            acc_ref[...] = jnp.zeros_like(acc_ref)

        idx, _ = which(i, j, k)
        for p in range(n_p):
            @pl.when(idx == p)
            def _(p=p):
                mine, fixed = p_refs[p][...].astype(BF16), other_ref[...].astype(BF16)
                pair = (mine, fixed) if pieces_first else (fixed, mine)
                acc_ref[...] += lax.dot_general(pair[0], pair[1], dims, preferred_element_type=F32)

        @pl.when(k == nk - 1)
        def _():
            o_ref[...] = acc_ref[...].astype(o_ref.dtype)

    t_rows = other_spec.block_shape[-2] if not pieces_first else o_spec.block_shape[-2]
    specs = [piece_spec(p, t_rows) for p in range(n_p)]
    in_specs = specs + [other_spec] if pieces_first else [other_spec] + specs
    operands = list(pieces) + [other] if pieces_first else [other] + list(pieces)
    blk = (n_p * 4 * t_rows * width + _nbytes(other_spec.block_shape, other.dtype)
           + _nbytes(acc_shape, o_dtype) + _nbytes(acc_shape, F32))
    res, got = _call_beside(
        beside, body, name, grid, in_specs, [o_spec], [jax.ShapeDtypeStruct(o_shape, o_dtype)],
        [pltpu.VMEM(acc_shape, F32)], _vmem_limit(blk), operands, ("parallel", "parallel", "arbitrary"), in_hbm=False)
    return res[0] if beside is None else (res[0], got)


def _swiglu_fn(gate_up):
    gate, up = gate_up[0], gate_up[1]
    return gate * jax.nn.sigmoid(gate) * up


def _ffn_in_swiglu(name, h, w, t_m):
    seq, d = h.shape
    n_half, n = w.shape[0] // 2, w.shape[2]

    def body(h_ref, wg_ref, wu_ref, a_ref, f_ref):
        hb = h_ref[...]
        a_ref[0] = lax.dot_general(hb, wg_ref[...], NN, preferred_element_type=F32).astype(a_ref.dtype)
        a_ref[1] = lax.dot_general(hb, wu_ref[...], NN, preferred_element_type=F32).astype(a_ref.dtype)
        f_ref[...] = _swiglu_fn(a_ref[...].astype(F32)).astype(f_ref.dtype)

    blk = 2 * t_m * d + 4 * d * n + 6 * t_m * n + 12 * t_m * n
    return pl.pallas_call(
        body, name=name, grid=(seq // t_m, n_half),
        in_specs=[_spec((t_m, d), lambda i, j: (i, 0)), _spec((None, d, n), lambda i, j: (j, 0, 0)),
                  _spec((None, d, n), lambda i, j: (j + n_half, 0, 0))],
        out_specs=[_spec((2, None, t_m, n), lambda i, j: (0, j, i, 0)), _spec((None, t_m, n), lambda i, j: (j, i, 0))],
        out_shape=[jax.ShapeDtypeStruct((2, n_half, seq, n), FFN_ACT), jax.ShapeDtypeStruct((n_half, seq, n), BF16)],
        compiler_params=pltpu.CompilerParams(dimension_semantics=("parallel", "parallel"),
                                             vmem_limit_bytes=_vmem_limit(blk)),
    )(h, w, w)


def _ffn_out_dx_swiglu(name, d_y, w, a, t_m):
    seq, d = d_y.shape
    n_half, n = w.shape[0], w.shape[1]

    def body(dy_ref, w_ref, a_ref, da_ref):
        d_f = lax.dot_general(dy_ref[...], w_ref[...], NT, preferred_element_type=F32)
        gate, up = a_ref[0].astype(F32), a_ref[1].astype(F32)
        s = jax.nn.sigmoid(gate)
        gs = gate * s
        da_ref[0] = (d_f * up * (s + gs * (1.0 - s))).astype(da_ref.dtype)
        da_ref[1] = (d_f * gs).astype(da_ref.dtype)

    blk = 2 * t_m * d + 2 * d * n + 8 * t_m * n + 24 * t_m * n
    return pl.pallas_call(
        body, name=name, grid=(seq // t_m, n_half),
        in_specs=[_spec((t_m, d), lambda i, j: (i, 0)), _spec((None, n, d), lambda i, j: (j, 0, 0)),
                  _spec((2, None, t_m, n), lambda i, j: (0, j, i, 0))],
        out_specs=_spec((2, None, t_m, n), lambda i, j: (0, j, i, 0)),
        out_shape=jax.ShapeDtypeStruct((2, n_half, seq, n), BF16),
        compiler_params=pltpu.CompilerParams(dimension_semantics=("parallel", "parallel"),
                                             vmem_limit_bytes=_vmem_limit(blk)),
    )(d_y, w, a)


def _merge_fn(y_sb, y_ssm, gates):
    half = gates.shape[-1] // 2
    return jax.nn.sigmoid(gates[:, :half]) * y_sb + jax.nn.sigmoid(gates[:, half:]) * y_ssm


def _up_merge(name, o_sb, s5_out, proj, gates_cb, w_sb, w_ssm, layer, t_rows):
    seq = o_sb.shape[0]
    d = w_sb.shape[2]

    def body(o_ref, s_ref, g_ref, w1_ref, w2_ref, m_ref, y1_ref, y2_ref):
        y_sb = lax.dot_general(o_ref[...], w1_ref[...], NN, preferred_element_type=F32)
        y_ssm = lax.dot_general(s_ref[...], w2_ref[...], NN, preferred_element_type=F32)
        m_ref[...] = _merge_fn(y_sb, y_ssm, g_ref[...]).astype(m_ref.dtype)
        y1_ref[...] = y_sb.astype(y1_ref.dtype)
        y2_ref[...] = y_ssm.astype(y2_ref.dtype)

    row = lambda width: _spec((t_rows, width), lambda i: (i, 0))
    whole = lambda w: _spec((None,) + w.shape[1:], lambda i: (layer, 0, 0))
    blk = t_rows * (2 * o_sb.shape[1] + 2 * s5_out.shape[1] + 8 * d + 6 * d + 24 * d) + 4 * d * (o_sb.shape[1] + s5_out.shape[1])
    return pl.pallas_call(
        body, name=name, grid=(seq // t_rows,),
        in_specs=[row(o_sb.shape[1]), row(s5_out.shape[1]), _spec((t_rows, 2 * d), lambda i: (i, gates_cb)),
                  whole(w_sb), whole(w_ssm)],
        out_specs=[row(d)] * 3, out_shape=[jax.ShapeDtypeStruct((seq, d), BF16)] * 3,
        compiler_params=pltpu.CompilerParams(dimension_semantics=("parallel",), vmem_limit_bytes=_vmem_limit(blk)),
    )(_in_hbm(o_sb), _in_hbm(s5_out), _in_hbm(proj), w_sb, w_ssm)


def _tile(n, pref=1024):
    t = pref
    while t >= LANES:
        if n % t == 0:
            return t
        t -= LANES
    return n


def _rowwise(name, fn, ins, outs, grid):
    n_in = len(ins)

    def body(*refs):
        vals = fn(*[r[...].astype(F32) for r in refs[:n_in]])
        if not isinstance(vals, (tuple, list)):
            vals = (vals,)
        for r, v in zip(refs[n_in:], vals):
            r[...] = v.astype(r.dtype)

    blk = sum(_nbytes(bs, a.dtype) for a, bs, _ in ins) + sum(_nbytes(bs, d) + _nbytes(bs, F32) for _, d, bs, _ in outs)
    return pl.pallas_call(
        body, name=name, grid=grid,
        in_specs=[_spec(bs, im) for _, bs, im in ins],
        out_specs=[_spec(bs, im) for _, _, bs, im in outs],
        out_shape=[jax.ShapeDtypeStruct(s, d) for s, d, _, _ in outs],
        compiler_params=pltpu.CompilerParams(dimension_semantics=("parallel",) * len(grid),
                                             vmem_limit_bytes=_vmem_limit(2 * blk)),
    )(*[_in_hbm(a) for a, _, _ in ins])


def _rowwise_vjp(name, fn, ins, cts, wrt, grid):
    n_in, n_ct = len(ins), len(cts)
    idx = [w[0] for w in wrt]

    def body(*refs):
        prim = [r[...].astype(F32) for r in refs[:n_in]]
        ct = tuple(r[...].astype(F32) for r in refs[n_in:n_in + n_ct])
        o_refs = refs[n_in + n_ct:]

        def g(*sel):
            full = list(prim)
            for i, s in zip(idx, sel):
                full[i] = s
            out = fn(*full)
            return tuple(out) if isinstance(out, (tuple, list)) else (out,)

        _, pull = jax.vjp(g, *[prim[i] for i in idx])
        grads = pull(ct)
        first = pl.program_id(0) == 0
        for d in range(1, len(grid)):
            first = jnp.logical_and(first, pl.program_id(d) == 0)
        for w, o_ref, gr in zip(wrt, o_refs, grads):
            if w[1] == "row":
                o_ref[...] = gr.astype(o_ref.dtype)
            else:
                @pl.when(first)
                def _(o_ref=o_ref):
                    o_ref[...] = jnp.zeros_like(o_ref)

                o_ref[...] += gr.astype(o_ref.dtype)

    blk = (sum(_nbytes(bs, a.dtype) + _nbytes(bs, F32) for a, bs, _ in list(ins) + list(cts))
           + sum(_nbytes(w[4], w[3]) + _nbytes(w[4], F32) for w in wrt))
    return pl.pallas_call(
        body, name=name, grid=grid,
        in_specs=[_spec(bs, im) for _, bs, im in list(ins) + list(cts)],
        out_specs=[_spec(w[4], w[5]) for w in wrt],
        out_shape=[jax.ShapeDtypeStruct(w[2], w[3]) for w in wrt],
        compiler_params=pltpu.CompilerParams(dimension_semantics=("arbitrary",) * len(grid),
                                             vmem_limit_bytes=_vmem_limit(2 * blk)),
    )(*[_in_hbm(a) for a, _, _ in list(ins) + list(cts)])


def _normalize(x):
    mu = jnp.mean(x, axis=-1, keepdims=True)
    xc = x - mu
    var = jnp.mean(xc * xc, axis=-1, keepdims=True)
    return xc * lax.rsqrt(var + LN_EPS)


def _modulate(x, sc, sh):
    return _normalize(x) * (1.0 + sc) + sh


def _make_resid_fns(alpha):
    def resid_ln(x, y, gate, g, b):
        return _normalize(alpha * x + (1.0 + gate) * y) * g + b

    def resid_ln_mod(x, y, gate, g, b, sc, sh):
        xn = resid_ln(x, y, gate, g, b)
        return xn, _modulate(xn, sc, sh)

    return resid_ln, resid_ln_mod


def _s5_act_fn(yc, u, d_skip):
    return jax.nn.gelu(yc + d_skip * u)


def _s5_gate_fn(y1, t):
    return y1 * jax.nn.sigmoid(t)


def _s5_head_specs(yc, proj, u_cb, w_glu, layer, t_rows):
    width = yc.shape[1]
    row = _spec((t_rows, width), lambda i: (i, 0))
    u_spec = _spec((t_rows, width), lambda i: (i, u_cb))
    vec = _spec((1, width), lambda i: (0, 0))
    w_spec = _spec((None,) + w_glu.shape[1:], lambda i: (layer, 0, 0))
    return row, u_spec, vec, w_spec


def _s5_head(name, yc, proj, u_cb, d_skip, b_glu, w_glu, layer, t_rows):
    seq, width = yc.shape
    row, u_spec, vec, w_spec = _s5_head_specs(yc, proj, u_cb, w_glu, layer, t_rows)

    def body(yc_ref, u_ref, d_ref, b_ref, w_ref, o_ref):
        y1 = _s5_act_fn(yc_ref[...], u_ref[...], d_ref[...])
        t = lax.dot_general(y1.astype(BF16), w_ref[...], NN, preferred_element_type=F32) + b_ref[...]
        o_ref[...] = _s5_gate_fn(y1, t).astype(o_ref.dtype)

    return pl.pallas_call(
        body, name=name, grid=(seq // t_rows,), in_specs=[row, u_spec, vec, vec, w_spec], out_specs=row,
        out_shape=jax.ShapeDtypeStruct((seq, width), BF16),
        compiler_params=pltpu.CompilerParams(dimension_semantics=("parallel",),
                                             vmem_limit_bytes=_vmem_limit(40 * t_rows * width)),
    )(_in_hbm(yc), _in_hbm(proj), d_skip, b_glu, w_glu)


def _s5_head_bwd(name, yc, proj, u_cb, d_out, d_skip, b_glu, w_glu, layer, t_rows):
    seq, width = yc.shape
    row, u_spec, vec, w_spec = _s5_head_specs(yc, proj, u_cb, w_glu, layer, t_rows)
    n_t = seq // t_rows

    def body(yc_ref, u_ref, do_ref, d_ref, b_ref, w_ref, dyc_ref, du_ref, dw_ref, dd_ref, db_ref, acc_ref):
        i = pl.program_id(0)

        @pl.when(i == 0)
        def _():
            acc_ref[...] = jnp.zeros_like(acc_ref)
            dd_ref[...] = jnp.zeros_like(dd_ref)
            db_ref[...] = jnp.zeros_like(db_ref)

        y1, pull_act = jax.vjp(_s5_act_fn, yc_ref[...], u_ref[...], d_ref[...])
        y1_b = y1.astype(BF16)
        t = lax.dot_general(y1_b, w_ref[...], NN, preferred_element_type=F32) + b_ref[...]
        _, pull_gate = jax.vjp(_s5_gate_fn, y1, t)
        d_y1, d_t = pull_gate(do_ref[...].astype(F32))
        d_t_b = d_t.astype(BF16)
        d_y1 = d_y1 + lax.dot_general(d_t_b, w_ref[...], NT, preferred_element_type=F32)
        acc_ref[...] += lax.dot_general(y1_b, d_t_b, TN, preferred_element_type=F32)
        db_ref[...] += jnp.sum(d_t, axis=0, keepdims=True)
        d_yc, d_u, d_d = pull_act(d_y1)
        dyc_ref[...] = d_yc
        du_ref[...] = d_u
        dd_ref[...] += d_d

        @pl.when(i == n_t - 1)
        def _():
            dw_ref[...] = acc_ref[...].astype(dw_ref.dtype)

    whole = _spec((width, width), lambda i: (0, 0))
    return pl.pallas_call(
        body, name=name, grid=(n_t,), in_specs=[row, u_spec, row, vec, vec, w_spec],
        out_specs=[row, row, whole, vec, vec],
        out_shape=[jax.ShapeDtypeStruct((seq, width), F32), jax.ShapeDtypeStruct((seq, width), F32),
                   jax.ShapeDtypeStruct((width, width), GRAD_WIRE), jax.ShapeDtypeStruct((1, width), F32),
                   jax.ShapeDtypeStruct((1, width), F32)],
        scratch_shapes=[pltpu.VMEM((width, width), F32)],
        compiler_params=pltpu.CompilerParams(dimension_semantics=("arbitrary",),
                                             vmem_limit_bytes=_vmem_limit(80 * t_rows * width)),
    )(_in_hbm(yc), _in_hbm(proj), _in_hbm(d_out), d_skip, b_glu, w_glu)


def _sb_tri(kind):
    row = lax.broadcasted_iota(jnp.int32, (SB_BLOCK, SB_BLOCK), 0)
    col = lax.broadcasted_iota(jnp.int32, (SB_BLOCK, SB_BLOCK), 1)
    if kind == "after":
        return (row > col).astype(BF16)
    if kind == "from":
        return (row >= col).astype(BF16)
    return col < row


def _split_dot(x, m):
    hi = x.astype(BF16)
    lo = (x - hi.astype(F32)).astype(BF16)
    return (lax.dot_general(hi, m, NN, preferred_element_type=F32)
            + lax.dot_general(lo, m, NN, preferred_element_type=F32))


def _sb_scores(qh, k2):
    z = lax.dot_general(qh, k2, NT, preferred_element_type=F32)
    log_beta = jnp.minimum(z, 0.0) - jnp.log(1.0 + jnp.exp(-jnp.abs(z)))
    return log_beta, log_beta - z


def _sb_attention_fwd(proj, sb_width, beside=None):
    seq = proj.shape[0]
    n_pair, n_q = sb_width // LANES, seq // (SB_BLOCK * SB_GROUP)
    scale = 1.0 / (HEAD_DIM ** 0.5)
    chains = [(s, h) for s in range(SB_GROUP) for h in range(2)]

    def body(q_ref, k_ref, v_ref, o_ref, o32_ref):
        first = pl.program_id(1) * SB_GROUP
        lane = lax.broadcasted_iota(jnp.int32, (SB_BLOCK, LANES), 1)
        m_after, causal = _sb_tri("after"), _sb_tri("mask")
        heads = [lane < HEAD_DIM, lane >= HEAD_DIM]
        rows = [pl.ds(s * SB_BLOCK, SB_BLOCK) for s in range(SB_GROUP)]
        qh = {(s, h): (jnp.where(heads[h], q_ref[rows[s], :], 0.0) * scale).astype(BF16) for s, h in chains}

        def key_rows(s, r):
            kb = first + s - r
            return kb >= 0, pl.ds(pl.multiple_of(jnp.maximum(kb, 0) * SB_BLOCK, SB_BLOCK), SB_BLOCK)

        def scores(r, diag):
            out = []
            for s in range(SB_GROUP):
                live, ks = key_rows(s, r)
                k2 = k_ref[ks, :].astype(BF16)
                for h in range(2):
                    log_beta, log_1m = _sb_scores(qh[s, h], k2)
                    if diag:
                        log_1m = jnp.where(causal, log_1m, 0.0)
                    else:
                        log_1m = jnp.where(live, log_1m, 0.0)
                    out += [log_beta + _split_dot(log_1m, m_after), jnp.sum(log_1m, axis=1, keepdims=True)]
            return tuple(out)

        def weigh(r, sc, carry, acc, diag):
            out = []
            for c, (s, h) in enumerate(chains):
                live, ks = key_rows(s, r)
                v2 = v_ref[ks, :].astype(BF16)
                w = jnp.exp(sc[2 * c] + carry[c])
                w = jnp.where(causal, w, 0.0) if diag else jnp.where(live, w, 0.0)
                out.append(acc[c] + lax.dot_general(w.astype(BF16), v2, NN, preferred_element_type=F32))
            return tuple(out)

        zero = jnp.zeros((SB_BLOCK, LANES), F32)
        zcol = jnp.zeros((SB_BLOCK, 1), F32)
        sc = scores(0, True)
        acc = weigh(0, sc, (zcol,) * len(chains), (zero,) * len(chains), True)
        carry = tuple(sc[2 * c + 1] for c in range(len(chains)))
        last = first + SB_GROUP - 1

        def loop(st):
            r, carry, acc = st
            sc = scores(r, False)
            after = tuple(carry[c] + sc[2 * c + 1] for c in range(len(chains)))
            top = jnp.max(after[0])
            for c in range(1, len(chains)):
                top = jnp.maximum(top, jnp.max(after[c]))
            acc = weigh(r, sc, carry, acc, False)
            return jnp.where(top < SB_UNDERFLOW, last + 1, r + 1), after, acc

        _, _, acc = lax.while_loop(lambda st: st[0] <= last, loop, (1, carry, acc))
        for s in range(SB_GROUP):
            out = jnp.where(heads[0], acc[2 * s], acc[2 * s + 1])
            o_ref[rows[s], :] = out.astype(o_ref.dtype)
            o32_ref[rows[s], :] = out

    q_spec = _spec((SB_BLOCK * SB_GROUP, LANES), lambda h, i: (i, h))
    kv = [_spec((seq, LANES), lambda h, i, o=o: (0, o + h)) for o in (n_pair, 2 * n_pair)]
    return _call_beside(
        beside, body, "sb_attention_fwd", (n_pair, n_q), [q_spec] + kv, [q_spec, q_spec],
        [jax.ShapeDtypeStruct((seq, sb_width), BF16), jax.ShapeDtypeStruct((seq, sb_width), F32)], [],
        _vmem_limit(2 * seq * LANES * 4), (proj, proj, proj), ("parallel", "arbitrary"))


def _sb_attention_bwd(proj, o32, do, sb_width, beside=None):
    seq = proj.shape[0]
    n_pair, n_q = sb_width // LANES, seq // (SB_BLOCK * SB_GROUP)
    scale = 1.0 / (HEAD_DIM ** 0.5)
    chains = [(s, h) for s in range(SB_GROUP) for h in range(2)]
    n_c = len(chains)

    def body(q_ref, k_ref, v_ref, o_ref, do_ref, dq_ref, dk_out_ref, dv_out_ref, dk_ref, dv_ref):
        qi = pl.program_id(1)
        first = qi * SB_GROUP

        @pl.when(qi == 0)
        def _():
            dk_ref[...] = jnp.zeros_like(dk_ref)
            dv_ref[...] = jnp.zeros_like(dv_ref)

        lane = lax.broadcasted_iota(jnp.int32, (SB_BLOCK, LANES), 1)
        m_after, m_from, causal = _sb_tri("after"), _sb_tri("from"), _sb_tri("mask")
        heads = [lane < HEAD_DIM, lane >= HEAD_DIM]
        rows = [pl.ds(s * SB_BLOCK, SB_BLOCK) for s in range(SB_GROUP)]
        qh, doh_b, total = {}, {}, {}
        for s, h in chains:
            qh[s, h] = (jnp.where(heads[h], q_ref[rows[s], :], 0.0) * scale).astype(BF16)
            doh = jnp.where(heads[h], do_ref[rows[s], :].astype(F32), 0.0)
            doh_b[s, h] = doh.astype(BF16)
            total[s, h] = jnp.sum(doh * o_ref[rows[s], :], axis=1, keepdims=True)

        def key_rows(s, r):
            kb = first + s - r
            return kb >= 0, pl.ds(pl.multiple_of(jnp.maximum(kb, 0) * SB_BLOCK, SB_BLOCK), SB_BLOCK)

        def scores(r, diag):
            out = []
            for s in range(SB_GROUP):
                live, ks = key_rows(s, r)
                k2 = k_ref[ks, :].astype(BF16)
                v2 = v_ref[ks, :].astype(BF16)
                for h in range(2):
                    log_beta, log_1m = _sb_scores(qh[s, h], k2)
                    log_1m = jnp.where(causal, log_1m, 0.0) if diag else jnp.where(live, log_1m, 0.0)
                    out += [log_beta + _split_dot(log_1m, m_after), jnp.sum(log_1m, axis=1, keepdims=True),
                            lax.dot_general(doh_b[s, h], v2, NT, preferred_element_type=F32), log_beta]
            return tuple(out)

        def pull(r, sc, carry, right, dq, diag):
            right_out, dq_out = [], []
            for s in range(SB_GROUP):
                live, ks = key_rows(s, r)
                k2 = k_ref[ks, :].astype(BF16)
                dv_blk, dk_blk = None, None
                for h in range(2):
                    c = 2 * s + h
                    arg, _, d_w, log_beta = sc[4 * c:4 * c + 4]
                    w = jnp.exp(arg + carry[c])
                    w = jnp.where(causal, w, 0.0) if diag else jnp.where(live, w, 0.0)
                    w_b = w.astype(BF16)
                    d_arg = d_w * w_b.astype(F32)
                    dv_h = lax.dot_general(w_b, doh_b[s, h], TN, preferred_element_type=F32)
                    d_log_1m = total[s, h] - right[c] - _split_dot(d_arg, m_from)
                    beta = jnp.exp(log_beta)
                    dz = d_arg * (1.0 - beta) - beta * d_log_1m
                    dz = jnp.where(causal, dz, 0.0) if diag else jnp.where(live, dz, 0.0)
                    dz_b = dz.astype(BF16)
                    dk_h = lax.dot_general(dz_b, qh[s, h], TN, preferred_element_type=F32)
                    dv_blk = dv_h if h == 0 else dv_blk + dv_h
                    dk_blk = dk_h if h == 0 else dk_blk + dk_h
                    dq_out.append(dq[c] + lax.dot_general(dz_b, k2, NN, preferred_element_type=F32))
                    right_out.append(right[c] + jnp.sum(d_arg, axis=1, keepdims=True))
                dv_ref[ks, :] += dv_blk
                dk_ref[ks, :] += dk_blk
            return tuple(right_out), tuple(dq_out)

        zero = jnp.zeros((SB_BLOCK, LANES), F32)
        zcol = jnp.zeros((SB_BLOCK, 1), F32)
        sc = scores(0, True)
        right, dq = pull(0, sc, (zcol,) * n_c, (zcol,) * n_c, (zero,) * n_c, True)
        carry = tuple(sc[4 * c + 1] for c in range(n_c))
        last = first + SB_GROUP - 1

        def loop(st):
            r, carry, right, dq = st
            sc = scores(r, False)
            after = tuple(carry[c] + sc[4 * c + 1] for c in range(n_c))
            top = jnp.max(after[0])
            for c in range(1, n_c):
                top = jnp.maximum(top, jnp.max(after[c]))
            right, dq = pull(r, sc, carry, right, dq, False)
            return jnp.where(top < SB_UNDERFLOW, last + 1, r + 1), after, right, dq

        _, _, _, dq = lax.while_loop(lambda st: st[0] <= last, loop, (1, carry, right, dq))
        for s in range(SB_GROUP):
            dq_ref[rows[s], :] = (jnp.where(heads[0], dq[2 * s], dq[2 * s + 1]) * scale).astype(dq_ref.dtype)

        @pl.when(qi == n_q - 1)
        def _():
            dk_out_ref[...] = dk_ref[...].astype(dk_out_ref.dtype)
            dv_out_ref[...] = dv_ref[...].astype(dv_out_ref.dtype)

    q_spec = _spec((SB_BLOCK * SB_GROUP, LANES), lambda h, i: (i, h))
    kv = [_spec((seq, LANES), lambda h, i, o=o: (0, o + h)) for o in (n_pair, 2 * n_pair)]
    full = _spec((seq, LANES), lambda h, i: (0, h))
    return _call_beside(
        beside, body, "sb_attention_bwd", (n_pair, n_q), [q_spec] + kv + [q_spec, q_spec], [q_spec, full, full],
        [jax.ShapeDtypeStruct((seq, sb_width), BF16)] * 3,
        [pltpu.VMEM((seq, LANES), F32), pltpu.VMEM((seq, LANES), F32)],
        _vmem_limit(4 * seq * LANES * 4), (proj, proj, proj, o32, do), ("parallel", "arbitrary"))


def _s5_discretize(a_re, a_im, log_dt, b_re, b_im, c_re, c_im):
    n_g, n_p = a_re.shape
    c_g = b_re.shape[-1]
    ns = n_g // SLAB_GROUPS
    dt = jnp.exp(log_dt)[:, None]
    xr, xi = a_re * dt, a_im * dt
    mag = jnp.exp(xr)
    lr, li = mag * jnp.cos(xi), mag * jnp.sin(xi)
    den = a_re * a_re + a_im * a_im
    fr = ((lr - 1.0) * a_re + li * a_im) / den
    fi = (li * a_re - (lr - 1.0) * a_im) / den
    bb_re = fr[..., None] * b_re - fi[..., None] * b_im
    bb_im = fr[..., None] * b_im + fi[..., None] * b_re
    eye = jnp.eye(SLAB_GROUPS, dtype=F32)

    def diag_b(m):
        m = jnp.transpose(m.reshape(ns, SLAB_GROUPS, n_p, c_g), (0, 1, 3, 2))
        m = m[:, :, :, None, :] * eye[None, :, None, :, None]
        return m.reshape(ns, SLAB_GROUPS * c_g, SLAB_GROUPS * n_p)

    def diag_c(m):
        m = jnp.transpose(m.reshape(ns, SLAB_GROUPS, c_g, n_p), (0, 1, 3, 2))
        m = m[:, :, :, None, :] * eye[None, :, None, :, None]
        return m.reshape(ns, SLAB_GROUPS * n_p, SLAB_GROUPS * c_g)

    bs = jnp.concatenate([diag_b(bb_re), diag_b(bb_im)], axis=-1)
    cs = jnp.concatenate([diag_c(c_re), -diag_c(c_im)], axis=1)
    lam = jnp.concatenate([lr.reshape(ns, 1, -1), li.reshape(ns, 1, -1)], axis=-1)
    return bs, cs, lam


def _s5_powers(a_re, a_im, log_dt, n):
    n_g, n_p = a_re.shape
    ns = n_g // SLAB_GROUPS
    dt = jnp.exp(log_dt)[:, None]
    mag = jnp.exp(a_re * dt)
    base_r, base_i = mag * jnp.cos(a_im * dt), mag * jnp.sin(a_im * dt)
    steps = jnp.arange(1, n + 1, dtype=jnp.int32)[:, None, None]
    pr, pi = jnp.ones((n, n_g, n_p), F32), jnp.zeros((n, n_g, n_p), F32)
    for b in range(n.bit_length()):
        take = ((steps >> b) & 1) == 1
        pr, pi = (jnp.where(take, pr * base_r - pi * base_i, pr), jnp.where(take, pr * base_i + pi * base_r, pi))
        base_r, base_i = base_r * base_r - base_i * base_i, 2.0 * base_r * base_i

    def slabs(re, im):
        one = lambda m: jnp.transpose(m.reshape(n, ns, SLAB_GROUPS * n_p), (1, 0, 2))
        return jnp.concatenate([one(re), one(im)], axis=-1)

    return slabs(pr, pi), slabs(pr[::-1], -pi[::-1])


def _lanes(j):
    return slice(j * LANES, (j + 1) * LANES)


def _tile8(k):
    return pl.ds(pl.multiple_of(k * SUBLANES, SUBLANES), SUBLANES)


def _s5_interleave(dst_ref, src_ref, t_seg):
    def body(k, _):
        dst_ref[_tile8(k), :] = src_ref[pl.ds(k, SUBLANES, stride=t_seg), :]
        return 0

    lax.fori_loop(0, t_seg, body, 0, unroll=4)


def _s5_join_segments(st_ref, end_ref, car_ref, tab_ref, row, order, n_pair):
    for j in range(n_pair):
        re, im = _lanes(j), _lanes(n_pair + j)
        cr, ci = st_ref[:, re], st_ref[:, im]
        tr, ti = tab_ref[row:row + 1, re], tab_ref[row:row + 1, im]
        for s in order:
            car_ref[s:s + 1, re] = cr
            car_ref[s:s + 1, im] = ci
            er, ei = end_ref[s:s + 1, re], end_ref[s:s + 1, im]
            cr, ci = er + tr * cr - ti * ci, ei + tr * ci + ti * cr
        st_ref[:, re] = cr
        st_ref[:, im] = ci


def _s5_add_carries(buf_ref, car_ref, tab_ref, t_seg, n_pair):
    def fix(k, _):
        rows = _tile8(k)
        tab = tab_ref[pl.ds(k, 1), :]
        for j in range(n_pair):
            re, im = _lanes(j), _lanes(n_pair + j)
            cr, ci = car_ref[:, re], car_ref[:, im]
            tr, ti = tab[:, re], tab[:, im]
            buf_ref[rows, re] += tr * cr - ti * ci
            buf_ref[rows, im] += tr * ci + ti * cr
        return 0

    lax.fori_loop(0, t_seg, fix, 0, unroll=2)


def _s5_scan_fwd(proj, u_col, bs, cs, lam, pw, t_blk, beside=None):
    seq = proj.shape[0]
    ns, _, w2 = bs.shape
    n_pair = w2 // (2 * LANES)
    t_seg, n_t = t_blk // SUBLANES, seq // t_blk

    def body(u_ref, bs_ref, cs_ref, lam_ref, pw_ref, yc_ref, h_ref, st_ref, end_ref, car_ref, ui_ref, bu_ref, yi_ref):
        @pl.when(pl.program_id(1) == 0)
        def _():
            st_ref[...] = jnp.zeros_like(st_ref)

        _s5_interleave(ui_ref, u_ref, t_seg)
        bu_ref[...] = lax.dot_general(ui_ref[...].astype(BF16), bs_ref[...], NN, preferred_element_type=F32)
        lam_r = [jnp.broadcast_to(lam_ref[:, _lanes(j)], (SUBLANES, LANES)) for j in range(n_pair)]
        lam_i = [jnp.broadcast_to(lam_ref[:, _lanes(n_pair + j)], (SUBLANES, LANES)) for j in range(n_pair)]

        def step(k, c):
            rows = _tile8(k)
            out = []
            for j in range(n_pair):
                hr, hi = c[2 * j], c[2 * j + 1]
                nr = lam_r[j] * hr - lam_i[j] * hi + bu_ref[rows, _lanes(j)]
                ni = lam_i[j] * hr + lam_r[j] * hi + bu_ref[rows, _lanes(n_pair + j)]
                h_ref[rows, _lanes(j)] = nr
                h_ref[rows, _lanes(n_pair + j)] = ni
                out += [nr, ni]
            return tuple(out)

        ends = lax.fori_loop(0, t_seg, step, (jnp.zeros((SUBLANES, LANES), F32),) * (2 * n_pair), unroll=4)
        for j in range(n_pair):
            end_ref[:, _lanes(j)] = ends[2 * j]
            end_ref[:, _lanes(n_pair + j)] = ends[2 * j + 1]
        _s5_join_segments(st_ref, end_ref, car_ref, pw_ref, t_seg - 1, list(range(SUBLANES)), n_pair)
        _s5_add_carries(h_ref, car_ref, pw_ref, t_seg, n_pair)
        yi_ref[...] = lax.dot_general(h_ref[...].astype(BF16), cs_ref[...], NN, preferred_element_type=F32)

        def scatter(k, _):
            yc_ref[pl.ds(k, SUBLANES, stride=t_seg), :] = yi_ref[_tile8(k), :]
            return 0

        lax.fori_loop(0, t_seg, scatter, 0, unroll=4)

    return _call_beside(
        beside, body, "s5_scan_fwd", (ns, n_t),
        [_spec((t_blk, LANES), lambda s, i: (i, u_col + s)),
         _spec((None, LANES, w2), lambda s, i: (s, 0, 0)),
         _spec((None, w2, LANES), lambda s, i: (s, 0, 0)),
         _spec((None, 1, w2), lambda s, i: (s, 0, 0)),
         _spec((None, t_seg, w2), lambda s, i: (s, 0, 0))],
        [_spec((t_blk, LANES), lambda s, i: (i, s)),
         _spec((None, t_blk, w2), lambda s, i: (s, i, 0))],
        [jax.ShapeDtypeStruct((seq, ns * LANES), F32), jax.ShapeDtypeStruct((ns, seq, w2), F32)],
        [pltpu.VMEM((1, w2), F32), pltpu.VMEM((SUBLANES, w2), F32), pltpu.VMEM((SUBLANES, w2), F32),
         pltpu.VMEM((t_blk, LANES), F32), pltpu.VMEM((t_blk, w2), F32), pltpu.VMEM((t_blk, LANES), F32)],
        _vmem_limit(3 * t_blk * w2 * 4), (proj, bs, cs, lam, pw), ("parallel", "arbitrary"))


def _s5_scan_bwd(proj, u_col, states, d_yc, du_extra, bs, cs, lam, qw, t_blk):
    seq = proj.shape[0]
    ns, _, w2 = bs.shape
    n_pair = w2 // (2 * LANES)
    t_seg, n_t = t_blk // SUBLANES, seq // t_blk

    def body(u_ref, h_ref, hp_ref, dyc_ref, dux_ref, bs_ref, cs_ref, lam_ref, qw_ref,
             du_ref, dbs_ref, dcs_ref, dlam_ref, g_ref, gd_ref, st_ref, end_ref, car_ref, ui_ref, dyi_ref, dui_ref):
        i = pl.program_id(1)

        @pl.when(i == 0)
        def _():
            st_ref[...] = jnp.zeros_like(st_ref)
            dbs_ref[...] = jnp.zeros_like(dbs_ref)
            dcs_ref[...] = jnp.zeros_like(dcs_ref)
            dlam_ref[...] = jnp.zeros_like(dlam_ref)

        _s5_interleave(ui_ref, u_ref, t_seg)
        _s5_interleave(dyi_ref, dyc_ref, t_seg)
        dyc_b = dyi_ref[...].astype(BF16)
        gd_ref[...] = lax.dot_general(dyc_b, cs_ref[...], NT, preferred_element_type=F32)
        lam_r = [jnp.broadcast_to(lam_ref[:, _lanes(j)], (SUBLANES, LANES)) for j in range(n_pair)]
        lam_i = [jnp.broadcast_to(lam_ref[:, _lanes(n_pair + j)], (SUBLANES, LANES)) for j in range(n_pair)]

        def step(kk, c):
            rows = _tile8(t_seg - 1 - kk)
            out = []
            for j in range(n_pair):
                gr_n, gi_n = c[2 * j], c[2 * j + 1]
                gr = gd_ref[rows, _lanes(j)] + lam_r[j] * gr_n + lam_i[j] * gi_n
                gi = gd_ref[rows, _lanes(n_pair + j)] + lam_r[j] * gi_n - lam_i[j] * gr_n
                g_ref[rows, _lanes(j)] = gr
                g_ref[rows, _lanes(n_pair + j)] = gi
                out += [gr, gi]
            return tuple(out)

        zero = jnp.zeros((SUBLANES, LANES), F32)
        firsts = lax.fori_loop(0, t_seg, step, (zero,) * (2 * n_pair), unroll=4)
        for j in range(n_pair):
            end_ref[:, _lanes(j)] = firsts[2 * j]
            end_ref[:, _lanes(n_pair + j)] = firsts[2 * j + 1]
        _s5_join_segments(st_ref, end_ref, car_ref, qw_ref, 0, list(range(SUBLANES))[::-1], n_pair)
        _s5_add_carries(g_ref, car_ref, qw_ref, t_seg, n_pair)

        def pair_up(k, c):
            rows, prev = _tile8(k), _tile8(k - 1)
            out = []
            for j in range(n_pair):
                re, im = _lanes(j), _lanes(n_pair + j)
                gr, gi, hr, hi = g_ref[rows, re], g_ref[rows, im], h_ref[prev, re], h_ref[prev, im]
                out += [c[2 * j] + gr * hr + gi * hi, c[2 * j + 1] + gi * hr - gr * hi]
            return tuple(out)

        acc = lax.fori_loop(1, t_seg, pair_up, (zero,) * (2 * n_pair), unroll=4)
        has_prev = (i < n_t - 1).astype(F32)
        first_seg = lax.broadcasted_iota(jnp.int32, (SUBLANES, LANES), 0) == 0
        last = _tile8(t_seg - 1)
        for j in range(n_pair):
            re, im = _lanes(j), _lanes(n_pair + j)
            gr, gi = g_ref[0:SUBLANES, re], g_ref[0:SUBLANES, im]
            hr = jnp.where(first_seg, hp_ref[SUBLANES - 1:, re] * has_prev, pltpu.roll(h_ref[last, re], 1, 0))
            hi = jnp.where(first_seg, hp_ref[SUBLANES - 1:, im] * has_prev, pltpu.roll(h_ref[last, im], 1, 0))
            dlam_ref[:, re] += jnp.sum(acc[2 * j] + gr * hr + gi * hi, axis=0, keepdims=True)
            dlam_ref[:, im] += jnp.sum(acc[2 * j + 1] + gi * hr - gr * hi, axis=0, keepdims=True)

        g_b = g_ref[...].astype(BF16)
        dui_ref[...] = lax.dot_general(g_b, bs_ref[...], NT, preferred_element_type=F32)
        dbs_ref[...] += lax.dot_general(ui_ref[...].astype(BF16), g_b, TN, preferred_element_type=F32)
        dcs_ref[...] += lax.dot_general(h_ref[...].astype(BF16), dyc_b, TN, preferred_element_type=F32)

        def scatter(k, _):
            rows = pl.ds(k, SUBLANES, stride=t_seg)
            du_ref[rows, :] = (dui_ref[_tile8(k), :] + dux_ref[rows, :]).astype(du_ref.dtype)
            return 0

        lax.fori_loop(0, t_seg, scatter, 0, unroll=4)

    rev = lambda i: n_t - 1 - i
    return pl.pallas_call(
        body, name="s5_scan_bwd", grid=(ns, n_t),
        in_specs=[_spec((t_blk, LANES), lambda s, i: (rev(i), u_col + s)),
                  _spec((None, t_blk, w2), lambda s, i: (s, rev(i), 0)),
                  _spec((None, SUBLANES, w2), lambda s, i: (s, jnp.maximum(rev(i) * t_seg - 1, 0), 0)),
                  _spec((t_blk, LANES), lambda s, i: (rev(i), s)),
                  _spec((t_blk, LANES), lambda s, i: (rev(i), s)),
                  _spec((None, LANES, w2), lambda s, i: (s, 0, 0)),
                  _spec((None, w2, LANES), lambda s, i: (s, 0, 0)),
                  _spec((None, 1, w2), lambda s, i: (s, 0, 0)),
                  _spec((None, t_seg, w2), lambda s, i: (s, 0, 0))],
        out_specs=[_spec((t_blk, LANES), lambda s, i: (rev(i), s)),
                   _spec((None, LANES, w2), lambda s, i: (s, 0, 0)),
                   _spec((None, w2, LANES), lambda s, i: (s, 0, 0)),
                   _spec((None, 1, w2), lambda s, i: (s, 0, 0))],
        out_shape=[jax.ShapeDtypeStruct((seq, ns * LANES), F32), jax.ShapeDtypeStruct(bs.shape, F32),
                   jax.ShapeDtypeStruct(cs.shape, F32), jax.ShapeDtypeStruct(lam.shape, F32)],
        scratch_shapes=[pltpu.VMEM((t_blk, w2), F32), pltpu.VMEM((t_blk, w2), F32), pltpu.VMEM((1, w2), F32),
                        pltpu.VMEM((SUBLANES, w2), F32), pltpu.VMEM((SUBLANES, w2), F32),
                        pltpu.VMEM((t_blk, LANES), F32), pltpu.VMEM((t_blk, LANES), F32), pltpu.VMEM((t_blk, LANES), F32)],
        compiler_params=pltpu.CompilerParams(dimension_semantics=("parallel", "arbitrary"),
                                             vmem_limit_bytes=_vmem_limit(5 * t_blk * w2 * 4)),
    )(*[_in_hbm(a) for a in (proj, states, states, d_yc, du_extra, bs, cs, lam, qw)])


def _loss_head(y, target, t_m):
    seq, d = y.shape

    def body(y_ref, t_ref, loss_ref, dy_ref):
        @pl.when(pl.program_id(0) == 0)
        def _():
            loss_ref[...] = jnp.zeros_like(loss_ref)

        diff = y_ref[...] - t_ref[...]
        dy_ref[...] = diff / d
        loss_ref[...] += 0.5 * jnp.sum(diff * diff) / d

    row = _spec((t_m, d), lambda i: (i, 0))
    return pl.pallas_call(
        body, name="loss_head", grid=(seq // t_m,), in_specs=[row, row],
        out_specs=[_spec((SUBLANES, LANES), lambda i: (0, 0)), row],
        out_shape=[jax.ShapeDtypeStruct((SUBLANES, LANES), F32), jax.ShapeDtypeStruct((seq, d), F32)],
        compiler_params=pltpu.CompilerParams(dimension_semantics=("arbitrary",),
                                             vmem_limit_bytes=_vmem_limit(6 * t_m * d * 4)),
    )(_in_hbm(y), _in_hbm(target))


def _adamw_fn(w, m, v, *partials):
    g = partials[0]
    for p in partials[1:]:
        g = g + p
    m2 = ADAM_B1 * m + (1.0 - ADAM_B1) * g
    v2 = ADAM_B2 * v + (1.0 - ADAM_B2) * (g * g)
    m_hat = m2 / (1.0 - ADAM_B1 ** ADAM_STEP)
    v_hat = v2 / (1.0 - ADAM_B2 ** ADAM_STEP)
    delta = -ADAM_LR * (m_hat / (jnp.sqrt(v_hat) + ADAM_EPS) + ADAM_WD * w)
    return g, delta, m2, v2


def _adamw(name, w, m, v, partials):
    rows, cols = w.shape
    t_r = rows
    for cand in (512, 256, 128, 64, 32, 16, 8):
        if rows % cand == 0 and cand * cols * 4 <= (1 << 20):
            t_r = cand
            break
    n_p = partials.shape[0]
    row = lambda i: (i, 0)
    ins = [(a, (t_r, cols), row) for a in (w, m, v)]
    ins += [(partials, (None, t_r, cols), (lambda i, j=j: (j, i, 0))) for j in range(n_p)]
    outs = [((rows, cols), F32, (t_r, cols), row)] * 4
    return _rowwise(name, _adamw_fn, ins, outs, (rows // t_r,))


SMALL_PARAMS = ("b_ada", "ssm_a_re", "ssm_a_im", "ssm_log_dt", "ssm_b_re", "ssm_b_im", "ssm_c_re", "ssm_c_im",
                "ssm_d", "b_glu", "ln1_g", "ln1_b", "ln2_g", "ln2_b")
WEIGHTS = ("w_ada", "b_ada", "w_in", "w_sb_up", "ssm_a_re", "ssm_a_im", "ssm_log_dt", "ssm_b_re", "ssm_b_im",
           "ssm_c_re", "ssm_c_im", "ssm_d", "w_glu", "b_glu", "w_ssm_up", "w_out", "ln1_g", "ln1_b", "w_ffn_in",
           "w_ffn_out", "ln2_g", "ln2_b")
ARG_NAMES = (("x", "c") + WEIGHTS + ("loss_target",) + tuple("m_" + n for n in WEIGHTS)
             + tuple("v_" + n for n in WEIGHTS))


def _pack(arrs):
    flat = jnp.concatenate([a.reshape(-1) for a in arrs])
    pad = (-flat.shape[0]) % (PACK_ROWS * LANES)
    return jnp.pad(flat, (0, pad)).reshape(-1, LANES)


def _unpack(packed, like):
    lead = packed.shape[:-2]
    flat = packed.reshape(lead + (-1,))
    out, off = [], 0
    for a in like:
        out.append(flat[..., off:off + a.size].reshape(lead + a.shape))
        off += a.size
    return out


def kernel(x, c, w_ada, b_ada, w_in, w_sb_up, ssm_a_re, ssm_a_im, ssm_log_dt, ssm_b_re, ssm_b_im, ssm_c_re,
           ssm_c_im, ssm_d, w_glu, b_glu, w_ssm_up, w_out, ln1_g, ln1_b, w_ffn_in, w_ffn_out, ln2_g, ln2_b,
           loss_target, m_w_ada, m_b_ada, m_w_in, m_w_sb_up, m_ssm_a_re, m_ssm_a_im, m_ssm_log_dt, m_ssm_b_re,
           m_ssm_b_im, m_ssm_c_re, m_ssm_c_im, m_ssm_d, m_w_glu, m_b_glu, m_w_ssm_up, m_w_out, m_ln1_g, m_ln1_b,
           m_w_ffn_in, m_w_ffn_out, m_ln2_g, m_ln2_b, v_w_ada, v_b_ada, v_w_in, v_w_sb_up, v_ssm_a_re, v_ssm_a_im,
           v_ssm_log_dt, v_ssm_b_re, v_ssm_b_im, v_ssm_c_re, v_ssm_c_im, v_ssm_d, v_w_glu, v_b_glu, v_w_ssm_up,
           v_w_out, v_ln1_g, v_ln1_b, v_w_ffn_in, v_w_ffn_out, v_ln2_g, v_ln2_b):
    given = locals()
    return _train_step({n: given[n] for n in ARG_NAMES})


def _train_step(p):
    x0 = p["x"][0]
    target = p["loss_target"][0]
    seq, d = x0.shape
    depth = p["w_ada"].shape[0]
    n_ada = p["w_ada"].shape[2]
    n_in = p["w_in"].shape[2]
    sb_w = p["w_sb_up"].shape[1]
    ssm_w = p["w_ssm_up"].shape[1]
    n_up = p["w_sb_up"].shape[2]
    n_ffn = p["w_ffn_in"].shape[2]
    ffn = N_DEV * p["w_ffn_out"].shape[1]
    in_cols = N_DEV * n_in
    alpha = (2 * depth) ** 0.25
    resid_ln, resid_ln_mod = _make_resid_fns(alpha)
    t_r = min(512, seq)
    n_r = seq // t_r
    t_m = min(1024, seq)
    n_m = seq // t_m
    t_d = _tile(d)
    assert n_ffn * (N_DEV // 2) == ffn and sb_w % LANES == 0 and ssm_w % LANES == 0 and d % LANES == 0
    assert n_in % LANES == 0 and n_up % LANES == 0 and seq % t_m == 0 and in_cols == 3 * sb_w + ssm_w + 2 * d
    assert (3 * sb_w) % ssm_w == 0 and (3 * sb_w + ssm_w) % (2 * d) == 0
    assert sb_w % n_in == 0 and ssm_w % n_in == 0 and d % n_in == 0 and seq % (SB_BLOCK * SB_GROUP) == 0
    proj_starts = [c // n_in for c in (0, sb_w, 2 * sb_w, 3 * sb_w, 3 * sb_w + ssm_w)]

    bf = lambda a: a.astype(BF16)
    got = _exchange("gather_first", [], [bf(p["w_in"][0]), p["c"]])
    wg_in = [got[0]] + [None] * (depth - 1)
    c_all = got[1].reshape(N_DEV, d)
    small_names = ("w_sb_up", "w_ssm_up", "w_glu", "w_out")
    wg_ffn_in, wg_ffn_out, wg = [None] * depth, [None] * depth, {}

    c_pad = jnp.pad(c_all, ((0, 2 * SUBLANES - N_DEV), (0, 0)))
    c_act = _rowwise("silu_c", lambda v: v * jax.nn.sigmoid(v), [(c_pad, c_pad.shape, lambda i: (0, 0))],
                     [(c_pad.shape, F32, c_pad.shape, lambda i: (0, 0))], (1,))[0]
    rows_c = c_pad.shape[0]
    mod_cols = [
        _mm(f"mod_{l}", c_act, p["w_ada"],
            _spec((rows_c, d), lambda i, j, k: (0, 0)), _spec((None, d, n_ada), lambda i, j, k, l=l: (l, 0, 0)),
            _spec((rows_c, n_ada), lambda i, j, k: (0, 0)), (rows_c, n_ada), F32, (1, 1, 1), NN)
        for l in range(depth)]
    mod_send = jnp.stack([m[:N_DEV] for m in mod_cols], axis=1)
    mod_recv = _exchange("exchange_mod", [mod_send], [])[0]
    mod_nobias = jnp.swapaxes(mod_recv, 0, 1).reshape(depth, N_DEV * n_ada)
    full2 = lambda a: (a, a.shape, lambda i: (0, 0))
    mod = _rowwise("mod_bias", lambda a, b: a + b, [full2(mod_nobias), full2(p["b_ada"])],
                   [(mod_nobias.shape, F32, mod_nobias.shape, lambda i: (0, 0))], (1,))[0]
    vec = lambda a: a.reshape(1, -1)
    mods = [[vec(mod[l, j * d:(j + 1) * d]) for j in range(6)] for l in range(depth)]
    ln = {n: [vec(p[n][l]) for l in range(depth)] for n in ("ln1_g", "ln1_b", "ln2_g", "ln2_b")}

    row_spec = lambda width: ((t_r, width), lambda i: (i, 0))
    col_spec = lambda width, cb: ((t_r, width), lambda i, cb=cb: (i, cb))
    vec_spec = lambda width: ((1, width), lambda i: (0, 0))
    rows_in = lambda a: (a,) + row_spec(a.shape[1])
    vec_in = lambda a: (a,) + vec_spec(a.shape[1])
    row_out = lambda width, dt: ((seq, width), dt) + row_spec(width)

    s5 = [_s5_discretize(*[p[n][l] for n in ("ssm_a_re", "ssm_a_im", "ssm_log_dt", "ssm_b_re", "ssm_b_im",
                                               "ssm_c_re", "ssm_c_im")]) for l in range(depth)]
    s5_b16 = [(bs.astype(BF16), cs.astype(BF16), lam) for bs, cs, lam in s5]
    t_scan = min(1024, seq)
    s5_pw = [_s5_powers(p["ssm_a_re"][l], p["ssm_a_im"][l], p["ssm_log_dt"][l], t_scan // SUBLANES)
             for l in range(depth)]
    u_col = 3 * sb_w // LANES
    gates_cb = (3 * sb_w + ssm_w) // (2 * d)
    ssm_d = [vec(p["ssm_d"][l]) for l in range(depth)]
    b_glu = [vec(p["b_glu"][l]) for l in range(depth)]
    n_half = N_DEV // 2

    h = _rowwise("modulate_in", _modulate, [rows_in(x0), vec_in(mods[0][1]), vec_in(mods[0][0])],
                 [row_out(d, BF16)], (n_r,))[0]
    saved = []
    x_cur = x0
    for l in range(depth):
        sv = {"x_in": x_cur, "h": h}
        last = l == depth - 1
        t_n = _tile(n_in)
        r_n = n_in // t_n
        proj = _mm(f"proj_{l}", h, wg_in[l],
                   _spec((t_m, d), lambda i, j, k: (i, 0)),
                   _spec((None, d, t_n), lambda i, j, k, r=r_n: (j // r, 0, j % r)),
                   _spec((t_m, t_n), lambda i, j, k: (i, j)), (seq, in_cols), F32, (n_m, N_DEV * r_n, 1), NN,
                   reread=(True, True))
        arriving = [bf(p["w_ffn_in"][l])] + ([bf(p[n]) for n in small_names] if l == 0 else [])
        (o_sb, o_sb32), got = _sb_attention_fwd(proj, sb_w, beside=_Exchange(gather=arriving))
        wg_ffn_in[l] = got[0]
        if l == 0:
            wg = dict(zip(small_names, got[1:]))
            for n in ("w_glu", "w_out"):
                wg[n] = jnp.swapaxes(wg[n], 0, 1).reshape(depth, -1, wg[n].shape[-1])
            for n in ("w_sb_up", "w_ssm_up"):
                wg[n] = jnp.transpose(wg[n], (1, 2, 0, 3)).reshape(depth, wg[n].shape[2], d)
        bs16, cs16, lam = s5_b16[l]
        arriving = [bf(p["w_ffn_out"][l])] + ([] if last else [bf(p["w_in"][l + 1])])
        (yc, states), got = _s5_scan_fwd(proj, u_col, bs16, cs16, lam, s5_pw[l][0], t_scan,
                                         beside=_Exchange(gather=arriving))
        wg_ffn_out[l] = got[0].reshape(n_half, n_ffn, d)
        if not last:
            wg_in[l + 1] = got[1]
        s5_out = _s5_head(f"s5_head_{l}", yc, proj, 3 * sb_w // ssm_w, ssm_d[l], b_glu[l], wg["w_glu"], l, t_r)

        merged, y_sb, y_ssm = _up_merge(f"up_merge_{l}", o_sb, s5_out, proj, gates_cb, wg["w_sb_up"], wg["w_ssm_up"],
                                        l, t_r)
        y_mix = _mm(f"out_proj_{l}", merged, wg["w_out"],
                    _spec((t_m, d), lambda i, j, k: (i, 0)), _spec((None, d, t_d), lambda i, j, k, l=l: (l, 0, j)),
                    _spec((t_m, t_d), lambda i, j, k: (i, j)), (seq, d), F32, (n_m, d // t_d, 1), NN)
        vecs_a = [mods[l][2], ln["ln1_g"][l], ln["ln1_b"][l], mods[l][4], mods[l][3]]
        x_mid, h2 = _rowwise(f"resid_mix_{l}", resid_ln_mod, [rows_in(x_cur), rows_in(y_mix)] + [vec_in(v) for v in vecs_a],
                             [row_out(d, F32), row_out(d, BF16)], (n_r,))
        a_ffn, f_act = _ffn_in_swiglu(f"ffn_in_{l}", h2, wg_ffn_in[l], t_r)
        y_ffn = _mm(f"ffn_out_{l}", f_act, wg_ffn_out[l],
                    _spec((None, t_m, n_ffn), lambda i, j, k: (k, i, 0)),
                    _spec((None, n_ffn, t_d), lambda i, j, k: (k, 0, j)),
                    _spec((t_m, t_d), lambda i, j, k: (i, j)), (seq, d), F32, (n_m, d // t_d, n_half), NN)
        vecs_b = [mods[l][5], ln["ln2_g"][l], ln["ln2_b"][l]] + ([] if last else [mods[l + 1][1], mods[l + 1][0]])
        outs_b = [row_out(d, F32)] + ([] if last else [row_out(d, BF16)])
        res = _rowwise(f"resid_ffn_{l}", resid_ln if last else resid_ln_mod,
                       [rows_in(x_mid), rows_in(y_ffn)] + [vec_in(v) for v in vecs_b], outs_b, (n_r,))
        sv.update(proj=proj, o_sb=o_sb, o_sb32=o_sb32, yc=yc, states=states, s5_out=s5_out,
                  y_sb=y_sb, y_ssm=y_ssm, merged=merged, y_mix=y_mix, x_mid=x_mid, h2=h2, a_ffn=a_ffn, f_act=f_act,
                  y_ffn=y_ffn, vecs_a=vecs_a, vecs_b=vecs_b)
        saved.append(sv)
        x_cur = res[0]
        h = None if last else res[1]

    loss_part, d_x = _loss_head(x_cur, target, t_r)
    loss = lax.psum(loss_part[0, 0], MESH_AXES)

    d_h_next = None
    grads = {n: [None] * depth for n in WEIGHTS}
    d_mod = [[None] * 6 for _ in range(depth)]
    land = {}
    waiting = []
    row_wrt = lambda i, width, dt: (i, "row", (seq, width), dt) + row_spec(width)
    sum_wrt = lambda i, width: (i, "sum", (1, width), F32) + vec_spec(width)
    for l in reversed(range(depth)):
        sv = saved[l]
        last = l == depth - 1
        ins_b = [rows_in(sv["x_mid"]), rows_in(sv["y_ffn"])] + [vec_in(v) for v in sv["vecs_b"]]
        cts_b = [rows_in(d_x)] + ([] if last else [rows_in(d_h_next)])
        wrt_b = [row_wrt(0, d, F32), row_wrt(1, d, BF16)] + [sum_wrt(2 + j, d) for j in range(len(sv["vecs_b"]))]
        res = _rowwise_vjp(f"resid_ffn_bwd_{l}", resid_ln if last else resid_ln_mod, ins_b, cts_b, wrt_b, (n_r,))
        d_x_mid, d_y_ffn = res[0], res[1]
        d_mod[l][5], grads["ln2_g"][l], grads["ln2_b"][l] = res[2], res[3], res[4]
        if not last:
            d_mod[l + 1][1], d_mod[l + 1][0] = res[5], res[6]
        d_a = _ffn_out_dx_swiglu(f"ffn_out_dx_{l}", d_y_ffn, wg_ffn_out[l], sv["a_ffn"], t_r).reshape(N_DEV, seq, n_ffn)
        g_ffn_out = _mm(f"ffn_out_dw_{l}", sv["f_act"], d_y_ffn,
                        _spec((None, t_m, n_ffn), lambda i, j, k: (i, k, 0)), _spec((t_m, t_d), lambda i, j, k: (k, j)),
                        _spec((None, n_ffn, t_d), lambda i, j, k: (i, 0, j)), (n_half, n_ffn, d), GRAD_WIRE,
                        (n_half, d // t_d, n_m), TN, reread=(False, True))
        d_h2 = _mm(f"ffn_in_dx_{l}", d_a, wg_ffn_in[l],
                   _spec((None, t_m, n_ffn), lambda i, j, k: (k, i, 0)),
                   _spec((None, t_d, n_ffn), lambda i, j, k: (k, j, 0)),
                   _spec((t_m, t_d), lambda i, j, k: (i, j)), (seq, d), BRANCH_CT, (n_m, d // t_d, N_DEV), NT)
        g_ffn_in = _mm(f"ffn_in_dw_{l}", sv["h2"], d_a,
                       _spec((t_m, t_d), lambda i, j, k: (k, j)), _spec((None, t_m, n_ffn), lambda i, j, k: (i, k, 0)),
                       _spec((None, t_d, n_ffn), lambda i, j, k: (i, j, 0)), (N_DEV, d, n_ffn), GRAD_WIRE,
                       (N_DEV, d // t_d, n_m), TN, reread=(True, False))
        ins_a = [rows_in(sv["x_in"]), rows_in(sv["y_mix"])] + [vec_in(v) for v in sv["vecs_a"]]
        wrt_a = [row_wrt(0, d, F32), row_wrt(1, d, BF16)] + [sum_wrt(2 + j, d) for j in range(5)]
        res = _rowwise_vjp(f"resid_mix_bwd_{l}", resid_ln_mod, ins_a, [rows_in(d_x_mid), rows_in(d_h2)], wrt_a, (n_r,))
        d_x_in, d_y_mix = res[0], res[1]
        d_mod[l][2], grads["ln1_g"][l], grads["ln1_b"][l], d_mod[l][4], d_mod[l][3] = res[2:7]
        d_merged = _mm(f"out_proj_dx_{l}", d_y_mix, wg["w_out"],
                       _spec((t_m, d), lambda i, j, k: (i, 0)), _spec((None, t_d, d), lambda i, j, k, l=l: (l, j, 0)),
                       _spec((t_m, t_d), lambda i, j, k: (i, j)), (seq, d), BRANCH_CT, (n_m, d // t_d, 1), NT)
        g_out = _mm(f"out_proj_dw_{l}", sv["merged"], d_y_mix,
                    _spec((t_m, t_d), lambda i, j, k: (k, i)), _spec((t_m, t_d), lambda i, j, k: (k, j)),
                    _spec((t_d, t_d), lambda i, j, k: (i, j)), (d, d), GRAD_WIRE, (d // t_d, d // t_d, n_m), TN, reread=(d > t_d, d > t_d))
        gates = (sv["proj"],) + col_spec(2 * d, gates_cb)
        d_y_sb, d_y_ssm, d_gates = _rowwise_vjp(
            f"merge_bwd_{l}", _merge_fn, [rows_in(sv["y_sb"]), rows_in(sv["y_ssm"]), gates], [rows_in(d_merged)],
            [row_wrt(0, d, BF16), row_wrt(1, d, BF16), row_wrt(2, 2 * d, BF16)], (n_r,))

        def up_bwd(name, act, d_y, w, dx_dtype, l=l):
            k_w = act.shape[1]
            dx = _mm(name + "_dx", d_y, w, _spec((t_m, d), lambda i, j, k: (i, 0)),
                     _spec((None, k_w, d), lambda i, j, k: (l, 0, 0)),
                     _spec((t_m, k_w), lambda i, j, k: (i, 0)), (seq, k_w), dx_dtype, (n_m, 1, 1), NT)
            dw = _mm(name + "_dw", act, d_y, _spec((t_m, k_w), lambda i, j, k: (k, 0)),
                     _spec((t_m, t_d), lambda i, j, k: (k, j)),
                     _spec((k_w, t_d), lambda i, j, k: (0, j)), (k_w, d), GRAD_WIRE, (1, d // t_d, n_m), TN,
                     reread=(d > t_d, False))
            return dx, jnp.swapaxes(dw.reshape(k_w, N_DEV, n_up), 0, 1)

        d_o_sb, g_sb_up = up_bwd(f"sb_up_{l}", sv["o_sb"], d_y_sb, wg["w_sb_up"], BF16)
        d_s5_out, g_ssm_up = up_bwd(f"ssm_up_{l}", sv["s5_out"], d_y_ssm, wg["w_ssm_up"], BRANCH_CT)
        waiting += [("w_ffn_in", g_ffn_in), ("w_ffn_out", g_ffn_out.reshape(N_DEV, -1, d)),
                    ("w_out", g_out.reshape(N_DEV, -1, d)), ("w_sb_up", g_sb_up), ("w_ssm_up", g_ssm_up)]
        levels = [l + 1] * (len(waiting) - 5) + [l] * 5
        (d_q, d_k, d_v), got = _sb_attention_bwd(
            sv["proj"], sv["o_sb32"], d_o_sb, sb_w,
            beside=_Exchange(layered=[(g, lv, depth, land.get(n)) for (n, g), lv in zip(waiting, levels)]))
        land.update({n: buf for (n, _), buf in zip(waiting, got)})
        d_yc, d_u_skip, g_glu, grads["ssm_d"][l], grads["b_glu"][l] = _s5_head_bwd(
            f"s5_head_bwd_{l}", sv["yc"], sv["proj"], 3 * sb_w // ssm_w, d_s5_out, ssm_d[l], b_glu[l], wg["w_glu"], l, t_r)
        bs16, cs16, lam = s5_b16[l]
        d_u, d_bs, d_cs, d_lam = _s5_scan_bwd(sv["proj"], u_col, sv["states"], d_yc, d_u_skip, bs16, cs16, lam,
                                              s5_pw[l][1], t_scan)
        raw = [p[n][l] for n in ("ssm_a_re", "ssm_a_im", "ssm_log_dt", "ssm_b_re", "ssm_b_im", "ssm_c_re", "ssm_c_im")]
        _, pull = jax.vjp(_s5_discretize, *raw)
        (grads["ssm_a_re"][l], grads["ssm_a_im"][l], grads["ssm_log_dt"][l], grads["ssm_b_re"][l],
         grads["ssm_b_im"][l], grads["ssm_c_re"][l], grads["ssm_c_im"][l]) = pull((d_bs, d_cs, d_lam))
        d_proj = [d_q, d_k, d_v, d_u, d_gates]
        g_in = _mm_pieces(f"proj_dw_{l}", d_proj, proj_starts, n_in, lambda i, j, k: i, sv["h"],
                          _spec((t_m, t_d), lambda i, j, k: (k, j)), False, lambda i, j, k: k,
                          _spec((None, t_d, n_in), lambda i, j, k: (i, j, 0)), (N_DEV, d, n_in), GRAD_WIRE,
                          (N_DEV, d // t_d, n_m), TN)
        waiting = [("w_in", g_in), ("w_glu", g_glu.reshape(N_DEV, -1, ssm_w))]
        closing = _Exchange(layered=[(g, 0, depth, land.get(n)) for n, g in waiting]) if l == 0 else None
        d_h = _mm_pieces(f"proj_dx_{l}", d_proj, proj_starts, n_in, lambda i, j, k: k, wg_in[l],
                         _spec((None, t_d, n_in), lambda i, j, k: (k, j, 0)), True, lambda i, j, k: i,
                         _spec((t_m, t_d), lambda i, j, k: (i, j)), (seq, d), BRANCH_CT, (n_m, d // t_d, N_DEV), NT,
                         beside=closing)
        if l == 0:
            d_h, got = d_h
            land.update({n: buf for (n, _), buf in zip(waiting, got)})
        d_x, d_h_next = d_x_in, d_h
    res = _rowwise_vjp("modulate_in_bwd", lambda v, sc, sh: (v, _modulate(v, sc, sh)),
                       [rows_in(x0), vec_in(mods[0][1]), vec_in(mods[0][0])], [rows_in(d_x), rows_in(d_h_next)],
                       [row_wrt(0, d, F32), sum_wrt(1, d), sum_wrt(2, d)], (n_r,))
    grad_x, d_mod[0][1], d_mod[0][0] = res

    d_mod_rows = jnp.concatenate([jnp.concatenate(d_mod[l], axis=1) for l in range(depth)], axis=0)
    grads["b_ada"] = [d_mod_rows[l] for l in range(depth)]
    small_local = [jnp.stack([g.reshape(p[n].shape[1:]) for g in grads[n]]) for n in SMALL_PARAMS]
    d_mod_send = jnp.swapaxes(d_mod_rows.reshape(depth, N_DEV, n_ada), 0, 1)
    small_sum, (d_mod_cols,) = _reduce_packed("exchange_last", _pack(small_local), [d_mod_send])
    d_mod_pad = jnp.pad(jnp.swapaxes(d_mod_cols, 0, 1), ((0, 0), (0, rows_c - N_DEV), (0, 0)))
    g_ada = [
        _mm(f"mod_dw_{l}", c_act, d_mod_pad,
            _spec((rows_c, d), lambda i, j, k: (0, 0)), _spec((None, rows_c, n_ada), lambda i, j, k, l=l: (l, 0, 0)),
            _spec((d, n_ada), lambda i, j, k: (0, 0)), (d, n_ada), F32, (1, 1, 1), TN)
        for l in range(depth)]

    out = {}

    def update(name, partials):
        shape = p[name].shape
        two_d = lambda a: a.reshape(-1, shape[-1])
        res = _adamw("adamw_" + name, two_d(p[name]), two_d(p["m_" + name]), two_d(p["v_" + name]),
                     partials.reshape(partials.shape[0], -1, shape[-1]))
        out[name] = [r.reshape(shape) for r in res]

    update("w_ada", jnp.stack(g_ada)[None])
    for n in ("w_in", "w_sb_up", "w_ssm_up", "w_ffn_in", "w_glu", "w_out", "w_ffn_out"):
        update(n, land[n])
    for n, g in zip(SMALL_PARAMS, _unpack(small_sum, [p[n] for n in SMALL_PARAMS])):
        update(n, g[None])

    return ((loss, grad_x[None]) + tuple(out[n][0] for n in WEIGHTS) + tuple(out[n][1] for n in WEIGHTS)
            + tuple(out[n][2] for n in WEIGHTS) + tuple(out[n][3] for n in WEIGHTS))
```

```python
import jax
import jax.numpy as jnp
from jax import lax
from jax.experimental import pallas as pl
from jax.experimental.pallas import tpu as pltpu

F32 = jnp.float32
BF16 = jnp.bfloat16
GRAD_WIRE = BF16
FFN_ACT = BF16
BRANCH_CT = BF16

N_DEV = 8
LANES = 128
SUBLANES = 8
VMEM_BYTES = 64 * 1024 * 1024
HEAD_DIM = 64
SB_BLOCK = 256
SB_GROUP = 2
SLAB_GROUPS = 8
LN_EPS = 1e-5
ADAM_LR, ADAM_B1, ADAM_B2, ADAM_EPS, ADAM_WD, ADAM_STEP = 0.001, 0.9, 0.999, 1e-08, 0.01, 10
SB_UNDERFLOW = -120.0

PACK_ROWS = 256
MESH_AXES = ("x", "y", "c")


def _vmem_limit(block_bytes):
    return int(min(max(3 * block_bytes + (8 << 20), 24 << 20), VMEM_BYTES - (8 << 20)))


def _nbytes(shape, dtype):
    n = 1
    for d in shape:
        if d is not None:
            n *= d
    return n * jnp.dtype(dtype).itemsize


def _spec(shape, fn):
    return pl.BlockSpec(shape, fn)


class _Exchange:
    def __init__(self, scatter=(), gather=(), layered=()):
        self.arrs = list(scatter) + [a for a, _, _, _ in layered] + list(gather)
        self.n = len(self.arrs)
        self.n_sc = len(scatter) + len(layered)
        self.layer = [None] * len(scatter) + [l for _, l, _, _ in layered] + [None] * len(gather)
        self.shapes = ([a.shape for a in scatter] + [(N_DEV, dp) + a.shape[1:] for a, _, dp, _ in layered]
                       + [(N_DEV,) + a.shape for a in gather])
        self.held = [(len(scatter) + i, b) for i, (_, _, _, b) in enumerate(layered) if b is not None]
        self.operands = self.arrs + [b for _, b in self.held]
        hbm = pl.BlockSpec(memory_space=pltpu.HBM)
        self.in_specs = [hbm] * len(self.operands)
        self.out_specs = [hbm] * self.n
        self.out_shape = [jax.ShapeDtypeStruct(s, a.dtype) for s, a in zip(self.shapes, self.arrs)]
        self.scratch = [pltpu.SemaphoreType.DMA((self.n, N_DEV - 1)), pltpu.SemaphoreType.DMA((self.n, N_DEV - 1)),
                        pltpu.SemaphoreType.DMA((self.n,))]

    def aliases(self, first_in, first_out):
        return {first_in + self.n + i: first_out + a for i, (a, _) in enumerate(self.held)}

    def copies(self, ins, outs, sems):
        send_sems, recv_sems, own_sems = sems
        x, y, c = lax.axis_index("x"), lax.axis_index("y"), lax.axis_index("c")
        me = 4 * x + 2 * y + c
        landing = [outs[a].at[me] if self.layer[a] is None else outs[a].at[me, self.layer[a]] for a in range(self.n)]
        out = [pltpu.make_async_copy(ins[a].at[me] if a < self.n_sc else ins[a], landing[a], own_sems.at[a])
               for a in range(self.n)]
        for k in range(1, N_DEV):
            px = 1 - x if k & 4 else x
            py = 1 - y if k & 2 else y
            pc = 1 - c if k & 1 else c
            peer = 4 * px + 2 * py + pc
            for a in range(self.n):
                out.append(pltpu.make_async_remote_copy(
                    src_ref=ins[a].at[peer] if a < self.n_sc else ins[a], dst_ref=landing[a],
                    send_sem=send_sems.at[a, k - 1], recv_sem=recv_sems.at[a, k - 1],
                    device_id=(px, py, pc), device_id_type=pl.DeviceIdType.MESH))
        return out


def _exchange(name, scatter, gather, layered=()):
    ex = _Exchange(scatter, gather, layered)

    def body(*refs):
        copies = ex.copies(refs[:ex.n], refs[len(ex.operands):len(ex.operands) + ex.n], refs[-3:])
        for cp in copies:
            cp.start()
        for cp in copies:
            cp.wait()

    return pl.pallas_call(body, name=name, in_specs=ex.in_specs, out_specs=ex.out_specs, out_shape=ex.out_shape,
                          input_output_aliases=ex.aliases(0, 0), scratch_shapes=ex.scratch)(*ex.operands)


def _reduce_packed(name, packed, scatter):
    rows = packed.shape[0]
    blk = rows // N_DEV
    ex = _Exchange(scatter=[packed.reshape(N_DEV, blk, LANES)] + list(scatter))
    n_in = len(ex.operands)

    def body(*refs):
        ins, outs = refs[:ex.n], refs[n_in:n_in + ex.n]
        total_ref = refs[n_in + ex.n]
        sems, (send2, recv2, own2, load_sem) = refs[n_in + ex.n + 1:n_in + ex.n + 4], refs[n_in + ex.n + 4:-2]
        land_v, sum_v = refs[-2:]
        copies = ex.copies(ins, outs, sems)
        for cp in copies:
            cp.start()
        for cp in copies:
            cp.wait()
        load = pltpu.make_async_copy(outs[0], land_v, load_sem)
        load.start()
        load.wait()
        acc = land_v[0]
        for i in range(1, N_DEV):
            acc = acc + land_v[i]
        sum_v[...] = acc
        x, y, c = lax.axis_index("x"), lax.axis_index("y"), lax.axis_index("c")
        me = 4 * x + 2 * y + c
        back = [pltpu.make_async_copy(sum_v, total_ref.at[me], own2)]
        for k in range(1, N_DEV):
            peer = (1 - x if k & 4 else x, 1 - y if k & 2 else y, 1 - c if k & 1 else c)
            back.append(pltpu.make_async_remote_copy(
                src_ref=sum_v, dst_ref=total_ref.at[me], send_sem=send2.at[k - 1], recv_sem=recv2.at[k - 1],
                device_id=peer, device_id_type=pl.DeviceIdType.MESH))
        for cp in back:
            cp.start()
        for cp in back:
            cp.wait()

    hbm = pl.BlockSpec(memory_space=pltpu.HBM)
    res = pl.pallas_call(
        body, name=name, in_specs=ex.in_specs, out_specs=ex.out_specs + [hbm],
        out_shape=ex.out_shape + [jax.ShapeDtypeStruct((N_DEV, blk, LANES), F32)],
        scratch_shapes=ex.scratch + [pltpu.SemaphoreType.DMA((N_DEV - 1,)), pltpu.SemaphoreType.DMA((N_DEV - 1,)),
                                     pltpu.SemaphoreType.DMA, pltpu.SemaphoreType.DMA,
                                     pltpu.VMEM((N_DEV, blk, LANES), F32), pltpu.VMEM((blk, LANES), F32)],
    )(*ex.operands)
    return res[-1].reshape(rows, LANES), res[1:-1]


def _call_beside(ex, body, name, grid, in_specs, out_specs, out_shape, scratch_shapes, vmem_bytes, operands,
                 semantics, in_hbm=True):
    if in_hbm:
        operands = [_in_hbm(a) for a in operands]
    if ex is None:
        res = pl.pallas_call(
            body, name=name, grid=grid, in_specs=in_specs, out_specs=out_specs, out_shape=out_shape,
            scratch_shapes=scratch_shapes,
            compiler_params=pltpu.CompilerParams(dimension_semantics=semantics, vmem_limit_bytes=vmem_bytes),
        )(*operands)
        return res, None
    n_in, n_out, n_scr = len(in_specs), len(out_specs), len(scratch_shapes)
    n_xin = len(ex.operands)

    def fused(*refs):
        mine = refs[:n_in] + refs[n_in + n_xin:n_in + n_xin + n_out]
        mine += refs[n_in + n_xin + n_out + ex.n:n_in + n_xin + n_out + ex.n + n_scr]
        first = pl.program_id(0) == 0
        last = pl.program_id(0) == grid[0] - 1
        for dim in range(1, len(grid)):
            first = jnp.logical_and(first, pl.program_id(dim) == 0)
            last = jnp.logical_and(last, pl.program_id(dim) == grid[dim] - 1)
        x_ins = refs[n_in:n_in + ex.n]
        x_outs = refs[n_in + n_xin + n_out:n_in + n_xin + n_out + ex.n]

        @pl.when(first)
        def _():
            for cp in ex.copies(x_ins, x_outs, refs[-3:]):
                cp.start()

        body(*mine)

        @pl.when(last)
        def _():
            for cp in ex.copies(x_ins, x_outs, refs[-3:]):
                cp.wait()

    res = pl.pallas_call(
        fused, name=name, grid=grid, in_specs=list(in_specs) + ex.in_specs, out_specs=list(out_specs) + ex.out_specs,
        out_shape=list(out_shape) + ex.out_shape, input_output_aliases=ex.aliases(n_in, n_out),
        scratch_shapes=list(scratch_shapes) + ex.scratch,
        compiler_params=pltpu.CompilerParams(dimension_semantics=("arbitrary",) * len(grid),
                                             vmem_limit_bytes=vmem_bytes),
    )(*operands, *ex.operands)
    return res[:n_out], res[n_out:]


NN = (((1,), (0,)), ((), ()))
NT = (((1,), (1,)), ((), ()))
TN = (((0,), (0,)), ((), ()))


def _in_hbm(a):
    return pltpu.with_memory_space_constraint(a, pltpu.HBM)


def _mm(name, a, b, a_spec, b_spec, o_spec, o_shape, o_dtype, grid, dims, beside=None, reread=(False, True)):
    nk = grid[2]
    a, b = (x if again else _in_hbm(x) for x, again in zip((a, b), reread))
    acc_shape = tuple(d for d in o_spec.block_shape if d is not None)

    def product(a_ref, b_ref):
        return lax.dot_general(a_ref[...].astype(BF16), b_ref[...].astype(BF16), dims, preferred_element_type=F32)

    def body_once(a_ref, b_ref, o_ref):
        o_ref[...] = product(a_ref, b_ref).astype(o_ref.dtype)

    def body(a_ref, b_ref, o_ref, acc_ref):
        k = pl.program_id(2)

        @pl.when(k == 0)
        def _():
            acc_ref[...] = product(a_ref, b_ref)

        @pl.when(k > 0)
        def _():
            acc_ref[...] += product(a_ref, b_ref)

        @pl.when(k == nk - 1)
        def _():
            o_ref[...] = acc_ref[...].astype(o_ref.dtype)

    blk = (_nbytes(a_spec.block_shape, a.dtype) + _nbytes(b_spec.block_shape, b.dtype)
           + _nbytes(acc_shape, o_dtype) + _nbytes(acc_shape, F32))
    res, got = _call_beside(
        beside, body_once if nk == 1 else body, name, grid, [a_spec, b_spec], [o_spec],
        [jax.ShapeDtypeStruct(o_shape, o_dtype)], [] if nk == 1 else [pltpu.VMEM(acc_shape, F32)],
        _vmem_limit(blk), (a, b), ("parallel", "parallel", "arbitrary"), in_hbm=False)
    return res[0] if beside is None else (res[0], got)


def _mm_pieces(name, pieces, starts, width, step_block, other, other_spec, pieces_first, piece_rows, o_spec, o_shape,
               o_dtype, grid, dims, beside=None):
    n_p, nk = len(pieces), grid[2]
    acc_shape = tuple(s for s in o_spec.block_shape if s is not None)

    def which(i, j, k):
        blk = step_block(i, j, k)
        idx = 0
        for s in starts[1:]:
            idx = idx + (blk >= s).astype(jnp.int32)
        return idx, blk

    def piece_spec(p, t_rows):
        def index(i, j, k):
            idx, blk = which(i, j, k)
            mine = idx == p
            return jnp.where(mine, piece_rows(i, j, k), 0), jnp.where(mine, blk - starts[p], 0)
        return _spec((t_rows, width), index)

    def body(*refs):
        p_refs = refs[:n_p] if pieces_first else refs[1:1 + n_p]
        other_ref = refs[n_p] if pieces_first else refs[0]
        o_ref, acc_ref = refs[n_p + 1], refs[n_p + 2]
        i, j, k = pl.program_id(0), pl.program_id(1), pl.program_id(2)

        @pl.when(k == 0)
        def _():
            acc_ref[...] = jnp.zeros_like(acc_ref)

        idx, _ = which(i, j, k)
        for p in range(n_p):
            @pl.when(idx == p)
            def _(p=p):
                mine, fixed = p_refs[p][...].astype(BF16), other_ref[...].astype(BF16)
                pair = (mine, fixed) if pieces_first else (fixed, mine)
                acc_ref[...] += lax.dot_general(pair[0], pair[1], dims, preferred_element_type=F32)

        @pl.when(k == nk - 1)
        def _():
            o_ref[...] = acc_ref[...].astype(o_ref.dtype)

    t_rows = other_spec.block_shape[-2] if not pieces_first else o_spec.block_shape[-2]
    specs = [piece_spec(p, t_rows) for p in range(n_p)]
    in_specs = specs + [other_spec] if pieces_first else [other_spec] + specs
    operands = list(pieces) + [other] if pieces_first else [other] + list(pieces)
    blk = (n_p * 4 * t_rows * width + _nbytes(other_spec.block_shape, other.dtype)
           + _nbytes(acc_shape, o_dtype) + _nbytes(acc_shape, F32))
    res, got = _call_beside(
        beside, body, name, grid, in_specs, [o_spec], [jax.ShapeDtypeStruct(o_shape, o_dtype)],
        [pltpu.VMEM(acc_shape, F32)], _vmem_limit(blk), operands, ("parallel", "parallel", "arbitrary"), in_hbm=False)
    return res[0] if beside is None else (res[0], got)


def _swiglu_fn(gate_up):
    gate, up = gate_up[0], gate_up[1]
    return gate * jax.nn.sigmoid(gate) * up


def _ffn_in_swiglu(name, h, w, t_m):
    seq, d = h.shape
    n_half, n = w.shape[0] // 2, w.shape[2]

    def body(h_ref, wg_ref, wu_ref, a_ref, f_ref):
        hb = h_ref[...]
        a_ref[0] = lax.dot_general(hb, wg_ref[...], NN, preferred_element_type=F32).astype(a_ref.dtype)
        a_ref[1] = lax.dot_general(hb, wu_ref[...], NN, preferred_element_type=F32).astype(a_ref.dtype)
        f_ref[...] = _swiglu_fn(a_ref[...].astype(F32)).astype(f_ref.dtype)

    blk = 2 * t_m * d + 4 * d * n + 6 * t_m * n + 12 * t_m * n
    return pl.pallas_call(
        body, name=name, grid=(seq // t_m, n_half),
        in_specs=[_spec((t_m, d), lambda i, j: (i, 0)), _spec((None, d, n), lambda i, j: (j, 0, 0)),
                  _spec((None, d, n), lambda i, j: (j + n_half, 0, 0))],
        out_specs=[_spec((2, None, t_m, n), lambda i, j: (0, j, i, 0)), _spec((None, t_m, n), lambda i, j: (j, i, 0))],
        out_shape=[jax.ShapeDtypeStruct((2, n_half, seq, n), FFN_ACT), jax.ShapeDtypeStruct((n_half, seq, n), BF16)],
        compiler_params=pltpu.CompilerParams(dimension_semantics=("parallel", "parallel"),
                                             vmem_limit_bytes=_vmem_limit(blk)),
    )(h, w, w)


def _ffn_out_dx_swiglu(name, d_y, w, a, t_m):
    seq, d = d_y.shape
    n_half, n = w.shape[0], w.shape[1]

    def body(dy_ref, w_ref, a_ref, da_ref):
        d_f = lax.dot_general(dy_ref[...], w_ref[...], NT, preferred_element_type=F32)
        gate, up = a_ref[0].astype(F32), a_ref[1].astype(F32)
        s = jax.nn.sigmoid(gate)
        gs = gate * s
        da_ref[0] = (d_f * up * (s + gs * (1.0 - s))).astype(da_ref.dtype)
        da_ref[1] = (d_f * gs).astype(da_ref.dtype)

    blk = 2 * t_m * d + 2 * d * n + 8 * t_m * n + 24 * t_m * n
    return pl.pallas_call(
        body, name=name, grid=(seq // t_m, n_half),
        in_specs=[_spec((t_m, d), lambda i, j: (i, 0)), _spec((None, n, d), lambda i, j: (j, 0, 0)),
                  _spec((2, None, t_m, n), lambda i, j: (0, j, i, 0))],
        out_specs=_spec((2, None, t_m, n), lambda i, j: (0, j, i, 0)),
        out_shape=jax.ShapeDtypeStruct((2, n_half, seq, n), BF16),
        compiler_params=pltpu.CompilerParams(dimension_semantics=("parallel", "parallel"),
                                             vmem_limit_bytes=_vmem_limit(blk)),
    )(d_y, w, a)


def _merge_fn(y_sb, y_ssm, gates):
    half = gates.shape[-1] // 2
    return jax.nn.sigmoid(gates[:, :half]) * y_sb + jax.nn.sigmoid(gates[:, half:]) * y_ssm


def _up_merge(name, o_sb, s5_out, proj, gates_cb, w_sb, w_ssm, layer, t_rows):
    seq = o_sb.shape[0]
    d = w_sb.shape[2]

    def body(o_ref, s_ref, g_ref, w1_ref, w2_ref, m_ref, y1_ref, y2_ref):
        y_sb = lax.dot_general(o_ref[...], w1_ref[...], NN, preferred_element_type=F32)
        y_ssm = lax.dot_general(s_ref[...], w2_ref[...], NN, preferred_element_type=F32)
        m_ref[...] = _merge_fn(y_sb, y_ssm, g_ref[...]).astype(m_ref.dtype)
        y1_ref[...] = y_sb.astype(y1_ref.dtype)
        y2_ref[...] = y_ssm.astype(y2_ref.dtype)

    row = lambda width: _spec((t_rows, width), lambda i: (i, 0))
    whole = lambda w: _spec((None,) + w.shape[1:], lambda i: (layer, 0, 0))
    blk = t_rows * (2 * o_sb.shape[1] + 2 * s5_out.shape[1] + 8 * d + 6 * d + 24 * d) + 4 * d * (o_sb.shape[1] + s5_out.shape[1])
    return pl.pallas_call(
        body, name=name, grid=(seq // t_rows,),
        in_specs=[row(o_sb.shape[1]), row(s5_out.shape[1]), _spec((t_rows, 2 * d), lambda i: (i, gates_cb)),
                  whole(w_sb), whole(w_ssm)],
        out_specs=[row(d)] * 3, out_shape=[jax.ShapeDtypeStruct((seq, d), BF16)] * 3,
        compiler_params=pltpu.CompilerParams(dimension_semantics=("parallel",), vmem_limit_bytes=_vmem_limit(blk)),
    )(_in_hbm(o_sb), _in_hbm(s5_out), _in_hbm(proj), w_sb, w_ssm)


def _tile(n, pref=1024):
    t = pref
    while t >= LANES:
        if n % t == 0:
            return t
        t -= LANES
    return n


def _rowwise(name, fn, ins, outs, grid):
    n_in = len(ins)

    def body(*refs):
        vals = fn(*[r[...].astype(F32) for r in refs[:n_in]])
        if not isinstance(vals, (tuple, list)):
            vals = (vals,)
        for r, v in zip(refs[n_in:], vals):
            r[...] = v.astype(r.dtype)

    blk = sum(_nbytes(bs, a.dtype) for a, bs, _ in ins) + sum(_nbytes(bs, d) + _nbytes(bs, F32) for _, d, bs, _ in outs)
    return pl.pallas_call(
        body, name=name, grid=grid,
        in_specs=[_spec(bs, im) for _, bs, im in ins],
        out_specs=[_spec(bs, im) for _, _, bs, im in outs],
        out_shape=[jax.ShapeDtypeStruct(s, d) for s, d, _, _ in outs],
        compiler_params=pltpu.CompilerParams(dimension_semantics=("parallel",) * len(grid),
                                             vmem_limit_bytes=_vmem_limit(2 * blk)),
    )(*[_in_hbm(a) for a, _, _ in ins])


def _rowwise_vjp(name, fn, ins, cts, wrt, grid):
    n_in, n_ct = len(ins), len(cts)
    idx = [w[0] for w in wrt]

    def body(*refs):
        prim = [r[...].astype(F32) for r in refs[:n_in]]
        ct = tuple(r[...].astype(F32) for r in refs[n_in:n_in + n_ct])
        o_refs = refs[n_in + n_ct:]

        def g(*sel):
            full = list(prim)
            for i, s in zip(idx, sel):
                full[i] = s
            out = fn(*full)
            return tuple(out) if isinstance(out, (tuple, list)) else (out,)

        _, pull = jax.vjp(g, *[prim[i] for i in idx])
        grads = pull(ct)
        first = pl.program_id(0) == 0
        for d in range(1, len(grid)):
            first = jnp.logical_and(first, pl.program_id(d) == 0)
        for w, o_ref, gr in zip(wrt, o_refs, grads):
            if w[1] == "row":
                o_ref[...] = gr.astype(o_ref.dtype)
            else:
                @pl.when(first)
                def _(o_ref=o_ref):
                    o_ref[...] = jnp.zeros_like(o_ref)

                o_ref[...] += gr.astype(o_ref.dtype)

    blk = (sum(_nbytes(bs, a.dtype) + _nbytes(bs, F32) for a, bs, _ in list(ins) + list(cts))
           + sum(_nbytes(w[4], w[3]) + _nbytes(w[4], F32) for w in wrt))
    return pl.pallas_call(
        body, name=name, grid=grid,
        in_specs=[_spec(bs, im) for _, bs, im in list(ins) + list(cts)],
        out_specs=[_spec(w[4], w[5]) for w in wrt],
        out_shape=[jax.ShapeDtypeStruct(w[2], w[3]) for w in wrt],
        compiler_params=pltpu.CompilerParams(dimension_semantics=("arbitrary",) * len(grid),
                                             vmem_limit_bytes=_vmem_limit(2 * blk)),
    )(*[_in_hbm(a) for a, _, _ in list(ins) + list(cts)])


def _normalize(x):
    mu = jnp.mean(x, axis=-1, keepdims=True)
    xc = x - mu
    var = jnp.mean(xc * xc, axis=-1, keepdims=True)
    return xc * lax.rsqrt(var + LN_EPS)


def _modulate(x, sc, sh):
    return _normalize(x) * (1.0 + sc) + sh


def _make_resid_fns(alpha):
    def resid_ln(x, y, gate, g, b):
        return _normalize(alpha * x + (1.0 + gate) * y) * g + b

    def resid_ln_mod(x, y, gate, g, b, sc, sh):
        xn = resid_ln(x, y, gate, g, b)
        return xn, _modulate(xn, sc, sh)

    return resid_ln, resid_ln_mod


def _s5_act_fn(yc, u, d_skip):
    return jax.nn.gelu(yc + d_skip * u)


def _s5_gate_fn(y1, t):
    return y1 * jax.nn.sigmoid(t)


def _s5_head_specs(yc, proj, u_cb, w_glu, layer, t_rows):
    width = yc.shape[1]
    row = _spec((t_rows, width), lambda i: (i, 0))
    u_spec = _spec((t_rows, width), lambda i: (i, u_cb))
    vec = _spec((1, width), lambda i: (0, 0))
    w_spec = _spec((None,) + w_glu.shape[1:], lambda i: (layer, 0, 0))
    return row, u_spec, vec, w_spec


def _s5_head(name, yc, proj, u_cb, d_skip, b_glu, w_glu, layer, t_rows):
    seq, width = yc.shape
    row, u_spec, vec, w_spec = _s5_head_specs(yc, proj, u_cb, w_glu, layer, t_rows)

    def body(yc_ref, u_ref, d_ref, b_ref, w_ref, o_ref):
        y1 = _s5_act_fn(yc_ref[...], u_ref[...], d_ref[...])
        t = lax.dot_general(y1.astype(BF16), w_ref[...], NN, preferred_element_type=F32) + b_ref[...]
        o_ref[...] = _s5_gate_fn(y1, t).astype(o_ref.dtype)

    return pl.pallas_call(
        body, name=name, grid=(seq // t_rows,), in_specs=[row, u_spec, vec, vec, w_spec], out_specs=row,
        out_shape=jax.ShapeDtypeStruct((seq, width), BF16),
        compiler_params=pltpu.CompilerParams(dimension_semantics=("parallel",),
                                             vmem_limit_bytes=_vmem_limit(40 * t_rows * width)),
    )(_in_hbm(yc), _in_hbm(proj), d_skip, b_glu, w_glu)


def _s5_head_bwd(name, yc, proj, u_cb, d_out, d_skip, b_glu, w_glu, layer, t_rows):
    seq, width = yc.shape
    row, u_spec, vec, w_spec = _s5_head_specs(yc, proj, u_cb, w_glu, layer, t_rows)
    n_t = seq // t_rows

    def body(yc_ref, u_ref, do_ref, d_ref, b_ref, w_ref, dyc_ref, du_ref, dw_ref, dd_ref, db_ref, acc_ref):
        i = pl.program_id(0)

        @pl.when(i == 0)
        def _():
            acc_ref[...] = jnp.zeros_like(acc_ref)
            dd_ref[...] = jnp.zeros_like(dd_ref)
            db_ref[...] = jnp.zeros_like(db_ref)

        y1, pull_act = jax.vjp(_s5_act_fn, yc_ref[...], u_ref[...], d_ref[...])
        y1_b = y1.astype(BF16)
        t = lax.dot_general(y1_b, w_ref[...], NN, preferred_element_type=F32) + b_ref[...]
        _, pull_gate = jax.vjp(_s5_gate_fn, y1, t)
        d_y1, d_t = pull_gate(do_ref[...].astype(F32))
        d_t_b = d_t.astype(BF16)
        d_y1 = d_y1 + lax.dot_general(d_t_b, w_ref[...], NT, preferred_element_type=F32)
        acc_ref[...] += lax.dot_general(y1_b, d_t_b, TN, preferred_element_type=F32)
        db_ref[...] += jnp.sum(d_t, axis=0, keepdims=True)
        d_yc, d_u, d_d = pull_act(d_y1)
        dyc_ref[...] = d_yc
        du_ref[...] = d_u
        dd_ref[...] += d_d

        @pl.when(i == n_t - 1)
        def _():
            dw_ref[...] = acc_ref[...].astype(dw_ref.dtype)

    whole = _spec((width, width), lambda i: (0, 0))
    return pl.pallas_call(
        body, name=name, grid=(n_t,), in_specs=[row, u_spec, row, vec, vec, w_spec],
        out_specs=[row, row, whole, vec, vec],
        out_shape=[jax.ShapeDtypeStruct((seq, width), F32), jax.ShapeDtypeStruct((seq, width), F32),
                   jax.ShapeDtypeStruct((width, width), GRAD_WIRE), jax.ShapeDtypeStruct((1, width), F32),
                   jax.ShapeDtypeStruct((1, width), F32)],
        scratch_shapes=[pltpu.VMEM((width, width), F32)],
        compiler_params=pltpu.CompilerParams(dimension_semantics=("arbitrary",),
                                             vmem_limit_bytes=_vmem_limit(80 * t_rows * width)),
    )(_in_hbm(yc), _in_hbm(proj), _in_hbm(d_out), d_skip, b_glu, w_glu)


def _sb_tri(kind):
    row = lax.broadcasted_iota(jnp.int32, (SB_BLOCK, SB_BLOCK), 0)
    col = lax.broadcasted_iota(jnp.int32, (SB_BLOCK, SB_BLOCK), 1)
    if kind == "after":
        return (row > col).astype(BF16)
    if kind == "from":
        return (row >= col).astype(BF16)
    return col < row


def _split_dot(x, m):
    hi = x.astype(BF16)
    lo = (x - hi.astype(F32)).astype(BF16)
    return (lax.dot_general(hi, m, NN, preferred_element_type=F32)
            + lax.dot_general(lo, m, NN, preferred_element_type=F32))


def _sb_scores(qh, k2):
    z = lax.dot_general(qh, k2, NT, preferred_element_type=F32)
    log_beta = jnp.minimum(z, 0.0) - jnp.log(1.0 + jnp.exp(-jnp.abs(z)))
    return log_beta, log_beta - z


def _sb_attention_fwd(proj, sb_width, beside=None):
    seq = proj.shape[0]
    n_pair, n_q = sb_width // LANES, seq // (SB_BLOCK * SB_GROUP)
    scale = 1.0 / (HEAD_DIM ** 0.5)
    chains = [(s, h) for s in range(SB_GROUP) for h in range(2)]

    def body(q_ref, k_ref, v_ref, o_ref, o32_ref):
        first = pl.program_id(1) * SB_GROUP
        lane = lax.broadcasted_iota(jnp.int32, (SB_BLOCK, LANES), 1)
        m_after, causal = _sb_tri("after"), _sb_tri("mask")
        heads = [lane < HEAD_DIM, lane >= HEAD_DIM]
        rows = [pl.ds(s * SB_BLOCK, SB_BLOCK) for s in range(SB_GROUP)]
        qh = {(s, h): (jnp.where(heads[h], q_ref[rows[s], :], 0.0) * scale).astype(BF16) for s, h in chains}

        def key_rows(s, r):
            kb = first + s - r
            return kb >= 0, pl.ds(pl.multiple_of(jnp.maximum(kb, 0) * SB_BLOCK, SB_BLOCK), SB_BLOCK)

        def scores(r, diag):
            out = []
            for s in range(SB_GROUP):
                live, ks = key_rows(s, r)
                k2 = k_ref[ks, :].astype(BF16)
                for h in range(2):
                    log_beta, log_1m = _sb_scores(qh[s, h], k2)
                    if diag:
                        log_1m = jnp.where(causal, log_1m, 0.0)
                    else:
                        log_1m = jnp.where(live, log_1m, 0.0)
                    out += [log_beta + _split_dot(log_1m, m_after), jnp.sum(log_1m, axis=1, keepdims=True)]
            return tuple(out)

        def weigh(r, sc, carry, acc, diag):
            out = []
            for c, (s, h) in enumerate(chains):
                live, ks = key_rows(s, r)
                v2 = v_ref[ks, :].astype(BF16)
                w = jnp.exp(sc[2 * c] + carry[c])
                w = jnp.where(causal, w, 0.0) if diag else jnp.where(live, w, 0.0)
                out.append(acc[c] + lax.dot_general(w.astype(BF16), v2, NN, preferred_element_type=F32))
            return tuple(out)

        zero = jnp.zeros((SB_BLOCK, LANES), F32)
        zcol = jnp.zeros((SB_BLOCK, 1), F32)
        sc = scores(0, True)
        acc = weigh(0, sc, (zcol,) * len(chains), (zero,) * len(chains), True)
        carry = tuple(sc[2 * c + 1] for c in range(len(chains)))
        last = first + SB_GROUP - 1

        def loop(st):
            r, carry, acc = st
            sc = scores(r, False)
            after = tuple(carry[c] + sc[2 * c + 1] for c in range(len(chains)))
            top = jnp.max(after[0])
            for c in range(1, len(chains)):
                top = jnp.maximum(top, jnp.max(after[c]))
            acc = weigh(r, sc, carry, acc, False)
            return jnp.where(top < SB_UNDERFLOW, last + 1, r + 1), after, acc

        _, _, acc = lax.while_loop(lambda st: st[0] <= last, loop, (1, carry, acc))
        for s in range(SB_GROUP):
            out = jnp.where(heads[0], acc[2 * s], acc[2 * s + 1])
            o_ref[rows[s], :] = out.astype(o_ref.dtype)
            o32_ref[rows[s], :] = out

    q_spec = _spec((SB_BLOCK * SB_GROUP, LANES), lambda h, i: (i, h))
    kv = [_spec((seq, LANES), lambda h, i, o=o: (0, o + h)) for o in (n_pair, 2 * n_pair)]
    return _call_beside(
        beside, body, "sb_attention_fwd", (n_pair, n_q), [q_spec] + kv, [q_spec, q_spec],
        [jax.ShapeDtypeStruct((seq, sb_width), BF16), jax.ShapeDtypeStruct((seq, sb_width), F32)], [],
        _vmem_limit(2 * seq * LANES * 4), (proj, proj, proj), ("parallel", "arbitrary"))


def _sb_attention_bwd(proj, o32, do, sb_width, beside=None):
    seq = proj.shape[0]
    n_pair, n_q = sb_width // LANES, seq // (SB_BLOCK * SB_GROUP)
    scale = 1.0 / (HEAD_DIM ** 0.5)
    chains = [(s, h) for s in range(SB_GROUP) for h in range(2)]
    n_c = len(chains)

    def body(q_ref, k_ref, v_ref, o_ref, do_ref, dq_ref, dk_out_ref, dv_out_ref, dk_ref, dv_ref):
        qi = pl.program_id(1)
        first = qi * SB_GROUP

        @pl.when(qi == 0)
        def _():
            dk_ref[...] = jnp.zeros_like(dk_ref)
            dv_ref[...] = jnp.zeros_like(dv_ref)

        lane = lax.broadcasted_iota(jnp.int32, (SB_BLOCK, LANES), 1)
        m_after, m_from, causal = _sb_tri("after"), _sb_tri("from"), _sb_tri("mask")
        heads = [lane < HEAD_DIM, lane >= HEAD_DIM]
        rows = [pl.ds(s * SB_BLOCK, SB_BLOCK) for s in range(SB_GROUP)]
        qh, doh_b, total = {}, {}, {}
        for s, h in chains:
            qh[s, h] = (jnp.where(heads[h], q_ref[rows[s], :], 0.0) * scale).astype(BF16)
            doh = jnp.where(heads[h], do_ref[rows[s], :].astype(F32), 0.0)
            doh_b[s, h] = doh.astype(BF16)
            total[s, h] = jnp.sum(doh * o_ref[rows[s], :], axis=1, keepdims=True)

        def key_rows(s, r):
            kb = first + s - r
            return kb >= 0, pl.ds(pl.multiple_of(jnp.maximum(kb, 0) * SB_BLOCK, SB_BLOCK), SB_BLOCK)

        def scores(r, diag):
            out = []
            for s in range(SB_GROUP):
                live, ks = key_rows(s, r)
                k2 = k_ref[ks, :].astype(BF16)
                v2 = v_ref[ks, :].astype(BF16)
                for h in range(2):
                    log_beta, log_1m = _sb_scores(qh[s, h], k2)
                    log_1m = jnp.where(causal, log_1m, 0.0) if diag else jnp.where(live, log_1m, 0.0)
                    out += [log_beta + _split_dot(log_1m, m_after), jnp.sum(log_1m, axis=1, keepdims=True),
                            lax.dot_general(doh_b[s, h], v2, NT, preferred_element_type=F32), log_beta]
            return tuple(out)

        def pull(r, sc, carry, right, dq, diag):
            right_out, dq_out = [], []
            for s in range(SB_GROUP):
                live, ks = key_rows(s, r)
                k2 = k_ref[ks, :].astype(BF16)
                dv_blk, dk_blk = None, None
                for h in range(2):
                    c = 2 * s + h
                    arg, _, d_w, log_beta = sc[4 * c:4 * c + 4]
                    w = jnp.exp(arg + carry[c])
                    w = jnp.where(causal, w, 0.0) if diag else jnp.where(live, w, 0.0)
                    w_b = w.astype(BF16)
                    d_arg = d_w * w_b.astype(F32)
                    dv_h = lax.dot_general(w_b, doh_b[s, h], TN, preferred_element_type=F32)
                    d_log_1m = total[s, h] - right[c] - _split_dot(d_arg, m_from)
                    beta = jnp.exp(log_beta)
                    dz = d_arg * (1.0 - beta) - beta * d_log_1m
                    dz = jnp.where(causal, dz, 0.0) if diag else jnp.where(live, dz, 0.0)
                    dz_b = dz.astype(BF16)
                    dk_h = lax.dot_general(dz_b, qh[s, h], TN, preferred_element_type=F32)
                    dv_blk = dv_h if h == 0 else dv_blk + dv_h
                    dk_blk = dk_h if h == 0 else dk_blk + dk_h
                    dq_out.append(dq[c] + lax.dot_general(dz_b, k2, NN, preferred_element_type=F32))
                    right_out.append(right[c] + jnp.sum(d_arg, axis=1, keepdims=True))
                dv_ref[ks, :] += dv_blk
                dk_ref[ks, :] += dk_blk
            return tuple(right_out), tuple(dq_out)

        zero = jnp.zeros((SB_BLOCK, LANES), F32)
        zcol = jnp.zeros((SB_BLOCK, 1), F32)
        sc = scores(0, True)
        right, dq = pull(0, sc, (zcol,) * n_c, (zcol,) * n_c, (zero,) * n_c, True)
        carry = tuple(sc[4 * c + 1] for c in range(n_c))
        last = first + SB_GROUP - 1

        def loop(st):
            r, carry, right, dq = st
            sc = scores(r, False)
            after = tuple(carry[c] + sc[4 * c + 1] for c in range(n_c))
            top = jnp.max(after[0])
            for c in range(1, n_c):
                top = jnp.maximum(top, jnp.max(after[c]))
            right, dq = pull(r, sc, carry, right, dq, False)
            return jnp.where(top < SB_UNDERFLOW, last + 1, r + 1), after, right, dq

        _, _, _, dq = lax.while_loop(lambda st: st[0] <= last, loop, (1, carry, right, dq))
        for s in range(SB_GROUP):
            dq_ref[rows[s], :] = (jnp.where(heads[0], dq[2 * s], dq[2 * s + 1]) * scale).astype(dq_ref.dtype)

        @pl.when(qi == n_q - 1)
        def _():
            dk_out_ref[...] = dk_ref[...].astype(dk_out_ref.dtype)
            dv_out_ref[...] = dv_ref[...].astype(dv_out_ref.dtype)

    q_spec = _spec((SB_BLOCK * SB_GROUP, LANES), lambda h, i: (i, h))
    kv = [_spec((seq, LANES), lambda h, i, o=o: (0, o + h)) for o in (n_pair, 2 * n_pair)]
    full = _spec((seq, LANES), lambda h, i: (0, h))
    return _call_beside(
        beside, body, "sb_attention_bwd", (n_pair, n_q), [q_spec] + kv + [q_spec, q_spec], [q_spec, full, full],
        [jax.ShapeDtypeStruct((seq, sb_width), BF16)] * 3,
        [pltpu.VMEM((seq, LANES), F32), pltpu.VMEM((seq, LANES), F32)],
        _vmem_limit(4 * seq * LANES * 4), (proj, proj, proj, o32, do), ("parallel", "arbitrary"))


def _s5_discretize(a_re, a_im, log_dt, b_re, b_im, c_re, c_im):
    n_g, n_p = a_re.shape
    c_g = b_re.shape[-1]
    ns = n_g // SLAB_GROUPS
    dt = jnp.exp(log_dt)[:, None]
    xr, xi = a_re * dt, a_im * dt
    mag = jnp.exp(xr)
    lr, li = mag * jnp.cos(xi), mag * jnp.sin(xi)
    den = a_re * a_re + a_im * a_im
    fr = ((lr - 1.0) * a_re + li * a_im) / den
    fi = (li * a_re - (lr - 1.0) * a_im) / den
    bb_re = fr[..., None] * b_re - fi[..., None] * b_im
    bb_im = fr[..., None] * b_im + fi[..., None] * b_re
    eye = jnp.eye(SLAB_GROUPS, dtype=F32)

    def diag_b(m):
        m = jnp.transpose(m.reshape(ns, SLAB_GROUPS, n_p, c_g), (0, 1, 3, 2))
        m = m[:, :, :, None, :] * eye[None, :, None, :, None]
        return m.reshape(ns, SLAB_GROUPS * c_g, SLAB_GROUPS * n_p)

    def diag_c(m):
        m = jnp.transpose(m.reshape(ns, SLAB_GROUPS, c_g, n_p), (0, 1, 3, 2))
        m = m[:, :, :, None, :] * eye[None, :, None, :, None]
        return m.reshape(ns, SLAB_GROUPS * n_p, SLAB_GROUPS * c_g)

    bs = jnp.concatenate([diag_b(bb_re), diag_b(bb_im)], axis=-1)
    cs = jnp.concatenate([diag_c(c_re), -diag_c(c_im)], axis=1)
    lam = jnp.concatenate([lr.reshape(ns, 1, -1), li.reshape(ns, 1, -1)], axis=-1)
    return bs, cs, lam


def _s5_powers(a_re, a_im, log_dt, n):
    n_g, n_p = a_re.shape
    ns = n_g // SLAB_GROUPS
    dt = jnp.exp(log_dt)[:, None]
    mag = jnp.exp(a_re * dt)
    base_r, base_i = mag * jnp.cos(a_im * dt), mag * jnp.sin(a_im * dt)
    steps = jnp.arange(1, n + 1, dtype=jnp.int32)[:, None, None]
    pr, pi = jnp.ones((n, n_g, n_p), F32), jnp.zeros((n, n_g, n_p), F32)
    for b in range(n.bit_length()):
        take = ((steps >> b) & 1) == 1
        pr, pi = (jnp.where(take, pr * base_r - pi * base_i, pr), jnp.where(take, pr * base_i + pi * base_r, pi))
        base_r, base_i = base_r * base_r - base_i * base_i, 2.0 * base_r * base_i

    def slabs(re, im):
        one = lambda m: jnp.transpose(m.reshape(n, ns, SLAB_GROUPS * n_p), (1, 0, 2))
        return jnp.concatenate([one(re), one(im)], axis=-1)

    return slabs(pr, pi), slabs(pr[::-1], -pi[::-1])


def _lanes(j):
    return slice(j * LANES, (j + 1) * LANES)


def _tile8(k):
    return pl.ds(pl.multiple_of(k * SUBLANES, SUBLANES), SUBLANES)


def _s5_interleave(dst_ref, src_ref, t_seg):
    def body(k, _):
        dst_ref[_tile8(k), :] = src_ref[pl.ds(k, SUBLANES, stride=t_seg), :]
        return 0

    lax.fori_loop(0, t_seg, body, 0, unroll=4)


def _s5_join_segments(st_ref, end_ref, car_ref, tab_ref, row, order, n_pair):
    for j in range(n_pair):
        re, im = _lanes(j), _lanes(n_pair + j)
        cr, ci = st_ref[:, re], st_ref[:, im]
        tr, ti = tab_ref[row:row + 1, re], tab_ref[row:row + 1, im]
        for s in order:
            car_ref[s:s + 1, re] = cr
            car_ref[s:s + 1, im] = ci
            er, ei = end_ref[s:s + 1, re], end_ref[s:s + 1, im]
            cr, ci = er + tr * cr - ti * ci, ei + tr * ci + ti * cr
        st_ref[:, re] = cr
        st_ref[:, im] = ci


def _s5_add_carries(buf_ref, car_ref, tab_ref, t_seg, n_pair):
    def fix(k, _):
        rows = _tile8(k)
        tab = tab_ref[pl.ds(k, 1), :]
        for j in range(n_pair):
            re, im = _lanes(j), _lanes(n_pair + j)
            cr, ci = car_ref[:, re], car_ref[:, im]
            tr, ti = tab[:, re], tab[:, im]
            buf_ref[rows, re] += tr * cr - ti * ci
            buf_ref[rows, im] += tr * ci + ti * cr
        return 0

    lax.fori_loop(0, t_seg, fix, 0, unroll=2)


def _s5_scan_fwd(proj, u_col, bs, cs, lam, pw, t_blk, beside=None):
    seq = proj.shape[0]
    ns, _, w2 = bs.shape
    n_pair = w2 // (2 * LANES)
    t_seg, n_t = t_blk // SUBLANES, seq // t_blk

    def body(u_ref, bs_ref, cs_ref, lam_ref, pw_ref, yc_ref, h_ref, st_ref, end_ref, car_ref, ui_ref, bu_ref, yi_ref):
        @pl.when(pl.program_id(1) == 0)
        def _():
            st_ref[...] = jnp.zeros_like(st_ref)

        _s5_interleave(ui_ref, u_ref, t_seg)
        bu_ref[...] = lax.dot_general(ui_ref[...].astype(BF16), bs_ref[...], NN, preferred_element_type=F32)
        lam_r = [jnp.broadcast_to(lam_ref[:, _lanes(j)], (SUBLANES, LANES)) for j in range(n_pair)]
        lam_i = [jnp.broadcast_to(lam_ref[:, _lanes(n_pair + j)], (SUBLANES, LANES)) for j in range(n_pair)]

        def step(k, c):
            rows = _tile8(k)
            out = []
            for j in range(n_pair):
                hr, hi = c[2 * j], c[2 * j + 1]
                nr = lam_r[j] * hr - lam_i[j] * hi + bu_ref[rows, _lanes(j)]
                ni = lam_i[j] * hr + lam_r[j] * hi + bu_ref[rows, _lanes(n_pair + j)]
                h_ref[rows, _lanes(j)] = nr
                h_ref[rows, _lanes(n_pair + j)] = ni
                out += [nr, ni]
            return tuple(out)

        ends = lax.fori_loop(0, t_seg, step, (jnp.zeros((SUBLANES, LANES), F32),) * (2 * n_pair), unroll=4)
        for j in range(n_pair):
            end_ref[:, _lanes(j)] = ends[2 * j]
            end_ref[:, _lanes(n_pair + j)] = ends[2 * j + 1]
        _s5_join_segments(st_ref, end_ref, car_ref, pw_ref, t_seg - 1, list(range(SUBLANES)), n_pair)
        _s5_add_carries(h_ref, car_ref, pw_ref, t_seg, n_pair)
        yi_ref[...] = lax.dot_general(h_ref[...].astype(BF16), cs_ref[...], NN, preferred_element_type=F32)

        def scatter(k, _):
            yc_ref[pl.ds(k, SUBLANES, stride=t_seg), :] = yi_ref[_tile8(k), :]
            return 0

        lax.fori_loop(0, t_seg, scatter, 0, unroll=4)

    return _call_beside(
        beside, body, "s5_scan_fwd", (ns, n_t),
        [_spec((t_blk, LANES), lambda s, i: (i, u_col + s)),
         _spec((None, LANES, w2), lambda s, i: (s, 0, 0)),
         _spec((None, w2, LANES), lambda s, i: (s, 0, 0)),
         _spec((None, 1, w2), lambda s, i: (s, 0, 0)),
         _spec((None, t_seg, w2), lambda s, i: (s, 0, 0))],
        [_spec((t_blk, LANES), lambda s, i: (i, s)),
         _spec((None, t_blk, w2), lambda s, i: (s, i, 0))],
        [jax.ShapeDtypeStruct((seq, ns * LANES), F32), jax.ShapeDtypeStruct((ns, seq, w2), F32)],
        [pltpu.VMEM((1, w2), F32), pltpu.VMEM((SUBLANES, w2), F32), pltpu.VMEM((SUBLANES, w2), F32),
         pltpu.VMEM((t_blk, LANES), F32), pltpu.VMEM((t_blk, w2), F32), pltpu.VMEM((t_blk, LANES), F32)],
        _vmem_limit(3 * t_blk * w2 * 4), (proj, bs, cs, lam, pw), ("parallel", "arbitrary"))


def _s5_scan_bwd(proj, u_col, states, d_yc, du_extra, bs, cs, lam, qw, t_blk):
    seq = proj.shape[0]
    ns, _, w2 = bs.shape
    n_pair = w2 // (2 * LANES)
    t_seg, n_t = t_blk // SUBLANES, seq // t_blk

    def body(u_ref, h_ref, hp_ref, dyc_ref, dux_ref, bs_ref, cs_ref, lam_ref, qw_ref,
             du_ref, dbs_ref, dcs_ref, dlam_ref, g_ref, gd_ref, st_ref, end_ref, car_ref, ui_ref, dyi_ref, dui_ref):
        i = pl.program_id(1)

        @pl.when(i == 0)
        def _():
            st_ref[...] = jnp.zeros_like(st_ref)
            dbs_ref[...] = jnp.zeros_like(dbs_ref)
            dcs_ref[...] = jnp.zeros_like(dcs_ref)
            dlam_ref[...] = jnp.zeros_like(dlam_ref)

        _s5_interleave(ui_ref, u_ref, t_seg)
        _s5_interleave(dyi_ref, dyc_ref, t_seg)
        dyc_b = dyi_ref[...].astype(BF16)
        gd_ref[...] = lax.dot_general(dyc_b, cs_ref[...], NT, preferred_element_type=F32)
        lam_r = [jnp.broadcast_to(lam_ref[:, _lanes(j)], (SUBLANES, LANES)) for j in range(n_pair)]
        lam_i = [jnp.broadcast_to(lam_ref[:, _lanes(n_pair + j)], (SUBLANES, LANES)) for j in range(n_pair)]

        def step(kk, c):
            rows = _tile8(t_seg - 1 - kk)
            out = []
            for j in range(n_pair):
                gr_n, gi_n = c[2 * j], c[2 * j + 1]
                gr = gd_ref[rows, _lanes(j)] + lam_r[j] * gr_n + lam_i[j] * gi_n
                gi = gd_ref[rows, _lanes(n_pair + j)] + lam_r[j] * gi_n - lam_i[j] * gr_n
                g_ref[rows, _lanes(j)] = gr
                g_ref[rows, _lanes(n_pair + j)] = gi
                out += [gr, gi]
            return tuple(out)

        zero = jnp.zeros((SUBLANES, LANES), F32)
        firsts = lax.fori_loop(0, t_seg, step, (zero,) * (2 * n_pair), unroll=4)
        for j in range(n_pair):
            end_ref[:, _lanes(j)] = firsts[2 * j]
            end_ref[:, _lanes(n_pair + j)] = firsts[2 * j + 1]
        _s5_join_segments(st_ref, end_ref, car_ref, qw_ref, 0, list(range(SUBLANES))[::-1], n_pair)
        _s5_add_carries(g_ref, car_ref, qw_ref, t_seg, n_pair)

        def pair_up(k, c):
            rows, prev = _tile8(k), _tile8(k - 1)
            out = []
            for j in range(n_pair):
                re, im = _lanes(j), _lanes(n_pair + j)
                gr, gi, hr, hi = g_ref[rows, re], g_ref[rows, im], h_ref[prev, re], h_ref[prev, im]
                out += [c[2 * j] + gr * hr + gi * hi, c[2 * j + 1] + gi * hr - gr * hi]
            return tuple(out)

        acc = lax.fori_loop(1, t_seg, pair_up, (zero,) * (2 * n_pair), unroll=4)
        has_prev = (i < n_t - 1).astype(F32)
        first_seg = lax.broadcasted_iota(jnp.int32, (SUBLANES, LANES), 0) == 0
        last = _tile8(t_seg - 1)
        for j in range(n_pair):
            re, im = _lanes(j), _lanes(n_pair + j)
            gr, gi = g_ref[0:SUBLANES, re], g_ref[0:SUBLANES, im]
            hr = jnp.where(first_seg, hp_ref[SUBLANES - 1:, re] * has_prev, pltpu.roll(h_ref[last, re], 1, 0))
            hi = jnp.where(first_seg, hp_ref[SUBLANES - 1:, im] * has_prev, pltpu.roll(h_ref[last, im], 1, 0))
            dlam_ref[:, re] += jnp.sum(acc[2 * j] + gr * hr + gi * hi, axis=0, keepdims=True)
            dlam_ref[:, im] += jnp.sum(acc[2 * j + 1] + gi * hr - gr * hi, axis=0, keepdims=True)

        g_b = g_ref[...].astype(BF16)
        dui_ref[...] = lax.dot_general(g_b, bs_ref[...], NT, preferred_element_type=F32)
        dbs_ref[...] += lax.dot_general(ui_ref[...].astype(BF16), g_b, TN, preferred_element_type=F32)
        dcs_ref[...] += lax.dot_general(h_ref[...].astype(BF16), dyc_b, TN, preferred_element_type=F32)

        def scatter(k, _):
            rows = pl.ds(k, SUBLANES, stride=t_seg)
            du_ref[rows, :] = (dui_ref[_tile8(k), :] + dux_ref[rows, :]).astype(du_ref.dtype)
            return 0

        lax.fori_loop(0, t_seg, scatter, 0, unroll=4)

    rev = lambda i: n_t - 1 - i
    return pl.pallas_call(
        body, name="s5_scan_bwd", grid=(ns, n_t),
        in_specs=[_spec((t_blk, LANES), lambda s, i: (rev(i), u_col + s)),
                  _spec((None, t_blk, w2), lambda s, i: (s, rev(i), 0)),
                  _spec((None, SUBLANES, w2), lambda s, i: (s, jnp.maximum(rev(i) * t_seg - 1, 0), 0)),
                  _spec((t_blk, LANES), lambda s, i: (rev(i), s)),
                  _spec((t_blk, LANES), lambda s, i: (rev(i), s)),
                  _spec((None, LANES, w2), lambda s, i: (s, 0, 0)),
                  _spec((None, w2, LANES), lambda s, i: (s, 0, 0)),
                  _spec((None, 1, w2), lambda s, i: (s, 0, 0)),
                  _spec((None, t_seg, w2), lambda s, i: (s, 0, 0))],
        out_specs=[_spec((t_blk, LANES), lambda s, i: (rev(i), s)),
                   _spec((None, LANES, w2), lambda s, i: (s, 0, 0)),
                   _spec((None, w2, LANES), lambda s, i: (s, 0, 0)),
                   _spec((None, 1, w2), lambda s, i: (s, 0, 0))],
        out_shape=[jax.ShapeDtypeStruct((seq, ns * LANES), F32), jax.ShapeDtypeStruct(bs.shape, F32),
                   jax.ShapeDtypeStruct(cs.shape, F32), jax.ShapeDtypeStruct(lam.shape, F32)],
        scratch_shapes=[pltpu.VMEM((t_blk, w2), F32), pltpu.VMEM((t_blk, w2), F32), pltpu.VMEM((1, w2), F32),
                        pltpu.VMEM((SUBLANES, w2), F32), pltpu.VMEM((SUBLANES, w2), F32),
                        pltpu.VMEM((t_blk, LANES), F32), pltpu.VMEM((t_blk, LANES), F32), pltpu.VMEM((t_blk, LANES), F32)],
        compiler_params=pltpu.CompilerParams(dimension_semantics=("parallel", "arbitrary"),
                                             vmem_limit_bytes=_vmem_limit(5 * t_blk * w2 * 4)),
    )(*[_in_hbm(a) for a in (proj, states, states, d_yc, du_extra, bs, cs, lam, qw)])


def _loss_head(last_fn, x, y, vecs, target, t_m):
    seq, d = x.shape
    n_v = len(vecs)

    def body(*refs):
        t_ref, loss_ref, dy_ref = refs[2 + n_v:]

        @pl.when(pl.program_id(0) == 0)
        def _():
            loss_ref[...] = jnp.zeros_like(loss_ref)

        diff = last_fn(*[r[...] for r in refs[:2 + n_v]]) - t_ref[...]
        dy_ref[...] = diff / d
        loss_ref[...] += 0.5 * jnp.sum(diff * diff) / d

    row = _spec((t_m, d), lambda i: (i, 0))
    vec = _spec((1, d), lambda i: (0, 0))
    return pl.pallas_call(
        body, name="loss_head", grid=(seq // t_m,), in_specs=[row, row] + [vec] * n_v + [row],
        out_specs=[_spec((SUBLANES, LANES), lambda i: (0, 0)), row],
        out_shape=[jax.ShapeDtypeStruct((SUBLANES, LANES), F32), jax.ShapeDtypeStruct((seq, d), F32)],
        compiler_params=pltpu.CompilerParams(dimension_semantics=("arbitrary",),
                                             vmem_limit_bytes=_vmem_limit(10 * t_m * d * 4)),
    )(_in_hbm(x), _in_hbm(y), *vecs, _in_hbm(target))


def _adamw_fn(w, m, v, *partials):
    g = partials[0]
    for p in partials[1:]:
        g = g + p
    m2 = ADAM_B1 * m + (1.0 - ADAM_B1) * g
    v2 = ADAM_B2 * v + (1.0 - ADAM_B2) * (g * g)
    m_hat = m2 / (1.0 - ADAM_B1 ** ADAM_STEP)
    v_hat = v2 / (1.0 - ADAM_B2 ** ADAM_STEP)
    delta = -ADAM_LR * (m_hat / (jnp.sqrt(v_hat) + ADAM_EPS) + ADAM_WD * w)
    return g, delta, m2, v2


def _adamw(name, w, m, v, partials):
    rows, cols = w.shape
    t_r = rows
    for cand in (512, 256, 128, 64, 32, 16, 8):
        if rows % cand == 0 and cand * cols * 4 <= (1 << 20):
            t_r = cand
            break
    n_p = partials.shape[0]
    row = lambda i: (i, 0)
    ins = [(a, (t_r, cols), row) for a in (w, m, v)]
    ins += [(partials, (None, t_r, cols), (lambda i, j=j: (j, i, 0))) for j in range(n_p)]
    outs = [((rows, cols), F32, (t_r, cols), row)] * 4
    return _rowwise(name, _adamw_fn, ins, outs, (rows // t_r,))


SMALL_PARAMS = ("b_ada", "ssm_a_re", "ssm_a_im", "ssm_log_dt", "ssm_b_re", "ssm_b_im", "ssm_c_re", "ssm_c_im",
                "ssm_d", "b_glu", "ln1_g", "ln1_b", "ln2_g", "ln2_b")
WEIGHTS = ("w_ada", "b_ada", "w_in", "w_sb_up", "ssm_a_re", "ssm_a_im", "ssm_log_dt", "ssm_b_re", "ssm_b_im",
           "ssm_c_re", "ssm_c_im", "ssm_d", "w_glu", "b_glu", "w_ssm_up", "w_out", "ln1_g", "ln1_b", "w_ffn_in",
           "w_ffn_out", "ln2_g", "ln2_b")
ARG_NAMES = (("x", "c") + WEIGHTS + ("loss_target",) + tuple("m_" + n for n in WEIGHTS)
             + tuple("v_" + n for n in WEIGHTS))


def _pack(arrs):
    flat = jnp.concatenate([a.reshape(-1) for a in arrs])
    pad = (-flat.shape[0]) % (PACK_ROWS * LANES)
    return jnp.pad(flat, (0, pad)).reshape(-1, LANES)


def _unpack(packed, like):
    lead = packed.shape[:-2]
    flat = packed.reshape(lead + (-1,))
    out, off = [], 0
    for a in like:
        out.append(flat[..., off:off + a.size].reshape(lead + a.shape))
        off += a.size
    return out


def kernel(x, c, w_ada, b_ada, w_in, w_sb_up, ssm_a_re, ssm_a_im, ssm_log_dt, ssm_b_re, ssm_b_im, ssm_c_re,
           ssm_c_im, ssm_d, w_glu, b_glu, w_ssm_up, w_out, ln1_g, ln1_b, w_ffn_in, w_ffn_out, ln2_g, ln2_b,
           loss_target, m_w_ada, m_b_ada, m_w_in, m_w_sb_up, m_ssm_a_re, m_ssm_a_im, m_ssm_log_dt, m_ssm_b_re,
           m_ssm_b_im, m_ssm_c_re, m_ssm_c_im, m_ssm_d, m_w_glu, m_b_glu, m_w_ssm_up, m_w_out, m_ln1_g, m_ln1_b,
           m_w_ffn_in, m_w_ffn_out, m_ln2_g, m_ln2_b, v_w_ada, v_b_ada, v_w_in, v_w_sb_up, v_ssm_a_re, v_ssm_a_im,
           v_ssm_log_dt, v_ssm_b_re, v_ssm_b_im, v_ssm_c_re, v_ssm_c_im, v_ssm_d, v_w_glu, v_b_glu, v_w_ssm_up,
           v_w_out, v_ln1_g, v_ln1_b, v_w_ffn_in, v_w_ffn_out, v_ln2_g, v_ln2_b):
    given = locals()
    return _train_step({n: given[n] for n in ARG_NAMES})


def _train_step(p):
    x0 = p["x"][0]
    target = p["loss_target"][0]
    seq, d = x0.shape
    depth = p["w_ada"].shape[0]
    n_ada = p["w_ada"].shape[2]
    n_in = p["w_in"].shape[2]
    sb_w = p["w_sb_up"].shape[1]
    ssm_w = p["w_ssm_up"].shape[1]
    n_up = p["w_sb_up"].shape[2]
    n_ffn = p["w_ffn_in"].shape[2]
    ffn = N_DEV * p["w_ffn_out"].shape[1]
    in_cols = N_DEV * n_in
    alpha = (2 * depth) ** 0.25
    resid_ln, resid_ln_mod = _make_resid_fns(alpha)
    t_r = min(512, seq)
    n_r = seq // t_r
    t_m = min(1024, seq)
    n_m = seq // t_m
    t_d = _tile(d)
    assert n_ffn * (N_DEV // 2) == ffn and sb_w % LANES == 0 and ssm_w % LANES == 0 and d % LANES == 0
    assert n_in % LANES == 0 and n_up % LANES == 0 and seq % t_m == 0 and in_cols == 3 * sb_w + ssm_w + 2 * d
    assert (3 * sb_w) % ssm_w == 0 and (3 * sb_w + ssm_w) % (2 * d) == 0
    assert sb_w % n_in == 0 and ssm_w % n_in == 0 and d % n_in == 0 and seq % (SB_BLOCK * SB_GROUP) == 0
    proj_starts = [c // n_in for c in (0, sb_w, 2 * sb_w, 3 * sb_w, 3 * sb_w + ssm_w)]

    bf = lambda a: a.astype(BF16)
    got = _exchange("gather_first", [], [bf(p["w_in"][0]), p["c"]])
    wg_in = [got[0]] + [None] * (depth - 1)
    c_all = got[1].reshape(N_DEV, d)
    small_names = ("w_sb_up", "w_ssm_up", "w_glu", "w_out")
    wg_ffn_in, wg_ffn_out, wg = [None] * depth, [None] * depth, {}

    c_pad = jnp.pad(c_all, ((0, 2 * SUBLANES - N_DEV), (0, 0)))
    c_act = _rowwise("silu_c", lambda v: v * jax.nn.sigmoid(v), [(c_pad, c_pad.shape, lambda i: (0, 0))],
                     [(c_pad.shape, F32, c_pad.shape, lambda i: (0, 0))], (1,))[0]
    rows_c = c_pad.shape[0]
    mod_cols = [
        _mm(f"mod_{l}", c_act, p["w_ada"],
            _spec((rows_c, d), lambda i, j, k: (0, 0)), _spec((None, d, n_ada), lambda i, j, k, l=l: (l, 0, 0)),
            _spec((rows_c, n_ada), lambda i, j, k: (0, 0)), (rows_c, n_ada), F32, (1, 1, 1), NN)
        for l in range(depth)]
    mod_send = jnp.stack([m[:N_DEV] for m in mod_cols], axis=1)
    mod_recv = _exchange("exchange_mod", [mod_send], [])[0]
    mod_nobias = jnp.swapaxes(mod_recv, 0, 1).reshape(depth, N_DEV * n_ada)
    full2 = lambda a: (a, a.shape, lambda i: (0, 0))
    mod = _rowwise("mod_bias", lambda a, b: a + b, [full2(mod_nobias), full2(p["b_ada"])],
                   [(mod_nobias.shape, F32, mod_nobias.shape, lambda i: (0, 0))], (1,))[0]
    vec = lambda a: a.reshape(1, -1)
    mods = [[vec(mod[l, j * d:(j + 1) * d]) for j in range(6)] for l in range(depth)]
    ln = {n: [vec(p[n][l]) for l in range(depth)] for n in ("ln1_g", "ln1_b", "ln2_g", "ln2_b")}

    row_spec = lambda width: ((t_r, width), lambda i: (i, 0))
    col_spec = lambda width, cb: ((t_r, width), lambda i, cb=cb: (i, cb))
    vec_spec = lambda width: ((1, width), lambda i: (0, 0))
    rows_in = lambda a: (a,) + row_spec(a.shape[1])
    vec_in = lambda a: (a,) + vec_spec(a.shape[1])
    row_out = lambda width, dt: ((seq, width), dt) + row_spec(width)

    s5 = [_s5_discretize(*[p[n][l] for n in ("ssm_a_re", "ssm_a_im", "ssm_log_dt", "ssm_b_re", "ssm_b_im",
                                               "ssm_c_re", "ssm_c_im")]) for l in range(depth)]
    s5_b16 = [(bs.astype(BF16), cs.astype(BF16), lam) for bs, cs, lam in s5]
    t_scan = min(1024, seq)
    s5_pw = [_s5_powers(p["ssm_a_re"][l], p["ssm_a_im"][l], p["ssm_log_dt"][l], t_scan // SUBLANES)
             for l in range(depth)]
    u_col = 3 * sb_w // LANES
    gates_cb = (3 * sb_w + ssm_w) // (2 * d)
    ssm_d = [vec(p["ssm_d"][l]) for l in range(depth)]
    b_glu = [vec(p["b_glu"][l]) for l in range(depth)]
    n_half = N_DEV // 2

    h = _rowwise("modulate_in", _modulate, [rows_in(x0), vec_in(mods[0][1]), vec_in(mods[0][0])],
                 [row_out(d, BF16)], (n_r,))[0]
    saved = []
    x_cur = x0
    for l in range(depth):
        sv = {"x_in": x_cur, "h": h}
        last = l == depth - 1
        t_n = _tile(n_in)
        r_n = n_in // t_n
        proj = _mm(f"proj_{l}", h, wg_in[l],
                   _spec((t_m, d), lambda i, j, k: (i, 0)),
                   _spec((None, d, t_n), lambda i, j, k, r=r_n: (j // r, 0, j % r)),
                   _spec((t_m, t_n), lambda i, j, k: (i, j)), (seq, in_cols), F32, (n_m, N_DEV * r_n, 1), NN,
                   reread=(True, True))
        arriving = [bf(p["w_ffn_in"][l])] + ([bf(p[n]) for n in small_names] if l == 0 else [])
        (o_sb, o_sb32), got = _sb_attention_fwd(proj, sb_w, beside=_Exchange(gather=arriving))
        wg_ffn_in[l] = got[0]
        if l == 0:
            wg = dict(zip(small_names, got[1:]))
            for n in ("w_glu", "w_out"):
                wg[n] = jnp.swapaxes(wg[n], 0, 1).reshape(depth, -1, wg[n].shape[-1])
            for n in ("w_sb_up", "w_ssm_up"):
                wg[n] = jnp.transpose(wg[n], (1, 2, 0, 3)).reshape(depth, wg[n].shape[2], d)
        bs16, cs16, lam = s5_b16[l]
        arriving = [bf(p["w_ffn_out"][l])] + ([] if last else [bf(p["w_in"][l + 1])])
        (yc, states), got = _s5_scan_fwd(proj, u_col, bs16, cs16, lam, s5_pw[l][0], t_scan,
                                         beside=_Exchange(gather=arriving))
        wg_ffn_out[l] = got[0].reshape(n_half, n_ffn, d)
        if not last:
            wg_in[l + 1] = got[1]
        s5_out = _s5_head(f"s5_head_{l}", yc, proj, 3 * sb_w // ssm_w, ssm_d[l], b_glu[l], wg["w_glu"], l, t_r)

        merged, y_sb, y_ssm = _up_merge(f"up_merge_{l}", o_sb, s5_out, proj, gates_cb, wg["w_sb_up"], wg["w_ssm_up"],
                                        l, t_r)
        y_mix = _mm(f"out_proj_{l}", merged, wg["w_out"],
                    _spec((t_m, d), lambda i, j, k: (i, 0)), _spec((None, d, t_d), lambda i, j, k, l=l: (l, 0, j)),
                    _spec((t_m, t_d), lambda i, j, k: (i, j)), (seq, d), F32, (n_m, d // t_d, 1), NN)
        vecs_a = [mods[l][2], ln["ln1_g"][l], ln["ln1_b"][l], mods[l][4], mods[l][3]]
        x_mid, h2 = _rowwise(f"resid_mix_{l}", resid_ln_mod, [rows_in(x_cur), rows_in(y_mix)] + [vec_in(v) for v in vecs_a],
                             [row_out(d, F32), row_out(d, BF16)], (n_r,))
        a_ffn, f_act = _ffn_in_swiglu(f"ffn_in_{l}", h2, wg_ffn_in[l], t_r)
        y_ffn = _mm(f"ffn_out_{l}", f_act, wg_ffn_out[l],
                    _spec((None, t_m, n_ffn), lambda i, j, k: (k, i, 0)),
                    _spec((None, n_ffn, t_d), lambda i, j, k: (k, 0, j)),
                    _spec((t_m, t_d), lambda i, j, k: (i, j)), (seq, d), F32, (n_m, d // t_d, n_half), NN)
        vecs_b = [mods[l][5], ln["ln2_g"][l], ln["ln2_b"][l]] + ([] if last else [mods[l + 1][1], mods[l + 1][0]])
        if not last:
            x_cur, h = _rowwise(f"resid_ffn_{l}", resid_ln_mod,
                                [rows_in(x_mid), rows_in(y_ffn)] + [vec_in(v) for v in vecs_b],
                                [row_out(d, F32), row_out(d, BF16)], (n_r,))
        sv.update(proj=proj, o_sb=o_sb, o_sb32=o_sb32, yc=yc, states=states, s5_out=s5_out,
                  y_sb=y_sb, y_ssm=y_ssm, merged=merged, y_mix=y_mix, x_mid=x_mid, h2=h2, a_ffn=a_ffn, f_act=f_act,
                  y_ffn=y_ffn, vecs_a=vecs_a, vecs_b=vecs_b)
        saved.append(sv)

    loss_part, d_x = _loss_head(resid_ln, saved[-1]["x_mid"], saved[-1]["y_ffn"], saved[-1]["vecs_b"], target, t_r)
    loss = lax.psum(loss_part[0, 0], MESH_AXES)

    d_h_next = None
    grads = {n: [None] * depth for n in WEIGHTS}
    d_mod = [[None] * 6 for _ in range(depth)]
    land = {}
    waiting = []
    row_wrt = lambda i, width, dt: (i, "row", (seq, width), dt) + row_spec(width)
    sum_wrt = lambda i, width: (i, "sum", (1, width), F32) + vec_spec(width)
    for l in reversed(range(depth)):
        sv = saved[l]
        last = l == depth - 1
        ins_b = [rows_in(sv["x_mid"]), rows_in(sv["y_ffn"])] + [vec_in(v) for v in sv["vecs_b"]]
        cts_b = [rows_in(d_x)] + ([] if last else [rows_in(d_h_next)])
        wrt_b = [row_wrt(0, d, F32), row_wrt(1, d, BF16)] + [sum_wrt(2 + j, d) for j in range(len(sv["vecs_b"]))]
        res = _rowwise_vjp(f"resid_ffn_bwd_{l}", resid_ln if last else resid_ln_mod, ins_b, cts_b, wrt_b, (n_r,))
        d_x_mid, d_y_ffn = res[0], res[1]
        d_mod[l][5], grads["ln2_g"][l], grads["ln2_b"][l] = res[2], res[3], res[4]
        if not last:
            d_mod[l + 1][1], d_mod[l + 1][0] = res[5], res[6]
        d_a = _ffn_out_dx_swiglu(f"ffn_out_dx_{l}", d_y_ffn, wg_ffn_out[l], sv["a_ffn"], t_r).reshape(N_DEV, seq, n_ffn)
        g_ffn_out = _mm(f"ffn_out_dw_{l}", sv["f_act"], d_y_ffn,
                        _spec((None, t_m, n_ffn), lambda i, j, k: (i, k, 0)), _spec((t_m, t_d), lambda i, j, k: (k, j)),
                        _spec((None, n_ffn, t_d), lambda i, j, k: (i, 0, j)), (n_half, n_ffn, d), GRAD_WIRE,
                        (n_half, d // t_d, n_m), TN, reread=(False, True))
        d_h2 = _mm(f"ffn_in_dx_{l}", d_a, wg_ffn_in[l],
                   _spec((None, t_m, n_ffn), lambda i, j, k: (k, i, 0)),
                   _spec((None, t_d, n_ffn), lambda i, j, k: (k, j, 0)),
                   _spec((t_m, t_d), lambda i, j, k: (i, j)), (seq, d), BRANCH_CT, (n_m, d // t_d, N_DEV), NT)
        g_ffn_in = _mm(f"ffn_in_dw_{l}", sv["h2"], d_a,
                       _spec((t_m, t_d), lambda i, j, k: (k, j)), _spec((None, t_m, n_ffn), lambda i, j, k: (i, k, 0)),
                       _spec((None, t_d, n_ffn), lambda i, j, k: (i, j, 0)), (N_DEV, d, n_ffn), GRAD_WIRE,
                       (N_DEV, d // t_d, n_m), TN, reread=(True, False))
        ins_a = [rows_in(sv["x_in"]), rows_in(sv["y_mix"])] + [vec_in(v) for v in sv["vecs_a"]]
        wrt_a = [row_wrt(0, d, F32), row_wrt(1, d, BF16)] + [sum_wrt(2 + j, d) for j in range(5)]
        res = _rowwise_vjp(f"resid_mix_bwd_{l}", resid_ln_mod, ins_a, [rows_in(d_x_mid), rows_in(d_h2)], wrt_a, (n_r,))
        d_x_in, d_y_mix = res[0], res[1]
        d_mod[l][2], grads["ln1_g"][l], grads["ln1_b"][l], d_mod[l][4], d_mod[l][3] = res[2:7]
        d_merged = _mm(f"out_proj_dx_{l}", d_y_mix, wg["w_out"],
                       _spec((t_m, d), lambda i, j, k: (i, 0)), _spec((None, t_d, d), lambda i, j, k, l=l: (l, j, 0)),
                       _spec((t_m, t_d), lambda i, j, k: (i, j)), (seq, d), BRANCH_CT, (n_m, d // t_d, 1), NT)
        g_out = _mm(f"out_proj_dw_{l}", sv["merged"], d_y_mix,
                    _spec((t_m, t_d), lambda i, j, k: (k, i)), _spec((t_m, t_d), lambda i, j, k: (k, j)),
                    _spec((t_d, t_d), lambda i, j, k: (i, j)), (d, d), GRAD_WIRE, (d // t_d, d // t_d, n_m), TN, reread=(d > t_d, d > t_d))
        gates = (sv["proj"],) + col_spec(2 * d, gates_cb)
        d_y_sb, d_y_ssm, d_gates = _rowwise_vjp(
            f"merge_bwd_{l}", _merge_fn, [rows_in(sv["y_sb"]), rows_in(sv["y_ssm"]), gates], [rows_in(d_merged)],
            [row_wrt(0, d, BF16), row_wrt(1, d, BF16), row_wrt(2, 2 * d, BF16)], (n_r,))

        def up_bwd(name, act, d_y, w, dx_dtype, l=l):
            k_w = act.shape[1]
            dx = _mm(name + "_dx", d_y, w, _spec((t_m, d), lambda i, j, k: (i, 0)),
                     _spec((None, k_w, d), lambda i, j, k: (l, 0, 0)),
                     _spec((t_m, k_w), lambda i, j, k: (i, 0)), (seq, k_w), dx_dtype, (n_m, 1, 1), NT)
            dw = _mm(name + "_dw", act, d_y, _spec((t_m, k_w), lambda i, j, k: (k, 0)),
                     _spec((t_m, t_d), lambda i, j, k: (k, j)),
                     _spec((k_w, t_d), lambda i, j, k: (0, j)), (k_w, d), GRAD_WIRE, (1, d // t_d, n_m), TN,
                     reread=(d > t_d, False))
            return dx, jnp.swapaxes(dw.reshape(k_w, N_DEV, n_up), 0, 1)

        d_o_sb, g_sb_up = up_bwd(f"sb_up_{l}", sv["o_sb"], d_y_sb, wg["w_sb_up"], BF16)
        d_s5_out, g_ssm_up = up_bwd(f"ssm_up_{l}", sv["s5_out"], d_y_ssm, wg["w_ssm_up"], BRANCH_CT)
        waiting += [("w_ffn_in", g_ffn_in), ("w_ffn_out", g_ffn_out.reshape(N_DEV, -1, d)),
                    ("w_out", g_out.reshape(N_DEV, -1, d)), ("w_sb_up", g_sb_up), ("w_ssm_up", g_ssm_up)]
        levels = [l + 1] * (len(waiting) - 5) + [l] * 5
        (d_q, d_k, d_v), got = _sb_attention_bwd(
            sv["proj"], sv["o_sb32"], d_o_sb, sb_w,
            beside=_Exchange(layered=[(g, lv, depth, land.get(n)) for (n, g), lv in zip(waiting, levels)]))
        land.update({n: buf for (n, _), buf in zip(waiting, got)})
        d_yc, d_u_skip, g_glu, grads["ssm_d"][l], grads["b_glu"][l] = _s5_head_bwd(
            f"s5_head_bwd_{l}", sv["yc"], sv["proj"], 3 * sb_w // ssm_w, d_s5_out, ssm_d[l], b_glu[l], wg["w_glu"], l, t_r)
        bs16, cs16, lam = s5_b16[l]
        d_u, d_bs, d_cs, d_lam = _s5_scan_bwd(sv["proj"], u_col, sv["states"], d_yc, d_u_skip, bs16, cs16, lam,
                                              s5_pw[l][1], t_scan)
        raw = [p[n][l] for n in ("ssm_a_re", "ssm_a_im", "ssm_log_dt", "ssm_b_re", "ssm_b_im", "ssm_c_re", "ssm_c_im")]
        _, pull = jax.vjp(_s5_discretize, *raw)
        (grads["ssm_a_re"][l], grads["ssm_a_im"][l], grads["ssm_log_dt"][l], grads["ssm_b_re"][l],
         grads["ssm_b_im"][l], grads["ssm_c_re"][l], grads["ssm_c_im"][l]) = pull((d_bs, d_cs, d_lam))
        d_proj = [d_q, d_k, d_v, d_u, d_gates]
        g_in = _mm_pieces(f"proj_dw_{l}", d_proj, proj_starts, n_in, lambda i, j, k: i, sv["h"],
                          _spec((t_m, t_d), lambda i, j, k: (k, j)), False, lambda i, j, k: k,
                          _spec((None, t_d, n_in), lambda i, j, k: (i, j, 0)), (N_DEV, d, n_in), GRAD_WIRE,
                          (N_DEV, d // t_d, n_m), TN)
        waiting = [("w_in", g_in), ("w_glu", g_glu.reshape(N_DEV, -1, ssm_w))]
        closing = _Exchange(layered=[(g, 0, depth, land.get(n)) for n, g in waiting]) if l == 0 else None
        d_h = _mm_pieces(f"proj_dx_{l}", d_proj, proj_starts, n_in, lambda i, j, k: k, wg_in[l],
                         _spec((None, t_d, n_in), lambda i, j, k: (k, j, 0)), True, lambda i, j, k: i,
                         _spec((t_m, t_d), lambda i, j, k: (i, j)), (seq, d), BRANCH_CT, (n_m, d // t_d, N_DEV), NT,
                         beside=closing)
        if l == 0:
            d_h, got = d_h
            land.update({n: buf for (n, _), buf in zip(waiting, got)})
        d_x, d_h_next = d_x_in, d_h
    res = _rowwise_vjp("modulate_in_bwd", lambda v, sc, sh: (v, _modulate(v, sc, sh)),
                       [rows_in(x0), vec_in(mods[0][1]), vec_in(mods[0][0])], [rows_in(d_x), rows_in(d_h_next)],
                       [row_wrt(0, d, F32), sum_wrt(1, d), sum_wrt(2, d)], (n_r,))
    grad_x, d_mod[0][1], d_mod[0][0] = res

    d_mod_rows = jnp.concatenate([jnp.concatenate(d_mod[l], axis=1) for l in range(depth)], axis=0)
    grads["b_ada"] = [d_mod_rows[l] for l in range(depth)]
    small_local = [jnp.stack([g.reshape(p[n].shape[1:]) for g in grads[n]]) for n in SMALL_PARAMS]
    d_mod_send = jnp.swapaxes(d_mod_rows.reshape(depth, N_DEV, n_ada), 0, 1)
    small_sum, (d_mod_cols,) = _reduce_packed("exchange_last", _pack(small_local), [d_mod_send])
    d_mod_pad = jnp.pad(jnp.swapaxes(d_mod_cols, 0, 1), ((0, 0), (0, rows_c - N_DEV), (0, 0)))
    g_ada = [
        _mm(f"mod_dw_{l}", c_act, d_mod_pad,
            _spec((rows_c, d), lambda i, j, k: (0, 0)), _spec((None, rows_c, n_ada), lambda i, j, k, l=l: (l, 0, 0)),
            _spec((d, n_ada), lambda i, j, k: (0, 0)), (d, n_ada), F32, (1, 1, 1), TN)
        for l in range(depth)]

    out = {}

    def update(name, partials):
        shape = p[name].shape
        two_d = lambda a: a.reshape(-1, shape[-1])
        res = _adamw("adamw_" + name, two_d(p[name]), two_d(p["m_" + name]), two_d(p["v_" + name]),
                     partials.reshape(partials.shape[0], -1, shape[-1]))
        out[name] = [r.reshape(shape) for r in res]

    update("w_ada", jnp.stack(g_ada)[None])
    for n in ("w_in", "w_sb_up", "w_ssm_up", "w_ffn_in", "w_glu", "w_out", "w_ffn_out"):
        update(n, land[n])
    for n, g in zip(SMALL_PARAMS, _unpack(small_sum, [p[n] for n in SMALL_PARAMS])):
        update(n, g[None])

    return ((loss, grad_x[None]) + tuple(out[n][0] for n in WEIGHTS) + tuple(out[n][1] for n in WEIGHTS)
            + tuple(out[n][2] for n in WEIGHTS) + tuple(out[n][3] for n in WEIGHTS))
```
